```python
import jax, jax.numpy as jnp
from jax import lax
import numpy as np

D_MODEL = 1024
BATCH = 8
SEQ = 8192
DEPTH = 4

GRID_W = 64
CTX_LEN = 256
N_MIXERS = 3
EPS = 1e-6
POOL_WINDOWS = (2, 4, 8, 16)
POOL_GROUPS = 4
POOL_GW = D_MODEL // POOL_GROUPS
HEAD_DIM = 128
N_HEADS = D_MODEL // HEAD_DIM
N_KV_HEADS = N_HEADS // 2
QKV_WIDTH = (N_HEADS + 2 * N_KV_HEADS) * HEAD_DIM
ROPE_BASE = 10000.0
Q_BLOCK = 128
CHUNK = 128
GMLP_HALF = 2 * D_MODEL
GMLP_GROUPS = 8
GMLP_GW = GMLP_HALF // GMLP_GROUPS
D_FF = 4 * D_MODEL

kernel_name = "hybrid_pool_gqa_gmlp_dit_block"


def n_layers_of(kind):
    return len(range(kind, DEPTH, N_MIXERS))


def rms_norm(x, g):
    xf = x.astype(jnp.float32)
    y = xf * lax.rsqrt(jnp.mean(xf * xf, axis=-1, keepdims=True) + EPS)
    return (y * g.astype(jnp.float32)).astype(x.dtype)


def layer_norm(x, g, b):
    xf = x.astype(jnp.float32)
    mu = jnp.mean(xf, axis=-1, keepdims=True)
    xc = xf - mu
    y = xc * lax.rsqrt(jnp.mean(xc * xc, axis=-1, keepdims=True) + EPS)
    return (y * g.astype(jnp.float32) + b.astype(jnp.float32)).astype(x.dtype)


def modulate(h, shift, scale):
    return h * (1 + scale[:, None, :]) + shift[:, None, :]


def pool_mix(h, w, scale):
    B, L, D = h.shape
    hf = h.astype(jnp.float32)
    cs = jnp.concatenate([jnp.zeros((B, 1, D), jnp.float32), jnp.cumsum(hf, axis=1)], axis=1)
    csg = cs.reshape(B, L + 1, POOL_GROUPS, POOL_GW)
    hg = hf.reshape(B, L, POOL_GROUPS, POOL_GW)
    pos = jnp.arange(L)
    outs = []
    for g, win in enumerate(POOL_WINDOWS):
        lo = jnp.clip(pos - win // 2, 0, L)
        hi = jnp.clip(pos + win - win // 2, 0, L)
        s = jnp.take(csg[:, :, g], hi, axis=1) - jnp.take(csg[:, :, g], lo, axis=1)
        cnt = (hi - lo).astype(jnp.float32)[None, :, None]
        outs.append(s / cnt - hg[:, :, g])
    p = jnp.stack(outs, axis=2).astype(h.dtype)
    y = jnp.einsum("blgc,gcd->blgd", p, w).reshape(B, L, D)
    return y * scale


def axial_rope_tables(L):
    rows_n = L // GRID_W
    row = jnp.repeat(jnp.arange(rows_n), GRID_W).astype(jnp.float32)
    col = jnp.tile(jnp.arange(GRID_W), rows_n).astype(jnp.float32)
    half = HEAD_DIM // 2
    inv = ROPE_BASE ** (-jnp.arange(0, half, 2, dtype=jnp.float32) / half)
    ang_r = row[:, None] * inv[None, :]
    ang_c = col[:, None] * inv[None, :]
    return jnp.cos(ang_r), jnp.sin(ang_r), jnp.cos(ang_c), jnp.sin(ang_c)


def rotate(x, cos, sin):
    x1, x2 = jnp.split(x, 2, axis=-1)
    cos = cos[None, :, None, :]
    sin = sin[None, :, None, :]
    return jnp.concatenate([x1 * cos - x2 * sin, x1 * sin + x2 * cos], axis=-1)


def apply_axial_rope(x, tables):
    cr, sr, cc, scol = tables
    xf = x.astype(jnp.float32)
    xr, xc = jnp.split(xf, 2, axis=-1)
    return jnp.concatenate([rotate(xr, cr, sr), rotate(xc, cc, scol)], axis=-1).astype(x.dtype)


def gqa_core(q, k, v):
    B, Lq = q.shape[0], q.shape[1]
    G = N_HEADS // N_KV_HEADS
    qg = q.reshape(B, Lq, N_KV_HEADS, G, HEAD_DIM)
    s = jnp.einsum("bqkgd,bskd->bkgqs", qg, k).astype(jnp.float32) * (HEAD_DIM ** -0.5)
    p = jax.nn.softmax(s, axis=-1).astype(v.dtype)
    o = jnp.einsum("bkgqs,bskd->bqkgd", p, v)
    return o.reshape(B, Lq, N_HEADS * HEAD_DIM)


def qkv_proj(h, w_qkv, q_g, k_g):
    B, L, _ = h.shape
    qkv = h @ w_qkv
    q, k, v = jnp.split(qkv, [N_HEADS * HEAD_DIM, (N_HEADS + N_KV_HEADS) * HEAD_DIM], axis=-1)
    q = rms_norm(q.reshape(B, L, N_HEADS, HEAD_DIM), q_g)
    k = rms_norm(k.reshape(B, L, N_KV_HEADS, HEAD_DIM), k_g)
    v = v.reshape(B, L, N_KV_HEADS, HEAD_DIM)
    return q, k, v


def attn_mix(h_ctx, h_lat, w_qkv, w_o, q_g, k_g, ctx_out):
    B, S, _ = h_lat.shape
    qc, kc, vc = qkv_proj(h_ctx, w_qkv, q_g, k_g)
    ql, kl, vl = qkv_proj(h_lat, w_qkv, q_g, k_g)
    tables = axial_rope_tables(S)
    ql = apply_axial_rope(ql, tables)
    kl = apply_axial_rope(kl, tables)
    k_all = jnp.concatenate([kc, kl], axis=1)
    v_all = jnp.concatenate([vc, vl], axis=1)
    nb = S // Q_BLOCK
    qb = ql.reshape(B, nb, Q_BLOCK, N_HEADS, HEAD_DIM).transpose(1, 0, 2, 3, 4)
    ob = lax.map(lambda qq: gqa_core(qq, k_all, v_all), qb)
    y_lat = ob.transpose(1, 0, 2, 3).reshape(B, S, N_HEADS * HEAD_DIM) @ w_o
    y_ctx = gqa_core(qc, kc, vc) @ w_o if ctx_out else None
    return y_ctx, y_lat


def gmlp_mix(h, w_in, ln_g, ln_b, ws, bs, w_out):
    B, L, _ = h.shape
    z = jax.nn.gelu(h @ w_in)
    u, v = jnp.split(z, 2, axis=-1)
    v = layer_norm(v, ln_g, ln_b)
    vg = v.reshape(B, L // CHUNK, CHUNK, GMLP_GROUPS, GMLP_GW)
    sv = jnp.einsum("gqp,bnpgc->bnqgc", ws, vg) + bs.T[None, None, :, :, None]
    return (u * sv.reshape(B, L, GMLP_HALF)) @ w_out


def sq_relu_mlp(h, w1, w2):
    return jnp.square(jax.nn.relu(h @ w1)) @ w2


def _fwd_setup_inputs(seed: int = 0) -> dict:
    key = jax.random.key(seed)
    ks = jax.random.split(key, 24)
    f32 = jnp.float32
    D = D_MODEL
    nP, nA, nG = n_layers_of(0), n_layers_of(1), n_layers_of(2)

    def nrm(k, shape, s):
        return jax.random.normal(k, shape, f32) * s

    return {
        "x": nrm(ks[0], (BATCH, SEQ, D), 1.0),
        "c": nrm(ks[1], (BATCH, D), 1.0),
        "ctx": nrm(ks[2], (BATCH, CTX_LEN, D), 1.0),
        "c_ctx": nrm(ks[3], (D,), 1.0),
        "ada_w": nrm(ks[4], (DEPTH, D, 6 * D), 0.5 * D ** -0.5),
        "ada_b": nrm(ks[5], (DEPTH, 6 * D), 0.02),
        "norm_g": 1.0 + nrm(ks[6], (DEPTH, 2, D), 0.02),
        "mlp_w1": nrm(ks[7], (DEPTH, D, D_FF), D ** -0.5),
        "mlp_w2": nrm(ks[8], (DEPTH, D_FF, D), D_FF ** -0.5),
        "pool_w": nrm(ks[9], (nP, POOL_GROUPS, POOL_GW, POOL_GW), POOL_GW ** -0.5),
        "pool_scale": 1.0 + nrm(ks[10], (nP, D), 0.02),
        "attn_w_qkv": nrm(ks[11], (nA, D, QKV_WIDTH), D ** -0.5),
        "attn_w_o": nrm(ks[12], (nA, N_HEADS * HEAD_DIM, D), (N_HEADS * HEAD_DIM) ** -0.5),
        "attn_q_g": 1.0 + nrm(ks[13], (nA, HEAD_DIM), 0.02),
        "attn_k_g": 1.0 + nrm(ks[14], (nA, HEAD_DIM), 0.02),
        "gm_w_in": nrm(ks[15], (nG, D, 2 * GMLP_HALF), D ** -0.5),
        "gm_ln_g": 1.0 + nrm(ks[16], (nG, GMLP_HALF), 0.02),
        "gm_ln_b": nrm(ks[17], (nG, GMLP_HALF), 0.02),
        "gm_ws": nrm(ks[18], (nG, GMLP_GROUPS, CHUNK, CHUNK), CHUNK ** -0.5),
        "gm_bs": nrm(ks[19], (nG, GMLP_GROUPS, CHUNK), 0.02),
        "gm_w_out": nrm(ks[20], (nG, GMLP_HALF, D), GMLP_HALF ** -0.5),
        "final_g": 1.0 + nrm(ks[21], (D,), 0.02),
    }


def _fwd_reference(x, c, ctx, c_ctx, ada_w, ada_b, norm_g, mlp_w1, mlp_w2, pool_w, pool_scale,
              attn_w_qkv, attn_w_o, attn_q_g, attn_k_g, gm_w_in, gm_ln_g, gm_ln_b, gm_ws, gm_bs,
              gm_w_out, final_g):
    last_ctx_read = max([i for i in range(DEPTH) if i % N_MIXERS == 1], default=-1)
    s_lat = jax.nn.silu(c)
    s_ctx = jax.nn.silu(c_ctx)[None, :]
    h_lat, h_ctx = x, ctx
    for i in range(DEPTH):
        kind, j = i % N_MIXERS, i // N_MIXERS
        ctx_in = i <= last_ctx_read
        ctx_out = i < last_ctx_read
        sh1, sc1, g1, sh2, sc2, g2 = jnp.split(s_lat @ ada_w[i] + ada_b[i], 6, axis=-1)
        a_l = modulate(rms_norm(h_lat, norm_g[i, 0]), sh1, sc1)
        if ctx_in:
            csh1, csc1, cg1, csh2, csc2, cg2 = jnp.split(s_ctx @ ada_w[i] + ada_b[i], 6, axis=-1)
            a_c = modulate(rms_norm(h_ctx, norm_g[i, 0]), csh1, csc1)
        y_c = None
        if kind == 0:
            y_l = pool_mix(a_l, pool_w[j], pool_scale[j])
            if ctx_out:
                y_c = pool_mix(a_c, pool_w[j], pool_scale[j])
        elif kind == 1:
            y_c, y_l = attn_mix(a_c, a_l, attn_w_qkv[j], attn_w_o[j], attn_q_g[j], attn_k_g[j], ctx_out)
        else:
            y_l = gmlp_mix(a_l, gm_w_in[j], gm_ln_g[j], gm_ln_b[j], gm_ws[j], gm_bs[j], gm_w_out[j])
            if ctx_out:
                y_c = gmlp_mix(a_c, gm_w_in[j], gm_ln_g[j], gm_ln_b[j], gm_ws[j], gm_bs[j], gm_w_out[j])
        h_lat = h_lat + g1[:, None, :] * y_l
        m_l = modulate(rms_norm(h_lat, norm_g[i, 1]), sh2, sc2)
        h_lat = h_lat + g2[:, None, :] * sq_relu_mlp(m_l, mlp_w1[i], mlp_w2[i])
        if ctx_out:
            h_ctx = h_ctx + cg1[:, None, :] * y_c
            m_c = modulate(rms_norm(h_ctx, norm_g[i, 1]), csh2, csc2)
            h_ctx = h_ctx + cg2[:, None, :] * sq_relu_mlp(m_c, mlp_w1[i], mlp_w2[i])
    return rms_norm(h_lat, final_g)


import jax as _jax
import jax.numpy as _jnp

TWIN_FORMAT = 'train_step'
FWD_PARAMS = ['x', 'c', 'ctx', 'c_ctx', 'ada_w', 'ada_b', 'norm_g', 'mlp_w1', 'mlp_w2', 'pool_w', 'pool_scale', 'attn_w_qkv', 'attn_w_o', 'attn_q_g', 'attn_k_g', 'gm_w_in', 'gm_ln_g', 'gm_ln_b', 'gm_ws', 'gm_bs', 'gm_w_out', 'final_g']
TWIN_WEIGHTS = ['c_ctx', 'ada_w', 'ada_b', 'norm_g', 'mlp_w1', 'mlp_w2', 'pool_w', 'pool_scale', 'attn_w_qkv', 'attn_w_o', 'attn_q_g', 'attn_k_g', 'gm_w_in', 'gm_ln_g', 'gm_ln_b', 'gm_ws', 'gm_bs', 'gm_w_out', 'final_g']
TWIN_DIFF_INPUT = 'x'
TWIN_INPUTS = ['x', 'c', 'ctx', 'c_ctx', 'ada_w', 'ada_b', 'norm_g', 'mlp_w1', 'mlp_w2', 'pool_w', 'pool_scale', 'attn_w_qkv', 'attn_w_o', 'attn_q_g', 'attn_k_g', 'gm_w_in', 'gm_ln_g', 'gm_ln_b', 'gm_ws', 'gm_bs', 'gm_w_out', 'final_g', 'loss_target', 'm_c_ctx', 'm_ada_w', 'm_ada_b', 'm_norm_g', 'm_mlp_w1', 'm_mlp_w2', 'm_pool_w', 'm_pool_scale', 'm_attn_w_qkv', 'm_attn_w_o', 'm_attn_q_g', 'm_attn_k_g', 'm_gm_w_in', 'm_gm_ln_g', 'm_gm_ln_b', 'm_gm_ws', 'm_gm_bs', 'm_gm_w_out', 'm_final_g', 'v_c_ctx', 'v_ada_w', 'v_ada_b', 'v_norm_g', 'v_mlp_w1', 'v_mlp_w2', 'v_pool_w', 'v_pool_scale', 'v_attn_w_qkv', 'v_attn_w_o', 'v_attn_q_g', 'v_attn_k_g', 'v_gm_w_in', 'v_gm_ln_g', 'v_gm_ln_b', 'v_gm_ws', 'v_gm_bs', 'v_gm_w_out', 'v_final_g']
TWIN_OUTPUTS = ['loss', 'grad_x', 'grad_c_ctx', 'grad_ada_w', 'grad_ada_b', 'grad_norm_g', 'grad_mlp_w1', 'grad_mlp_w2', 'grad_pool_w', 'grad_pool_scale', 'grad_attn_w_qkv', 'grad_attn_w_o', 'grad_attn_q_g', 'grad_attn_k_g', 'grad_gm_w_in', 'grad_gm_ln_g', 'grad_gm_ln_b', 'grad_gm_ws', 'grad_gm_bs', 'grad_gm_w_out', 'grad_final_g', 'delta_c_ctx', 'delta_ada_w', 'delta_ada_b', 'delta_norm_g', 'delta_mlp_w1', 'delta_mlp_w2', 'delta_pool_w', 'delta_pool_scale', 'delta_attn_w_qkv', 'delta_attn_w_o', 'delta_attn_q_g', 'delta_attn_k_g', 'delta_gm_w_in', 'delta_gm_ln_g', 'delta_gm_ln_b', 'delta_gm_ws', 'delta_gm_bs', 'delta_gm_w_out', 'delta_final_g', 'new_m_c_ctx', 'new_m_ada_w', 'new_m_ada_b', 'new_m_norm_g', 'new_m_mlp_w1', 'new_m_mlp_w2', 'new_m_pool_w', 'new_m_pool_scale', 'new_m_attn_w_qkv', 'new_m_attn_w_o', 'new_m_attn_q_g', 'new_m_attn_k_g', 'new_m_gm_w_in', 'new_m_gm_ln_g', 'new_m_gm_ln_b', 'new_m_gm_ws', 'new_m_gm_bs', 'new_m_gm_w_out', 'new_m_final_g', 'new_v_c_ctx', 'new_v_ada_w', 'new_v_ada_b', 'new_v_norm_g', 'new_v_mlp_w1', 'new_v_mlp_w2', 'new_v_pool_w', 'new_v_pool_scale', 'new_v_attn_w_qkv', 'new_v_attn_w_o', 'new_v_attn_q_g', 'new_v_attn_k_g', 'new_v_gm_w_in', 'new_v_gm_ln_g', 'new_v_gm_ln_b', 'new_v_gm_ws', 'new_v_gm_bs', 'new_v_gm_w_out', 'new_v_final_g']
TWIN_LEAF_KINDS = {'loss': 'loss', 'grad_x': 'grad_x', 'grad_c_ctx': 'grad_w', 'grad_ada_w': 'grad_w', 'grad_ada_b': 'grad_w', 'grad_norm_g': 'grad_w', 'grad_mlp_w1': 'grad_w', 'grad_mlp_w2': 'grad_w', 'grad_pool_w': 'grad_w', 'grad_pool_scale': 'grad_w', 'grad_attn_w_qkv': 'grad_w', 'grad_attn_w_o': 'grad_w', 'grad_attn_q_g': 'grad_w', 'grad_attn_k_g': 'grad_w', 'grad_gm_w_in': 'grad_w', 'grad_gm_ln_g': 'grad_w', 'grad_gm_ln_b': 'grad_w', 'grad_gm_ws': 'grad_w', 'grad_gm_bs': 'grad_w', 'grad_gm_w_out': 'grad_w', 'grad_final_g': 'grad_w', 'delta_c_ctx': 'delta_w', 'delta_ada_w': 'delta_w', 'delta_ada_b': 'delta_w', 'delta_norm_g': 'delta_w', 'delta_mlp_w1': 'delta_w', 'delta_mlp_w2': 'delta_w', 'delta_pool_w': 'delta_w', 'delta_pool_scale': 'delta_w', 'delta_attn_w_qkv': 'delta_w', 'delta_attn_w_o': 'delta_w', 'delta_attn_q_g': 'delta_w', 'delta_attn_k_g': 'delta_w', 'delta_gm_w_in': 'delta_w', 'delta_gm_ln_g': 'delta_w', 'delta_gm_ln_b': 'delta_w', 'delta_gm_ws': 'delta_w', 'delta_gm_bs': 'delta_w', 'delta_gm_w_out': 'delta_w', 'delta_final_g': 'delta_w', 'new_m_c_ctx': 'new_m', 'new_m_ada_w': 'new_m', 'new_m_ada_b': 'new_m', 'new_m_norm_g': 'new_m', 'new_m_mlp_w1': 'new_m', 'new_m_mlp_w2': 'new_m', 'new_m_pool_w': 'new_m', 'new_m_pool_scale': 'new_m', 'new_m_attn_w_qkv': 'new_m', 'new_m_attn_w_o': 'new_m', 'new_m_attn_q_g': 'new_m', 'new_m_attn_k_g': 'new_m', 'new_m_gm_w_in': 'new_m', 'new_m_gm_ln_g': 'new_m', 'new_m_gm_ln_b': 'new_m', 'new_m_gm_ws': 'new_m', 'new_m_gm_bs': 'new_m', 'new_m_gm_w_out': 'new_m', 'new_m_final_g': 'new_m', 'new_v_c_ctx': 'new_v', 'new_v_ada_w': 'new_v', 'new_v_ada_b': 'new_v', 'new_v_norm_g': 'new_v', 'new_v_mlp_w1': 'new_v', 'new_v_mlp_w2': 'new_v', 'new_v_pool_w': 'new_v', 'new_v_pool_scale': 'new_v', 'new_v_attn_w_qkv': 'new_v', 'new_v_attn_w_o': 'new_v', 'new_v_attn_q_g': 'new_v', 'new_v_attn_k_g': 'new_v', 'new_v_gm_w_in': 'new_v', 'new_v_gm_ln_g': 'new_v', 'new_v_gm_ln_b': 'new_v', 'new_v_gm_ws': 'new_v', 'new_v_gm_bs': 'new_v', 'new_v_gm_w_out': 'new_v', 'new_v_final_g': 'new_v'}


def _forward(args):
    return _fwd_reference(*[args[k] for k in FWD_PARAMS])


def _output_shape():
    def fwd():
        inp = _fwd_setup_inputs(0)
        return _fwd_reference(*[inp[k] for k in FWD_PARAMS])
    out = _jax.eval_shape(fwd)
    return out.shape, out.dtype

N_MICROBATCH = 1
ADAM_LR = 0.001
ADAM_B1 = 0.9
ADAM_B2 = 0.999
ADAM_EPS = 1e-08
ADAM_WD = 0.01
ADAM_STEP = 10
PER_EXAMPLE_BATCH_AXIS = {'x': 0, 'c': 0, 'ctx': 0, 'loss_target': 0}
SHARED_INPUTS = []
_WEIGHT_DTYPES = {'c_ctx': _jnp.float32, 'ada_w': _jnp.float32, 'ada_b': _jnp.float32, 'norm_g': _jnp.float32, 'mlp_w1': _jnp.float32, 'mlp_w2': _jnp.float32, 'pool_w': _jnp.float32, 'pool_scale': _jnp.float32, 'attn_w_qkv': _jnp.float32, 'attn_w_o': _jnp.float32, 'attn_q_g': _jnp.float32, 'attn_k_g': _jnp.float32, 'gm_w_in': _jnp.float32, 'gm_ln_g': _jnp.float32, 'gm_ln_b': _jnp.float32, 'gm_ws': _jnp.float32, 'gm_bs': _jnp.float32, 'gm_w_out': _jnp.float32, 'final_g': _jnp.float32}
MOMENT_SCALE = {'c_ctx': 1.403767e-02, 'ada_w': 1.044378e-01, 'ada_b': 1.803717e-01, 'norm_g': 8.299130e-02, 'mlp_w1': 5.298256e-02, 'mlp_w2': 9.702620e-02, 'pool_w': 6.209831e-02, 'pool_scale': 9.033069e-02, 'attn_w_qkv': 1.922320e-02, 'attn_w_o': 2.665488e-02, 'attn_q_g': 1.920080e-02, 'attn_k_g': 1.902172e-02, 'gm_w_in': 3.536058e-02, 'gm_ln_g': 3.174855e-02, 'gm_ln_b': 3.337156e-02, 'gm_ws': 4.582532e-02, 'gm_bs': 4.757628e-02, 'gm_w_out': 4.572024e-02, 'final_g': 6.461234e+01}


def _to_microbatches(a, axis):
    t = _jnp.moveaxis(a, axis, 0)
    t = t.reshape((N_MICROBATCH, t.shape[0] // N_MICROBATCH) + t.shape[1:])
    return _jnp.moveaxis(t, 1, axis + 1)


def setup_inputs(seed: int = 0) -> dict:
    inp = _fwd_setup_inputs(seed)
    key = _jax.random.fold_in(_jax.random.key(seed), 7919)
    shape, _ = _output_shape()
    out = dict(inp)
    out["loss_target"] = _jax.random.normal(_jax.random.fold_in(key, 0), shape, _jnp.float32)
    for i, name in enumerate(TWIN_WEIGHTS):
        w = inp[name].astype(_jnp.float32)
        if MOMENT_SCALE is None:
            s = _jnp.sqrt(_jnp.mean(_jnp.square(w)) + 1e-30)
        else:
            s = MOMENT_SCALE[name]
        km, kv = _jax.random.split(_jax.random.fold_in(key, i + 1))
        out[name] = w
        out["m_" + name] = s * _jax.random.normal(km, w.shape, _jnp.float32)
        out["v_" + name] = (s * s) * _jax.random.uniform(kv, w.shape, _jnp.float32, 0.5, 1.5)
    if N_MICROBATCH > 1:
        for name, axis in PER_EXAMPLE_BATCH_AXIS.items():
            out[name] = _to_microbatches(out[name], axis)
    return {'x': out['x'], 'c': out['c'], 'ctx': out['ctx'], 'c_ctx': out['c_ctx'], 'ada_w': out['ada_w'], 'ada_b': out['ada_b'], 'norm_g': out['norm_g'], 'mlp_w1': out['mlp_w1'], 'mlp_w2': out['mlp_w2'], 'pool_w': out['pool_w'], 'pool_scale': out['pool_scale'], 'attn_w_qkv': out['attn_w_qkv'], 'attn_w_o': out['attn_w_o'], 'attn_q_g': out['attn_q_g'], 'attn_k_g': out['attn_k_g'], 'gm_w_in': out['gm_w_in'], 'gm_ln_g': out['gm_ln_g'], 'gm_ln_b': out['gm_ln_b'], 'gm_ws': out['gm_ws'], 'gm_bs': out['gm_bs'], 'gm_w_out': out['gm_w_out'], 'final_g': out['final_g'], 'loss_target': out['loss_target'], 'm_c_ctx': out['m_c_ctx'], 'm_ada_w': out['m_ada_w'], 'm_ada_b': out['m_ada_b'], 'm_norm_g': out['m_norm_g'], 'm_mlp_w1': out['m_mlp_w1'], 'm_mlp_w2': out['m_mlp_w2'], 'm_pool_w': out['m_pool_w'], 'm_pool_scale': out['m_pool_scale'], 'm_attn_w_qkv': out['m_attn_w_qkv'], 'm_attn_w_o': out['m_attn_w_o'], 'm_attn_q_g': out['m_attn_q_g'], 'm_attn_k_g': out['m_attn_k_g'], 'm_gm_w_in': out['m_gm_w_in'], 'm_gm_ln_g': out['m_gm_ln_g'], 'm_gm_ln_b': out['m_gm_ln_b'], 'm_gm_ws': out['m_gm_ws'], 'm_gm_bs': out['m_gm_bs'], 'm_gm_w_out': out['m_gm_w_out'], 'm_final_g': out['m_final_g'], 'v_c_ctx': out['v_c_ctx'], 'v_ada_w': out['v_ada_w'], 'v_ada_b': out['v_ada_b'], 'v_norm_g': out['v_norm_g'], 'v_mlp_w1': out['v_mlp_w1'], 'v_mlp_w2': out['v_mlp_w2'], 'v_pool_w': out['v_pool_w'], 'v_pool_scale': out['v_pool_scale'], 'v_attn_w_qkv': out['v_attn_w_qkv'], 'v_attn_w_o': out['v_attn_w_o'], 'v_attn_q_g': out['v_attn_q_g'], 'v_attn_k_g': out['v_attn_k_g'], 'v_gm_w_in': out['v_gm_w_in'], 'v_gm_ln_g': out['v_gm_ln_g'], 'v_gm_ln_b': out['v_gm_ln_b'], 'v_gm_ws': out['v_gm_ws'], 'v_gm_bs': out['v_gm_bs'], 'v_gm_w_out': out['v_gm_w_out'], 'v_final_g': out['v_final_g']}


def _loss(weights, diff, rest, loss_target):
    with _jax.named_scope("forward"):
        args = {**rest, TWIN_DIFF_INPUT: diff, **{k: w.astype(_WEIGHT_DTYPES[k]) for k, w in weights.items()}}
        y = _forward(args)
    with _jax.named_scope("loss_head"):
        err = _jnp.square(y.astype(_jnp.float32) - loss_target)
        return 0.5 * _jnp.sum(_jnp.mean(err, axis=-1)) if err.ndim else 0.5 * err


def _adamw(w, g, m, v):
    m = ADAM_B1 * m + (1.0 - ADAM_B1) * g
    v = ADAM_B2 * v + (1.0 - ADAM_B2) * _jnp.square(g)
    m_hat = m / (1.0 - ADAM_B1 ** ADAM_STEP)
    v_hat = v / (1.0 - ADAM_B2 ** ADAM_STEP)
    delta = -ADAM_LR * (m_hat / (_jnp.sqrt(v_hat) + ADAM_EPS) + ADAM_WD * w)
    return delta, m, v


def reference(x, c, ctx, c_ctx, ada_w, ada_b, norm_g, mlp_w1, mlp_w2, pool_w, pool_scale, attn_w_qkv, attn_w_o, attn_q_g, attn_k_g, gm_w_in, gm_ln_g, gm_ln_b, gm_ws, gm_bs, gm_w_out, final_g, loss_target, m_c_ctx, m_ada_w, m_ada_b, m_norm_g, m_mlp_w1, m_mlp_w2, m_pool_w, m_pool_scale, m_attn_w_qkv, m_attn_w_o, m_attn_q_g, m_attn_k_g, m_gm_w_in, m_gm_ln_g, m_gm_ln_b, m_gm_ws, m_gm_bs, m_gm_w_out, m_final_g, v_c_ctx, v_ada_w, v_ada_b, v_norm_g, v_mlp_w1, v_mlp_w2, v_pool_w, v_pool_scale, v_attn_w_qkv, v_attn_w_o, v_attn_q_g, v_attn_k_g, v_gm_w_in, v_gm_ln_g, v_gm_ln_b, v_gm_ws, v_gm_bs, v_gm_w_out, v_final_g):
    given = dict(x=x, c=c, ctx=ctx, c_ctx=c_ctx, ada_w=ada_w, ada_b=ada_b, norm_g=norm_g, mlp_w1=mlp_w1, mlp_w2=mlp_w2, pool_w=pool_w, pool_scale=pool_scale, attn_w_qkv=attn_w_qkv, attn_w_o=attn_w_o, attn_q_g=attn_q_g, attn_k_g=attn_k_g, gm_w_in=gm_w_in, gm_ln_g=gm_ln_g, gm_ln_b=gm_ln_b, gm_ws=gm_ws, gm_bs=gm_bs, gm_w_out=gm_w_out, final_g=final_g, loss_target=loss_target, m_c_ctx=m_c_ctx, m_ada_w=m_ada_w, m_ada_b=m_ada_b, m_norm_g=m_norm_g, m_mlp_w1=m_mlp_w1, m_mlp_w2=m_mlp_w2, m_pool_w=m_pool_w, m_pool_scale=m_pool_scale, m_attn_w_qkv=m_attn_w_qkv, m_attn_w_o=m_attn_w_o, m_attn_q_g=m_attn_q_g, m_attn_k_g=m_attn_k_g, m_gm_w_in=m_gm_w_in, m_gm_ln_g=m_gm_ln_g, m_gm_ln_b=m_gm_ln_b, m_gm_ws=m_gm_ws, m_gm_bs=m_gm_bs, m_gm_w_out=m_gm_w_out, m_final_g=m_final_g, v_c_ctx=v_c_ctx, v_ada_w=v_ada_w, v_ada_b=v_ada_b, v_norm_g=v_norm_g, v_mlp_w1=v_mlp_w1, v_mlp_w2=v_mlp_w2, v_pool_w=v_pool_w, v_pool_scale=v_pool_scale, v_attn_w_qkv=v_attn_w_qkv, v_attn_w_o=v_attn_w_o, v_attn_q_g=v_attn_q_g, v_attn_k_g=v_attn_k_g, v_gm_w_in=v_gm_w_in, v_gm_ln_g=v_gm_ln_g, v_gm_ln_b=v_gm_ln_b, v_gm_ws=v_gm_ws, v_gm_bs=v_gm_bs, v_gm_w_out=v_gm_w_out, v_final_g=v_final_g)
    weights = {n: given[n] for n in TWIN_WEIGHTS}
    shared = {n: given[n] for n in SHARED_INPUTS}
    per_example = {n: given[n] for n in ['x', 'c', 'ctx']}
    grad_fn = _jax.value_and_grad(_loss, argnums=(0, 1))

    def one_microbatch(ex, loss_target):
        ex = dict(ex)
        diff = ex.pop(TWIN_DIFF_INPUT)
        return grad_fn(weights, diff, {**shared, **ex}, loss_target)

    if N_MICROBATCH == 1:
        loss, (grad_w, grad_x) = one_microbatch(per_example, given["loss_target"])
    else:
        def body(carry, xs):
            loss_sum, grad_sum = carry
            l_k, (gw_k, gx_k) = one_microbatch(xs[0], xs[1])
            with _jax.named_scope("update"):
                return (loss_sum + l_k, _jax.tree.map(_jnp.add, grad_sum, gw_k)), gx_k

        init = (_jnp.zeros((), _jnp.float32), _jax.tree.map(_jnp.zeros_like, weights))
        (loss, grad_w), grad_x = _jax.lax.scan(body, init, (per_example, given["loss_target"]))
    with _jax.named_scope("update"):
        delta_w, new_m, new_v = {}, {}, {}
        for n in TWIN_WEIGHTS:
            delta_w[n], new_m[n], new_v[n] = _adamw(weights[n], grad_w[n], given["m_" + n], given["v_" + n])
    return (loss, grad_x, *[grad_w[n] for n in TWIN_WEIGHTS], *[delta_w[n] for n in TWIN_WEIGHTS],
            *[new_m[n] for n in TWIN_WEIGHTS], *[new_v[n] for n in TWIN_WEIGHTS])
```

```python
import functools
import math

import jax
import jax.numpy as jnp
from jax import lax
from jax.experimental import pallas as pl
from jax.experimental.pallas import tpu as pltpu

F32 = jnp.float32
BF16 = jnp.bfloat16
MESH = pl.DeviceIdType.MESH

EPS = 1e-6
GRID_W = 64
ROPE_BASE = 10000.0
POOL_WINDOWS = (2, 4, 8, 16)
HALO = 8
DEPTH = 4
N_MIXERS = 3

ADAM_LR = 0.001
ADAM_B1 = 0.9
ADAM_B2 = 0.999
ADAM_EPS = 1e-08
ADAM_WD = 0.01
ADAM_STEP = 10

VMEM_LIMIT_BYTES = 56 * 1024 * 1024
LANES = 128
SUBLANES = 8
N_DEV = 8
N_CHIPS = 4

SH1, SC1, G1, SH2, SC2, G2, NG0, NG1 = range(8)


def _dot(a, b):
    return jnp.dot(a, b, preferred_element_type=F32)


def _dot_nt(a, b):
    return lax.dot_general(a, b, (((1,), (1,)), ((), ())), preferred_element_type=F32)


def _dot_tn(a, b):
    return lax.dot_general(a, b, (((0,), (0,)), ((), ())), preferred_element_type=F32)


def _params(**kw):
    return pltpu.CompilerParams(vmem_limit_bytes=VMEM_LIMIT_BYTES, **kw)


def _full(shape):
    nd = len(shape)
    return pl.BlockSpec(shape, lambda *_: (0,) * nd)


def _rows(tm, width):
    return pl.BlockSpec((tm, width), lambda i: (i, 0))


def _group_of(nct, groups):
    if groups == 1:
        return lambda i: 0
    return lambda i: jnp.where(i >= nct, 1, 0)


def _mods_spec(nct, groups, d):
    grp = _group_of(nct, groups)
    return pl.BlockSpec((None, 8, d), lambda i: (grp(i), 0, 0))


def _first_of_group(i, nct, groups):
    if groups == 1:
        return i == 0
    return jnp.logical_or(i == 0, i == nct)


def _rowsum(v):
    return jnp.sum(v, axis=0, keepdims=True)


def _rms_parts(x):
    r = lax.rsqrt(jnp.mean(x * x, axis=-1, keepdims=True) + EPS)
    return x * r, r


def _normmod(x, md, which):
    ng, sh, sc = (md[NG0:NG0 + 1], md[SH1:SH1 + 1], md[SC1:SC1 + 1]) if which == 0 else (
        md[NG1:NG1 + 1], md[SH2:SH2 + 1], md[SC2:SC2 + 1])
    xhat, r = _rms_parts(x)
    n = xhat * ng
    return n * (1.0 + sc) + sh, (xhat, r, n)


def _normmod_bwd(da, parts, md, which):
    xhat, r, n = parts
    ng, sc = (md[NG0:NG0 + 1], md[SC1:SC1 + 1]) if which == 0 else (md[NG1:NG1 + 1], md[SC2:SC2 + 1])
    dsh = _rowsum(da)
    dsc = _rowsum(da * n)
    dn = da * (1.0 + sc)
    dng = _rowsum(dn * xhat)
    dxhat = dn * ng
    dx = r * (dxhat - xhat * jnp.mean(dxhat * xhat, axis=-1, keepdims=True))
    return dx, dsh, dsc, dng


def _acc_rows(ref, first, rows):
    @pl.when(first)
    def _():
        ref[...] = jnp.zeros(ref.shape, ref.dtype)

    for r, v in rows.items():
        ref[r:r + 1, :] += v


def _shift_up(x, k):
    if k == 0:
        return x
    return pltpu.roll(x, x.shape[0] - k, axis=0)


def _gelu(x):
    k = math.sqrt(2.0 / math.pi)
    return 0.5 * x * (1.0 + jnp.tanh(k * (x + 0.044715 * x * x * x)))


def _gelu_grad(x):
    k = math.sqrt(2.0 / math.pi)
    t = jnp.tanh(k * (x + 0.044715 * x * x * x))
    return 0.5 * (1.0 + t) + 0.5 * x * (1.0 - t * t) * k * (1.0 + 3.0 * 0.044715 * x * x)


def _silu(x):
    return x / (1.0 + jnp.exp(-x))


def _silu_grad(x):
    s = 1.0 / (1.0 + jnp.exp(-x))
    return s * (1.0 + x * (1.0 - s))


def _ff_chunk(ff):
    return min(ff, 1024)


def _mlp_fwd(h, mods, w1, w2, layer, *, nct, tm):
    rows, d = h.shape
    groups = mods.shape[0]
    ff = w1.shape[-1]
    fc = _ff_chunk(ff)

    def body(h_ref, md_ref, w1_ref, w2_ref, h2_ref, u_ref, o_ref):
        x = h_ref[...]
        md = md_ref[...]
        m, _ = _normmod(x, md, 1)
        mb = m.astype(BF16)
        acc = jnp.zeros((tm, d), F32)
        for k in range(ff // fc):
            u = _dot(mb, w1_ref[:, k * fc:(k + 1) * fc])
            u_ref[:, k * fc:(k + 1) * fc] = u.astype(BF16)
            acc = acc + _dot(jnp.square(jnp.maximum(u, 0.0)).astype(BF16), w2_ref[k * fc:(k + 1) * fc, :])
        o_ref[...] = acc.astype(BF16)
        h2_ref[...] = x + md[G2:G2 + 1] * acc

    return pl.pallas_call(
        body, name=f"mlp_fwd_{layer}", grid=(rows // tm,),
        in_specs=[_rows(tm, d), _mods_spec(nct, groups, d),
                  pl.BlockSpec((None, d, ff), lambda i: (layer, 0, 0)),
                  pl.BlockSpec((None, ff, d), lambda i: (layer, 0, 0))],
        out_specs=[_rows(tm, d), _rows(tm, ff), _rows(tm, d)],
        out_shape=[jax.ShapeDtypeStruct((rows, d), F32), jax.ShapeDtypeStruct((rows, ff), BF16),
                   jax.ShapeDtypeStruct((rows, d), BF16)],
        compiler_params=_params(),
    )(h, mods, w1, w2)


def _mlp_bwd(h1, dh2, u, o, mods, w1, w2, layer, *, nct, tm):
    rows, d = h1.shape
    groups = mods.shape[0]
    ff = w1.shape[-1]
    fc = _ff_chunk(ff)

    def body(h_ref, g_ref, u_ref, o_ref, md_ref, w1_ref, w2_ref, dh_ref, du_ref, dob_ref, mb_ref, dmd_ref):
        i = pl.program_id(0)
        x = h_ref[...]
        g = g_ref[...]
        md = md_ref[...]
        m, parts = _normmod(x, md, 1)
        mb_ref[...] = m.astype(BF16)
        dg2 = _rowsum(g * o_ref[...].astype(F32))
        dob = (g * md[G2:G2 + 1]).astype(BF16)
        dob_ref[...] = dob
        dm = jnp.zeros((tm, d), F32)
        for k in range(ff // fc):
            uk = u_ref[:, k * fc:(k + 1) * fc].astype(F32)
            dr = _dot_nt(dob, w2_ref[k * fc:(k + 1) * fc, :])
            duk = (dr * (2.0 * jnp.maximum(uk, 0.0))).astype(BF16)
            du_ref[:, k * fc:(k + 1) * fc] = duk
            dm = dm + _dot_nt(duk, w1_ref[:, k * fc:(k + 1) * fc])
        dx, dsh, dsc, dng = _normmod_bwd(dm, parts, md, 1)
        dh_ref[...] = g + dx
        _acc_rows(dmd_ref, _first_of_group(i, nct, groups), {SH2: dsh, SC2: dsc, G2: dg2, NG1: dng})

    return pl.pallas_call(
        body, name=f"mlp_bwd_{layer}", grid=(rows // tm,),
        in_specs=[_rows(tm, d), _rows(tm, d), _rows(tm, ff), _rows(tm, d), _mods_spec(nct, groups, d),
                  pl.BlockSpec((None, d, ff), lambda i: (layer, 0, 0)),
                  pl.BlockSpec((None, ff, d), lambda i: (layer, 0, 0))],
        out_specs=[_rows(tm, d), _rows(tm, ff), _rows(tm, d), _rows(tm, d), _mods_spec(nct, groups, d)],
        out_shape=[jax.ShapeDtypeStruct((rows, d), F32), jax.ShapeDtypeStruct((rows, ff), BF16),
                   jax.ShapeDtypeStruct((rows, d), BF16), jax.ShapeDtypeStruct((rows, d), BF16),
                   jax.ShapeDtypeStruct((groups, 8, d), F32)],
        compiler_params=_params(),
    )(h1, dh2, u, o, mods, w1, w2)


def _div_tile(n, cap):
    if n <= cap:
        return n
    return max(t for t in range(LANES, cap + 1, LANES) if n % t == 0)


def _mm_tn(a, b, name, *, relu2=False):
    rows, m = a.shape
    n = b.shape[1]
    tmm = min(m, 1024)
    tn = min(n, 2048)
    tr = _div_tile(rows, 768)

    def body(a_ref, b_ref, o_ref, acc_ref):
        r = pl.program_id(2)

        @pl.when(r == 0)
        def _():
            acc_ref[...] = jnp.zeros(acc_ref.shape, F32)

        av = a_ref[...]
        if relu2:
            av = jnp.square(jnp.maximum(av.astype(F32), 0.0)).astype(BF16)
        acc_ref[...] += _dot_tn(av, b_ref[...])

        @pl.when(r == pl.num_programs(2) - 1)
        def _():
            o_ref[...] = acc_ref[...].astype(BF16)

    return pl.pallas_call(
        body, name=name, grid=(m // tmm, n // tn, rows // tr),
        in_specs=[pl.BlockSpec((tr, tmm), lambda i, j, r: (r, i)), pl.BlockSpec((tr, tn), lambda i, j, r: (r, j))],
        out_specs=pl.BlockSpec((tmm, tn), lambda i, j, r: (i, j)),
        out_shape=jax.ShapeDtypeStruct((m, n), BF16),
        scratch_shapes=[pltpu.VMEM((tmm, tn), F32)],
        compiler_params=_params(),
    )(a, b)


def _halo_specs(tm, d, rows):
    per = tm // HALO
    prev = pl.BlockSpec((HALO, d), lambda i: (jnp.maximum(i * per - 1, 0), 0))
    nxt = pl.BlockSpec((HALO, d), lambda i: (jnp.minimum((i + 1) * per, rows // HALO - 1), 0))
    return prev, _rows(tm, d), nxt


def _segment_positions(i, tm, nct, groups, seg_lens):
    if groups == 1:
        start, length = 0, seg_lens[-1]
    else:
        start = jnp.where(i >= nct, nct, 0)
        length = jnp.where(i >= nct, seg_lens[1], seg_lens[0])
    rid = lax.broadcasted_iota(jnp.int32, (tm + 2 * HALO, 1), 0)
    pos = (i - start) * tm - HALO + rid
    return pos, length


def _window_count(pos, length, w):
    hi = jnp.minimum(pos + (w - w // 2), length)
    lo = jnp.maximum(pos - w // 2, 0)
    return (hi - lo).astype(F32)


def _window_sum(xg, w, lead):
    b, k = xg, 1
    while k < w:
        b = b + _shift_up(b, k)
        k *= 2
    return _shift_up(b, HALO - lead)[0:xg.shape[0] - 2 * HALO]


def _pooled(ext, md, pos, length, gw):
    tm = ext.shape[0] - 2 * HALO
    a_ext, parts = _normmod(ext, md, 0)
    valid = jnp.logical_and(pos >= 0, pos < length)
    a_ext = jnp.where(valid, a_ext, 0.0)
    pos_c = pos[HALO:HALO + tm]
    ps = []
    for g, w in enumerate(POOL_WINDOWS):
        xg = a_ext[:, g * gw:(g + 1) * gw]
        s = _window_sum(xg, w, w // 2)
        ps.append(s / _window_count(pos_c, length, w) - xg[HALO:HALO + tm])
    return ps, parts


def _pool_fwd(h, mods, pw, pscale, layer, *, nct, tm, seg_lens):
    rows, d = h.shape
    groups = mods.shape[0]
    pg, gw = pw.shape[1], pw.shape[-1]

    def body(prev_ref, cur_ref, next_ref, md_ref, pw_ref, ps_ref, out_ref):
        i = pl.program_id(0)
        md = md_ref[...]
        cur = cur_ref[...]
        ext = jnp.concatenate([prev_ref[...], cur, next_ref[...]], axis=0)
        pos, length = _segment_positions(i, tm, nct, groups, seg_lens)
        ps, _ = _pooled(ext, md, pos, length, gw)
        for g in range(pg):
            yg = _dot(ps[g].astype(BF16), pw_ref[g]) * ps_ref[:, g * gw:(g + 1) * gw]
            out_ref[:, g * gw:(g + 1) * gw] = cur[:, g * gw:(g + 1) * gw] + md[G1:G1 + 1, g * gw:(g + 1) * gw] * yg

    j = layer // N_MIXERS
    return pl.pallas_call(
        body, name=f"pool_fwd_{layer}", grid=(rows // tm,),
        in_specs=[*_halo_specs(tm, d, rows), _mods_spec(nct, groups, d),
                  pl.BlockSpec((None, pg, gw, gw), lambda i: (j, 0, 0, 0)), _full((1, d))],
        out_specs=_rows(tm, d),
        out_shape=jax.ShapeDtypeStruct((rows, d), F32),
        compiler_params=_params(),
    )(h, h, h, mods, pw, pscale[j:j + 1])


def _pool_bwd_weights(h, dh1, mods, pw, pscale, layer, *, nct, tm, seg_lens):
    rows, d = h.shape
    groups = mods.shape[0]
    pg, gw = pw.shape[1], pw.shape[-1]

    def body(prev_ref, cur_ref, next_ref, g_ref, md_ref, pw_ref, ps_ref, dp_ref, dmd_ref, dps_ref, dpw_ref):
        i = pl.program_id(0)
        md = md_ref[...]
        ext = jnp.concatenate([prev_ref[...], cur_ref[...], next_ref[...]], axis=0)
        pos, length = _segment_positions(i, tm, nct, groups, seg_lens)
        ps, _ = _pooled(ext, md, pos, length, gw)
        gup = g_ref[...]

        @pl.when(i == 0)
        def _():
            dps_ref[...] = jnp.zeros(dps_ref.shape, F32)
            dpw_ref[...] = jnp.zeros(dpw_ref.shape, F32)

        dg1 = []
        for g in range(pg):
            cols = slice(g * gw, (g + 1) * gw)
            pb = ps[g].astype(BF16)
            yp = _dot(pb, pw_ref[g])
            sc = ps_ref[:, cols]
            dg1.append(_rowsum(gup[:, cols] * (yp * sc)))
            dy = gup[:, cols] * md[G1:G1 + 1, cols]
            dps_ref[0:1, cols] += _rowsum(dy * yp)
            dyp = (dy * sc).astype(BF16)
            dp_ref[:, cols] = _dot_nt(dyp, pw_ref[g])
            dpw_ref[g] += _dot_tn(pb, dyp)
        _acc_rows(dmd_ref, _first_of_group(i, nct, groups), {G1: jnp.concatenate(dg1, axis=1)})

    j = layer // N_MIXERS
    return pl.pallas_call(
        body, name=f"pool_bwd_w_{layer}", grid=(rows // tm,),
        in_specs=[*_halo_specs(tm, d, rows), _rows(tm, d), _mods_spec(nct, groups, d),
                  pl.BlockSpec((None, pg, gw, gw), lambda i: (j, 0, 0, 0)), _full((1, d))],
        out_specs=[_rows(tm, d), _mods_spec(nct, groups, d), _full((8, d)), _full((pg, gw, gw))],
        out_shape=[jax.ShapeDtypeStruct((rows, d), F32), jax.ShapeDtypeStruct((groups, 8, d), F32),
                   jax.ShapeDtypeStruct((8, d), F32), jax.ShapeDtypeStruct((pg, gw, gw), F32)],
        compiler_params=_params(),
    )(h, h, h, dh1, mods, pw, pscale[j:j + 1])


def _pool_bwd_input(dp, h, dh1, mods, layer, *, nct, tm, seg_lens, gw):
    rows, d = h.shape
    groups = mods.shape[0]

    def body(prev_ref, cur_ref, next_ref, h_ref, g_ref, md_ref, dh_ref, dmd_ref):
        i = pl.program_id(0)
        md = md_ref[...]
        dp_cur = cur_ref[...]
        ext = jnp.concatenate([prev_ref[...], dp_cur, next_ref[...]], axis=0)
        pos, length = _segment_positions(i, tm, nct, groups, seg_lens)
        valid = jnp.logical_and(pos >= 0, pos < length)
        das = []
        for g, w in enumerate(POOL_WINDOWS):
            cols = slice(g * gw, (g + 1) * gw)
            q = jnp.where(valid, ext[:, cols] / jnp.maximum(_window_count(pos, length, w), 1.0), 0.0)
            das.append(_window_sum(q, w, w // 2 - 1) - dp_cur[:, cols])
        da = jnp.concatenate(das, axis=1)
        _, parts = _normmod(h_ref[...], md, 0)
        dx, dsh, dsc, dng = _normmod_bwd(da, parts, md, 0)
        dh_ref[...] = g_ref[...] + dx
        _acc_rows(dmd_ref, _first_of_group(i, nct, groups), {SH1: dsh, SC1: dsc, NG0: dng})

    return pl.pallas_call(
        body, name=f"pool_bwd_x_{layer}", grid=(rows // tm,),
        in_specs=[*_halo_specs(tm, d, rows), _rows(tm, d), _rows(tm, d), _mods_spec(nct, groups, d)],
        out_specs=[_rows(tm, d), _mods_spec(nct, groups, d)],
        out_shape=[jax.ShapeDtypeStruct((rows, d), F32), jax.ShapeDtypeStruct((groups, 8, d), F32)],
        compiler_params=_params(),
    )(dp, dp, dp, h, dh1, mods)


def _rope_tables(n_ctx, seq, hd):
    half = hd // 2
    t = jnp.arange(seq)
    row = (t // GRID_W).astype(F32)
    col = (t % GRID_W).astype(F32)
    inv = ROPE_BASE ** (-jnp.arange(0, half, 2, dtype=F32) / half)
    ar = row[:, None] * inv[None, :]
    ac = col[:, None] * inv[None, :]
    cos = jnp.concatenate([jnp.cos(ar), jnp.cos(ar), jnp.cos(ac), jnp.cos(ac)], axis=1)
    sin = jnp.concatenate([-jnp.sin(ar), jnp.sin(ar), -jnp.sin(ac), jnp.sin(ac)], axis=1)
    cos = jnp.concatenate([jnp.ones((n_ctx, hd), F32), cos], axis=0)
    sin = jnp.concatenate([jnp.zeros((n_ctx, hd), F32), sin], axis=0)
    return cos, sin


def _rope_partner(x):
    hd = x.shape[-1]
    q = hd // 4
    lane = lax.broadcasted_iota(jnp.int32, x.shape, 1)
    first = (lane % (2 * q)) < q
    return jnp.where(first, pltpu.roll(x, hd - q, axis=1), pltpu.roll(x, q, axis=1))


def _normmod_call(h, mods, which, name, *, nct, tm):
    rows, d = h.shape
    groups = mods.shape[0]

    def body(h_ref, md_ref, a_ref):
        a, _ = _normmod(h_ref[...], md_ref[...], which)
        a_ref[...] = a.astype(BF16)

    return pl.pallas_call(
        body, name=name, grid=(rows // tm,),
        in_specs=[_rows(tm, d), _mods_spec(nct, groups, d)],
        out_specs=_rows(tm, d), out_shape=jax.ShapeDtypeStruct((rows, d), BF16),
        compiler_params=_params(),
    )(h, mods)


def _qkv_fwd(xa, wqkv, cos, sin, gains, *, nh, nkv, nct, tm):
    rows, d = xa.shape
    qw = wqkv.shape[-1]
    hd = cos.shape[-1]

    def body(x_ref, w_ref, cos_ref, sin_ref, gn_ref, qkv_ref, q_ref, k_ref, v_ref):
        qkv = _dot(x_ref[...], w_ref[...])
        qkv_ref[...] = qkv
        c, s = cos_ref[...], sin_ref[...]
        for hh in range(nh + nkv):
            xh = qkv[:, hh * hd:(hh + 1) * hd]
            xhat, _ = _rms_parts(xh)
            y = xhat * (gn_ref[0:1, :] if hh < nh else gn_ref[1:2, :])
            rot = (y * c + _rope_partner(y) * s).astype(BF16)
            if hh < nh:
                q_ref[:, hh * hd:(hh + 1) * hd] = rot
            else:
                k_ref[:, (hh - nh) * hd:(hh - nh + 1) * hd] = rot
        v_ref[...] = qkv[:, (nh + nkv) * hd:].astype(BF16)

    return pl.pallas_call(
        body, name="attn_qkv_fwd", grid=(rows // tm,),
        in_specs=[_rows(tm, d), pl.BlockSpec((None, d, qw), lambda i: (0, 0, 0)), _rows(tm, hd), _rows(tm, hd),
                  _full((8, hd))],
        out_specs=[_rows(tm, qw), pl.BlockSpec((tm, nh * hd), lambda i: (jnp.maximum(i - nct, 0), 0)),
                   _rows(tm, nkv * hd), _rows(tm, nkv * hd)],
        out_shape=[jax.ShapeDtypeStruct((rows, qw), F32), jax.ShapeDtypeStruct((rows - nct * tm, nh * hd), BF16),
                   jax.ShapeDtypeStruct((rows, nkv * hd), BF16), jax.ShapeDtypeStruct((rows, nkv * hd), BF16)],
        compiler_params=_params(),
    )(xa, wqkv, cos, sin, gains)


def _attn_tiles(seq, total):
    return _div_tile(seq, 512), _div_tile(total, 768)


def _stack_heads(ref_val, hd):
    return jnp.concatenate([ref_val[:, :hd], ref_val[:, hd:]], axis=0)


def _flash_fwd(q, k, v, *, n_ctx, hd):
    total = k.shape[0]
    seq = total - n_ctx
    nkv = k.shape[1] // hd
    tq, tk = _attn_tiles(seq, total)
    nk = total // tk
    scale = hd ** -0.5

    def body(q_ref, k_ref, v_ref, o_ref, lse_ref, m_sc, l_sc, acc_sc):
        ki = pl.program_id(2)

        @pl.when(ki == 0)
        def _():
            m_sc[...] = jnp.full(m_sc.shape, -jnp.inf, F32)
            l_sc[...] = jnp.zeros(l_sc.shape, F32)
            acc_sc[...] = jnp.zeros(acc_sc.shape, F32)

        q2 = _stack_heads(q_ref[...], hd)
        s = _dot_nt(q2, k_ref[...]) * scale
        m_old = m_sc[...]
        m_new = jnp.maximum(m_old, jnp.max(s, axis=-1, keepdims=True))
        alpha = jnp.exp(m_old - m_new)
        p = jnp.exp(s - m_new)
        l_sc[...] = alpha * l_sc[...] + jnp.sum(p, axis=-1, keepdims=True)
        acc_sc[...] = alpha * acc_sc[...] + _dot(p.astype(BF16), v_ref[...])
        m_sc[...] = m_new

        @pl.when(ki == nk - 1)
        def _():
            o2 = acc_sc[...] / l_sc[...]
            lse = m_sc[...] + jnp.log(l_sc[...])
            o_ref[:, :hd] = o2[:tq].astype(BF16)
            o_ref[:, hd:] = o2[tq:].astype(BF16)
            lse_ref[:, 0:1] = lse[:tq]
            lse_ref[:, 1:2] = lse[tq:]

    return pl.pallas_call(
        body, name="attn_flash_fwd", grid=(nkv, seq // tq, nk),
        in_specs=[pl.BlockSpec((tq, 2 * hd), lambda h, i, j: (i, h)),
                  pl.BlockSpec((tk, hd), lambda h, i, j: (j, h)),
                  pl.BlockSpec((tk, hd), lambda h, i, j: (j, h))],
        out_specs=[pl.BlockSpec((tq, 2 * hd), lambda h, i, j: (i, h)),
                   pl.BlockSpec((None, tq, 2), lambda h, i, j: (h, i, 0))],
        out_shape=[jax.ShapeDtypeStruct((seq, 2 * nkv * hd), BF16), jax.ShapeDtypeStruct((nkv, seq, 2), F32)],
        scratch_shapes=[pltpu.VMEM((2 * tq, 1), F32), pltpu.VMEM((2 * tq, 1), F32), pltpu.VMEM((2 * tq, hd), F32)],
        compiler_params=_params(),
    )(q, k, v)


def _flash_bwd(q, k, v, o, do, lse, *, n_ctx, hd):
    total = k.shape[0]
    seq = total - n_ctx
    nkv = k.shape[1] // hd
    tq, tk = _attn_tiles(seq, total)
    scale = hd ** -0.5

    def body(q_ref, k_ref, v_ref, o_ref, do_ref, lse_ref, dq_ref, dk_ref, dv_ref):
        ki, qi = pl.program_id(1), pl.program_id(2)
        q2 = _stack_heads(q_ref[...], hd)
        do2 = _stack_heads(do_ref[...], hd)
        o2 = _stack_heads(o_ref[...], hd)
        lse2 = jnp.concatenate([lse_ref[:, 0:1], lse_ref[:, 1:2]], axis=0)
        delta = jnp.sum(do2.astype(F32) * o2.astype(F32), axis=-1, keepdims=True)
        kk, vv = k_ref[...], v_ref[...]
        p = jnp.exp(_dot_nt(q2, kk) * scale - lse2)
        dp = _dot_nt(do2, vv)
        ds = (p * (dp - delta) * scale).astype(BF16)

        @pl.when(qi == 0)
        def _():
            dk_ref[...] = jnp.zeros(dk_ref.shape, F32)
            dv_ref[...] = jnp.zeros(dv_ref.shape, F32)

        dv_ref[...] += _dot_tn(p.astype(BF16), do2)
        dk_ref[...] += _dot_tn(ds, q2)
        dq2 = _dot(ds, kk)
        rows = pl.ds(pl.multiple_of(qi * tq, tq), tq)

        @pl.when(ki == 0)
        def _():
            dq_ref[rows, :hd] = dq2[:tq]
            dq_ref[rows, hd:] = dq2[tq:]

        @pl.when(ki > 0)
        def _():
            dq_ref[rows, :hd] += dq2[:tq]
            dq_ref[rows, hd:] += dq2[tq:]

    return pl.pallas_call(
        body, name="attn_flash_bwd", grid=(nkv, total // tk, seq // tq),
        in_specs=[pl.BlockSpec((tq, 2 * hd), lambda h, j, i: (i, h)),
                  pl.BlockSpec((tk, hd), lambda h, j, i: (j, h)),
                  pl.BlockSpec((tk, hd), lambda h, j, i: (j, h)),
                  pl.BlockSpec((tq, 2 * hd), lambda h, j, i: (i, h)),
                  pl.BlockSpec((tq, 2 * hd), lambda h, j, i: (i, h)),
                  pl.BlockSpec((None, tq, 2), lambda h, j, i: (h, i, 0))],
        out_specs=[pl.BlockSpec((seq, 2 * hd), lambda h, j, i: (0, h)),
                   pl.BlockSpec((tk, hd), lambda h, j, i: (j, h)),
                   pl.BlockSpec((tk, hd), lambda h, j, i: (j, h))],
        out_shape=[jax.ShapeDtypeStruct((seq, 2 * nkv * hd), F32), jax.ShapeDtypeStruct((total, nkv * hd), F32),
                   jax.ShapeDtypeStruct((total, nkv * hd), F32)],
        compiler_params=_params(),
    )(q, k, v, o, do, lse)


def _proj_fwd(o, wo, hc, mods, *, n_ctx, tm):
    seq, d = o.shape
    off = n_ctx // tm

    def body(o_ref, w_ref, h_ref, md_ref, h1_ref, y_ref):
        y = _dot(o_ref[...], w_ref[...])
        y_ref[...] = y.astype(BF16)
        h1_ref[...] = h_ref[...] + md_ref[G1:G1 + 1, :] * y

    return pl.pallas_call(
        body, name="attn_proj_fwd", grid=(seq // tm,),
        in_specs=[_rows(tm, d), pl.BlockSpec((None, d, d), lambda i: (0, 0, 0)),
                  pl.BlockSpec((tm, d), lambda i: (i + off, 0)), pl.BlockSpec((None, 8, d), lambda i: (1, 0, 0))],
        out_specs=[_rows(tm, d), _rows(tm, d)],
        out_shape=[jax.ShapeDtypeStruct((seq, d), F32), jax.ShapeDtypeStruct((seq, d), BF16)],
        compiler_params=_params(),
    )(o, wo, hc, mods)


def _proj_bwd(dh1, y, mods, wo, *, tm):
    seq, d = dh1.shape

    def body(g_ref, y_ref, md_ref, w_ref, do_ref, dyb_ref, dmd_ref):
        i = pl.program_id(0)
        g = g_ref[...]
        dyb = (g * md_ref[G1:G1 + 1, :]).astype(BF16)
        dyb_ref[...] = dyb
        do_ref[...] = _dot_nt(dyb, w_ref[...]).astype(BF16)
        _acc_rows(dmd_ref, i == 0, {G1: _rowsum(g * y_ref[...].astype(F32))})

    return pl.pallas_call(
        body, name="attn_proj_bwd", grid=(seq // tm,),
        in_specs=[_rows(tm, d), _rows(tm, d), pl.BlockSpec((None, 8, d), lambda i: (1, 0, 0)),
                  pl.BlockSpec((None, d, d), lambda i: (0, 0, 0))],
        out_specs=[_rows(tm, d), _rows(tm, d), pl.BlockSpec((None, 8, d), lambda i: (0, 0, 0))],
        out_shape=[jax.ShapeDtypeStruct((seq, d), BF16), jax.ShapeDtypeStruct((seq, d), BF16),
                   jax.ShapeDtypeStruct((1, 8, d), F32)],
        compiler_params=_params(),
    )(dh1, y, mods, wo)


def _qkv_bwd(qkv, dq, dk, dv, cos, sin, gains, *, nh, nkv, nct, tm):
    rows, qw = qkv.shape
    hd = cos.shape[-1]

    def body(qkv_ref, dq_ref, dk_ref, dv_ref, cos_ref, sin_ref, gn_ref, out_ref, dgn_ref):
        i = pl.program_id(0)
        c, s = cos_ref[...], sin_ref[...]
        is_lat = (i >= nct).astype(F32)
        dqg = jnp.zeros((1, hd), F32)
        dkg = jnp.zeros((1, hd), F32)
        for hh in range(nh + nkv):
            if hh < nh:
                dr = dq_ref[:, hh * hd:(hh + 1) * hd] * is_lat
                gn = gn_ref[0:1, :]
            else:
                dr = dk_ref[:, (hh - nh) * hd:(hh - nh + 1) * hd]
                gn = gn_ref[1:2, :]
            dy = dr * c + _rope_partner(dr * s)
            xhat, r = _rms_parts(qkv_ref[:, hh * hd:(hh + 1) * hd])
            dgh = _rowsum(dy * xhat)
            if hh < nh:
                dqg = dqg + dgh
            else:
                dkg = dkg + dgh
            dxhat = dy * gn
            dx = r * (dxhat - xhat * jnp.mean(dxhat * xhat, axis=-1, keepdims=True))
            out_ref[:, hh * hd:(hh + 1) * hd] = dx.astype(BF16)
        out_ref[:, (nh + nkv) * hd:] = dv_ref[...].astype(BF16)
        _acc_rows(dgn_ref, i == 0, {0: dqg, 1: dkg})

    return pl.pallas_call(
        body, name="attn_qkv_bwd", grid=(rows // tm,),
        in_specs=[_rows(tm, qw), pl.BlockSpec((tm, nh * hd), lambda i: (jnp.maximum(i - nct, 0), 0)),
                  _rows(tm, nkv * hd), _rows(tm, nkv * hd), _rows(tm, hd), _rows(tm, hd), _full((8, hd))],
        out_specs=[_rows(tm, qw), _full((8, hd))],
        out_shape=[jax.ShapeDtypeStruct((rows, qw), BF16), jax.ShapeDtypeStruct((8, hd), F32)],
        compiler_params=_params(),
    )(qkv, dq, dk, dv, cos, sin, gains)


def _attn_in_bwd(dqkv, wqkv, hc, dh1, mods, *, nct, tm):
    rows, d = hc.shape
    qw = dqkv.shape[1]

    def body(dz_ref, w_ref, h_ref, g_ref, md_ref, dh_ref, dmd_ref):
        i = pl.program_id(0)
        md = md_ref[...]
        da = _dot_nt(dz_ref[...], w_ref[...])
        _, parts = _normmod(h_ref[...], md, 0)
        dx, dsh, dsc, dng = _normmod_bwd(da, parts, md, 0)
        dh_ref[...] = g_ref[...] * (i >= nct).astype(F32) + dx
        _acc_rows(dmd_ref, _first_of_group(i, nct, 2), {SH1: dsh, SC1: dsc, NG0: dng})

    return pl.pallas_call(
        body, name="attn_in_bwd", grid=(rows // tm,),
        in_specs=[_rows(tm, qw), pl.BlockSpec((None, d, qw), lambda i: (0, 0, 0)), _rows(tm, d),
                  pl.BlockSpec((tm, d), lambda i: (jnp.maximum(i - nct, 0), 0)), _mods_spec(nct, 2, d)],
        out_specs=[_rows(tm, d), _mods_spec(nct, 2, d)],
        out_shape=[jax.ShapeDtypeStruct((rows, d), F32), jax.ShapeDtypeStruct((2, 8, d), F32)],
        compiler_params=_params(),
    )(dqkv, wqkv, hc, dh1, mods)


def _gmlp_gate(zp, lng, lnb, ws_ref, bs_ref, gg, ch):
    half = zp.shape[1] // 2
    ggw = half // gg
    z = _gelu(zp)
    u, v = z[:, :half], z[:, half:]
    vc = v - jnp.mean(v, axis=-1, keepdims=True)
    rs = lax.rsqrt(jnp.mean(vc * vc, axis=-1, keepdims=True) + EPS)
    vhat = vc * rs
    vln = (vhat * lng + lnb).astype(BF16)
    chunks = []
    for n in range(zp.shape[0] // ch):
        groups = []
        for g in range(gg):
            groups.append(_dot(ws_ref[g], vln[n * ch:(n + 1) * ch, g * ggw:(g + 1) * ggw]) + bs_ref[g])
        chunks.append(jnp.concatenate(groups, axis=1))
    sv = jnp.concatenate(chunks, axis=0) if len(chunks) > 1 else chunks[0]
    return u, sv, vhat, rs, vln


def _gmlp_fwd(h, mods, w_in, lng, lnb, ws, bs, w_out, *, tm):
    seq, d = h.shape
    zw = w_in.shape[-1]
    half = zw // 2
    gg, ch = ws.shape[0], ws.shape[-1]

    def body(h_ref, md_ref, win_ref, lng_ref, lnb_ref, ws_ref, bs_ref, wout_ref, h1_ref, zp_ref, y_ref):
        x = h_ref[...]
        md = md_ref[...]
        a, _ = _normmod(x, md, 0)
        zp = _dot(a.astype(BF16), win_ref[...])
        zp_ref[...] = zp.astype(BF16)
        u, sv, _, _, _ = _gmlp_gate(zp, lng_ref[...], lnb_ref[...], ws_ref, bs_ref, gg, ch)
        y = _dot((u * sv).astype(BF16), wout_ref[...])
        y_ref[...] = y.astype(BF16)
        h1_ref[...] = x + md[G1:G1 + 1] * y

    return pl.pallas_call(
        body, name="gmlp_fwd", grid=(seq // tm,),
        in_specs=[_rows(tm, d), pl.BlockSpec((None, 8, d), lambda i: (1, 0, 0)),
                  pl.BlockSpec((None, d, zw), lambda i: (0, 0, 0)), _full((1, half)), _full((1, half)),
                  _full((gg, ch, ch)), _full((gg, ch, 1)), pl.BlockSpec((None, half, d), lambda i: (0, 0, 0))],
        out_specs=[_rows(tm, d), _rows(tm, zw), _rows(tm, d)],
        out_shape=[jax.ShapeDtypeStruct((seq, d), F32), jax.ShapeDtypeStruct((seq, zw), BF16),
                   jax.ShapeDtypeStruct((seq, d), BF16)],
        compiler_params=_params(),
    )(h, mods, w_in, lng, lnb, ws, bs, w_out)


def _gmlp_bwd(h, dh1, zpre, y, mods, w_in, lng, lnb, ws, ws_t, bs, w_out, *, tm):
    seq, d = h.shape
    zw = w_in.shape[-1]
    half = zw // 2
    gg, ch = ws.shape[0], ws.shape[-1]
    ggw = half // gg

    def body(h_ref, g_ref, zp_ref, y_ref, md_ref, win_ref, lng_ref, lnb_ref, ws_ref, wst_ref, bs_ref, wout_ref,
             dh_ref, dzp_ref, gated_ref, dyb_ref, ab_ref, dmd_ref, dln_ref, dws_ref, dbs_ref):
        i = pl.program_id(0)
        x = h_ref[...]
        md = md_ref[...]
        a, parts = _normmod(x, md, 0)
        ab_ref[...] = a.astype(BF16)
        zp = zp_ref[...].astype(F32)
        lng_v = lng_ref[...]
        u, sv, vhat, rs, vln = _gmlp_gate(zp, lng_v, lnb_ref[...], ws_ref, bs_ref, gg, ch)
        g = g_ref[...]
        dg1 = _rowsum(g * y_ref[...].astype(F32))
        dyb = (g * md[G1:G1 + 1]).astype(BF16)
        dyb_ref[...] = dyb
        gated_ref[...] = (u * sv).astype(BF16)
        dgated = _dot_nt(dyb, wout_ref[...])
        du = dgated * sv
        dsv = dgated * u

        @pl.when(i == 0)
        def _():
            dws_ref[...] = jnp.zeros(dws_ref.shape, F32)
            dbs_ref[...] = jnp.zeros(dbs_ref.shape, F32)
            dln_ref[...] = jnp.zeros(dln_ref.shape, F32)

        chunks = []
        for n in range(tm // ch):
            groups = []
            for gi in range(gg):
                blk = dsv[n * ch:(n + 1) * ch, gi * ggw:(gi + 1) * ggw]
                dbs_ref[gi] += jnp.sum(blk, axis=-1, keepdims=True)
                blk_b = blk.astype(BF16)
                dws_ref[gi] += _dot_nt(blk_b, vln[n * ch:(n + 1) * ch, gi * ggw:(gi + 1) * ggw])
                groups.append(_dot(wst_ref[gi], blk_b))
            chunks.append(jnp.concatenate(groups, axis=1))
        dvln = jnp.concatenate(chunks, axis=0) if len(chunks) > 1 else chunks[0]
        dln_ref[0:1, :] += _rowsum(dvln * vhat)
        dln_ref[1:2, :] += _rowsum(dvln)
        dvhat = dvln * lng_v
        dv = rs * (dvhat - jnp.mean(dvhat, axis=-1, keepdims=True)
                   - vhat * jnp.mean(dvhat * vhat, axis=-1, keepdims=True))
        dzp = (jnp.concatenate([du, dv], axis=1) * _gelu_grad(zp)).astype(BF16)
        dzp_ref[...] = dzp
        da = _dot_nt(dzp, win_ref[...])
        dx, dsh, dsc, dng = _normmod_bwd(da, parts, md, 0)
        dh_ref[...] = g + dx
        _acc_rows(dmd_ref, i == 0, {SH1: dsh, SC1: dsc, G1: dg1, NG0: dng})

    return pl.pallas_call(
        body, name="gmlp_bwd", grid=(seq // tm,),
        in_specs=[_rows(tm, d), _rows(tm, d), _rows(tm, zw), _rows(tm, d),
                  pl.BlockSpec((None, 8, d), lambda i: (1, 0, 0)),
                  pl.BlockSpec((None, d, zw), lambda i: (0, 0, 0)), _full((1, half)), _full((1, half)),
                  _full((gg, ch, ch)), _full((gg, ch, ch)), _full((gg, ch, 1)),
                  pl.BlockSpec((None, half, d), lambda i: (0, 0, 0))],
        out_specs=[_rows(tm, d), _rows(tm, zw), _rows(tm, half), _rows(tm, d), _rows(tm, d),
                   pl.BlockSpec((None, 8, d), lambda i: (0, 0, 0)), _full((8, half)), _full((gg, ch, ch)),
                   _full((gg, ch, 1))],
        out_shape=[jax.ShapeDtypeStruct((seq, d), F32), jax.ShapeDtypeStruct((seq, zw), BF16),
                   jax.ShapeDtypeStruct((seq, half), BF16), jax.ShapeDtypeStruct((seq, d), BF16),
                   jax.ShapeDtypeStruct((seq, d), BF16), jax.ShapeDtypeStruct((1, 8, d), F32),
                   jax.ShapeDtypeStruct((8, half), F32), jax.ShapeDtypeStruct((gg, ch, ch), F32),
                   jax.ShapeDtypeStruct((gg, ch, 1), F32)],
        compiler_params=_params(),
    )(h, dh1, zpre, y, mods, w_in, lng, lnb, ws, ws_t, bs, w_out)


def _final_loss(h, tgt, fg, *, tm):
    seq, d = h.shape

    def body(h_ref, t_ref, g_ref, dh_ref, acc_ref):
        i = pl.program_id(0)
        gain = g_ref[...]
        xhat, r = _rms_parts(h_ref[...])
        err = xhat * gain - t_ref[...]
        dy = err * (1.0 / d)
        dxhat = dy * gain
        dh_ref[...] = r * (dxhat - xhat * jnp.mean(dxhat * xhat, axis=-1, keepdims=True))
        part = jnp.sum(_rowsum(err * err), axis=-1, keepdims=True) * (0.5 / d)
        _acc_rows(acc_ref, i == 0, {0: _rowsum(dy * xhat), 1: jnp.broadcast_to(part, (1, d))})

    return pl.pallas_call(
        body, name="final_loss", grid=(seq // tm,),
        in_specs=[_rows(tm, d), _rows(tm, d), _full((1, d))],
        out_specs=[_rows(tm, d), _full((8, d))],
        out_shape=[jax.ShapeDtypeStruct((seq, d), F32), jax.ShapeDtypeStruct((8, d), F32)],
        compiler_params=_params(),
    )(h, tgt, fg)


def _ada_fwd(c_all, ada_w, ada_b_cols):
    depth, d, ncs = ada_w.shape

    def body(c_ref, w_ref, b_ref, o_ref):
        s = _silu(c_ref[...]).astype(BF16)
        o_ref[...] = _dot(s, w_ref[...].astype(BF16)) + b_ref[...]

    return pl.pallas_call(
        body, name="ada_fwd", grid=(depth,),
        in_specs=[_full((16, d)), pl.BlockSpec((None, d, ncs), lambda i: (i, 0, 0)),
                  pl.BlockSpec((None, 1, ncs), lambda i: (i, 0, 0))],
        out_specs=pl.BlockSpec((None, 16, ncs), lambda i: (i, 0, 0)),
        out_shape=jax.ShapeDtypeStruct((depth, 16, ncs), F32),
        compiler_params=_params(),
    )(c_all, ada_w, ada_b_cols.reshape(depth, 1, ncs))


def _ada_bwd(c_all, c_all_t, dmod, ada_w):
    depth, d, ncs = ada_w.shape

    def body(c_ref, ct_ref, dm_ref, w_ref, gw_ref, dc_ref):
        i = pl.program_id(0)
        dm = dm_ref[...]
        dctx = _rowsum(dm[8:16])
        rid = lax.broadcasted_iota(jnp.int32, (8, ncs), 0)
        low = jnp.where(rid == 0, jnp.broadcast_to(dctx, (8, ncs)), 0.0)
        dm16 = jnp.concatenate([dm[0:8], low], axis=0).astype(BF16)
        gw_ref[...] = _dot(_silu(ct_ref[...]).astype(BF16), dm16)

        @pl.when(i == 0)
        def _():
            dc_ref[...] = jnp.zeros(dc_ref.shape, F32)

        dc_ref[...] += _dot_nt(low.astype(BF16), w_ref[...].astype(BF16)) * _silu_grad(c_ref[8:9, :])

    return pl.pallas_call(
        body, name="ada_bwd", grid=(depth,),
        in_specs=[_full((16, d)), _full((d, 16)), pl.BlockSpec((None, 16, ncs), lambda i: (i, 0, 0)),
                  pl.BlockSpec((None, d, ncs), lambda i: (i, 0, 0))],
        out_specs=[pl.BlockSpec((None, d, ncs), lambda i: (i, 0, 0)), _full((8, d))],
        out_shape=[jax.ShapeDtypeStruct((depth, d, ncs), F32), jax.ShapeDtypeStruct((8, d), F32)],
        compiler_params=_params(),
    )(c_all, c_all_t, dmod, ada_w)


def _adamw_math(w, g, m, v):
    m = ADAM_B1 * m + (1.0 - ADAM_B1) * g
    v = ADAM_B2 * v + (1.0 - ADAM_B2) * jnp.square(g)
    m_hat = m / (1.0 - ADAM_B1 ** ADAM_STEP)
    v_hat = v / (1.0 - ADAM_B2 ** ADAM_STEP)
    delta = -ADAM_LR * (m_hat / (jnp.sqrt(v_hat) + ADAM_EPS) + ADAM_WD * w)
    return delta, m, v


def _adamw(ga, gb, w, m, v, name):
    rows, cols = w.shape
    tr = rows
    while tr * cols * 4 > (1 << 20) and tr % 16 == 0:
        tr //= 2

    def body(ga_ref, gb_ref, w_ref, m_ref, v_ref, g_out, d_out, m_out, v_out):
        g = ga_ref[...] + gb_ref[...]
        delta, m_new, v_new = _adamw_math(w_ref[...], g, m_ref[...], v_ref[...])
        g_out[...] = g
        d_out[...] = delta
        m_out[...] = m_new
        v_out[...] = v_new

    spec = _rows(tr, cols)
    return pl.pallas_call(
        body, name=name, grid=(rows // tr,),
        in_specs=[spec] * 5, out_specs=[spec] * 4,
        out_shape=[jax.ShapeDtypeStruct((rows, cols), F32)] * 4,
        compiler_params=_params(),
    )(ga, gb, w, m, v)


def _sum_devices(gathered, name):
    n, rows, cols = gathered.shape
    tr = rows
    while tr * cols * 4 * n > (4 << 20) and tr % 16 == 0:
        tr //= 2

    def body(x_ref, o_ref):
        acc = x_ref[0]
        for j in range(1, n):
            acc = acc + x_ref[j]
        o_ref[...] = acc

    return pl.pallas_call(
        body, name=name, grid=(rows // tr,),
        in_specs=[pl.BlockSpec((n, tr, cols), lambda i: (0, i, 0))], out_specs=_rows(tr, cols),
        out_shape=jax.ShapeDtypeStruct((rows, cols), F32),
        compiler_params=_params(),
    )(gathered)


def _sum_partials(own, landed, name):
    rows, cols = own.shape
    tr = rows
    while tr * cols * 2 > (1 << 20) and tr % 32 == 0:
        tr //= 2

    def body(o_ref, l_ref, out_ref):
        acc = o_ref[...].astype(F32)
        for p in range(3):
            acc = acc + l_ref[p].astype(F32)
        out_ref[...] = acc

    return pl.pallas_call(
        body, name=name, grid=(rows // tr,),
        in_specs=[_rows(tr, cols), pl.BlockSpec((3, tr, cols), lambda i: (0, i, 0))], out_specs=_rows(tr, cols),
        out_shape=jax.ShapeDtypeStruct((rows, cols), F32),
        compiler_params=_params(),
    )(own, landed)


def _my_place():
    return lax.axis_index("x"), lax.axis_index("y"), lax.axis_index("c")


def _other_chips(x, y):
    return [(1 - x, y), (x, 1 - y), (1 - x, 1 - y)]


def _all_gather_small(block, name):
    rows, cols = block.shape

    def body(x_ref, out_ref, send_sems, recv_sems, local_sem):
        x, y, c = _my_place()
        me, sibling = (x, y, c), (x, y, 1 - c)
        chips = _other_chips(x, y)

        def slot(px, py, pc):
            return out_ref.at[4 * px + 2 * py + pc]

        def copy(k, blk, to, src=None):
            return pltpu.make_async_remote_copy(
                src_ref=slot(*blk) if src is None else src, dst_ref=slot(*blk),
                send_sem=send_sems.at[k], recv_sem=recv_sems.at[k], device_id=to, device_id_type=MESH)

        mine = pltpu.make_async_copy(x_ref, slot(*me), local_sem)
        mine.start()
        first = [copy(0, me, sibling, src=x_ref)]
        first += [copy(1 + j, me, (*chip, c), src=x_ref) for j, chip in enumerate(chips)]
        for cp in first:
            cp.start()
        passed = [copy(4 + j, (*chip, c), sibling) for j, chip in enumerate(chips)]
        for j, chip in enumerate(chips):
            copy(1 + j, (*chip, c), me).wait_recv()
            passed[j].start()
        copy(0, sibling, me).wait_recv()
        for j, chip in enumerate(chips):
            copy(4 + j, (*chip, 1 - c), me).wait_recv()
        for cp in first + passed:
            cp.wait_send()
        mine.wait()

    return pl.pallas_call(
        body, name=name,
        out_shape=jax.ShapeDtypeStruct((N_DEV, rows, cols), block.dtype),
        in_specs=[pl.BlockSpec(memory_space=pltpu.VMEM)],
        out_specs=pl.BlockSpec(memory_space=pltpu.VMEM),
        scratch_shapes=[pltpu.SemaphoreType.DMA((7,)), pltpu.SemaphoreType.DMA((7,)), pltpu.SemaphoreType.DMA],
        compiler_params=_params(),
    )(block)


def _shard_slice(ref, axis, k, size):
    idx = [slice(None)] * len(ref.shape)
    idx[axis] = pl.ds(k * size, size)
    return ref.at[tuple(idx)]


def _gather_weights(shards, axes):
    n = len(shards)
    out_shapes = []
    for s, ax in zip(shards, axes):
        shp = list(s.shape)
        shp[ax] *= N_CHIPS
        out_shapes.append(jax.ShapeDtypeStruct(tuple(shp), s.dtype))

    def body(*refs):
        ins, outs = refs[:n], refs[n:2 * n]
        send_sems, recv_sems, local_sems = refs[2 * n:]
        x, y, c = _my_place()
        chips = _other_chips(x, y)
        for kk in range(N_CHIPS):
            @pl.when(2 * x + y == kk)
            def _(kk=kk):
                local, sends = [], []
                for j in range(n):
                    size = ins[j].shape[axes[j]]
                    mine = _shard_slice(outs[j], axes[j], kk, size)
                    local.append(pltpu.make_async_copy(ins[j], mine, local_sems.at[j]))
                    for p, chip in enumerate(chips):
                        sends.append(pltpu.make_async_remote_copy(
                            src_ref=ins[j], dst_ref=mine, send_sem=send_sems.at[3 * j + p],
                            recv_sem=recv_sems.at[3 * j + p], device_id=(*chip, c), device_id_type=MESH))
                for cp in local + sends:
                    cp.start()
                for cp in sends:
                    cp.wait()
                for cp in local:
                    cp.wait()

    any_spec = pl.BlockSpec(memory_space=pl.ANY)
    return pl.pallas_call(
        body, name="gather_weights", out_shape=out_shapes,
        in_specs=[any_spec] * n, out_specs=[any_spec] * n,
        scratch_shapes=[pltpu.SemaphoreType.DMA((3 * n,)), pltpu.SemaphoreType.DMA((3 * n,)),
                        pltpu.SemaphoreType.DMA((n,))],
        compiler_params=_params(),
    )(*shards)


def _scatter_grads(groups):
    flat = [g for layers, _ in groups for g in layers]
    n = len(flat)
    out_shapes = []
    for layers, ax in groups:
        shp = list(layers[0].shape)
        shp[ax] //= N_CHIPS
        out_shapes.append(jax.ShapeDtypeStruct((len(layers), *shp), layers[0].dtype))
        out_shapes.append(jax.ShapeDtypeStruct((3, len(layers), *shp), layers[0].dtype))

    def body(*refs):
        ins = refs[:n]
        outs = refs[n:n + 2 * len(groups)]
        send_sems, recv_sems, local_sems = refs[n + 2 * len(groups):]
        x, y, c = _my_place()
        chips = _other_chips(x, y)
        for kk in range(N_CHIPS):
            @pl.when(2 * x + y == kk)
            def _(kk=kk):
                kx, ky = kk // 2, kk % 2
                peer_k = [2 * (1 - kx) + ky, 2 * kx + (1 - ky), 2 * (1 - kx) + (1 - ky)]
                local, sends = [], []
                j = 0
                for gi, (layers, ax) in enumerate(groups):
                    own_ref, land_ref = outs[2 * gi], outs[2 * gi + 1]
                    size = layers[0].shape[ax] // N_CHIPS
                    for li in range(len(layers)):
                        src_own = _shard_slice(ins[j], ax, kk, size)
                        local.append(pltpu.make_async_copy(src_own, own_ref.at[li], local_sems.at[j]))
                        for p, chip in enumerate(chips):
                            sends.append(pltpu.make_async_remote_copy(
                                src_ref=_shard_slice(ins[j], ax, peer_k[p], size), dst_ref=land_ref.at[p, li],
                                send_sem=send_sems.at[3 * j + p], recv_sem=recv_sems.at[3 * j + p],
                                device_id=(*chip, c), device_id_type=MESH))
                        j += 1
                for cp in local + sends:
                    cp.start()
                for cp in sends:
                    cp.wait()
                for cp in local:
                    cp.wait()

    any_spec = pl.BlockSpec(memory_space=pl.ANY)
    return pl.pallas_call(
        body, name="scatter_grads", out_shape=out_shapes,
        in_specs=[any_spec] * n, out_specs=[any_spec] * len(out_shapes),
        scratch_shapes=[pltpu.SemaphoreType.DMA((3 * n,)), pltpu.SemaphoreType.DMA((3 * n,)),
                        pltpu.SemaphoreType.DMA((n,))],
        compiler_params=_params(),
    )(*flat)


def _swap_with_sibling(parts):
    n = len(parts)

    def body(*refs):
        ins, outs = refs[:n], refs[n:2 * n]
        send_sems, recv_sems = refs[2 * n:]
        x, y, c = _my_place()
        copies = [pltpu.make_async_remote_copy(
            src_ref=ins[j], dst_ref=outs[j], send_sem=send_sems.at[j], recv_sem=recv_sems.at[j],
            device_id=(x, y, 1 - c), device_id_type=MESH) for j in range(n)]
        for cp in copies:
            cp.start()
        for cp in copies:
            cp.wait()

    any_spec = pl.BlockSpec(memory_space=pl.ANY)
    return pl.pallas_call(
        body, name="swap_with_sibling", out_shape=[jax.ShapeDtypeStruct(p.shape, p.dtype) for p in parts],
        in_specs=[any_spec] * n, out_specs=[any_spec] * n,
        scratch_shapes=[pltpu.SemaphoreType.DMA((n,)), pltpu.SemaphoreType.DMA((n,))],
        compiler_params=_params(),
    )(*parts)


TILE_ELEMS = SUBLANES * LANES


def _pack(arrays):
    parts = []
    for a in arrays:
        flat = a.reshape(-1).astype(F32)
        pad = (-flat.shape[0]) % TILE_ELEMS
        if pad:
            flat = jnp.concatenate([flat, jnp.zeros((pad,), F32)])
        parts.append(flat.reshape(-1, LANES))
    return jnp.concatenate(parts, axis=0) if len(parts) > 1 else parts[0]


def _unpack(buf, shapes):
    out, r = [], 0
    lead = buf.shape[:-2]
    for shp in shapes:
        size = math.prod(shp)
        nr = -(-size // TILE_ELEMS) * SUBLANES
        flat = buf[..., r:r + nr, :].reshape(*lead, nr * LANES)[..., :size]
        out.append(flat.reshape(*lead, *shp))
        r += nr
    return out


def _chip_cols(a, k, width):
    return lax.dynamic_slice_in_dim(a, k * width, width, axis=a.ndim - 1)


def _across_chips(gathered, c0_only_shape):
    return gathered.reshape(2, 2, 2, *c0_only_shape)[:, :, 0].reshape(N_CHIPS, *c0_only_shape)


def kernel(x, c, ctx, c_ctx, ada_w, ada_b, norm_g, mlp_w1, mlp_w2, pool_w, pool_scale, attn_w_qkv, attn_w_o, attn_q_g, attn_k_g, gm_w_in, gm_ln_g, gm_ln_b, gm_ws, gm_bs, gm_w_out, final_g, loss_target, m_c_ctx, m_ada_w, m_ada_b, m_norm_g, m_mlp_w1, m_mlp_w2, m_pool_w, m_pool_scale, m_attn_w_qkv, m_attn_w_o, m_attn_q_g, m_attn_k_g, m_gm_w_in, m_gm_ln_g, m_gm_ln_b, m_gm_ws, m_gm_bs, m_gm_w_out, m_final_g, v_c_ctx, v_ada_w, v_ada_b, v_norm_g, v_mlp_w1, v_mlp_w2, v_pool_w, v_pool_scale, v_attn_w_qkv, v_attn_w_o, v_attn_q_g, v_attn_k_g, v_gm_w_in, v_gm_ln_g, v_gm_ln_b, v_gm_ws, v_gm_bs, v_gm_w_out, v_final_g):
    seq, d = x.shape[1], x.shape[2]
    n_ctx = ctx.shape[1]
    total = n_ctx + seq
    hd = attn_q_g.shape[-1]
    nh = d // hd
    nkv = nh // 2
    gg, ch = gm_ws.shape[1], gm_ws.shape[-1]
    half = gm_w_out.shape[1] * N_CHIPS
    pgw = pool_w.shape[-1]
    tm = min(256, n_ctx)
    nct = n_ctx // tm
    seg_lens = (n_ctx, seq)

    mx, my, mc = _my_place()
    chip = 2 * mx + my
    me = 4 * mx + 2 * my + mc

    c_rows = jnp.concatenate([c, jnp.zeros((7, d), F32)], axis=0)
    c_gath = _all_gather_small(c_rows, "gather_cond")[:, 0, :]
    c_all = jnp.concatenate([c_gath, c_ctx[None, :], jnp.zeros((7, d), F32)], axis=0)
    ncs = ada_w.shape[-1]
    ada_cols = _ada_fwd(c_all, ada_w, _chip_cols(ada_b, chip, ncs))
    small_shapes = [ada_cols.shape, norm_g.shape, pool_scale.shape, gm_ln_g.shape, gm_ln_b.shape]
    gathered = _all_gather_small(_pack([ada_cols, norm_g, pool_scale, gm_ln_g, gm_ln_b]), "gather_small_params")
    per_chip = _across_chips(gathered, gathered.shape[1:])
    ada_g, ng_g, ps_g, lng_g, lnb_g = _unpack(per_chip, small_shapes)

    def join_last(a):
        return jnp.moveaxis(a, 0, -2).reshape(*a.shape[1:-1], N_CHIPS * a.shape[-1])

    ada_full = join_last(ada_g)
    ng_full = join_last(ng_g)
    ps_full = join_last(ps_g)
    lng_full = join_last(lng_g)
    lnb_full = join_last(lnb_g)
    mod_lat = lax.dynamic_slice_in_dim(ada_full, me, 1, axis=1).reshape(DEPTH, 6, d)
    mod_ctx = ada_full[:, 8].reshape(DEPTH, 6, d)
    mods = jnp.stack([jnp.concatenate([mod_ctx, ng_full], axis=1), jnp.concatenate([mod_lat, ng_full], axis=1)],
                     axis=1)

    w1_f, w2_f, pw_f, wqkv_f, wo_f, win_f, wout_f = _gather_weights(
        [w.astype(BF16) for w in (mlp_w1, mlp_w2, pool_w, attn_w_qkv, attn_w_o, gm_w_in, gm_w_out)],
        [2, 1, 2, 2, 1, 2, 1])

    gains = jnp.concatenate([attn_q_g, attn_k_g, jnp.zeros((6, hd), F32)], axis=0)
    ws_b = gm_ws[0].astype(BF16)
    ws_t = jnp.swapaxes(gm_ws[0], 1, 2).astype(BF16)
    bs_col = gm_bs[0][:, :, None]
    cos, sin = _rope_tables(n_ctx, seq, hd)
    lat = lambda i: mods[i, 1:2]

    hc0 = jnp.concatenate([ctx[0], x[0]], axis=0)
    ha0 = _pool_fwd(hc0, mods[0], pw_f, ps_full, 0, nct=nct, tm=tm, seg_lens=seg_lens)
    hc1, u0, o0 = _mlp_fwd(ha0, mods[0], w1_f, w2_f, 0, nct=nct, tm=tm)
    xa1 = _normmod_call(hc1, mods[1], 0, "attn_in_fwd", nct=nct, tm=tm)
    qkv, q_r, k_r, v_b = _qkv_fwd(xa1, wqkv_f, cos, sin, gains, nh=nh, nkv=nkv, nct=nct, tm=tm)
    o_att, lse = _flash_fwd(q_r, k_r, v_b, n_ctx=n_ctx, hd=hd)
    ha1, y1 = _proj_fwd(o_att, wo_f, hc1, mods[1], n_ctx=n_ctx, tm=tm)
    h2, u1, o1 = _mlp_fwd(ha1, lat(1), w1_f, w2_f, 1, nct=0, tm=tm)
    ha2, zpre, y2 = _gmlp_fwd(h2, mods[2], win_f, lng_full, lnb_full, ws_b, bs_col, wout_f, tm=tm)
    h3, u2, o2 = _mlp_fwd(ha2, lat(2), w1_f, w2_f, 2, nct=0, tm=tm)
    ha3 = _pool_fwd(h3, lat(3), pw_f, ps_full, 3, nct=0, tm=tm, seg_lens=seg_lens)
    h4, u3, o3 = _mlp_fwd(ha3, lat(3), w1_f, w2_f, 3, nct=0, tm=tm)
    dh4, fin_acc = _final_loss(h4, loss_target[0], final_g[None, :], tm=tm)

    dw1, dw2 = [None] * DEPTH, [None] * DEPTH
    dmods = [None] * DEPTH

    def mlp_back(i, h_in, dh_out, u, o, md, n_ct):
        dh_in, du, dob, mb, dmd = _mlp_bwd(h_in, dh_out, u, o, md, w1_f, w2_f, i, nct=n_ct, tm=tm)
        dw1[i] = _mm_tn(mb, du, f"mlp_dw1_{i}")
        dw2[i] = _mm_tn(u, dob, f"mlp_dw2_{i}", relu2=True)
        return dh_in, dmd

    def pool_back(i, h_in, dh_out, md, n_ct):
        dp, dmd_a, dps, dpw = _pool_bwd_weights(h_in, dh_out, md, pw_f, ps_full, i, nct=n_ct, tm=tm,
                                                seg_lens=seg_lens)
        dh_in, dmd_b = _pool_bwd_input(dp, h_in, dh_out, md, i, nct=n_ct, tm=tm, seg_lens=seg_lens, gw=pgw)
        return dh_in, dmd_a + dmd_b, dps, dpw

    zero_grp = jnp.zeros((1, 8, d), F32)
    dha3, dmd3 = mlp_back(3, ha3, dh4, u3, o3, lat(3), 0)
    dh3, dmd3p, dps3, dpw3 = pool_back(3, h3, dha3, lat(3), 0)
    dmods[3] = jnp.concatenate([zero_grp, dmd3 + dmd3p], axis=0)
    dha2, dmd2 = mlp_back(2, ha2, dh3, u2, o2, lat(2), 0)
    dh2, dzpre, gated, dyb2, ab2, dmd2g, dln, dws, dbs = _gmlp_bwd(
        h2, dha2, zpre, y2, mods[2], win_f, lng_full, lnb_full, ws_b, ws_t, bs_col, wout_f, tm=tm)
    dwin = _mm_tn(ab2, dzpre, "gmlp_dw_in")
    dwout = _mm_tn(gated, dyb2, "gmlp_dw_out")
    dmods[2] = jnp.concatenate([zero_grp, dmd2 + dmd2g], axis=0)
    dha1, dmd1 = mlp_back(1, ha1, dh2, u1, o1, lat(1), 0)
    do_att, dyb1, dmd1p = _proj_bwd(dha1, y1, mods[1], wo_f, tm=tm)
    dwo = _mm_tn(o_att, dyb1, "attn_dw_o")
    dq, dk, dv = _flash_bwd(q_r, k_r, v_b, o_att, do_att, lse, n_ctx=n_ctx, hd=hd)
    dqkv, dgains = _qkv_bwd(qkv, dq, dk, dv, cos, sin, gains, nh=nh, nkv=nkv, nct=nct, tm=tm)
    dwqkv = _mm_tn(xa1, dqkv, "attn_dw_qkv")
    dhc1, dmd1i = _attn_in_bwd(dqkv, wqkv_f, hc1, dha1, mods[1], nct=nct, tm=tm)
    dmods[1] = dmd1i + jnp.concatenate([zero_grp, dmd1 + dmd1p], axis=0)
    dha0, dmd0 = mlp_back(0, ha0, dhc1, u0, o0, mods[0], nct)
    dhc0, dmd0p, dps0, dpw0 = pool_back(0, hc0, dha0, mods[0], nct)
    dmods[0] = dmd0 + dmd0p
    grad_x = dhc0[n_ctx:][None]

    groups = [(dw1, 1), (dw2, 0), ([dpw0.astype(BF16), dpw3.astype(BF16)], 1), ([dwqkv], 1), ([dwo], 0),
              ([dwin], 1), ([dwout], 0)]
    scattered = _scatter_grads(groups)
    big = [(mlp_w1, m_mlp_w1, v_mlp_w1), (mlp_w2, m_mlp_w2, v_mlp_w2), (pool_w, m_pool_w, v_pool_w),
           (attn_w_qkv, m_attn_w_qkv, v_attn_w_qkv), (attn_w_o, m_attn_w_o, v_attn_w_o),
           (gm_w_in, m_gm_w_in, v_gm_w_in), (gm_w_out, m_gm_w_out, v_gm_w_out)]
    names = ["mlp_w1", "mlp_w2", "pool_w", "attn_w_qkv", "attn_w_o", "gm_w_in", "gm_w_out"]
    partial = []
    for gi, (w, _, _) in enumerate(big):
        cols = w.shape[-1]
        own, landed = scattered[2 * gi], scattered[2 * gi + 1]
        partial.append(_sum_partials(own.reshape(-1, cols), landed.reshape(3, -1, cols), f"sum_chips_{names[gi]}"))
    from_sibling = _swap_with_sibling(partial)
    big_out = {}
    for gi, (w, m, v) in enumerate(big):
        cols = w.shape[-1]
        res = _adamw(partial[gi], from_sibling[gi], w.reshape(-1, cols), m.reshape(-1, cols), v.reshape(-1, cols),
                     f"adamw_{names[gi]}")
        big_out[names[gi]] = [r.reshape(w.shape) for r in res]

    dmods_all = jnp.stack(dmods, axis=0)
    small_grads = [dmods_all, dws, dbs, dgains, dln, dps0, dps3, fin_acc]
    sg_shapes = [a.shape for a in small_grads]
    sg_gath = _all_gather_small(_pack(small_grads), "gather_small_grads")
    sg_sum = _sum_devices(sg_gath, "sum_small_grads")
    s_dmods, s_dws, s_dbs, s_dgains, s_dln, s_dps0, s_dps3, s_fin = _unpack(sg_sum, sg_shapes)
    loss = s_fin[1, 0]

    dm_dev = _unpack(sg_gath, sg_shapes[:1])[0]
    dm_lat = jnp.moveaxis(dm_dev[:, :, 1, :6, :], 0, 1).reshape(DEPTH, N_DEV, 6 * d)
    dm_ctx = jnp.moveaxis(dm_dev[:, :, 0, :6, :], 0, 1).reshape(DEPTH, N_DEV, 6 * d)
    dmod16 = _chip_cols(jnp.concatenate([dm_lat, dm_ctx], axis=1), chip, ncs)
    g_ada_w, dcc_part = _ada_bwd(c_all, c_all.T, dmod16, ada_w)
    dcc_gath = _all_gather_small(dcc_part, "gather_d_c_ctx")
    dcc_chips = _across_chips(dcc_gath, dcc_gath.shape[1:])
    dcc = _sum_devices(dcc_chips, "sum_d_c_ctx")[0]
    ada_res = _adamw(g_ada_w.reshape(-1, ncs), jnp.zeros((DEPTH * d, ncs), F32), ada_w.reshape(-1, ncs),
                     m_ada_w.reshape(-1, ncs), v_ada_w.reshape(-1, ncs), "adamw_ada_w")
    big_out["ada_w"] = [r.reshape(ada_w.shape) for r in ada_res]

    def cols_of(a, width):
        return _chip_cols(a, chip, width)

    zero = lambda a: jnp.zeros(a.shape, F32)
    ngw = norm_g.shape[-1]
    small = {
        "c_ctx": (dcc, zero(dcc), c_ctx, m_c_ctx, v_c_ctx),
        "ada_b": (s_dmods[:, 0, :6].reshape(DEPTH, 6 * d), s_dmods[:, 1, :6].reshape(DEPTH, 6 * d), ada_b, m_ada_b,
                  v_ada_b),
        "norm_g": (cols_of(s_dmods[:, 0, 6:8], ngw), cols_of(s_dmods[:, 1, 6:8], ngw), norm_g, m_norm_g, v_norm_g),
        "pool_scale": (cols_of(jnp.stack([s_dps0[0], s_dps3[0]]), pool_scale.shape[-1]), zero(pool_scale),
                       pool_scale, m_pool_scale, v_pool_scale),
        "attn_q_g": (s_dgains[0:1], zero(attn_q_g), attn_q_g, m_attn_q_g, v_attn_q_g),
        "attn_k_g": (s_dgains[1:2], zero(attn_k_g), attn_k_g, m_attn_k_g, v_attn_k_g),
        "gm_ln_g": (cols_of(s_dln[0:1], gm_ln_g.shape[-1]), zero(gm_ln_g), gm_ln_g, m_gm_ln_g, v_gm_ln_g),
        "gm_ln_b": (cols_of(s_dln[1:2], gm_ln_b.shape[-1]), zero(gm_ln_b), gm_ln_b, m_gm_ln_b, v_gm_ln_b),
        "gm_ws": (s_dws[None], zero(gm_ws), gm_ws, m_gm_ws, v_gm_ws),
        "gm_bs": (s_dbs[None, :, :, 0], zero(gm_bs), gm_bs, m_gm_bs, v_gm_bs),
        "final_g": (s_fin[0], zero(final_g), final_g, m_final_g, v_final_g),
    }
    keys = list(small)
    packed = [_pack([small[k][t] for k in keys]) for t in range(5)]
    res = _adamw(*packed, "adamw_small")
    shapes = [small[k][2].shape for k in keys]
    small_out = {k: [] for k in keys}
    for r in res:
        for k, a in zip(keys, _unpack(r, shapes)):
            small_out[k].append(a)

    order = ["c_ctx", "ada_w", "ada_b", "norm_g", "mlp_w1", "mlp_w2", "pool_w", "pool_scale", "attn_w_qkv",
             "attn_w_o", "attn_q_g", "attn_k_g", "gm_w_in", "gm_ln_g", "gm_ln_b", "gm_ws", "gm_bs", "gm_w_out",
             "final_g"]
    allo = {**big_out, **small_out}
    outs = [loss, grad_x]
    for t in range(4):
        outs += [allo[k][t] for k in order]
    return tuple(outs)
```

```python
import functools
import math

import jax
import jax.numpy as jnp
from jax import lax
from jax.experimental import pallas as pl
from jax.experimental.pallas import tpu as pltpu

F32 = jnp.float32
BF16 = jnp.bfloat16
MESH = pl.DeviceIdType.MESH

EPS = 1e-6
GRID_W = 64
ROPE_BASE = 10000.0
POOL_WINDOWS = (2, 4, 8, 16)
HALO = 8
DEPTH = 4
N_MIXERS = 3

ADAM_LR = 0.001
ADAM_B1 = 0.9
ADAM_B2 = 0.999
ADAM_EPS = 1e-08
ADAM_WD = 0.01
ADAM_STEP = 10

VMEM_LIMIT_BYTES = 56 * 1024 * 1024
LANES = 128
SUBLANES = 8
N_DEV = 8
N_CHIPS = 4

SH1, SC1, G1, SH2, SC2, G2, NG0, NG1 = range(8)


def _dot(a, b):
    return jnp.dot(a, b, preferred_element_type=F32)


def _dot_nt(a, b):
    return lax.dot_general(a, b, (((1,), (1,)), ((), ())), preferred_element_type=F32)


def _dot_tn(a, b):
    return lax.dot_general(a, b, (((0,), (0,)), ((), ())), preferred_element_type=F32)


def _params(**kw):
    return pltpu.CompilerParams(vmem_limit_bytes=VMEM_LIMIT_BYTES, **kw)


def _full(shape):
    nd = len(shape)
    return pl.BlockSpec(shape, lambda *_: (0,) * nd)


def _rows(tm, width):
    return pl.BlockSpec((tm, width), lambda i: (i, 0))


def _group_of(nct, groups):
    if groups == 1:
        return lambda i: 0
    return lambda i: jnp.where(i >= nct, 1, 0)


def _mods_spec(nct, groups, d):
    grp = _group_of(nct, groups)
    return pl.BlockSpec((None, 8, d), lambda i: (grp(i), 0, 0))


def _first_of_group(i, nct, groups):
    if groups == 1:
        return i == 0
    return jnp.logical_or(i == 0, i == nct)


def _rowsum(v):
    return jnp.sum(v, axis=0, keepdims=True)


def _rms_parts(x):
    r = lax.rsqrt(jnp.mean(x * x, axis=-1, keepdims=True) + EPS)
    return x * r, r


def _normmod(x, md, which):
    ng, sh, sc = (md[NG0:NG0 + 1], md[SH1:SH1 + 1], md[SC1:SC1 + 1]) if which == 0 else (
        md[NG1:NG1 + 1], md[SH2:SH2 + 1], md[SC2:SC2 + 1])
    xhat, r = _rms_parts(x)
    n = xhat * ng
    return n * (1.0 + sc) + sh, (xhat, r, n)


def _normmod_bwd(da, parts, md, which):
    xhat, r, n = parts
    ng, sc = (md[NG0:NG0 + 1], md[SC1:SC1 + 1]) if which == 0 else (md[NG1:NG1 + 1], md[SC2:SC2 + 1])
    dsh = _rowsum(da)
    dsc = _rowsum(da * n)
    dn = da * (1.0 + sc)
    dng = _rowsum(dn * xhat)
    dxhat = dn * ng
    dx = r * (dxhat - xhat * jnp.mean(dxhat * xhat, axis=-1, keepdims=True))
    return dx, dsh, dsc, dng


def _acc_rows(ref, first, rows):
    @pl.when(first)
    def _():
        ref[...] = jnp.zeros(ref.shape, ref.dtype)

    for r, v in rows.items():
        ref[r:r + 1, :] += v


def _shift_up(x, k):
    if k == 0:
        return x
    return pltpu.roll(x, x.shape[0] - k, axis=0)


def _gelu(x):
    k = math.sqrt(2.0 / math.pi)
    return 0.5 * x * (1.0 + jnp.tanh(k * (x + 0.044715 * x * x * x)))


def _gelu_grad(x):
    k = math.sqrt(2.0 / math.pi)
    t = jnp.tanh(k * (x + 0.044715 * x * x * x))
    return 0.5 * (1.0 + t) + 0.5 * x * (1.0 - t * t) * k * (1.0 + 3.0 * 0.044715 * x * x)


def _silu(x):
    return x / (1.0 + jnp.exp(-x))


def _silu_grad(x):
    s = 1.0 / (1.0 + jnp.exp(-x))
    return s * (1.0 + x * (1.0 - s))


def _ff_chunk(ff):
    return min(ff, 1024)


def _mlp_fwd(h, mods, w1, w2, layer, *, nct, tm):
    rows, d = h.shape
    groups = mods.shape[0]
    ff = w1.shape[-1]
    fc = _ff_chunk(ff)

    def body(h_ref, md_ref, w1_ref, w2_ref, h2_ref, u_ref, o_ref):
        x = h_ref[...]
        md = md_ref[...]
        m, _ = _normmod(x, md, 1)
        mb = m.astype(BF16)
        acc = jnp.zeros((tm, d), F32)
        for k in range(ff // fc):
            u = _dot(mb, w1_ref[:, k * fc:(k + 1) * fc])
            u_ref[:, k * fc:(k + 1) * fc] = u.astype(BF16)
            acc = acc + _dot(jnp.square(jnp.maximum(u, 0.0)).astype(BF16), w2_ref[k * fc:(k + 1) * fc, :])
        o_ref[...] = acc.astype(BF16)
        h2_ref[...] = x + md[G2:G2 + 1] * acc

    return pl.pallas_call(
        body, name=f"mlp_fwd_{layer}", grid=(rows // tm,),
        in_specs=[_rows(tm, d), _mods_spec(nct, groups, d),
                  pl.BlockSpec((None, d, ff), lambda i: (layer, 0, 0)),
                  pl.BlockSpec((None, ff, d), lambda i: (layer, 0, 0))],
        out_specs=[_rows(tm, d), _rows(tm, ff), _rows(tm, d)],
        out_shape=[jax.ShapeDtypeStruct((rows, d), F32), jax.ShapeDtypeStruct((rows, ff), BF16),
                   jax.ShapeDtypeStruct((rows, d), BF16)],
        compiler_params=_params(),
    )(h, mods, w1, w2)


def _mlp_bwd(h1, dh2, u, o, mods, w1, w2, layer, *, nct, tm):
    rows, d = h1.shape
    groups = mods.shape[0]
    ff = w1.shape[-1]
    fc = _ff_chunk(ff)

    def body(h_ref, g_ref, u_ref, o_ref, md_ref, w1_ref, w2_ref, dh_ref, du_ref, dob_ref, mb_ref, dmd_ref):
        i = pl.program_id(0)
        x = h_ref[...]
        g = g_ref[...]
        md = md_ref[...]
        m, parts = _normmod(x, md, 1)
        mb_ref[...] = m.astype(BF16)
        dg2 = _rowsum(g * o_ref[...].astype(F32))
        dob = (g * md[G2:G2 + 1]).astype(BF16)
        dob_ref[...] = dob
        dm = jnp.zeros((tm, d), F32)
        for k in range(ff // fc):
            uk = u_ref[:, k * fc:(k + 1) * fc].astype(F32)
            dr = _dot_nt(dob, w2_ref[k * fc:(k + 1) * fc, :])
            duk = (dr * (2.0 * jnp.maximum(uk, 0.0))).astype(BF16)
            du_ref[:, k * fc:(k + 1) * fc] = duk
            dm = dm + _dot_nt(duk, w1_ref[:, k * fc:(k + 1) * fc])
        dx, dsh, dsc, dng = _normmod_bwd(dm, parts, md, 1)
        dh_ref[...] = g + dx
        _acc_rows(dmd_ref, _first_of_group(i, nct, groups), {SH2: dsh, SC2: dsc, G2: dg2, NG1: dng})

    return pl.pallas_call(
        body, name=f"mlp_bwd_{layer}", grid=(rows // tm,),
        in_specs=[_rows(tm, d), _rows(tm, d), _rows(tm, ff), _rows(tm, d), _mods_spec(nct, groups, d),
                  pl.BlockSpec((None, d, ff), lambda i: (layer, 0, 0)),
                  pl.BlockSpec((None, ff, d), lambda i: (layer, 0, 0))],
        out_specs=[_rows(tm, d), _rows(tm, ff), _rows(tm, d), _rows(tm, d), _mods_spec(nct, groups, d)],
        out_shape=[jax.ShapeDtypeStruct((rows, d), F32), jax.ShapeDtypeStruct((rows, ff), BF16),
                   jax.ShapeDtypeStruct((rows, d), BF16), jax.ShapeDtypeStruct((rows, d), BF16),
                   jax.ShapeDtypeStruct((groups, 8, d), F32)],
        compiler_params=_params(),
    )(h1, dh2, u, o, mods, w1, w2)


def _div_tile(n, cap):
    if n <= cap:
        return n
    return max(t for t in range(LANES, cap + 1, LANES) if n % t == 0)


def _mm_tn(a, b, name, *, relu2=False):
    rows, m = a.shape
    n = b.shape[1]
    tmm = min(m, 1024)
    tn = min(n, 2048)
    tr = _div_tile(rows, 768)

    def body(a_ref, b_ref, o_ref, acc_ref):
        r = pl.program_id(2)

        @pl.when(r == 0)
        def _():
            acc_ref[...] = jnp.zeros(acc_ref.shape, F32)

        av = a_ref[...]
        if relu2:
            av = jnp.square(jnp.maximum(av.astype(F32), 0.0)).astype(BF16)
        acc_ref[...] += _dot_tn(av, b_ref[...])

        @pl.when(r == pl.num_programs(2) - 1)
        def _():
            o_ref[...] = acc_ref[...].astype(BF16)

    return pl.pallas_call(
        body, name=name, grid=(m // tmm, n // tn, rows // tr),
        in_specs=[pl.BlockSpec((tr, tmm), lambda i, j, r: (r, i)), pl.BlockSpec((tr, tn), lambda i, j, r: (r, j))],
        out_specs=pl.BlockSpec((tmm, tn), lambda i, j, r: (i, j)),
        out_shape=jax.ShapeDtypeStruct((m, n), BF16),
        scratch_shapes=[pltpu.VMEM((tmm, tn), F32)],
        compiler_params=_params(),
    )(a, b)


def _halo_specs(tm, d, rows):
    per = tm // HALO
    prev = pl.BlockSpec((HALO, d), lambda i: (jnp.maximum(i * per - 1, 0), 0))
    nxt = pl.BlockSpec((HALO, d), lambda i: (jnp.minimum((i + 1) * per, rows // HALO - 1), 0))
    return prev, _rows(tm, d), nxt


def _segment_positions(i, tm, nct, groups, seg_lens):
    if groups == 1:
        start, length = 0, seg_lens[-1]
    else:
        start = jnp.where(i >= nct, nct, 0)
        length = jnp.where(i >= nct, seg_lens[1], seg_lens[0])
    rid = lax.broadcasted_iota(jnp.int32, (tm + 2 * HALO, 1), 0)
    pos = (i - start) * tm - HALO + rid
    return pos, length


def _window_count(pos, length, w):
    hi = jnp.minimum(pos + (w - w // 2), length)
    lo = jnp.maximum(pos - w // 2, 0)
    return (hi - lo).astype(F32)


def _window_sum(xg, w, lead):
    b, k = xg, 1
    while k < w:
        b = b + _shift_up(b, k)
        k *= 2
    return _shift_up(b, HALO - lead)[0:xg.shape[0] - 2 * HALO]


def _pooled(ext, md, pos, length, gw):
    tm = ext.shape[0] - 2 * HALO
    a_ext, parts = _normmod(ext, md, 0)
    valid = jnp.logical_and(pos >= 0, pos < length)
    a_ext = jnp.where(valid, a_ext, 0.0)
    pos_c = pos[HALO:HALO + tm]
    ps = []
    for g, w in enumerate(POOL_WINDOWS):
        xg = a_ext[:, g * gw:(g + 1) * gw]
        s = _window_sum(xg, w, w // 2)
        ps.append(s / _window_count(pos_c, length, w) - xg[HALO:HALO + tm])
    return ps, parts


def _pool_fwd(h, mods, pw, pscale, layer, *, nct, tm, seg_lens):
    rows, d = h.shape
    groups = mods.shape[0]
    pg, gw = pw.shape[1], pw.shape[-1]

    def body(prev_ref, cur_ref, next_ref, md_ref, pw_ref, ps_ref, out_ref):
        i = pl.program_id(0)
        md = md_ref[...]
        cur = cur_ref[...]
        ext = jnp.concatenate([prev_ref[...], cur, next_ref[...]], axis=0)
        pos, length = _segment_positions(i, tm, nct, groups, seg_lens)
        ps, _ = _pooled(ext, md, pos, length, gw)
        for g in range(pg):
            yg = _dot(ps[g].astype(BF16), pw_ref[g]) * ps_ref[:, g * gw:(g + 1) * gw]
            out_ref[:, g * gw:(g + 1) * gw] = cur[:, g * gw:(g + 1) * gw] + md[G1:G1 + 1, g * gw:(g + 1) * gw] * yg

    j = layer // N_MIXERS
    return pl.pallas_call(
        body, name=f"pool_fwd_{layer}", grid=(rows // tm,),
        in_specs=[*_halo_specs(tm, d, rows), _mods_spec(nct, groups, d),
                  pl.BlockSpec((None, pg, gw, gw), lambda i: (j, 0, 0, 0)), _full((1, d))],
        out_specs=_rows(tm, d),
        out_shape=jax.ShapeDtypeStruct((rows, d), F32),
        compiler_params=_params(),
    )(h, h, h, mods, pw, pscale[j:j + 1])


def _pool_bwd_weights(h, dh1, mods, pw, pscale, layer, *, nct, tm, seg_lens):
    rows, d = h.shape
    groups = mods.shape[0]
    pg, gw = pw.shape[1], pw.shape[-1]

    def body(prev_ref, cur_ref, next_ref, g_ref, md_ref, pw_ref, ps_ref, dp_ref, dmd_ref, dps_ref, dpw_ref):
        i = pl.program_id(0)
        md = md_ref[...]
        ext = jnp.concatenate([prev_ref[...], cur_ref[...], next_ref[...]], axis=0)
        pos, length = _segment_positions(i, tm, nct, groups, seg_lens)
        ps, _ = _pooled(ext, md, pos, length, gw)
        gup = g_ref[...]

        @pl.when(i == 0)
        def _():
            dps_ref[...] = jnp.zeros(dps_ref.shape, F32)
            dpw_ref[...] = jnp.zeros(dpw_ref.shape, F32)

        dg1 = []
        for g in range(pg):
            cols = slice(g * gw, (g + 1) * gw)
            pb = ps[g].astype(BF16)
            yp = _dot(pb, pw_ref[g])
            sc = ps_ref[:, cols]
            dg1.append(_rowsum(gup[:, cols] * (yp * sc)))
            dy = gup[:, cols] * md[G1:G1 + 1, cols]
            dps_ref[0:1, cols] += _rowsum(dy * yp)
            dyp = (dy * sc).astype(BF16)
            dp_ref[:, cols] = _dot_nt(dyp, pw_ref[g])
            dpw_ref[g] += _dot_tn(pb, dyp)
        _acc_rows(dmd_ref, _first_of_group(i, nct, groups), {G1: jnp.concatenate(dg1, axis=1)})

    j = layer // N_MIXERS
    return pl.pallas_call(
        body, name=f"pool_bwd_w_{layer}", grid=(rows // tm,),
        in_specs=[*_halo_specs(tm, d, rows), _rows(tm, d), _mods_spec(nct, groups, d),
                  pl.BlockSpec((None, pg, gw, gw), lambda i: (j, 0, 0, 0)), _full((1, d))],
        out_specs=[_rows(tm, d), _mods_spec(nct, groups, d), _full((8, d)), _full((pg, gw, gw))],
        out_shape=[jax.ShapeDtypeStruct((rows, d), F32), jax.ShapeDtypeStruct((groups, 8, d), F32),
                   jax.ShapeDtypeStruct((8, d), F32), jax.ShapeDtypeStruct((pg, gw, gw), F32)],
        compiler_params=_params(),
    )(h, h, h, dh1, mods, pw, pscale[j:j + 1])


def _pool_bwd_input(dp, h, dh1, mods, layer, *, nct, tm, seg_lens, gw):
    rows, d = h.shape
    groups = mods.shape[0]

    def body(prev_ref, cur_ref, next_ref, h_ref, g_ref, md_ref, dh_ref, dmd_ref):
        i = pl.program_id(0)
        md = md_ref[...]
        dp_cur = cur_ref[...]
        ext = jnp.concatenate([prev_ref[...], dp_cur, next_ref[...]], axis=0)
        pos, length = _segment_positions(i, tm, nct, groups, seg_lens)
        valid = jnp.logical_and(pos >= 0, pos < length)
        das = []
        for g, w in enumerate(POOL_WINDOWS):
            cols = slice(g * gw, (g + 1) * gw)
            q = jnp.where(valid, ext[:, cols] / jnp.maximum(_window_count(pos, length, w), 1.0), 0.0)
            das.append(_window_sum(q, w, w // 2 - 1) - dp_cur[:, cols])
        da = jnp.concatenate(das, axis=1)
        _, parts = _normmod(h_ref[...], md, 0)
        dx, dsh, dsc, dng = _normmod_bwd(da, parts, md, 0)
        dh_ref[...] = g_ref[...] + dx
        _acc_rows(dmd_ref, _first_of_group(i, nct, groups), {SH1: dsh, SC1: dsc, NG0: dng})

    return pl.pallas_call(
        body, name=f"pool_bwd_x_{layer}", grid=(rows // tm,),
        in_specs=[*_halo_specs(tm, d, rows), _rows(tm, d), _rows(tm, d), _mods_spec(nct, groups, d)],
        out_specs=[_rows(tm, d), _mods_spec(nct, groups, d)],
        out_shape=[jax.ShapeDtypeStruct((rows, d), F32), jax.ShapeDtypeStruct((groups, 8, d), F32)],
        compiler_params=_params(),
    )(dp, dp, dp, h, dh1, mods)


def _rope_tables(n_ctx, seq, hd):
    half = hd // 2
    t = jnp.arange(seq)
    row = (t // GRID_W).astype(F32)
    col = (t % GRID_W).astype(F32)
    inv = ROPE_BASE ** (-jnp.arange(0, half, 2, dtype=F32) / half)
    ar = row[:, None] * inv[None, :]
    ac = col[:, None] * inv[None, :]
    cos = jnp.concatenate([jnp.cos(ar), jnp.cos(ar), jnp.cos(ac), jnp.cos(ac)], axis=1)
    sin = jnp.concatenate([-jnp.sin(ar), jnp.sin(ar), -jnp.sin(ac), jnp.sin(ac)], axis=1)
    cos = jnp.concatenate([jnp.ones((n_ctx, hd), F32), cos], axis=0)
    sin = jnp.concatenate([jnp.zeros((n_ctx, hd), F32), sin], axis=0)
    return cos, sin


def _rope_partner(x):
    hd = x.shape[-1]
    q = hd // 4
    lane = lax.broadcasted_iota(jnp.int32, x.shape, 1)
    first = (lane % (2 * q)) < q
    return jnp.where(first, pltpu.roll(x, hd - q, axis=1), pltpu.roll(x, q, axis=1))


def _normmod_call(h, mods, which, name, *, nct, tm):
    rows, d = h.shape
    groups = mods.shape[0]

    def body(h_ref, md_ref, a_ref):
        a, _ = _normmod(h_ref[...], md_ref[...], which)
        a_ref[...] = a.astype(BF16)

    return pl.pallas_call(
        body, name=name, grid=(rows // tm,),
        in_specs=[_rows(tm, d), _mods_spec(nct, groups, d)],
        out_specs=_rows(tm, d), out_shape=jax.ShapeDtypeStruct((rows, d), BF16),
        compiler_params=_params(),
    )(h, mods)


def _qkv_fwd(xa, wqkv, cos, sin, gains, *, nh, nkv, nct, tm):
    rows, d = xa.shape
    qw = wqkv.shape[-1]
    hd = cos.shape[-1]

    def body(x_ref, w_ref, cos_ref, sin_ref, gn_ref, qkv_ref, q_ref, k_ref, v_ref):
        qkv = _dot(x_ref[...], w_ref[...])
        qkv_ref[...] = qkv
        c, s = cos_ref[...], sin_ref[...]
        for hh in range(nh + nkv):
            xh = qkv[:, hh * hd:(hh + 1) * hd]
            xhat, _ = _rms_parts(xh)
            y = xhat * (gn_ref[0:1, :] if hh < nh else gn_ref[1:2, :])
            rot = (y * c + _rope_partner(y) * s).astype(BF16)
            if hh < nh:
                q_ref[:, hh * hd:(hh + 1) * hd] = rot
            else:
                k_ref[:, (hh - nh) * hd:(hh - nh + 1) * hd] = rot
        v_ref[...] = qkv[:, (nh + nkv) * hd:].astype(BF16)

    return pl.pallas_call(
        body, name="attn_qkv_fwd", grid=(rows // tm,),
        in_specs=[_rows(tm, d), pl.BlockSpec((None, d, qw), lambda i: (0, 0, 0)), _rows(tm, hd), _rows(tm, hd),
                  _full((8, hd))],
        out_specs=[_rows(tm, qw), pl.BlockSpec((tm, nh * hd), lambda i: (jnp.maximum(i - nct, 0), 0)),
                   _rows(tm, nkv * hd), _rows(tm, nkv * hd)],
        out_shape=[jax.ShapeDtypeStruct((rows, qw), F32), jax.ShapeDtypeStruct((rows - nct * tm, nh * hd), BF16),
                   jax.ShapeDtypeStruct((rows, nkv * hd), BF16), jax.ShapeDtypeStruct((rows, nkv * hd), BF16)],
        compiler_params=_params(),
    )(xa, wqkv, cos, sin, gains)


ATTN_Q_TILE_CAP = 1024
ATTN_KV_TILE_CAP = 4224
ATTN_ROW_GROUP = 256
LOG2E = 1.4426950408889634


def _attn_tiles(seq, total):
    tq = _div_tile(seq, ATTN_Q_TILE_CAP)
    return tq, _div_tile(total, ATTN_KV_TILE_CAP), min(ATTN_ROW_GROUP, tq)


def _flash_fwd(q, k, v, *, n_ctx, hd):
    total = k.shape[0]
    seq = total - n_ctx
    nkv = k.shape[1] // hd
    tq, tk, rg = _attn_tiles(seq, total)
    nk = total // tk
    scale = hd ** -0.5
    c2 = scale * LOG2E

    def body(q_ref, k_ref, v_ref, o_ref, lse_ref, m_sc, l_sc, acc_sc):
        ki = pl.program_id(2)

        @pl.when(ki == 0)
        def _():
            m_sc[...] = jnp.full(m_sc.shape, -jnp.inf, F32)
            l_sc[...] = jnp.zeros(l_sc.shape, F32)
            acc_sc[...] = jnp.zeros(acc_sc.shape, F32)

        kk, vv = k_ref[...], v_ref[...]
        for g in range(2):
            for sub in range(tq // rg):
                rows = slice(g * tq + sub * rg, g * tq + (sub + 1) * rg)
                s = _dot_nt(q_ref[sub * rg:(sub + 1) * rg, g * hd:(g + 1) * hd], kk)
                m_old = m_sc[rows]
                m_new = jnp.maximum(m_old, jnp.max(s, axis=-1, keepdims=True))
                alpha = jnp.exp2((m_old - m_new) * c2)
                p = jnp.exp2((s - m_new) * c2)
                l_sc[rows] = alpha * l_sc[rows] + jnp.sum(p, axis=-1, keepdims=True)
                acc_sc[rows] = alpha * acc_sc[rows] + _dot(p.astype(BF16), vv)
                m_sc[rows] = m_new

        @pl.when(ki == nk - 1)
        def _():
            o2 = acc_sc[...] / l_sc[...]
            lse = m_sc[...] * scale + jnp.log(l_sc[...])
            o_ref[:, :hd] = o2[:tq].astype(BF16)
            o_ref[:, hd:] = o2[tq:].astype(BF16)
            lse_ref[:, 0:1] = lse[:tq]
            lse_ref[:, 1:2] = lse[tq:]

    return pl.pallas_call(
        body, name="attn_flash_fwd", grid=(nkv, seq // tq, nk),
        in_specs=[pl.BlockSpec((tq, 2 * hd), lambda h, i, j: (i, h)),
                  pl.BlockSpec((tk, hd), lambda h, i, j: (j, h)),
                  pl.BlockSpec((tk, hd), lambda h, i, j: (j, h))],
        out_specs=[pl.BlockSpec((tq, 2 * hd), lambda h, i, j: (i, h)),
                   pl.BlockSpec((None, tq, 2), lambda h, i, j: (h, i, 0))],
        out_shape=[jax.ShapeDtypeStruct((seq, 2 * nkv * hd), BF16), jax.ShapeDtypeStruct((nkv, seq, 2), F32)],
        scratch_shapes=[pltpu.VMEM((2 * tq, 1), F32), pltpu.VMEM((2 * tq, 1), F32), pltpu.VMEM((2 * tq, hd), F32)],
        compiler_params=_params(),
    )(q, k, v)


def _flash_bwd(q, k, v, o, do, lse, *, n_ctx, hd):
    total = k.shape[0]
    seq = total - n_ctx
    nkv = k.shape[1] // hd
    tq, tk, rg = _attn_tiles(seq, total)
    scale = hd ** -0.5
    c2 = scale * LOG2E

    def body(q_ref, k_ref, v_ref, o_ref, do_ref, lse_ref, dq_ref, dk_ref, dv_ref):
        ki, qi = pl.program_id(1), pl.program_id(2)
        kk, vv = k_ref[...], v_ref[...]

        @pl.when(qi == 0)
        def _():
            dk_ref[...] = jnp.zeros(dk_ref.shape, F32)
            dv_ref[...] = jnp.zeros(dv_ref.shape, F32)

        dk_acc = jnp.zeros((tk, hd), F32)
        dv_acc = jnp.zeros((tk, hd), F32)
        for g in range(2):
            for sub in range(tq // rg):
                rs = slice(sub * rg, (sub + 1) * rg)
                cs = slice(g * hd, (g + 1) * hd)
                qq = q_ref[rs, cs]
                dd = do_ref[rs, cs]
                delta = jnp.sum(dd.astype(F32) * o_ref[rs, cs].astype(F32), axis=-1, keepdims=True)
                p = jnp.exp2(_dot_nt(qq, kk) * c2 - lse_ref[rs, g:g + 1] * LOG2E)
                dp = _dot_nt(dd, vv)
                ds = (p * (dp - delta) * scale).astype(BF16)
                dv_acc = dv_acc + _dot_tn(p.astype(BF16), dd)
                dk_acc = dk_acc + _dot_tn(ds, qq)
                dq = _dot(ds, kk)
                rows = pl.ds(pl.multiple_of(qi * tq, tq) + sub * rg, rg)

                @pl.when(ki == 0)
                def _():
                    dq_ref[rows, cs] = dq

                @pl.when(ki > 0)
                def _():
                    dq_ref[rows, cs] += dq
        dk_ref[...] += dk_acc
        dv_ref[...] += dv_acc

    return pl.pallas_call(
        body, name="attn_flash_bwd", grid=(nkv, total // tk, seq // tq),
        in_specs=[pl.BlockSpec((tq, 2 * hd), lambda h, j, i: (i, h)),
                  pl.BlockSpec((tk, hd), lambda h, j, i: (j, h)),
                  pl.BlockSpec((tk, hd), lambda h, j, i: (j, h)),
                  pl.BlockSpec((tq, 2 * hd), lambda h, j, i: (i, h)),
                  pl.BlockSpec((tq, 2 * hd), lambda h, j, i: (i, h)),
                  pl.BlockSpec((None, tq, 2), lambda h, j, i: (h, i, 0))],
        out_specs=[pl.BlockSpec((seq, 2 * hd), lambda h, j, i: (0, h)),
                   pl.BlockSpec((tk, hd), lambda h, j, i: (j, h)),
                   pl.BlockSpec((tk, hd), lambda h, j, i: (j, h))],
        out_shape=[jax.ShapeDtypeStruct((seq, 2 * nkv * hd), F32), jax.ShapeDtypeStruct((total, nkv * hd), F32),
                   jax.ShapeDtypeStruct((total, nkv * hd), F32)],
        compiler_params=_params(),
    )(q, k, v, o, do, lse)


def _proj_fwd(o, wo, hc, mods, *, n_ctx, tm):
    seq, d = o.shape
    off = n_ctx // tm

    def body(o_ref, w_ref, h_ref, md_ref, h1_ref, y_ref):
        y = _dot(o_ref[...], w_ref[...])
        y_ref[...] = y.astype(BF16)
        h1_ref[...] = h_ref[...] + md_ref[G1:G1 + 1, :] * y

    return pl.pallas_call(
        body, name="attn_proj_fwd", grid=(seq // tm,),
        in_specs=[_rows(tm, d), pl.BlockSpec((None, d, d), lambda i: (0, 0, 0)),
                  pl.BlockSpec((tm, d), lambda i: (i + off, 0)), pl.BlockSpec((None, 8, d), lambda i: (1, 0, 0))],
        out_specs=[_rows(tm, d), _rows(tm, d)],
        out_shape=[jax.ShapeDtypeStruct((seq, d), F32), jax.ShapeDtypeStruct((seq, d), BF16)],
        compiler_params=_params(),
    )(o, wo, hc, mods)


def _proj_bwd(dh1, y, mods, wo, *, tm):
    seq, d = dh1.shape

    def body(g_ref, y_ref, md_ref, w_ref, do_ref, dyb_ref, dmd_ref):
        i = pl.program_id(0)
        g = g_ref[...]
        dyb = (g * md_ref[G1:G1 + 1, :]).astype(BF16)
        dyb_ref[...] = dyb
        do_ref[...] = _dot_nt(dyb, w_ref[...]).astype(BF16)
        _acc_rows(dmd_ref, i == 0, {G1: _rowsum(g * y_ref[...].astype(F32))})

    return pl.pallas_call(
        body, name="attn_proj_bwd", grid=(seq // tm,),
        in_specs=[_rows(tm, d), _rows(tm, d), pl.BlockSpec((None, 8, d), lambda i: (1, 0, 0)),
                  pl.BlockSpec((None, d, d), lambda i: (0, 0, 0))],
        out_specs=[_rows(tm, d), _rows(tm, d), pl.BlockSpec((None, 8, d), lambda i: (0, 0, 0))],
        out_shape=[jax.ShapeDtypeStruct((seq, d), BF16), jax.ShapeDtypeStruct((seq, d), BF16),
                   jax.ShapeDtypeStruct((1, 8, d), F32)],
        compiler_params=_params(),
    )(dh1, y, mods, wo)


def _qkv_bwd(qkv, dq, dk, dv, cos, sin, gains, *, nh, nkv, nct, tm):
    rows, qw = qkv.shape
    hd = cos.shape[-1]

    def body(qkv_ref, dq_ref, dk_ref, dv_ref, cos_ref, sin_ref, gn_ref, out_ref, dgn_ref):
        i = pl.program_id(0)
        c, s = cos_ref[...], sin_ref[...]
        is_lat = (i >= nct).astype(F32)
        dqg = jnp.zeros((1, hd), F32)
        dkg = jnp.zeros((1, hd), F32)
        for hh in range(nh + nkv):
            if hh < nh:
                dr = dq_ref[:, hh * hd:(hh + 1) * hd] * is_lat
                gn = gn_ref[0:1, :]
            else:
                dr = dk_ref[:, (hh - nh) * hd:(hh - nh + 1) * hd]
                gn = gn_ref[1:2, :]
            dy = dr * c + _rope_partner(dr * s)
            xhat, r = _rms_parts(qkv_ref[:, hh * hd:(hh + 1) * hd])
            dgh = _rowsum(dy * xhat)
            if hh < nh:
                dqg = dqg + dgh
            else:
                dkg = dkg + dgh
            dxhat = dy * gn
            dx = r * (dxhat - xhat * jnp.mean(dxhat * xhat, axis=-1, keepdims=True))
            out_ref[:, hh * hd:(hh + 1) * hd] = dx.astype(BF16)
        out_ref[:, (nh + nkv) * hd:] = dv_ref[...].astype(BF16)
        _acc_rows(dgn_ref, i == 0, {0: dqg, 1: dkg})

    return pl.pallas_call(
        body, name="attn_qkv_bwd", grid=(rows // tm,),
        in_specs=[_rows(tm, qw), pl.BlockSpec((tm, nh * hd), lambda i: (jnp.maximum(i - nct, 0), 0)),
                  _rows(tm, nkv * hd), _rows(tm, nkv * hd), _rows(tm, hd), _rows(tm, hd), _full((8, hd))],
        out_specs=[_rows(tm, qw), _full((8, hd))],
        out_shape=[jax.ShapeDtypeStruct((rows, qw), BF16), jax.ShapeDtypeStruct((8, hd), F32)],
        compiler_params=_params(),
    )(qkv, dq, dk, dv, cos, sin, gains)


def _attn_in_bwd(dqkv, wqkv, hc, dh1, mods, *, nct, tm):
    rows, d = hc.shape
    qw = dqkv.shape[1]

    def body(dz_ref, w_ref, h_ref, g_ref, md_ref, dh_ref, dmd_ref):
        i = pl.program_id(0)
        md = md_ref[...]
        da = _dot_nt(dz_ref[...], w_ref[...])
        _, parts = _normmod(h_ref[...], md, 0)
        dx, dsh, dsc, dng = _normmod_bwd(da, parts, md, 0)
        dh_ref[...] = g_ref[...] * (i >= nct).astype(F32) + dx
        _acc_rows(dmd_ref, _first_of_group(i, nct, 2), {SH1: dsh, SC1: dsc, NG0: dng})

    return pl.pallas_call(
        body, name="attn_in_bwd", grid=(rows // tm,),
        in_specs=[_rows(tm, qw), pl.BlockSpec((None, d, qw), lambda i: (0, 0, 0)), _rows(tm, d),
                  pl.BlockSpec((tm, d), lambda i: (jnp.maximum(i - nct, 0), 0)), _mods_spec(nct, 2, d)],
        out_specs=[_rows(tm, d), _mods_spec(nct, 2, d)],
        out_shape=[jax.ShapeDtypeStruct((rows, d), F32), jax.ShapeDtypeStruct((2, 8, d), F32)],
        compiler_params=_params(),
    )(dqkv, wqkv, hc, dh1, mods)


def _gmlp_gate(zp, lng, lnb, ws_ref, bs_ref, gg, ch):
    half = zp.shape[1] // 2
    ggw = half // gg
    z = _gelu(zp)
    u, v = z[:, :half], z[:, half:]
    vc = v - jnp.mean(v, axis=-1, keepdims=True)
    rs = lax.rsqrt(jnp.mean(vc * vc, axis=-1, keepdims=True) + EPS)
    vhat = vc * rs
    vln = (vhat * lng + lnb).astype(BF16)
    chunks = []
    for n in range(zp.shape[0] // ch):
        groups = []
        for g in range(gg):
            groups.append(_dot(ws_ref[g], vln[n * ch:(n + 1) * ch, g * ggw:(g + 1) * ggw]) + bs_ref[g])
        chunks.append(jnp.concatenate(groups, axis=1))
    sv = jnp.concatenate(chunks, axis=0) if len(chunks) > 1 else chunks[0]
    return u, sv, vhat, rs, vln


def _gmlp_fwd(h, mods, w_in, lng, lnb, ws, bs, w_out, *, tm):
    seq, d = h.shape
    zw = w_in.shape[-1]
    half = zw // 2
    gg, ch = ws.shape[0], ws.shape[-1]

    def body(h_ref, md_ref, win_ref, lng_ref, lnb_ref, ws_ref, bs_ref, wout_ref, h1_ref, zp_ref, y_ref):
        x = h_ref[...]
        md = md_ref[...]
        a, _ = _normmod(x, md, 0)
        zp = _dot(a.astype(BF16), win_ref[...])
        zp_ref[...] = zp.astype(BF16)
        u, sv, _, _, _ = _gmlp_gate(zp, lng_ref[...], lnb_ref[...], ws_ref, bs_ref, gg, ch)
        y = _dot((u * sv).astype(BF16), wout_ref[...])
        y_ref[...] = y.astype(BF16)
        h1_ref[...] = x + md[G1:G1 + 1] * y

    return pl.pallas_call(
        body, name="gmlp_fwd", grid=(seq // tm,),
        in_specs=[_rows(tm, d), pl.BlockSpec((None, 8, d), lambda i: (1, 0, 0)),
                  pl.BlockSpec((None, d, zw), lambda i: (0, 0, 0)), _full((1, half)), _full((1, half)),
                  _full((gg, ch, ch)), _full((gg, ch, 1)), pl.BlockSpec((None, half, d), lambda i: (0, 0, 0))],
        out_specs=[_rows(tm, d), _rows(tm, zw), _rows(tm, d)],
        out_shape=[jax.ShapeDtypeStruct((seq, d), F32), jax.ShapeDtypeStruct((seq, zw), BF16),
                   jax.ShapeDtypeStruct((seq, d), BF16)],
        compiler_params=_params(),
    )(h, mods, w_in, lng, lnb, ws, bs, w_out)


def _gmlp_bwd(h, dh1, zpre, y, mods, w_in, lng, lnb, ws, ws_t, bs, w_out, *, tm):
    seq, d = h.shape
    zw = w_in.shape[-1]
    half = zw // 2
    gg, ch = ws.shape[0], ws.shape[-1]
    ggw = half // gg

    def body(h_ref, g_ref, zp_ref, y_ref, md_ref, win_ref, lng_ref, lnb_ref, ws_ref, wst_ref, bs_ref, wout_ref,
             dh_ref, dzp_ref, gated_ref, dyb_ref, ab_ref, dmd_ref, dln_ref, dws_ref, dbs_ref):
        i = pl.program_id(0)
        x = h_ref[...]
        md = md_ref[...]
        a, parts = _normmod(x, md, 0)
        ab_ref[...] = a.astype(BF16)
        zp = zp_ref[...].astype(F32)
        lng_v = lng_ref[...]
        u, sv, vhat, rs, vln = _gmlp_gate(zp, lng_v, lnb_ref[...], ws_ref, bs_ref, gg, ch)
        g = g_ref[...]
        dg1 = _rowsum(g * y_ref[...].astype(F32))
        dyb = (g * md[G1:G1 + 1]).astype(BF16)
        dyb_ref[...] = dyb
        gated_ref[...] = (u * sv).astype(BF16)
        dgated = _dot_nt(dyb, wout_ref[...])
        du = dgated * sv
        dsv = dgated * u

        @pl.when(i == 0)
        def _():
            dws_ref[...] = jnp.zeros(dws_ref.shape, F32)
            dbs_ref[...] = jnp.zeros(dbs_ref.shape, F32)
            dln_ref[...] = jnp.zeros(dln_ref.shape, F32)

        chunks = []
        for n in range(tm // ch):
            groups = []
            for gi in range(gg):
                blk = dsv[n * ch:(n + 1) * ch, gi * ggw:(gi + 1) * ggw]
                dbs_ref[gi] += jnp.sum(blk, axis=-1, keepdims=True)
                blk_b = blk.astype(BF16)
                dws_ref[gi] += _dot_nt(blk_b, vln[n * ch:(n + 1) * ch, gi * ggw:(gi + 1) * ggw])
                groups.append(_dot(wst_ref[gi], blk_b))
            chunks.append(jnp.concatenate(groups, axis=1))
        dvln = jnp.concatenate(chunks, axis=0) if len(chunks) > 1 else chunks[0]
        dln_ref[0:1, :] += _rowsum(dvln * vhat)
        dln_ref[1:2, :] += _rowsum(dvln)
        dvhat = dvln * lng_v
        dv = rs * (dvhat - jnp.mean(dvhat, axis=-1, keepdims=True)
                   - vhat * jnp.mean(dvhat * vhat, axis=-1, keepdims=True))
        dzp = (jnp.concatenate([du, dv], axis=1) * _gelu_grad(zp)).astype(BF16)
        dzp_ref[...] = dzp
        da = _dot_nt(dzp, win_ref[...])
        dx, dsh, dsc, dng = _normmod_bwd(da, parts, md, 0)
        dh_ref[...] = g + dx
        _acc_rows(dmd_ref, i == 0, {SH1: dsh, SC1: dsc, G1: dg1, NG0: dng})

    return pl.pallas_call(
        body, name="gmlp_bwd", grid=(seq // tm,),
        in_specs=[_rows(tm, d), _rows(tm, d), _rows(tm, zw), _rows(tm, d),
                  pl.BlockSpec((None, 8, d), lambda i: (1, 0, 0)),
                  pl.BlockSpec((None, d, zw), lambda i: (0, 0, 0)), _full((1, half)), _full((1, half)),
                  _full((gg, ch, ch)), _full((gg, ch, ch)), _full((gg, ch, 1)),
                  pl.BlockSpec((None, half, d), lambda i: (0, 0, 0))],
        out_specs=[_rows(tm, d), _rows(tm, zw), _rows(tm, half), _rows(tm, d), _rows(tm, d),
                   pl.BlockSpec((None, 8, d), lambda i: (0, 0, 0)), _full((8, half)), _full((gg, ch, ch)),
                   _full((gg, ch, 1))],
        out_shape=[jax.ShapeDtypeStruct((seq, d), F32), jax.ShapeDtypeStruct((seq, zw), BF16),
                   jax.ShapeDtypeStruct((seq, half), BF16), jax.ShapeDtypeStruct((seq, d), BF16),
                   jax.ShapeDtypeStruct((seq, d), BF16), jax.ShapeDtypeStruct((1, 8, d), F32),
                   jax.ShapeDtypeStruct((8, half), F32), jax.ShapeDtypeStruct((gg, ch, ch), F32),
                   jax.ShapeDtypeStruct((gg, ch, 1), F32)],
        compiler_params=_params(),
    )(h, dh1, zpre, y, mods, w_in, lng, lnb, ws, ws_t, bs, w_out)


def _final_loss(h, tgt, fg, *, tm):
    seq, d = h.shape

    def body(h_ref, t_ref, g_ref, dh_ref, acc_ref):
        i = pl.program_id(0)
        gain = g_ref[...]
        xhat, r = _rms_parts(h_ref[...])
        err = xhat * gain - t_ref[...]
        dy = err * (1.0 / d)
        dxhat = dy * gain
        dh_ref[...] = r * (dxhat - xhat * jnp.mean(dxhat * xhat, axis=-1, keepdims=True))
        part = jnp.sum(_rowsum(err * err), axis=-1, keepdims=True) * (0.5 / d)
        _acc_rows(acc_ref, i == 0, {0: _rowsum(dy * xhat), 1: jnp.broadcast_to(part, (1, d))})

    return pl.pallas_call(
        body, name="final_loss", grid=(seq // tm,),
        in_specs=[_rows(tm, d), _rows(tm, d), _full((1, d))],
        out_specs=[_rows(tm, d), _full((8, d))],
        out_shape=[jax.ShapeDtypeStruct((seq, d), F32), jax.ShapeDtypeStruct((8, d), F32)],
        compiler_params=_params(),
    )(h, tgt, fg)


def _ada_fwd(c_all, ada_w, ada_b_cols):
    depth, d, ncs = ada_w.shape

    def body(c_ref, w_ref, b_ref, o_ref):
        s = _silu(c_ref[...]).astype(BF16)
        o_ref[...] = _dot(s, w_ref[...].astype(BF16)) + b_ref[...]

    return pl.pallas_call(
        body, name="ada_fwd", grid=(depth,),
        in_specs=[_full((16, d)), pl.BlockSpec((None, d, ncs), lambda i: (i, 0, 0)),
                  pl.BlockSpec((None, 1, ncs), lambda i: (i, 0, 0))],
        out_specs=pl.BlockSpec((None, 16, ncs), lambda i: (i, 0, 0)),
        out_shape=jax.ShapeDtypeStruct((depth, 16, ncs), F32),
        compiler_params=_params(),
    )(c_all, ada_w, ada_b_cols.reshape(depth, 1, ncs))


def _ada_bwd(c_all, c_all_t, dmod, ada_w):
    depth, d, ncs = ada_w.shape

    def body(c_ref, ct_ref, dm_ref, w_ref, gw_ref, dc_ref):
        i = pl.program_id(0)
        dm = dm_ref[...]
        dctx = _rowsum(dm[8:16])
        rid = lax.broadcasted_iota(jnp.int32, (8, ncs), 0)
        low = jnp.where(rid == 0, jnp.broadcast_to(dctx, (8, ncs)), 0.0)
        dm16 = jnp.concatenate([dm[0:8], low], axis=0).astype(BF16)
        gw_ref[...] = _dot(_silu(ct_ref[...]).astype(BF16), dm16)

        @pl.when(i == 0)
        def _():
            dc_ref[...] = jnp.zeros(dc_ref.shape, F32)

        dc_ref[...] += _dot_nt(low.astype(BF16), w_ref[...].astype(BF16)) * _silu_grad(c_ref[8:9, :])

    return pl.pallas_call(
        body, name="ada_bwd", grid=(depth,),
        in_specs=[_full((16, d)), _full((d, 16)), pl.BlockSpec((None, 16, ncs), lambda i: (i, 0, 0)),
                  pl.BlockSpec((None, d, ncs), lambda i: (i, 0, 0))],
        out_specs=[pl.BlockSpec((None, d, ncs), lambda i: (i, 0, 0)), _full((8, d))],
        out_shape=[jax.ShapeDtypeStruct((depth, d, ncs), F32), jax.ShapeDtypeStruct((8, d), F32)],
        compiler_params=_params(),
    )(c_all, c_all_t, dmod, ada_w)


def _adamw_math(w, g, m, v):
    m = ADAM_B1 * m + (1.0 - ADAM_B1) * g
    v = ADAM_B2 * v + (1.0 - ADAM_B2) * jnp.square(g)
    m_hat = m / (1.0 - ADAM_B1 ** ADAM_STEP)
    v_hat = v / (1.0 - ADAM_B2 ** ADAM_STEP)
    delta = -ADAM_LR * (m_hat / (jnp.sqrt(v_hat) + ADAM_EPS) + ADAM_WD * w)
    return delta, m, v


def _adamw(ga, gb, w, m, v, name):
    rows, cols = w.shape
    tr = rows
    while tr * cols * 4 > (1 << 20) and tr % 16 == 0:
        tr //= 2

    def body(ga_ref, gb_ref, w_ref, m_ref, v_ref, g_out, d_out, m_out, v_out):
        g = ga_ref[...] + gb_ref[...]
        delta, m_new, v_new = _adamw_math(w_ref[...], g, m_ref[...], v_ref[...])
        g_out[...] = g
        d_out[...] = delta
        m_out[...] = m_new
        v_out[...] = v_new

    spec = _rows(tr, cols)
    return pl.pallas_call(
        body, name=name, grid=(rows // tr,),
        in_specs=[spec] * 5, out_specs=[spec] * 4,
        out_shape=[jax.ShapeDtypeStruct((rows, cols), F32)] * 4,
        compiler_params=_params(),
    )(ga, gb, w, m, v)


def _sum_devices(gathered, name):
    n, rows, cols = gathered.shape
    tr = rows
    while tr * cols * 4 * n > (4 << 20) and tr % 16 == 0:
        tr //= 2

    def body(x_ref, o_ref):
        acc = x_ref[0]
        for j in range(1, n):
            acc = acc + x_ref[j]
        o_ref[...] = acc

    return pl.pallas_call(
        body, name=name, grid=(rows // tr,),
        in_specs=[pl.BlockSpec((n, tr, cols), lambda i: (0, i, 0))], out_specs=_rows(tr, cols),
        out_shape=jax.ShapeDtypeStruct((rows, cols), F32),
        compiler_params=_params(),
    )(gathered)


def _sum_partials(own, landed, name):
    rows, cols = own.shape
    tr = rows
    while tr * cols * 2 > (1 << 20) and tr % 32 == 0:
        tr //= 2

    def body(o_ref, l_ref, out_ref):
        acc = o_ref[...].astype(F32)
        for p in range(3):
            acc = acc + l_ref[p].astype(F32)
        out_ref[...] = acc

    return pl.pallas_call(
        body, name=name, grid=(rows // tr,),
        in_specs=[_rows(tr, cols), pl.BlockSpec((3, tr, cols), lambda i: (0, i, 0))], out_specs=_rows(tr, cols),
        out_shape=jax.ShapeDtypeStruct((rows, cols), F32),
        compiler_params=_params(),
    )(own, landed)


def _my_place():
    return lax.axis_index("x"), lax.axis_index("y"), lax.axis_index("c")


def _other_chips(x, y):
    return [(1 - x, y), (x, 1 - y), (1 - x, 1 - y)]


def _all_gather_small(block, name):
    rows, cols = block.shape

    def body(x_ref, out_ref, send_sems, recv_sems, local_sem):
        x, y, c = _my_place()
        me, sibling = (x, y, c), (x, y, 1 - c)
        chips = _other_chips(x, y)

        def slot(px, py, pc):
            return out_ref.at[4 * px + 2 * py + pc]

        def copy(k, blk, to, src=None):
            return pltpu.make_async_remote_copy(
                src_ref=slot(*blk) if src is None else src, dst_ref=slot(*blk),
                send_sem=send_sems.at[k], recv_sem=recv_sems.at[k], device_id=to, device_id_type=MESH)

        mine = pltpu.make_async_copy(x_ref, slot(*me), local_sem)
        mine.start()
        first = [copy(0, me, sibling, src=x_ref)]
        first += [copy(1 + j, me, (*chip, c), src=x_ref) for j, chip in enumerate(chips)]
        for cp in first:
            cp.start()
        passed = [copy(4 + j, (*chip, c), sibling) for j, chip in enumerate(chips)]
        for j, chip in enumerate(chips):
            copy(1 + j, (*chip, c), me).wait_recv()
            passed[j].start()
        copy(0, sibling, me).wait_recv()
        for j, chip in enumerate(chips):
            copy(4 + j, (*chip, 1 - c), me).wait_recv()
        for cp in first + passed:
            cp.wait_send()
        mine.wait()

    return pl.pallas_call(
        body, name=name,
        out_shape=jax.ShapeDtypeStruct((N_DEV, rows, cols), block.dtype),
        in_specs=[pl.BlockSpec(memory_space=pltpu.VMEM)],
        out_specs=pl.BlockSpec(memory_space=pltpu.VMEM),
        scratch_shapes=[pltpu.SemaphoreType.DMA((7,)), pltpu.SemaphoreType.DMA((7,)), pltpu.SemaphoreType.DMA],
        compiler_params=_params(),
    )(block)


def _shard_slice(ref, axis, k, size):
    idx = [slice(None)] * len(ref.shape)
    idx[axis] = pl.ds(k * size, size)
    return ref.at[tuple(idx)]


def _gather_weights(shards, axes):
    n = len(shards)
    out_shapes = []
    for s, ax in zip(shards, axes):
        shp = list(s.shape)
        shp[ax] *= N_CHIPS
        out_shapes.append(jax.ShapeDtypeStruct(tuple(shp), s.dtype))

    def body(*refs):
        ins, outs = refs[:n], refs[n:2 * n]
        send_sems, recv_sems, local_sems = refs[2 * n:]
        x, y, c = _my_place()
        chips = _other_chips(x, y)
        for kk in range(N_CHIPS):
            @pl.when(2 * x + y == kk)
            def _(kk=kk):
                local, sends = [], []
                for j in range(n):
                    size = ins[j].shape[axes[j]]
                    mine = _shard_slice(outs[j], axes[j], kk, size)
                    local.append(pltpu.make_async_copy(ins[j], mine, local_sems.at[j]))
                    for p, chip in enumerate(chips):
                        sends.append(pltpu.make_async_remote_copy(
                            src_ref=ins[j], dst_ref=mine, send_sem=send_sems.at[3 * j + p],
                            recv_sem=recv_sems.at[3 * j + p], device_id=(*chip, c), device_id_type=MESH))
                for cp in local + sends:
                    cp.start()
                for cp in sends:
                    cp.wait()
                for cp in local:
                    cp.wait()

    any_spec = pl.BlockSpec(memory_space=pl.ANY)
    return pl.pallas_call(
        body, name="gather_weights", out_shape=out_shapes,
        in_specs=[any_spec] * n, out_specs=[any_spec] * n,
        scratch_shapes=[pltpu.SemaphoreType.DMA((3 * n,)), pltpu.SemaphoreType.DMA((3 * n,)),
                        pltpu.SemaphoreType.DMA((n,))],
        compiler_params=_params(),
    )(*shards)


def _scatter_grads(groups):
    flat = [g for layers, _ in groups for g in layers]
    n = len(flat)
    out_shapes = []
    for layers, ax in groups:
        shp = list(layers[0].shape)
        shp[ax] //= N_CHIPS
        out_shapes.append(jax.ShapeDtypeStruct((len(layers), *shp), layers[0].dtype))
        out_shapes.append(jax.ShapeDtypeStruct((3, len(layers), *shp), layers[0].dtype))

    def body(*refs):
        ins = refs[:n]
        outs = refs[n:n + 2 * len(groups)]
        send_sems, recv_sems, local_sems = refs[n + 2 * len(groups):]
        x, y, c = _my_place()
        chips = _other_chips(x, y)
        for kk in range(N_CHIPS):
            @pl.when(2 * x + y == kk)
            def _(kk=kk):
                kx, ky = kk // 2, kk % 2
                peer_k = [2 * (1 - kx) + ky, 2 * kx + (1 - ky), 2 * (1 - kx) + (1 - ky)]
                local, sends = [], []
                j = 0
                for gi, (layers, ax) in enumerate(groups):
                    own_ref, land_ref = outs[2 * gi], outs[2 * gi + 1]
                    size = layers[0].shape[ax] // N_CHIPS
                    for li in range(len(layers)):
                        src_own = _shard_slice(ins[j], ax, kk, size)
                        local.append(pltpu.make_async_copy(src_own, own_ref.at[li], local_sems.at[j]))
                        for p, chip in enumerate(chips):
                            sends.append(pltpu.make_async_remote_copy(
                                src_ref=_shard_slice(ins[j], ax, peer_k[p], size), dst_ref=land_ref.at[p, li],
                                send_sem=send_sems.at[3 * j + p], recv_sem=recv_sems.at[3 * j + p],
                                device_id=(*chip, c), device_id_type=MESH))
                        j += 1
                for cp in local + sends:
                    cp.start()
                for cp in sends:
                    cp.wait()
                for cp in local:
                    cp.wait()

    any_spec = pl.BlockSpec(memory_space=pl.ANY)
    return pl.pallas_call(
        body, name="scatter_grads", out_shape=out_shapes,
        in_specs=[any_spec] * n, out_specs=[any_spec] * len(out_shapes),
        scratch_shapes=[pltpu.SemaphoreType.DMA((3 * n,)), pltpu.SemaphoreType.DMA((3 * n,)),
                        pltpu.SemaphoreType.DMA((n,))],
        compiler_params=_params(),
    )(*flat)


def _swap_with_sibling(parts):
    n = len(parts)

    def body(*refs):
        ins, outs = refs[:n], refs[n:2 * n]
        send_sems, recv_sems = refs[2 * n:]
        x, y, c = _my_place()
        copies = [pltpu.make_async_remote_copy(
            src_ref=ins[j], dst_ref=outs[j], send_sem=send_sems.at[j], recv_sem=recv_sems.at[j],
            device_id=(x, y, 1 - c), device_id_type=MESH) for j in range(n)]
        for cp in copies:
            cp.start()
        for cp in copies:
            cp.wait()

    any_spec = pl.BlockSpec(memory_space=pl.ANY)
    return pl.pallas_call(
        body, name="swap_with_sibling", out_shape=[jax.ShapeDtypeStruct(p.shape, p.dtype) for p in parts],
        in_specs=[any_spec] * n, out_specs=[any_spec] * n,
        scratch_shapes=[pltpu.SemaphoreType.DMA((n,)), pltpu.SemaphoreType.DMA((n,))],
        compiler_params=_params(),
    )(*parts)


TILE_ELEMS = SUBLANES * LANES


def _pack(arrays):
    parts = []
    for a in arrays:
        flat = a.reshape(-1).astype(F32)
        pad = (-flat.shape[0]) % TILE_ELEMS
        if pad:
            flat = jnp.concatenate([flat, jnp.zeros((pad,), F32)])
        parts.append(flat.reshape(-1, LANES))
    return jnp.concatenate(parts, axis=0) if len(parts) > 1 else parts[0]


def _unpack(buf, shapes):
    out, r = [], 0
    lead = buf.shape[:-2]
    for shp in shapes:
        size = math.prod(shp)
        nr = -(-size // TILE_ELEMS) * SUBLANES
        flat = buf[..., r:r + nr, :].reshape(*lead, nr * LANES)[..., :size]
        out.append(flat.reshape(*lead, *shp))
        r += nr
    return out


def _chip_cols(a, k, width):
    return lax.dynamic_slice_in_dim(a, k * width, width, axis=a.ndim - 1)


def _across_chips(gathered, c0_only_shape):
    return gathered.reshape(2, 2, 2, *c0_only_shape)[:, :, 0].reshape(N_CHIPS, *c0_only_shape)


def kernel(x, c, ctx, c_ctx, ada_w, ada_b, norm_g, mlp_w1, mlp_w2, pool_w, pool_scale, attn_w_qkv, attn_w_o, attn_q_g, attn_k_g, gm_w_in, gm_ln_g, gm_ln_b, gm_ws, gm_bs, gm_w_out, final_g, loss_target, m_c_ctx, m_ada_w, m_ada_b, m_norm_g, m_mlp_w1, m_mlp_w2, m_pool_w, m_pool_scale, m_attn_w_qkv, m_attn_w_o, m_attn_q_g, m_attn_k_g, m_gm_w_in, m_gm_ln_g, m_gm_ln_b, m_gm_ws, m_gm_bs, m_gm_w_out, m_final_g, v_c_ctx, v_ada_w, v_ada_b, v_norm_g, v_mlp_w1, v_mlp_w2, v_pool_w, v_pool_scale, v_attn_w_qkv, v_attn_w_o, v_attn_q_g, v_attn_k_g, v_gm_w_in, v_gm_ln_g, v_gm_ln_b, v_gm_ws, v_gm_bs, v_gm_w_out, v_final_g):
    seq, d = x.shape[1], x.shape[2]
    n_ctx = ctx.shape[1]
    total = n_ctx + seq
    hd = attn_q_g.shape[-1]
    nh = d // hd
    nkv = nh // 2
    gg, ch = gm_ws.shape[1], gm_ws.shape[-1]
    half = gm_w_out.shape[1] * N_CHIPS
    pgw = pool_w.shape[-1]
    tm = min(256, n_ctx)
    nct = n_ctx // tm
    seg_lens = (n_ctx, seq)

    mx, my, mc = _my_place()
    chip = 2 * mx + my
    me = 4 * mx + 2 * my + mc

    c_rows = jnp.concatenate([c, jnp.zeros((7, d), F32)], axis=0)
    c_gath = _all_gather_small(c_rows, "gather_cond")[:, 0, :]
    c_all = jnp.concatenate([c_gath, c_ctx[None, :], jnp.zeros((7, d), F32)], axis=0)
    ncs = ada_w.shape[-1]
    ada_cols = _ada_fwd(c_all, ada_w, _chip_cols(ada_b, chip, ncs))
    small_shapes = [ada_cols.shape, norm_g.shape, pool_scale.shape, gm_ln_g.shape, gm_ln_b.shape]
    gathered = _all_gather_small(_pack([ada_cols, norm_g, pool_scale, gm_ln_g, gm_ln_b]), "gather_small_params")
    per_chip = _across_chips(gathered, gathered.shape[1:])
    ada_g, ng_g, ps_g, lng_g, lnb_g = _unpack(per_chip, small_shapes)

    def join_last(a):
        return jnp.moveaxis(a, 0, -2).reshape(*a.shape[1:-1], N_CHIPS * a.shape[-1])

    ada_full = join_last(ada_g)
    ng_full = join_last(ng_g)
    ps_full = join_last(ps_g)
    lng_full = join_last(lng_g)
    lnb_full = join_last(lnb_g)
    mod_lat = lax.dynamic_slice_in_dim(ada_full, me, 1, axis=1).reshape(DEPTH, 6, d)
    mod_ctx = ada_full[:, 8].reshape(DEPTH, 6, d)
    mods = jnp.stack([jnp.concatenate([mod_ctx, ng_full], axis=1), jnp.concatenate([mod_lat, ng_full], axis=1)],
                     axis=1)

    w1_f, w2_f, pw_f, wqkv_f, wo_f, win_f, wout_f = _gather_weights(
        [w.astype(BF16) for w in (mlp_w1, mlp_w2, pool_w, attn_w_qkv, attn_w_o, gm_w_in, gm_w_out)],
        [2, 1, 2, 2, 1, 2, 1])

    gains = jnp.concatenate([attn_q_g, attn_k_g, jnp.zeros((6, hd), F32)], axis=0)
    ws_b = gm_ws[0].astype(BF16)
    ws_t = jnp.swapaxes(gm_ws[0], 1, 2).astype(BF16)
    bs_col = gm_bs[0][:, :, None]
    cos, sin = _rope_tables(n_ctx, seq, hd)
    lat = lambda i: mods[i, 1:2]

    hc0 = jnp.concatenate([ctx[0], x[0]], axis=0)
    ha0 = _pool_fwd(hc0, mods[0], pw_f, ps_full, 0, nct=nct, tm=tm, seg_lens=seg_lens)
    hc1, u0, o0 = _mlp_fwd(ha0, mods[0], w1_f, w2_f, 0, nct=nct, tm=tm)
    xa1 = _normmod_call(hc1, mods[1], 0, "attn_in_fwd", nct=nct, tm=tm)
    qkv, q_r, k_r, v_b = _qkv_fwd(xa1, wqkv_f, cos, sin, gains, nh=nh, nkv=nkv, nct=nct, tm=tm)
    o_att, lse = _flash_fwd(q_r, k_r, v_b, n_ctx=n_ctx, hd=hd)
    ha1, y1 = _proj_fwd(o_att, wo_f, hc1, mods[1], n_ctx=n_ctx, tm=tm)
    h2, u1, o1 = _mlp_fwd(ha1, lat(1), w1_f, w2_f, 1, nct=0, tm=tm)
    ha2, zpre, y2 = _gmlp_fwd(h2, mods[2], win_f, lng_full, lnb_full, ws_b, bs_col, wout_f, tm=tm)
    h3, u2, o2 = _mlp_fwd(ha2, lat(2), w1_f, w2_f, 2, nct=0, tm=tm)
    ha3 = _pool_fwd(h3, lat(3), pw_f, ps_full, 3, nct=0, tm=tm, seg_lens=seg_lens)
    h4, u3, o3 = _mlp_fwd(ha3, lat(3), w1_f, w2_f, 3, nct=0, tm=tm)
    dh4, fin_acc = _final_loss(h4, loss_target[0], final_g[None, :], tm=tm)

    dw1, dw2 = [None] * DEPTH, [None] * DEPTH
    dmods = [None] * DEPTH

    def mlp_back(i, h_in, dh_out, u, o, md, n_ct):
        dh_in, du, dob, mb, dmd = _mlp_bwd(h_in, dh_out, u, o, md, w1_f, w2_f, i, nct=n_ct, tm=tm)
        dw1[i] = _mm_tn(mb, du, f"mlp_dw1_{i}")
        dw2[i] = _mm_tn(u, dob, f"mlp_dw2_{i}", relu2=True)
        return dh_in, dmd

    def pool_back(i, h_in, dh_out, md, n_ct):
        dp, dmd_a, dps, dpw = _pool_bwd_weights(h_in, dh_out, md, pw_f, ps_full, i, nct=n_ct, tm=tm,
                                                seg_lens=seg_lens)
        dh_in, dmd_b = _pool_bwd_input(dp, h_in, dh_out, md, i, nct=n_ct, tm=tm, seg_lens=seg_lens, gw=pgw)
        return dh_in, dmd_a + dmd_b, dps, dpw

    zero_grp = jnp.zeros((1, 8, d), F32)
    dha3, dmd3 = mlp_back(3, ha3, dh4, u3, o3, lat(3), 0)
    dh3, dmd3p, dps3, dpw3 = pool_back(3, h3, dha3, lat(3), 0)
    dmods[3] = jnp.concatenate([zero_grp, dmd3 + dmd3p], axis=0)
    dha2, dmd2 = mlp_back(2, ha2, dh3, u2, o2, lat(2), 0)
    dh2, dzpre, gated, dyb2, ab2, dmd2g, dln, dws, dbs = _gmlp_bwd(
        h2, dha2, zpre, y2, mods[2], win_f, lng_full, lnb_full, ws_b, ws_t, bs_col, wout_f, tm=tm)
    dwin = _mm_tn(ab2, dzpre, "gmlp_dw_in")
    dwout = _mm_tn(gated, dyb2, "gmlp_dw_out")
    dmods[2] = jnp.concatenate([zero_grp, dmd2 + dmd2g], axis=0)
    dha1, dmd1 = mlp_back(1, ha1, dh2, u1, o1, lat(1), 0)
    do_att, dyb1, dmd1p = _proj_bwd(dha1, y1, mods[1], wo_f, tm=tm)
    dwo = _mm_tn(o_att, dyb1, "attn_dw_o")
    dq, dk, dv = _flash_bwd(q_r, k_r, v_b, o_att, do_att, lse, n_ctx=n_ctx, hd=hd)
    dqkv, dgains = _qkv_bwd(qkv, dq, dk, dv, cos, sin, gains, nh=nh, nkv=nkv, nct=nct, tm=tm)
    dwqkv = _mm_tn(xa1, dqkv, "attn_dw_qkv")
    dhc1, dmd1i = _attn_in_bwd(dqkv, wqkv_f, hc1, dha1, mods[1], nct=nct, tm=tm)
    dmods[1] = dmd1i + jnp.concatenate([zero_grp, dmd1 + dmd1p], axis=0)
    dha0, dmd0 = mlp_back(0, ha0, dhc1, u0, o0, mods[0], nct)
    dhc0, dmd0p, dps0, dpw0 = pool_back(0, hc0, dha0, mods[0], nct)
    dmods[0] = dmd0 + dmd0p
    grad_x = dhc0[n_ctx:][None]

    groups = [(dw1, 1), (dw2, 0), ([dpw0.astype(BF16), dpw3.astype(BF16)], 1), ([dwqkv], 1), ([dwo], 0),
              ([dwin], 1), ([dwout], 0)]
    scattered = _scatter_grads(groups)
    big = [(mlp_w1, m_mlp_w1, v_mlp_w1), (mlp_w2, m_mlp_w2, v_mlp_w2), (pool_w, m_pool_w, v_pool_w),
           (attn_w_qkv, m_attn_w_qkv, v_attn_w_qkv), (attn_w_o, m_attn_w_o, v_attn_w_o),
           (gm_w_in, m_gm_w_in, v_gm_w_in), (gm_w_out, m_gm_w_out, v_gm_w_out)]
    names = ["mlp_w1", "mlp_w2", "pool_w", "attn_w_qkv", "attn_w_o", "gm_w_in", "gm_w_out"]
    partial = []
    for gi, (w, _, _) in enumerate(big):
        cols = w.shape[-1]
        own, landed = scattered[2 * gi], scattered[2 * gi + 1]
        partial.append(_sum_partials(own.reshape(-1, cols), landed.reshape(3, -1, cols), f"sum_chips_{names[gi]}"))
    from_sibling = _swap_with_sibling(partial)
    big_out = {}
    for gi, (w, m, v) in enumerate(big):
        cols = w.shape[-1]
        res = _adamw(partial[gi], from_sibling[gi], w.reshape(-1, cols), m.reshape(-1, cols), v.reshape(-1, cols),
                     f"adamw_{names[gi]}")
        big_out[names[gi]] = [r.reshape(w.shape) for r in res]

    dmods_all = jnp.stack(dmods, axis=0)
    small_grads = [dmods_all, dws, dbs, dgains, dln, dps0, dps3, fin_acc]
    sg_shapes = [a.shape for a in small_grads]
    sg_gath = _all_gather_small(_pack(small_grads), "gather_small_grads")
    sg_sum = _sum_devices(sg_gath, "sum_small_grads")
    s_dmods, s_dws, s_dbs, s_dgains, s_dln, s_dps0, s_dps3, s_fin = _unpack(sg_sum, sg_shapes)
    loss = s_fin[1, 0]

    dm_dev = _unpack(sg_gath, sg_shapes[:1])[0]
    dm_lat = jnp.moveaxis(dm_dev[:, :, 1, :6, :], 0, 1).reshape(DEPTH, N_DEV, 6 * d)
    dm_ctx = jnp.moveaxis(dm_dev[:, :, 0, :6, :], 0, 1).reshape(DEPTH, N_DEV, 6 * d)
    dmod16 = _chip_cols(jnp.concatenate([dm_lat, dm_ctx], axis=1), chip, ncs)
    g_ada_w, dcc_part = _ada_bwd(c_all, c_all.T, dmod16, ada_w)
    dcc_gath = _all_gather_small(dcc_part, "gather_d_c_ctx")
    dcc_chips = _across_chips(dcc_gath, dcc_gath.shape[1:])
    dcc = _sum_devices(dcc_chips, "sum_d_c_ctx")[0]
    ada_res = _adamw(g_ada_w.reshape(-1, ncs), jnp.zeros((DEPTH * d, ncs), F32), ada_w.reshape(-1, ncs),
                     m_ada_w.reshape(-1, ncs), v_ada_w.reshape(-1, ncs), "adamw_ada_w")
    big_out["ada_w"] = [r.reshape(ada_w.shape) for r in ada_res]

    def cols_of(a, width):
        return _chip_cols(a, chip, width)

    zero = lambda a: jnp.zeros(a.shape, F32)
    ngw = norm_g.shape[-1]
    small = {
        "c_ctx": (dcc, zero(dcc), c_ctx, m_c_ctx, v_c_ctx),
        "ada_b": (s_dmods[:, 0, :6].reshape(DEPTH, 6 * d), s_dmods[:, 1, :6].reshape(DEPTH, 6 * d), ada_b, m_ada_b,
                  v_ada_b),
        "norm_g": (cols_of(s_dmods[:, 0, 6:8], ngw), cols_of(s_dmods[:, 1, 6:8], ngw), norm_g, m_norm_g, v_norm_g),
        "pool_scale": (cols_of(jnp.stack([s_dps0[0], s_dps3[0]]), pool_scale.shape[-1]), zero(pool_scale),
                       pool_scale, m_pool_scale, v_pool_scale),
        "attn_q_g": (s_dgains[0:1], zero(attn_q_g), attn_q_g, m_attn_q_g, v_attn_q_g),
        "attn_k_g": (s_dgains[1:2], zero(attn_k_g), attn_k_g, m_attn_k_g, v_attn_k_g),
        "gm_ln_g": (cols_of(s_dln[0:1], gm_ln_g.shape[-1]), zero(gm_ln_g), gm_ln_g, m_gm_ln_g, v_gm_ln_g),
        "gm_ln_b": (cols_of(s_dln[1:2], gm_ln_b.shape[-1]), zero(gm_ln_b), gm_ln_b, m_gm_ln_b, v_gm_ln_b),
        "gm_ws": (s_dws[None], zero(gm_ws), gm_ws, m_gm_ws, v_gm_ws),
        "gm_bs": (s_dbs[None, :, :, 0], zero(gm_bs), gm_bs, m_gm_bs, v_gm_bs),
        "final_g": (s_fin[0], zero(final_g), final_g, m_final_g, v_final_g),
    }
    keys = list(small)
    packed = [_pack([small[k][t] for k in keys]) for t in range(5)]
    res = _adamw(*packed, "adamw_small")
    shapes = [small[k][2].shape for k in keys]
    small_out = {k: [] for k in keys}
    for r in res:
        for k, a in zip(keys, _unpack(r, shapes)):
            small_out[k].append(a)

    order = ["c_ctx", "ada_w", "ada_b", "norm_g", "mlp_w1", "mlp_w2", "pool_w", "pool_scale", "attn_w_qkv",
             "attn_w_o", "attn_q_g", "attn_k_g", "gm_w_in", "gm_ln_g", "gm_ln_b", "gm_ws", "gm_bs", "gm_w_out",
             "final_g"]
    allo = {**big_out, **small_out}
    outs = [loss, grad_x]
    for t in range(4):
        outs += [allo[k][t] for k in order]
    return tuple(outs)
```

```python
import functools
import math

import jax
import jax.numpy as jnp
from jax import lax
from jax.experimental import pallas as pl
from jax.experimental.pallas import tpu as pltpu

F32 = jnp.float32
BF16 = jnp.bfloat16
MESH = pl.DeviceIdType.MESH

EPS = 1e-6
GRID_W = 64
ROPE_BASE = 10000.0
POOL_WINDOWS = (2, 4, 8, 16)
HALO = 8
DEPTH = 4
N_MIXERS = 3

ADAM_LR = 0.001
ADAM_B1 = 0.9
ADAM_B2 = 0.999
ADAM_EPS = 1e-08
ADAM_WD = 0.01
ADAM_STEP = 10

VMEM_LIMIT_BYTES = 56 * 1024 * 1024
LANES = 128
SUBLANES = 8
N_DEV = 8
N_CHIPS = 4

SH1, SC1, G1, SH2, SC2, G2, NG0, NG1 = range(8)


def _dot(a, b):
    return jnp.dot(a, b, preferred_element_type=F32)


def _dot_nt(a, b):
    return lax.dot_general(a, b, (((1,), (1,)), ((), ())), preferred_element_type=F32)


def _dot_tn(a, b):
    return lax.dot_general(a, b, (((0,), (0,)), ((), ())), preferred_element_type=F32)


def _dot_blocks(a, w_ref):
    return jnp.concatenate([_dot(a, w_ref[k]) for k in range(w_ref.shape[0])], axis=1)


def _dot_nt_blocks(a, w_ref):
    nb, _, w = w_ref.shape
    acc = _dot_nt(a[:, 0:w], w_ref[0])
    for k in range(1, nb):
        acc = acc + _dot_nt(a[:, k * w:(k + 1) * w], w_ref[k])
    return acc


def _params(**kw):
    return pltpu.CompilerParams(vmem_limit_bytes=VMEM_LIMIT_BYTES, **kw)


def _full(shape):
    nd = len(shape)
    return pl.BlockSpec(shape, lambda *_: (0,) * nd)


def _rows(tm, width):
    return pl.BlockSpec((tm, width), lambda i: (i, 0))


def _group_of(nct, groups):
    if groups == 1:
        return lambda i: 0
    return lambda i: jnp.where(i >= nct, 1, 0)


def _mods_spec(nct, groups, d):
    grp = _group_of(nct, groups)
    return pl.BlockSpec((None, 8, d), lambda i: (grp(i), 0, 0))


def _first_of_group(i, nct, groups):
    if groups == 1:
        return i == 0
    return jnp.logical_or(i == 0, i == nct)


def _rowsum(v):
    return jnp.sum(v, axis=0, keepdims=True)


def _rms_parts(x):
    r = lax.rsqrt(jnp.mean(x * x, axis=-1, keepdims=True) + EPS)
    return x * r, r


def _normmod(x, md, which):
    ng, sh, sc = (md[NG0:NG0 + 1], md[SH1:SH1 + 1], md[SC1:SC1 + 1]) if which == 0 else (
        md[NG1:NG1 + 1], md[SH2:SH2 + 1], md[SC2:SC2 + 1])
    xhat, r = _rms_parts(x)
    n = xhat * ng
    return n * (1.0 + sc) + sh, (xhat, r, n)


def _normmod_bwd(da, parts, md, which):
    xhat, r, n = parts
    ng, sc = (md[NG0:NG0 + 1], md[SC1:SC1 + 1]) if which == 0 else (md[NG1:NG1 + 1], md[SC2:SC2 + 1])
    dsh = _rowsum(da)
    dsc = _rowsum(da * n)
    dn = da * (1.0 + sc)
    dng = _rowsum(dn * xhat)
    dxhat = dn * ng
    dx = r * (dxhat - xhat * jnp.mean(dxhat * xhat, axis=-1, keepdims=True))
    return dx, dsh, dsc, dng


def _acc_rows(ref, first, rows):
    @pl.when(first)
    def _():
        ref[...] = jnp.zeros(ref.shape, ref.dtype)

    for r, v in rows.items():
        ref[r:r + 1, :] += v


def _shift_up(x, k):
    if k == 0:
        return x
    return pltpu.roll(x, x.shape[0] - k, axis=0)


def _gelu(x):
    k = math.sqrt(2.0 / math.pi)
    return 0.5 * x * (1.0 + jnp.tanh(k * (x + 0.044715 * x * x * x)))


def _gelu_grad(x):
    k = math.sqrt(2.0 / math.pi)
    t = jnp.tanh(k * (x + 0.044715 * x * x * x))
    return 0.5 * (1.0 + t) + 0.5 * x * (1.0 - t * t) * k * (1.0 + 3.0 * 0.044715 * x * x)


def _silu(x):
    return x / (1.0 + jnp.exp(-x))


def _silu_grad(x):
    s = 1.0 / (1.0 + jnp.exp(-x))
    return s * (1.0 + x * (1.0 - s))


def _mlp_fwd(h, mods, w1, w2, layer, *, nct, tm):
    rows, d = h.shape
    groups = mods.shape[0]
    nb, _, fc = w1.shape
    ff = nb * fc

    def body(h_ref, md_ref, w1_ref, w2_ref, h2_ref, u_ref, o_ref):
        x = h_ref[...]
        md = md_ref[...]
        m, _ = _normmod(x, md, 1)
        mb = m.astype(BF16)
        acc = jnp.zeros((tm, d), F32)
        for k in range(nb):
            u = _dot(mb, w1_ref[k])
            u_ref[:, k * fc:(k + 1) * fc] = u.astype(BF16)
            acc = acc + _dot(jnp.square(jnp.maximum(u, 0.0)).astype(BF16), w2_ref[k])
        o_ref[...] = acc.astype(BF16)
        h2_ref[...] = x + md[G2:G2 + 1] * acc

    return pl.pallas_call(
        body, name=f"mlp_fwd_{layer}", grid=(rows // tm,),
        in_specs=[_rows(tm, d), _mods_spec(nct, groups, d), _full(w1.shape), _full(w2.shape)],
        out_specs=[_rows(tm, d), _rows(tm, ff), _rows(tm, d)],
        out_shape=[jax.ShapeDtypeStruct((rows, d), F32), jax.ShapeDtypeStruct((rows, ff), BF16),
                   jax.ShapeDtypeStruct((rows, d), BF16)],
        compiler_params=_params(),
    )(h, mods, w1, w2)


def _mlp_bwd(h1, dh2, u, o, mods, w1, w2, layer, *, nct, tm):
    rows, d = h1.shape
    groups = mods.shape[0]
    nb, _, fc = w1.shape
    ff = nb * fc

    def body(h_ref, g_ref, u_ref, o_ref, md_ref, w1_ref, w2_ref, dh_ref, du_ref, dob_ref, mb_ref, dmd_ref):
        i = pl.program_id(0)
        x = h_ref[...]
        g = g_ref[...]
        md = md_ref[...]
        m, parts = _normmod(x, md, 1)
        mb_ref[...] = m.astype(BF16)
        dg2 = _rowsum(g * o_ref[...].astype(F32))
        dob = (g * md[G2:G2 + 1]).astype(BF16)
        dob_ref[...] = dob
        dm = jnp.zeros((tm, d), F32)
        for k in range(nb):
            uk = u_ref[:, k * fc:(k + 1) * fc].astype(F32)
            dr = _dot_nt(dob, w2_ref[k])
            duk = (dr * (2.0 * jnp.maximum(uk, 0.0))).astype(BF16)
            du_ref[:, k * fc:(k + 1) * fc] = duk
            dm = dm + _dot_nt(duk, w1_ref[k])
        dx, dsh, dsc, dng = _normmod_bwd(dm, parts, md, 1)
        dh_ref[...] = g + dx
        _acc_rows(dmd_ref, _first_of_group(i, nct, groups), {SH2: dsh, SC2: dsc, G2: dg2, NG1: dng})

    return pl.pallas_call(
        body, name=f"mlp_bwd_{layer}", grid=(rows // tm,),
        in_specs=[_rows(tm, d), _rows(tm, d), _rows(tm, ff), _rows(tm, d), _mods_spec(nct, groups, d),
                  _full(w1.shape), _full(w2.shape)],
        out_specs=[_rows(tm, d), _rows(tm, ff), _rows(tm, d), _rows(tm, d), _mods_spec(nct, groups, d)],
        out_shape=[jax.ShapeDtypeStruct((rows, d), F32), jax.ShapeDtypeStruct((rows, ff), BF16),
                   jax.ShapeDtypeStruct((rows, d), BF16), jax.ShapeDtypeStruct((rows, d), BF16),
                   jax.ShapeDtypeStruct((groups, 8, d), F32)],
        compiler_params=_params(),
    )(h1, dh2, u, o, mods, w1, w2)


def _div_tile(n, cap):
    if n <= cap:
        return n
    return max(t for t in range(LANES, cap + 1, LANES) if n % t == 0)


def _mm_tn(a, b, name, *, relu2=False, col_blocks=1):
    rows, m = a.shape
    n = b.shape[1]
    tmm = min(m, 1024)
    tn = min(n // col_blocks, 2048)
    per_block = n // col_blocks // tn
    tr = _div_tile(rows, 768)

    def body(a_ref, b_ref, o_ref, acc_ref):
        r = pl.program_id(2)

        @pl.when(r == 0)
        def _():
            acc_ref[...] = jnp.zeros(acc_ref.shape, F32)

        av = a_ref[...]
        if relu2:
            av = jnp.square(jnp.maximum(av.astype(F32), 0.0)).astype(BF16)
        acc_ref[...] += _dot_tn(av, b_ref[...])

        @pl.when(r == pl.num_programs(2) - 1)
        def _():
            o_ref[...] = acc_ref[...].astype(BF16)

    return pl.pallas_call(
        body, name=name, grid=(m // tmm, n // tn, rows // tr),
        in_specs=[pl.BlockSpec((tr, tmm), lambda i, j, r: (r, i)), pl.BlockSpec((tr, tn), lambda i, j, r: (r, j))],
        out_specs=pl.BlockSpec((None, tmm, tn), lambda i, j, r: (j // per_block, i, j % per_block)),
        out_shape=jax.ShapeDtypeStruct((col_blocks, m, n // col_blocks), BF16),
        scratch_shapes=[pltpu.VMEM((tmm, tn), F32)],
        compiler_params=_params(),
    )(a, b)


def _halo_specs(tm, d, rows):
    per = tm // HALO
    prev = pl.BlockSpec((HALO, d), lambda i: (jnp.maximum(i * per - 1, 0), 0))
    nxt = pl.BlockSpec((HALO, d), lambda i: (jnp.minimum((i + 1) * per, rows // HALO - 1), 0))
    return prev, _rows(tm, d), nxt


def _segment_positions(i, tm, nct, groups, seg_lens):
    if groups == 1:
        start, length = 0, seg_lens[-1]
    else:
        start = jnp.where(i >= nct, nct, 0)
        length = jnp.where(i >= nct, seg_lens[1], seg_lens[0])
    rid = lax.broadcasted_iota(jnp.int32, (tm + 2 * HALO, 1), 0)
    pos = (i - start) * tm - HALO + rid
    return pos, length


def _window_count(pos, length, w):
    hi = jnp.minimum(pos + (w - w // 2), length)
    lo = jnp.maximum(pos - w // 2, 0)
    return (hi - lo).astype(F32)


def _window_sum(xg, w, lead):
    b, k = xg, 1
    while k < w:
        b = b + _shift_up(b, k)
        k *= 2
    return _shift_up(b, HALO - lead)[0:xg.shape[0] - 2 * HALO]


def _pooled(ext, md, pos, length, gw):
    tm = ext.shape[0] - 2 * HALO
    a_ext, parts = _normmod(ext, md, 0)
    valid = jnp.logical_and(pos >= 0, pos < length)
    a_ext = jnp.where(valid, a_ext, 0.0)
    pos_c = pos[HALO:HALO + tm]
    ps = []
    for g, w in enumerate(POOL_WINDOWS):
        xg = a_ext[:, g * gw:(g + 1) * gw]
        s = _window_sum(xg, w, w // 2)
        ps.append(s / _window_count(pos_c, length, w) - xg[HALO:HALO + tm])
    return ps, parts


def _pool_fwd(h, mods, pw, pscale, layer, *, nct, tm, seg_lens):
    rows, d = h.shape
    groups = mods.shape[0]
    pg, gw = pw.shape[1], pw.shape[-1]

    def body(prev_ref, cur_ref, next_ref, md_ref, pw_ref, ps_ref, out_ref):
        i = pl.program_id(0)
        md = md_ref[...]
        cur = cur_ref[...]
        ext = jnp.concatenate([prev_ref[...], cur, next_ref[...]], axis=0)
        pos, length = _segment_positions(i, tm, nct, groups, seg_lens)
        ps, _ = _pooled(ext, md, pos, length, gw)
        for g in range(pg):
            yg = _dot(ps[g].astype(BF16), pw_ref[g]) * ps_ref[:, g * gw:(g + 1) * gw]
            out_ref[:, g * gw:(g + 1) * gw] = cur[:, g * gw:(g + 1) * gw] + md[G1:G1 + 1, g * gw:(g + 1) * gw] * yg

    j = layer // N_MIXERS
    return pl.pallas_call(
        body, name=f"pool_fwd_{layer}", grid=(rows // tm,),
        in_specs=[*_halo_specs(tm, d, rows), _mods_spec(nct, groups, d),
                  pl.BlockSpec((None, pg, gw, gw), lambda i: (j, 0, 0, 0)), _full((1, d))],
        out_specs=_rows(tm, d),
        out_shape=jax.ShapeDtypeStruct((rows, d), F32),
        compiler_params=_params(),
    )(h, h, h, mods, pw, pscale[j:j + 1])


def _pool_bwd_weights(h, dh1, mods, pw, pscale, layer, *, nct, tm, seg_lens):
    rows, d = h.shape
    groups = mods.shape[0]
    pg, gw = pw.shape[1], pw.shape[-1]

    def body(prev_ref, cur_ref, next_ref, g_ref, md_ref, pw_ref, ps_ref, dp_ref, dmd_ref, dps_ref, dpw_ref):
        i = pl.program_id(0)
        md = md_ref[...]
        ext = jnp.concatenate([prev_ref[...], cur_ref[...], next_ref[...]], axis=0)
        pos, length = _segment_positions(i, tm, nct, groups, seg_lens)
        ps, _ = _pooled(ext, md, pos, length, gw)
        gup = g_ref[...]

        @pl.when(i == 0)
        def _():
            dps_ref[...] = jnp.zeros(dps_ref.shape, F32)
            dpw_ref[...] = jnp.zeros(dpw_ref.shape, F32)

        dg1 = []
        for g in range(pg):
            cols = slice(g * gw, (g + 1) * gw)
            pb = ps[g].astype(BF16)
            yp = _dot(pb, pw_ref[g])
            sc = ps_ref[:, cols]
            dg1.append(_rowsum(gup[:, cols] * (yp * sc)))
            dy = gup[:, cols] * md[G1:G1 + 1, cols]
            dps_ref[0:1, cols] += _rowsum(dy * yp)
            dyp = (dy * sc).astype(BF16)
            dp_ref[:, cols] = _dot_nt(dyp, pw_ref[g])
            dpw_ref[g] += _dot_tn(pb, dyp)
        _acc_rows(dmd_ref, _first_of_group(i, nct, groups), {G1: jnp.concatenate(dg1, axis=1)})

    j = layer // N_MIXERS
    return pl.pallas_call(
        body, name=f"pool_bwd_w_{layer}", grid=(rows // tm,),
        in_specs=[*_halo_specs(tm, d, rows), _rows(tm, d), _mods_spec(nct, groups, d),
                  pl.BlockSpec((None, pg, gw, gw), lambda i: (j, 0, 0, 0)), _full((1, d))],
        out_specs=[_rows(tm, d), _mods_spec(nct, groups, d), _full((8, d)), _full((pg, gw, gw))],
        out_shape=[jax.ShapeDtypeStruct((rows, d), F32), jax.ShapeDtypeStruct((groups, 8, d), F32),
                   jax.ShapeDtypeStruct((8, d), F32), jax.ShapeDtypeStruct((pg, gw, gw), F32)],
        compiler_params=_params(),
    )(h, h, h, dh1, mods, pw, pscale[j:j + 1])


def _pool_bwd_input(dp, h, dh1, mods, layer, *, nct, tm, seg_lens, gw):
    rows, d = h.shape
    groups = mods.shape[0]

    def body(prev_ref, cur_ref, next_ref, h_ref, g_ref, md_ref, dh_ref, dmd_ref):
        i = pl.program_id(0)
        md = md_ref[...]
        dp_cur = cur_ref[...]
        ext = jnp.concatenate([prev_ref[...], dp_cur, next_ref[...]], axis=0)
        pos, length = _segment_positions(i, tm, nct, groups, seg_lens)
        valid = jnp.logical_and(pos >= 0, pos < length)
        das = []
        for g, w in enumerate(POOL_WINDOWS):
            cols = slice(g * gw, (g + 1) * gw)
            q = jnp.where(valid, ext[:, cols] / jnp.maximum(_window_count(pos, length, w), 1.0), 0.0)
            das.append(_window_sum(q, w, w // 2 - 1) - dp_cur[:, cols])
        da = jnp.concatenate(das, axis=1)
        _, parts = _normmod(h_ref[...], md, 0)
        dx, dsh, dsc, dng = _normmod_bwd(da, parts, md, 0)
        dh_ref[...] = g_ref[...] + dx
        _acc_rows(dmd_ref, _first_of_group(i, nct, groups), {SH1: dsh, SC1: dsc, NG0: dng})

    return pl.pallas_call(
        body, name=f"pool_bwd_x_{layer}", grid=(rows // tm,),
        in_specs=[*_halo_specs(tm, d, rows), _rows(tm, d), _rows(tm, d), _mods_spec(nct, groups, d)],
        out_specs=[_rows(tm, d), _mods_spec(nct, groups, d)],
        out_shape=[jax.ShapeDtypeStruct((rows, d), F32), jax.ShapeDtypeStruct((groups, 8, d), F32)],
        compiler_params=_params(),
    )(dp, dp, dp, h, dh1, mods)


def _rope_tables(n_ctx, seq, hd):
    half = hd // 2
    t = jnp.arange(seq)
    row = (t // GRID_W).astype(F32)
    col = (t % GRID_W).astype(F32)
    inv = ROPE_BASE ** (-jnp.arange(0, half, 2, dtype=F32) / half)
    ar = row[:, None] * inv[None, :]
    ac = col[:, None] * inv[None, :]
    cos = jnp.concatenate([jnp.cos(ar), jnp.cos(ar), jnp.cos(ac), jnp.cos(ac)], axis=1)
    sin = jnp.concatenate([-jnp.sin(ar), jnp.sin(ar), -jnp.sin(ac), jnp.sin(ac)], axis=1)
    cos = jnp.concatenate([jnp.ones((n_ctx, hd), F32), cos], axis=0)
    sin = jnp.concatenate([jnp.zeros((n_ctx, hd), F32), sin], axis=0)
    return cos, sin


def _rope_partner(x):
    hd = x.shape[-1]
    q = hd // 4
    lane = lax.broadcasted_iota(jnp.int32, x.shape, 1)
    first = (lane % (2 * q)) < q
    return jnp.where(first, pltpu.roll(x, hd - q, axis=1), pltpu.roll(x, q, axis=1))


def _normmod_call(h, mods, which, name, *, nct, tm):
    rows, d = h.shape
    groups = mods.shape[0]

    def body(h_ref, md_ref, a_ref):
        a, _ = _normmod(h_ref[...], md_ref[...], which)
        a_ref[...] = a.astype(BF16)

    return pl.pallas_call(
        body, name=name, grid=(rows // tm,),
        in_specs=[_rows(tm, d), _mods_spec(nct, groups, d)],
        out_specs=_rows(tm, d), out_shape=jax.ShapeDtypeStruct((rows, d), BF16),
        compiler_params=_params(),
    )(h, mods)


def _qkv_fwd(xa, wqkv, cos, sin, gains, *, nh, nkv, nct, tm):
    rows, d = xa.shape
    qw = wqkv.shape[0] * wqkv.shape[-1]
    hd = cos.shape[-1]

    def body(x_ref, w_ref, cos_ref, sin_ref, gn_ref, qkv_ref, q_ref, k_ref, v_ref):
        qkv = _dot_blocks(x_ref[...], w_ref)
        qkv_ref[...] = qkv
        c, s = cos_ref[...], sin_ref[...]
        for hh in range(nh + nkv):
            xh = qkv[:, hh * hd:(hh + 1) * hd]
            xhat, _ = _rms_parts(xh)
            y = xhat * (gn_ref[0:1, :] if hh < nh else gn_ref[1:2, :])
            rot = (y * c + _rope_partner(y) * s).astype(BF16)
            if hh < nh:
                q_ref[:, hh * hd:(hh + 1) * hd] = rot
            else:
                k_ref[:, (hh - nh) * hd:(hh - nh + 1) * hd] = rot
        v_ref[...] = qkv[:, (nh + nkv) * hd:].astype(BF16)

    return pl.pallas_call(
        body, name="attn_qkv_fwd", grid=(rows // tm,),
        in_specs=[_rows(tm, d), _full(wqkv.shape), _rows(tm, hd), _rows(tm, hd), _full((8, hd))],
        out_specs=[_rows(tm, qw), pl.BlockSpec((tm, nh * hd), lambda i: (jnp.maximum(i - nct, 0), 0)),
                   _rows(tm, nkv * hd), _rows(tm, nkv * hd)],
        out_shape=[jax.ShapeDtypeStruct((rows, qw), F32), jax.ShapeDtypeStruct((rows - nct * tm, nh * hd), BF16),
                   jax.ShapeDtypeStruct((rows, nkv * hd), BF16), jax.ShapeDtypeStruct((rows, nkv * hd), BF16)],
        compiler_params=_params(),
    )(xa, wqkv, cos, sin, gains)


ATTN_Q_TILE_CAP = 1024
ATTN_KV_TILE_CAP = 4224
ATTN_ROW_GROUP = 256
LOG2E = 1.4426950408889634


def _attn_tiles(seq, total):
    tq = _div_tile(seq, ATTN_Q_TILE_CAP)
    return tq, _div_tile(total, ATTN_KV_TILE_CAP), min(ATTN_ROW_GROUP, tq)


def _flash_fwd(q, k, v, *, n_ctx, hd):
    total = k.shape[0]
    seq = total - n_ctx
    nkv = k.shape[1] // hd
    tq, tk, rg = _attn_tiles(seq, total)
    nk = total // tk
    scale = hd ** -0.5
    c2 = scale * LOG2E

    def body(q_ref, k_ref, v_ref, o_ref, lse_ref, m_sc, l_sc, acc_sc):
        ki = pl.program_id(2)

        @pl.when(ki == 0)
        def _():
            m_sc[...] = jnp.full(m_sc.shape, -jnp.inf, F32)
            l_sc[...] = jnp.zeros(l_sc.shape, F32)
            acc_sc[...] = jnp.zeros(acc_sc.shape, F32)

        kk, vv = k_ref[...], v_ref[...]
        for g in range(2):
            for sub in range(tq // rg):
                rows = slice(g * tq + sub * rg, g * tq + (sub + 1) * rg)
                s = _dot_nt(q_ref[sub * rg:(sub + 1) * rg, g * hd:(g + 1) * hd], kk)
                m_old = m_sc[rows]
                m_new = jnp.maximum(m_old, jnp.max(s, axis=-1, keepdims=True))
                alpha = jnp.exp2((m_old - m_new) * c2)
                p = jnp.exp2((s - m_new) * c2)
                l_sc[rows] = alpha * l_sc[rows] + jnp.sum(p, axis=-1, keepdims=True)
                acc_sc[rows] = alpha * acc_sc[rows] + _dot(p.astype(BF16), vv)
                m_sc[rows] = m_new

        @pl.when(ki == nk - 1)
        def _():
            o2 = acc_sc[...] / l_sc[...]
            lse = m_sc[...] * scale + jnp.log(l_sc[...])
            o_ref[:, :hd] = o2[:tq].astype(BF16)
            o_ref[:, hd:] = o2[tq:].astype(BF16)
            lse_ref[:, 0:1] = lse[:tq]
            lse_ref[:, 1:2] = lse[tq:]

    return pl.pallas_call(
        body, name="attn_flash_fwd", grid=(nkv, seq // tq, nk),
        in_specs=[pl.BlockSpec((tq, 2 * hd), lambda h, i, j: (i, h)),
                  pl.BlockSpec((tk, hd), lambda h, i, j: (j, h)),
                  pl.BlockSpec((tk, hd), lambda h, i, j: (j, h))],
        out_specs=[pl.BlockSpec((tq, 2 * hd), lambda h, i, j: (i, h)),
                   pl.BlockSpec((None, tq, 2), lambda h, i, j: (h, i, 0))],
        out_shape=[jax.ShapeDtypeStruct((seq, 2 * nkv * hd), BF16), jax.ShapeDtypeStruct((nkv, seq, 2), F32)],
        scratch_shapes=[pltpu.VMEM((2 * tq, 1), F32), pltpu.VMEM((2 * tq, 1), F32), pltpu.VMEM((2 * tq, hd), F32)],
        compiler_params=_params(),
    )(q, k, v)


def _flash_bwd(q, k, v, o, do, lse, *, n_ctx, hd):
    total = k.shape[0]
    seq = total - n_ctx
    nkv = k.shape[1] // hd
    tq, tk, rg = _attn_tiles(seq, total)
    scale = hd ** -0.5
    c2 = scale * LOG2E

    def body(q_ref, k_ref, v_ref, o_ref, do_ref, lse_ref, dq_ref, dk_ref, dv_ref):
        ki, qi = pl.program_id(1), pl.program_id(2)
        kk, vv = k_ref[...], v_ref[...]

        @pl.when(qi == 0)
        def _():
            dk_ref[...] = jnp.zeros(dk_ref.shape, F32)
            dv_ref[...] = jnp.zeros(dv_ref.shape, F32)

        dk_acc = jnp.zeros((tk, hd), F32)
        dv_acc = jnp.zeros((tk, hd), F32)
        for g in range(2):
            for sub in range(tq // rg):
                rs = slice(sub * rg, (sub + 1) * rg)
                cs = slice(g * hd, (g + 1) * hd)
                qq = q_ref[rs, cs]
                dd = do_ref[rs, cs]
                delta = jnp.sum(dd.astype(F32) * o_ref[rs, cs].astype(F32), axis=-1, keepdims=True)
                p = jnp.exp2(_dot_nt(qq, kk) * c2 - lse_ref[rs, g:g + 1] * LOG2E)
                dp = _dot_nt(dd, vv)
                ds = (p * (dp - delta) * scale).astype(BF16)
                dv_acc = dv_acc + _dot_tn(p.astype(BF16), dd)
                dk_acc = dk_acc + _dot_tn(ds, qq)
                dq = _dot(ds, kk)
                rows = pl.ds(pl.multiple_of(qi * tq, tq) + sub * rg, rg)

                @pl.when(ki == 0)
                def _():
                    dq_ref[rows, cs] = dq

                @pl.when(ki > 0)
                def _():
                    dq_ref[rows, cs] += dq
        dk_ref[...] += dk_acc
        dv_ref[...] += dv_acc

    return pl.pallas_call(
        body, name="attn_flash_bwd", grid=(nkv, total // tk, seq // tq),
        in_specs=[pl.BlockSpec((tq, 2 * hd), lambda h, j, i: (i, h)),
                  pl.BlockSpec((tk, hd), lambda h, j, i: (j, h)),
                  pl.BlockSpec((tk, hd), lambda h, j, i: (j, h)),
                  pl.BlockSpec((tq, 2 * hd), lambda h, j, i: (i, h)),
                  pl.BlockSpec((tq, 2 * hd), lambda h, j, i: (i, h)),
                  pl.BlockSpec((None, tq, 2), lambda h, j, i: (h, i, 0))],
        out_specs=[pl.BlockSpec((seq, 2 * hd), lambda h, j, i: (0, h)),
                   pl.BlockSpec((tk, hd), lambda h, j, i: (j, h)),
                   pl.BlockSpec((tk, hd), lambda h, j, i: (j, h))],
        out_shape=[jax.ShapeDtypeStruct((seq, 2 * nkv * hd), F32), jax.ShapeDtypeStruct((total, nkv * hd), F32),
                   jax.ShapeDtypeStruct((total, nkv * hd), F32)],
        compiler_params=_params(),
    )(q, k, v, o, do, lse)


def _proj_fwd(o, wo, hc, mods, *, n_ctx, tm):
    seq, d = o.shape
    off = n_ctx // tm

    def body(o_ref, w_ref, h_ref, md_ref, h1_ref, y_ref):
        y = _dot(o_ref[...], w_ref[...])
        y_ref[...] = y.astype(BF16)
        h1_ref[...] = h_ref[...] + md_ref[G1:G1 + 1, :] * y

    return pl.pallas_call(
        body, name="attn_proj_fwd", grid=(seq // tm,),
        in_specs=[_rows(tm, d), _full((d, d)),
                  pl.BlockSpec((tm, d), lambda i: (i + off, 0)), pl.BlockSpec((None, 8, d), lambda i: (1, 0, 0))],
        out_specs=[_rows(tm, d), _rows(tm, d)],
        out_shape=[jax.ShapeDtypeStruct((seq, d), F32), jax.ShapeDtypeStruct((seq, d), BF16)],
        compiler_params=_params(),
    )(o, wo, hc, mods)


def _proj_bwd(dh1, y, mods, wo, *, tm):
    seq, d = dh1.shape

    def body(g_ref, y_ref, md_ref, w_ref, do_ref, dyb_ref, dmd_ref):
        i = pl.program_id(0)
        g = g_ref[...]
        dyb = (g * md_ref[G1:G1 + 1, :]).astype(BF16)
        dyb_ref[...] = dyb
        do_ref[...] = _dot_nt(dyb, w_ref[...]).astype(BF16)
        _acc_rows(dmd_ref, i == 0, {G1: _rowsum(g * y_ref[...].astype(F32))})

    return pl.pallas_call(
        body, name="attn_proj_bwd", grid=(seq // tm,),
        in_specs=[_rows(tm, d), _rows(tm, d), pl.BlockSpec((None, 8, d), lambda i: (1, 0, 0)), _full((d, d))],
        out_specs=[_rows(tm, d), _rows(tm, d), pl.BlockSpec((None, 8, d), lambda i: (0, 0, 0))],
        out_shape=[jax.ShapeDtypeStruct((seq, d), BF16), jax.ShapeDtypeStruct((seq, d), BF16),
                   jax.ShapeDtypeStruct((1, 8, d), F32)],
        compiler_params=_params(),
    )(dh1, y, mods, wo)


def _qkv_bwd(qkv, dq, dk, dv, cos, sin, gains, *, nh, nkv, nct, tm):
    rows, qw = qkv.shape
    hd = cos.shape[-1]

    def body(qkv_ref, dq_ref, dk_ref, dv_ref, cos_ref, sin_ref, gn_ref, out_ref, dgn_ref):
        i = pl.program_id(0)
        c, s = cos_ref[...], sin_ref[...]
        is_lat = (i >= nct).astype(F32)
        dqg = jnp.zeros((1, hd), F32)
        dkg = jnp.zeros((1, hd), F32)
        for hh in range(nh + nkv):
            if hh < nh:
                dr = dq_ref[:, hh * hd:(hh + 1) * hd] * is_lat
                gn = gn_ref[0:1, :]
            else:
                dr = dk_ref[:, (hh - nh) * hd:(hh - nh + 1) * hd]
                gn = gn_ref[1:2, :]
            dy = dr * c + _rope_partner(dr * s)
            xhat, r = _rms_parts(qkv_ref[:, hh * hd:(hh + 1) * hd])
            dgh = _rowsum(dy * xhat)
            if hh < nh:
                dqg = dqg + dgh
            else:
                dkg = dkg + dgh
            dxhat = dy * gn
            dx = r * (dxhat - xhat * jnp.mean(dxhat * xhat, axis=-1, keepdims=True))
            out_ref[:, hh * hd:(hh + 1) * hd] = dx.astype(BF16)
        out_ref[:, (nh + nkv) * hd:] = dv_ref[...].astype(BF16)
        _acc_rows(dgn_ref, i == 0, {0: dqg, 1: dkg})

    return pl.pallas_call(
        body, name="attn_qkv_bwd", grid=(rows // tm,),
        in_specs=[_rows(tm, qw), pl.BlockSpec((tm, nh * hd), lambda i: (jnp.maximum(i - nct, 0), 0)),
                  _rows(tm, nkv * hd), _rows(tm, nkv * hd), _rows(tm, hd), _rows(tm, hd), _full((8, hd))],
        out_specs=[_rows(tm, qw), _full((8, hd))],
        out_shape=[jax.ShapeDtypeStruct((rows, qw), BF16), jax.ShapeDtypeStruct((8, hd), F32)],
        compiler_params=_params(),
    )(qkv, dq, dk, dv, cos, sin, gains)


def _attn_in_bwd(dqkv, wqkv, hc, dh1, mods, *, nct, tm):
    rows, d = hc.shape
    qw = dqkv.shape[1]

    def body(dz_ref, w_ref, h_ref, g_ref, md_ref, dh_ref, dmd_ref):
        i = pl.program_id(0)
        md = md_ref[...]
        da = _dot_nt_blocks(dz_ref[...], w_ref)
        _, parts = _normmod(h_ref[...], md, 0)
        dx, dsh, dsc, dng = _normmod_bwd(da, parts, md, 0)
        dh_ref[...] = g_ref[...] * (i >= nct).astype(F32) + dx
        _acc_rows(dmd_ref, _first_of_group(i, nct, 2), {SH1: dsh, SC1: dsc, NG0: dng})

    return pl.pallas_call(
        body, name="attn_in_bwd", grid=(rows // tm,),
        in_specs=[_rows(tm, qw), _full(wqkv.shape), _rows(tm, d),
                  pl.BlockSpec((tm, d), lambda i: (jnp.maximum(i - nct, 0), 0)), _mods_spec(nct, 2, d)],
        out_specs=[_rows(tm, d), _mods_spec(nct, 2, d)],
        out_shape=[jax.ShapeDtypeStruct((rows, d), F32), jax.ShapeDtypeStruct((2, 8, d), F32)],
        compiler_params=_params(),
    )(dqkv, wqkv, hc, dh1, mods)


def _gmlp_gate(zp, lng, lnb, ws_ref, bs_ref, gg, ch):
    half = zp.shape[1] // 2
    ggw = half // gg
    z = _gelu(zp)
    u, v = z[:, :half], z[:, half:]
    vc = v - jnp.mean(v, axis=-1, keepdims=True)
    rs = lax.rsqrt(jnp.mean(vc * vc, axis=-1, keepdims=True) + EPS)
    vhat = vc * rs
    vln = (vhat * lng + lnb).astype(BF16)
    chunks = []
    for n in range(zp.shape[0] // ch):
        groups = []
        for g in range(gg):
            groups.append(_dot(ws_ref[g], vln[n * ch:(n + 1) * ch, g * ggw:(g + 1) * ggw]) + bs_ref[g])
        chunks.append(jnp.concatenate(groups, axis=1))
    sv = jnp.concatenate(chunks, axis=0) if len(chunks) > 1 else chunks[0]
    return u, sv, vhat, rs, vln


def _gmlp_fwd(h, mods, w_in, lng, lnb, ws, bs, w_out, *, tm):
    seq, d = h.shape
    zw = w_in.shape[0] * w_in.shape[-1]
    half = zw // 2
    gg, ch = ws.shape[0], ws.shape[-1]

    def body(h_ref, md_ref, win_ref, lng_ref, lnb_ref, ws_ref, bs_ref, wout_ref, h1_ref, zp_ref, y_ref):
        x = h_ref[...]
        md = md_ref[...]
        a, _ = _normmod(x, md, 0)
        zp = _dot_blocks(a.astype(BF16), win_ref)
        zp_ref[...] = zp.astype(BF16)
        u, sv, _, _, _ = _gmlp_gate(zp, lng_ref[...], lnb_ref[...], ws_ref, bs_ref, gg, ch)
        y = _dot((u * sv).astype(BF16), wout_ref[...])
        y_ref[...] = y.astype(BF16)
        h1_ref[...] = x + md[G1:G1 + 1] * y

    return pl.pallas_call(
        body, name="gmlp_fwd", grid=(seq // tm,),
        in_specs=[_rows(tm, d), pl.BlockSpec((None, 8, d), lambda i: (1, 0, 0)),
                  _full(w_in.shape), _full((1, half)), _full((1, half)),
                  _full((gg, ch, ch)), _full((gg, ch, 1)), _full((half, d))],
        out_specs=[_rows(tm, d), _rows(tm, zw), _rows(tm, d)],
        out_shape=[jax.ShapeDtypeStruct((seq, d), F32), jax.ShapeDtypeStruct((seq, zw), BF16),
                   jax.ShapeDtypeStruct((seq, d), BF16)],
        compiler_params=_params(),
    )(h, mods, w_in, lng, lnb, ws, bs, w_out)


def _gmlp_bwd(h, dh1, zpre, y, mods, w_in, lng, lnb, ws, ws_t, bs, w_out, *, tm):
    seq, d = h.shape
    zw = w_in.shape[0] * w_in.shape[-1]
    half = zw // 2
    gg, ch = ws.shape[0], ws.shape[-1]
    ggw = half // gg

    def body(h_ref, g_ref, zp_ref, y_ref, md_ref, win_ref, lng_ref, lnb_ref, ws_ref, wst_ref, bs_ref, wout_ref,
             dh_ref, dzp_ref, gated_ref, dyb_ref, ab_ref, dmd_ref, dln_ref, dws_ref, dbs_ref):
        i = pl.program_id(0)
        x = h_ref[...]
        md = md_ref[...]
        a, parts = _normmod(x, md, 0)
        ab_ref[...] = a.astype(BF16)
        zp = zp_ref[...].astype(F32)
        lng_v = lng_ref[...]
        u, sv, vhat, rs, vln = _gmlp_gate(zp, lng_v, lnb_ref[...], ws_ref, bs_ref, gg, ch)
        g = g_ref[...]
        dg1 = _rowsum(g * y_ref[...].astype(F32))
        dyb = (g * md[G1:G1 + 1]).astype(BF16)
        dyb_ref[...] = dyb
        gated_ref[...] = (u * sv).astype(BF16)
        dgated = _dot_nt(dyb, wout_ref[...])
        du = dgated * sv
        dsv = dgated * u

        @pl.when(i == 0)
        def _():
            dws_ref[...] = jnp.zeros(dws_ref.shape, F32)
            dbs_ref[...] = jnp.zeros(dbs_ref.shape, F32)
            dln_ref[...] = jnp.zeros(dln_ref.shape, F32)

        chunks = []
        for n in range(tm // ch):
            groups = []
            for gi in range(gg):
                blk = dsv[n * ch:(n + 1) * ch, gi * ggw:(gi + 1) * ggw]
                dbs_ref[gi] += jnp.sum(blk, axis=-1, keepdims=True)
                blk_b = blk.astype(BF16)
                dws_ref[gi] += _dot_nt(blk_b, vln[n * ch:(n + 1) * ch, gi * ggw:(gi + 1) * ggw])
                groups.append(_dot(wst_ref[gi], blk_b))
            chunks.append(jnp.concatenate(groups, axis=1))
        dvln = jnp.concatenate(chunks, axis=0) if len(chunks) > 1 else chunks[0]
        dln_ref[0:1, :] += _rowsum(dvln * vhat)
        dln_ref[1:2, :] += _rowsum(dvln)
        dvhat = dvln * lng_v
        dv = rs * (dvhat - jnp.mean(dvhat, axis=-1, keepdims=True)
                   - vhat * jnp.mean(dvhat * vhat, axis=-1, keepdims=True))
        dzp = (jnp.concatenate([du, dv], axis=1) * _gelu_grad(zp)).astype(BF16)
        dzp_ref[...] = dzp
        da = _dot_nt_blocks(dzp, win_ref)
        dx, dsh, dsc, dng = _normmod_bwd(da, parts, md, 0)
        dh_ref[...] = g + dx
        _acc_rows(dmd_ref, i == 0, {SH1: dsh, SC1: dsc, G1: dg1, NG0: dng})

    return pl.pallas_call(
        body, name="gmlp_bwd", grid=(seq // tm,),
        in_specs=[_rows(tm, d), _rows(tm, d), _rows(tm, zw), _rows(tm, d),
                  pl.BlockSpec((None, 8, d), lambda i: (1, 0, 0)),
                  _full(w_in.shape), _full((1, half)), _full((1, half)),
                  _full((gg, ch, ch)), _full((gg, ch, ch)), _full((gg, ch, 1)), _full((half, d))],
        out_specs=[_rows(tm, d), _rows(tm, zw), _rows(tm, half), _rows(tm, d), _rows(tm, d),
                   pl.BlockSpec((None, 8, d), lambda i: (0, 0, 0)), _full((8, half)), _full((gg, ch, ch)),
                   _full((gg, ch, 1))],
        out_shape=[jax.ShapeDtypeStruct((seq, d), F32), jax.ShapeDtypeStruct((seq, zw), BF16),
                   jax.ShapeDtypeStruct((seq, half), BF16), jax.ShapeDtypeStruct((seq, d), BF16),
                   jax.ShapeDtypeStruct((seq, d), BF16), jax.ShapeDtypeStruct((1, 8, d), F32),
                   jax.ShapeDtypeStruct((8, half), F32), jax.ShapeDtypeStruct((gg, ch, ch), F32),
                   jax.ShapeDtypeStruct((gg, ch, 1), F32)],
        compiler_params=_params(),
    )(h, dh1, zpre, y, mods, w_in, lng, lnb, ws, ws_t, bs, w_out)


def _final_loss(h, tgt, fg, *, tm):
    seq, d = h.shape

    def body(h_ref, t_ref, g_ref, dh_ref, acc_ref):
        i = pl.program_id(0)
        gain = g_ref[...]
        xhat, r = _rms_parts(h_ref[...])
        err = xhat * gain - t_ref[...]
        dy = err * (1.0 / d)
        dxhat = dy * gain
        dh_ref[...] = r * (dxhat - xhat * jnp.mean(dxhat * xhat, axis=-1, keepdims=True))
        part = jnp.sum(_rowsum(err * err), axis=-1, keepdims=True) * (0.5 / d)
        _acc_rows(acc_ref, i == 0, {0: _rowsum(dy * xhat), 1: jnp.broadcast_to(part, (1, d))})

    return pl.pallas_call(
        body, name="final_loss", grid=(seq // tm,),
        in_specs=[_rows(tm, d), _rows(tm, d), _full((1, d))],
        out_specs=[_rows(tm, d), _full((8, d))],
        out_shape=[jax.ShapeDtypeStruct((seq, d), F32), jax.ShapeDtypeStruct((8, d), F32)],
        compiler_params=_params(),
    )(h, tgt, fg)


def _ada_fwd(c_all, ada_w, ada_b_cols):
    depth, d, ncs = ada_w.shape

    def body(c_ref, w_ref, b_ref, o_ref):
        s = _silu(c_ref[...]).astype(BF16)
        o_ref[...] = _dot(s, w_ref[...].astype(BF16)) + b_ref[...]

    return pl.pallas_call(
        body, name="ada_fwd", grid=(depth,),
        in_specs=[_full((16, d)), pl.BlockSpec((None, d, ncs), lambda i: (i, 0, 0)),
                  pl.BlockSpec((None, 1, ncs), lambda i: (i, 0, 0))],
        out_specs=pl.BlockSpec((None, 16, ncs), lambda i: (i, 0, 0)),
        out_shape=jax.ShapeDtypeStruct((depth, 16, ncs), F32),
        compiler_params=_params(),
    )(c_all, ada_w, ada_b_cols.reshape(depth, 1, ncs))


def _ada_bwd(c_all, c_all_t, dmod, ada_w):
    depth, d, ncs = ada_w.shape

    def body(c_ref, ct_ref, dm_ref, w_ref, gw_ref, dc_ref):
        i = pl.program_id(0)
        dm = dm_ref[...]
        dctx = _rowsum(dm[8:16])
        rid = lax.broadcasted_iota(jnp.int32, (8, ncs), 0)
        low = jnp.where(rid == 0, jnp.broadcast_to(dctx, (8, ncs)), 0.0)
        dm16 = jnp.concatenate([dm[0:8], low], axis=0).astype(BF16)
        gw_ref[...] = _dot(_silu(ct_ref[...]).astype(BF16), dm16)

        @pl.when(i == 0)
        def _():
            dc_ref[...] = jnp.zeros(dc_ref.shape, F32)

        dc_ref[...] += _dot_nt(low.astype(BF16), w_ref[...].astype(BF16)) * _silu_grad(c_ref[8:9, :])

    return pl.pallas_call(
        body, name="ada_bwd", grid=(depth,),
        in_specs=[_full((16, d)), _full((d, 16)), pl.BlockSpec((None, 16, ncs), lambda i: (i, 0, 0)),
                  pl.BlockSpec((None, d, ncs), lambda i: (i, 0, 0))],
        out_specs=[pl.BlockSpec((None, d, ncs), lambda i: (i, 0, 0)), _full((8, d))],
        out_shape=[jax.ShapeDtypeStruct((depth, d, ncs), F32), jax.ShapeDtypeStruct((8, d), F32)],
        compiler_params=_params(),
    )(c_all, c_all_t, dmod, ada_w)


def _adamw_math(w, g, m, v):
    m = ADAM_B1 * m + (1.0 - ADAM_B1) * g
    v = ADAM_B2 * v + (1.0 - ADAM_B2) * jnp.square(g)
    m_hat = m / (1.0 - ADAM_B1 ** ADAM_STEP)
    v_hat = v / (1.0 - ADAM_B2 ** ADAM_STEP)
    delta = -ADAM_LR * (m_hat / (jnp.sqrt(v_hat) + ADAM_EPS) + ADAM_WD * w)
    return delta, m, v


def _adamw(ga, gb, w, m, v, name):
    rows, cols = w.shape
    tr = rows
    while tr * cols * 4 > (1 << 20) and tr % 16 == 0:
        tr //= 2

    def body(ga_ref, gb_ref, w_ref, m_ref, v_ref, g_out, d_out, m_out, v_out):
        g = ga_ref[...] + gb_ref[...]
        delta, m_new, v_new = _adamw_math(w_ref[...], g, m_ref[...], v_ref[...])
        g_out[...] = g
        d_out[...] = delta
        m_out[...] = m_new
        v_out[...] = v_new

    spec = _rows(tr, cols)
    return pl.pallas_call(
        body, name=name, grid=(rows // tr,),
        in_specs=[spec] * 5, out_specs=[spec] * 4,
        out_shape=[jax.ShapeDtypeStruct((rows, cols), F32)] * 4,
        compiler_params=_params(),
    )(ga, gb, w, m, v)


def _sum_devices(gathered, name):
    n, rows, cols = gathered.shape
    tr = rows
    while tr * cols * 4 * n > (4 << 20) and tr % 16 == 0:
        tr //= 2

    def body(x_ref, o_ref):
        acc = x_ref[0]
        for j in range(1, n):
            acc = acc + x_ref[j]
        o_ref[...] = acc

    return pl.pallas_call(
        body, name=name, grid=(rows // tr,),
        in_specs=[pl.BlockSpec((n, tr, cols), lambda i: (0, i, 0))], out_specs=_rows(tr, cols),
        out_shape=jax.ShapeDtypeStruct((rows, cols), F32),
        compiler_params=_params(),
    )(gathered)


def _sum_partials(owns, landeds, name):
    n = len(owns)
    cols = owns[0].shape[-1]
    owns = [o.reshape(-1, cols) for o in owns]
    landeds = [l.reshape(3, -1, cols) for l in landeds]
    rows = owns[0].shape[0]
    tr = rows
    while tr * cols * 2 * n > (1 << 20) and tr % 32 == 0:
        tr //= 2

    def body(*refs):
        out_ref = refs[-1]
        for li in range(n):
            acc = refs[li][...].astype(F32)
            for p in range(3):
                acc = acc + refs[n + li][p].astype(F32)
            out_ref[li] = acc

    out = pl.pallas_call(
        body, name=name, grid=(rows // tr,),
        in_specs=[_rows(tr, cols)] * n + [pl.BlockSpec((3, tr, cols), lambda i: (0, i, 0))] * n,
        out_specs=pl.BlockSpec((n, tr, cols), lambda i: (0, i, 0)),
        out_shape=jax.ShapeDtypeStruct((n, rows, cols), F32),
        compiler_params=_params(),
    )(*owns, *landeds)
    return out.reshape(n * rows, cols)


def _my_place():
    return lax.axis_index("x"), lax.axis_index("y"), lax.axis_index("c")


def _other_chips(x, y):
    return [(1 - x, y), (x, 1 - y), (1 - x, 1 - y)]


def _all_gather_small(block, name):
    rows, cols = block.shape

    def body(x_ref, out_ref, send_sems, recv_sems, local_sem):
        x, y, c = _my_place()
        me, sibling = (x, y, c), (x, y, 1 - c)
        chips = _other_chips(x, y)

        def slot(px, py, pc):
            return out_ref.at[4 * px + 2 * py + pc]

        def copy(k, blk, to, src=None):
            return pltpu.make_async_remote_copy(
                src_ref=slot(*blk) if src is None else src, dst_ref=slot(*blk),
                send_sem=send_sems.at[k], recv_sem=recv_sems.at[k], device_id=to, device_id_type=MESH)

        mine = pltpu.make_async_copy(x_ref, slot(*me), local_sem)
        mine.start()
        first = [copy(0, me, sibling, src=x_ref)]
        first += [copy(1 + j, me, (*chip, c), src=x_ref) for j, chip in enumerate(chips)]
        for cp in first:
            cp.start()
        passed = [copy(4 + j, (*chip, c), sibling) for j, chip in enumerate(chips)]
        for j, chip in enumerate(chips):
            copy(1 + j, (*chip, c), me).wait_recv()
            passed[j].start()
        copy(0, sibling, me).wait_recv()
        for j, chip in enumerate(chips):
            copy(4 + j, (*chip, 1 - c), me).wait_recv()
        for cp in first + passed:
            cp.wait_send()
        mine.wait()

    return pl.pallas_call(
        body, name=name,
        out_shape=jax.ShapeDtypeStruct((N_DEV, rows, cols), block.dtype),
        in_specs=[pl.BlockSpec(memory_space=pltpu.VMEM)],
        out_specs=pl.BlockSpec(memory_space=pltpu.VMEM),
        scratch_shapes=[pltpu.SemaphoreType.DMA((7,)), pltpu.SemaphoreType.DMA((7,)), pltpu.SemaphoreType.DMA],
        compiler_params=_params(),
    )(block)


HBM_SPEC = pl.BlockSpec(memory_space=pltpu.HBM)
SEM_SPEC = pl.BlockSpec(memory_space=pltpu.SEMAPHORE)
DATAFLOW_EFFECT = pltpu.SideEffectType.DATAFLOW_SIDE_EFFECTING


def _gather_views(src, land, p, x, y):
    return src, land.at[2 * x + y]


def _scatter_views(src, land, p, x, y):
    peer_chip = (2 * (1 - x) + y, 2 * x + (1 - y), 2 * (1 - x) + (1 - y))[p]
    return src.at[peer_chip], land.at[p]


def _chip_exchange_copies(srcs, lands, send_sems, recv_sems, views):
    x, y, c = _my_place()
    copies = []
    for j, (src, land) in enumerate(zip(srcs, lands)):
        for p, chip in enumerate(_other_chips(x, y)):
            s_view, d_view = views(src, land, p, x, y)
            copies.append(pltpu.make_async_remote_copy(
                src_ref=s_view, dst_ref=d_view, send_sem=send_sems.at[3 * j + p], recv_sem=recv_sems.at[3 * j + p],
                device_id=(*chip, c), device_id_type=MESH))
    return copies


def _exchange_start(srcs, lands, views, name):
    n = len(srcs)

    def body(*refs):
        send_sems, recv_sems = refs[2 * n], refs[2 * n + 1]
        token = refs[-1]
        for cp in _chip_exchange_copies(refs[:n], refs[n:2 * n], send_sems, recv_sems, views):
            cp.start()
        token[...] = jnp.zeros(token.shape, token.dtype)

    operands = [pltpu.with_memory_space_constraint(a, pltpu.HBM) for a in (*srcs, *lands)]
    out = pl.pallas_call(
        body, name=name,
        out_shape=(pltpu.SemaphoreType.DMA((3 * n,)), pltpu.SemaphoreType.DMA((3 * n,)),
                   *[pltpu.HBM(a.shape, a.dtype) for a in operands], jax.ShapeDtypeStruct((8, LANES), F32)),
        in_specs=[HBM_SPEC] * (2 * n),
        out_specs=(SEM_SPEC, SEM_SPEC, *[HBM_SPEC] * (2 * n), pl.BlockSpec(memory_space=pltpu.VMEM)),
        input_output_aliases={i: 2 + i for i in range(2 * n)},
        compiler_params=pltpu.CompilerParams(has_side_effects=DATAFLOW_EFFECT),
    )(*operands)
    return out[0], out[1], list(out[2:2 + n]), list(out[2 + n:2 + 2 * n]), out[-1]


def _exchange_wait(send_sems, recv_sems, srcs, lands, views, after, name):
    n = len(srcs)

    def body(*refs):
        send, recv = refs[2 * n], refs[2 * n + 1]
        for cp in _chip_exchange_copies(refs[:n], refs[n:2 * n], send, recv, views):
            cp.wait_send()
            cp.wait_recv()

    out = pl.pallas_call(
        body, name=name,
        out_shape=tuple(pltpu.HBM(a.shape, a.dtype) for a in (*srcs, *lands)),
        in_specs=[HBM_SPEC] * (2 * n) + [SEM_SPEC, SEM_SPEC, HBM_SPEC],
        out_specs=tuple([HBM_SPEC] * (2 * n)),
        input_output_aliases={i: i for i in range(2 * n)},
        compiler_params=pltpu.CompilerParams(has_side_effects=DATAFLOW_EFFECT),
    )(*srcs, *lands, send_sems, recv_sems, pltpu.with_memory_space_constraint(after, pltpu.HBM))
    return list(out[:n]), list(out[n:])


def _landing_for_gather(shard, chip):
    land = lax.empty((N_CHIPS, *shard.shape), shard.dtype)
    return lax.dynamic_update_index_in_dim(land, shard, chip, 0)


def _swap_with_sibling(parts):
    n = len(parts)

    def body(*refs):
        ins, outs = refs[:n], refs[n:2 * n]
        send_sems, recv_sems = refs[2 * n:]
        x, y, c = _my_place()
        copies = [pltpu.make_async_remote_copy(
            src_ref=ins[j], dst_ref=outs[j], send_sem=send_sems.at[j], recv_sem=recv_sems.at[j],
            device_id=(x, y, 1 - c), device_id_type=MESH) for j in range(n)]
        for cp in copies:
            cp.start()
        for cp in copies:
            cp.wait()

    any_spec = pl.BlockSpec(memory_space=pl.ANY)
    return pl.pallas_call(
        body, name="swap_with_sibling", out_shape=[jax.ShapeDtypeStruct(p.shape, p.dtype) for p in parts],
        in_specs=[any_spec] * n, out_specs=[any_spec] * n,
        scratch_shapes=[pltpu.SemaphoreType.DMA((n,)), pltpu.SemaphoreType.DMA((n,))],
        compiler_params=_params(),
    )(*parts)


TILE_ELEMS = SUBLANES * LANES


def _pack(arrays):
    parts = []
    for a in arrays:
        flat = a.reshape(-1).astype(F32)
        pad = (-flat.shape[0]) % TILE_ELEMS
        if pad:
            flat = jnp.concatenate([flat, jnp.zeros((pad,), F32)])
        parts.append(flat.reshape(-1, LANES))
    return jnp.concatenate(parts, axis=0) if len(parts) > 1 else parts[0]


def _unpack(buf, shapes):
    out, r = [], 0
    lead = buf.shape[:-2]
    for shp in shapes:
        size = math.prod(shp)
        nr = -(-size // TILE_ELEMS) * SUBLANES
        flat = buf[..., r:r + nr, :].reshape(*lead, nr * LANES)[..., :size]
        out.append(flat.reshape(*lead, *shp))
        r += nr
    return out


def _chip_cols(a, k, width):
    return lax.dynamic_slice_in_dim(a, k * width, width, axis=a.ndim - 1)


def _across_chips(gathered, c0_only_shape):
    return gathered.reshape(2, 2, 2, *c0_only_shape)[:, :, 0].reshape(N_CHIPS, *c0_only_shape)


def kernel(x, c, ctx, c_ctx, ada_w, ada_b, norm_g, mlp_w1, mlp_w2, pool_w, pool_scale, attn_w_qkv, attn_w_o, attn_q_g, attn_k_g, gm_w_in, gm_ln_g, gm_ln_b, gm_ws, gm_bs, gm_w_out, final_g, loss_target, m_c_ctx, m_ada_w, m_ada_b, m_norm_g, m_mlp_w1, m_mlp_w2, m_pool_w, m_pool_scale, m_attn_w_qkv, m_attn_w_o, m_attn_q_g, m_attn_k_g, m_gm_w_in, m_gm_ln_g, m_gm_ln_b, m_gm_ws, m_gm_bs, m_gm_w_out, m_final_g, v_c_ctx, v_ada_w, v_ada_b, v_norm_g, v_mlp_w1, v_mlp_w2, v_pool_w, v_pool_scale, v_attn_w_qkv, v_attn_w_o, v_attn_q_g, v_attn_k_g, v_gm_w_in, v_gm_ln_g, v_gm_ln_b, v_gm_ws, v_gm_bs, v_gm_w_out, v_final_g):
    seq, d = x.shape[1], x.shape[2]
    n_ctx = ctx.shape[1]
    total = n_ctx + seq
    hd = attn_q_g.shape[-1]
    nh = d // hd
    nkv = nh // 2
    gg, ch = gm_ws.shape[1], gm_ws.shape[-1]
    half = gm_w_out.shape[1] * N_CHIPS
    pgw = pool_w.shape[-1]
    tm = min(256, n_ctx)
    nct = n_ctx // tm
    seg_lens = (n_ctx, seq)

    mx, my, mc = _my_place()
    chip = 2 * mx + my
    me = 4 * mx + 2 * my + mc

    c_rows = jnp.concatenate([c, jnp.zeros((7, d), F32)], axis=0)
    c_gath = _all_gather_small(c_rows, "gather_cond")[:, 0, :]
    c_all = jnp.concatenate([c_gath, c_ctx[None, :], jnp.zeros((7, d), F32)], axis=0)
    ncs = ada_w.shape[-1]
    ada_cols = _ada_fwd(c_all, ada_w, _chip_cols(ada_b, chip, ncs))
    small_shapes = [ada_cols.shape, norm_g.shape, pool_scale.shape, gm_ln_g.shape, gm_ln_b.shape]
    gathered = _all_gather_small(_pack([ada_cols, norm_g, pool_scale, gm_ln_g, gm_ln_b]), "gather_small_params")
    per_chip = _across_chips(gathered, gathered.shape[1:])
    ada_g, ng_g, ps_g, lng_g, lnb_g = _unpack(per_chip, small_shapes)

    def join_last(a):
        return jnp.moveaxis(a, 0, -2).reshape(*a.shape[1:-1], N_CHIPS * a.shape[-1])

    ada_full = join_last(ada_g)
    ng_full = join_last(ng_g)
    ps_full = join_last(ps_g)
    lng_full = join_last(lng_g)
    lnb_full = join_last(lnb_g)
    mod_lat = lax.dynamic_slice_in_dim(ada_full, me, 1, axis=1).reshape(DEPTH, 6, d)
    mod_ctx = ada_full[:, 8].reshape(DEPTH, 6, d)
    mods = jnp.stack([jnp.concatenate([mod_ctx, ng_full], axis=1), jnp.concatenate([mod_lat, ng_full], axis=1)],
                     axis=1)

    weight_groups = [
        [mlp_w1[0], mlp_w2[0], pool_w],
        [mlp_w1[1], mlp_w2[1], attn_w_qkv[0], attn_w_o[0]],
        [mlp_w1[2], mlp_w2[2], gm_w_in[0], gm_w_out[0], mlp_w1[3], mlp_w2[3]],
    ]
    gathers = []
    for gi, ws_f32 in enumerate(weight_groups):
        shards, _ = lax.optimization_barrier(([w.astype(BF16) for w in ws_f32], mods))
        lands = [_landing_for_gather(s, chip) for s in shards]
        gathers.append(_exchange_start(shards, lands, _gather_views, f"gather_weights_{gi}_start"))

    def gathered(gi, after):
        send, recv, srcs, lands, _ = gathers[gi]
        return _exchange_wait(send, recv, srcs, lands, _gather_views, after, f"gather_weights_{gi}_wait")[1]

    def rows_joined(a):
        return a.reshape(-1, a.shape[-1])

    w1_b, w2_b = [None] * DEPTH, [None] * DEPTH
    w1_b[0], w2_b[0], pw_land = gathered(0, ps_full)
    pw_f = jnp.transpose(pw_land, (1, 2, 0, 3, 4)).reshape(pool_w.shape[0], pool_w.shape[1], pgw, pgw)

    gains = jnp.concatenate([attn_q_g, attn_k_g, jnp.zeros((6, hd), F32)], axis=0)
    ws_b = gm_ws[0].astype(BF16)
    ws_t = jnp.swapaxes(gm_ws[0], 1, 2).astype(BF16)
    bs_col = gm_bs[0][:, :, None]
    cos, sin = _rope_tables(n_ctx, seq, hd)
    lat = lambda i: mods[i, 1:2]

    hc0 = jnp.concatenate([ctx[0], x[0]], axis=0)
    ha0 = _pool_fwd(hc0, mods[0], pw_f, ps_full, 0, nct=nct, tm=tm, seg_lens=seg_lens)
    hc1, u0, o0 = _mlp_fwd(ha0, mods[0], w1_b[0], w2_b[0], 0, nct=nct, tm=tm)
    w1_b[1], w2_b[1], wqkv_b, wo_land = gathered(1, hc1)
    wo_f = rows_joined(wo_land)
    xa1 = _normmod_call(hc1, mods[1], 0, "attn_in_fwd", nct=nct, tm=tm)
    qkv, q_r, k_r, v_b = _qkv_fwd(xa1, wqkv_b, cos, sin, gains, nh=nh, nkv=nkv, nct=nct, tm=tm)
    o_att, lse = _flash_fwd(q_r, k_r, v_b, n_ctx=n_ctx, hd=hd)
    ha1, y1 = _proj_fwd(o_att, wo_f, hc1, mods[1], n_ctx=n_ctx, tm=tm)
    h2, u1, o1 = _mlp_fwd(ha1, lat(1), w1_b[1], w2_b[1], 1, nct=0, tm=tm)
    w1_b[2], w2_b[2], win_b, wout_land, w1_b[3], w2_b[3] = gathered(2, h2)
    wout_f = rows_joined(wout_land)
    ha2, zpre, y2 = _gmlp_fwd(h2, mods[2], win_b, lng_full, lnb_full, ws_b, bs_col, wout_f, tm=tm)
    h3, u2, o2 = _mlp_fwd(ha2, lat(2), w1_b[2], w2_b[2], 2, nct=0, tm=tm)
    ha3 = _pool_fwd(h3, lat(3), pw_f, ps_full, 3, nct=0, tm=tm, seg_lens=seg_lens)
    h4, u3, o3 = _mlp_fwd(ha3, lat(3), w1_b[3], w2_b[3], 3, nct=0, tm=tm)
    dh4, fin_acc = _final_loss(h4, loss_target[0], final_g[None, :], tm=tm)

    dmods = [None] * DEPTH
    scatters = [None] * DEPTH

    def blocked_rows(g):
        return g.reshape(N_CHIPS, g.shape[1] // N_CHIPS, g.shape[2])

    def blocked_pool(dpw):
        pg = dpw.shape[0]
        return jnp.transpose(dpw.astype(BF16).reshape(pg, N_CHIPS, pgw // N_CHIPS, pgw), (1, 0, 2, 3))

    def scatter_start(i, grads):
        lands = [lax.empty((3, *g.shape[1:]), g.dtype) for g in grads]
        scatters[i] = _exchange_start(grads, lands, _scatter_views, f"scatter_grads_{i}_start")
        return scatters[i][4][0:1, 0:1]

    def mlp_back(i, h_in, dh_out, u, o, md, n_ct):
        dh_in, du, dob, mb, dmd = _mlp_bwd(h_in, dh_out, u, o, md, w1_b[i], w2_b[i], i, nct=n_ct, tm=tm)
        dw1 = _mm_tn(mb, du, f"mlp_dw1_{i}", col_blocks=N_CHIPS)
        dw2 = blocked_rows(_mm_tn(u, dob, f"mlp_dw2_{i}", relu2=True))
        return dh_in, dmd, [dw1, dw2]

    def pool_back(i, h_in, dh_out, md, n_ct):
        dp, dmd_a, dps, dpw = _pool_bwd_weights(h_in, dh_out, md, pw_f, ps_full, i, nct=n_ct, tm=tm,
                                                seg_lens=seg_lens)
        dh_in, dmd_b = _pool_bwd_input(dp, h_in, dh_out, md, i, nct=n_ct, tm=tm, seg_lens=seg_lens, gw=pgw)
        return dh_in, dmd_a + dmd_b, dps, dpw

    zero_grp = jnp.zeros((1, 8, d), F32)
    dha3, dmd3, dws3 = mlp_back(3, ha3, dh4, u3, o3, lat(3), 0)
    dh3, dmd3p, dps3, dpw3 = pool_back(3, h3, dha3, lat(3), 0)
    dmods[3] = jnp.concatenate([zero_grp, dmd3 + dmd3p], axis=0)
    tok = scatter_start(3, dws3 + [blocked_pool(dpw3)])
    dha2, dmd2, dws2 = mlp_back(2, ha2, dh3, u2, o2, lat(2) + tok, 0)
    dh2, dzpre, gated, dyb2, ab2, dmd2g, dln, dws, dbs = _gmlp_bwd(
        h2, dha2, zpre, y2, mods[2], win_b, lng_full, lnb_full, ws_b, ws_t, bs_col, wout_f, tm=tm)
    dwin = _mm_tn(ab2, dzpre, "gmlp_dw_in", col_blocks=N_CHIPS)
    dwout = blocked_rows(_mm_tn(gated, dyb2, "gmlp_dw_out"))
    dmods[2] = jnp.concatenate([zero_grp, dmd2 + dmd2g], axis=0)
    tok = scatter_start(2, dws2 + [dwin, dwout])
    dha1, dmd1, dws1 = mlp_back(1, ha1, dh2, u1, o1, lat(1) + tok, 0)
    do_att, dyb1, dmd1p = _proj_bwd(dha1, y1, mods[1], wo_f, tm=tm)
    dwo = blocked_rows(_mm_tn(o_att, dyb1, "attn_dw_o"))
    dq, dk, dv = _flash_bwd(q_r, k_r, v_b, o_att, do_att, lse, n_ctx=n_ctx, hd=hd)
    dqkv, dgains = _qkv_bwd(qkv, dq, dk, dv, cos, sin, gains, nh=nh, nkv=nkv, nct=nct, tm=tm)
    dwqkv = _mm_tn(xa1, dqkv, "attn_dw_qkv", col_blocks=N_CHIPS)
    dhc1, dmd1i = _attn_in_bwd(dqkv, wqkv_b, hc1, dha1, mods[1], nct=nct, tm=tm)
    dmods[1] = dmd1i + jnp.concatenate([zero_grp, dmd1 + dmd1p], axis=0)
    tok = scatter_start(1, dws1 + [dwqkv, dwo])
    dha0, dmd0, dws0 = mlp_back(0, ha0, dhc1, u0, o0, mods[0] + tok, nct)
    dhc0, dmd0p, dps0, dpw0 = pool_back(0, hc0, dha0, mods[0], nct)
    dmods[0] = dmd0 + dmd0p
    grad_x = dhc0[n_ctx:][None]

    dmods_all = jnp.stack(dmods, axis=0)
    small_grads = [dmods_all, dws, dbs, dgains, dln, dps0, dps3, fin_acc]
    sg_shapes = [a.shape for a in small_grads]
    sg_gath = _all_gather_small(_pack(small_grads), "gather_small_grads")
    sg_sum = _sum_devices(sg_gath, "sum_small_grads")
    s_dmods, s_dws, s_dbs, s_dgains, s_dln, s_dps0, s_dps3, s_fin = _unpack(sg_sum, sg_shapes)
    loss = s_fin[1, 0]

    dm_dev = _unpack(sg_gath, sg_shapes[:1])[0]
    dm_lat = jnp.moveaxis(dm_dev[:, :, 1, :6, :], 0, 1).reshape(DEPTH, N_DEV, 6 * d)
    dm_ctx = jnp.moveaxis(dm_dev[:, :, 0, :6, :], 0, 1).reshape(DEPTH, N_DEV, 6 * d)
    dmod16 = _chip_cols(jnp.concatenate([dm_lat, dm_ctx], axis=1), chip, ncs)
    g_ada_w, dcc_part = _ada_bwd(c_all, c_all.T, dmod16, ada_w)
    dcc_gath = _all_gather_small(dcc_part, "gather_d_c_ctx")
    dcc_chips = _across_chips(dcc_gath, dcc_gath.shape[1:])
    dcc = _sum_devices(dcc_chips, "sum_d_c_ctx")[0]

    grads0, _ = lax.optimization_barrier((dws0 + [blocked_pool(dpw0)], dcc))
    after_last_start = sg_sum + scatter_start(0, grads0)
    own, landed = [None] * DEPTH, [None] * DEPTH
    for i in (3, 2, 1, 0):
        send, recv, srcs, lands, _ = scatters[i]
        srcs, landed[i] = _exchange_wait(send, recv, srcs, lands, _scatter_views, after_last_start,
                                         f"scatter_grads_{i}_wait")
        own[i] = [lax.dynamic_index_in_dim(s, chip, 0, keepdims=False) for s in srcs]

    def summed(name, picks):
        return _sum_partials([own[i][j] for i, j in picks], [landed[i][j] for i, j in picks], f"sum_chips_{name}")

    big = [("mlp_w1", mlp_w1, m_mlp_w1, v_mlp_w1, [(i, 0) for i in range(DEPTH)]),
           ("mlp_w2", mlp_w2, m_mlp_w2, v_mlp_w2, [(i, 1) for i in range(DEPTH)]),
           ("pool_w", pool_w, m_pool_w, v_pool_w, [(0, 2), (3, 2)]),
           ("attn_w_qkv", attn_w_qkv, m_attn_w_qkv, v_attn_w_qkv, [(1, 2)]),
           ("attn_w_o", attn_w_o, m_attn_w_o, v_attn_w_o, [(1, 3)]),
           ("gm_w_in", gm_w_in, m_gm_w_in, v_gm_w_in, [(2, 2)]),
           ("gm_w_out", gm_w_out, m_gm_w_out, v_gm_w_out, [(2, 3)])]
    partial = [summed(name, picks) for name, _, _, _, picks in big]
    from_sibling = _swap_with_sibling(partial)
    big_out = {}
    for (name, w, m, v, _), mine, theirs in zip(big, partial, from_sibling):
        cols = w.shape[-1]
        res = _adamw(mine, theirs, w.reshape(-1, cols), m.reshape(-1, cols), v.reshape(-1, cols), f"adamw_{name}")
        big_out[name] = [r.reshape(w.shape) for r in res]
    ada_res = _adamw(g_ada_w.reshape(-1, ncs), jnp.zeros((DEPTH * d, ncs), F32), ada_w.reshape(-1, ncs),
                     m_ada_w.reshape(-1, ncs), v_ada_w.reshape(-1, ncs), "adamw_ada_w")
    big_out["ada_w"] = [r.reshape(ada_w.shape) for r in ada_res]

    def cols_of(a, width):
        return _chip_cols(a, chip, width)

    zero = lambda a: jnp.zeros(a.shape, F32)
    ngw = norm_g.shape[-1]
    small = {
        "c_ctx": (dcc, zero(dcc), c_ctx, m_c_ctx, v_c_ctx),
        "ada_b": (s_dmods[:, 0, :6].reshape(DEPTH, 6 * d), s_dmods[:, 1, :6].reshape(DEPTH, 6 * d), ada_b, m_ada_b,
                  v_ada_b),
        "norm_g": (cols_of(s_dmods[:, 0, 6:8], ngw), cols_of(s_dmods[:, 1, 6:8], ngw), norm_g, m_norm_g, v_norm_g),
        "pool_scale": (cols_of(jnp.stack([s_dps0[0], s_dps3[0]]), pool_scale.shape[-1]), zero(pool_scale),
                       pool_scale, m_pool_scale, v_pool_scale),
        "attn_q_g": (s_dgains[0:1], zero(attn_q_g), attn_q_g, m_attn_q_g, v_attn_q_g),
        "attn_k_g": (s_dgains[1:2], zero(attn_k_g), attn_k_g, m_attn_k_g, v_attn_k_g),
        "gm_ln_g": (cols_of(s_dln[0:1], gm_ln_g.shape[-1]), zero(gm_ln_g), gm_ln_g, m_gm_ln_g, v_gm_ln_g),
        "gm_ln_b": (cols_of(s_dln[1:2], gm_ln_b.shape[-1]), zero(gm_ln_b), gm_ln_b, m_gm_ln_b, v_gm_ln_b),
        "gm_ws": (s_dws[None], zero(gm_ws), gm_ws, m_gm_ws, v_gm_ws),
        "gm_bs": (s_dbs[None, :, :, 0], zero(gm_bs), gm_bs, m_gm_bs, v_gm_bs),
        "final_g": (s_fin[0], zero(final_g), final_g, m_final_g, v_final_g),
    }
    keys = list(small)
    packed = [_pack([small[k][t] for k in keys]) for t in range(5)]
    res = _adamw(*packed, "adamw_small")
    shapes = [small[k][2].shape for k in keys]
    small_out = {k: [] for k in keys}
    for r in res:
        for k, a in zip(keys, _unpack(r, shapes)):
            small_out[k].append(a)

    order = ["c_ctx", "ada_w", "ada_b", "norm_g", "mlp_w1", "mlp_w2", "pool_w", "pool_scale", "attn_w_qkv",
             "attn_w_o", "attn_q_g", "attn_k_g", "gm_w_in", "gm_ln_g", "gm_ln_b", "gm_ws", "gm_bs", "gm_w_out",
             "final_g"]
    allo = {**big_out, **small_out}
    outs = [loss, grad_x]
    for t in range(4):
        outs += [allo[k][t] for k in order]
    return tuple(outs)
```

```python
import functools
import math

import jax
import jax.numpy as jnp
from jax import lax
from jax.experimental import pallas as pl
from jax.experimental.pallas import tpu as pltpu

F32 = jnp.float32
BF16 = jnp.bfloat16
MESH = pl.DeviceIdType.MESH

EPS = 1e-6
GRID_W = 64
ROPE_BASE = 10000.0
POOL_WINDOWS = (2, 4, 8, 16)
HALO = 8
DEPTH = 4
N_MIXERS = 3

ADAM_LR = 0.001
ADAM_B1 = 0.9
ADAM_B2 = 0.999
ADAM_EPS = 1e-08
ADAM_WD = 0.01
ADAM_STEP = 10

VMEM_LIMIT_BYTES = 56 * 1024 * 1024
LANES = 128
SUBLANES = 8
N_DEV = 8
N_CHIPS = 4

SH1, SC1, G1, SH2, SC2, G2, NG0, NG1 = range(8)


def _dot(a, b):
    return jnp.dot(a, b, preferred_element_type=F32)


def _dot_nt(a, b):
    return lax.dot_general(a, b, (((1,), (1,)), ((), ())), preferred_element_type=F32)


def _dot_tn(a, b):
    return lax.dot_general(a, b, (((0,), (0,)), ((), ())), preferred_element_type=F32)


def _dot_blocks(a, w_ref):
    return jnp.concatenate([_dot(a, w_ref[k]) for k in range(w_ref.shape[0])], axis=1)


def _dot_nt_blocks(a, w_ref):
    nb, _, w = w_ref.shape
    acc = _dot_nt(a[:, 0:w], w_ref[0])
    for k in range(1, nb):
        acc = acc + _dot_nt(a[:, k * w:(k + 1) * w], w_ref[k])
    return acc


def _params(**kw):
    return pltpu.CompilerParams(vmem_limit_bytes=VMEM_LIMIT_BYTES, **kw)


def _full(shape):
    nd = len(shape)
    return pl.BlockSpec(shape, lambda *_: (0,) * nd)


def _rows(tm, width):
    return pl.BlockSpec((tm, width), lambda i: (i, 0))


def _group_of(nct, groups):
    if groups == 1:
        return lambda i: 0
    return lambda i: jnp.where(i >= nct, 1, 0)


def _mods_spec(nct, groups, d):
    grp = _group_of(nct, groups)
    return pl.BlockSpec((None, 8, d), lambda i: (grp(i), 0, 0))


def _first_of_group(i, nct, groups):
    if groups == 1:
        return i == 0
    return jnp.logical_or(i == 0, i == nct)


def _rowsum(v):
    return jnp.sum(v, axis=0, keepdims=True)


def _rms_parts(x):
    r = lax.rsqrt(jnp.mean(x * x, axis=-1, keepdims=True) + EPS)
    return x * r, r


def _normmod(x, md, which):
    ng, sh, sc = (md[NG0:NG0 + 1], md[SH1:SH1 + 1], md[SC1:SC1 + 1]) if which == 0 else (
        md[NG1:NG1 + 1], md[SH2:SH2 + 1], md[SC2:SC2 + 1])
    xhat, r = _rms_parts(x)
    n = xhat * ng
    return n * (1.0 + sc) + sh, (xhat, r, n)


def _normmod_bwd(da, parts, md, which):
    xhat, r, n = parts
    ng, sc = (md[NG0:NG0 + 1], md[SC1:SC1 + 1]) if which == 0 else (md[NG1:NG1 + 1], md[SC2:SC2 + 1])
    dsh = _rowsum(da)
    dsc = _rowsum(da * n)
    dn = da * (1.0 + sc)
    dng = _rowsum(dn * xhat)
    dxhat = dn * ng
    dx = r * (dxhat - xhat * jnp.mean(dxhat * xhat, axis=-1, keepdims=True))
    return dx, dsh, dsc, dng


def _acc_rows(ref, first, rows):
    @pl.when(first)
    def _():
        ref[...] = jnp.zeros(ref.shape, ref.dtype)

    for r, v in rows.items():
        ref[r:r + 1, :] += v


def _shift_up(x, k):
    if k == 0:
        return x
    return pltpu.roll(x, x.shape[0] - k, axis=0)


def _gelu(x):
    k = math.sqrt(2.0 / math.pi)
    return 0.5 * x * (1.0 + jnp.tanh(k * (x + 0.044715 * x * x * x)))


def _gelu_grad(x):
    k = math.sqrt(2.0 / math.pi)
    t = jnp.tanh(k * (x + 0.044715 * x * x * x))
    return 0.5 * (1.0 + t) + 0.5 * x * (1.0 - t * t) * k * (1.0 + 3.0 * 0.044715 * x * x)


def _silu(x):
    return x / (1.0 + jnp.exp(-x))


def _silu_grad(x):
    s = 1.0 / (1.0 + jnp.exp(-x))
    return s * (1.0 + x * (1.0 - s))


def _mlp_fwd(h, mods, w1, w2, layer, *, nct, tm):
    rows, d = h.shape
    groups = mods.shape[0]
    nb, _, fc = w1.shape
    ff = nb * fc

    def body(h_ref, md_ref, w1_ref, w2_ref, h2_ref, u_ref, o_ref):
        x = h_ref[...]
        md = md_ref[...]
        m, _ = _normmod(x, md, 1)
        mb = m.astype(BF16)
        acc = jnp.zeros((tm, d), F32)
        for k in range(nb):
            u = _dot(mb, w1_ref[k])
            u_ref[:, k * fc:(k + 1) * fc] = u.astype(BF16)
            acc = acc + _dot(jnp.square(jnp.maximum(u, 0.0)).astype(BF16), w2_ref[k])
        o_ref[...] = acc.astype(BF16)
        h2_ref[...] = x + md[G2:G2 + 1] * acc

    return pl.pallas_call(
        body, name=f"mlp_fwd_{layer}", grid=(rows // tm,),
        in_specs=[_rows(tm, d), _mods_spec(nct, groups, d), _full(w1.shape), _full(w2.shape)],
        out_specs=[_rows(tm, d), _rows(tm, ff), _rows(tm, d)],
        out_shape=[jax.ShapeDtypeStruct((rows, d), F32), jax.ShapeDtypeStruct((rows, ff), BF16),
                   jax.ShapeDtypeStruct((rows, d), BF16)],
        compiler_params=_params(),
    )(h, mods, w1, w2)


def _mlp_bwd(h1, dh2, u, o, mods, w1, w2, layer, *, nct, tm):
    rows, d = h1.shape
    groups = mods.shape[0]
    nb, _, fc = w1.shape
    ff = nb * fc

    def body(h_ref, g_ref, u_ref, o_ref, md_ref, w1_ref, w2_ref, dh_ref, du_ref, dob_ref, mb_ref, dmd_ref):
        i = pl.program_id(0)
        x = h_ref[...]
        g = g_ref[...]
        md = md_ref[...]
        m, parts = _normmod(x, md, 1)
        mb_ref[...] = m.astype(BF16)
        dg2 = _rowsum(g * o_ref[...].astype(F32))
        dob = (g * md[G2:G2 + 1]).astype(BF16)
        dob_ref[...] = dob
        dm = jnp.zeros((tm, d), F32)
        for k in range(nb):
            uk = u_ref[:, k * fc:(k + 1) * fc].astype(F32)
            dr = _dot_nt(dob, w2_ref[k])
            duk = (dr * (2.0 * jnp.maximum(uk, 0.0))).astype(BF16)
            du_ref[:, k * fc:(k + 1) * fc] = duk
            dm = dm + _dot_nt(duk, w1_ref[k])
        dx, dsh, dsc, dng = _normmod_bwd(dm, parts, md, 1)
        dh_ref[...] = g + dx
        _acc_rows(dmd_ref, _first_of_group(i, nct, groups), {SH2: dsh, SC2: dsc, G2: dg2, NG1: dng})

    return pl.pallas_call(
        body, name=f"mlp_bwd_{layer}", grid=(rows // tm,),
        in_specs=[_rows(tm, d), _rows(tm, d), _rows(tm, ff), _rows(tm, d), _mods_spec(nct, groups, d),
                  _full(w1.shape), _full(w2.shape)],
        out_specs=[_rows(tm, d), _rows(tm, ff), _rows(tm, d), _rows(tm, d), _mods_spec(nct, groups, d)],
        out_shape=[jax.ShapeDtypeStruct((rows, d), F32), jax.ShapeDtypeStruct((rows, ff), BF16),
                   jax.ShapeDtypeStruct((rows, d), BF16), jax.ShapeDtypeStruct((rows, d), BF16),
                   jax.ShapeDtypeStruct((groups, 8, d), F32)],
        compiler_params=_params(),
    )(h1, dh2, u, o, mods, w1, w2)


def _div_tile(n, cap):
    if n <= cap:
        return n
    return max(t for t in range(LANES, cap + 1, LANES) if n % t == 0)


def _mm_tn(a, b, name, *, relu2=False, col_blocks=1):
    rows, m = a.shape
    n = b.shape[1]
    tmm = min(m, 1024)
    tn = min(n // col_blocks, 2048)
    per_block = n // col_blocks // tn
    tr = _div_tile(rows, 1024)

    def body(a_ref, b_ref, o_ref, acc_ref):
        r = pl.program_id(2)

        @pl.when(r == 0)
        def _():
            acc_ref[...] = jnp.zeros(acc_ref.shape, F32)

        av = a_ref[...]
        if relu2:
            av = jnp.square(jnp.maximum(av.astype(F32), 0.0)).astype(BF16)
        acc_ref[...] += _dot_tn(av, b_ref[...])

        @pl.when(r == pl.num_programs(2) - 1)
        def _():
            o_ref[...] = acc_ref[...].astype(BF16)

    return pl.pallas_call(
        body, name=name, grid=(m // tmm, n // tn, rows // tr),
        in_specs=[pl.BlockSpec((tr, tmm), lambda i, j, r: (r, i)), pl.BlockSpec((tr, tn), lambda i, j, r: (r, j))],
        out_specs=pl.BlockSpec((None, tmm, tn), lambda i, j, r: (j // per_block, i, j % per_block)),
        out_shape=jax.ShapeDtypeStruct((col_blocks, m, n // col_blocks), BF16),
        scratch_shapes=[pltpu.VMEM((tmm, tn), F32)],
        compiler_params=_params(),
    )(a, b)


def _halo_specs(tm, d, rows):
    per = tm // HALO
    prev = pl.BlockSpec((HALO, d), lambda i: (jnp.maximum(i * per - 1, 0), 0))
    nxt = pl.BlockSpec((HALO, d), lambda i: (jnp.minimum((i + 1) * per, rows // HALO - 1), 0))
    return prev, _rows(tm, d), nxt


def _segment_positions(i, tm, nct, groups, seg_lens):
    if groups == 1:
        start, length = 0, seg_lens[-1]
    else:
        start = jnp.where(i >= nct, nct, 0)
        length = jnp.where(i >= nct, seg_lens[1], seg_lens[0])
    rid = lax.broadcasted_iota(jnp.int32, (tm + 2 * HALO, 1), 0)
    pos = (i - start) * tm - HALO + rid
    return pos, length


def _window_count(pos, length, w):
    hi = jnp.minimum(pos + (w - w // 2), length)
    lo = jnp.maximum(pos - w // 2, 0)
    return (hi - lo).astype(F32)


def _window_sum(xg, w, lead):
    b, k = xg, 1
    while k < w:
        b = b + _shift_up(b, k)
        k *= 2
    return _shift_up(b, HALO - lead)[0:xg.shape[0] - 2 * HALO]


def _pooled(ext, md, pos, length, gw):
    tm = ext.shape[0] - 2 * HALO
    a_ext, parts = _normmod(ext, md, 0)
    valid = jnp.logical_and(pos >= 0, pos < length)
    a_ext = jnp.where(valid, a_ext, 0.0)
    pos_c = pos[HALO:HALO + tm]
    ps = []
    for g, w in enumerate(POOL_WINDOWS):
        xg = a_ext[:, g * gw:(g + 1) * gw]
        s = _window_sum(xg, w, w // 2)
        ps.append(s / _window_count(pos_c, length, w) - xg[HALO:HALO + tm])
    return ps, parts


def _pool_fwd(h, mods, pw, pscale, layer, *, nct, tm, seg_lens):
    rows, d = h.shape
    groups = mods.shape[0]
    pg, gw = pw.shape[1], pw.shape[-1]

    def body(prev_ref, cur_ref, next_ref, md_ref, pw_ref, ps_ref, out_ref):
        i = pl.program_id(0)
        md = md_ref[...]
        cur = cur_ref[...]
        ext = jnp.concatenate([prev_ref[...], cur, next_ref[...]], axis=0)
        pos, length = _segment_positions(i, tm, nct, groups, seg_lens)
        ps, _ = _pooled(ext, md, pos, length, gw)
        for g in range(pg):
            yg = _dot(ps[g].astype(BF16), pw_ref[g]) * ps_ref[:, g * gw:(g + 1) * gw]
            out_ref[:, g * gw:(g + 1) * gw] = cur[:, g * gw:(g + 1) * gw] + md[G1:G1 + 1, g * gw:(g + 1) * gw] * yg

    j = layer // N_MIXERS
    return pl.pallas_call(
        body, name=f"pool_fwd_{layer}", grid=(rows // tm,),
        in_specs=[*_halo_specs(tm, d, rows), _mods_spec(nct, groups, d),
                  pl.BlockSpec((None, pg, gw, gw), lambda i: (j, 0, 0, 0)), _full((1, d))],
        out_specs=_rows(tm, d),
        out_shape=jax.ShapeDtypeStruct((rows, d), F32),
        compiler_params=_params(),
    )(h, h, h, mods, pw, pscale[j:j + 1])


def _pool_bwd_weights(h, dh1, mods, pw, pscale, layer, *, nct, tm, seg_lens):
    rows, d = h.shape
    groups = mods.shape[0]
    pg, gw = pw.shape[1], pw.shape[-1]

    def body(prev_ref, cur_ref, next_ref, g_ref, md_ref, pw_ref, ps_ref, dp_ref, dmd_ref, dps_ref, dpw_ref):
        i = pl.program_id(0)
        md = md_ref[...]
        ext = jnp.concatenate([prev_ref[...], cur_ref[...], next_ref[...]], axis=0)
        pos, length = _segment_positions(i, tm, nct, groups, seg_lens)
        ps, _ = _pooled(ext, md, pos, length, gw)
        gup = g_ref[...]

        @pl.when(i == 0)
        def _():
            dps_ref[...] = jnp.zeros(dps_ref.shape, F32)
            dpw_ref[...] = jnp.zeros(dpw_ref.shape, F32)

        dg1 = []
        for g in range(pg):
            cols = slice(g * gw, (g + 1) * gw)
            pb = ps[g].astype(BF16)
            yp = _dot(pb, pw_ref[g])
            sc = ps_ref[:, cols]
            dg1.append(_rowsum(gup[:, cols] * (yp * sc)))
            dy = gup[:, cols] * md[G1:G1 + 1, cols]
            dps_ref[0:1, cols] += _rowsum(dy * yp)
            dyp = (dy * sc).astype(BF16)
            dp_ref[:, cols] = _dot_nt(dyp, pw_ref[g])
            dpw_ref[g] += _dot_tn(pb, dyp)
        _acc_rows(dmd_ref, _first_of_group(i, nct, groups), {G1: jnp.concatenate(dg1, axis=1)})

    j = layer // N_MIXERS
    return pl.pallas_call(
        body, name=f"pool_bwd_w_{layer}", grid=(rows // tm,),
        in_specs=[*_halo_specs(tm, d, rows), _rows(tm, d), _mods_spec(nct, groups, d),
                  pl.BlockSpec((None, pg, gw, gw), lambda i: (j, 0, 0, 0)), _full((1, d))],
        out_specs=[_rows(tm, d), _mods_spec(nct, groups, d), _full((8, d)), _full((pg, gw, gw))],
        out_shape=[jax.ShapeDtypeStruct((rows, d), F32), jax.ShapeDtypeStruct((groups, 8, d), F32),
                   jax.ShapeDtypeStruct((8, d), F32), jax.ShapeDtypeStruct((pg, gw, gw), F32)],
        compiler_params=_params(),
    )(h, h, h, dh1, mods, pw, pscale[j:j + 1])


def _pool_bwd_input(dp, h, dh1, mods, layer, *, nct, tm, seg_lens, gw):
    rows, d = h.shape
    groups = mods.shape[0]

    def body(prev_ref, cur_ref, next_ref, h_ref, g_ref, md_ref, dh_ref, dmd_ref):
        i = pl.program_id(0)
        md = md_ref[...]
        dp_cur = cur_ref[...]
        ext = jnp.concatenate([prev_ref[...], dp_cur, next_ref[...]], axis=0)
        pos, length = _segment_positions(i, tm, nct, groups, seg_lens)
        valid = jnp.logical_and(pos >= 0, pos < length)
        das = []
        for g, w in enumerate(POOL_WINDOWS):
            cols = slice(g * gw, (g + 1) * gw)
            q = jnp.where(valid, ext[:, cols] / jnp.maximum(_window_count(pos, length, w), 1.0), 0.0)
            das.append(_window_sum(q, w, w // 2 - 1) - dp_cur[:, cols])
        da = jnp.concatenate(das, axis=1)
        _, parts = _normmod(h_ref[...], md, 0)
        dx, dsh, dsc, dng = _normmod_bwd(da, parts, md, 0)
        dh_ref[...] = g_ref[...] + dx
        _acc_rows(dmd_ref, _first_of_group(i, nct, groups), {SH1: dsh, SC1: dsc, NG0: dng})

    return pl.pallas_call(
        body, name=f"pool_bwd_x_{layer}", grid=(rows // tm,),
        in_specs=[*_halo_specs(tm, d, rows), _rows(tm, d), _rows(tm, d), _mods_spec(nct, groups, d)],
        out_specs=[_rows(tm, d), _mods_spec(nct, groups, d)],
        out_shape=[jax.ShapeDtypeStruct((rows, d), F32), jax.ShapeDtypeStruct((groups, 8, d), F32)],
        compiler_params=_params(),
    )(dp, dp, dp, h, dh1, mods)


def _rope_tables(n_ctx, seq, hd):
    half = hd // 2
    t = jnp.arange(seq)
    row = (t // GRID_W).astype(F32)
    col = (t % GRID_W).astype(F32)
    inv = ROPE_BASE ** (-jnp.arange(0, half, 2, dtype=F32) / half)
    ar = row[:, None] * inv[None, :]
    ac = col[:, None] * inv[None, :]
    cos = jnp.concatenate([jnp.cos(ar), jnp.cos(ar), jnp.cos(ac), jnp.cos(ac)], axis=1)
    sin = jnp.concatenate([-jnp.sin(ar), jnp.sin(ar), -jnp.sin(ac), jnp.sin(ac)], axis=1)
    cos = jnp.concatenate([jnp.ones((n_ctx, hd), F32), cos], axis=0)
    sin = jnp.concatenate([jnp.zeros((n_ctx, hd), F32), sin], axis=0)
    return cos, sin


def _rope_partner(x):
    hd = x.shape[-1]
    q = hd // 4
    lane = lax.broadcasted_iota(jnp.int32, x.shape, 1)
    first = (lane % (2 * q)) < q
    return jnp.where(first, pltpu.roll(x, hd - q, axis=1), pltpu.roll(x, q, axis=1))


def _normmod_call(h, mods, which, name, *, nct, tm):
    rows, d = h.shape
    groups = mods.shape[0]

    def body(h_ref, md_ref, a_ref):
        a, _ = _normmod(h_ref[...], md_ref[...], which)
        a_ref[...] = a.astype(BF16)

    return pl.pallas_call(
        body, name=name, grid=(rows // tm,),
        in_specs=[_rows(tm, d), _mods_spec(nct, groups, d)],
        out_specs=_rows(tm, d), out_shape=jax.ShapeDtypeStruct((rows, d), BF16),
        compiler_params=_params(),
    )(h, mods)


def _qkv_fwd(xa, wqkv, cos, sin, gains, *, nh, nkv, nct, tm):
    rows, d = xa.shape
    qw = wqkv.shape[0] * wqkv.shape[-1]
    hd = cos.shape[-1]

    def body(x_ref, w_ref, cos_ref, sin_ref, gn_ref, qkv_ref, q_ref, k_ref, v_ref):
        qkv = _dot_blocks(x_ref[...], w_ref)
        qkv_ref[...] = qkv
        c, s = cos_ref[...], sin_ref[...]
        for hh in range(nh + nkv):
            xh = qkv[:, hh * hd:(hh + 1) * hd]
            xhat, _ = _rms_parts(xh)
            y = xhat * (gn_ref[0:1, :] if hh < nh else gn_ref[1:2, :])
            rot = (y * c + _rope_partner(y) * s).astype(BF16)
            if hh < nh:
                q_ref[:, hh * hd:(hh + 1) * hd] = rot
            else:
                k_ref[:, (hh - nh) * hd:(hh - nh + 1) * hd] = rot
        v_ref[...] = qkv[:, (nh + nkv) * hd:].astype(BF16)

    return pl.pallas_call(
        body, name="attn_qkv_fwd", grid=(rows // tm,),
        in_specs=[_rows(tm, d), _full(wqkv.shape), _rows(tm, hd), _rows(tm, hd), _full((8, hd))],
        out_specs=[_rows(tm, qw), pl.BlockSpec((tm, nh * hd), lambda i: (jnp.maximum(i - nct, 0), 0)),
                   _rows(tm, nkv * hd), _rows(tm, nkv * hd)],
        out_shape=[jax.ShapeDtypeStruct((rows, qw), F32), jax.ShapeDtypeStruct((rows - nct * tm, nh * hd), BF16),
                   jax.ShapeDtypeStruct((rows, nkv * hd), BF16), jax.ShapeDtypeStruct((rows, nkv * hd), BF16)],
        compiler_params=_params(),
    )(xa, wqkv, cos, sin, gains)


ATTN_Q_TILE_CAP = 1024
ATTN_KV_TILE_CAP = 4224
ATTN_ROW_GROUP = 256
LOG2E = 1.4426950408889634


def _attn_tiles(seq, total):
    tq = _div_tile(seq, ATTN_Q_TILE_CAP)
    return tq, _div_tile(total, ATTN_KV_TILE_CAP), min(ATTN_ROW_GROUP, tq)


def _flash_fwd(q, k, v, *, n_ctx, hd):
    total = k.shape[0]
    seq = total - n_ctx
    nkv = k.shape[1] // hd
    tq, tk, rg = _attn_tiles(seq, total)
    nk = total // tk
    scale = hd ** -0.5
    c2 = scale * LOG2E

    def body(q_ref, k_ref, v_ref, o_ref, lse_ref, m_sc, l_sc, acc_sc):
        ki = pl.program_id(2)

        @pl.when(ki == 0)
        def _():
            m_sc[...] = jnp.full(m_sc.shape, -jnp.inf, F32)
            l_sc[...] = jnp.zeros(l_sc.shape, F32)
            acc_sc[...] = jnp.zeros(acc_sc.shape, F32)

        kk, vv = k_ref[...], v_ref[...]
        for g in range(2):
            for sub in range(tq // rg):
                rows = slice(g * tq + sub * rg, g * tq + (sub + 1) * rg)
                s = _dot_nt(q_ref[sub * rg:(sub + 1) * rg, g * hd:(g + 1) * hd], kk)
                m_old = m_sc[rows]
                m_new = jnp.maximum(m_old, jnp.max(s, axis=-1, keepdims=True))
                alpha = jnp.exp2((m_old - m_new) * c2)
                p = jnp.exp2((s - m_new) * c2)
                l_sc[rows] = alpha * l_sc[rows] + jnp.sum(p, axis=-1, keepdims=True)
                acc_sc[rows] = alpha * acc_sc[rows] + _dot(p.astype(BF16), vv)
                m_sc[rows] = m_new

        @pl.when(ki == nk - 1)
        def _():
            o2 = acc_sc[...] / l_sc[...]
            lse = m_sc[...] * scale + jnp.log(l_sc[...])
            o_ref[:, :hd] = o2[:tq].astype(BF16)
            o_ref[:, hd:] = o2[tq:].astype(BF16)
            lse_ref[:, 0:1] = lse[:tq]
            lse_ref[:, 1:2] = lse[tq:]

    return pl.pallas_call(
        body, name="attn_flash_fwd", grid=(nkv, seq // tq, nk),
        in_specs=[pl.BlockSpec((tq, 2 * hd), lambda h, i, j: (i, h)),
                  pl.BlockSpec((tk, hd), lambda h, i, j: (j, h)),
                  pl.BlockSpec((tk, hd), lambda h, i, j: (j, h))],
        out_specs=[pl.BlockSpec((tq, 2 * hd), lambda h, i, j: (i, h)),
                   pl.BlockSpec((None, tq, 2), lambda h, i, j: (h, i, 0))],
        out_shape=[jax.ShapeDtypeStruct((seq, 2 * nkv * hd), BF16), jax.ShapeDtypeStruct((nkv, seq, 2), F32)],
        scratch_shapes=[pltpu.VMEM((2 * tq, 1), F32), pltpu.VMEM((2 * tq, 1), F32), pltpu.VMEM((2 * tq, hd), F32)],
        compiler_params=_params(),
    )(q, k, v)


def _flash_bwd(q, k, v, o, do, lse, *, n_ctx, hd):
    total = k.shape[0]
    seq = total - n_ctx
    nkv = k.shape[1] // hd
    tq, tk, rg = _attn_tiles(seq, total)
    scale = hd ** -0.5
    c2 = scale * LOG2E

    def body(q_ref, k_ref, v_ref, o_ref, do_ref, lse_ref, dq_ref, dk_ref, dv_ref):
        ki, qi = pl.program_id(1), pl.program_id(2)
        kk, vv = k_ref[...], v_ref[...]

        @pl.when(qi == 0)
        def _():
            dk_ref[...] = jnp.zeros(dk_ref.shape, F32)
            dv_ref[...] = jnp.zeros(dv_ref.shape, F32)

        dk_acc = jnp.zeros((tk, hd), F32)
        dv_acc = jnp.zeros((tk, hd), F32)
        for g in range(2):
            for sub in range(tq // rg):
                rs = slice(sub * rg, (sub + 1) * rg)
                cs = slice(g * hd, (g + 1) * hd)
                qq = q_ref[rs, cs]
                dd = do_ref[rs, cs]
                delta = jnp.sum(dd.astype(F32) * o_ref[rs, cs].astype(F32), axis=-1, keepdims=True)
                p = jnp.exp2(_dot_nt(qq, kk) * c2 - lse_ref[rs, g:g + 1] * LOG2E)
                dp = _dot_nt(dd, vv)
                ds = (p * (dp - delta) * scale).astype(BF16)
                dv_acc = dv_acc + _dot_tn(p.astype(BF16), dd)
                dk_acc = dk_acc + _dot_tn(ds, qq)
                dq = _dot(ds, kk)
                rows = pl.ds(pl.multiple_of(qi * tq, tq) + sub * rg, rg)

                @pl.when(ki == 0)
                def _():
                    dq_ref[rows, cs] = dq

                @pl.when(ki > 0)
                def _():
                    dq_ref[rows, cs] += dq
        dk_ref[...] += dk_acc
        dv_ref[...] += dv_acc

    return pl.pallas_call(
        body, name="attn_flash_bwd", grid=(nkv, total // tk, seq // tq),
        in_specs=[pl.BlockSpec((tq, 2 * hd), lambda h, j, i: (i, h)),
                  pl.BlockSpec((tk, hd), lambda h, j, i: (j, h)),
                  pl.BlockSpec((tk, hd), lambda h, j, i: (j, h)),
                  pl.BlockSpec((tq, 2 * hd), lambda h, j, i: (i, h)),
                  pl.BlockSpec((tq, 2 * hd), lambda h, j, i: (i, h)),
                  pl.BlockSpec((None, tq, 2), lambda h, j, i: (h, i, 0))],
        out_specs=[pl.BlockSpec((seq, 2 * hd), lambda h, j, i: (0, h)),
                   pl.BlockSpec((tk, hd), lambda h, j, i: (j, h)),
                   pl.BlockSpec((tk, hd), lambda h, j, i: (j, h))],
        out_shape=[jax.ShapeDtypeStruct((seq, 2 * nkv * hd), F32), jax.ShapeDtypeStruct((total, nkv * hd), F32),
                   jax.ShapeDtypeStruct((total, nkv * hd), F32)],
        compiler_params=_params(),
    )(q, k, v, o, do, lse)


def _proj_fwd(o, wo, hc, mods, *, n_ctx, tm):
    seq, d = o.shape
    off = n_ctx // tm

    def body(o_ref, w_ref, h_ref, md_ref, h1_ref, y_ref):
        y = _dot(o_ref[...], w_ref[...])
        y_ref[...] = y.astype(BF16)
        h1_ref[...] = h_ref[...] + md_ref[G1:G1 + 1, :] * y

    return pl.pallas_call(
        body, name="attn_proj_fwd", grid=(seq // tm,),
        in_specs=[_rows(tm, d), _full((d, d)),
                  pl.BlockSpec((tm, d), lambda i: (i + off, 0)), pl.BlockSpec((None, 8, d), lambda i: (1, 0, 0))],
        out_specs=[_rows(tm, d), _rows(tm, d)],
        out_shape=[jax.ShapeDtypeStruct((seq, d), F32), jax.ShapeDtypeStruct((seq, d), BF16)],
        compiler_params=_params(),
    )(o, wo, hc, mods)


def _proj_bwd(dh1, y, mods, wo, *, tm):
    seq, d = dh1.shape

    def body(g_ref, y_ref, md_ref, w_ref, do_ref, dyb_ref, dmd_ref):
        i = pl.program_id(0)
        g = g_ref[...]
        dyb = (g * md_ref[G1:G1 + 1, :]).astype(BF16)
        dyb_ref[...] = dyb
        do_ref[...] = _dot_nt(dyb, w_ref[...]).astype(BF16)
        _acc_rows(dmd_ref, i == 0, {G1: _rowsum(g * y_ref[...].astype(F32))})

    return pl.pallas_call(
        body, name="attn_proj_bwd", grid=(seq // tm,),
        in_specs=[_rows(tm, d), _rows(tm, d), pl.BlockSpec((None, 8, d), lambda i: (1, 0, 0)), _full((d, d))],
        out_specs=[_rows(tm, d), _rows(tm, d), pl.BlockSpec((None, 8, d), lambda i: (0, 0, 0))],
        out_shape=[jax.ShapeDtypeStruct((seq, d), BF16), jax.ShapeDtypeStruct((seq, d), BF16),
                   jax.ShapeDtypeStruct((1, 8, d), F32)],
        compiler_params=_params(),
    )(dh1, y, mods, wo)


def _qkv_bwd(qkv, dq, dk, dv, cos, sin, gains, *, nh, nkv, nct, tm):
    rows, qw = qkv.shape
    hd = cos.shape[-1]

    def body(qkv_ref, dq_ref, dk_ref, dv_ref, cos_ref, sin_ref, gn_ref, out_ref, dgn_ref):
        i = pl.program_id(0)
        c, s = cos_ref[...], sin_ref[...]
        is_lat = (i >= nct).astype(F32)
        dqg = jnp.zeros((1, hd), F32)
        dkg = jnp.zeros((1, hd), F32)
        for hh in range(nh + nkv):
            if hh < nh:
                dr = dq_ref[:, hh * hd:(hh + 1) * hd] * is_lat
                gn = gn_ref[0:1, :]
            else:
                dr = dk_ref[:, (hh - nh) * hd:(hh - nh + 1) * hd]
                gn = gn_ref[1:2, :]
            dy = dr * c + _rope_partner(dr * s)
            xhat, r = _rms_parts(qkv_ref[:, hh * hd:(hh + 1) * hd])
            dgh = _rowsum(dy * xhat)
            if hh < nh:
                dqg = dqg + dgh
            else:
                dkg = dkg + dgh
            dxhat = dy * gn
            dx = r * (dxhat - xhat * jnp.mean(dxhat * xhat, axis=-1, keepdims=True))
            out_ref[:, hh * hd:(hh + 1) * hd] = dx.astype(BF16)
        out_ref[:, (nh + nkv) * hd:] = dv_ref[...].astype(BF16)
        _acc_rows(dgn_ref, i == 0, {0: dqg, 1: dkg})

    return pl.pallas_call(
        body, name="attn_qkv_bwd", grid=(rows // tm,),
        in_specs=[_rows(tm, qw), pl.BlockSpec((tm, nh * hd), lambda i: (jnp.maximum(i - nct, 0), 0)),
                  _rows(tm, nkv * hd), _rows(tm, nkv * hd), _rows(tm, hd), _rows(tm, hd), _full((8, hd))],
        out_specs=[_rows(tm, qw), _full((8, hd))],
        out_shape=[jax.ShapeDtypeStruct((rows, qw), BF16), jax.ShapeDtypeStruct((8, hd), F32)],
        compiler_params=_params(),
    )(qkv, dq, dk, dv, cos, sin, gains)


def _attn_in_bwd(dqkv, wqkv, hc, dh1, mods, *, nct, tm):
    rows, d = hc.shape
    qw = dqkv.shape[1]

    def body(dz_ref, w_ref, h_ref, g_ref, md_ref, dh_ref, dmd_ref):
        i = pl.program_id(0)
        md = md_ref[...]
        da = _dot_nt_blocks(dz_ref[...], w_ref)
        _, parts = _normmod(h_ref[...], md, 0)
        dx, dsh, dsc, dng = _normmod_bwd(da, parts, md, 0)
        dh_ref[...] = g_ref[...] * (i >= nct).astype(F32) + dx
        _acc_rows(dmd_ref, _first_of_group(i, nct, 2), {SH1: dsh, SC1: dsc, NG0: dng})

    return pl.pallas_call(
        body, name="attn_in_bwd", grid=(rows // tm,),
        in_specs=[_rows(tm, qw), _full(wqkv.shape), _rows(tm, d),
                  pl.BlockSpec((tm, d), lambda i: (jnp.maximum(i - nct, 0), 0)), _mods_spec(nct, 2, d)],
        out_specs=[_rows(tm, d), _mods_spec(nct, 2, d)],
        out_shape=[jax.ShapeDtypeStruct((rows, d), F32), jax.ShapeDtypeStruct((2, 8, d), F32)],
        compiler_params=_params(),
    )(dqkv, wqkv, hc, dh1, mods)


def _gmlp_gate(zp, lng, lnb, ws_ref, bs_ref, gg, ch):
    half = zp.shape[1] // 2
    ggw = half // gg
    z = _gelu(zp)
    u, v = z[:, :half], z[:, half:]
    vc = v - jnp.mean(v, axis=-1, keepdims=True)
    rs = lax.rsqrt(jnp.mean(vc * vc, axis=-1, keepdims=True) + EPS)
    vhat = vc * rs
    vln = (vhat * lng + lnb).astype(BF16)
    chunks = []
    for n in range(zp.shape[0] // ch):
        groups = []
        for g in range(gg):
            groups.append(_dot(ws_ref[g], vln[n * ch:(n + 1) * ch, g * ggw:(g + 1) * ggw]) + bs_ref[g])
        chunks.append(jnp.concatenate(groups, axis=1))
    sv = jnp.concatenate(chunks, axis=0) if len(chunks) > 1 else chunks[0]
    return u, sv, vhat, rs, vln


def _gmlp_fwd(h, mods, w_in, lng, lnb, ws, bs, w_out, *, tm):
    seq, d = h.shape
    zw = w_in.shape[0] * w_in.shape[-1]
    half = zw // 2
    gg, ch = ws.shape[0], ws.shape[-1]

    def body(h_ref, md_ref, win_ref, lng_ref, lnb_ref, ws_ref, bs_ref, wout_ref, h1_ref, zp_ref, y_ref):
        x = h_ref[...]
        md = md_ref[...]
        a, _ = _normmod(x, md, 0)
        zp = _dot_blocks(a.astype(BF16), win_ref)
        zp_ref[...] = zp.astype(BF16)
        u, sv, _, _, _ = _gmlp_gate(zp, lng_ref[...], lnb_ref[...], ws_ref, bs_ref, gg, ch)
        y = _dot((u * sv).astype(BF16), wout_ref[...])
        y_ref[...] = y.astype(BF16)
        h1_ref[...] = x + md[G1:G1 + 1] * y

    return pl.pallas_call(
        body, name="gmlp_fwd", grid=(seq // tm,),
        in_specs=[_rows(tm, d), pl.BlockSpec((None, 8, d), lambda i: (1, 0, 0)),
                  _full(w_in.shape), _full((1, half)), _full((1, half)),
                  _full((gg, ch, ch)), _full((gg, ch, 1)), _full((half, d))],
        out_specs=[_rows(tm, d), _rows(tm, zw), _rows(tm, d)],
        out_shape=[jax.ShapeDtypeStruct((seq, d), F32), jax.ShapeDtypeStruct((seq, zw), BF16),
                   jax.ShapeDtypeStruct((seq, d), BF16)],
        compiler_params=_params(),
    )(h, mods, w_in, lng, lnb, ws, bs, w_out)


def _gmlp_bwd(h, dh1, zpre, y, mods, w_in, lng, lnb, ws, ws_t, bs, w_out, *, tm):
    seq, d = h.shape
    zw = w_in.shape[0] * w_in.shape[-1]
    half = zw // 2
    gg, ch = ws.shape[0], ws.shape[-1]
    ggw = half // gg

    def body(h_ref, g_ref, zp_ref, y_ref, md_ref, win_ref, lng_ref, lnb_ref, ws_ref, wst_ref, bs_ref, wout_ref,
             dh_ref, dzp_ref, gated_ref, dyb_ref, ab_ref, dmd_ref, dln_ref, dws_ref, dbs_ref):
        i = pl.program_id(0)
        x = h_ref[...]
        md = md_ref[...]
        a, parts = _normmod(x, md, 0)
        ab_ref[...] = a.astype(BF16)
        zp = zp_ref[...].astype(F32)
        lng_v = lng_ref[...]
        u, sv, vhat, rs, vln = _gmlp_gate(zp, lng_v, lnb_ref[...], ws_ref, bs_ref, gg, ch)
        g = g_ref[...]
        dg1 = _rowsum(g * y_ref[...].astype(F32))
        dyb = (g * md[G1:G1 + 1]).astype(BF16)
        dyb_ref[...] = dyb
        gated_ref[...] = (u * sv).astype(BF16)
        dgated = _dot_nt(dyb, wout_ref[...])
        du = dgated * sv
        dsv = dgated * u

        @pl.when(i == 0)
        def _():
            dws_ref[...] = jnp.zeros(dws_ref.shape, F32)
            dbs_ref[...] = jnp.zeros(dbs_ref.shape, F32)
            dln_ref[...] = jnp.zeros(dln_ref.shape, F32)

        chunks = []
        for n in range(tm // ch):
            groups = []
            for gi in range(gg):
                blk = dsv[n * ch:(n + 1) * ch, gi * ggw:(gi + 1) * ggw]
                dbs_ref[gi] += jnp.sum(blk, axis=-1, keepdims=True)
                blk_b = blk.astype(BF16)
                dws_ref[gi] += _dot_nt(blk_b, vln[n * ch:(n + 1) * ch, gi * ggw:(gi + 1) * ggw])
                groups.append(_dot(wst_ref[gi], blk_b))
            chunks.append(jnp.concatenate(groups, axis=1))
        dvln = jnp.concatenate(chunks, axis=0) if len(chunks) > 1 else chunks[0]
        dln_ref[0:1, :] += _rowsum(dvln * vhat)
        dln_ref[1:2, :] += _rowsum(dvln)
        dvhat = dvln * lng_v
        dv = rs * (dvhat - jnp.mean(dvhat, axis=-1, keepdims=True)
                   - vhat * jnp.mean(dvhat * vhat, axis=-1, keepdims=True))
        dzp = (jnp.concatenate([du, dv], axis=1) * _gelu_grad(zp)).astype(BF16)
        dzp_ref[...] = dzp
        da = _dot_nt_blocks(dzp, win_ref)
        dx, dsh, dsc, dng = _normmod_bwd(da, parts, md, 0)
        dh_ref[...] = g + dx
        _acc_rows(dmd_ref, i == 0, {SH1: dsh, SC1: dsc, G1: dg1, NG0: dng})

    return pl.pallas_call(
        body, name="gmlp_bwd", grid=(seq // tm,),
        in_specs=[_rows(tm, d), _rows(tm, d), _rows(tm, zw), _rows(tm, d),
                  pl.BlockSpec((None, 8, d), lambda i: (1, 0, 0)),
                  _full(w_in.shape), _full((1, half)), _full((1, half)),
                  _full((gg, ch, ch)), _full((gg, ch, ch)), _full((gg, ch, 1)), _full((half, d))],
        out_specs=[_rows(tm, d), _rows(tm, zw), _rows(tm, half), _rows(tm, d), _rows(tm, d),
                   pl.BlockSpec((None, 8, d), lambda i: (0, 0, 0)), _full((8, half)), _full((gg, ch, ch)),
                   _full((gg, ch, 1))],
        out_shape=[jax.ShapeDtypeStruct((seq, d), F32), jax.ShapeDtypeStruct((seq, zw), BF16),
                   jax.ShapeDtypeStruct((seq, half), BF16), jax.ShapeDtypeStruct((seq, d), BF16),
                   jax.ShapeDtypeStruct((seq, d), BF16), jax.ShapeDtypeStruct((1, 8, d), F32),
                   jax.ShapeDtypeStruct((8, half), F32), jax.ShapeDtypeStruct((gg, ch, ch), F32),
                   jax.ShapeDtypeStruct((gg, ch, 1), F32)],
        compiler_params=_params(),
    )(h, dh1, zpre, y, mods, w_in, lng, lnb, ws, ws_t, bs, w_out)


def _final_loss(h, tgt, fg, *, tm):
    seq, d = h.shape

    def body(h_ref, t_ref, g_ref, dh_ref, acc_ref):
        i = pl.program_id(0)
        gain = g_ref[...]
        xhat, r = _rms_parts(h_ref[...])
        err = xhat * gain - t_ref[...]
        dy = err * (1.0 / d)
        dxhat = dy * gain
        dh_ref[...] = r * (dxhat - xhat * jnp.mean(dxhat * xhat, axis=-1, keepdims=True))
        part = jnp.sum(_rowsum(err * err), axis=-1, keepdims=True) * (0.5 / d)
        _acc_rows(acc_ref, i == 0, {0: _rowsum(dy * xhat), 1: jnp.broadcast_to(part, (1, d))})

    return pl.pallas_call(
        body, name="final_loss", grid=(seq // tm,),
        in_specs=[_rows(tm, d), _rows(tm, d), _full((1, d))],
        out_specs=[_rows(tm, d), _full((8, d))],
        out_shape=[jax.ShapeDtypeStruct((seq, d), F32), jax.ShapeDtypeStruct((8, d), F32)],
        compiler_params=_params(),
    )(h, tgt, fg)


def _ada_fwd(c_all, ada_w, ada_b_cols):
    depth, d, ncs = ada_w.shape

    def body(c_ref, w_ref, b_ref, o_ref):
        s = _silu(c_ref[...]).astype(BF16)
        o_ref[...] = _dot(s, w_ref[...].astype(BF16)) + b_ref[...]

    return pl.pallas_call(
        body, name="ada_fwd", grid=(depth,),
        in_specs=[_full((16, d)), pl.BlockSpec((None, d, ncs), lambda i: (i, 0, 0)),
                  pl.BlockSpec((None, 1, ncs), lambda i: (i, 0, 0))],
        out_specs=pl.BlockSpec((None, 16, ncs), lambda i: (i, 0, 0)),
        out_shape=jax.ShapeDtypeStruct((depth, 16, ncs), F32),
        compiler_params=_params(),
    )(c_all, ada_w, ada_b_cols.reshape(depth, 1, ncs))


def _ada_bwd(c_all, c_all_t, dmod, ada_w):
    depth, d, ncs = ada_w.shape

    def body(c_ref, ct_ref, dm_ref, w_ref, gw_ref, dc_ref):
        i = pl.program_id(0)
        dm = dm_ref[...]
        dctx = _rowsum(dm[8:16])
        rid = lax.broadcasted_iota(jnp.int32, (8, ncs), 0)
        low = jnp.where(rid == 0, jnp.broadcast_to(dctx, (8, ncs)), 0.0)
        dm16 = jnp.concatenate([dm[0:8], low], axis=0).astype(BF16)
        gw_ref[...] = _dot(_silu(ct_ref[...]).astype(BF16), dm16)

        @pl.when(i == 0)
        def _():
            dc_ref[...] = jnp.zeros(dc_ref.shape, F32)

        dc_ref[...] += _dot_nt(low.astype(BF16), w_ref[...].astype(BF16)) * _silu_grad(c_ref[8:9, :])

    return pl.pallas_call(
        body, name="ada_bwd", grid=(depth,),
        in_specs=[_full((16, d)), _full((d, 16)), pl.BlockSpec((None, 16, ncs), lambda i: (i, 0, 0)),
                  pl.BlockSpec((None, d, ncs), lambda i: (i, 0, 0))],
        out_specs=[pl.BlockSpec((None, d, ncs), lambda i: (i, 0, 0)), _full((8, d))],
        out_shape=[jax.ShapeDtypeStruct((depth, d, ncs), F32), jax.ShapeDtypeStruct((8, d), F32)],
        compiler_params=_params(),
    )(c_all, c_all_t, dmod, ada_w)


def _adamw_math(w, g, m, v):
    m = ADAM_B1 * m + (1.0 - ADAM_B1) * g
    v = ADAM_B2 * v + (1.0 - ADAM_B2) * jnp.square(g)
    m_hat = m / (1.0 - ADAM_B1 ** ADAM_STEP)
    v_hat = v / (1.0 - ADAM_B2 ** ADAM_STEP)
    delta = -ADAM_LR * (m_hat / (jnp.sqrt(v_hat) + ADAM_EPS) + ADAM_WD * w)
    return delta, m, v


def _adamw(ga, gb, w, m, v, name):
    rows, cols = w.shape
    tr = rows
    while tr * cols * 4 > (1 << 20) and tr % 16 == 0:
        tr //= 2

    def body(ga_ref, gb_ref, w_ref, m_ref, v_ref, g_out, d_out, m_out, v_out):
        g = ga_ref[...] + gb_ref[...]
        delta, m_new, v_new = _adamw_math(w_ref[...], g, m_ref[...], v_ref[...])
        g_out[...] = g
        d_out[...] = delta
        m_out[...] = m_new
        v_out[...] = v_new

    spec = _rows(tr, cols)
    return pl.pallas_call(
        body, name=name, grid=(rows // tr,),
        in_specs=[spec] * 5, out_specs=[spec] * 4,
        out_shape=[jax.ShapeDtypeStruct((rows, cols), F32)] * 4,
        compiler_params=_params(),
    )(ga, gb, w, m, v)


def _sum_devices(gathered, name):
    n, rows, cols = gathered.shape
    tr = rows
    while tr * cols * 4 * n > (4 << 20) and tr % 16 == 0:
        tr //= 2

    def body(x_ref, o_ref):
        acc = x_ref[0]
        for j in range(1, n):
            acc = acc + x_ref[j]
        o_ref[...] = acc

    return pl.pallas_call(
        body, name=name, grid=(rows // tr,),
        in_specs=[pl.BlockSpec((n, tr, cols), lambda i: (0, i, 0))], out_specs=_rows(tr, cols),
        out_shape=jax.ShapeDtypeStruct((rows, cols), F32),
        compiler_params=_params(),
    )(gathered)


def _sum_partials(owns, landeds, name):
    n = len(owns)
    cols = owns[0].shape[-1]
    owns = [o.reshape(-1, cols) for o in owns]
    landeds = [l.reshape(3, -1, cols) for l in landeds]
    rows = owns[0].shape[0]
    tr = rows
    while tr * cols * 2 * n > (1 << 20) and tr % 32 == 0:
        tr //= 2

    def body(*refs):
        out_ref = refs[-1]
        for li in range(n):
            acc = refs[li][...].astype(F32)
            for p in range(3):
                acc = acc + refs[n + li][p].astype(F32)
            out_ref[li] = acc

    out = pl.pallas_call(
        body, name=name, grid=(rows // tr,),
        in_specs=[_rows(tr, cols)] * n + [pl.BlockSpec((3, tr, cols), lambda i: (0, i, 0))] * n,
        out_specs=pl.BlockSpec((n, tr, cols), lambda i: (0, i, 0)),
        out_shape=jax.ShapeDtypeStruct((n, rows, cols), F32),
        compiler_params=_params(),
    )(*owns, *landeds)
    return out.reshape(n * rows, cols)


def _my_place():
    return lax.axis_index("x"), lax.axis_index("y"), lax.axis_index("c")


def _other_chips(x, y):
    return [(1 - x, y), (x, 1 - y), (1 - x, 1 - y)]


def _all_gather_small(block, name):
    rows, cols = block.shape

    def body(x_ref, out_ref, send_sems, recv_sems, local_sem):
        x, y, c = _my_place()
        me, sibling = (x, y, c), (x, y, 1 - c)
        chips = _other_chips(x, y)

        def slot(px, py, pc):
            return out_ref.at[4 * px + 2 * py + pc]

        def copy(k, blk, to, src=None):
            return pltpu.make_async_remote_copy(
                src_ref=slot(*blk) if src is None else src, dst_ref=slot(*blk),
                send_sem=send_sems.at[k], recv_sem=recv_sems.at[k], device_id=to, device_id_type=MESH)

        mine = pltpu.make_async_copy(x_ref, slot(*me), local_sem)
        mine.start()
        first = [copy(0, me, sibling, src=x_ref)]
        first += [copy(1 + j, me, (*chip, c), src=x_ref) for j, chip in enumerate(chips)]
        for cp in first:
            cp.start()
        passed = [copy(4 + j, (*chip, c), sibling) for j, chip in enumerate(chips)]
        for j, chip in enumerate(chips):
            copy(1 + j, (*chip, c), me).wait_recv()
            passed[j].start()
        copy(0, sibling, me).wait_recv()
        for j, chip in enumerate(chips):
            copy(4 + j, (*chip, 1 - c), me).wait_recv()
        for cp in first + passed:
            cp.wait_send()
        mine.wait()

    return pl.pallas_call(
        body, name=name,
        out_shape=jax.ShapeDtypeStruct((N_DEV, rows, cols), block.dtype),
        in_specs=[pl.BlockSpec(memory_space=pltpu.VMEM)],
        out_specs=pl.BlockSpec(memory_space=pltpu.VMEM),
        scratch_shapes=[pltpu.SemaphoreType.DMA((7,)), pltpu.SemaphoreType.DMA((7,)), pltpu.SemaphoreType.DMA],
        compiler_params=_params(),
    )(block)


HBM_SPEC = pl.BlockSpec(memory_space=pltpu.HBM)
SEM_SPEC = pl.BlockSpec(memory_space=pltpu.SEMAPHORE)
DATAFLOW_EFFECT = pltpu.SideEffectType.DATAFLOW_SIDE_EFFECTING


def _gather_views(src, land, p, x, y):
    return src, land.at[2 * x + y]


def _scatter_views(src, land, p, x, y):
    peer_chip = (2 * (1 - x) + y, 2 * x + (1 - y), 2 * (1 - x) + (1 - y))[p]
    return src.at[peer_chip], land.at[p]


def _chip_exchange_copies(srcs, lands, send_sems, recv_sems, views):
    x, y, c = _my_place()
    copies = []
    for j, (src, land) in enumerate(zip(srcs, lands)):
        for p, chip in enumerate(_other_chips(x, y)):
            s_view, d_view = views(src, land, p, x, y)
            copies.append(pltpu.make_async_remote_copy(
                src_ref=s_view, dst_ref=d_view, send_sem=send_sems.at[3 * j + p], recv_sem=recv_sems.at[3 * j + p],
                device_id=(*chip, c), device_id_type=MESH))
    return copies


def _exchange_start(srcs, lands, views, name):
    n = len(srcs)

    def body(*refs):
        send_sems, recv_sems = refs[2 * n], refs[2 * n + 1]
        token = refs[-1]
        for cp in _chip_exchange_copies(refs[:n], refs[n:2 * n], send_sems, recv_sems, views):
            cp.start()
        token[...] = jnp.zeros(token.shape, token.dtype)

    operands = [pltpu.with_memory_space_constraint(a, pltpu.HBM) for a in (*srcs, *lands)]
    out = pl.pallas_call(
        body, name=name,
        out_shape=(pltpu.SemaphoreType.DMA((3 * n,)), pltpu.SemaphoreType.DMA((3 * n,)),
                   *[pltpu.HBM(a.shape, a.dtype) for a in operands], jax.ShapeDtypeStruct((8, LANES), F32)),
        in_specs=[HBM_SPEC] * (2 * n),
        out_specs=(SEM_SPEC, SEM_SPEC, *[HBM_SPEC] * (2 * n), pl.BlockSpec(memory_space=pltpu.VMEM)),
        input_output_aliases={i: 2 + i for i in range(2 * n)},
        compiler_params=pltpu.CompilerParams(has_side_effects=DATAFLOW_EFFECT),
    )(*operands)
    return out[0], out[1], list(out[2:2 + n]), list(out[2 + n:2 + 2 * n]), out[-1]


def _exchange_wait(send_sems, recv_sems, srcs, lands, views, after, name):
    n = len(srcs)

    def body(*refs):
        send, recv = refs[2 * n], refs[2 * n + 1]
        for cp in _chip_exchange_copies(refs[:n], refs[n:2 * n], send, recv, views):
            cp.wait_send()
            cp.wait_recv()

    out = pl.pallas_call(
        body, name=name,
        out_shape=tuple(pltpu.HBM(a.shape, a.dtype) for a in (*srcs, *lands)),
        in_specs=[HBM_SPEC] * (2 * n) + [SEM_SPEC, SEM_SPEC, HBM_SPEC],
        out_specs=tuple([HBM_SPEC] * (2 * n)),
        input_output_aliases={i: i for i in range(2 * n)},
        compiler_params=pltpu.CompilerParams(has_side_effects=DATAFLOW_EFFECT),
    )(*srcs, *lands, send_sems, recv_sems, pltpu.with_memory_space_constraint(after, pltpu.HBM))
    return list(out[:n]), list(out[n:])


def _landing_for_gather(shard, chip):
    land = lax.empty((N_CHIPS, *shard.shape), shard.dtype)
    return lax.dynamic_update_index_in_dim(land, shard, chip, 0)


def _swap_with_sibling(parts):
    n = len(parts)

    def body(*refs):
        ins, outs = refs[:n], refs[n:2 * n]
        send_sems, recv_sems = refs[2 * n:]
        x, y, c = _my_place()
        copies = [pltpu.make_async_remote_copy(
            src_ref=ins[j], dst_ref=outs[j], send_sem=send_sems.at[j], recv_sem=recv_sems.at[j],
            device_id=(x, y, 1 - c), device_id_type=MESH) for j in range(n)]
        for cp in copies:
            cp.start()
        for cp in copies:
            cp.wait()

    any_spec = pl.BlockSpec(memory_space=pl.ANY)
    return pl.pallas_call(
        body, name="swap_with_sibling", out_shape=[jax.ShapeDtypeStruct(p.shape, p.dtype) for p in parts],
        in_specs=[any_spec] * n, out_specs=[any_spec] * n,
        scratch_shapes=[pltpu.SemaphoreType.DMA((n,)), pltpu.SemaphoreType.DMA((n,))],
        compiler_params=_params(),
    )(*parts)


TILE_ELEMS = SUBLANES * LANES


def _pack(arrays):
    parts = []
    for a in arrays:
        flat = a.reshape(-1).astype(F32)
        pad = (-flat.shape[0]) % TILE_ELEMS
        if pad:
            flat = jnp.concatenate([flat, jnp.zeros((pad,), F32)])
        parts.append(flat.reshape(-1, LANES))
    return jnp.concatenate(parts, axis=0) if len(parts) > 1 else parts[0]


def _unpack(buf, shapes):
    out, r = [], 0
    lead = buf.shape[:-2]
    for shp in shapes:
        size = math.prod(shp)
        nr = -(-size // TILE_ELEMS) * SUBLANES
        flat = buf[..., r:r + nr, :].reshape(*lead, nr * LANES)[..., :size]
        out.append(flat.reshape(*lead, *shp))
        r += nr
    return out


def _chip_cols(a, k, width):
    return lax.dynamic_slice_in_dim(a, k * width, width, axis=a.ndim - 1)


def _across_chips(gathered, c0_only_shape):
    return gathered.reshape(2, 2, 2, *c0_only_shape)[:, :, 0].reshape(N_CHIPS, *c0_only_shape)


def kernel(x, c, ctx, c_ctx, ada_w, ada_b, norm_g, mlp_w1, mlp_w2, pool_w, pool_scale, attn_w_qkv, attn_w_o, attn_q_g, attn_k_g, gm_w_in, gm_ln_g, gm_ln_b, gm_ws, gm_bs, gm_w_out, final_g, loss_target, m_c_ctx, m_ada_w, m_ada_b, m_norm_g, m_mlp_w1, m_mlp_w2, m_pool_w, m_pool_scale, m_attn_w_qkv, m_attn_w_o, m_attn_q_g, m_attn_k_g, m_gm_w_in, m_gm_ln_g, m_gm_ln_b, m_gm_ws, m_gm_bs, m_gm_w_out, m_final_g, v_c_ctx, v_ada_w, v_ada_b, v_norm_g, v_mlp_w1, v_mlp_w2, v_pool_w, v_pool_scale, v_attn_w_qkv, v_attn_w_o, v_attn_q_g, v_attn_k_g, v_gm_w_in, v_gm_ln_g, v_gm_ln_b, v_gm_ws, v_gm_bs, v_gm_w_out, v_final_g):
    seq, d = x.shape[1], x.shape[2]
    n_ctx = ctx.shape[1]
    total = n_ctx + seq
    hd = attn_q_g.shape[-1]
    nh = d // hd
    nkv = nh // 2
    gg, ch = gm_ws.shape[1], gm_ws.shape[-1]
    half = gm_w_out.shape[1] * N_CHIPS
    pgw = pool_w.shape[-1]
    tm = min(256, n_ctx)
    nct = n_ctx // tm
    seg_lens = (n_ctx, seq)

    mx, my, mc = _my_place()
    chip = 2 * mx + my
    me = 4 * mx + 2 * my + mc

    c_rows = jnp.concatenate([c, jnp.zeros((7, d), F32)], axis=0)
    c_gath = _all_gather_small(c_rows, "gather_cond")[:, 0, :]
    c_all = jnp.concatenate([c_gath, c_ctx[None, :], jnp.zeros((7, d), F32)], axis=0)
    ncs = ada_w.shape[-1]
    ada_cols = _ada_fwd(c_all, ada_w, _chip_cols(ada_b, chip, ncs))
    small_shapes = [ada_cols.shape, norm_g.shape, pool_scale.shape, gm_ln_g.shape, gm_ln_b.shape]
    gathered = _all_gather_small(_pack([ada_cols, norm_g, pool_scale, gm_ln_g, gm_ln_b]), "gather_small_params")
    per_chip = _across_chips(gathered, gathered.shape[1:])
    ada_g, ng_g, ps_g, lng_g, lnb_g = _unpack(per_chip, small_shapes)

    def join_last(a):
        return jnp.moveaxis(a, 0, -2).reshape(*a.shape[1:-1], N_CHIPS * a.shape[-1])

    ada_full = join_last(ada_g)
    ng_full = join_last(ng_g)
    ps_full = join_last(ps_g)
    lng_full = join_last(lng_g)
    lnb_full = join_last(lnb_g)
    mod_lat = lax.dynamic_slice_in_dim(ada_full, me, 1, axis=1).reshape(DEPTH, 6, d)
    mod_ctx = ada_full[:, 8].reshape(DEPTH, 6, d)
    mods = jnp.stack([jnp.concatenate([mod_ctx, ng_full], axis=1), jnp.concatenate([mod_lat, ng_full], axis=1)],
                     axis=1)

    weight_groups = [
        [mlp_w1[0], mlp_w2[0], pool_w],
        [mlp_w1[1], mlp_w2[1], attn_w_qkv[0], attn_w_o[0]],
        [mlp_w1[2], mlp_w2[2], gm_w_in[0], gm_w_out[0], mlp_w1[3], mlp_w2[3]],
    ]
    gathers = [None] * len(weight_groups)

    def gather_start(gi, after):
        shards, _ = lax.optimization_barrier(([w.astype(BF16) for w in weight_groups[gi]], after))
        lands = [_landing_for_gather(s, chip) for s in shards]
        gathers[gi] = _exchange_start(shards, lands, _gather_views, f"gather_weights_{gi}_start")
        return gathers[gi][4][0:1, 0:1]

    def gathered(gi, after):
        send, recv, srcs, lands, _ = gathers[gi]
        return _exchange_wait(send, recv, srcs, lands, _gather_views, after, f"gather_weights_{gi}_wait")[1]

    def rows_joined(a):
        return a.reshape(-1, a.shape[-1])

    w1_b, w2_b = [None] * DEPTH, [None] * DEPTH
    gather_start(0, mods)
    w1_b[0], w2_b[0], pw_land = gathered(0, ps_full)
    mods0 = mods[0] + gather_start(1, w1_b[0])
    pw_f = jnp.transpose(pw_land, (1, 2, 0, 3, 4)).reshape(pool_w.shape[0], pool_w.shape[1], pgw, pgw)

    gains = jnp.concatenate([attn_q_g, attn_k_g, jnp.zeros((6, hd), F32)], axis=0)
    ws_b = gm_ws[0].astype(BF16)
    ws_t = jnp.swapaxes(gm_ws[0], 1, 2).astype(BF16)
    bs_col = gm_bs[0][:, :, None]
    cos, sin = _rope_tables(n_ctx, seq, hd)
    lat = lambda i: mods[i, 1:2]

    hc0 = jnp.concatenate([ctx[0], x[0]], axis=0)
    ha0 = _pool_fwd(hc0, mods0, pw_f, ps_full, 0, nct=nct, tm=tm, seg_lens=seg_lens)
    hc1, u0, o0 = _mlp_fwd(ha0, mods0, w1_b[0], w2_b[0], 0, nct=nct, tm=tm)
    w1_b[1], w2_b[1], wqkv_b, wo_land = gathered(1, hc1)
    mods1 = mods[1] + gather_start(2, w1_b[1])
    wo_f = rows_joined(wo_land)
    xa1 = _normmod_call(hc1, mods1, 0, "attn_in_fwd", nct=nct, tm=tm)
    qkv, q_r, k_r, v_b = _qkv_fwd(xa1, wqkv_b, cos, sin, gains, nh=nh, nkv=nkv, nct=nct, tm=tm)
    o_att, lse = _flash_fwd(q_r, k_r, v_b, n_ctx=n_ctx, hd=hd)
    ha1, y1 = _proj_fwd(o_att, wo_f, hc1, mods1, n_ctx=n_ctx, tm=tm)
    h2, u1, o1 = _mlp_fwd(ha1, lat(1), w1_b[1], w2_b[1], 1, nct=0, tm=tm)
    w1_b[2], w2_b[2], win_b, wout_land, w1_b[3], w2_b[3] = gathered(2, h2)
    wout_f = rows_joined(wout_land)
    ha2, zpre, y2 = _gmlp_fwd(h2, mods[2], win_b, lng_full, lnb_full, ws_b, bs_col, wout_f, tm=tm)
    h3, u2, o2 = _mlp_fwd(ha2, lat(2), w1_b[2], w2_b[2], 2, nct=0, tm=tm)
    ha3 = _pool_fwd(h3, lat(3), pw_f, ps_full, 3, nct=0, tm=tm, seg_lens=seg_lens)
    h4, u3, o3 = _mlp_fwd(ha3, lat(3), w1_b[3], w2_b[3], 3, nct=0, tm=tm)
    dh4, fin_acc = _final_loss(h4, loss_target[0], final_g[None, :], tm=tm)

    dmods = [None] * DEPTH
    scatters = [None] * DEPTH

    def blocked_rows(g):
        return g.reshape(N_CHIPS, g.shape[1] // N_CHIPS, g.shape[2])

    def blocked_pool(dpw):
        pg = dpw.shape[0]
        return jnp.transpose(dpw.astype(BF16).reshape(pg, N_CHIPS, pgw // N_CHIPS, pgw), (1, 0, 2, 3))

    def scatter_start(i, grads):
        lands = [lax.empty((3, *g.shape[1:]), g.dtype) for g in grads]
        scatters[i] = _exchange_start(grads, lands, _scatter_views, f"scatter_grads_{i}_start")
        return scatters[i][4][0:1, 0:1]

    def mlp_back(i, h_in, dh_out, u, o, md, n_ct):
        dh_in, du, dob, mb, dmd = _mlp_bwd(h_in, dh_out, u, o, md, w1_b[i], w2_b[i], i, nct=n_ct, tm=tm)
        dw1 = _mm_tn(mb, du, f"mlp_dw1_{i}", col_blocks=N_CHIPS)
        dw2 = blocked_rows(_mm_tn(u, dob, f"mlp_dw2_{i}", relu2=True))
        return dh_in, dmd, [dw1, dw2]

    def pool_back(i, h_in, dh_out, md, n_ct):
        dp, dmd_a, dps, dpw = _pool_bwd_weights(h_in, dh_out, md, pw_f, ps_full, i, nct=n_ct, tm=tm,
                                                seg_lens=seg_lens)
        dh_in, dmd_b = _pool_bwd_input(dp, h_in, dh_out, md, i, nct=n_ct, tm=tm, seg_lens=seg_lens, gw=pgw)
        return dh_in, dmd_a + dmd_b, dps, dpw

    zero_grp = jnp.zeros((1, 8, d), F32)
    dha3, dmd3, dws3 = mlp_back(3, ha3, dh4, u3, o3, lat(3), 0)
    dh3, dmd3p, dps3, dpw3 = pool_back(3, h3, dha3, lat(3), 0)
    dmods[3] = jnp.concatenate([zero_grp, dmd3 + dmd3p], axis=0)
    tok = scatter_start(3, dws3 + [blocked_pool(dpw3)])
    dha2, dmd2, dws2 = mlp_back(2, ha2, dh3, u2, o2, lat(2) + tok, 0)
    dh2, dzpre, gated, dyb2, ab2, dmd2g, dln, dws, dbs = _gmlp_bwd(
        h2, dha2, zpre, y2, mods[2], win_b, lng_full, lnb_full, ws_b, ws_t, bs_col, wout_f, tm=tm)
    dwin = _mm_tn(ab2, dzpre, "gmlp_dw_in", col_blocks=N_CHIPS)
    dwout = blocked_rows(_mm_tn(gated, dyb2, "gmlp_dw_out"))
    dmods[2] = jnp.concatenate([zero_grp, dmd2 + dmd2g], axis=0)
    tok = scatter_start(2, dws2 + [dwin, dwout])
    dha1, dmd1, dws1 = mlp_back(1, ha1, dh2, u1, o1, lat(1) + tok, 0)
    do_att, dyb1, dmd1p = _proj_bwd(dha1, y1, mods[1], wo_f, tm=tm)
    dwo = blocked_rows(_mm_tn(o_att, dyb1, "attn_dw_o"))
    dq, dk, dv = _flash_bwd(q_r, k_r, v_b, o_att, do_att, lse, n_ctx=n_ctx, hd=hd)
    dqkv, dgains = _qkv_bwd(qkv, dq, dk, dv, cos, sin, gains, nh=nh, nkv=nkv, nct=nct, tm=tm)
    dwqkv = _mm_tn(xa1, dqkv, "attn_dw_qkv", col_blocks=N_CHIPS)
    dhc1, dmd1i = _attn_in_bwd(dqkv, wqkv_b, hc1, dha1, mods[1], nct=nct, tm=tm)
    dmods[1] = dmd1i + jnp.concatenate([zero_grp, dmd1 + dmd1p], axis=0)
    tok = scatter_start(1, dws1 + [dwqkv, dwo])
    dha0, dmd0, dws0 = mlp_back(0, ha0, dhc1, u0, o0, mods[0] + tok, nct)
    dhc0, dmd0p, dps0, dpw0 = pool_back(0, hc0, dha0, mods[0], nct)
    dmods[0] = dmd0 + dmd0p
    grad_x = dhc0[n_ctx:][None]

    dmods_all = jnp.stack(dmods, axis=0)
    small_grads = [dmods_all, dws, dbs, dgains, dln, dps0, dps3, fin_acc]
    sg_shapes = [a.shape for a in small_grads]
    sg_gath = _all_gather_small(_pack(small_grads), "gather_small_grads")
    sg_sum = _sum_devices(sg_gath, "sum_small_grads")
    s_dmods, s_dws, s_dbs, s_dgains, s_dln, s_dps0, s_dps3, s_fin = _unpack(sg_sum, sg_shapes)
    loss = s_fin[1, 0]

    dm_dev = _unpack(sg_gath, sg_shapes[:1])[0]
    dm_lat = jnp.moveaxis(dm_dev[:, :, 1, :6, :], 0, 1).reshape(DEPTH, N_DEV, 6 * d)
    dm_ctx = jnp.moveaxis(dm_dev[:, :, 0, :6, :], 0, 1).reshape(DEPTH, N_DEV, 6 * d)
    dmod16 = _chip_cols(jnp.concatenate([dm_lat, dm_ctx], axis=1), chip, ncs)
    g_ada_w, dcc_part = _ada_bwd(c_all, c_all.T, dmod16, ada_w)
    dcc_gath = _all_gather_small(dcc_part, "gather_d_c_ctx")
    dcc_chips = _across_chips(dcc_gath, dcc_gath.shape[1:])
    dcc = _sum_devices(dcc_chips, "sum_d_c_ctx")[0]

    grads0, _ = lax.optimization_barrier((dws0 + [blocked_pool(dpw0)], dcc))
    after_last_start = sg_sum + scatter_start(0, grads0)
    own, landed = [None] * DEPTH, [None] * DEPTH
    for i in (3, 2, 1, 0):
        send, recv, srcs, lands, _ = scatters[i]
        srcs, landed[i] = _exchange_wait(send, recv, srcs, lands, _scatter_views, after_last_start,
                                         f"scatter_grads_{i}_wait")
        own[i] = [lax.dynamic_index_in_dim(s, chip, 0, keepdims=False) for s in srcs]

    def summed(name, picks):
        return _sum_partials([own[i][j] for i, j in picks], [landed[i][j] for i, j in picks], f"sum_chips_{name}")

    big = [("mlp_w1", mlp_w1, m_mlp_w1, v_mlp_w1, [(i, 0) for i in range(DEPTH)]),
           ("mlp_w2", mlp_w2, m_mlp_w2, v_mlp_w2, [(i, 1) for i in range(DEPTH)]),
           ("pool_w", pool_w, m_pool_w, v_pool_w, [(0, 2), (3, 2)]),
           ("attn_w_qkv", attn_w_qkv, m_attn_w_qkv, v_attn_w_qkv, [(1, 2)]),
           ("attn_w_o", attn_w_o, m_attn_w_o, v_attn_w_o, [(1, 3)]),
           ("gm_w_in", gm_w_in, m_gm_w_in, v_gm_w_in, [(2, 2)]),
           ("gm_w_out", gm_w_out, m_gm_w_out, v_gm_w_out, [(2, 3)])]
    partial = [summed(name, picks) for name, _, _, _, picks in big]
    from_sibling = _swap_with_sibling(partial)
    big_out = {}
    for (name, w, m, v, _), mine, theirs in zip(big, partial, from_sibling):
        cols = w.shape[-1]
        res = _adamw(mine, theirs, w.reshape(-1, cols), m.reshape(-1, cols), v.reshape(-1, cols), f"adamw_{name}")
        big_out[name] = [r.reshape(w.shape) for r in res]
    ada_res = _adamw(g_ada_w.reshape(-1, ncs), jnp.zeros((DEPTH * d, ncs), F32), ada_w.reshape(-1, ncs),
                     m_ada_w.reshape(-1, ncs), v_ada_w.reshape(-1, ncs), "adamw_ada_w")
    big_out["ada_w"] = [r.reshape(ada_w.shape) for r in ada_res]

    def cols_of(a, width):
        return _chip_cols(a, chip, width)

    zero = lambda a: jnp.zeros(a.shape, F32)
    ngw = norm_g.shape[-1]
    small = {
        "c_ctx": (dcc, zero(dcc), c_ctx, m_c_ctx, v_c_ctx),
        "ada_b": (s_dmods[:, 0, :6].reshape(DEPTH, 6 * d), s_dmods[:, 1, :6].reshape(DEPTH, 6 * d), ada_b, m_ada_b,
                  v_ada_b),
        "norm_g": (cols_of(s_dmods[:, 0, 6:8], ngw), cols_of(s_dmods[:, 1, 6:8], ngw), norm_g, m_norm_g, v_norm_g),
        "pool_scale": (cols_of(jnp.stack([s_dps0[0], s_dps3[0]]), pool_scale.shape[-1]), zero(pool_scale),
                       pool_scale, m_pool_scale, v_pool_scale),
        "attn_q_g": (s_dgains[0:1], zero(attn_q_g), attn_q_g, m_attn_q_g, v_attn_q_g),
        "attn_k_g": (s_dgains[1:2], zero(attn_k_g), attn_k_g, m_attn_k_g, v_attn_k_g),
        "gm_ln_g": (cols_of(s_dln[0:1], gm_ln_g.shape[-1]), zero(gm_ln_g), gm_ln_g, m_gm_ln_g, v_gm_ln_g),
        "gm_ln_b": (cols_of(s_dln[1:2], gm_ln_b.shape[-1]), zero(gm_ln_b), gm_ln_b, m_gm_ln_b, v_gm_ln_b),
        "gm_ws": (s_dws[None], zero(gm_ws), gm_ws, m_gm_ws, v_gm_ws),
        "gm_bs": (s_dbs[None, :, :, 0], zero(gm_bs), gm_bs, m_gm_bs, v_gm_bs),
        "final_g": (s_fin[0], zero(final_g), final_g, m_final_g, v_final_g),
    }
    keys = list(small)
    packed = [_pack([small[k][t] for k in keys]) for t in range(5)]
    res = _adamw(*packed, "adamw_small")
    shapes = [small[k][2].shape for k in keys]
    small_out = {k: [] for k in keys}
    for r in res:
        for k, a in zip(keys, _unpack(r, shapes)):
            small_out[k].append(a)

    order = ["c_ctx", "ada_w", "ada_b", "norm_g", "mlp_w1", "mlp_w2", "pool_w", "pool_scale", "attn_w_qkv",
             "attn_w_o", "attn_q_g", "attn_k_g", "gm_w_in", "gm_ln_g", "gm_ln_b", "gm_ws", "gm_bs", "gm_w_out",
             "final_g"]
    allo = {**big_out, **small_out}
    outs = [loss, grad_x]
    for t in range(4):
        outs += [allo[k][t] for k in order]
    return tuple(outs)
```

```python
import functools
import math

import jax
import jax.numpy as jnp
from jax import lax
from jax.experimental import pallas as pl
from jax.experimental.pallas import tpu as pltpu

F32 = jnp.float32
BF16 = jnp.bfloat16
MESH = pl.DeviceIdType.MESH

EPS = 1e-6
GRID_W = 64
ROPE_BASE = 10000.0
POOL_WINDOWS = (2, 4, 8, 16)
HALO = 8
DEPTH = 4
N_MIXERS = 3

ADAM_LR = 0.001
ADAM_B1 = 0.9
ADAM_B2 = 0.999
ADAM_EPS = 1e-08
ADAM_WD = 0.01
ADAM_STEP = 10

VMEM_LIMIT_BYTES = 56 * 1024 * 1024
LANES = 128
SUBLANES = 8
N_DEV = 8
N_CHIPS = 4

SH1, SC1, G1, SH2, SC2, G2, NG0, NG1 = range(8)


def _dot(a, b):
    return jnp.dot(a, b, preferred_element_type=F32)


def _dot_nt(a, b):
    return lax.dot_general(a, b, (((1,), (1,)), ((), ())), preferred_element_type=F32)


def _dot_tn(a, b):
    return lax.dot_general(a, b, (((0,), (0,)), ((), ())), preferred_element_type=F32)


def _dot_blocks(a, w_ref):
    return jnp.concatenate([_dot(a, w_ref[k]) for k in range(w_ref.shape[0])], axis=1)


def _dot_nt_blocks(a, w_ref):
    nb, _, w = w_ref.shape
    acc = _dot_nt(a[:, 0:w], w_ref[0])
    for k in range(1, nb):
        acc = acc + _dot_nt(a[:, k * w:(k + 1) * w], w_ref[k])
    return acc


def _params(**kw):
    return pltpu.CompilerParams(vmem_limit_bytes=VMEM_LIMIT_BYTES, **kw)


def _full(shape):
    nd = len(shape)
    return pl.BlockSpec(shape, lambda *_: (0,) * nd)


def _rows(tm, width):
    return pl.BlockSpec((tm, width), lambda i: (i, 0))


def _group_of(nct, groups):
    if groups == 1:
        return lambda i: 0
    return lambda i: jnp.where(i >= nct, 1, 0)


def _mods_spec(nct, groups, d):
    grp = _group_of(nct, groups)
    return pl.BlockSpec((None, 8, d), lambda i: (grp(i), 0, 0))


def _first_of_group(i, nct, groups):
    if groups == 1:
        return i == 0
    return jnp.logical_or(i == 0, i == nct)


def _rowsum(v):
    return jnp.sum(v, axis=0, keepdims=True)


def _rms_parts(x):
    r = lax.rsqrt(jnp.mean(x * x, axis=-1, keepdims=True) + EPS)
    return x * r, r


def _normmod(x, md, which):
    ng, sh, sc = (md[NG0:NG0 + 1], md[SH1:SH1 + 1], md[SC1:SC1 + 1]) if which == 0 else (
        md[NG1:NG1 + 1], md[SH2:SH2 + 1], md[SC2:SC2 + 1])
    xhat, r = _rms_parts(x)
    n = xhat * ng
    return n * (1.0 + sc) + sh, (xhat, r, n)


def _normmod_bwd(da, parts, md, which):
    xhat, r, n = parts
    ng, sc = (md[NG0:NG0 + 1], md[SC1:SC1 + 1]) if which == 0 else (md[NG1:NG1 + 1], md[SC2:SC2 + 1])
    dsh = _rowsum(da)
    dsc = _rowsum(da * n)
    dn = da * (1.0 + sc)
    dng = _rowsum(dn * xhat)
    dxhat = dn * ng
    dx = r * (dxhat - xhat * jnp.mean(dxhat * xhat, axis=-1, keepdims=True))
    return dx, dsh, dsc, dng


def _acc_rows(ref, first, rows):
    @pl.when(first)
    def _():
        ref[...] = jnp.zeros(ref.shape, ref.dtype)

    for r, v in rows.items():
        ref[r:r + 1, :] += v


def _shift_up(x, k):
    if k == 0:
        return x
    return pltpu.roll(x, x.shape[0] - k, axis=0)


def _gelu(x):
    k = math.sqrt(2.0 / math.pi)
    return 0.5 * x * (1.0 + jnp.tanh(k * (x + 0.044715 * x * x * x)))


def _gelu_grad(x):
    k = math.sqrt(2.0 / math.pi)
    t = jnp.tanh(k * (x + 0.044715 * x * x * x))
    return 0.5 * (1.0 + t) + 0.5 * x * (1.0 - t * t) * k * (1.0 + 3.0 * 0.044715 * x * x)


def _silu(x):
    return x / (1.0 + jnp.exp(-x))


def _silu_grad(x):
    s = 1.0 / (1.0 + jnp.exp(-x))
    return s * (1.0 + x * (1.0 - s))


def _mlp_fwd(h, mods, w1, w2, layer, *, nct, tm):
    rows, d = h.shape
    groups = mods.shape[0]
    nb, _, fc = w1.shape
    ff = nb * fc

    def body(h_ref, md_ref, w1_ref, w2_ref, h2_ref, u_ref, o_ref):
        x = h_ref[...]
        md = md_ref[...]
        m, _ = _normmod(x, md, 1)
        mb = m.astype(BF16)
        acc = jnp.zeros((tm, d), F32)
        for k in range(nb):
            u = _dot(mb, w1_ref[k])
            u_ref[:, k * fc:(k + 1) * fc] = u.astype(BF16)
            acc = acc + _dot(jnp.square(jnp.maximum(u, 0.0)).astype(BF16), w2_ref[k])
        o_ref[...] = acc.astype(BF16)
        h2_ref[...] = x + md[G2:G2 + 1] * acc

    return pl.pallas_call(
        body, name=f"mlp_fwd_{layer}", grid=(rows // tm,),
        in_specs=[_rows(tm, d), _mods_spec(nct, groups, d), _full(w1.shape), _full(w2.shape)],
        out_specs=[_rows(tm, d), _rows(tm, ff), _rows(tm, d)],
        out_shape=[jax.ShapeDtypeStruct((rows, d), F32), jax.ShapeDtypeStruct((rows, ff), BF16),
                   jax.ShapeDtypeStruct((rows, d), BF16)],
        compiler_params=_params(),
    )(h, mods, w1, w2)


def _mlp_bwd(h1, dh2, u, o, mods, w1, w2, layer, *, nct, tm):
    rows, d = h1.shape
    groups = mods.shape[0]
    nb, _, fc = w1.shape
    ff = nb * fc

    def body(h_ref, g_ref, u_ref, o_ref, md_ref, w1_ref, w2_ref, dh_ref, du_ref, dob_ref, mb_ref, dmd_ref):
        i = pl.program_id(0)
        x = h_ref[...]
        g = g_ref[...]
        md = md_ref[...]
        m, parts = _normmod(x, md, 1)
        mb_ref[...] = m.astype(BF16)
        dg2 = _rowsum(g * o_ref[...].astype(F32))
        dob = (g * md[G2:G2 + 1]).astype(BF16)
        dob_ref[...] = dob
        dm = jnp.zeros((tm, d), F32)
        for k in range(nb):
            uk = u_ref[:, k * fc:(k + 1) * fc].astype(F32)
            dr = _dot_nt(dob, w2_ref[k])
            duk = (dr * (2.0 * jnp.maximum(uk, 0.0))).astype(BF16)
            du_ref[:, k * fc:(k + 1) * fc] = duk
            dm = dm + _dot_nt(duk, w1_ref[k])
        dx, dsh, dsc, dng = _normmod_bwd(dm, parts, md, 1)
        dh_ref[...] = g + dx
        _acc_rows(dmd_ref, _first_of_group(i, nct, groups), {SH2: dsh, SC2: dsc, G2: dg2, NG1: dng})

    return pl.pallas_call(
        body, name=f"mlp_bwd_{layer}", grid=(rows // tm,),
        in_specs=[_rows(tm, d), _rows(tm, d), _rows(tm, ff), _rows(tm, d), _mods_spec(nct, groups, d),
                  _full(w1.shape), _full(w2.shape)],
        out_specs=[_rows(tm, d), _rows(tm, ff), _rows(tm, d), _rows(tm, d), _mods_spec(nct, groups, d)],
        out_shape=[jax.ShapeDtypeStruct((rows, d), F32), jax.ShapeDtypeStruct((rows, ff), BF16),
                   jax.ShapeDtypeStruct((rows, d), BF16), jax.ShapeDtypeStruct((rows, d), BF16),
                   jax.ShapeDtypeStruct((groups, 8, d), F32)],
        compiler_params=_params(),
    )(h1, dh2, u, o, mods, w1, w2)


def _div_tile(n, cap):
    if n <= cap:
        return n
    return max(t for t in range(LANES, cap + 1, LANES) if n % t == 0)


def _mm_tn(a, b, name, *, relu2=False, col_blocks=1):
    rows, m = a.shape
    n = b.shape[1]
    tmm = min(m, 1024)
    tn = min(n // col_blocks, 2048)
    per_block = n // col_blocks // tn
    tr = _div_tile(rows, 1024)

    def body(a_ref, b_ref, o_ref, acc_ref):
        r = pl.program_id(2)

        @pl.when(r == 0)
        def _():
            acc_ref[...] = jnp.zeros(acc_ref.shape, F32)

        av = a_ref[...]
        if relu2:
            av = jnp.square(jnp.maximum(av.astype(F32), 0.0)).astype(BF16)
        acc_ref[...] += _dot_tn(av, b_ref[...])

        @pl.when(r == pl.num_programs(2) - 1)
        def _():
            o_ref[...] = acc_ref[...].astype(BF16)

    return pl.pallas_call(
        body, name=name, grid=(m // tmm, n // tn, rows // tr),
        in_specs=[pl.BlockSpec((tr, tmm), lambda i, j, r: (r, i)), pl.BlockSpec((tr, tn), lambda i, j, r: (r, j))],
        out_specs=pl.BlockSpec((None, tmm, tn), lambda i, j, r: (j // per_block, i, j % per_block)),
        out_shape=jax.ShapeDtypeStruct((col_blocks, m, n // col_blocks), BF16),
        scratch_shapes=[pltpu.VMEM((tmm, tn), F32)],
        compiler_params=_params(),
    )(a, b)


def _halo_specs(tm, d, rows):
    per = tm // HALO
    prev = pl.BlockSpec((HALO, d), lambda i: (jnp.maximum(i * per - 1, 0), 0))
    nxt = pl.BlockSpec((HALO, d), lambda i: (jnp.minimum((i + 1) * per, rows // HALO - 1), 0))
    return prev, _rows(tm, d), nxt


def _segment_positions(i, tm, nct, groups, seg_lens):
    if groups == 1:
        start, length = 0, seg_lens[-1]
    else:
        start = jnp.where(i >= nct, nct, 0)
        length = jnp.where(i >= nct, seg_lens[1], seg_lens[0])
    rid = lax.broadcasted_iota(jnp.int32, (tm + 2 * HALO, 1), 0)
    pos = (i - start) * tm - HALO + rid
    return pos, length


def _window_count(pos, length, w):
    hi = jnp.minimum(pos + (w - w // 2), length)
    lo = jnp.maximum(pos - w // 2, 0)
    return (hi - lo).astype(F32)


def _window_sum(xg, w, lead):
    b, k = xg, 1
    while k < w:
        b = b + _shift_up(b, k)
        k *= 2
    return _shift_up(b, HALO - lead)[0:xg.shape[0] - 2 * HALO]


def _pooled(ext, md, pos, length, gw):
    tm = ext.shape[0] - 2 * HALO
    a_ext, parts = _normmod(ext, md, 0)
    valid = jnp.logical_and(pos >= 0, pos < length)
    a_ext = jnp.where(valid, a_ext, 0.0)
    pos_c = pos[HALO:HALO + tm]
    ps = []
    for g, w in enumerate(POOL_WINDOWS):
        xg = a_ext[:, g * gw:(g + 1) * gw]
        s = _window_sum(xg, w, w // 2)
        ps.append(s * (1.0 / _window_count(pos_c, length, w)) - xg[HALO:HALO + tm])
    return ps, parts


def _pool_fwd(h, mods, pw, pscale, layer, *, nct, tm, seg_lens):
    rows, d = h.shape
    groups = mods.shape[0]
    pg, gw = pw.shape[1], pw.shape[-1]

    def body(prev_ref, cur_ref, next_ref, md_ref, pw_ref, ps_ref, out_ref):
        i = pl.program_id(0)
        md = md_ref[...]
        cur = cur_ref[...]
        ext = jnp.concatenate([prev_ref[...], cur, next_ref[...]], axis=0)
        pos, length = _segment_positions(i, tm, nct, groups, seg_lens)
        ps, _ = _pooled(ext, md, pos, length, gw)
        for g in range(pg):
            yg = _dot(ps[g].astype(BF16), pw_ref[g]) * ps_ref[:, g * gw:(g + 1) * gw]
            out_ref[:, g * gw:(g + 1) * gw] = cur[:, g * gw:(g + 1) * gw] + md[G1:G1 + 1, g * gw:(g + 1) * gw] * yg

    j = layer // N_MIXERS
    return pl.pallas_call(
        body, name=f"pool_fwd_{layer}", grid=(rows // tm,),
        in_specs=[*_halo_specs(tm, d, rows), _mods_spec(nct, groups, d),
                  pl.BlockSpec((None, pg, gw, gw), lambda i: (j, 0, 0, 0)), _full((1, d))],
        out_specs=_rows(tm, d),
        out_shape=jax.ShapeDtypeStruct((rows, d), F32),
        compiler_params=_params(),
    )(h, h, h, mods, pw, pscale[j:j + 1])


def _pool_bwd_weights(h, dh1, mods, pw, pscale, layer, *, nct, tm, seg_lens):
    rows, d = h.shape
    groups = mods.shape[0]
    pg, gw = pw.shape[1], pw.shape[-1]

    def body(prev_ref, cur_ref, next_ref, g_ref, md_ref, pw_ref, ps_ref, dp_ref, dmd_ref, dps_ref, dpw_ref):
        i = pl.program_id(0)
        md = md_ref[...]
        ext = jnp.concatenate([prev_ref[...], cur_ref[...], next_ref[...]], axis=0)
        pos, length = _segment_positions(i, tm, nct, groups, seg_lens)
        ps, _ = _pooled(ext, md, pos, length, gw)
        gup = g_ref[...]

        @pl.when(i == 0)
        def _():
            dps_ref[...] = jnp.zeros(dps_ref.shape, F32)
            dpw_ref[...] = jnp.zeros(dpw_ref.shape, F32)

        dg1 = []
        for g in range(pg):
            cols = slice(g * gw, (g + 1) * gw)
            pb = ps[g].astype(BF16)
            yp = _dot(pb, pw_ref[g])
            sc = ps_ref[:, cols]
            dg1.append(_rowsum(gup[:, cols] * (yp * sc)))
            dy = gup[:, cols] * md[G1:G1 + 1, cols]
            dps_ref[0:1, cols] += _rowsum(dy * yp)
            dyp = (dy * sc).astype(BF16)
            dp_ref[:, cols] = _dot_nt(dyp, pw_ref[g])
            dpw_ref[g] += _dot_tn(pb, dyp)
        _acc_rows(dmd_ref, _first_of_group(i, nct, groups), {G1: jnp.concatenate(dg1, axis=1)})

    j = layer // N_MIXERS
    return pl.pallas_call(
        body, name=f"pool_bwd_w_{layer}", grid=(rows // tm,),
        in_specs=[*_halo_specs(tm, d, rows), _rows(tm, d), _mods_spec(nct, groups, d),
                  pl.BlockSpec((None, pg, gw, gw), lambda i: (j, 0, 0, 0)), _full((1, d))],
        out_specs=[_rows(tm, d), _mods_spec(nct, groups, d), _full((8, d)), _full((pg, gw, gw))],
        out_shape=[jax.ShapeDtypeStruct((rows, d), F32), jax.ShapeDtypeStruct((groups, 8, d), F32),
                   jax.ShapeDtypeStruct((8, d), F32), jax.ShapeDtypeStruct((pg, gw, gw), F32)],
        compiler_params=_params(),
    )(h, h, h, dh1, mods, pw, pscale[j:j + 1])


def _pool_bwd_input(dp, h, dh1, mods, layer, *, nct, tm, seg_lens, gw):
    rows, d = h.shape
    groups = mods.shape[0]

    def body(prev_ref, cur_ref, next_ref, h_ref, g_ref, md_ref, dh_ref, dmd_ref):
        i = pl.program_id(0)
        md = md_ref[...]
        dp_cur = cur_ref[...]
        ext = jnp.concatenate([prev_ref[...], dp_cur, next_ref[...]], axis=0)
        pos, length = _segment_positions(i, tm, nct, groups, seg_lens)
        valid = jnp.logical_and(pos >= 0, pos < length)
        das = []
        for g, w in enumerate(POOL_WINDOWS):
            cols = slice(g * gw, (g + 1) * gw)
            q = jnp.where(valid, ext[:, cols] * (1.0 / jnp.maximum(_window_count(pos, length, w), 1.0)), 0.0)
            das.append(_window_sum(q, w, w // 2 - 1) - dp_cur[:, cols])
        da = jnp.concatenate(das, axis=1)
        _, parts = _normmod(h_ref[...], md, 0)
        dx, dsh, dsc, dng = _normmod_bwd(da, parts, md, 0)
        dh_ref[...] = g_ref[...] + dx
        _acc_rows(dmd_ref, _first_of_group(i, nct, groups), {SH1: dsh, SC1: dsc, NG0: dng})

    return pl.pallas_call(
        body, name=f"pool_bwd_x_{layer}", grid=(rows // tm,),
        in_specs=[*_halo_specs(tm, d, rows), _rows(tm, d), _rows(tm, d), _mods_spec(nct, groups, d)],
        out_specs=[pl.BlockSpec((tm, d), lambda i: (jnp.maximum(i - nct, 0), 0)), _mods_spec(nct, groups, d)],
        out_shape=[jax.ShapeDtypeStruct((rows - nct * tm, d), F32), jax.ShapeDtypeStruct((groups, 8, d), F32)],
        compiler_params=_params(),
    )(dp, dp, dp, h, dh1, mods)


def _rope_tables(n_ctx, seq, hd):
    half = hd // 2
    t = jnp.arange(seq)
    row = (t // GRID_W).astype(F32)
    col = (t % GRID_W).astype(F32)
    inv = ROPE_BASE ** (-jnp.arange(0, half, 2, dtype=F32) / half)
    ar = row[:, None] * inv[None, :]
    ac = col[:, None] * inv[None, :]
    cos = jnp.concatenate([jnp.cos(ar), jnp.cos(ar), jnp.cos(ac), jnp.cos(ac)], axis=1)
    sin = jnp.concatenate([-jnp.sin(ar), jnp.sin(ar), -jnp.sin(ac), jnp.sin(ac)], axis=1)
    cos = jnp.concatenate([jnp.ones((n_ctx, hd), F32), cos], axis=0)
    sin = jnp.concatenate([jnp.zeros((n_ctx, hd), F32), sin], axis=0)
    return cos, sin


def _rope_partner(x):
    hd = x.shape[-1]
    q = hd // 4
    lane = lax.broadcasted_iota(jnp.int32, x.shape, 1)
    first = (lane % (2 * q)) < q
    return jnp.where(first, pltpu.roll(x, hd - q, axis=1), pltpu.roll(x, q, axis=1))


def _normmod_call(h, mods, which, name, *, nct, tm):
    rows, d = h.shape
    groups = mods.shape[0]

    def body(h_ref, md_ref, a_ref):
        a, _ = _normmod(h_ref[...], md_ref[...], which)
        a_ref[...] = a.astype(BF16)

    return pl.pallas_call(
        body, name=name, grid=(rows // tm,),
        in_specs=[_rows(tm, d), _mods_spec(nct, groups, d)],
        out_specs=_rows(tm, d), out_shape=jax.ShapeDtypeStruct((rows, d), BF16),
        compiler_params=_params(),
    )(h, mods)


def _qkv_fwd(xa, wqkv, cos, sin, gains, *, nh, nkv, nct, tm):
    rows, d = xa.shape
    qw = wqkv.shape[0] * wqkv.shape[-1]
    hd = cos.shape[-1]

    def body(x_ref, w_ref, cos_ref, sin_ref, gn_ref, qkv_ref, q_ref, k_ref, v_ref):
        qkv = _dot_blocks(x_ref[...], w_ref)
        qkv_ref[...] = qkv
        c, s = cos_ref[...], sin_ref[...]
        for hh in range(nh + nkv):
            xh = qkv[:, hh * hd:(hh + 1) * hd]
            xhat, _ = _rms_parts(xh)
            y = xhat * (gn_ref[0:1, :] if hh < nh else gn_ref[1:2, :])
            rot = (y * c + _rope_partner(y) * s).astype(BF16)
            if hh < nh:
                q_ref[:, hh * hd:(hh + 1) * hd] = rot
            else:
                k_ref[:, (hh - nh) * hd:(hh - nh + 1) * hd] = rot
        v_ref[...] = qkv[:, (nh + nkv) * hd:].astype(BF16)

    return pl.pallas_call(
        body, name="attn_qkv_fwd", grid=(rows // tm,),
        in_specs=[_rows(tm, d), _full(wqkv.shape), _rows(tm, hd), _rows(tm, hd), _full((8, hd))],
        out_specs=[_rows(tm, qw), pl.BlockSpec((tm, nh * hd), lambda i: (jnp.maximum(i - nct, 0), 0)),
                   _rows(tm, nkv * hd), _rows(tm, nkv * hd)],
        out_shape=[jax.ShapeDtypeStruct((rows, qw), F32), jax.ShapeDtypeStruct((rows - nct * tm, nh * hd), BF16),
                   jax.ShapeDtypeStruct((rows, nkv * hd), BF16), jax.ShapeDtypeStruct((rows, nkv * hd), BF16)],
        compiler_params=_params(),
    )(xa, wqkv, cos, sin, gains)


ATTN_Q_TILE_CAP = 1024
ATTN_KV_TILE_CAP = 4224
ATTN_ROW_GROUP = 256
LOG2E = 1.4426950408889634


def _attn_tiles(seq, total):
    tq = _div_tile(seq, ATTN_Q_TILE_CAP)
    return tq, _div_tile(total, ATTN_KV_TILE_CAP), min(ATTN_ROW_GROUP, tq)


def _flash_fwd(q, k, v, *, n_ctx, hd):
    total = k.shape[0]
    seq = total - n_ctx
    nkv = k.shape[1] // hd
    tq, tk, rg = _attn_tiles(seq, total)
    nk = total // tk
    scale = hd ** -0.5
    c2 = scale * LOG2E

    def body(q_ref, k_ref, v_ref, o_ref, lse_ref, m_sc, l_sc, acc_sc):
        ki = pl.program_id(2)

        @pl.when(ki == 0)
        def _():
            m_sc[...] = jnp.full(m_sc.shape, -jnp.inf, F32)
            l_sc[...] = jnp.zeros(l_sc.shape, F32)
            acc_sc[...] = jnp.zeros(acc_sc.shape, F32)

        kk, vv = k_ref[...], v_ref[...]
        groups = [(g, sub) for g in range(2) for sub in range(tq // rg)]

        def scores(g, sub):
            return _dot_nt(q_ref[sub * rg:(sub + 1) * rg, g * hd:(g + 1) * hd], kk)

        s_next = scores(*groups[0])
        for idx, (g, sub) in enumerate(groups):
            s = s_next
            if idx + 1 < len(groups):
                s_next = scores(*groups[idx + 1])
            rows = slice(g * tq + sub * rg, g * tq + (sub + 1) * rg)
            m_old = m_sc[rows]
            m_new = jnp.maximum(m_old, jnp.max(s, axis=-1, keepdims=True))
            alpha = jnp.exp2((m_old - m_new) * c2)
            p = jnp.exp2((s - m_new) * c2)
            l_sc[rows] = alpha * l_sc[rows] + jnp.sum(p, axis=-1, keepdims=True)
            acc_sc[rows] = alpha * acc_sc[rows] + _dot(p.astype(BF16), vv)
            m_sc[rows] = m_new

        @pl.when(ki == nk - 1)
        def _():
            o2 = acc_sc[...] / l_sc[...]
            lse = m_sc[...] * scale + jnp.log(l_sc[...])
            o_ref[:, :hd] = o2[:tq].astype(BF16)
            o_ref[:, hd:] = o2[tq:].astype(BF16)
            lse_ref[:, 0:1] = lse[:tq]
            lse_ref[:, 1:2] = lse[tq:]

    return pl.pallas_call(
        body, name="attn_flash_fwd", grid=(nkv, seq // tq, nk),
        in_specs=[pl.BlockSpec((tq, 2 * hd), lambda h, i, j: (i, h)),
                  pl.BlockSpec((tk, hd), lambda h, i, j: (j, h)),
                  pl.BlockSpec((tk, hd), lambda h, i, j: (j, h))],
        out_specs=[pl.BlockSpec((tq, 2 * hd), lambda h, i, j: (i, h)),
                   pl.BlockSpec((None, tq, 2), lambda h, i, j: (h, i, 0))],
        out_shape=[jax.ShapeDtypeStruct((seq, 2 * nkv * hd), BF16), jax.ShapeDtypeStruct((nkv, seq, 2), F32)],
        scratch_shapes=[pltpu.VMEM((2 * tq, 1), F32), pltpu.VMEM((2 * tq, 1), F32), pltpu.VMEM((2 * tq, hd), F32)],
        compiler_params=_params(),
    )(q, k, v)


def _flash_bwd(q, k, v, o, do, lse, *, n_ctx, hd):
    total = k.shape[0]
    seq = total - n_ctx
    nkv = k.shape[1] // hd
    tq, tk, rg = _attn_tiles(seq, total)
    scale = hd ** -0.5
    c2 = scale * LOG2E

    def body(q_ref, k_ref, v_ref, o_ref, do_ref, lse_ref, dq_ref, dk_ref, dv_ref):
        ki, qi = pl.program_id(1), pl.program_id(2)
        kk, vv = k_ref[...], v_ref[...]

        @pl.when(qi == 0)
        def _():
            dk_ref[...] = jnp.zeros(dk_ref.shape, F32)
            dv_ref[...] = jnp.zeros(dv_ref.shape, F32)

        dk_acc = jnp.zeros((tk, hd), F32)
        dv_acc = jnp.zeros((tk, hd), F32)
        for g in range(2):
            for sub in range(tq // rg):
                rs = slice(sub * rg, (sub + 1) * rg)
                cs = slice(g * hd, (g + 1) * hd)
                qq = q_ref[rs, cs]
                dd = do_ref[rs, cs]
                delta = jnp.sum(dd.astype(F32) * o_ref[rs, cs].astype(F32), axis=-1, keepdims=True)
                p = jnp.exp2(_dot_nt(qq, kk) * c2 - lse_ref[rs, g:g + 1] * LOG2E)
                dp = _dot_nt(dd, vv)
                ds = (p * (dp - delta) * scale).astype(BF16)
                dv_acc = dv_acc + _dot_tn(p.astype(BF16), dd)
                dk_acc = dk_acc + _dot_tn(ds, qq)
                dq = _dot(ds, kk)
                rows = pl.ds(pl.multiple_of(qi * tq, tq) + sub * rg, rg)

                @pl.when(ki == 0)
                def _():
                    dq_ref[rows, cs] = dq

                @pl.when(ki > 0)
                def _():
                    dq_ref[rows, cs] += dq
        dk_ref[...] += dk_acc
        dv_ref[...] += dv_acc

    return pl.pallas_call(
        body, name="attn_flash_bwd", grid=(nkv, total // tk, seq // tq),
        in_specs=[pl.BlockSpec((tq, 2 * hd), lambda h, j, i: (i, h)),
                  pl.BlockSpec((tk, hd), lambda h, j, i: (j, h)),
                  pl.BlockSpec((tk, hd), lambda h, j, i: (j, h)),
                  pl.BlockSpec((tq, 2 * hd), lambda h, j, i: (i, h)),
                  pl.BlockSpec((tq, 2 * hd), lambda h, j, i: (i, h)),
                  pl.BlockSpec((None, tq, 2), lambda h, j, i: (h, i, 0))],
        out_specs=[pl.BlockSpec((seq, 2 * hd), lambda h, j, i: (0, h)),
                   pl.BlockSpec((tk, hd), lambda h, j, i: (j, h)),
                   pl.BlockSpec((tk, hd), lambda h, j, i: (j, h))],
        out_shape=[jax.ShapeDtypeStruct((seq, 2 * nkv * hd), F32), jax.ShapeDtypeStruct((total, nkv * hd), F32),
                   jax.ShapeDtypeStruct((total, nkv * hd), F32)],
        compiler_params=_params(),
    )(q, k, v, o, do, lse)


def _proj_fwd(o, wo, hc, mods, *, n_ctx, tm):
    seq, d = o.shape
    off = n_ctx // tm

    def body(o_ref, w_ref, h_ref, md_ref, h1_ref, y_ref):
        y = _dot(o_ref[...], w_ref[...])
        y_ref[...] = y.astype(BF16)
        h1_ref[...] = h_ref[...] + md_ref[G1:G1 + 1, :] * y

    return pl.pallas_call(
        body, name="attn_proj_fwd", grid=(seq // tm,),
        in_specs=[_rows(tm, d), _full((d, d)),
                  pl.BlockSpec((tm, d), lambda i: (i + off, 0)), pl.BlockSpec((None, 8, d), lambda i: (1, 0, 0))],
        out_specs=[_rows(tm, d), _rows(tm, d)],
        out_shape=[jax.ShapeDtypeStruct((seq, d), F32), jax.ShapeDtypeStruct((seq, d), BF16)],
        compiler_params=_params(),
    )(o, wo, hc, mods)


def _proj_bwd(dh1, y, mods, wo, *, tm):
    seq, d = dh1.shape

    def body(g_ref, y_ref, md_ref, w_ref, do_ref, dyb_ref, dmd_ref):
        i = pl.program_id(0)
        g = g_ref[...]
        dyb = (g * md_ref[G1:G1 + 1, :]).astype(BF16)
        dyb_ref[...] = dyb
        do_ref[...] = _dot_nt(dyb, w_ref[...]).astype(BF16)
        _acc_rows(dmd_ref, i == 0, {G1: _rowsum(g * y_ref[...].astype(F32))})

    return pl.pallas_call(
        body, name="attn_proj_bwd", grid=(seq // tm,),
        in_specs=[_rows(tm, d), _rows(tm, d), pl.BlockSpec((None, 8, d), lambda i: (1, 0, 0)), _full((d, d))],
        out_specs=[_rows(tm, d), _rows(tm, d), pl.BlockSpec((None, 8, d), lambda i: (0, 0, 0))],
        out_shape=[jax.ShapeDtypeStruct((seq, d), BF16), jax.ShapeDtypeStruct((seq, d), BF16),
                   jax.ShapeDtypeStruct((1, 8, d), F32)],
        compiler_params=_params(),
    )(dh1, y, mods, wo)


def _qkv_bwd(qkv, dq, dk, dv, cos, sin, gains, *, nh, nkv, nct, tm):
    rows, qw = qkv.shape
    hd = cos.shape[-1]

    def body(qkv_ref, dq_ref, dk_ref, dv_ref, cos_ref, sin_ref, gn_ref, out_ref, dgn_ref):
        i = pl.program_id(0)
        c, s = cos_ref[...], sin_ref[...]
        is_lat = (i >= nct).astype(F32)
        dqg = jnp.zeros((1, hd), F32)
        dkg = jnp.zeros((1, hd), F32)
        for hh in range(nh + nkv):
            if hh < nh:
                dr = dq_ref[:, hh * hd:(hh + 1) * hd] * is_lat
                gn = gn_ref[0:1, :]
            else:
                dr = dk_ref[:, (hh - nh) * hd:(hh - nh + 1) * hd]
                gn = gn_ref[1:2, :]
            dy = dr * c + _rope_partner(dr * s)
            xhat, r = _rms_parts(qkv_ref[:, hh * hd:(hh + 1) * hd])
            dgh = _rowsum(dy * xhat)
            if hh < nh:
                dqg = dqg + dgh
            else:
                dkg = dkg + dgh
            dxhat = dy * gn
            dx = r * (dxhat - xhat * jnp.mean(dxhat * xhat, axis=-1, keepdims=True))
            out_ref[:, hh * hd:(hh + 1) * hd] = dx.astype(BF16)
        out_ref[:, (nh + nkv) * hd:] = dv_ref[...].astype(BF16)
        _acc_rows(dgn_ref, i == 0, {0: dqg, 1: dkg})

    return pl.pallas_call(
        body, name="attn_qkv_bwd", grid=(rows // tm,),
        in_specs=[_rows(tm, qw), pl.BlockSpec((tm, nh * hd), lambda i: (jnp.maximum(i - nct, 0), 0)),
                  _rows(tm, nkv * hd), _rows(tm, nkv * hd), _rows(tm, hd), _rows(tm, hd), _full((8, hd))],
        out_specs=[_rows(tm, qw), _full((8, hd))],
        out_shape=[jax.ShapeDtypeStruct((rows, qw), BF16), jax.ShapeDtypeStruct((8, hd), F32)],
        compiler_params=_params(),
    )(qkv, dq, dk, dv, cos, sin, gains)


def _attn_in_bwd(dqkv, wqkv, hc, dh1, mods, *, nct, tm):
    rows, d = hc.shape
    qw = dqkv.shape[1]

    def body(dz_ref, w_ref, h_ref, g_ref, md_ref, dh_ref, dmd_ref):
        i = pl.program_id(0)
        md = md_ref[...]
        da = _dot_nt_blocks(dz_ref[...], w_ref)
        _, parts = _normmod(h_ref[...], md, 0)
        dx, dsh, dsc, dng = _normmod_bwd(da, parts, md, 0)
        dh_ref[...] = g_ref[...] * (i >= nct).astype(F32) + dx
        _acc_rows(dmd_ref, _first_of_group(i, nct, 2), {SH1: dsh, SC1: dsc, NG0: dng})

    return pl.pallas_call(
        body, name="attn_in_bwd", grid=(rows // tm,),
        in_specs=[_rows(tm, qw), _full(wqkv.shape), _rows(tm, d),
                  pl.BlockSpec((tm, d), lambda i: (jnp.maximum(i - nct, 0), 0)), _mods_spec(nct, 2, d)],
        out_specs=[_rows(tm, d), _mods_spec(nct, 2, d)],
        out_shape=[jax.ShapeDtypeStruct((rows, d), F32), jax.ShapeDtypeStruct((2, 8, d), F32)],
        compiler_params=_params(),
    )(dqkv, wqkv, hc, dh1, mods)


def _gmlp_gate(zp, lng, lnb, ws_ref, bs_ref, gg, ch):
    half = zp.shape[1] // 2
    ggw = half // gg
    z = _gelu(zp)
    u, v = z[:, :half], z[:, half:]
    vc = v - jnp.mean(v, axis=-1, keepdims=True)
    rs = lax.rsqrt(jnp.mean(vc * vc, axis=-1, keepdims=True) + EPS)
    vhat = vc * rs
    vln = (vhat * lng + lnb).astype(BF16)
    chunks = []
    for n in range(zp.shape[0] // ch):
        groups = []
        for g in range(gg):
            groups.append(_dot(ws_ref[g], vln[n * ch:(n + 1) * ch, g * ggw:(g + 1) * ggw]) + bs_ref[g])
        chunks.append(jnp.concatenate(groups, axis=1))
    sv = jnp.concatenate(chunks, axis=0) if len(chunks) > 1 else chunks[0]
    return u, sv, vhat, rs, vln


def _gmlp_fwd(h, mods, w_in, lng, lnb, ws, bs, w_out, *, tm):
    seq, d = h.shape
    zw = w_in.shape[0] * w_in.shape[-1]
    half = zw // 2
    gg, ch = ws.shape[0], ws.shape[-1]

    def body(h_ref, md_ref, win_ref, lng_ref, lnb_ref, ws_ref, bs_ref, wout_ref, h1_ref, zp_ref, y_ref):
        x = h_ref[...]
        md = md_ref[...]
        a, _ = _normmod(x, md, 0)
        zp = _dot_blocks(a.astype(BF16), win_ref)
        zp_ref[...] = zp.astype(BF16)
        u, sv, _, _, _ = _gmlp_gate(zp, lng_ref[...], lnb_ref[...], ws_ref, bs_ref, gg, ch)
        y = _dot((u * sv).astype(BF16), wout_ref[...])
        y_ref[...] = y.astype(BF16)
        h1_ref[...] = x + md[G1:G1 + 1] * y

    return pl.pallas_call(
        body, name="gmlp_fwd", grid=(seq // tm,),
        in_specs=[_rows(tm, d), pl.BlockSpec((None, 8, d), lambda i: (1, 0, 0)),
                  _full(w_in.shape), _full((1, half)), _full((1, half)),
                  _full((gg, ch, ch)), _full((gg, ch, 1)), _full((half, d))],
        out_specs=[_rows(tm, d), _rows(tm, zw), _rows(tm, d)],
        out_shape=[jax.ShapeDtypeStruct((seq, d), F32), jax.ShapeDtypeStruct((seq, zw), BF16),
                   jax.ShapeDtypeStruct((seq, d), BF16)],
        compiler_params=_params(),
    )(h, mods, w_in, lng, lnb, ws, bs, w_out)


def _gmlp_bwd(h, dh1, zpre, y, mods, w_in, lng, lnb, ws, ws_t, bs, w_out, *, tm):
    seq, d = h.shape
    zw = w_in.shape[0] * w_in.shape[-1]
    half = zw // 2
    gg, ch = ws.shape[0], ws.shape[-1]
    ggw = half // gg

    def body(h_ref, g_ref, zp_ref, y_ref, md_ref, win_ref, lng_ref, lnb_ref, ws_ref, wst_ref, bs_ref, wout_ref,
             dh_ref, dzp_ref, gated_ref, dyb_ref, ab_ref, dmd_ref, dln_ref, dws_ref, dbs_ref):
        i = pl.program_id(0)
        x = h_ref[...]
        md = md_ref[...]
        a, parts = _normmod(x, md, 0)
        ab_ref[...] = a.astype(BF16)
        zp = zp_ref[...].astype(F32)
        lng_v = lng_ref[...]
        u, sv, vhat, rs, vln = _gmlp_gate(zp, lng_v, lnb_ref[...], ws_ref, bs_ref, gg, ch)
        g = g_ref[...]
        dg1 = _rowsum(g * y_ref[...].astype(F32))
        dyb = (g * md[G1:G1 + 1]).astype(BF16)
        dyb_ref[...] = dyb
        gated_ref[...] = (u * sv).astype(BF16)
        dgated = _dot_nt(dyb, wout_ref[...])
        du = dgated * sv
        dsv = dgated * u

        @pl.when(i == 0)
        def _():
            dws_ref[...] = jnp.zeros(dws_ref.shape, F32)
            dbs_ref[...] = jnp.zeros(dbs_ref.shape, F32)
            dln_ref[...] = jnp.zeros(dln_ref.shape, F32)

        chunks = []
        for n in range(tm // ch):
            groups = []
            for gi in range(gg):
                blk = dsv[n * ch:(n + 1) * ch, gi * ggw:(gi + 1) * ggw]
                dbs_ref[gi] += jnp.sum(blk, axis=-1, keepdims=True)
                blk_b = blk.astype(BF16)
                dws_ref[gi] += _dot_nt(blk_b, vln[n * ch:(n + 1) * ch, gi * ggw:(gi + 1) * ggw])
                groups.append(_dot(wst_ref[gi], blk_b))
            chunks.append(jnp.concatenate(groups, axis=1))
        dvln = jnp.concatenate(chunks, axis=0) if len(chunks) > 1 else chunks[0]
        dln_ref[0:1, :] += _rowsum(dvln * vhat)
        dln_ref[1:2, :] += _rowsum(dvln)
        dvhat = dvln * lng_v
        dv = rs * (dvhat - jnp.mean(dvhat, axis=-1, keepdims=True)
                   - vhat * jnp.mean(dvhat * vhat, axis=-1, keepdims=True))
        dzp = (jnp.concatenate([du, dv], axis=1) * _gelu_grad(zp)).astype(BF16)
        dzp_ref[...] = dzp
        da = _dot_nt_blocks(dzp, win_ref)
        dx, dsh, dsc, dng = _normmod_bwd(da, parts, md, 0)
        dh_ref[...] = g + dx
        _acc_rows(dmd_ref, i == 0, {SH1: dsh, SC1: dsc, G1: dg1, NG0: dng})

    return pl.pallas_call(
        body, name="gmlp_bwd", grid=(seq // tm,),
        in_specs=[_rows(tm, d), _rows(tm, d), _rows(tm, zw), _rows(tm, d),
                  pl.BlockSpec((None, 8, d), lambda i: (1, 0, 0)),
                  _full(w_in.shape), _full((1, half)), _full((1, half)),
                  _full((gg, ch, ch)), _full((gg, ch, ch)), _full((gg, ch, 1)), _full((half, d))],
        out_specs=[_rows(tm, d), _rows(tm, zw), _rows(tm, half), _rows(tm, d), _rows(tm, d),
                   pl.BlockSpec((None, 8, d), lambda i: (0, 0, 0)), _full((8, half)), _full((gg, ch, ch)),
                   _full((gg, ch, 1))],
        out_shape=[jax.ShapeDtypeStruct((seq, d), F32), jax.ShapeDtypeStruct((seq, zw), BF16),
                   jax.ShapeDtypeStruct((seq, half), BF16), jax.ShapeDtypeStruct((seq, d), BF16),
                   jax.ShapeDtypeStruct((seq, d), BF16), jax.ShapeDtypeStruct((1, 8, d), F32),
                   jax.ShapeDtypeStruct((8, half), F32), jax.ShapeDtypeStruct((gg, ch, ch), F32),
                   jax.ShapeDtypeStruct((gg, ch, 1), F32)],
        compiler_params=_params(),
    )(h, dh1, zpre, y, mods, w_in, lng, lnb, ws, ws_t, bs, w_out)


def _final_loss(h, tgt, fg, *, tm):
    seq, d = h.shape

    def body(h_ref, t_ref, g_ref, dh_ref, acc_ref):
        i = pl.program_id(0)
        gain = g_ref[...]
        xhat, r = _rms_parts(h_ref[...])
        err = xhat * gain - t_ref[...]
        dy = err * (1.0 / d)
        dxhat = dy * gain
        dh_ref[...] = r * (dxhat - xhat * jnp.mean(dxhat * xhat, axis=-1, keepdims=True))
        part = jnp.sum(_rowsum(err * err), axis=-1, keepdims=True) * (0.5 / d)
        _acc_rows(acc_ref, i == 0, {0: _rowsum(dy * xhat), 1: jnp.broadcast_to(part, (1, d))})

    return pl.pallas_call(
        body, name="final_loss", grid=(seq // tm,),
        in_specs=[_rows(tm, d), _rows(tm, d), _full((1, d))],
        out_specs=[_rows(tm, d), _full((8, d))],
        out_shape=[jax.ShapeDtypeStruct((seq, d), F32), jax.ShapeDtypeStruct((8, d), F32)],
        compiler_params=_params(),
    )(h, tgt, fg)


def _ada_fwd(c_all, ada_w, ada_b_cols):
    depth, d, ncs = ada_w.shape

    def body(c_ref, w_ref, b_ref, o_ref):
        s = _silu(c_ref[...]).astype(BF16)
        o_ref[...] = _dot(s, w_ref[...].astype(BF16)) + b_ref[...]

    return pl.pallas_call(
        body, name="ada_fwd", grid=(depth,),
        in_specs=[_full((16, d)), pl.BlockSpec((None, d, ncs), lambda i: (i, 0, 0)),
                  pl.BlockSpec((None, 1, ncs), lambda i: (i, 0, 0))],
        out_specs=pl.BlockSpec((None, 16, ncs), lambda i: (i, 0, 0)),
        out_shape=jax.ShapeDtypeStruct((depth, 16, ncs), F32),
        compiler_params=_params(),
    )(c_all, ada_w, ada_b_cols.reshape(depth, 1, ncs))


def _ada_bwd(c_all, c_all_t, dmod, ada_w):
    depth, d, ncs = ada_w.shape

    def body(c_ref, ct_ref, dm_ref, w_ref, gw_ref, dc_ref):
        i = pl.program_id(0)
        dm = dm_ref[...]
        dctx = _rowsum(dm[8:16])
        rid = lax.broadcasted_iota(jnp.int32, (8, ncs), 0)
        low = jnp.where(rid == 0, jnp.broadcast_to(dctx, (8, ncs)), 0.0)
        dm16 = jnp.concatenate([dm[0:8], low], axis=0).astype(BF16)
        gw_ref[...] = _dot(_silu(ct_ref[...]).astype(BF16), dm16)

        @pl.when(i == 0)
        def _():
            dc_ref[...] = jnp.zeros(dc_ref.shape, F32)

        dc_ref[...] += _dot_nt(low.astype(BF16), w_ref[...].astype(BF16)) * _silu_grad(c_ref[8:9, :])

    return pl.pallas_call(
        body, name="ada_bwd", grid=(depth,),
        in_specs=[_full((16, d)), _full((d, 16)), pl.BlockSpec((None, 16, ncs), lambda i: (i, 0, 0)),
                  pl.BlockSpec((None, d, ncs), lambda i: (i, 0, 0))],
        out_specs=[pl.BlockSpec((None, d, ncs), lambda i: (i, 0, 0)), _full((8, d))],
        out_shape=[jax.ShapeDtypeStruct((depth, d, ncs), F32), jax.ShapeDtypeStruct((8, d), F32)],
        compiler_params=_params(),
    )(c_all, c_all_t, dmod, ada_w)


def _adamw_math(w, g, m, v):
    m = ADAM_B1 * m + (1.0 - ADAM_B1) * g
    v = ADAM_B2 * v + (1.0 - ADAM_B2) * jnp.square(g)
    m_hat = m * (1.0 / (1.0 - ADAM_B1 ** ADAM_STEP))
    v_hat = v * (1.0 / (1.0 - ADAM_B2 ** ADAM_STEP))
    delta = -ADAM_LR * (m_hat / (jnp.sqrt(v_hat) + ADAM_EPS) + ADAM_WD * w)
    return delta, m, v


def _adamw(ga, gb, w, m, v, name):
    rows, cols = w.shape
    tr = rows
    while tr * cols * 4 > (1 << 20) and tr % 16 == 0:
        tr //= 2
    grads = [ga] if gb is None else [ga, gb]

    def body(*refs):
        w_ref, m_ref, v_ref, g_out, d_out, m_out, v_out = refs[len(grads):]
        g = refs[0][...] if gb is None else refs[0][...] + refs[1][...]
        delta, m_new, v_new = _adamw_math(w_ref[...], g, m_ref[...], v_ref[...])
        g_out[...] = g
        d_out[...] = delta
        m_out[...] = m_new
        v_out[...] = v_new

    spec = _rows(tr, cols)
    return pl.pallas_call(
        body, name=name, grid=(rows // tr,),
        in_specs=[spec] * (len(grads) + 3), out_specs=[spec] * 4,
        out_shape=[jax.ShapeDtypeStruct((rows, cols), F32)] * 4,
        compiler_params=_params(),
    )(*grads, w, m, v)


def _sum_devices(gathered, name):
    n, rows, cols = gathered.shape
    tr = rows
    while tr * cols * 4 * n > (4 << 20) and tr % 16 == 0:
        tr //= 2

    def body(x_ref, o_ref):
        acc = x_ref[0]
        for j in range(1, n):
            acc = acc + x_ref[j]
        o_ref[...] = acc

    return pl.pallas_call(
        body, name=name, grid=(rows // tr,),
        in_specs=[pl.BlockSpec((n, tr, cols), lambda i: (0, i, 0))], out_specs=_rows(tr, cols),
        out_shape=jax.ShapeDtypeStruct((rows, cols), F32),
        compiler_params=_params(),
    )(gathered)


def _sum_partials(owns, landeds, name):
    n = len(owns)
    cols = owns[0].shape[-1]
    owns = [o.reshape(-1, cols) for o in owns]
    landeds = [l.reshape(3, -1, cols) for l in landeds]
    rows = owns[0].shape[0]
    tr = rows
    while tr * cols * 2 * n > (1 << 20) and tr % 32 == 0:
        tr //= 2

    def body(*refs):
        out_ref = refs[-1]
        for li in range(n):
            acc = refs[li][...].astype(F32)
            for p in range(3):
                acc = acc + refs[n + li][p].astype(F32)
            out_ref[li] = acc

    out = pl.pallas_call(
        body, name=name, grid=(rows // tr,),
        in_specs=[_rows(tr, cols)] * n + [pl.BlockSpec((3, tr, cols), lambda i: (0, i, 0))] * n,
        out_specs=pl.BlockSpec((n, tr, cols), lambda i: (0, i, 0)),
        out_shape=jax.ShapeDtypeStruct((n, rows, cols), F32),
        compiler_params=_params(),
    )(*owns, *landeds)
    return out.reshape(n * rows, cols)


def _my_place():
    return lax.axis_index("x"), lax.axis_index("y"), lax.axis_index("c")


def _other_chips(x, y):
    return [(1 - x, y), (x, 1 - y), (1 - x, 1 - y)]


def _all_gather_small(block, name):
    rows, cols = block.shape

    def body(x_ref, out_ref, send_sems, recv_sems, local_sem):
        x, y, c = _my_place()
        me, sibling = (x, y, c), (x, y, 1 - c)
        chips = _other_chips(x, y)

        def slot(px, py, pc):
            return out_ref.at[4 * px + 2 * py + pc]

        def copy(k, blk, to, src=None):
            return pltpu.make_async_remote_copy(
                src_ref=slot(*blk) if src is None else src, dst_ref=slot(*blk),
                send_sem=send_sems.at[k], recv_sem=recv_sems.at[k], device_id=to, device_id_type=MESH)

        mine = pltpu.make_async_copy(x_ref, slot(*me), local_sem)
        mine.start()
        first = [copy(0, me, sibling, src=x_ref)]
        first += [copy(1 + j, me, (*chip, c), src=x_ref) for j, chip in enumerate(chips)]
        for cp in first:
            cp.start()
        passed = [copy(4 + j, (*chip, c), sibling) for j, chip in enumerate(chips)]
        for j, chip in enumerate(chips):
            copy(1 + j, (*chip, c), me).wait_recv()
            passed[j].start()
        copy(0, sibling, me).wait_recv()
        for j, chip in enumerate(chips):
            copy(4 + j, (*chip, 1 - c), me).wait_recv()
        for cp in first + passed:
            cp.wait_send()
        mine.wait()

    return pl.pallas_call(
        body, name=name,
        out_shape=jax.ShapeDtypeStruct((N_DEV, rows, cols), block.dtype),
        in_specs=[pl.BlockSpec(memory_space=pltpu.VMEM)],
        out_specs=pl.BlockSpec(memory_space=pltpu.VMEM),
        scratch_shapes=[pltpu.SemaphoreType.DMA((7,)), pltpu.SemaphoreType.DMA((7,)), pltpu.SemaphoreType.DMA],
        compiler_params=_params(),
    )(block)


HBM_SPEC = pl.BlockSpec(memory_space=pltpu.HBM)
SEM_SPEC = pl.BlockSpec(memory_space=pltpu.SEMAPHORE)
DATAFLOW_EFFECT = pltpu.SideEffectType.DATAFLOW_SIDE_EFFECTING


def _gather_views(src, land, p, x, y):
    return src, land.at[2 * x + y]


def _scatter_views(src, land, p, x, y):
    peer_chip = (2 * (1 - x) + y, 2 * x + (1 - y), 2 * (1 - x) + (1 - y))[p]
    return src.at[peer_chip], land.at[p]


def _chip_exchange_copies(srcs, lands, send_sems, recv_sems, views):
    x, y, c = _my_place()
    copies = []
    for j, (src, land) in enumerate(zip(srcs, lands)):
        for p, chip in enumerate(_other_chips(x, y)):
            s_view, d_view = views(src, land, p, x, y)
            copies.append(pltpu.make_async_remote_copy(
                src_ref=s_view, dst_ref=d_view, send_sem=send_sems.at[3 * j + p], recv_sem=recv_sems.at[3 * j + p],
                device_id=(*chip, c), device_id_type=MESH))
    return copies


def _exchange_start(srcs, lands, views, name):
    n = len(srcs)

    def body(*refs):
        send_sems, recv_sems = refs[2 * n], refs[2 * n + 1]
        token = refs[-1]
        for cp in _chip_exchange_copies(refs[:n], refs[n:2 * n], send_sems, recv_sems, views):
            cp.start()
        token[...] = jnp.zeros(token.shape, token.dtype)

    operands = [pltpu.with_memory_space_constraint(a, pltpu.HBM) for a in (*srcs, *lands)]
    out = pl.pallas_call(
        body, name=name,
        out_shape=(pltpu.SemaphoreType.DMA((3 * n,)), pltpu.SemaphoreType.DMA((3 * n,)),
                   *[pltpu.HBM(a.shape, a.dtype) for a in operands], jax.ShapeDtypeStruct((8, LANES), F32)),
        in_specs=[HBM_SPEC] * (2 * n),
        out_specs=(SEM_SPEC, SEM_SPEC, *[HBM_SPEC] * (2 * n), pl.BlockSpec(memory_space=pltpu.VMEM)),
        input_output_aliases={i: 2 + i for i in range(2 * n)},
        compiler_params=pltpu.CompilerParams(has_side_effects=DATAFLOW_EFFECT),
    )(*operands)
    return out[0], out[1], list(out[2:2 + n]), list(out[2 + n:2 + 2 * n]), out[-1]


def _exchange_wait(send_sems, recv_sems, srcs, lands, views, after, name):
    n = len(srcs)

    def body(*refs):
        send, recv = refs[2 * n], refs[2 * n + 1]
        for cp in _chip_exchange_copies(refs[:n], refs[n:2 * n], send, recv, views):
            cp.wait_send()
            cp.wait_recv()

    out = pl.pallas_call(
        body, name=name,
        out_shape=tuple(pltpu.HBM(a.shape, a.dtype) for a in (*srcs, *lands)),
        in_specs=[HBM_SPEC] * (2 * n) + [SEM_SPEC, SEM_SPEC, HBM_SPEC],
        out_specs=tuple([HBM_SPEC] * (2 * n)),
        input_output_aliases={i: i for i in range(2 * n)},
        compiler_params=pltpu.CompilerParams(has_side_effects=DATAFLOW_EFFECT),
    )(*srcs, *lands, send_sems, recv_sems, pltpu.with_memory_space_constraint(after, pltpu.HBM))
    return list(out[:n]), list(out[n:])


def _landing_for_gather(shard, chip):
    land = lax.empty((N_CHIPS, *shard.shape), shard.dtype)
    return lax.dynamic_update_index_in_dim(land, shard, chip, 0)


def _swap_with_sibling(parts):
    n = len(parts)

    def body(*refs):
        ins, outs = refs[:n], refs[n:2 * n]
        send_sems, recv_sems = refs[2 * n:]
        x, y, c = _my_place()
        copies = [pltpu.make_async_remote_copy(
            src_ref=ins[j], dst_ref=outs[j], send_sem=send_sems.at[j], recv_sem=recv_sems.at[j],
            device_id=(x, y, 1 - c), device_id_type=MESH) for j in range(n)]
        for cp in copies:
            cp.start()
        for cp in copies:
            cp.wait()

    any_spec = pl.BlockSpec(memory_space=pl.ANY)
    return pl.pallas_call(
        body, name="swap_with_sibling", out_shape=[jax.ShapeDtypeStruct(p.shape, p.dtype) for p in parts],
        in_specs=[any_spec] * n, out_specs=[any_spec] * n,
        scratch_shapes=[pltpu.SemaphoreType.DMA((n,)), pltpu.SemaphoreType.DMA((n,))],
        compiler_params=_params(),
    )(*parts)


TILE_ELEMS = SUBLANES * LANES


def _pack(arrays):
    parts = []
    for a in arrays:
        flat = a.reshape(-1).astype(F32)
        pad = (-flat.shape[0]) % TILE_ELEMS
        if pad:
            flat = jnp.concatenate([flat, jnp.zeros((pad,), F32)])
        parts.append(flat.reshape(-1, LANES))
    return jnp.concatenate(parts, axis=0) if len(parts) > 1 else parts[0]


def _unpack(buf, shapes):
    out, r = [], 0
    lead = buf.shape[:-2]
    for shp in shapes:
        size = math.prod(shp)
        nr = -(-size // TILE_ELEMS) * SUBLANES
        flat = buf[..., r:r + nr, :].reshape(*lead, nr * LANES)[..., :size]
        out.append(flat.reshape(*lead, *shp))
        r += nr
    return out


def _chip_cols(a, k, width):
    return lax.dynamic_slice_in_dim(a, k * width, width, axis=a.ndim - 1)


def _across_chips(gathered, c0_only_shape):
    return gathered.reshape(2, 2, 2, *c0_only_shape)[:, :, 0].reshape(N_CHIPS, *c0_only_shape)


def kernel(x, c, ctx, c_ctx, ada_w, ada_b, norm_g, mlp_w1, mlp_w2, pool_w, pool_scale, attn_w_qkv, attn_w_o, attn_q_g, attn_k_g, gm_w_in, gm_ln_g, gm_ln_b, gm_ws, gm_bs, gm_w_out, final_g, loss_target, m_c_ctx, m_ada_w, m_ada_b, m_norm_g, m_mlp_w1, m_mlp_w2, m_pool_w, m_pool_scale, m_attn_w_qkv, m_attn_w_o, m_attn_q_g, m_attn_k_g, m_gm_w_in, m_gm_ln_g, m_gm_ln_b, m_gm_ws, m_gm_bs, m_gm_w_out, m_final_g, v_c_ctx, v_ada_w, v_ada_b, v_norm_g, v_mlp_w1, v_mlp_w2, v_pool_w, v_pool_scale, v_attn_w_qkv, v_attn_w_o, v_attn_q_g, v_attn_k_g, v_gm_w_in, v_gm_ln_g, v_gm_ln_b, v_gm_ws, v_gm_bs, v_gm_w_out, v_final_g):
    seq, d = x.shape[1], x.shape[2]
    n_ctx = ctx.shape[1]
    total = n_ctx + seq
    hd = attn_q_g.shape[-1]
    nh = d // hd
    nkv = nh // 2
    gg, ch = gm_ws.shape[1], gm_ws.shape[-1]
    half = gm_w_out.shape[1] * N_CHIPS
    pgw = pool_w.shape[-1]
    tm = min(256, n_ctx)
    nct = n_ctx // tm
    seg_lens = (n_ctx, seq)

    mx, my, mc = _my_place()
    chip = 2 * mx + my
    me = 4 * mx + 2 * my + mc

    c_rows = jnp.concatenate([c, jnp.zeros((7, d), F32)], axis=0)
    c_gath = _all_gather_small(c_rows, "gather_cond")[:, 0, :]
    c_all = jnp.concatenate([c_gath, c_ctx[None, :], jnp.zeros((7, d), F32)], axis=0)
    ncs = ada_w.shape[-1]
    ada_cols = _ada_fwd(c_all, ada_w, _chip_cols(ada_b, chip, ncs))
    small_shapes = [ada_cols.shape, norm_g.shape, pool_scale.shape, gm_ln_g.shape, gm_ln_b.shape]
    gathered = _all_gather_small(_pack([ada_cols, norm_g, pool_scale, gm_ln_g, gm_ln_b]), "gather_small_params")
    per_chip = _across_chips(gathered, gathered.shape[1:])
    ada_g, ng_g, ps_g, lng_g, lnb_g = _unpack(per_chip, small_shapes)

    def join_last(a):
        return jnp.moveaxis(a, 0, -2).reshape(*a.shape[1:-1], N_CHIPS * a.shape[-1])

    ada_full = join_last(ada_g)
    ng_full = join_last(ng_g)
    ps_full = join_last(ps_g)
    lng_full = join_last(lng_g)
    lnb_full = join_last(lnb_g)
    mod_lat = lax.dynamic_slice_in_dim(ada_full, me, 1, axis=1).reshape(DEPTH, 6, d)
    mod_ctx = ada_full[:, 8].reshape(DEPTH, 6, d)
    mods = jnp.stack([jnp.concatenate([mod_ctx, ng_full], axis=1), jnp.concatenate([mod_lat, ng_full], axis=1)],
                     axis=1)

    weight_groups = [
        [mlp_w1[0], mlp_w2[0], pool_w],
        [mlp_w1[1], mlp_w2[1], attn_w_qkv[0], attn_w_o[0]],
        [mlp_w1[2], mlp_w2[2], gm_w_in[0], gm_w_out[0], mlp_w1[3], mlp_w2[3]],
    ]
    gathers = [None] * len(weight_groups)

    def gather_start(gi, after):
        shards, _ = lax.optimization_barrier(([w.astype(BF16) for w in weight_groups[gi]], after))
        lands = [_landing_for_gather(s, chip) for s in shards]
        gathers[gi] = _exchange_start(shards, lands, _gather_views, f"gather_weights_{gi}_start")
        return gathers[gi][4][0:1, 0:1]

    def gathered(gi, after):
        send, recv, srcs, lands, _ = gathers[gi]
        return _exchange_wait(send, recv, srcs, lands, _gather_views, after, f"gather_weights_{gi}_wait")[1]

    def rows_joined(a):
        return a.reshape(-1, a.shape[-1])

    w1_b, w2_b = [None] * DEPTH, [None] * DEPTH
    gather_start(0, mods)
    w1_b[0], w2_b[0], pw_land = gathered(0, ps_full)
    mods0 = mods[0] + gather_start(1, w1_b[0])
    pw_f = jnp.transpose(pw_land, (1, 2, 0, 3, 4)).reshape(pool_w.shape[0], pool_w.shape[1], pgw, pgw)

    gains = jnp.concatenate([attn_q_g, attn_k_g, jnp.zeros((6, hd), F32)], axis=0)
    ws_b = gm_ws[0].astype(BF16)
    ws_t = jnp.swapaxes(gm_ws[0], 1, 2).astype(BF16)
    bs_col = gm_bs[0][:, :, None]
    cos, sin = _rope_tables(n_ctx, seq, hd)
    lat = lambda i: mods[i, 1:2]

    hc0 = jnp.concatenate([ctx[0], x[0]], axis=0)
    ha0 = _pool_fwd(hc0, mods0, pw_f, ps_full, 0, nct=nct, tm=tm, seg_lens=seg_lens)
    hc1, u0, o0 = _mlp_fwd(ha0, mods0, w1_b[0], w2_b[0], 0, nct=nct, tm=tm)
    w1_b[1], w2_b[1], wqkv_b, wo_land = gathered(1, hc1)
    mods1 = mods[1] + gather_start(2, w1_b[1])
    wo_f = rows_joined(wo_land)
    xa1 = _normmod_call(hc1, mods1, 0, "attn_in_fwd", nct=nct, tm=tm)
    qkv, q_r, k_r, v_b = _qkv_fwd(xa1, wqkv_b, cos, sin, gains, nh=nh, nkv=nkv, nct=nct, tm=tm)
    o_att, lse = _flash_fwd(q_r, k_r, v_b, n_ctx=n_ctx, hd=hd)
    ha1, y1 = _proj_fwd(o_att, wo_f, hc1, mods1, n_ctx=n_ctx, tm=tm)
    h2, u1, o1 = _mlp_fwd(ha1, lat(1), w1_b[1], w2_b[1], 1, nct=0, tm=tm)
    w1_b[2], w2_b[2], win_b, wout_land, w1_b[3], w2_b[3] = gathered(2, h2)
    wout_f = rows_joined(wout_land)
    ha2, zpre, y2 = _gmlp_fwd(h2, mods[2], win_b, lng_full, lnb_full, ws_b, bs_col, wout_f, tm=tm)
    h3, u2, o2 = _mlp_fwd(ha2, lat(2), w1_b[2], w2_b[2], 2, nct=0, tm=tm)
    ha3 = _pool_fwd(h3, lat(3), pw_f, ps_full, 3, nct=0, tm=tm, seg_lens=seg_lens)
    h4, u3, o3 = _mlp_fwd(ha3, lat(3), w1_b[3], w2_b[3], 3, nct=0, tm=tm)
    dh4, fin_acc = _final_loss(h4, loss_target[0], final_g[None, :], tm=tm)

    dmods = [None] * DEPTH
    scatters = [None] * DEPTH

    def blocked_rows(g):
        return g.reshape(N_CHIPS, g.shape[1] // N_CHIPS, g.shape[2])

    def blocked_pool(dpw):
        pg = dpw.shape[0]
        return jnp.transpose(dpw.astype(BF16).reshape(pg, N_CHIPS, pgw // N_CHIPS, pgw), (1, 0, 2, 3))

    def scatter_start(i, grads):
        lands = [lax.empty((3, *g.shape[1:]), g.dtype) for g in grads]
        scatters[i] = _exchange_start(grads, lands, _scatter_views, f"scatter_grads_{i}_start")
        return scatters[i][4][0:1, 0:1]

    def mlp_back(i, h_in, dh_out, u, o, md, n_ct):
        dh_in, du, dob, mb, dmd = _mlp_bwd(h_in, dh_out, u, o, md, w1_b[i], w2_b[i], i, nct=n_ct, tm=tm)
        dw1 = _mm_tn(mb, du, f"mlp_dw1_{i}", col_blocks=N_CHIPS)
        dw2 = blocked_rows(_mm_tn(u, dob, f"mlp_dw2_{i}", relu2=True))
        return dh_in, dmd, [dw1, dw2]

    def pool_back(i, h_in, dh_out, md, n_ct):
        dp, dmd_a, dps, dpw = _pool_bwd_weights(h_in, dh_out, md, pw_f, ps_full, i, nct=n_ct, tm=tm,
                                                seg_lens=seg_lens)
        dh_in, dmd_b = _pool_bwd_input(dp, h_in, dh_out, md, i, nct=n_ct, tm=tm, seg_lens=seg_lens, gw=pgw)
        return dh_in, dmd_a + dmd_b, dps, dpw

    zero_grp = jnp.zeros((1, 8, d), F32)
    dha3, dmd3, dws3 = mlp_back(3, ha3, dh4, u3, o3, lat(3), 0)
    dh3, dmd3p, dps3, dpw3 = pool_back(3, h3, dha3, lat(3), 0)
    dmods[3] = jnp.concatenate([zero_grp, dmd3 + dmd3p], axis=0)
    tok = scatter_start(3, dws3 + [blocked_pool(dpw3)])
    dha2, dmd2, dws2 = mlp_back(2, ha2, dh3, u2, o2, lat(2) + tok, 0)
    dh2, dzpre, gated, dyb2, ab2, dmd2g, dln, dws, dbs = _gmlp_bwd(
        h2, dha2, zpre, y2, mods[2], win_b, lng_full, lnb_full, ws_b, ws_t, bs_col, wout_f, tm=tm)
    dwin = _mm_tn(ab2, dzpre, "gmlp_dw_in", col_blocks=N_CHIPS)
    dwout = blocked_rows(_mm_tn(gated, dyb2, "gmlp_dw_out"))
    dmods[2] = jnp.concatenate([zero_grp, dmd2 + dmd2g], axis=0)
    tok = scatter_start(2, dws2 + [dwin, dwout])
    dha1, dmd1, dws1 = mlp_back(1, ha1, dh2, u1, o1, lat(1) + tok, 0)
    do_att, dyb1, dmd1p = _proj_bwd(dha1, y1, mods[1], wo_f, tm=tm)
    dwo = blocked_rows(_mm_tn(o_att, dyb1, "attn_dw_o"))
    dq, dk, dv = _flash_bwd(q_r, k_r, v_b, o_att, do_att, lse, n_ctx=n_ctx, hd=hd)
    dqkv, dgains = _qkv_bwd(qkv, dq, dk, dv, cos, sin, gains, nh=nh, nkv=nkv, nct=nct, tm=tm)
    dwqkv = _mm_tn(xa1, dqkv, "attn_dw_qkv", col_blocks=N_CHIPS)
    dhc1, dmd1i = _attn_in_bwd(dqkv, wqkv_b, hc1, dha1, mods[1], nct=nct, tm=tm)
    dmods[1] = dmd1i + jnp.concatenate([zero_grp, dmd1 + dmd1p], axis=0)
    tok = scatter_start(1, dws1 + [dwqkv, dwo])
    dha0, dmd0, dws0 = mlp_back(0, ha0, dhc1, u0, o0, mods[0] + tok, nct)
    dhc0, dmd0p, dps0, dpw0 = pool_back(0, hc0, dha0, mods[0], nct)
    dmods[0] = dmd0 + dmd0p
    grad_x = dhc0[None]

    dmods_all = jnp.stack(dmods, axis=0)
    small_grads = [dmods_all, dws, dbs, dgains, dln, dps0, dps3, fin_acc]
    sg_shapes = [a.shape for a in small_grads]
    sg_gath = _all_gather_small(_pack(small_grads), "gather_small_grads")
    sg_sum = _sum_devices(sg_gath, "sum_small_grads")
    s_dmods, s_dws, s_dbs, s_dgains, s_dln, s_dps0, s_dps3, s_fin = _unpack(sg_sum, sg_shapes)
    loss = s_fin[1, 0]

    dm_dev = _unpack(sg_gath, sg_shapes[:1])[0]
    dm_lat = jnp.moveaxis(dm_dev[:, :, 1, :6, :], 0, 1).reshape(DEPTH, N_DEV, 6 * d)
    dm_ctx = jnp.moveaxis(dm_dev[:, :, 0, :6, :], 0, 1).reshape(DEPTH, N_DEV, 6 * d)
    dmod16 = _chip_cols(jnp.concatenate([dm_lat, dm_ctx], axis=1), chip, ncs)
    g_ada_w, dcc_part = _ada_bwd(c_all, c_all.T, dmod16, ada_w)
    dcc_gath = _all_gather_small(dcc_part, "gather_d_c_ctx")
    dcc_chips = _across_chips(dcc_gath, dcc_gath.shape[1:])
    dcc = _sum_devices(dcc_chips, "sum_d_c_ctx")[0]

    grads0, _ = lax.optimization_barrier((dws0 + [blocked_pool(dpw0)], dcc))
    after_last_start = sg_sum + scatter_start(0, grads0)
    own, landed = [None] * DEPTH, [None] * DEPTH
    for i in (3, 2, 1, 0):
        send, recv, srcs, lands, _ = scatters[i]
        srcs, landed[i] = _exchange_wait(send, recv, srcs, lands, _scatter_views, after_last_start,
                                         f"scatter_grads_{i}_wait")
        own[i] = [lax.dynamic_index_in_dim(s, chip, 0, keepdims=False) for s in srcs]

    def summed(name, picks):
        return _sum_partials([own[i][j] for i, j in picks], [landed[i][j] for i, j in picks], f"sum_chips_{name}")

    big = [("mlp_w1", mlp_w1, m_mlp_w1, v_mlp_w1, [(i, 0) for i in range(DEPTH)]),
           ("mlp_w2", mlp_w2, m_mlp_w2, v_mlp_w2, [(i, 1) for i in range(DEPTH)]),
           ("pool_w", pool_w, m_pool_w, v_pool_w, [(0, 2), (3, 2)]),
           ("attn_w_qkv", attn_w_qkv, m_attn_w_qkv, v_attn_w_qkv, [(1, 2)]),
           ("attn_w_o", attn_w_o, m_attn_w_o, v_attn_w_o, [(1, 3)]),
           ("gm_w_in", gm_w_in, m_gm_w_in, v_gm_w_in, [(2, 2)]),
           ("gm_w_out", gm_w_out, m_gm_w_out, v_gm_w_out, [(2, 3)])]
    partial = [summed(name, picks) for name, _, _, _, picks in big]
    from_sibling = _swap_with_sibling(partial)
    big_out = {}
    for (name, w, m, v, _), mine, theirs in zip(big, partial, from_sibling):
        cols = w.shape[-1]
        res = _adamw(mine, theirs, w.reshape(-1, cols), m.reshape(-1, cols), v.reshape(-1, cols), f"adamw_{name}")
        big_out[name] = [r.reshape(w.shape) for r in res]
    ada_res = _adamw(g_ada_w.reshape(-1, ncs), None, ada_w.reshape(-1, ncs),
                     m_ada_w.reshape(-1, ncs), v_ada_w.reshape(-1, ncs), "adamw_ada_w")
    big_out["ada_w"] = [r.reshape(ada_w.shape) for r in ada_res]

    def cols_of(a, width):
        return _chip_cols(a, chip, width)

    zero = lambda a: jnp.zeros(a.shape, F32)
    ngw = norm_g.shape[-1]
    small = {
        "c_ctx": (dcc, zero(dcc), c_ctx, m_c_ctx, v_c_ctx),
        "ada_b": (s_dmods[:, 0, :6].reshape(DEPTH, 6 * d), s_dmods[:, 1, :6].reshape(DEPTH, 6 * d), ada_b, m_ada_b,
                  v_ada_b),
        "norm_g": (cols_of(s_dmods[:, 0, 6:8], ngw), cols_of(s_dmods[:, 1, 6:8], ngw), norm_g, m_norm_g, v_norm_g),
        "pool_scale": (cols_of(jnp.stack([s_dps0[0], s_dps3[0]]), pool_scale.shape[-1]), zero(pool_scale),
                       pool_scale, m_pool_scale, v_pool_scale),
        "attn_q_g": (s_dgains[0:1], zero(attn_q_g), attn_q_g, m_attn_q_g, v_attn_q_g),
        "attn_k_g": (s_dgains[1:2], zero(attn_k_g), attn_k_g, m_attn_k_g, v_attn_k_g),
        "gm_ln_g": (cols_of(s_dln[0:1], gm_ln_g.shape[-1]), zero(gm_ln_g), gm_ln_g, m_gm_ln_g, v_gm_ln_g),
        "gm_ln_b": (cols_of(s_dln[1:2], gm_ln_b.shape[-1]), zero(gm_ln_b), gm_ln_b, m_gm_ln_b, v_gm_ln_b),
        "gm_ws": (s_dws[None], zero(gm_ws), gm_ws, m_gm_ws, v_gm_ws),
        "gm_bs": (s_dbs[None, :, :, 0], zero(gm_bs), gm_bs, m_gm_bs, v_gm_bs),
        "final_g": (s_fin[0], zero(final_g), final_g, m_final_g, v_final_g),
    }
    keys = list(small)
    packed = [_pack([small[k][t] for k in keys]) for t in range(5)]
    res = _adamw(*packed, "adamw_small")
    shapes = [small[k][2].shape for k in keys]
    small_out = {k: [] for k in keys}
    for r in res:
        for k, a in zip(keys, _unpack(r, shapes)):
            small_out[k].append(a)

    order = ["c_ctx", "ada_w", "ada_b", "norm_g", "mlp_w1", "mlp_w2", "pool_w", "pool_scale", "attn_w_qkv",
             "attn_w_o", "attn_q_g", "attn_k_g", "gm_w_in", "gm_ln_g", "gm_ln_b", "gm_ws", "gm_bs", "gm_w_out",
             "final_g"]
    allo = {**big_out, **small_out}
    outs = [loss, grad_x]
    for t in range(4):
        outs += [allo[k][t] for k in order]
    return tuple(outs)
```

```python
import functools
import math

import numpy as np
import jax
import jax.numpy as jnp
from jax import lax
from jax.experimental import pallas as pl
from jax.experimental.pallas import tpu as pltpu

F32 = jnp.float32
BF16 = jnp.bfloat16
MESH = pl.DeviceIdType.MESH

EPS = 1e-6
GRID_W = 64
ROPE_BASE = 10000.0
POOL_WINDOWS = (2, 4, 8, 16)
HALO = 8
DEPTH = 4
N_MIXERS = 3

ADAM_LR = 0.001
ADAM_B1 = 0.9
ADAM_B2 = 0.999
ADAM_EPS = 1e-08
ADAM_WD = 0.01
ADAM_STEP = 10

VMEM_LIMIT_BYTES = 56 * 1024 * 1024
LANES = 128
SUBLANES = 8
N_DEV = 8
N_CHIPS = 4

SH1, SC1, G1, SH2, SC2, G2, NG0, NG1 = range(8)


def _dot(a, b):
    return jnp.dot(a, b, preferred_element_type=F32)


def _dot_nt(a, b):
    return lax.dot_general(a, b, (((1,), (1,)), ((), ())), preferred_element_type=F32)


def _dot_tn(a, b):
    return lax.dot_general(a, b, (((0,), (0,)), ((), ())), preferred_element_type=F32)


def _dot_blocks(a, w_ref):
    return jnp.concatenate([_dot(a, w_ref[k]) for k in range(w_ref.shape[0])], axis=1)


def _dot_nt_blocks(a, w_ref):
    nb, _, w = w_ref.shape
    acc = _dot_nt(a[:, 0:w], w_ref[0])
    for k in range(1, nb):
        acc = acc + _dot_nt(a[:, k * w:(k + 1) * w], w_ref[k])
    return acc


def _params(**kw):
    return pltpu.CompilerParams(vmem_limit_bytes=VMEM_LIMIT_BYTES, **kw)


def _full(shape):
    nd = len(shape)
    return pl.BlockSpec(shape, lambda *_: (0,) * nd)


def _rows(tm, width):
    return pl.BlockSpec((tm, width), lambda i: (i, 0))


def _group_of(nct, groups):
    if groups == 1:
        return lambda i: 0
    return lambda i: jnp.where(i >= nct, 1, 0)


def _mods_spec(nct, groups, d):
    grp = _group_of(nct, groups)
    return pl.BlockSpec((None, 8, d), lambda i: (grp(i), 0, 0))


def _first_of_group(i, nct, groups):
    if groups == 1:
        return i == 0
    return jnp.logical_or(i == 0, i == nct)


def _rowsum(v):
    return jnp.sum(v, axis=0, keepdims=True)


def _rms_parts(x):
    r = lax.rsqrt(jnp.mean(x * x, axis=-1, keepdims=True) + EPS)
    return x * r, r


def _normmod(x, md, which):
    ng, sh, sc = (md[NG0:NG0 + 1], md[SH1:SH1 + 1], md[SC1:SC1 + 1]) if which == 0 else (
        md[NG1:NG1 + 1], md[SH2:SH2 + 1], md[SC2:SC2 + 1])
    xhat, r = _rms_parts(x)
    n = xhat * ng
    return n * (1.0 + sc) + sh, (xhat, r, n)


def _normmod_bwd(da, parts, md, which):
    xhat, r, n = parts
    ng, sc = (md[NG0:NG0 + 1], md[SC1:SC1 + 1]) if which == 0 else (md[NG1:NG1 + 1], md[SC2:SC2 + 1])
    dsh = _rowsum(da)
    dsc = _rowsum(da * n)
    dn = da * (1.0 + sc)
    dng = _rowsum(dn * xhat)
    dxhat = dn * ng
    dx = r * (dxhat - xhat * jnp.mean(dxhat * xhat, axis=-1, keepdims=True))
    return dx, dsh, dsc, dng


def _acc_rows(ref, first, rows):
    @pl.when(first)
    def _():
        ref[...] = jnp.zeros(ref.shape, ref.dtype)

    for r, v in rows.items():
        ref[r:r + 1, :] += v


def _shift_up(x, k):
    if k == 0:
        return x
    return pltpu.roll(x, x.shape[0] - k, axis=0)


def _gelu(x):
    k = math.sqrt(2.0 / math.pi)
    return 0.5 * x * (1.0 + jnp.tanh(k * (x + 0.044715 * x * x * x)))


def _gelu_grad(x):
    k = math.sqrt(2.0 / math.pi)
    t = jnp.tanh(k * (x + 0.044715 * x * x * x))
    return 0.5 * (1.0 + t) + 0.5 * x * (1.0 - t * t) * k * (1.0 + 3.0 * 0.044715 * x * x)


def _silu(x):
    return x / (1.0 + jnp.exp(-x))


def _silu_grad(x):
    s = 1.0 / (1.0 + jnp.exp(-x))
    return s * (1.0 + x * (1.0 - s))


def _mlp_fwd(h, mods, w1, w2, layer, *, nct, tm):
    rows, d = h.shape
    groups = mods.shape[0]
    nb, _, fc = w1.shape
    ff = nb * fc

    def body(h_ref, md_ref, w1_ref, w2_ref, h2_ref, u_ref, o_ref):
        x = h_ref[...]
        md = md_ref[...]
        m, _ = _normmod(x, md, 1)
        mb = m.astype(BF16)
        acc = jnp.zeros((tm, d), F32)
        for k in range(nb):
            u = _dot(mb, w1_ref[k])
            u_ref[:, k * fc:(k + 1) * fc] = u.astype(BF16)
            acc = acc + _dot(jnp.square(jnp.maximum(u, 0.0)).astype(BF16), w2_ref[k])
        o_ref[...] = acc.astype(BF16)
        h2_ref[...] = x + md[G2:G2 + 1] * acc

    return pl.pallas_call(
        body, name=f"mlp_fwd_{layer}", grid=(rows // tm,),
        in_specs=[_rows(tm, d), _mods_spec(nct, groups, d), _full(w1.shape), _full(w2.shape)],
        out_specs=[_rows(tm, d), _rows(tm, ff), _rows(tm, d)],
        out_shape=[jax.ShapeDtypeStruct((rows, d), F32), jax.ShapeDtypeStruct((rows, ff), BF16),
                   jax.ShapeDtypeStruct((rows, d), BF16)],
        compiler_params=_params(),
    )(h, mods, w1, w2)


def _mlp_bwd(h1, dh2, u, o, mods, w1, w2, layer, *, nct, tm):
    rows, d = h1.shape
    groups = mods.shape[0]
    nb, _, fc = w1.shape
    ff = nb * fc

    def body(h_ref, g_ref, u_ref, o_ref, md_ref, w1_ref, w2_ref, dh_ref, du_ref, dob_ref, mb_ref, dmd_ref):
        i = pl.program_id(0)
        x = h_ref[...]
        g = g_ref[...]
        md = md_ref[...]
        m, parts = _normmod(x, md, 1)
        mb_ref[...] = m.astype(BF16)
        dg2 = _rowsum(g * o_ref[...].astype(F32))
        dob = (g * md[G2:G2 + 1]).astype(BF16)
        dob_ref[...] = dob
        dm = jnp.zeros((tm, d), F32)
        for k in range(nb):
            uk = u_ref[:, k * fc:(k + 1) * fc].astype(F32)
            dr = _dot_nt(dob, w2_ref[k])
            duk = (dr * (2.0 * jnp.maximum(uk, 0.0))).astype(BF16)
            du_ref[:, k * fc:(k + 1) * fc] = duk
            dm = dm + _dot_nt(duk, w1_ref[k])
        dx, dsh, dsc, dng = _normmod_bwd(dm, parts, md, 1)
        dh_ref[...] = g + dx
        _acc_rows(dmd_ref, _first_of_group(i, nct, groups), {SH2: dsh, SC2: dsc, G2: dg2, NG1: dng})

    return pl.pallas_call(
        body, name=f"mlp_bwd_{layer}", grid=(rows // tm,),
        in_specs=[_rows(tm, d), _rows(tm, d), _rows(tm, ff), _rows(tm, d), _mods_spec(nct, groups, d),
                  _full(w1.shape), _full(w2.shape)],
        out_specs=[_rows(tm, d), _rows(tm, ff), _rows(tm, d), _rows(tm, d), _mods_spec(nct, groups, d)],
        out_shape=[jax.ShapeDtypeStruct((rows, d), F32), jax.ShapeDtypeStruct((rows, ff), BF16),
                   jax.ShapeDtypeStruct((rows, d), BF16), jax.ShapeDtypeStruct((rows, d), BF16),
                   jax.ShapeDtypeStruct((groups, 8, d), F32)],
        compiler_params=_params(),
    )(h1, dh2, u, o, mods, w1, w2)


def _div_tile(n, cap):
    if n <= cap:
        return n
    return max(t for t in range(LANES, cap + 1, LANES) if n % t == 0)


def _mm_tn(a, b, name, *, relu2=False, col_blocks=1):
    rows, m = a.shape
    n = b.shape[1]
    tmm = min(m, 1024)
    tn = min(n // col_blocks, 2048)
    per_block = n // col_blocks // tn
    tr = _div_tile(rows, 1024)

    def body(a_ref, b_ref, o_ref, acc_ref):
        r = pl.program_id(2)

        @pl.when(r == 0)
        def _():
            acc_ref[...] = jnp.zeros(acc_ref.shape, F32)

        av = a_ref[...]
        if relu2:
            av = jnp.square(jnp.maximum(av.astype(F32), 0.0)).astype(BF16)
        acc_ref[...] += _dot_tn(av, b_ref[...])

        @pl.when(r == pl.num_programs(2) - 1)
        def _():
            o_ref[...] = acc_ref[...].astype(BF16)

    return pl.pallas_call(
        body, name=name, grid=(m // tmm, n // tn, rows // tr),
        in_specs=[pl.BlockSpec((tr, tmm), lambda i, j, r: (r, i)), pl.BlockSpec((tr, tn), lambda i, j, r: (r, j))],
        out_specs=pl.BlockSpec((None, tmm, tn), lambda i, j, r: (j // per_block, i, j % per_block)),
        out_shape=jax.ShapeDtypeStruct((col_blocks, m, n // col_blocks), BF16),
        scratch_shapes=[pltpu.VMEM((tmm, tn), F32)],
        compiler_params=_params(),
    )(a, b)


def _halo_specs(tm, d, rows):
    per = tm // HALO
    prev = pl.BlockSpec((HALO, d), lambda i: (jnp.maximum(i * per - 1, 0), 0))
    nxt = pl.BlockSpec((HALO, d), lambda i: (jnp.minimum((i + 1) * per, rows // HALO - 1), 0))
    return prev, _rows(tm, d), nxt


def _segment_positions(i, tm, nct, groups, seg_lens):
    if groups == 1:
        start, length = 0, seg_lens[-1]
    else:
        start = jnp.where(i >= nct, nct, 0)
        length = jnp.where(i >= nct, seg_lens[1], seg_lens[0])
    rid = lax.broadcasted_iota(jnp.int32, (tm + 2 * HALO, 1), 0)
    pos = (i - start) * tm - HALO + rid
    return pos, length


def _window_count(pos, length, w):
    hi = jnp.minimum(pos + (w - w // 2), length)
    lo = jnp.maximum(pos - w // 2, 0)
    return (hi - lo).astype(F32)


def _window_sum(xg, w, lead):
    b, k = xg, 1
    while k < w:
        b = b + _shift_up(b, k)
        k *= 2
    return _shift_up(b, HALO - lead)[0:xg.shape[0] - 2 * HALO]


def _pooled(ext, md, pos, length, gw):
    tm = ext.shape[0] - 2 * HALO
    a_ext, parts = _normmod(ext, md, 0)
    valid = jnp.logical_and(pos >= 0, pos < length)
    a_ext = jnp.where(valid, a_ext, 0.0)
    pos_c = pos[HALO:HALO + tm]
    ps = []
    for g, w in enumerate(POOL_WINDOWS):
        xg = a_ext[:, g * gw:(g + 1) * gw]
        s = _window_sum(xg, w, w // 2)
        ps.append(s * (1.0 / _window_count(pos_c, length, w)) - xg[HALO:HALO + tm])
    return ps, parts


def _pool_fwd(h, mods, pw, pscale, layer, *, nct, tm, seg_lens):
    rows, d = h.shape
    groups = mods.shape[0]
    pg, gw = pw.shape[1], pw.shape[-1]

    def body(prev_ref, cur_ref, next_ref, md_ref, pw_ref, ps_ref, out_ref):
        i = pl.program_id(0)
        md = md_ref[...]
        cur = cur_ref[...]
        ext = jnp.concatenate([prev_ref[...], cur, next_ref[...]], axis=0)
        pos, length = _segment_positions(i, tm, nct, groups, seg_lens)
        ps, _ = _pooled(ext, md, pos, length, gw)
        for g in range(pg):
            yg = _dot(ps[g].astype(BF16), pw_ref[g]) * ps_ref[:, g * gw:(g + 1) * gw]
            out_ref[:, g * gw:(g + 1) * gw] = cur[:, g * gw:(g + 1) * gw] + md[G1:G1 + 1, g * gw:(g + 1) * gw] * yg

    j = layer // N_MIXERS
    return pl.pallas_call(
        body, name=f"pool_fwd_{layer}", grid=(rows // tm,),
        in_specs=[*_halo_specs(tm, d, rows), _mods_spec(nct, groups, d),
                  pl.BlockSpec((None, pg, gw, gw), lambda i: (j, 0, 0, 0)), _full((1, d))],
        out_specs=_rows(tm, d),
        out_shape=jax.ShapeDtypeStruct((rows, d), F32),
        compiler_params=_params(),
    )(h, h, h, mods, pw, pscale[j:j + 1])


def _pool_bwd_weights(h, dh1, mods, pw, pscale, layer, *, nct, tm, seg_lens):
    rows, d = h.shape
    groups = mods.shape[0]
    pg, gw = pw.shape[1], pw.shape[-1]

    def body(prev_ref, cur_ref, next_ref, g_ref, md_ref, pw_ref, ps_ref, dp_ref, dmd_ref, dps_ref, dpw_ref):
        i = pl.program_id(0)
        md = md_ref[...]
        ext = jnp.concatenate([prev_ref[...], cur_ref[...], next_ref[...]], axis=0)
        pos, length = _segment_positions(i, tm, nct, groups, seg_lens)
        ps, _ = _pooled(ext, md, pos, length, gw)
        gup = g_ref[...]

        @pl.when(i == 0)
        def _():
            dps_ref[...] = jnp.zeros(dps_ref.shape, F32)
            dpw_ref[...] = jnp.zeros(dpw_ref.shape, F32)

        dg1 = []
        for g in range(pg):
            cols = slice(g * gw, (g + 1) * gw)
            pb = ps[g].astype(BF16)
            yp = _dot(pb, pw_ref[g])
            sc = ps_ref[:, cols]
            dg1.append(_rowsum(gup[:, cols] * (yp * sc)))
            dy = gup[:, cols] * md[G1:G1 + 1, cols]
            dps_ref[0:1, cols] += _rowsum(dy * yp)
            dyp = (dy * sc).astype(BF16)
            dp_ref[:, cols] = _dot_nt(dyp, pw_ref[g])
            dpw_ref[g] += _dot_tn(pb, dyp)
        _acc_rows(dmd_ref, _first_of_group(i, nct, groups), {G1: jnp.concatenate(dg1, axis=1)})

    j = layer // N_MIXERS
    return pl.pallas_call(
        body, name=f"pool_bwd_w_{layer}", grid=(rows // tm,),
        in_specs=[*_halo_specs(tm, d, rows), _rows(tm, d), _mods_spec(nct, groups, d),
                  pl.BlockSpec((None, pg, gw, gw), lambda i: (j, 0, 0, 0)), _full((1, d))],
        out_specs=[_rows(tm, d), _mods_spec(nct, groups, d), _full((8, d)), _full((pg, gw, gw))],
        out_shape=[jax.ShapeDtypeStruct((rows, d), F32), jax.ShapeDtypeStruct((groups, 8, d), F32),
                   jax.ShapeDtypeStruct((8, d), F32), jax.ShapeDtypeStruct((pg, gw, gw), F32)],
        compiler_params=_params(),
    )(h, h, h, dh1, mods, pw, pscale[j:j + 1])


def _pool_bwd_input(dp, h, dh1, mods, layer, *, nct, tm, seg_lens, gw):
    rows, d = h.shape
    groups = mods.shape[0]

    def body(prev_ref, cur_ref, next_ref, h_ref, g_ref, md_ref, dh_ref, dmd_ref):
        i = pl.program_id(0)
        md = md_ref[...]
        dp_cur = cur_ref[...]
        ext = jnp.concatenate([prev_ref[...], dp_cur, next_ref[...]], axis=0)
        pos, length = _segment_positions(i, tm, nct, groups, seg_lens)
        valid = jnp.logical_and(pos >= 0, pos < length)
        das = []
        for g, w in enumerate(POOL_WINDOWS):
            cols = slice(g * gw, (g + 1) * gw)
            q = jnp.where(valid, ext[:, cols] * (1.0 / jnp.maximum(_window_count(pos, length, w), 1.0)), 0.0)
            das.append(_window_sum(q, w, w // 2 - 1) - dp_cur[:, cols])
        da = jnp.concatenate(das, axis=1)
        _, parts = _normmod(h_ref[...], md, 0)
        dx, dsh, dsc, dng = _normmod_bwd(da, parts, md, 0)
        dh_ref[...] = g_ref[...] + dx
        _acc_rows(dmd_ref, _first_of_group(i, nct, groups), {SH1: dsh, SC1: dsc, NG0: dng})

    return pl.pallas_call(
        body, name=f"pool_bwd_x_{layer}", grid=(rows // tm,),
        in_specs=[*_halo_specs(tm, d, rows), _rows(tm, d), _rows(tm, d), _mods_spec(nct, groups, d)],
        out_specs=[pl.BlockSpec((tm, d), lambda i: (jnp.maximum(i - nct, 0), 0)), _mods_spec(nct, groups, d)],
        out_shape=[jax.ShapeDtypeStruct((rows - nct * tm, d), F32), jax.ShapeDtypeStruct((groups, 8, d), F32)],
        compiler_params=_params(),
    )(dp, dp, dp, h, dh1, mods)


def _rope_tables(n_ctx, seq, hd):
    half = hd // 2
    n_rows = seq // GRID_W
    inv = np.float32(ROPE_BASE) ** (-np.arange(0, half, 2, dtype=np.float32) / np.float32(half))
    ar = np.arange(n_rows, dtype=np.float32)[:, None] * inv[None, :]
    ac = np.arange(GRID_W, dtype=np.float32)[:, None] * inv[None, :]

    def over_tokens(row_part, col_part):
        r = jnp.repeat(jnp.asarray(row_part, F32), GRID_W, axis=0)
        c = jnp.tile(jnp.asarray(col_part, F32), (n_rows, 1))
        return r, c

    cr, cc = over_tokens(np.cos(ar), np.cos(ac))
    sr, sc = over_tokens(np.sin(ar), np.sin(ac))
    cos = jnp.concatenate([cr, cr, cc, cc], axis=1)
    sin = jnp.concatenate([-sr, sr, -sc, sc], axis=1)
    cos = jnp.concatenate([jnp.ones((n_ctx, hd), F32), cos], axis=0)
    sin = jnp.concatenate([jnp.zeros((n_ctx, hd), F32), sin], axis=0)
    return cos, sin


def _rope_partner(x):
    hd = x.shape[-1]
    q = hd // 4
    lane = lax.broadcasted_iota(jnp.int32, x.shape, 1)
    first = (lane % (2 * q)) < q
    return jnp.where(first, pltpu.roll(x, hd - q, axis=1), pltpu.roll(x, q, axis=1))


def _normmod_call(h, mods, which, name, *, nct, tm):
    rows, d = h.shape
    groups = mods.shape[0]

    def body(h_ref, md_ref, a_ref):
        a, _ = _normmod(h_ref[...], md_ref[...], which)
        a_ref[...] = a.astype(BF16)

    return pl.pallas_call(
        body, name=name, grid=(rows // tm,),
        in_specs=[_rows(tm, d), _mods_spec(nct, groups, d)],
        out_specs=_rows(tm, d), out_shape=jax.ShapeDtypeStruct((rows, d), BF16),
        compiler_params=_params(),
    )(h, mods)


def _qkv_fwd(xa, wqkv, cos, sin, gains, *, nh, nkv, nct, tm):
    rows, d = xa.shape
    qw = wqkv.shape[0] * wqkv.shape[-1]
    hd = cos.shape[-1]

    def body(x_ref, w_ref, cos_ref, sin_ref, gn_ref, qkv_ref, q_ref, k_ref, v_ref):
        qkv = _dot_blocks(x_ref[...], w_ref)
        qkv_ref[...] = qkv
        c, s = cos_ref[...], sin_ref[...]
        for hh in range(nh + nkv):
            xh = qkv[:, hh * hd:(hh + 1) * hd]
            xhat, _ = _rms_parts(xh)
            y = xhat * (gn_ref[0:1, :] if hh < nh else gn_ref[1:2, :])
            rot = (y * c + _rope_partner(y) * s).astype(BF16)
            if hh < nh:
                q_ref[:, hh * hd:(hh + 1) * hd] = rot
            else:
                k_ref[:, (hh - nh) * hd:(hh - nh + 1) * hd] = rot
        v_ref[...] = qkv[:, (nh + nkv) * hd:].astype(BF16)

    return pl.pallas_call(
        body, name="attn_qkv_fwd", grid=(rows // tm,),
        in_specs=[_rows(tm, d), _full(wqkv.shape), _rows(tm, hd), _rows(tm, hd), _full((8, hd))],
        out_specs=[_rows(tm, qw), pl.BlockSpec((tm, nh * hd), lambda i: (jnp.maximum(i - nct, 0), 0)),
                   _rows(tm, nkv * hd), _rows(tm, nkv * hd)],
        out_shape=[jax.ShapeDtypeStruct((rows, qw), F32), jax.ShapeDtypeStruct((rows - nct * tm, nh * hd), BF16),
                   jax.ShapeDtypeStruct((rows, nkv * hd), BF16), jax.ShapeDtypeStruct((rows, nkv * hd), BF16)],
        compiler_params=_params(),
    )(xa, wqkv, cos, sin, gains)


ATTN_Q_TILE_CAP = 1024
ATTN_KV_TILE_CAP = 4224
ATTN_ROW_GROUP = 256
LOG2E = 1.4426950408889634


def _attn_tiles(seq, total):
    tq = _div_tile(seq, ATTN_Q_TILE_CAP)
    return tq, _div_tile(total, ATTN_KV_TILE_CAP), min(ATTN_ROW_GROUP, tq)


def _flash_fwd(q, k, v, *, n_ctx, hd):
    total = k.shape[0]
    seq = total - n_ctx
    nkv = k.shape[1] // hd
    tq, tk, rg = _attn_tiles(seq, total)
    nk = total // tk
    scale = hd ** -0.5
    c2 = scale * LOG2E

    def body(q_ref, k_ref, v_ref, o_ref, lse_ref, m_sc, l_sc, acc_sc):
        ki = pl.program_id(2)

        @pl.when(ki == 0)
        def _():
            m_sc[...] = jnp.full(m_sc.shape, -jnp.inf, F32)
            l_sc[...] = jnp.zeros(l_sc.shape, F32)
            acc_sc[...] = jnp.zeros(acc_sc.shape, F32)

        kk, vv = k_ref[...], v_ref[...]
        groups = [(g, sub) for g in range(2) for sub in range(tq // rg)]

        def scores(g, sub):
            return _dot_nt(q_ref[sub * rg:(sub + 1) * rg, g * hd:(g + 1) * hd], kk)

        s_next = scores(*groups[0])
        for idx, (g, sub) in enumerate(groups):
            s = s_next
            if idx + 1 < len(groups):
                s_next = scores(*groups[idx + 1])
            rows = slice(g * tq + sub * rg, g * tq + (sub + 1) * rg)
            m_old = m_sc[rows]
            m_new = jnp.maximum(m_old, jnp.max(s, axis=-1, keepdims=True))
            alpha = jnp.exp2((m_old - m_new) * c2)
            p = jnp.exp2((s - m_new) * c2)
            l_sc[rows] = alpha * l_sc[rows] + jnp.sum(p, axis=-1, keepdims=True)
            acc_sc[rows] = alpha * acc_sc[rows] + _dot(p.astype(BF16), vv)
            m_sc[rows] = m_new

        @pl.when(ki == nk - 1)
        def _():
            o2 = acc_sc[...] / l_sc[...]
            lse = m_sc[...] * scale + jnp.log(l_sc[...])
            o_ref[:, :hd] = o2[:tq].astype(BF16)
            o_ref[:, hd:] = o2[tq:].astype(BF16)
            lse_ref[:, 0:1] = lse[:tq]
            lse_ref[:, 1:2] = lse[tq:]

    return pl.pallas_call(
        body, name="attn_flash_fwd", grid=(nkv, seq // tq, nk),
        in_specs=[pl.BlockSpec((tq, 2 * hd), lambda h, i, j: (i, h)),
                  pl.BlockSpec((tk, hd), lambda h, i, j: (j, h)),
                  pl.BlockSpec((tk, hd), lambda h, i, j: (j, h))],
        out_specs=[pl.BlockSpec((tq, 2 * hd), lambda h, i, j: (i, h)),
                   pl.BlockSpec((None, tq, 2), lambda h, i, j: (h, i, 0))],
        out_shape=[jax.ShapeDtypeStruct((seq, 2 * nkv * hd), BF16), jax.ShapeDtypeStruct((nkv, seq, 2), F32)],
        scratch_shapes=[pltpu.VMEM((2 * tq, 1), F32), pltpu.VMEM((2 * tq, 1), F32), pltpu.VMEM((2 * tq, hd), F32)],
        compiler_params=_params(),
    )(q, k, v)


def _flash_bwd(q, k, v, o, do, lse, *, n_ctx, hd):
    total = k.shape[0]
    seq = total - n_ctx
    nkv = k.shape[1] // hd
    tq, tk, rg = _attn_tiles(seq, total)
    scale = hd ** -0.5
    c2 = scale * LOG2E

    def body(q_ref, k_ref, v_ref, o_ref, do_ref, lse_ref, dq_ref, dk_ref, dv_ref):
        ki, qi = pl.program_id(1), pl.program_id(2)
        kk, vv = k_ref[...], v_ref[...]

        @pl.when(qi == 0)
        def _():
            dk_ref[...] = jnp.zeros(dk_ref.shape, F32)
            dv_ref[...] = jnp.zeros(dv_ref.shape, F32)

        dk_acc = jnp.zeros((tk, hd), F32)
        dv_acc = jnp.zeros((tk, hd), F32)
        for g in range(2):
            for sub in range(tq // rg):
                rs = slice(sub * rg, (sub + 1) * rg)
                cs = slice(g * hd, (g + 1) * hd)
                qq = q_ref[rs, cs]
                dd = do_ref[rs, cs]
                delta = jnp.sum(dd.astype(F32) * o_ref[rs, cs].astype(F32), axis=-1, keepdims=True)
                p = jnp.exp2(_dot_nt(qq, kk) * c2 - lse_ref[rs, g:g + 1] * LOG2E)
                dp = _dot_nt(dd, vv)
                ds = (p * (dp - delta) * scale).astype(BF16)
                dv_acc = dv_acc + _dot_tn(p.astype(BF16), dd)
                dk_acc = dk_acc + _dot_tn(ds, qq)
                dq = _dot(ds, kk)
                rows = pl.ds(pl.multiple_of(qi * tq, tq) + sub * rg, rg)

                @pl.when(ki == 0)
                def _():
                    dq_ref[rows, cs] = dq

                @pl.when(ki > 0)
                def _():
                    dq_ref[rows, cs] += dq
        dk_ref[...] += dk_acc
        dv_ref[...] += dv_acc

    return pl.pallas_call(
        body, name="attn_flash_bwd", grid=(nkv, total // tk, seq // tq),
        in_specs=[pl.BlockSpec((tq, 2 * hd), lambda h, j, i: (i, h)),
                  pl.BlockSpec((tk, hd), lambda h, j, i: (j, h)),
                  pl.BlockSpec((tk, hd), lambda h, j, i: (j, h)),
                  pl.BlockSpec((tq, 2 * hd), lambda h, j, i: (i, h)),
                  pl.BlockSpec((tq, 2 * hd), lambda h, j, i: (i, h)),
                  pl.BlockSpec((None, tq, 2), lambda h, j, i: (h, i, 0))],
        out_specs=[pl.BlockSpec((seq, 2 * hd), lambda h, j, i: (0, h)),
                   pl.BlockSpec((tk, hd), lambda h, j, i: (j, h)),
                   pl.BlockSpec((tk, hd), lambda h, j, i: (j, h))],
        out_shape=[jax.ShapeDtypeStruct((seq, 2 * nkv * hd), F32), jax.ShapeDtypeStruct((total, nkv * hd), F32),
                   jax.ShapeDtypeStruct((total, nkv * hd), F32)],
        compiler_params=_params(),
    )(q, k, v, o, do, lse)


def _proj_fwd(o, wo, hc, mods, *, n_ctx, tm):
    seq, d = o.shape
    off = n_ctx // tm

    def body(o_ref, w_ref, h_ref, md_ref, h1_ref, y_ref):
        y = _dot(o_ref[...], w_ref[...])
        y_ref[...] = y.astype(BF16)
        h1_ref[...] = h_ref[...] + md_ref[G1:G1 + 1, :] * y

    return pl.pallas_call(
        body, name="attn_proj_fwd", grid=(seq // tm,),
        in_specs=[_rows(tm, d), _full((d, d)),
                  pl.BlockSpec((tm, d), lambda i: (i + off, 0)), pl.BlockSpec((None, 8, d), lambda i: (1, 0, 0))],
        out_specs=[_rows(tm, d), _rows(tm, d)],
        out_shape=[jax.ShapeDtypeStruct((seq, d), F32), jax.ShapeDtypeStruct((seq, d), BF16)],
        compiler_params=_params(),
    )(o, wo, hc, mods)


def _proj_bwd(dh1, y, mods, wo, *, tm):
    seq, d = dh1.shape

    def body(g_ref, y_ref, md_ref, w_ref, do_ref, dyb_ref, dmd_ref):
        i = pl.program_id(0)
        g = g_ref[...]
        dyb = (g * md_ref[G1:G1 + 1, :]).astype(BF16)
        dyb_ref[...] = dyb
        do_ref[...] = _dot_nt(dyb, w_ref[...]).astype(BF16)
        _acc_rows(dmd_ref, i == 0, {G1: _rowsum(g * y_ref[...].astype(F32))})

    return pl.pallas_call(
        body, name="attn_proj_bwd", grid=(seq // tm,),
        in_specs=[_rows(tm, d), _rows(tm, d), pl.BlockSpec((None, 8, d), lambda i: (1, 0, 0)), _full((d, d))],
        out_specs=[_rows(tm, d), _rows(tm, d), pl.BlockSpec((None, 8, d), lambda i: (0, 0, 0))],
        out_shape=[jax.ShapeDtypeStruct((seq, d), BF16), jax.ShapeDtypeStruct((seq, d), BF16),
                   jax.ShapeDtypeStruct((1, 8, d), F32)],
        compiler_params=_params(),
    )(dh1, y, mods, wo)


def _qkv_bwd(qkv, dq, dk, dv, cos, sin, gains, *, nh, nkv, nct, tm):
    rows, qw = qkv.shape
    hd = cos.shape[-1]

    def body(qkv_ref, dq_ref, dk_ref, dv_ref, cos_ref, sin_ref, gn_ref, out_ref, dgn_ref):
        i = pl.program_id(0)
        c, s = cos_ref[...], sin_ref[...]
        is_lat = (i >= nct).astype(F32)
        dqg = jnp.zeros((1, hd), F32)
        dkg = jnp.zeros((1, hd), F32)
        for hh in range(nh + nkv):
            if hh < nh:
                dr = dq_ref[:, hh * hd:(hh + 1) * hd] * is_lat
                gn = gn_ref[0:1, :]
            else:
                dr = dk_ref[:, (hh - nh) * hd:(hh - nh + 1) * hd]
                gn = gn_ref[1:2, :]
            dy = dr * c + _rope_partner(dr * s)
            xhat, r = _rms_parts(qkv_ref[:, hh * hd:(hh + 1) * hd])
            dgh = _rowsum(dy * xhat)
            if hh < nh:
                dqg = dqg + dgh
            else:
                dkg = dkg + dgh
            dxhat = dy * gn
            dx = r * (dxhat - xhat * jnp.mean(dxhat * xhat, axis=-1, keepdims=True))
            out_ref[:, hh * hd:(hh + 1) * hd] = dx.astype(BF16)
        out_ref[:, (nh + nkv) * hd:] = dv_ref[...].astype(BF16)
        _acc_rows(dgn_ref, i == 0, {0: dqg, 1: dkg})

    return pl.pallas_call(
        body, name="attn_qkv_bwd", grid=(rows // tm,),
        in_specs=[_rows(tm, qw), pl.BlockSpec((tm, nh * hd), lambda i: (jnp.maximum(i - nct, 0), 0)),
                  _rows(tm, nkv * hd), _rows(tm, nkv * hd), _rows(tm, hd), _rows(tm, hd), _full((8, hd))],
        out_specs=[_rows(tm, qw), _full((8, hd))],
        out_shape=[jax.ShapeDtypeStruct((rows, qw), BF16), jax.ShapeDtypeStruct((8, hd), F32)],
        compiler_params=_params(),
    )(qkv, dq, dk, dv, cos, sin, gains)


def _attn_in_bwd(dqkv, wqkv, hc, dh1, mods, *, nct, tm):
    rows, d = hc.shape
    qw = dqkv.shape[1]

    def body(dz_ref, w_ref, h_ref, g_ref, md_ref, dh_ref, dmd_ref):
        i = pl.program_id(0)
        md = md_ref[...]
        da = _dot_nt_blocks(dz_ref[...], w_ref)
        _, parts = _normmod(h_ref[...], md, 0)
        dx, dsh, dsc, dng = _normmod_bwd(da, parts, md, 0)
        dh_ref[...] = g_ref[...] * (i >= nct).astype(F32) + dx
        _acc_rows(dmd_ref, _first_of_group(i, nct, 2), {SH1: dsh, SC1: dsc, NG0: dng})

    return pl.pallas_call(
        body, name="attn_in_bwd", grid=(rows // tm,),
        in_specs=[_rows(tm, qw), _full(wqkv.shape), _rows(tm, d),
                  pl.BlockSpec((tm, d), lambda i: (jnp.maximum(i - nct, 0), 0)), _mods_spec(nct, 2, d)],
        out_specs=[_rows(tm, d), _mods_spec(nct, 2, d)],
        out_shape=[jax.ShapeDtypeStruct((rows, d), F32), jax.ShapeDtypeStruct((2, 8, d), F32)],
        compiler_params=_params(),
    )(dqkv, wqkv, hc, dh1, mods)


def _gmlp_gate(zp, lng, lnb, ws_ref, bs_ref, gg, ch):
    half = zp.shape[1] // 2
    ggw = half // gg
    z = _gelu(zp)
    u, v = z[:, :half], z[:, half:]
    vc = v - jnp.mean(v, axis=-1, keepdims=True)
    rs = lax.rsqrt(jnp.mean(vc * vc, axis=-1, keepdims=True) + EPS)
    vhat = vc * rs
    vln = (vhat * lng + lnb).astype(BF16)
    chunks = []
    for n in range(zp.shape[0] // ch):
        groups = []
        for g in range(gg):
            groups.append(_dot(ws_ref[g], vln[n * ch:(n + 1) * ch, g * ggw:(g + 1) * ggw]) + bs_ref[g])
        chunks.append(jnp.concatenate(groups, axis=1))
    sv = jnp.concatenate(chunks, axis=0) if len(chunks) > 1 else chunks[0]
    return u, sv, vhat, rs, vln


def _gmlp_fwd(h, mods, w_in, lng, lnb, ws, bs, w_out, *, tm):
    seq, d = h.shape
    zw = w_in.shape[0] * w_in.shape[-1]
    half = zw // 2
    gg, ch = ws.shape[0], ws.shape[-1]

    def body(h_ref, md_ref, win_ref, lng_ref, lnb_ref, ws_ref, bs_ref, wout_ref, h1_ref, zp_ref, y_ref):
        x = h_ref[...]
        md = md_ref[...]
        a, _ = _normmod(x, md, 0)
        zp = _dot_blocks(a.astype(BF16), win_ref)
        zp_ref[...] = zp.astype(BF16)
        u, sv, _, _, _ = _gmlp_gate(zp, lng_ref[...], lnb_ref[...], ws_ref, bs_ref, gg, ch)
        y = _dot((u * sv).astype(BF16), wout_ref[...])
        y_ref[...] = y.astype(BF16)
        h1_ref[...] = x + md[G1:G1 + 1] * y

    return pl.pallas_call(
        body, name="gmlp_fwd", grid=(seq // tm,),
        in_specs=[_rows(tm, d), pl.BlockSpec((None, 8, d), lambda i: (1, 0, 0)),
                  _full(w_in.shape), _full((1, half)), _full((1, half)),
                  _full((gg, ch, ch)), _full((gg, ch, 1)), _full((half, d))],
        out_specs=[_rows(tm, d), _rows(tm, zw), _rows(tm, d)],
        out_shape=[jax.ShapeDtypeStruct((seq, d), F32), jax.ShapeDtypeStruct((seq, zw), BF16),
                   jax.ShapeDtypeStruct((seq, d), BF16)],
        compiler_params=_params(),
    )(h, mods, w_in, lng, lnb, ws, bs, w_out)


def _gmlp_bwd(h, dh1, zpre, y, mods, w_in, lng, lnb, ws, ws_t, bs, w_out, *, tm):
    seq, d = h.shape
    zw = w_in.shape[0] * w_in.shape[-1]
    half = zw // 2
    gg, ch = ws.shape[0], ws.shape[-1]
    ggw = half // gg

    def body(h_ref, g_ref, zp_ref, y_ref, md_ref, win_ref, lng_ref, lnb_ref, ws_ref, wst_ref, bs_ref, wout_ref,
             dh_ref, dzp_ref, gated_ref, dyb_ref, ab_ref, dmd_ref, dln_ref, dws_ref, dbs_ref):
        i = pl.program_id(0)
        x = h_ref[...]
        md = md_ref[...]
        a, parts = _normmod(x, md, 0)
        ab_ref[...] = a.astype(BF16)
        zp = zp_ref[...].astype(F32)
        lng_v = lng_ref[...]
        u, sv, vhat, rs, vln = _gmlp_gate(zp, lng_v, lnb_ref[...], ws_ref, bs_ref, gg, ch)
        g = g_ref[...]
        dg1 = _rowsum(g * y_ref[...].astype(F32))
        dyb = (g * md[G1:G1 + 1]).astype(BF16)
        dyb_ref[...] = dyb
        gated_ref[...] = (u * sv).astype(BF16)
        dgated = _dot_nt(dyb, wout_ref[...])
        du = dgated * sv
        dsv = dgated * u

        @pl.when(i == 0)
        def _():
            dws_ref[...] = jnp.zeros(dws_ref.shape, F32)
            dbs_ref[...] = jnp.zeros(dbs_ref.shape, F32)
            dln_ref[...] = jnp.zeros(dln_ref.shape, F32)

        chunks = []
        for n in range(tm // ch):
            groups = []
            for gi in range(gg):
                blk = dsv[n * ch:(n + 1) * ch, gi * ggw:(gi + 1) * ggw]
                dbs_ref[gi] += jnp.sum(blk, axis=-1, keepdims=True)
                blk_b = blk.astype(BF16)
                dws_ref[gi] += _dot_nt(blk_b, vln[n * ch:(n + 1) * ch, gi * ggw:(gi + 1) * ggw])
                groups.append(_dot(wst_ref[gi], blk_b))
            chunks.append(jnp.concatenate(groups, axis=1))
        dvln = jnp.concatenate(chunks, axis=0) if len(chunks) > 1 else chunks[0]
        dln_ref[0:1, :] += _rowsum(dvln * vhat)
        dln_ref[1:2, :] += _rowsum(dvln)
        dvhat = dvln * lng_v
        dv = rs * (dvhat - jnp.mean(dvhat, axis=-1, keepdims=True)
                   - vhat * jnp.mean(dvhat * vhat, axis=-1, keepdims=True))
        dzp = (jnp.concatenate([du, dv], axis=1) * _gelu_grad(zp)).astype(BF16)
        dzp_ref[...] = dzp
        da = _dot_nt_blocks(dzp, win_ref)
        dx, dsh, dsc, dng = _normmod_bwd(da, parts, md, 0)
        dh_ref[...] = g + dx
        _acc_rows(dmd_ref, i == 0, {SH1: dsh, SC1: dsc, G1: dg1, NG0: dng})

    return pl.pallas_call(
        body, name="gmlp_bwd", grid=(seq // tm,),
        in_specs=[_rows(tm, d), _rows(tm, d), _rows(tm, zw), _rows(tm, d),
                  pl.BlockSpec((None, 8, d), lambda i: (1, 0, 0)),
                  _full(w_in.shape), _full((1, half)), _full((1, half)),
                  _full((gg, ch, ch)), _full((gg, ch, ch)), _full((gg, ch, 1)), _full((half, d))],
        out_specs=[_rows(tm, d), _rows(tm, zw), _rows(tm, half), _rows(tm, d), _rows(tm, d),
                   pl.BlockSpec((None, 8, d), lambda i: (0, 0, 0)), _full((8, half)), _full((gg, ch, ch)),
                   _full((gg, ch, 1))],
        out_shape=[jax.ShapeDtypeStruct((seq, d), F32), jax.ShapeDtypeStruct((seq, zw), BF16),
                   jax.ShapeDtypeStruct((seq, half), BF16), jax.ShapeDtypeStruct((seq, d), BF16),
                   jax.ShapeDtypeStruct((seq, d), BF16), jax.ShapeDtypeStruct((1, 8, d), F32),
                   jax.ShapeDtypeStruct((8, half), F32), jax.ShapeDtypeStruct((gg, ch, ch), F32),
                   jax.ShapeDtypeStruct((gg, ch, 1), F32)],
        compiler_params=_params(),
    )(h, dh1, zpre, y, mods, w_in, lng, lnb, ws, ws_t, bs, w_out)


def _final_loss(h, tgt, fg, *, tm):
    seq, d = h.shape

    def body(h_ref, t_ref, g_ref, dh_ref, acc_ref):
        i = pl.program_id(0)
        gain = g_ref[...]
        xhat, r = _rms_parts(h_ref[...])
        err = xhat * gain - t_ref[...]
        dy = err * (1.0 / d)
        dxhat = dy * gain
        dh_ref[...] = r * (dxhat - xhat * jnp.mean(dxhat * xhat, axis=-1, keepdims=True))
        part = jnp.sum(_rowsum(err * err), axis=-1, keepdims=True) * (0.5 / d)
        _acc_rows(acc_ref, i == 0, {0: _rowsum(dy * xhat), 1: jnp.broadcast_to(part, (1, d))})

    return pl.pallas_call(
        body, name="final_loss", grid=(seq // tm,),
        in_specs=[_rows(tm, d), _rows(tm, d), _full((1, d))],
        out_specs=[_rows(tm, d), _full((8, d))],
        out_shape=[jax.ShapeDtypeStruct((seq, d), F32), jax.ShapeDtypeStruct((8, d), F32)],
        compiler_params=_params(),
    )(h, tgt, fg)


def _ada_fwd(c_all, ada_w, ada_b_cols):
    depth, d, ncs = ada_w.shape

    def body(c_ref, w_ref, b_ref, o_ref):
        s = _silu(c_ref[...]).astype(BF16)
        o_ref[...] = _dot(s, w_ref[...].astype(BF16)) + b_ref[...]

    return pl.pallas_call(
        body, name="ada_fwd", grid=(depth,),
        in_specs=[_full((16, d)), pl.BlockSpec((None, d, ncs), lambda i: (i, 0, 0)),
                  pl.BlockSpec((None, 1, ncs), lambda i: (i, 0, 0))],
        out_specs=pl.BlockSpec((None, 16, ncs), lambda i: (i, 0, 0)),
        out_shape=jax.ShapeDtypeStruct((depth, 16, ncs), F32),
        compiler_params=_params(),
    )(c_all, ada_w, ada_b_cols.reshape(depth, 1, ncs))


def _ada_bwd(c_all, c_all_t, dmod, ada_w):
    depth, d, ncs = ada_w.shape

    def body(c_ref, ct_ref, dm_ref, w_ref, gw_ref, dc_ref):
        i = pl.program_id(0)
        dm = dm_ref[...]
        dctx = _rowsum(dm[8:16])
        rid = lax.broadcasted_iota(jnp.int32, (8, ncs), 0)
        low = jnp.where(rid == 0, jnp.broadcast_to(dctx, (8, ncs)), 0.0)
        dm16 = jnp.concatenate([dm[0:8], low], axis=0).astype(BF16)
        gw_ref[...] = _dot(_silu(ct_ref[...]).astype(BF16), dm16)

        @pl.when(i == 0)
        def _():
            dc_ref[...] = jnp.zeros(dc_ref.shape, F32)

        dc_ref[...] += _dot_nt(low.astype(BF16), w_ref[...].astype(BF16)) * _silu_grad(c_ref[8:9, :])

    return pl.pallas_call(
        body, name="ada_bwd", grid=(depth,),
        in_specs=[_full((16, d)), _full((d, 16)), pl.BlockSpec((None, 16, ncs), lambda i: (i, 0, 0)),
                  pl.BlockSpec((None, d, ncs), lambda i: (i, 0, 0))],
        out_specs=[pl.BlockSpec((None, d, ncs), lambda i: (i, 0, 0)), _full((8, d))],
        out_shape=[jax.ShapeDtypeStruct((depth, d, ncs), F32), jax.ShapeDtypeStruct((8, d), F32)],
        compiler_params=_params(),
    )(c_all, c_all_t, dmod, ada_w)


def _adamw_math(w, g, m, v):
    m = ADAM_B1 * m + (1.0 - ADAM_B1) * g
    v = ADAM_B2 * v + (1.0 - ADAM_B2) * jnp.square(g)
    m_hat = m * (1.0 / (1.0 - ADAM_B1 ** ADAM_STEP))
    v_hat = v * (1.0 / (1.0 - ADAM_B2 ** ADAM_STEP))
    delta = -ADAM_LR * (m_hat / (jnp.sqrt(v_hat) + ADAM_EPS) + ADAM_WD * w)
    return delta, m, v


def _adamw(ga, gb, w, m, v, name):
    rows, cols = w.shape
    tr = rows
    while tr * cols * 4 > (1 << 20) and tr % 16 == 0:
        tr //= 2
    grads = [ga] if gb is None else [ga, gb]

    def body(*refs):
        w_ref, m_ref, v_ref, g_out, d_out, m_out, v_out = refs[len(grads):]
        g = refs[0][...] if gb is None else refs[0][...] + refs[1][...]
        delta, m_new, v_new = _adamw_math(w_ref[...], g, m_ref[...], v_ref[...])
        g_out[...] = g
        d_out[...] = delta
        m_out[...] = m_new
        v_out[...] = v_new

    spec = _rows(tr, cols)
    return pl.pallas_call(
        body, name=name, grid=(rows // tr,),
        in_specs=[spec] * (len(grads) + 3), out_specs=[spec] * 4,
        out_shape=[jax.ShapeDtypeStruct((rows, cols), F32)] * 4,
        compiler_params=_params(),
    )(*grads, w, m, v)


def _sum_devices(gathered, name):
    n, rows, cols = gathered.shape
    tr = rows
    while tr * cols * 4 * n > (4 << 20) and tr % 16 == 0:
        tr //= 2

    def body(x_ref, o_ref):
        acc = x_ref[0]
        for j in range(1, n):
            acc = acc + x_ref[j]
        o_ref[...] = acc

    return pl.pallas_call(
        body, name=name, grid=(rows // tr,),
        in_specs=[pl.BlockSpec((n, tr, cols), lambda i: (0, i, 0))], out_specs=_rows(tr, cols),
        out_shape=jax.ShapeDtypeStruct((rows, cols), F32),
        compiler_params=_params(),
    )(gathered)


def _sum_partials(blocked, landeds, chip, name):
    n = len(blocked)
    cols = blocked[0].shape[-1]
    blocked = [b.reshape(N_CHIPS, -1, cols) for b in blocked]
    landeds = [l.reshape(3, -1, cols) for l in landeds]
    rows = blocked[0].shape[1]
    tr = rows
    while tr * cols * 2 * n > (1 << 20) and tr % 32 == 0:
        tr //= 2

    def body(chip_ref, *refs):
        out_ref = refs[-1]
        for li in range(n):
            acc = refs[li][...].astype(F32)
            for p in range(3):
                acc = acc + refs[n + li][p].astype(F32)
            out_ref[li] = acc

    out = pl.pallas_call(
        body, name=name,
        grid_spec=pltpu.PrefetchScalarGridSpec(
            num_scalar_prefetch=1, grid=(rows // tr,),
            in_specs=[pl.BlockSpec((None, tr, cols), lambda i, k: (k[0], i, 0))] * n
            + [pl.BlockSpec((3, tr, cols), lambda i, k: (0, i, 0))] * n,
            out_specs=pl.BlockSpec((n, tr, cols), lambda i, k: (0, i, 0))),
        out_shape=jax.ShapeDtypeStruct((n, rows, cols), F32),
        compiler_params=_params(),
    )(jnp.reshape(chip, (1,)).astype(jnp.int32), *blocked, *landeds)
    return out.reshape(n * rows, cols)


def _my_place():
    return lax.axis_index("x"), lax.axis_index("y"), lax.axis_index("c")


def _other_chips(x, y):
    return [(1 - x, y), (x, 1 - y), (1 - x, 1 - y)]


def _all_gather_small(block, name):
    rows, cols = block.shape

    def body(x_ref, out_ref, send_sems, recv_sems, local_sem):
        x, y, c = _my_place()
        me, sibling = (x, y, c), (x, y, 1 - c)
        chips = _other_chips(x, y)

        def slot(px, py, pc):
            return out_ref.at[4 * px + 2 * py + pc]

        def copy(k, blk, to, src=None):
            return pltpu.make_async_remote_copy(
                src_ref=slot(*blk) if src is None else src, dst_ref=slot(*blk),
                send_sem=send_sems.at[k], recv_sem=recv_sems.at[k], device_id=to, device_id_type=MESH)

        mine = pltpu.make_async_copy(x_ref, slot(*me), local_sem)
        mine.start()
        first = [copy(0, me, sibling, src=x_ref)]
        first += [copy(1 + j, me, (*chip, c), src=x_ref) for j, chip in enumerate(chips)]
        for cp in first:
            cp.start()
        passed = [copy(4 + j, (*chip, c), sibling) for j, chip in enumerate(chips)]
        for j, chip in enumerate(chips):
            copy(1 + j, (*chip, c), me).wait_recv()
            passed[j].start()
        copy(0, sibling, me).wait_recv()
        for j, chip in enumerate(chips):
            copy(4 + j, (*chip, 1 - c), me).wait_recv()
        for cp in first + passed:
            cp.wait_send()
        mine.wait()

    return pl.pallas_call(
        body, name=name,
        out_shape=jax.ShapeDtypeStruct((N_DEV, rows, cols), block.dtype),
        in_specs=[pl.BlockSpec(memory_space=pltpu.VMEM)],
        out_specs=pl.BlockSpec(memory_space=pltpu.VMEM),
        scratch_shapes=[pltpu.SemaphoreType.DMA((7,)), pltpu.SemaphoreType.DMA((7,)), pltpu.SemaphoreType.DMA],
        compiler_params=_params(),
    )(block)


HBM_SPEC = pl.BlockSpec(memory_space=pltpu.HBM)
SEM_SPEC = pl.BlockSpec(memory_space=pltpu.SEMAPHORE)
DATAFLOW_EFFECT = pltpu.SideEffectType.DATAFLOW_SIDE_EFFECTING


def _same_core_of_other_chips(x, y, c):
    return [(*chip, c) for chip in _other_chips(x, y)]


def _sibling_core(x, y, c):
    return [(x, y, 1 - c)]


def _gather_views(src, land, p, x, y):
    return src, land.at[2 * x + y]


def _scatter_views(src, land, p, x, y):
    peer_chip = (2 * (1 - x) + y, 2 * x + (1 - y), 2 * (1 - x) + (1 - y))[p]
    return src.at[peer_chip], land.at[p]


def _whole_views(src, land, p, x, y):
    return src, land


GATHER_PLAN = (_same_core_of_other_chips, _gather_views, 3)
SCATTER_PLAN = (_same_core_of_other_chips, _scatter_views, 3)
SIBLING_PLAN = (_sibling_core, _whole_views, 1)


def _exchange_copies(srcs, lands, send_sems, recv_sems, plan):
    peers_of, views, n_peers = plan
    x, y, c = _my_place()
    copies = []
    for j, (src, land) in enumerate(zip(srcs, lands)):
        for p, peer in enumerate(peers_of(x, y, c)):
            s_view, d_view = views(src, land, p, x, y)
            k = n_peers * j + p
            copies.append(pltpu.make_async_remote_copy(
                src_ref=s_view, dst_ref=d_view, send_sem=send_sems.at[k], recv_sem=recv_sems.at[k],
                device_id=peer, device_id_type=MESH))
    return copies


def _exchange_start(srcs, lands, plan, name):
    n = len(srcs)

    def body(*refs):
        send_sems, recv_sems = refs[2 * n], refs[2 * n + 1]
        token = refs[-1]
        for cp in _exchange_copies(refs[:n], refs[n:2 * n], send_sems, recv_sems, plan):
            cp.start()
        token[...] = jnp.zeros(token.shape, token.dtype)

    operands = [pltpu.with_memory_space_constraint(a, pltpu.HBM) for a in (*srcs, *lands)]
    out = pl.pallas_call(
        body, name=name,
        out_shape=(pltpu.SemaphoreType.DMA((plan[2] * n,)), pltpu.SemaphoreType.DMA((plan[2] * n,)),
                   *[pltpu.HBM(a.shape, a.dtype) for a in operands], jax.ShapeDtypeStruct((8, LANES), F32)),
        in_specs=[HBM_SPEC] * (2 * n),
        out_specs=(SEM_SPEC, SEM_SPEC, *[HBM_SPEC] * (2 * n), pl.BlockSpec(memory_space=pltpu.VMEM)),
        input_output_aliases={i: 2 + i for i in range(2 * n)},
        compiler_params=pltpu.CompilerParams(has_side_effects=DATAFLOW_EFFECT),
    )(*operands)
    return out[0], out[1], list(out[2:2 + n]), list(out[2 + n:2 + 2 * n]), out[-1]


def _exchange_wait(send_sems, recv_sems, srcs, lands, plan, after, name):
    n = len(srcs)

    def body(*refs):
        send, recv = refs[2 * n], refs[2 * n + 1]
        for cp in _exchange_copies(refs[:n], refs[n:2 * n], send, recv, plan):
            cp.wait_send()
            cp.wait_recv()

    out = pl.pallas_call(
        body, name=name,
        out_shape=tuple(pltpu.HBM(a.shape, a.dtype) for a in (*srcs, *lands)),
        in_specs=[HBM_SPEC] * (2 * n) + [SEM_SPEC, SEM_SPEC, HBM_SPEC],
        out_specs=tuple([HBM_SPEC] * (2 * n)),
        input_output_aliases={i: i for i in range(2 * n)},
        compiler_params=pltpu.CompilerParams(has_side_effects=DATAFLOW_EFFECT),
    )(*srcs, *lands, send_sems, recv_sems, pltpu.with_memory_space_constraint(after, pltpu.HBM))
    return list(out[:n]), list(out[n:])


def _landing_for_gather(shard, chip):
    land = lax.empty((N_CHIPS, *shard.shape), shard.dtype)
    return lax.dynamic_update_index_in_dim(land, shard, chip, 0)


TILE_ELEMS = SUBLANES * LANES


def _pack(arrays):
    parts = []
    for a in arrays:
        flat = a.reshape(-1).astype(F32)
        pad = (-flat.shape[0]) % TILE_ELEMS
        if pad:
            flat = jnp.concatenate([flat, jnp.zeros((pad,), F32)])
        parts.append(flat.reshape(-1, LANES))
    return jnp.concatenate(parts, axis=0) if len(parts) > 1 else parts[0]


def _unpack(buf, shapes):
    out, r = [], 0
    lead = buf.shape[:-2]
    for shp in shapes:
        size = math.prod(shp)
        nr = -(-size // TILE_ELEMS) * SUBLANES
        flat = buf[..., r:r + nr, :].reshape(*lead, nr * LANES)[..., :size]
        out.append(flat.reshape(*lead, *shp))
        r += nr
    return out


def _chip_cols(a, k, width):
    return lax.dynamic_slice_in_dim(a, k * width, width, axis=a.ndim - 1)


def _across_chips(gathered, c0_only_shape):
    return gathered.reshape(2, 2, 2, *c0_only_shape)[:, :, 0].reshape(N_CHIPS, *c0_only_shape)


def kernel(x, c, ctx, c_ctx, ada_w, ada_b, norm_g, mlp_w1, mlp_w2, pool_w, pool_scale, attn_w_qkv, attn_w_o, attn_q_g, attn_k_g, gm_w_in, gm_ln_g, gm_ln_b, gm_ws, gm_bs, gm_w_out, final_g, loss_target, m_c_ctx, m_ada_w, m_ada_b, m_norm_g, m_mlp_w1, m_mlp_w2, m_pool_w, m_pool_scale, m_attn_w_qkv, m_attn_w_o, m_attn_q_g, m_attn_k_g, m_gm_w_in, m_gm_ln_g, m_gm_ln_b, m_gm_ws, m_gm_bs, m_gm_w_out, m_final_g, v_c_ctx, v_ada_w, v_ada_b, v_norm_g, v_mlp_w1, v_mlp_w2, v_pool_w, v_pool_scale, v_attn_w_qkv, v_attn_w_o, v_attn_q_g, v_attn_k_g, v_gm_w_in, v_gm_ln_g, v_gm_ln_b, v_gm_ws, v_gm_bs, v_gm_w_out, v_final_g):
    seq, d = x.shape[1], x.shape[2]
    n_ctx = ctx.shape[1]
    total = n_ctx + seq
    hd = attn_q_g.shape[-1]
    nh = d // hd
    nkv = nh // 2
    gg, ch = gm_ws.shape[1], gm_ws.shape[-1]
    half = gm_w_out.shape[1] * N_CHIPS
    pgw = pool_w.shape[-1]
    tm = min(256, n_ctx)
    nct = n_ctx // tm
    seg_lens = (n_ctx, seq)

    mx, my, mc = _my_place()
    chip = 2 * mx + my
    me = 4 * mx + 2 * my + mc

    c_rows = jnp.concatenate([c, jnp.zeros((7, d), F32)], axis=0)
    c_gath = _all_gather_small(c_rows, "gather_cond")[:, 0, :]
    c_all = jnp.concatenate([c_gath, c_ctx[None, :], jnp.zeros((7, d), F32)], axis=0)
    ncs = ada_w.shape[-1]
    ada_cols = _ada_fwd(c_all, ada_w, _chip_cols(ada_b, chip, ncs))
    small_shapes = [ada_cols.shape, norm_g.shape, pool_scale.shape, gm_ln_g.shape, gm_ln_b.shape]
    gathered = _all_gather_small(_pack([ada_cols, norm_g, pool_scale, gm_ln_g, gm_ln_b]), "gather_small_params")
    per_chip = _across_chips(gathered, gathered.shape[1:])
    ada_g, ng_g, ps_g, lng_g, lnb_g = _unpack(per_chip, small_shapes)

    def join_last(a):
        return jnp.moveaxis(a, 0, -2).reshape(*a.shape[1:-1], N_CHIPS * a.shape[-1])

    ada_full = join_last(ada_g)
    ng_full = join_last(ng_g)
    ps_full = join_last(ps_g)
    lng_full = join_last(lng_g)
    lnb_full = join_last(lnb_g)
    mod_lat = lax.dynamic_slice_in_dim(ada_full, me, 1, axis=1).reshape(DEPTH, 6, d)
    mod_ctx = ada_full[:, 8].reshape(DEPTH, 6, d)
    mods = jnp.stack([jnp.concatenate([mod_ctx, ng_full], axis=1), jnp.concatenate([mod_lat, ng_full], axis=1)],
                     axis=1)

    weight_groups = [
        [pool_w],
        [mlp_w1[0], mlp_w2[0]],
        [mlp_w1[1], mlp_w2[1], attn_w_qkv[0], attn_w_o[0]],
        [mlp_w1[2], mlp_w2[2], gm_w_in[0], gm_w_out[0], mlp_w1[3], mlp_w2[3]],
    ]
    gathers = [None] * len(weight_groups)

    def gather_start(gi, after):
        shards, _ = lax.optimization_barrier(([w.astype(BF16) for w in weight_groups[gi]], after))
        lands = [_landing_for_gather(s, chip) for s in shards]
        gathers[gi] = _exchange_start(shards, lands, GATHER_PLAN, f"gather_weights_{gi}_start")
        return gathers[gi][4][0:1, 0:1]

    def gathered(gi, after):
        send, recv, srcs, lands, _ = gathers[gi]
        return _exchange_wait(send, recv, srcs, lands, GATHER_PLAN, after, f"gather_weights_{gi}_wait")[1]

    def rows_joined(a):
        return a.reshape(-1, a.shape[-1])

    w1_b, w2_b = [None] * DEPTH, [None] * DEPTH
    gather_start(0, mods)
    pw_land, = gathered(0, ps_full)
    behind_gather_1 = gather_start(1, pw_land)
    pw_f = jnp.transpose(pw_land, (1, 2, 0, 3, 4)).reshape(pool_w.shape[0], pool_w.shape[1], pgw, pgw)

    gains = jnp.concatenate([attn_q_g, attn_k_g, jnp.zeros((6, hd), F32)], axis=0)
    ws_b = gm_ws[0].astype(BF16)
    ws_t = jnp.swapaxes(gm_ws[0], 1, 2).astype(BF16)
    bs_col = gm_bs[0][:, :, None]
    cos, sin = _rope_tables(n_ctx, seq, hd)
    lat = lambda i: mods[i, 1:2]

    hc0 = jnp.concatenate([ctx[0] + behind_gather_1, x[0]], axis=0)
    ha0 = _pool_fwd(hc0, mods[0] + behind_gather_1, pw_f, ps_full, 0, nct=nct, tm=tm, seg_lens=seg_lens)
    w1_b[0], w2_b[0] = gathered(1, ha0)
    mods0 = mods[0] + gather_start(2, w1_b[0])
    hc1, u0, o0 = _mlp_fwd(ha0, mods0, w1_b[0], w2_b[0], 0, nct=nct, tm=tm)
    w1_b[1], w2_b[1], wqkv_b, wo_land = gathered(2, hc1)
    mods1 = mods[1] + gather_start(3, w1_b[1])
    wo_f = rows_joined(wo_land)
    xa1 = _normmod_call(hc1, mods1, 0, "attn_in_fwd", nct=nct, tm=tm)
    qkv, q_r, k_r, v_b = _qkv_fwd(xa1, wqkv_b, cos, sin, gains, nh=nh, nkv=nkv, nct=nct, tm=tm)
    o_att, lse = _flash_fwd(q_r, k_r, v_b, n_ctx=n_ctx, hd=hd)
    ha1, y1 = _proj_fwd(o_att, wo_f, hc1, mods1, n_ctx=n_ctx, tm=tm)
    h2, u1, o1 = _mlp_fwd(ha1, lat(1), w1_b[1], w2_b[1], 1, nct=0, tm=tm)
    w1_b[2], w2_b[2], win_b, wout_land, w1_b[3], w2_b[3] = gathered(3, h2)
    wout_f = rows_joined(wout_land)
    ha2, zpre, y2 = _gmlp_fwd(h2, mods[2], win_b, lng_full, lnb_full, ws_b, bs_col, wout_f, tm=tm)
    h3, u2, o2 = _mlp_fwd(ha2, lat(2), w1_b[2], w2_b[2], 2, nct=0, tm=tm)
    ha3 = _pool_fwd(h3, lat(3), pw_f, ps_full, 3, nct=0, tm=tm, seg_lens=seg_lens)
    h4, u3, o3 = _mlp_fwd(ha3, lat(3), w1_b[3], w2_b[3], 3, nct=0, tm=tm)
    dh4, fin_acc = _final_loss(h4, loss_target[0], final_g[None, :], tm=tm)

    dmods = [None] * DEPTH
    scatters = [None] * (DEPTH + 1)

    def blocked_rows(g):
        return g.reshape(N_CHIPS, g.shape[1] // N_CHIPS, g.shape[2])

    def blocked_pool(dpw):
        pg = dpw.shape[0]
        return jnp.transpose(dpw.astype(BF16).reshape(pg, N_CHIPS, pgw // N_CHIPS, pgw), (1, 0, 2, 3))

    def scatter_start(i, grads):
        lands = [lax.empty((3, *g.shape[1:]), g.dtype) for g in grads]
        scatters[i] = _exchange_start(grads, lands, SCATTER_PLAN, f"scatter_grads_{i}_start")
        return scatters[i][4][0:1, 0:1]

    def mlp_back(i, h_in, dh_out, u, o, md, n_ct):
        dh_in, du, dob, mb, dmd = _mlp_bwd(h_in, dh_out, u, o, md, w1_b[i], w2_b[i], i, nct=n_ct, tm=tm)
        dw1 = _mm_tn(mb, du, f"mlp_dw1_{i}", col_blocks=N_CHIPS)
        dw2 = blocked_rows(_mm_tn(u, dob, f"mlp_dw2_{i}", relu2=True))
        return dh_in, dmd, [dw1, dw2]

    def pool_back(i, h_in, dh_out, md, n_ct):
        dp, dmd_a, dps, dpw = _pool_bwd_weights(h_in, dh_out, md, pw_f, ps_full, i, nct=n_ct, tm=tm,
                                                seg_lens=seg_lens)
        dh_in, dmd_b = _pool_bwd_input(dp, h_in, dh_out, md, i, nct=n_ct, tm=tm, seg_lens=seg_lens, gw=pgw)
        return dh_in, dmd_a + dmd_b, dps, dpw

    zero_grp = jnp.zeros((1, 8, d), F32)
    dha3, dmd3, dws3 = mlp_back(3, ha3, dh4, u3, o3, lat(3), 0)
    dh3, dmd3p, dps3, dpw3 = pool_back(3, h3, dha3, lat(3), 0)
    dmods[3] = jnp.concatenate([zero_grp, dmd3 + dmd3p], axis=0)
    tok = scatter_start(3, dws3 + [blocked_pool(dpw3)])
    dha2, dmd2, dws2 = mlp_back(2, ha2, dh3, u2, o2, lat(2) + tok, 0)
    dh2, dzpre, gated, dyb2, ab2, dmd2g, dln, dws, dbs = _gmlp_bwd(
        h2, dha2, zpre, y2, mods[2], win_b, lng_full, lnb_full, ws_b, ws_t, bs_col, wout_f, tm=tm)
    dwin = _mm_tn(ab2, dzpre, "gmlp_dw_in", col_blocks=N_CHIPS)
    dwout = blocked_rows(_mm_tn(gated, dyb2, "gmlp_dw_out"))
    dmods[2] = jnp.concatenate([zero_grp, dmd2 + dmd2g], axis=0)
    tok = scatter_start(2, dws2 + [dwin, dwout])
    dha1, dmd1, dws1 = mlp_back(1, ha1, dh2, u1, o1, lat(1) + tok, 0)
    do_att, dyb1, dmd1p = _proj_bwd(dha1, y1, mods[1], wo_f, tm=tm)
    dwo = blocked_rows(_mm_tn(o_att, dyb1, "attn_dw_o"))
    dq, dk, dv = _flash_bwd(q_r, k_r, v_b, o_att, do_att, lse, n_ctx=n_ctx, hd=hd)
    dqkv, dgains = _qkv_bwd(qkv, dq, dk, dv, cos, sin, gains, nh=nh, nkv=nkv, nct=nct, tm=tm)
    dwqkv = _mm_tn(xa1, dqkv, "attn_dw_qkv", col_blocks=N_CHIPS)
    dhc1, dmd1i = _attn_in_bwd(dqkv, wqkv_b, hc1, dha1, mods[1], nct=nct, tm=tm)
    dmods[1] = dmd1i + jnp.concatenate([zero_grp, dmd1 + dmd1p], axis=0)
    tok = scatter_start(1, dws1 + [dwqkv, dwo])
    dha0, dmd0, dws0 = mlp_back(0, ha0, dhc1, u0, o0, mods[0] + tok, nct)
    tok = scatter_start(0, dws0)
    dhc0, dmd0p, dps0, dpw0 = pool_back(0, hc0, dha0, mods[0] + tok, nct)
    dmods[0] = dmd0 + dmd0p
    grad_x = dhc0[None]
    scatter_start(DEPTH, [blocked_pool(dpw0)])

    dmods_all = jnp.stack(dmods, axis=0)
    small_grads = [dmods_all, dws, dbs, dgains, dln, dps0, dps3, fin_acc]
    sg_shapes = [a.shape for a in small_grads]
    sg_gath = _all_gather_small(_pack(small_grads), "gather_small_grads")
    sg_sum = _sum_devices(sg_gath, "sum_small_grads")
    s_dmods, s_dws, s_dbs, s_dgains, s_dln, s_dps0, s_dps3, s_fin = _unpack(sg_sum, sg_shapes)
    loss = s_fin[1, 0]

    sources, landed = [None] * len(scatters), [None] * len(scatters)
    for i in (3, 2, 1, 0, DEPTH):
        send, recv, srcs, lands, _ = scatters[i]
        sources[i], landed[i] = _exchange_wait(send, recv, srcs, lands, SCATTER_PLAN, sg_sum, f"scatter_grads_{i}_wait")

    def summed(name, picks):
        return _sum_partials([sources[i][j] for i, j in picks], [landed[i][j] for i, j in picks], chip,
                             f"sum_chips_{name}")

    big = [("mlp_w1", mlp_w1, m_mlp_w1, v_mlp_w1, [(i, 0) for i in range(DEPTH)]),
           ("mlp_w2", mlp_w2, m_mlp_w2, v_mlp_w2, [(i, 1) for i in range(DEPTH)]),
           ("pool_w", pool_w, m_pool_w, v_pool_w, [(DEPTH, 0), (3, 2)]),
           ("attn_w_qkv", attn_w_qkv, m_attn_w_qkv, v_attn_w_qkv, [(1, 2)]),
           ("attn_w_o", attn_w_o, m_attn_w_o, v_attn_w_o, [(1, 3)]),
           ("gm_w_in", gm_w_in, m_gm_w_in, v_gm_w_in, [(2, 2)]),
           ("gm_w_out", gm_w_out, m_gm_w_out, v_gm_w_out, [(2, 3)])]
    partial = [summed(name, picks) for name, _, _, _, picks in big]
    swap = _exchange_start(partial, [lax.empty(p.shape, p.dtype) for p in partial], SIBLING_PLAN,
                           "swap_with_sibling_start")
    behind_swap = swap[4][0:1, 0:1]

    dm_dev = _unpack(sg_gath, sg_shapes[:1])[0]
    dm_lat = jnp.moveaxis(dm_dev[:, :, 1, :6, :], 0, 1).reshape(DEPTH, N_DEV, 6 * d)
    dm_ctx = jnp.moveaxis(dm_dev[:, :, 0, :6, :], 0, 1).reshape(DEPTH, N_DEV, 6 * d)
    dmod16 = _chip_cols(jnp.concatenate([dm_lat, dm_ctx], axis=1), chip, ncs) + behind_swap
    g_ada_w, dcc_part = _ada_bwd(c_all, c_all.T, dmod16, ada_w)
    dcc_gath = _all_gather_small(dcc_part, "gather_d_c_ctx")
    dcc_chips = _across_chips(dcc_gath, dcc_gath.shape[1:])
    dcc_rows = _sum_devices(dcc_chips, "sum_d_c_ctx")
    dcc = dcc_rows[0]
    ada_res = _adamw(g_ada_w.reshape(-1, ncs), None, ada_w.reshape(-1, ncs),
                     m_ada_w.reshape(-1, ncs), v_ada_w.reshape(-1, ncs), "adamw_ada_w")

    partial, from_sibling = _exchange_wait(swap[0], swap[1], swap[2], swap[3], SIBLING_PLAN, ada_res[1],
                                           "swap_with_sibling_wait")
    big_out = {}
    for (name, w, m, v, _), mine, theirs in zip(big, partial, from_sibling):
        cols = w.shape[-1]
        res = _adamw(mine, theirs, w.reshape(-1, cols), m.reshape(-1, cols), v.reshape(-1, cols), f"adamw_{name}")
        big_out[name] = [r.reshape(w.shape) for r in res]
    big_out["ada_w"] = [r.reshape(ada_w.shape) for r in ada_res]

    def cols_of(a, width):
        return _chip_cols(a, chip, width)

    zero = lambda a: jnp.zeros(a.shape, F32)
    ngw = norm_g.shape[-1]
    small = {
        "c_ctx": (dcc, zero(dcc), c_ctx, m_c_ctx, v_c_ctx),
        "ada_b": (s_dmods[:, 0, :6].reshape(DEPTH, 6 * d), s_dmods[:, 1, :6].reshape(DEPTH, 6 * d), ada_b, m_ada_b,
                  v_ada_b),
        "norm_g": (cols_of(s_dmods[:, 0, 6:8], ngw), cols_of(s_dmods[:, 1, 6:8], ngw), norm_g, m_norm_g, v_norm_g),
        "pool_scale": (cols_of(jnp.stack([s_dps0[0], s_dps3[0]]), pool_scale.shape[-1]), zero(pool_scale),
                       pool_scale, m_pool_scale, v_pool_scale),
        "attn_q_g": (s_dgains[0:1], zero(attn_q_g), attn_q_g, m_attn_q_g, v_attn_q_g),
        "attn_k_g": (s_dgains[1:2], zero(attn_k_g), attn_k_g, m_attn_k_g, v_attn_k_g),
        "gm_ln_g": (cols_of(s_dln[0:1], gm_ln_g.shape[-1]), zero(gm_ln_g), gm_ln_g, m_gm_ln_g, v_gm_ln_g),
        "gm_ln_b": (cols_of(s_dln[1:2], gm_ln_b.shape[-1]), zero(gm_ln_b), gm_ln_b, m_gm_ln_b, v_gm_ln_b),
        "gm_ws": (s_dws[None], zero(gm_ws), gm_ws, m_gm_ws, v_gm_ws),
        "gm_bs": (s_dbs[None, :, :, 0], zero(gm_bs), gm_bs, m_gm_bs, v_gm_bs),
        "final_g": (s_fin[0], zero(final_g), final_g, m_final_g, v_final_g),
    }
    keys = list(small)
    packed = [_pack([small[k][t] for k in keys]) for t in range(5)]
    res = _adamw(*packed, "adamw_small")
    shapes = [small[k][2].shape for k in keys]
    small_out = {k: [] for k in keys}
    for r in res:
        for k, a in zip(keys, _unpack(r, shapes)):
            small_out[k].append(a)

    order = ["c_ctx", "ada_w", "ada_b", "norm_g", "mlp_w1", "mlp_w2", "pool_w", "pool_scale", "attn_w_qkv",
             "attn_w_o", "attn_q_g", "attn_k_g", "gm_w_in", "gm_ln_g", "gm_ln_b", "gm_ws", "gm_bs", "gm_w_out",
             "final_g"]
    allo = {**big_out, **small_out}
    outs = [loss, grad_x]
    for t in range(4):
        outs += [allo[k][t] for k in order]
    return tuple(outs)
```

```python
import functools
import math

import numpy as np
import jax
import jax.numpy as jnp
from jax import lax
from jax.experimental import pallas as pl
from jax.experimental.pallas import tpu as pltpu

F32 = jnp.float32
BF16 = jnp.bfloat16
MESH = pl.DeviceIdType.MESH

EPS = 1e-6
GRID_W = 64
ROPE_BASE = 10000.0
POOL_WINDOWS = (2, 4, 8, 16)
HALO = 8
DEPTH = 4
N_MIXERS = 3

ADAM_LR = 0.001
ADAM_B1 = 0.9
ADAM_B2 = 0.999
ADAM_EPS = 1e-08
ADAM_WD = 0.01
ADAM_STEP = 10

VMEM_LIMIT_BYTES = 56 * 1024 * 1024
LANES = 128
SUBLANES = 8
N_DEV = 8
N_CHIPS = 4

SH1, SC1, G1, SH2, SC2, G2, NG0, NG1 = range(8)


def _dot(a, b):
    return jnp.dot(a, b, preferred_element_type=F32)


def _dot_nt(a, b):
    return lax.dot_general(a, b, (((1,), (1,)), ((), ())), preferred_element_type=F32)


def _dot_tn(a, b):
    return lax.dot_general(a, b, (((0,), (0,)), ((), ())), preferred_element_type=F32)


def _dot_blocks(a, w_ref):
    return jnp.concatenate([_dot(a, w_ref[k]) for k in range(w_ref.shape[0])], axis=1)


def _dot_nt_blocks(a, w_ref):
    nb, _, w = w_ref.shape
    acc = _dot_nt(a[:, 0:w], w_ref[0])
    for k in range(1, nb):
        acc = acc + _dot_nt(a[:, k * w:(k + 1) * w], w_ref[k])
    return acc


def _params(**kw):
    return pltpu.CompilerParams(vmem_limit_bytes=VMEM_LIMIT_BYTES, **kw)


def _full(shape):
    nd = len(shape)
    return pl.BlockSpec(shape, lambda *_: (0,) * nd)


def _rows(tm, width):
    return pl.BlockSpec((tm, width), lambda i: (i, 0))


def _group_of(nct, groups):
    if groups == 1:
        return lambda i: 0
    return lambda i: jnp.where(i >= nct, 1, 0)


def _mods_spec(nct, groups, d):
    grp = _group_of(nct, groups)
    return pl.BlockSpec((None, 8, d), lambda i: (grp(i), 0, 0))


def _first_of_group(i, nct, groups):
    if groups == 1:
        return i == 0
    return jnp.logical_or(i == 0, i == nct)


def _rowsum(v):
    return jnp.sum(v, axis=0, keepdims=True)


def _rms_parts(x):
    r = lax.rsqrt(jnp.mean(x * x, axis=-1, keepdims=True) + EPS)
    return x * r, r


def _normmod(x, md, which):
    ng, sh, sc = (md[NG0:NG0 + 1], md[SH1:SH1 + 1], md[SC1:SC1 + 1]) if which == 0 else (
        md[NG1:NG1 + 1], md[SH2:SH2 + 1], md[SC2:SC2 + 1])
    xhat, r = _rms_parts(x)
    n = xhat * ng
    return n * (1.0 + sc) + sh, (xhat, r, n)


def _normmod_bwd(da, parts, md, which):
    xhat, r, n = parts
    ng, sc = (md[NG0:NG0 + 1], md[SC1:SC1 + 1]) if which == 0 else (md[NG1:NG1 + 1], md[SC2:SC2 + 1])
    dsh = _rowsum(da)
    dsc = _rowsum(da * n)
    dn = da * (1.0 + sc)
    dng = _rowsum(dn * xhat)
    dxhat = dn * ng
    dx = r * (dxhat - xhat * jnp.mean(dxhat * xhat, axis=-1, keepdims=True))
    return dx, dsh, dsc, dng


def _acc_rows(ref, first, rows):
    @pl.when(first)
    def _():
        ref[...] = jnp.zeros(ref.shape, ref.dtype)

    for r, v in rows.items():
        ref[r:r + 1, :] += v


def _shift_up(x, k):
    if k == 0:
        return x
    return pltpu.roll(x, x.shape[0] - k, axis=0)


def _gelu(x):
    k = math.sqrt(2.0 / math.pi)
    return 0.5 * x * (1.0 + jnp.tanh(k * (x + 0.044715 * x * x * x)))


def _gelu_grad(x):
    k = math.sqrt(2.0 / math.pi)
    t = jnp.tanh(k * (x + 0.044715 * x * x * x))
    return 0.5 * (1.0 + t) + 0.5 * x * (1.0 - t * t) * k * (1.0 + 3.0 * 0.044715 * x * x)


def _silu(x):
    return x / (1.0 + jnp.exp(-x))


def _silu_grad(x):
    s = 1.0 / (1.0 + jnp.exp(-x))
    return s * (1.0 + x * (1.0 - s))


def _mlp_fwd(h, mods, w1, w2, layer, *, nct, tm):
    rows, d = h.shape
    groups = mods.shape[0]
    nb, _, fc = w1.shape
    ff = nb * fc

    def body(h_ref, md_ref, w1_ref, w2_ref, h2_ref, u_ref, o_ref):
        x = h_ref[...]
        md = md_ref[...]
        m, _ = _normmod(x, md, 1)
        mb = m.astype(BF16)
        acc = jnp.zeros((tm, d), F32)
        for k in range(nb):
            u = _dot(mb, w1_ref[k])
            u_ref[:, k * fc:(k + 1) * fc] = u.astype(BF16)
            acc = acc + _dot(jnp.square(jnp.maximum(u, 0.0)).astype(BF16), w2_ref[k])
        o_ref[...] = acc.astype(BF16)
        h2_ref[...] = x + md[G2:G2 + 1] * acc

    return pl.pallas_call(
        body, name=f"mlp_fwd_{layer}", grid=(rows // tm,),
        in_specs=[_rows(tm, d), _mods_spec(nct, groups, d), _full(w1.shape), _full(w2.shape)],
        out_specs=[_rows(tm, d), _rows(tm, ff), _rows(tm, d)],
        out_shape=[jax.ShapeDtypeStruct((rows, d), F32), jax.ShapeDtypeStruct((rows, ff), BF16),
                   jax.ShapeDtypeStruct((rows, d), BF16)],
        compiler_params=_params(),
    )(h, mods, w1, w2)


def _mlp_bwd(h1, dh2, u, o, mods, w1, w2, layer, *, nct, tm):
    rows, d = h1.shape
    groups = mods.shape[0]
    nb, _, fc = w1.shape
    ff = nb * fc

    def body(h_ref, g_ref, u_ref, o_ref, md_ref, w1_ref, w2_ref, dh_ref, du_ref, dob_ref, mb_ref, dmd_ref):
        i = pl.program_id(0)
        x = h_ref[...]
        g = g_ref[...]
        md = md_ref[...]
        m, parts = _normmod(x, md, 1)
        mb_ref[...] = m.astype(BF16)
        dg2 = _rowsum(g * o_ref[...].astype(F32))
        dob = (g * md[G2:G2 + 1]).astype(BF16)
        dob_ref[...] = dob
        dm = jnp.zeros((tm, d), F32)
        for k in range(nb):
            uk = u_ref[:, k * fc:(k + 1) * fc].astype(F32)
            dr = _dot_nt(dob, w2_ref[k])
            duk = (dr * (2.0 * jnp.maximum(uk, 0.0))).astype(BF16)
            du_ref[:, k * fc:(k + 1) * fc] = duk
            dm = dm + _dot_nt(duk, w1_ref[k])
        dx, dsh, dsc, dng = _normmod_bwd(dm, parts, md, 1)
        dh_ref[...] = g + dx
        _acc_rows(dmd_ref, _first_of_group(i, nct, groups), {SH2: dsh, SC2: dsc, G2: dg2, NG1: dng})

    return pl.pallas_call(
        body, name=f"mlp_bwd_{layer}", grid=(rows // tm,),
        in_specs=[_rows(tm, d), _rows(tm, d), _rows(tm, ff), _rows(tm, d), _mods_spec(nct, groups, d),
                  _full(w1.shape), _full(w2.shape)],
        out_specs=[_rows(tm, d), _rows(tm, ff), _rows(tm, d), _rows(tm, d), _mods_spec(nct, groups, d)],
        out_shape=[jax.ShapeDtypeStruct((rows, d), F32), jax.ShapeDtypeStruct((rows, ff), BF16),
                   jax.ShapeDtypeStruct((rows, d), BF16), jax.ShapeDtypeStruct((rows, d), BF16),
                   jax.ShapeDtypeStruct((groups, 8, d), F32)],
        compiler_params=_params(),
    )(h1, dh2, u, o, mods, w1, w2)


def _div_tile(n, cap):
    if n <= cap:
        return n
    return max(t for t in range(LANES, cap + 1, LANES) if n % t == 0)


DW_TOKEN_TILE_CAP = 4224


def _mm_tn(a, b, name, *, relu2=False, col_blocks=1):
    rows, m = a.shape
    n = b.shape[1]
    tmm = min(m, 1024)
    tn = min(n // col_blocks, 2048)
    per_block = n // col_blocks // tn
    tr = _div_tile(rows, DW_TOKEN_TILE_CAP)

    def body(a_ref, b_ref, o_ref, acc_ref):
        r = pl.program_id(2)

        @pl.when(r == 0)
        def _():
            acc_ref[...] = jnp.zeros(acc_ref.shape, F32)

        av = a_ref[...]
        if relu2:
            av = jnp.square(jnp.maximum(av.astype(F32), 0.0)).astype(BF16)
        acc_ref[...] += _dot_tn(av, b_ref[...])

        @pl.when(r == pl.num_programs(2) - 1)
        def _():
            o_ref[...] = acc_ref[...].astype(BF16)

    return pl.pallas_call(
        body, name=name, grid=(m // tmm, n // tn, rows // tr),
        in_specs=[pl.BlockSpec((tr, tmm), lambda i, j, r: (r, i)), pl.BlockSpec((tr, tn), lambda i, j, r: (r, j))],
        out_specs=pl.BlockSpec((None, tmm, tn), lambda i, j, r: (j // per_block, i, j % per_block)),
        out_shape=jax.ShapeDtypeStruct((col_blocks, m, n // col_blocks), BF16),
        scratch_shapes=[pltpu.VMEM((tmm, tn), F32)],
        compiler_params=_params(),
    )(a, b)


def _halo_specs(tm, d, rows):
    per = tm // HALO
    prev = pl.BlockSpec((HALO, d), lambda i: (jnp.maximum(i * per - 1, 0), 0))
    nxt = pl.BlockSpec((HALO, d), lambda i: (jnp.minimum((i + 1) * per, rows // HALO - 1), 0))
    return prev, _rows(tm, d), nxt


def _segment_positions(i, tm, nct, groups, seg_lens):
    if groups == 1:
        start, length = 0, seg_lens[-1]
    else:
        start = jnp.where(i >= nct, nct, 0)
        length = jnp.where(i >= nct, seg_lens[1], seg_lens[0])
    rid = lax.broadcasted_iota(jnp.int32, (tm + 2 * HALO, 1), 0)
    pos = (i - start) * tm - HALO + rid
    return pos, length


def _window_count(pos, length, w):
    hi = jnp.minimum(pos + (w - w // 2), length)
    lo = jnp.maximum(pos - w // 2, 0)
    return (hi - lo).astype(F32)


def _window_sum(xg, w, lead):
    b, k = xg, 1
    while k < w:
        b = b + _shift_up(b, k)
        k *= 2
    return _shift_up(b, HALO - lead)[0:xg.shape[0] - 2 * HALO]


def _pooled(ext, md, pos, length, gw):
    tm = ext.shape[0] - 2 * HALO
    a_ext, parts = _normmod(ext, md, 0)
    valid = jnp.logical_and(pos >= 0, pos < length)
    a_ext = jnp.where(valid, a_ext, 0.0)
    pos_c = pos[HALO:HALO + tm]
    ps = []
    for g, w in enumerate(POOL_WINDOWS):
        xg = a_ext[:, g * gw:(g + 1) * gw]
        s = _window_sum(xg, w, w // 2)
        ps.append(s * (1.0 / _window_count(pos_c, length, w)) - xg[HALO:HALO + tm])
    return ps, parts


def _pool_fwd(h, mods, pw, pscale, layer, *, nct, tm, seg_lens):
    rows, d = h.shape
    groups = mods.shape[0]
    pg, gw = pw.shape[1], pw.shape[-1]

    def body(prev_ref, cur_ref, next_ref, md_ref, pw_ref, ps_ref, out_ref):
        i = pl.program_id(0)
        md = md_ref[...]
        cur = cur_ref[...]
        ext = jnp.concatenate([prev_ref[...], cur, next_ref[...]], axis=0)
        pos, length = _segment_positions(i, tm, nct, groups, seg_lens)
        ps, _ = _pooled(ext, md, pos, length, gw)
        for g in range(pg):
            yg = _dot(ps[g].astype(BF16), pw_ref[g]) * ps_ref[:, g * gw:(g + 1) * gw]
            out_ref[:, g * gw:(g + 1) * gw] = cur[:, g * gw:(g + 1) * gw] + md[G1:G1 + 1, g * gw:(g + 1) * gw] * yg

    j = layer // N_MIXERS
    return pl.pallas_call(
        body, name=f"pool_fwd_{layer}", grid=(rows // tm,),
        in_specs=[*_halo_specs(tm, d, rows), _mods_spec(nct, groups, d),
                  pl.BlockSpec((None, pg, gw, gw), lambda i: (j, 0, 0, 0)), _full((1, d))],
        out_specs=_rows(tm, d),
        out_shape=jax.ShapeDtypeStruct((rows, d), F32),
        compiler_params=_params(),
    )(h, h, h, mods, pw, pscale[j:j + 1])


def _pool_bwd_weights(h, dh1, mods, pw, pscale, layer, *, nct, tm, seg_lens):
    rows, d = h.shape
    groups = mods.shape[0]
    pg, gw = pw.shape[1], pw.shape[-1]

    def body(prev_ref, cur_ref, next_ref, g_ref, md_ref, pw_ref, ps_ref, dp_ref, dmd_ref, dps_ref, dpw_ref):
        i = pl.program_id(0)
        md = md_ref[...]
        ext = jnp.concatenate([prev_ref[...], cur_ref[...], next_ref[...]], axis=0)
        pos, length = _segment_positions(i, tm, nct, groups, seg_lens)
        ps, _ = _pooled(ext, md, pos, length, gw)
        gup = g_ref[...]

        @pl.when(i == 0)
        def _():
            dps_ref[...] = jnp.zeros(dps_ref.shape, F32)
            dpw_ref[...] = jnp.zeros(dpw_ref.shape, F32)

        dg1 = []
        for g in range(pg):
            cols = slice(g * gw, (g + 1) * gw)
            pb = ps[g].astype(BF16)
            yp = _dot(pb, pw_ref[g])
            sc = ps_ref[:, cols]
            dg1.append(_rowsum(gup[:, cols] * (yp * sc)))
            dy = gup[:, cols] * md[G1:G1 + 1, cols]
            dps_ref[0:1, cols] += _rowsum(dy * yp)
            dyp = (dy * sc).astype(BF16)
            dp_ref[:, cols] = _dot_nt(dyp, pw_ref[g])
            dpw_ref[g] += _dot_tn(pb, dyp)
        _acc_rows(dmd_ref, _first_of_group(i, nct, groups), {G1: jnp.concatenate(dg1, axis=1)})

    j = layer // N_MIXERS
    return pl.pallas_call(
        body, name=f"pool_bwd_w_{layer}", grid=(rows // tm,),
        in_specs=[*_halo_specs(tm, d, rows), _rows(tm, d), _mods_spec(nct, groups, d),
                  pl.BlockSpec((None, pg, gw, gw), lambda i: (j, 0, 0, 0)), _full((1, d))],
        out_specs=[_rows(tm, d), _mods_spec(nct, groups, d), _full((8, d)), _full((pg, gw, gw))],
        out_shape=[jax.ShapeDtypeStruct((rows, d), F32), jax.ShapeDtypeStruct((groups, 8, d), F32),
                   jax.ShapeDtypeStruct((8, d), F32), jax.ShapeDtypeStruct((pg, gw, gw), F32)],
        compiler_params=_params(),
    )(h, h, h, dh1, mods, pw, pscale[j:j + 1])


def _pool_bwd_input(dp, h, dh1, mods, layer, *, nct, tm, seg_lens, gw):
    rows, d = h.shape
    groups = mods.shape[0]

    def body(prev_ref, cur_ref, next_ref, h_ref, g_ref, md_ref, dh_ref, dmd_ref):
        i = pl.program_id(0)
        md = md_ref[...]
        dp_cur = cur_ref[...]
        ext = jnp.concatenate([prev_ref[...], dp_cur, next_ref[...]], axis=0)
        pos, length = _segment_positions(i, tm, nct, groups, seg_lens)
        valid = jnp.logical_and(pos >= 0, pos < length)
        das = []
        for g, w in enumerate(POOL_WINDOWS):
            cols = slice(g * gw, (g + 1) * gw)
            q = jnp.where(valid, ext[:, cols] * (1.0 / jnp.maximum(_window_count(pos, length, w), 1.0)), 0.0)
            das.append(_window_sum(q, w, w // 2 - 1) - dp_cur[:, cols])
        da = jnp.concatenate(das, axis=1)
        _, parts = _normmod(h_ref[...], md, 0)
        dx, dsh, dsc, dng = _normmod_bwd(da, parts, md, 0)
        dh_ref[...] = g_ref[...] + dx
        _acc_rows(dmd_ref, _first_of_group(i, nct, groups), {SH1: dsh, SC1: dsc, NG0: dng})

    return pl.pallas_call(
        body, name=f"pool_bwd_x_{layer}", grid=(rows // tm,),
        in_specs=[*_halo_specs(tm, d, rows), _rows(tm, d), _rows(tm, d), _mods_spec(nct, groups, d)],
        out_specs=[pl.BlockSpec((tm, d), lambda i: (jnp.maximum(i - nct, 0), 0)), _mods_spec(nct, groups, d)],
        out_shape=[jax.ShapeDtypeStruct((rows - nct * tm, d), F32), jax.ShapeDtypeStruct((groups, 8, d), F32)],
        compiler_params=_params(),
    )(dp, dp, dp, h, dh1, mods)


def _rope_tables(n_ctx, seq, hd):
    half = hd // 2
    n_rows = seq // GRID_W
    inv = np.float32(ROPE_BASE) ** (-np.arange(0, half, 2, dtype=np.float32) / np.float32(half))
    ar = np.arange(n_rows, dtype=np.float32)[:, None] * inv[None, :]
    ac = np.arange(GRID_W, dtype=np.float32)[:, None] * inv[None, :]

    def over_tokens(row_part, col_part):
        r = jnp.repeat(jnp.asarray(row_part, F32), GRID_W, axis=0)
        c = jnp.tile(jnp.asarray(col_part, F32), (n_rows, 1))
        return r, c

    cr, cc = over_tokens(np.cos(ar), np.cos(ac))
    sr, sc = over_tokens(np.sin(ar), np.sin(ac))
    cos = jnp.concatenate([cr, cr, cc, cc], axis=1)
    sin = jnp.concatenate([-sr, sr, -sc, sc], axis=1)
    cos = jnp.concatenate([jnp.ones((n_ctx, hd), F32), cos], axis=0)
    sin = jnp.concatenate([jnp.zeros((n_ctx, hd), F32), sin], axis=0)
    return cos, sin


def _rope_partner(x):
    hd = x.shape[-1]
    q = hd // 4
    lane = lax.broadcasted_iota(jnp.int32, x.shape, 1)
    first = (lane % (2 * q)) < q
    return jnp.where(first, pltpu.roll(x, hd - q, axis=1), pltpu.roll(x, q, axis=1))


def _normmod_call(h, mods, which, name, *, nct, tm):
    rows, d = h.shape
    groups = mods.shape[0]

    def body(h_ref, md_ref, a_ref):
        a, _ = _normmod(h_ref[...], md_ref[...], which)
        a_ref[...] = a.astype(BF16)

    return pl.pallas_call(
        body, name=name, grid=(rows // tm,),
        in_specs=[_rows(tm, d), _mods_spec(nct, groups, d)],
        out_specs=_rows(tm, d), out_shape=jax.ShapeDtypeStruct((rows, d), BF16),
        compiler_params=_params(),
    )(h, mods)


def _qkv_fwd(xa, wqkv, cos, sin, gains, *, nh, nkv, nct, tm):
    rows, d = xa.shape
    qw = wqkv.shape[0] * wqkv.shape[-1]
    hd = cos.shape[-1]

    def body(x_ref, w_ref, cos_ref, sin_ref, gn_ref, qkv_ref, q_ref, k_ref, v_ref):
        qkv = _dot_blocks(x_ref[...], w_ref)
        qkv_ref[...] = qkv
        c, s = cos_ref[...], sin_ref[...]
        for hh in range(nh + nkv):
            xh = qkv[:, hh * hd:(hh + 1) * hd]
            xhat, _ = _rms_parts(xh)
            y = xhat * (gn_ref[0:1, :] if hh < nh else gn_ref[1:2, :])
            rot = (y * c + _rope_partner(y) * s).astype(BF16)
            if hh < nh:
                q_ref[:, hh * hd:(hh + 1) * hd] = rot
            else:
                k_ref[:, (hh - nh) * hd:(hh - nh + 1) * hd] = rot
        v_ref[...] = qkv[:, (nh + nkv) * hd:].astype(BF16)

    return pl.pallas_call(
        body, name="attn_qkv_fwd", grid=(rows // tm,),
        in_specs=[_rows(tm, d), _full(wqkv.shape), _rows(tm, hd), _rows(tm, hd), _full((8, hd))],
        out_specs=[_rows(tm, qw), pl.BlockSpec((tm, nh * hd), lambda i: (jnp.maximum(i - nct, 0), 0)),
                   _rows(tm, nkv * hd), _rows(tm, nkv * hd)],
        out_shape=[jax.ShapeDtypeStruct((rows, qw), F32), jax.ShapeDtypeStruct((rows - nct * tm, nh * hd), BF16),
                   jax.ShapeDtypeStruct((rows, nkv * hd), BF16), jax.ShapeDtypeStruct((rows, nkv * hd), BF16)],
        compiler_params=_params(),
    )(xa, wqkv, cos, sin, gains)


ATTN_Q_TILE_CAP = 1024
ATTN_KV_TILE_CAP = 4224
ATTN_ROW_GROUP = 256
LOG2E = 1.4426950408889634


def _attn_tiles(seq, total):
    tq = _div_tile(seq, ATTN_Q_TILE_CAP)
    return tq, _div_tile(total, ATTN_KV_TILE_CAP), min(ATTN_ROW_GROUP, tq)


def _flash_fwd(q, k, v, *, n_ctx, hd):
    total = k.shape[0]
    seq = total - n_ctx
    nkv = k.shape[1] // hd
    tq, tk, rg = _attn_tiles(seq, total)
    nk = total // tk
    scale = hd ** -0.5
    c2 = scale * LOG2E

    def body(q_ref, k_ref, v_ref, o_ref, lse_ref, m_sc, l_sc, acc_sc):
        ki = pl.program_id(2)

        @pl.when(ki == 0)
        def _():
            m_sc[...] = jnp.full(m_sc.shape, -jnp.inf, F32)
            l_sc[...] = jnp.zeros(l_sc.shape, F32)
            acc_sc[...] = jnp.zeros(acc_sc.shape, F32)

        kk, vv = k_ref[...], v_ref[...]
        groups = [(g, sub) for g in range(2) for sub in range(tq // rg)]

        def scores(g, sub):
            return _dot_nt(q_ref[sub * rg:(sub + 1) * rg, g * hd:(g + 1) * hd], kk)

        s_next = scores(*groups[0])
        for idx, (g, sub) in enumerate(groups):
            s = s_next
            if idx + 1 < len(groups):
                s_next = scores(*groups[idx + 1])
            rows = slice(g * tq + sub * rg, g * tq + (sub + 1) * rg)
            m_old = m_sc[rows]
            m_new = jnp.maximum(m_old, jnp.max(s, axis=-1, keepdims=True))
            alpha = jnp.exp2((m_old - m_new) * c2)
            p = jnp.exp2((s - m_new) * c2)
            l_sc[rows] = alpha * l_sc[rows] + jnp.sum(p, axis=-1, keepdims=True)
            acc_sc[rows] = alpha * acc_sc[rows] + _dot(p.astype(BF16), vv)
            m_sc[rows] = m_new

        @pl.when(ki == nk - 1)
        def _():
            o2 = acc_sc[...] / l_sc[...]
            lse = m_sc[...] * scale + jnp.log(l_sc[...])
            o_ref[:, :hd] = o2[:tq].astype(BF16)
            o_ref[:, hd:] = o2[tq:].astype(BF16)
            lse_ref[:, 0:1] = lse[:tq]
            lse_ref[:, 1:2] = lse[tq:]

    return pl.pallas_call(
        body, name="attn_flash_fwd", grid=(nkv, seq // tq, nk),
        in_specs=[pl.BlockSpec((tq, 2 * hd), lambda h, i, j: (i, h)),
                  pl.BlockSpec((tk, hd), lambda h, i, j: (j, h)),
                  pl.BlockSpec((tk, hd), lambda h, i, j: (j, h))],
        out_specs=[pl.BlockSpec((tq, 2 * hd), lambda h, i, j: (i, h)),
                   pl.BlockSpec((None, tq, 2), lambda h, i, j: (h, i, 0))],
        out_shape=[jax.ShapeDtypeStruct((seq, 2 * nkv * hd), BF16), jax.ShapeDtypeStruct((nkv, seq, 2), F32)],
        scratch_shapes=[pltpu.VMEM((2 * tq, 1), F32), pltpu.VMEM((2 * tq, 1), F32), pltpu.VMEM((2 * tq, hd), F32)],
        compiler_params=_params(),
    )(q, k, v)


def _flash_bwd(q, k, v, o, do, lse, *, n_ctx, hd):
    total = k.shape[0]
    seq = total - n_ctx
    nkv = k.shape[1] // hd
    tq, tk, rg = _attn_tiles(seq, total)
    scale = hd ** -0.5
    c2 = scale * LOG2E

    def body(q_ref, k_ref, v_ref, o_ref, do_ref, lse_ref, dq_ref, dk_ref, dv_ref):
        ki, qi = pl.program_id(1), pl.program_id(2)
        kk, vv = k_ref[...], v_ref[...]

        @pl.when(qi == 0)
        def _():
            dk_ref[...] = jnp.zeros(dk_ref.shape, F32)
            dv_ref[...] = jnp.zeros(dv_ref.shape, F32)

        dk_acc = jnp.zeros((tk, hd), F32)
        dv_acc = jnp.zeros((tk, hd), F32)
        for g in range(2):
            for sub in range(tq // rg):
                rs = slice(sub * rg, (sub + 1) * rg)
                cs = slice(g * hd, (g + 1) * hd)
                qq = q_ref[rs, cs]
                dd = do_ref[rs, cs]
                delta = jnp.sum(dd.astype(F32) * o_ref[rs, cs].astype(F32), axis=-1, keepdims=True)
                p = jnp.exp2(_dot_nt(qq, kk) * c2 - lse_ref[rs, g:g + 1] * LOG2E)
                dp = _dot_nt(dd, vv)
                ds = (p * (dp - delta) * scale).astype(BF16)
                dv_acc = dv_acc + _dot_tn(p.astype(BF16), dd)
                dk_acc = dk_acc + _dot_tn(ds, qq)
                dq = _dot(ds, kk)
                rows = pl.ds(pl.multiple_of(qi * tq, tq) + sub * rg, rg)

                @pl.when(ki == 0)
                def _():
                    dq_ref[rows, cs] = dq

                @pl.when(ki > 0)
                def _():
                    dq_ref[rows, cs] += dq
        dk_ref[...] += dk_acc
        dv_ref[...] += dv_acc

    return pl.pallas_call(
        body, name="attn_flash_bwd", grid=(nkv, total // tk, seq // tq),
        in_specs=[pl.BlockSpec((tq, 2 * hd), lambda h, j, i: (i, h)),
                  pl.BlockSpec((tk, hd), lambda h, j, i: (j, h)),
                  pl.BlockSpec((tk, hd), lambda h, j, i: (j, h)),
                  pl.BlockSpec((tq, 2 * hd), lambda h, j, i: (i, h)),
                  pl.BlockSpec((tq, 2 * hd), lambda h, j, i: (i, h)),
                  pl.BlockSpec((None, tq, 2), lambda h, j, i: (h, i, 0))],
        out_specs=[pl.BlockSpec((seq, 2 * hd), lambda h, j, i: (0, h)),
                   pl.BlockSpec((tk, hd), lambda h, j, i: (j, h)),
                   pl.BlockSpec((tk, hd), lambda h, j, i: (j, h))],
        out_shape=[jax.ShapeDtypeStruct((seq, 2 * nkv * hd), F32), jax.ShapeDtypeStruct((total, nkv * hd), F32),
                   jax.ShapeDtypeStruct((total, nkv * hd), F32)],
        compiler_params=_params(),
    )(q, k, v, o, do, lse)


def _proj_fwd(o, wo, hc, mods, *, n_ctx, tm):
    seq, d = o.shape
    off = n_ctx // tm

    def body(o_ref, w_ref, h_ref, md_ref, h1_ref, y_ref):
        y = _dot(o_ref[...], w_ref[...])
        y_ref[...] = y.astype(BF16)
        h1_ref[...] = h_ref[...] + md_ref[G1:G1 + 1, :] * y

    return pl.pallas_call(
        body, name="attn_proj_fwd", grid=(seq // tm,),
        in_specs=[_rows(tm, d), _full((d, d)),
                  pl.BlockSpec((tm, d), lambda i: (i + off, 0)), pl.BlockSpec((None, 8, d), lambda i: (1, 0, 0))],
        out_specs=[_rows(tm, d), _rows(tm, d)],
        out_shape=[jax.ShapeDtypeStruct((seq, d), F32), jax.ShapeDtypeStruct((seq, d), BF16)],
        compiler_params=_params(),
    )(o, wo, hc, mods)


def _proj_bwd(dh1, y, mods, wo, *, tm):
    seq, d = dh1.shape

    def body(g_ref, y_ref, md_ref, w_ref, do_ref, dyb_ref, dmd_ref):
        i = pl.program_id(0)
        g = g_ref[...]
        dyb = (g * md_ref[G1:G1 + 1, :]).astype(BF16)
        dyb_ref[...] = dyb
        do_ref[...] = _dot_nt(dyb, w_ref[...]).astype(BF16)
        _acc_rows(dmd_ref, i == 0, {G1: _rowsum(g * y_ref[...].astype(F32))})

    return pl.pallas_call(
        body, name="attn_proj_bwd", grid=(seq // tm,),
        in_specs=[_rows(tm, d), _rows(tm, d), pl.BlockSpec((None, 8, d), lambda i: (1, 0, 0)), _full((d, d))],
        out_specs=[_rows(tm, d), _rows(tm, d), pl.BlockSpec((None, 8, d), lambda i: (0, 0, 0))],
        out_shape=[jax.ShapeDtypeStruct((seq, d), BF16), jax.ShapeDtypeStruct((seq, d), BF16),
                   jax.ShapeDtypeStruct((1, 8, d), F32)],
        compiler_params=_params(),
    )(dh1, y, mods, wo)


def _qkv_bwd(qkv, dq, dk, dv, cos, sin, gains, *, nh, nkv, nct, tm):
    rows, qw = qkv.shape
    hd = cos.shape[-1]

    def body(qkv_ref, dq_ref, dk_ref, dv_ref, cos_ref, sin_ref, gn_ref, out_ref, dgn_ref):
        i = pl.program_id(0)
        c, s = cos_ref[...], sin_ref[...]
        is_lat = (i >= nct).astype(F32)
        dqg = jnp.zeros((1, hd), F32)
        dkg = jnp.zeros((1, hd), F32)
        for hh in range(nh + nkv):
            if hh < nh:
                dr = dq_ref[:, hh * hd:(hh + 1) * hd] * is_lat
                gn = gn_ref[0:1, :]
            else:
                dr = dk_ref[:, (hh - nh) * hd:(hh - nh + 1) * hd]
                gn = gn_ref[1:2, :]
            dy = dr * c + _rope_partner(dr * s)
            xhat, r = _rms_parts(qkv_ref[:, hh * hd:(hh + 1) * hd])
            dgh = _rowsum(dy * xhat)
            if hh < nh:
                dqg = dqg + dgh
            else:
                dkg = dkg + dgh
            dxhat = dy * gn
            dx = r * (dxhat - xhat * jnp.mean(dxhat * xhat, axis=-1, keepdims=True))
            out_ref[:, hh * hd:(hh + 1) * hd] = dx.astype(BF16)
        out_ref[:, (nh + nkv) * hd:] = dv_ref[...].astype(BF16)
        _acc_rows(dgn_ref, i == 0, {0: dqg, 1: dkg})

    return pl.pallas_call(
        body, name="attn_qkv_bwd", grid=(rows // tm,),
        in_specs=[_rows(tm, qw), pl.BlockSpec((tm, nh * hd), lambda i: (jnp.maximum(i - nct, 0), 0)),
                  _rows(tm, nkv * hd), _rows(tm, nkv * hd), _rows(tm, hd), _rows(tm, hd), _full((8, hd))],
        out_specs=[_rows(tm, qw), _full((8, hd))],
        out_shape=[jax.ShapeDtypeStruct((rows, qw), BF16), jax.ShapeDtypeStruct((8, hd), F32)],
        compiler_params=_params(),
    )(qkv, dq, dk, dv, cos, sin, gains)


def _attn_in_bwd(dqkv, wqkv, hc, dh1, mods, *, nct, tm):
    rows, d = hc.shape
    qw = dqkv.shape[1]

    def body(dz_ref, w_ref, h_ref, g_ref, md_ref, dh_ref, dmd_ref):
        i = pl.program_id(0)
        md = md_ref[...]
        da = _dot_nt_blocks(dz_ref[...], w_ref)
        _, parts = _normmod(h_ref[...], md, 0)
        dx, dsh, dsc, dng = _normmod_bwd(da, parts, md, 0)
        dh_ref[...] = g_ref[...] * (i >= nct).astype(F32) + dx
        _acc_rows(dmd_ref, _first_of_group(i, nct, 2), {SH1: dsh, SC1: dsc, NG0: dng})

    return pl.pallas_call(
        body, name="attn_in_bwd", grid=(rows // tm,),
        in_specs=[_rows(tm, qw), _full(wqkv.shape), _rows(tm, d),
                  pl.BlockSpec((tm, d), lambda i: (jnp.maximum(i - nct, 0), 0)), _mods_spec(nct, 2, d)],
        out_specs=[_rows(tm, d), _mods_spec(nct, 2, d)],
        out_shape=[jax.ShapeDtypeStruct((rows, d), F32), jax.ShapeDtypeStruct((2, 8, d), F32)],
        compiler_params=_params(),
    )(dqkv, wqkv, hc, dh1, mods)


def _gmlp_gate(zp, lng, lnb, ws_ref, bs_ref, gg, ch):
    half = zp.shape[1] // 2
    ggw = half // gg
    z = _gelu(zp)
    u, v = z[:, :half], z[:, half:]
    vc = v - jnp.mean(v, axis=-1, keepdims=True)
    rs = lax.rsqrt(jnp.mean(vc * vc, axis=-1, keepdims=True) + EPS)
    vhat = vc * rs
    vln = (vhat * lng + lnb).astype(BF16)
    chunks = []
    for n in range(zp.shape[0] // ch):
        groups = []
        for g in range(gg):
            groups.append(_dot(ws_ref[g], vln[n * ch:(n + 1) * ch, g * ggw:(g + 1) * ggw]) + bs_ref[g])
        chunks.append(jnp.concatenate(groups, axis=1))
    sv = jnp.concatenate(chunks, axis=0) if len(chunks) > 1 else chunks[0]
    return u, sv, vhat, rs, vln


def _gmlp_fwd(h, mods, w_in, lng, lnb, ws, bs, w_out, *, tm):
    seq, d = h.shape
    zw = w_in.shape[0] * w_in.shape[-1]
    half = zw // 2
    gg, ch = ws.shape[0], ws.shape[-1]

    def body(h_ref, md_ref, win_ref, lng_ref, lnb_ref, ws_ref, bs_ref, wout_ref, h1_ref, zp_ref, y_ref):
        x = h_ref[...]
        md = md_ref[...]
        a, _ = _normmod(x, md, 0)
        zp = _dot_blocks(a.astype(BF16), win_ref)
        zp_ref[...] = zp.astype(BF16)
        u, sv, _, _, _ = _gmlp_gate(zp, lng_ref[...], lnb_ref[...], ws_ref, bs_ref, gg, ch)
        y = _dot((u * sv).astype(BF16), wout_ref[...])
        y_ref[...] = y.astype(BF16)
        h1_ref[...] = x + md[G1:G1 + 1] * y

    return pl.pallas_call(
        body, name="gmlp_fwd", grid=(seq // tm,),
        in_specs=[_rows(tm, d), pl.BlockSpec((None, 8, d), lambda i: (1, 0, 0)),
                  _full(w_in.shape), _full((1, half)), _full((1, half)),
                  _full((gg, ch, ch)), _full((gg, ch, 1)), _full((half, d))],
        out_specs=[_rows(tm, d), _rows(tm, zw), _rows(tm, d)],
        out_shape=[jax.ShapeDtypeStruct((seq, d), F32), jax.ShapeDtypeStruct((seq, zw), BF16),
                   jax.ShapeDtypeStruct((seq, d), BF16)],
        compiler_params=_params(),
    )(h, mods, w_in, lng, lnb, ws, bs, w_out)


def _gmlp_bwd(h, dh1, zpre, y, mods, w_in, lng, lnb, ws, ws_t, bs, w_out, *, tm):
    seq, d = h.shape
    zw = w_in.shape[0] * w_in.shape[-1]
    half = zw // 2
    gg, ch = ws.shape[0], ws.shape[-1]
    ggw = half // gg

    def body(h_ref, g_ref, zp_ref, y_ref, md_ref, win_ref, lng_ref, lnb_ref, ws_ref, wst_ref, bs_ref, wout_ref,
             dh_ref, dzp_ref, gated_ref, dyb_ref, ab_ref, dmd_ref, dln_ref, dws_ref, dbs_ref):
        i = pl.program_id(0)
        x = h_ref[...]
        md = md_ref[...]
        a, parts = _normmod(x, md, 0)
        ab_ref[...] = a.astype(BF16)
        zp = zp_ref[...].astype(F32)
        lng_v = lng_ref[...]
        u, sv, vhat, rs, vln = _gmlp_gate(zp, lng_v, lnb_ref[...], ws_ref, bs_ref, gg, ch)
        g = g_ref[...]
        dg1 = _rowsum(g * y_ref[...].astype(F32))
        dyb = (g * md[G1:G1 + 1]).astype(BF16)
        dyb_ref[...] = dyb
        gated_ref[...] = (u * sv).astype(BF16)
        dgated = _dot_nt(dyb, wout_ref[...])
        du = dgated * sv
        dsv = dgated * u

        @pl.when(i == 0)
        def _():
            dws_ref[...] = jnp.zeros(dws_ref.shape, F32)
            dbs_ref[...] = jnp.zeros(dbs_ref.shape, F32)
            dln_ref[...] = jnp.zeros(dln_ref.shape, F32)

        chunks = []
        for n in range(tm // ch):
            groups = []
            for gi in range(gg):
                blk = dsv[n * ch:(n + 1) * ch, gi * ggw:(gi + 1) * ggw]
                dbs_ref[gi] += jnp.sum(blk, axis=-1, keepdims=True)
                blk_b = blk.astype(BF16)
                dws_ref[gi] += _dot_nt(blk_b, vln[n * ch:(n + 1) * ch, gi * ggw:(gi + 1) * ggw])
                groups.append(_dot(wst_ref[gi], blk_b))
            chunks.append(jnp.concatenate(groups, axis=1))
        dvln = jnp.concatenate(chunks, axis=0) if len(chunks) > 1 else chunks[0]
        dln_ref[0:1, :] += _rowsum(dvln * vhat)
        dln_ref[1:2, :] += _rowsum(dvln)
        dvhat = dvln * lng_v
        dv = rs * (dvhat - jnp.mean(dvhat, axis=-1, keepdims=True)
                   - vhat * jnp.mean(dvhat * vhat, axis=-1, keepdims=True))
        dzp = (jnp.concatenate([du, dv], axis=1) * _gelu_grad(zp)).astype(BF16)
        dzp_ref[...] = dzp
        da = _dot_nt_blocks(dzp, win_ref)
        dx, dsh, dsc, dng = _normmod_bwd(da, parts, md, 0)
        dh_ref[...] = g + dx
        _acc_rows(dmd_ref, i == 0, {SH1: dsh, SC1: dsc, G1: dg1, NG0: dng})

    return pl.pallas_call(
        body, name="gmlp_bwd", grid=(seq // tm,),
        in_specs=[_rows(tm, d), _rows(tm, d), _rows(tm, zw), _rows(tm, d),
                  pl.BlockSpec((None, 8, d), lambda i: (1, 0, 0)),
                  _full(w_in.shape), _full((1, half)), _full((1, half)),
                  _full((gg, ch, ch)), _full((gg, ch, ch)), _full((gg, ch, 1)), _full((half, d))],
        out_specs=[_rows(tm, d), _rows(tm, zw), _rows(tm, half), _rows(tm, d), _rows(tm, d),
                   pl.BlockSpec((None, 8, d), lambda i: (0, 0, 0)), _full((8, half)), _full((gg, ch, ch)),
                   _full((gg, ch, 1))],
        out_shape=[jax.ShapeDtypeStruct((seq, d), F32), jax.ShapeDtypeStruct((seq, zw), BF16),
                   jax.ShapeDtypeStruct((seq, half), BF16), jax.ShapeDtypeStruct((seq, d), BF16),
                   jax.ShapeDtypeStruct((seq, d), BF16), jax.ShapeDtypeStruct((1, 8, d), F32),
                   jax.ShapeDtypeStruct((8, half), F32), jax.ShapeDtypeStruct((gg, ch, ch), F32),
                   jax.ShapeDtypeStruct((gg, ch, 1), F32)],
        compiler_params=_params(),
    )(h, dh1, zpre, y, mods, w_in, lng, lnb, ws, ws_t, bs, w_out)


def _final_loss(h, tgt, fg, *, tm):
    seq, d = h.shape

    def body(h_ref, t_ref, g_ref, dh_ref, acc_ref):
        i = pl.program_id(0)
        gain = g_ref[...]
        xhat, r = _rms_parts(h_ref[...])
        err = xhat * gain - t_ref[...]
        dy = err * (1.0 / d)
        dxhat = dy * gain
        dh_ref[...] = r * (dxhat - xhat * jnp.mean(dxhat * xhat, axis=-1, keepdims=True))
        part = jnp.sum(_rowsum(err * err), axis=-1, keepdims=True) * (0.5 / d)
        _acc_rows(acc_ref, i == 0, {0: _rowsum(dy * xhat), 1: jnp.broadcast_to(part, (1, d))})

    return pl.pallas_call(
        body, name="final_loss", grid=(seq // tm,),
        in_specs=[_rows(tm, d), _rows(tm, d), _full((1, d))],
        out_specs=[_rows(tm, d), _full((8, d))],
        out_shape=[jax.ShapeDtypeStruct((seq, d), F32), jax.ShapeDtypeStruct((8, d), F32)],
        compiler_params=_params(),
    )(h, tgt, fg)


def _ada_fwd(c_all, ada_w, ada_b_cols):
    depth, d, ncs = ada_w.shape

    def body(c_ref, w_ref, b_ref, o_ref):
        s = _silu(c_ref[...]).astype(BF16)
        o_ref[...] = _dot(s, w_ref[...].astype(BF16)) + b_ref[...]

    return pl.pallas_call(
        body, name="ada_fwd", grid=(depth,),
        in_specs=[_full((16, d)), pl.BlockSpec((None, d, ncs), lambda i: (i, 0, 0)),
                  pl.BlockSpec((None, 1, ncs), lambda i: (i, 0, 0))],
        out_specs=pl.BlockSpec((None, 16, ncs), lambda i: (i, 0, 0)),
        out_shape=jax.ShapeDtypeStruct((depth, 16, ncs), F32),
        compiler_params=_params(),
    )(c_all, ada_w, ada_b_cols.reshape(depth, 1, ncs))


def _ada_bwd(c_all, c_all_t, dmod, ada_w):
    depth, d, ncs = ada_w.shape

    def body(c_ref, ct_ref, dm_ref, w_ref, gw_ref, dc_ref):
        i = pl.program_id(0)
        dm = dm_ref[...]
        dctx = _rowsum(dm[8:16])
        rid = lax.broadcasted_iota(jnp.int32, (8, ncs), 0)
        low = jnp.where(rid == 0, jnp.broadcast_to(dctx, (8, ncs)), 0.0)
        dm16 = jnp.concatenate([dm[0:8], low], axis=0).astype(BF16)
        gw_ref[...] = _dot(_silu(ct_ref[...]).astype(BF16), dm16)

        @pl.when(i == 0)
        def _():
            dc_ref[...] = jnp.zeros(dc_ref.shape, F32)

        dc_ref[...] += _dot_nt(low.astype(BF16), w_ref[...].astype(BF16)) * _silu_grad(c_ref[8:9, :])

    return pl.pallas_call(
        body, name="ada_bwd", grid=(depth,),
        in_specs=[_full((16, d)), _full((d, 16)), pl.BlockSpec((None, 16, ncs), lambda i: (i, 0, 0)),
                  pl.BlockSpec((None, d, ncs), lambda i: (i, 0, 0))],
        out_specs=[pl.BlockSpec((None, d, ncs), lambda i: (i, 0, 0)), _full((8, d))],
        out_shape=[jax.ShapeDtypeStruct((depth, d, ncs), F32), jax.ShapeDtypeStruct((8, d), F32)],
        compiler_params=_params(),
    )(c_all, c_all_t, dmod, ada_w)


def _adamw_math(w, g, m, v):
    m = ADAM_B1 * m + (1.0 - ADAM_B1) * g
    v = ADAM_B2 * v + (1.0 - ADAM_B2) * jnp.square(g)
    m_hat = m * (1.0 / (1.0 - ADAM_B1 ** ADAM_STEP))
    v_hat = v * (1.0 / (1.0 - ADAM_B2 ** ADAM_STEP))
    delta = -ADAM_LR * (m_hat / (jnp.sqrt(v_hat) + ADAM_EPS) + ADAM_WD * w)
    return delta, m, v


def _adamw(ga, gb, w, m, v, name):
    rows, cols = w.shape
    tr = rows
    while tr * cols * 4 > (1 << 20) and tr % 16 == 0:
        tr //= 2
    grads = [ga] if gb is None else [ga, gb]

    def body(*refs):
        w_ref, m_ref, v_ref, g_out, d_out, m_out, v_out = refs[len(grads):]
        g = refs[0][...] if gb is None else refs[0][...] + refs[1][...]
        delta, m_new, v_new = _adamw_math(w_ref[...], g, m_ref[...], v_ref[...])
        g_out[...] = g
        d_out[...] = delta
        m_out[...] = m_new
        v_out[...] = v_new

    spec = _rows(tr, cols)
    return pl.pallas_call(
        body, name=name, grid=(rows // tr,),
        in_specs=[spec] * (len(grads) + 3), out_specs=[spec] * 4,
        out_shape=[jax.ShapeDtypeStruct((rows, cols), F32)] * 4,
        compiler_params=_params(),
    )(*grads, w, m, v)


def _sum_devices(gathered, name):
    n, rows, cols = gathered.shape
    tr = rows
    while tr * cols * 4 * n > (4 << 20) and tr % 16 == 0:
        tr //= 2

    def body(x_ref, o_ref):
        acc = x_ref[0]
        for j in range(1, n):
            acc = acc + x_ref[j]
        o_ref[...] = acc

    return pl.pallas_call(
        body, name=name, grid=(rows // tr,),
        in_specs=[pl.BlockSpec((n, tr, cols), lambda i: (0, i, 0))], out_specs=_rows(tr, cols),
        out_shape=jax.ShapeDtypeStruct((rows, cols), F32),
        compiler_params=_params(),
    )(gathered)


def _sum_partials(blocked, landeds, chip, name):
    n = len(blocked)
    cols = blocked[0].shape[-1]
    blocked = [b.reshape(N_CHIPS, -1, cols) for b in blocked]
    landeds = [l.reshape(3, -1, cols) for l in landeds]
    rows = blocked[0].shape[1]
    tr = rows
    while tr * cols * 2 * n > (1 << 20) and tr % 32 == 0:
        tr //= 2

    def body(chip_ref, *refs):
        out_ref = refs[-1]
        for li in range(n):
            acc = refs[li][...].astype(F32)
            for p in range(3):
                acc = acc + refs[n + li][p].astype(F32)
            out_ref[li] = acc

    out = pl.pallas_call(
        body, name=name,
        grid_spec=pltpu.PrefetchScalarGridSpec(
            num_scalar_prefetch=1, grid=(rows // tr,),
            in_specs=[pl.BlockSpec((None, tr, cols), lambda i, k: (k[0], i, 0))] * n
            + [pl.BlockSpec((3, tr, cols), lambda i, k: (0, i, 0))] * n,
            out_specs=pl.BlockSpec((n, tr, cols), lambda i, k: (0, i, 0))),
        out_shape=jax.ShapeDtypeStruct((n, rows, cols), F32),
        compiler_params=_params(),
    )(jnp.reshape(chip, (1,)).astype(jnp.int32), *blocked, *landeds)
    return out.reshape(n * rows, cols)


def _my_place():
    return lax.axis_index("x"), lax.axis_index("y"), lax.axis_index("c")


def _other_chips(x, y):
    return [(1 - x, y), (x, 1 - y), (1 - x, 1 - y)]


def _all_gather_small(block, name):
    rows, cols = block.shape

    def body(x_ref, out_ref, send_sems, recv_sems, local_sem):
        x, y, c = _my_place()
        me, sibling = (x, y, c), (x, y, 1 - c)
        chips = _other_chips(x, y)

        def slot(px, py, pc):
            return out_ref.at[4 * px + 2 * py + pc]

        def copy(k, blk, to, src=None):
            return pltpu.make_async_remote_copy(
                src_ref=slot(*blk) if src is None else src, dst_ref=slot(*blk),
                send_sem=send_sems.at[k], recv_sem=recv_sems.at[k], device_id=to, device_id_type=MESH)

        mine = pltpu.make_async_copy(x_ref, slot(*me), local_sem)
        mine.start()
        first = [copy(0, me, sibling, src=x_ref)]
        first += [copy(1 + j, me, (*chip, c), src=x_ref) for j, chip in enumerate(chips)]
        for cp in first:
            cp.start()
        passed = [copy(4 + j, (*chip, c), sibling) for j, chip in enumerate(chips)]
        for j, chip in enumerate(chips):
            copy(1 + j, (*chip, c), me).wait_recv()
            passed[j].start()
        copy(0, sibling, me).wait_recv()
        for j, chip in enumerate(chips):
            copy(4 + j, (*chip, 1 - c), me).wait_recv()
        for cp in first + passed:
            cp.wait_send()
        mine.wait()

    return pl.pallas_call(
        body, name=name,
        out_shape=jax.ShapeDtypeStruct((N_DEV, rows, cols), block.dtype),
        in_specs=[pl.BlockSpec(memory_space=pltpu.VMEM)],
        out_specs=pl.BlockSpec(memory_space=pltpu.VMEM),
        scratch_shapes=[pltpu.SemaphoreType.DMA((7,)), pltpu.SemaphoreType.DMA((7,)), pltpu.SemaphoreType.DMA],
        compiler_params=_params(),
    )(block)


HBM_SPEC = pl.BlockSpec(memory_space=pltpu.HBM)
SEM_SPEC = pl.BlockSpec(memory_space=pltpu.SEMAPHORE)
DATAFLOW_EFFECT = pltpu.SideEffectType.DATAFLOW_SIDE_EFFECTING


def _same_core_of_other_chips(x, y, c):
    return [(*chip, c) for chip in _other_chips(x, y)]


def _sibling_core(x, y, c):
    return [(x, y, 1 - c)]


def _gather_views(src, land, p, x, y):
    return src, land.at[2 * x + y]


def _scatter_views(src, land, p, x, y):
    peer_chip = (2 * (1 - x) + y, 2 * x + (1 - y), 2 * (1 - x) + (1 - y))[p]
    return src.at[peer_chip], land.at[p]


def _whole_views(src, land, p, x, y):
    return src, land


GATHER_PLAN = (_same_core_of_other_chips, _gather_views, 3)
SCATTER_PLAN = (_same_core_of_other_chips, _scatter_views, 3)
SIBLING_PLAN = (_sibling_core, _whole_views, 1)


def _exchange_copies(srcs, lands, send_sems, recv_sems, plan):
    peers_of, views, n_peers = plan
    x, y, c = _my_place()
    copies = []
    for j, (src, land) in enumerate(zip(srcs, lands)):
        for p, peer in enumerate(peers_of(x, y, c)):
            s_view, d_view = views(src, land, p, x, y)
            k = n_peers * j + p
            copies.append(pltpu.make_async_remote_copy(
                src_ref=s_view, dst_ref=d_view, send_sem=send_sems.at[k], recv_sem=recv_sems.at[k],
                device_id=peer, device_id_type=MESH))
    return copies


def _exchange_start(srcs, lands, plan, name):
    n = len(srcs)

    def body(*refs):
        send_sems, recv_sems = refs[2 * n], refs[2 * n + 1]
        token = refs[-1]
        for cp in _exchange_copies(refs[:n], refs[n:2 * n], send_sems, recv_sems, plan):
            cp.start()
        token[...] = jnp.zeros(token.shape, token.dtype)

    operands = [pltpu.with_memory_space_constraint(a, pltpu.HBM) for a in (*srcs, *lands)]
    out = pl.pallas_call(
        body, name=name,
        out_shape=(pltpu.SemaphoreType.DMA((plan[2] * n,)), pltpu.SemaphoreType.DMA((plan[2] * n,)),
                   *[pltpu.HBM(a.shape, a.dtype) for a in operands], jax.ShapeDtypeStruct((8, LANES), F32)),
        in_specs=[HBM_SPEC] * (2 * n),
        out_specs=(SEM_SPEC, SEM_SPEC, *[HBM_SPEC] * (2 * n), pl.BlockSpec(memory_space=pltpu.VMEM)),
        input_output_aliases={i: 2 + i for i in range(2 * n)},
        compiler_params=pltpu.CompilerParams(has_side_effects=DATAFLOW_EFFECT),
    )(*operands)
    return out[0], out[1], list(out[2:2 + n]), list(out[2 + n:2 + 2 * n]), out[-1]


def _exchange_wait(send_sems, recv_sems, srcs, lands, plan, after, name):
    n = len(srcs)

    def body(*refs):
        send, recv = refs[2 * n], refs[2 * n + 1]
        for cp in _exchange_copies(refs[:n], refs[n:2 * n], send, recv, plan):
            cp.wait_send()
            cp.wait_recv()

    out = pl.pallas_call(
        body, name=name,
        out_shape=tuple(pltpu.HBM(a.shape, a.dtype) for a in (*srcs, *lands)),
        in_specs=[HBM_SPEC] * (2 * n) + [SEM_SPEC, SEM_SPEC, HBM_SPEC],
        out_specs=tuple([HBM_SPEC] * (2 * n)),
        input_output_aliases={i: i for i in range(2 * n)},
        compiler_params=pltpu.CompilerParams(has_side_effects=DATAFLOW_EFFECT),
    )(*srcs, *lands, send_sems, recv_sems, pltpu.with_memory_space_constraint(after, pltpu.HBM))
    return list(out[:n]), list(out[n:])


def _landing_for_gather(shard, chip):
    land = lax.empty((N_CHIPS, *shard.shape), shard.dtype)
    return lax.dynamic_update_index_in_dim(land, shard, chip, 0)


TILE_ELEMS = SUBLANES * LANES


def _pack(arrays):
    parts = []
    for a in arrays:
        flat = a.reshape(-1).astype(F32)
        pad = (-flat.shape[0]) % TILE_ELEMS
        if pad:
            flat = jnp.concatenate([flat, jnp.zeros((pad,), F32)])
        parts.append(flat.reshape(-1, LANES))
    return jnp.concatenate(parts, axis=0) if len(parts) > 1 else parts[0]


def _unpack(buf, shapes):
    out, r = [], 0
    lead = buf.shape[:-2]
    for shp in shapes:
        size = math.prod(shp)
        nr = -(-size // TILE_ELEMS) * SUBLANES
        flat = buf[..., r:r + nr, :].reshape(*lead, nr * LANES)[..., :size]
        out.append(flat.reshape(*lead, *shp))
        r += nr
    return out


def _chip_cols(a, k, width):
    return lax.dynamic_slice_in_dim(a, k * width, width, axis=a.ndim - 1)


def _across_chips(gathered, c0_only_shape):
    return gathered.reshape(2, 2, 2, *c0_only_shape)[:, :, 0].reshape(N_CHIPS, *c0_only_shape)


def kernel(x, c, ctx, c_ctx, ada_w, ada_b, norm_g, mlp_w1, mlp_w2, pool_w, pool_scale, attn_w_qkv, attn_w_o, attn_q_g, attn_k_g, gm_w_in, gm_ln_g, gm_ln_b, gm_ws, gm_bs, gm_w_out, final_g, loss_target, m_c_ctx, m_ada_w, m_ada_b, m_norm_g, m_mlp_w1, m_mlp_w2, m_pool_w, m_pool_scale, m_attn_w_qkv, m_attn_w_o, m_attn_q_g, m_attn_k_g, m_gm_w_in, m_gm_ln_g, m_gm_ln_b, m_gm_ws, m_gm_bs, m_gm_w_out, m_final_g, v_c_ctx, v_ada_w, v_ada_b, v_norm_g, v_mlp_w1, v_mlp_w2, v_pool_w, v_pool_scale, v_attn_w_qkv, v_attn_w_o, v_attn_q_g, v_attn_k_g, v_gm_w_in, v_gm_ln_g, v_gm_ln_b, v_gm_ws, v_gm_bs, v_gm_w_out, v_final_g):
    seq, d = x.shape[1], x.shape[2]
    n_ctx = ctx.shape[1]
    total = n_ctx + seq
    hd = attn_q_g.shape[-1]
    nh = d // hd
    nkv = nh // 2
    gg, ch = gm_ws.shape[1], gm_ws.shape[-1]
    half = gm_w_out.shape[1] * N_CHIPS
    pgw = pool_w.shape[-1]
    tm = min(256, n_ctx)
    nct = n_ctx // tm
    seg_lens = (n_ctx, seq)

    mx, my, mc = _my_place()
    chip = 2 * mx + my
    me = 4 * mx + 2 * my + mc

    c_rows = jnp.concatenate([c, jnp.zeros((7, d), F32)], axis=0)
    c_gath = _all_gather_small(c_rows, "gather_cond")[:, 0, :]
    c_all = jnp.concatenate([c_gath, c_ctx[None, :], jnp.zeros((7, d), F32)], axis=0)
    ncs = ada_w.shape[-1]
    ada_cols = _ada_fwd(c_all, ada_w, _chip_cols(ada_b, chip, ncs))
    small_shapes = [ada_cols.shape, norm_g.shape, pool_scale.shape, gm_ln_g.shape, gm_ln_b.shape]
    gathered = _all_gather_small(_pack([ada_cols, norm_g, pool_scale, gm_ln_g, gm_ln_b]), "gather_small_params")
    per_chip = _across_chips(gathered, gathered.shape[1:])
    ada_g, ng_g, ps_g, lng_g, lnb_g = _unpack(per_chip, small_shapes)

    def join_last(a):
        return jnp.moveaxis(a, 0, -2).reshape(*a.shape[1:-1], N_CHIPS * a.shape[-1])

    ada_full = join_last(ada_g)
    ng_full = join_last(ng_g)
    ps_full = join_last(ps_g)
    lng_full = join_last(lng_g)
    lnb_full = join_last(lnb_g)
    mod_lat = lax.dynamic_slice_in_dim(ada_full, me, 1, axis=1).reshape(DEPTH, 6, d)
    mod_ctx = ada_full[:, 8].reshape(DEPTH, 6, d)
    mods = jnp.stack([jnp.concatenate([mod_ctx, ng_full], axis=1), jnp.concatenate([mod_lat, ng_full], axis=1)],
                     axis=1)

    weight_groups = [
        [pool_w],
        [mlp_w1[0], mlp_w2[0]],
        [attn_w_qkv[0], attn_w_o[0]],
        [mlp_w1[1], mlp_w2[1], mlp_w1[2], mlp_w2[2], gm_w_in[0], gm_w_out[0], mlp_w1[3], mlp_w2[3]],
    ]
    gathers = [None] * len(weight_groups)

    def gather_start(gi, after):
        shards, _ = lax.optimization_barrier(([w.astype(BF16) for w in weight_groups[gi]], after))
        lands = [_landing_for_gather(s, chip) for s in shards]
        gathers[gi] = _exchange_start(shards, lands, GATHER_PLAN, f"gather_weights_{gi}_start")
        return gathers[gi][4][0:1, 0:1]

    def gathered(gi, after):
        send, recv, srcs, lands, _ = gathers[gi]
        return _exchange_wait(send, recv, srcs, lands, GATHER_PLAN, after, f"gather_weights_{gi}_wait")[1]

    def rows_joined(a):
        return a.reshape(-1, a.shape[-1])

    w1_b, w2_b = [None] * DEPTH, [None] * DEPTH
    gather_start(0, mods)
    pw_land, = gathered(0, ps_full)
    behind_gather_1 = gather_start(1, pw_land)
    pw_f = jnp.transpose(pw_land, (1, 2, 0, 3, 4)).reshape(pool_w.shape[0], pool_w.shape[1], pgw, pgw)

    gains = jnp.concatenate([attn_q_g, attn_k_g, jnp.zeros((6, hd), F32)], axis=0)
    ws_b = gm_ws[0].astype(BF16)
    ws_t = jnp.swapaxes(gm_ws[0], 1, 2).astype(BF16)
    bs_col = gm_bs[0][:, :, None]
    cos, sin = _rope_tables(n_ctx, seq, hd)
    lat = lambda i: mods[i, 1:2]

    hc0 = jnp.concatenate([ctx[0] + behind_gather_1, x[0]], axis=0)
    ha0 = _pool_fwd(hc0, mods[0] + behind_gather_1, pw_f, ps_full, 0, nct=nct, tm=tm, seg_lens=seg_lens)
    w1_b[0], w2_b[0] = gathered(1, ha0)
    mods0 = mods[0] + gather_start(2, w1_b[0])
    hc1, u0, o0 = _mlp_fwd(ha0, mods0, w1_b[0], w2_b[0], 0, nct=nct, tm=tm)
    wqkv_b, wo_land = gathered(2, hc1)
    mods1 = mods[1] + gather_start(3, wqkv_b)
    wo_f = rows_joined(wo_land)
    xa1 = _normmod_call(hc1, mods1, 0, "attn_in_fwd", nct=nct, tm=tm)
    qkv, q_r, k_r, v_b = _qkv_fwd(xa1, wqkv_b, cos, sin, gains, nh=nh, nkv=nkv, nct=nct, tm=tm)
    o_att, lse = _flash_fwd(q_r, k_r, v_b, n_ctx=n_ctx, hd=hd)
    ha1, y1 = _proj_fwd(o_att, wo_f, hc1, mods1, n_ctx=n_ctx, tm=tm)
    w1_b[1], w2_b[1], w1_b[2], w2_b[2], win_b, wout_land, w1_b[3], w2_b[3] = gathered(3, ha1)
    h2, u1, o1 = _mlp_fwd(ha1, lat(1), w1_b[1], w2_b[1], 1, nct=0, tm=tm)
    wout_f = rows_joined(wout_land)
    ha2, zpre, y2 = _gmlp_fwd(h2, mods[2], win_b, lng_full, lnb_full, ws_b, bs_col, wout_f, tm=tm)
    h3, u2, o2 = _mlp_fwd(ha2, lat(2), w1_b[2], w2_b[2], 2, nct=0, tm=tm)
    ha3 = _pool_fwd(h3, lat(3), pw_f, ps_full, 3, nct=0, tm=tm, seg_lens=seg_lens)
    h4, u3, o3 = _mlp_fwd(ha3, lat(3), w1_b[3], w2_b[3], 3, nct=0, tm=tm)
    dh4, fin_acc = _final_loss(h4, loss_target[0], final_g[None, :], tm=tm)

    dmods = [None] * DEPTH
    scatters = [None] * (DEPTH + 1)

    def blocked_rows(g):
        return g.reshape(N_CHIPS, g.shape[1] // N_CHIPS, g.shape[2])

    def blocked_pool(dpw):
        pg = dpw.shape[0]
        return jnp.transpose(dpw.astype(BF16).reshape(pg, N_CHIPS, pgw // N_CHIPS, pgw), (1, 0, 2, 3))

    def scatter_start(i, grads):
        lands = [lax.empty((3, *g.shape[1:]), g.dtype) for g in grads]
        scatters[i] = _exchange_start(grads, lands, SCATTER_PLAN, f"scatter_grads_{i}_start")
        return scatters[i][4][0:1, 0:1]

    def mlp_back(i, h_in, dh_out, u, o, md, n_ct):
        dh_in, du, dob, mb, dmd = _mlp_bwd(h_in, dh_out, u, o, md, w1_b[i], w2_b[i], i, nct=n_ct, tm=tm)
        dw1 = _mm_tn(mb, du, f"mlp_dw1_{i}", col_blocks=N_CHIPS)
        dw2 = blocked_rows(_mm_tn(u, dob, f"mlp_dw2_{i}", relu2=True))
        return dh_in, dmd, [dw1, dw2]

    def pool_back(i, h_in, dh_out, md, n_ct):
        dp, dmd_a, dps, dpw = _pool_bwd_weights(h_in, dh_out, md, pw_f, ps_full, i, nct=n_ct, tm=tm,
                                                seg_lens=seg_lens)
        dh_in, dmd_b = _pool_bwd_input(dp, h_in, dh_out, md, i, nct=n_ct, tm=tm, seg_lens=seg_lens, gw=pgw)
        return dh_in, dmd_a + dmd_b, dps, dpw

    zero_grp = jnp.zeros((1, 8, d), F32)
    dha3, dmd3, dws3 = mlp_back(3, ha3, dh4, u3, o3, lat(3), 0)
    dh3, dmd3p, dps3, dpw3 = pool_back(3, h3, dha3, lat(3), 0)
    dmods[3] = jnp.concatenate([zero_grp, dmd3 + dmd3p], axis=0)
    tok = scatter_start(3, dws3 + [blocked_pool(dpw3)])
    dha2, dmd2, dws2 = mlp_back(2, ha2, dh3, u2, o2, lat(2) + tok, 0)
    dh2, dzpre, gated, dyb2, ab2, dmd2g, dln, dws, dbs = _gmlp_bwd(
        h2, dha2, zpre, y2, mods[2], win_b, lng_full, lnb_full, ws_b, ws_t, bs_col, wout_f, tm=tm)
    dwin = _mm_tn(ab2, dzpre, "gmlp_dw_in", col_blocks=N_CHIPS)
    dwout = blocked_rows(_mm_tn(gated, dyb2, "gmlp_dw_out"))
    dmods[2] = jnp.concatenate([zero_grp, dmd2 + dmd2g], axis=0)
    tok = scatter_start(2, dws2 + [dwin, dwout])
    dha1, dmd1, dws1 = mlp_back(1, ha1, dh2, u1, o1, lat(1) + tok, 0)
    do_att, dyb1, dmd1p = _proj_bwd(dha1, y1, mods[1], wo_f, tm=tm)
    dwo = blocked_rows(_mm_tn(o_att, dyb1, "attn_dw_o"))
    dq, dk, dv = _flash_bwd(q_r, k_r, v_b, o_att, do_att, lse, n_ctx=n_ctx, hd=hd)
    dqkv, dgains = _qkv_bwd(qkv, dq, dk, dv, cos, sin, gains, nh=nh, nkv=nkv, nct=nct, tm=tm)
    dwqkv = _mm_tn(xa1, dqkv, "attn_dw_qkv", col_blocks=N_CHIPS)
    dhc1, dmd1i = _attn_in_bwd(dqkv, wqkv_b, hc1, dha1, mods[1], nct=nct, tm=tm)
    dmods[1] = dmd1i + jnp.concatenate([zero_grp, dmd1 + dmd1p], axis=0)
    tok = scatter_start(1, dws1 + [dwqkv, dwo])
    dha0, dmd0, dws0 = mlp_back(0, ha0, dhc1, u0, o0, mods[0] + tok, nct)
    tok = scatter_start(0, dws0)
    dhc0, dmd0p, dps0, dpw0 = pool_back(0, hc0, dha0, mods[0] + tok, nct)
    dmods[0] = dmd0 + dmd0p
    grad_x = dhc0[None]
    scatter_start(DEPTH, [blocked_pool(dpw0)])

    dmods_all = jnp.stack(dmods, axis=0)
    small_grads = [dmods_all, dws, dbs, dgains, dln, dps0, dps3, fin_acc]
    sg_shapes = [a.shape for a in small_grads]
    sg_gath = _all_gather_small(_pack(small_grads), "gather_small_grads")
    sg_sum = _sum_devices(sg_gath, "sum_small_grads")
    s_dmods, s_dws, s_dbs, s_dgains, s_dln, s_dps0, s_dps3, s_fin = _unpack(sg_sum, sg_shapes)
    loss = s_fin[1, 0]

    sources, landed = [None] * len(scatters), [None] * len(scatters)
    for i in (3, 2, 1, 0, DEPTH):
        send, recv, srcs, lands, _ = scatters[i]
        sources[i], landed[i] = _exchange_wait(send, recv, srcs, lands, SCATTER_PLAN, sg_sum, f"scatter_grads_{i}_wait")

    def summed(name, picks):
        return _sum_partials([sources[i][j] for i, j in picks], [landed[i][j] for i, j in picks], chip,
                             f"sum_chips_{name}")

    big = [("mlp_w1", mlp_w1, m_mlp_w1, v_mlp_w1, [(i, 0) for i in range(DEPTH)]),
           ("mlp_w2", mlp_w2, m_mlp_w2, v_mlp_w2, [(i, 1) for i in range(DEPTH)]),
           ("pool_w", pool_w, m_pool_w, v_pool_w, [(DEPTH, 0), (3, 2)]),
           ("attn_w_qkv", attn_w_qkv, m_attn_w_qkv, v_attn_w_qkv, [(1, 2)]),
           ("attn_w_o", attn_w_o, m_attn_w_o, v_attn_w_o, [(1, 3)]),
           ("gm_w_in", gm_w_in, m_gm_w_in, v_gm_w_in, [(2, 2)]),
           ("gm_w_out", gm_w_out, m_gm_w_out, v_gm_w_out, [(2, 3)])]
    partial = [summed(name, picks) for name, _, _, _, picks in big]
    swap = _exchange_start(partial, [lax.empty(p.shape, p.dtype) for p in partial], SIBLING_PLAN,
                           "swap_with_sibling_start")
    behind_swap = swap[4][0:1, 0:1]

    dm_dev = _unpack(sg_gath, sg_shapes[:1])[0]
    dm_lat = jnp.moveaxis(dm_dev[:, :, 1, :6, :], 0, 1).reshape(DEPTH, N_DEV, 6 * d)
    dm_ctx = jnp.moveaxis(dm_dev[:, :, 0, :6, :], 0, 1).reshape(DEPTH, N_DEV, 6 * d)
    dmod16 = _chip_cols(jnp.concatenate([dm_lat, dm_ctx], axis=1), chip, ncs) + behind_swap
    g_ada_w, dcc_part = _ada_bwd(c_all, c_all.T, dmod16, ada_w)
    dcc_gath = _all_gather_small(dcc_part, "gather_d_c_ctx")
    dcc_chips = _across_chips(dcc_gath, dcc_gath.shape[1:])
    dcc_rows = _sum_devices(dcc_chips, "sum_d_c_ctx")
    dcc = dcc_rows[0]
    ada_res = _adamw(g_ada_w.reshape(-1, ncs), None, ada_w.reshape(-1, ncs),
                     m_ada_w.reshape(-1, ncs), v_ada_w.reshape(-1, ncs), "adamw_ada_w")

    partial, from_sibling = _exchange_wait(swap[0], swap[1], swap[2], swap[3], SIBLING_PLAN, ada_res[1],
                                           "swap_with_sibling_wait")
    big_out = {}
    for (name, w, m, v, _), mine, theirs in zip(big, partial, from_sibling):
        cols = w.shape[-1]
        res = _adamw(mine, theirs, w.reshape(-1, cols), m.reshape(-1, cols), v.reshape(-1, cols), f"adamw_{name}")
        big_out[name] = [r.reshape(w.shape) for r in res]
    big_out["ada_w"] = [r.reshape(ada_w.shape) for r in ada_res]

    def cols_of(a, width):
        return _chip_cols(a, chip, width)

    zero = lambda a: jnp.zeros(a.shape, F32)
    ngw = norm_g.shape[-1]
    small = {
        "c_ctx": (dcc, zero(dcc), c_ctx, m_c_ctx, v_c_ctx),
        "ada_b": (s_dmods[:, 0, :6].reshape(DEPTH, 6 * d), s_dmods[:, 1, :6].reshape(DEPTH, 6 * d), ada_b, m_ada_b,
                  v_ada_b),
        "norm_g": (cols_of(s_dmods[:, 0, 6:8], ngw), cols_of(s_dmods[:, 1, 6:8], ngw), norm_g, m_norm_g, v_norm_g),
        "pool_scale": (cols_of(jnp.stack([s_dps0[0], s_dps3[0]]), pool_scale.shape[-1]), zero(pool_scale),
                       pool_scale, m_pool_scale, v_pool_scale),
        "attn_q_g": (s_dgains[0:1], zero(attn_q_g), attn_q_g, m_attn_q_g, v_attn_q_g),
        "attn_k_g": (s_dgains[1:2], zero(attn_k_g), attn_k_g, m_attn_k_g, v_attn_k_g),
        "gm_ln_g": (cols_of(s_dln[0:1], gm_ln_g.shape[-1]), zero(gm_ln_g), gm_ln_g, m_gm_ln_g, v_gm_ln_g),
        "gm_ln_b": (cols_of(s_dln[1:2], gm_ln_b.shape[-1]), zero(gm_ln_b), gm_ln_b, m_gm_ln_b, v_gm_ln_b),
        "gm_ws": (s_dws[None], zero(gm_ws), gm_ws, m_gm_ws, v_gm_ws),
        "gm_bs": (s_dbs[None, :, :, 0], zero(gm_bs), gm_bs, m_gm_bs, v_gm_bs),
        "final_g": (s_fin[0], zero(final_g), final_g, m_final_g, v_final_g),
    }
    keys = list(small)
    packed = [_pack([small[k][t] for k in keys]) for t in range(5)]
    res = _adamw(*packed, "adamw_small")
    shapes = [small[k][2].shape for k in keys]
    small_out = {k: [] for k in keys}
    for r in res:
        for k, a in zip(keys, _unpack(r, shapes)):
            small_out[k].append(a)

    order = ["c_ctx", "ada_w", "ada_b", "norm_g", "mlp_w1", "mlp_w2", "pool_w", "pool_scale", "attn_w_qkv",
             "attn_w_o", "attn_q_g", "attn_k_g", "gm_w_in", "gm_ln_g", "gm_ln_b", "gm_ws", "gm_bs", "gm_w_out",
             "final_g"]
    allo = {**big_out, **small_out}
    outs = [loss, grad_x]
    for t in range(4):
        outs += [allo[k][t] for k in order]
    return tuple(outs)
```

```python
import functools
import math

import numpy as np
import jax
import jax.numpy as jnp
from jax import lax
from jax.experimental import pallas as pl
from jax.experimental.pallas import tpu as pltpu

F32 = jnp.float32
BF16 = jnp.bfloat16
MESH = pl.DeviceIdType.MESH

EPS = 1e-6
GRID_W = 64
ROPE_BASE = 10000.0
POOL_WINDOWS = (2, 4, 8, 16)
HALO = 8
DEPTH = 4
N_MIXERS = 3

ADAM_LR = 0.001
ADAM_B1 = 0.9
ADAM_B2 = 0.999
ADAM_EPS = 1e-08
ADAM_WD = 0.01
ADAM_STEP = 10

VMEM_LIMIT_BYTES = 56 * 1024 * 1024
LANES = 128
SUBLANES = 8
N_DEV = 8
N_CHIPS = 4

SH1, SC1, G1, SH2, SC2, G2, NG0, NG1 = range(8)


def _dot(a, b):
    return jnp.dot(a, b, preferred_element_type=F32)


def _dot_nt(a, b):
    return lax.dot_general(a, b, (((1,), (1,)), ((), ())), preferred_element_type=F32)


def _dot_tn(a, b):
    return lax.dot_general(a, b, (((0,), (0,)), ((), ())), preferred_element_type=F32)


def _dot_blocks(a, w_ref):
    return jnp.concatenate([_dot(a, w_ref[k]) for k in range(w_ref.shape[0])], axis=1)


def _dot_nt_blocks(a, w_ref):
    nb, _, w = w_ref.shape
    acc = _dot_nt(a[:, 0:w], w_ref[0])
    for k in range(1, nb):
        acc = acc + _dot_nt(a[:, k * w:(k + 1) * w], w_ref[k])
    return acc


def _params(**kw):
    return pltpu.CompilerParams(vmem_limit_bytes=VMEM_LIMIT_BYTES, **kw)


def _full(shape):
    nd = len(shape)
    return pl.BlockSpec(shape, lambda *_: (0,) * nd)


def _rows(tm, width):
    return pl.BlockSpec((tm, width), lambda i: (i, 0))


def _group_of(nct, groups):
    if groups == 1:
        return lambda i: 0
    return lambda i: jnp.where(i >= nct, 1, 0)


def _mods_spec(nct, groups, d):
    grp = _group_of(nct, groups)
    return pl.BlockSpec((None, 8, d), lambda i: (grp(i), 0, 0))


def _first_of_group(i, nct, groups):
    if groups == 1:
        return i == 0
    return jnp.logical_or(i == 0, i == nct)


def _rowsum(v):
    return jnp.sum(v, axis=0, keepdims=True)


def _rms_parts(x):
    r = lax.rsqrt(jnp.mean(x * x, axis=-1, keepdims=True) + EPS)
    return x * r, r


def _normmod(x, md, which):
    ng, sh, sc = (md[NG0:NG0 + 1], md[SH1:SH1 + 1], md[SC1:SC1 + 1]) if which == 0 else (
        md[NG1:NG1 + 1], md[SH2:SH2 + 1], md[SC2:SC2 + 1])
    xhat, r = _rms_parts(x)
    n = xhat * ng
    return n * (1.0 + sc) + sh, (xhat, r, n)


def _normmod_bwd(da, parts, md, which):
    xhat, r, n = parts
    ng, sc = (md[NG0:NG0 + 1], md[SC1:SC1 + 1]) if which == 0 else (md[NG1:NG1 + 1], md[SC2:SC2 + 1])
    dsh = _rowsum(da)
    dsc = _rowsum(da * n)
    dn = da * (1.0 + sc)
    dng = _rowsum(dn * xhat)
    dxhat = dn * ng
    dx = r * (dxhat - xhat * jnp.mean(dxhat * xhat, axis=-1, keepdims=True))
    return dx, dsh, dsc, dng


def _acc_rows(ref, first, rows):
    @pl.when(first)
    def _():
        ref[...] = jnp.zeros(ref.shape, ref.dtype)

    for r, v in rows.items():
        ref[r:r + 1, :] += v


def _shift_up(x, k):
    if k == 0:
        return x
    return pltpu.roll(x, x.shape[0] - k, axis=0)


GELU_K = math.sqrt(2.0 / math.pi)
GELU_C = 0.044715


def _gelu_and_grad(x):
    x2 = x * x
    t = jnp.tanh(x * (GELU_K + (GELU_K * GELU_C) * x2))
    half_x = 0.5 * x
    one_t = 1.0 + t
    grad = 0.5 * one_t + half_x * (1.0 - t * t) * (GELU_K + (3.0 * GELU_K * GELU_C) * x2)
    return half_x * one_t, grad


def _gelu(x):
    return 0.5 * x * (1.0 + jnp.tanh(x * (GELU_K + (GELU_K * GELU_C) * (x * x))))


def _silu(x):
    return x / (1.0 + jnp.exp(-x))


def _silu_grad(x):
    s = 1.0 / (1.0 + jnp.exp(-x))
    return s * (1.0 + x * (1.0 - s))


def _mlp_fwd(h, mods, w1, w2, layer, *, nct, tm):
    rows, d = h.shape
    groups = mods.shape[0]
    nb, _, fc = w1.shape
    ff = nb * fc

    def body(h_ref, md_ref, w1_ref, w2_ref, h2_ref, u_ref, o_ref):
        x = h_ref[...]
        md = md_ref[...]
        m, _ = _normmod(x, md, 1)
        mb = m.astype(BF16)
        acc = jnp.zeros((tm, d), F32)
        for k in range(nb):
            u = _dot(mb, w1_ref[k])
            u_ref[:, k * fc:(k + 1) * fc] = u.astype(BF16)
            acc = acc + _dot(jnp.square(jnp.maximum(u, 0.0)).astype(BF16), w2_ref[k])
        o_ref[...] = acc.astype(BF16)
        h2_ref[...] = x + md[G2:G2 + 1] * acc

    return pl.pallas_call(
        body, name=f"mlp_fwd_{layer}", grid=(rows // tm,),
        in_specs=[_rows(tm, d), _mods_spec(nct, groups, d), _full(w1.shape), _full(w2.shape)],
        out_specs=[_rows(tm, d), _rows(tm, ff), _rows(tm, d)],
        out_shape=[jax.ShapeDtypeStruct((rows, d), F32), jax.ShapeDtypeStruct((rows, ff), BF16),
                   jax.ShapeDtypeStruct((rows, d), BF16)],
        compiler_params=_params(),
    )(h, mods, w1, w2)


def _mlp_bwd(h1, dh2, u, o, mods, w1, w2, layer, *, nct, tm):
    rows, d = h1.shape
    groups = mods.shape[0]
    nb, _, fc = w1.shape
    ff = nb * fc

    def body(h_ref, g_ref, u_ref, o_ref, md_ref, w1_ref, w2_ref, dh_ref, du_ref, dob_ref, mb_ref, dmd_ref):
        i = pl.program_id(0)
        x = h_ref[...]
        g = g_ref[...]
        md = md_ref[...]
        m, parts = _normmod(x, md, 1)
        mb_ref[...] = m.astype(BF16)
        dg2 = _rowsum(g * o_ref[...].astype(F32))
        dob = (g * md[G2:G2 + 1]).astype(BF16)
        dob_ref[...] = dob
        dm = jnp.zeros((tm, d), F32)
        for k in range(nb):
            uk = u_ref[:, k * fc:(k + 1) * fc].astype(F32)
            dr = _dot_nt(dob, w2_ref[k])
            duk = (dr * (2.0 * jnp.maximum(uk, 0.0))).astype(BF16)
            du_ref[:, k * fc:(k + 1) * fc] = duk
            dm = dm + _dot_nt(duk, w1_ref[k])
        dx, dsh, dsc, dng = _normmod_bwd(dm, parts, md, 1)
        dh_ref[...] = g + dx
        _acc_rows(dmd_ref, _first_of_group(i, nct, groups), {SH2: dsh, SC2: dsc, G2: dg2, NG1: dng})

    return pl.pallas_call(
        body, name=f"mlp_bwd_{layer}", grid=(rows // tm,),
        in_specs=[_rows(tm, d), _rows(tm, d), _rows(tm, ff), _rows(tm, d), _mods_spec(nct, groups, d),
                  _full(w1.shape), _full(w2.shape)],
        out_specs=[_rows(tm, d), _rows(tm, ff), _rows(tm, d), _rows(tm, d), _mods_spec(nct, groups, d)],
        out_shape=[jax.ShapeDtypeStruct((rows, d), F32), jax.ShapeDtypeStruct((rows, ff), BF16),
                   jax.ShapeDtypeStruct((rows, d), BF16), jax.ShapeDtypeStruct((rows, d), BF16),
                   jax.ShapeDtypeStruct((groups, 8, d), F32)],
        compiler_params=_params(),
    )(h1, dh2, u, o, mods, w1, w2)


def _div_tile(n, cap):
    if n <= cap:
        return n
    return max(t for t in range(LANES, cap + 1, LANES) if n % t == 0)


DW_TOKEN_TILE_CAP = 4224


def _mm_tn(a, b, name, *, relu2=False, col_blocks=1):
    rows, m = a.shape
    n = b.shape[1]
    tmm = min(m, 1024)
    tn = min(n // col_blocks, 2048)
    per_block = n // col_blocks // tn
    tr = _div_tile(rows, DW_TOKEN_TILE_CAP)

    def body(a_ref, b_ref, o_ref, acc_ref):
        r = pl.program_id(2)

        @pl.when(r == 0)
        def _():
            acc_ref[...] = jnp.zeros(acc_ref.shape, F32)

        av = a_ref[...]
        if relu2:
            av = jnp.square(jnp.maximum(av.astype(F32), 0.0)).astype(BF16)
        acc_ref[...] += _dot_tn(av, b_ref[...])

        @pl.when(r == pl.num_programs(2) - 1)
        def _():
            o_ref[...] = acc_ref[...].astype(BF16)

    return pl.pallas_call(
        body, name=name, grid=(m // tmm, n // tn, rows // tr),
        in_specs=[pl.BlockSpec((tr, tmm), lambda i, j, r: (r, i)), pl.BlockSpec((tr, tn), lambda i, j, r: (r, j))],
        out_specs=pl.BlockSpec((None, tmm, tn), lambda i, j, r: (j // per_block, i, j % per_block)),
        out_shape=jax.ShapeDtypeStruct((col_blocks, m, n // col_blocks), BF16),
        scratch_shapes=[pltpu.VMEM((tmm, tn), F32)],
        compiler_params=_params(),
    )(a, b)


def _halo_specs(tm, d, rows):
    per = tm // HALO
    prev = pl.BlockSpec((HALO, d), lambda i: (jnp.maximum(i * per - 1, 0), 0))
    nxt = pl.BlockSpec((HALO, d), lambda i: (jnp.minimum((i + 1) * per, rows // HALO - 1), 0))
    return prev, _rows(tm, d), nxt


def _segment_positions(i, tm, nct, groups, seg_lens):
    if groups == 1:
        start, length = 0, seg_lens[-1]
    else:
        start = jnp.where(i >= nct, nct, 0)
        length = jnp.where(i >= nct, seg_lens[1], seg_lens[0])
    rid = lax.broadcasted_iota(jnp.int32, (tm + 2 * HALO, 1), 0)
    pos = (i - start) * tm - HALO + rid
    return pos, length


def _window_count(pos, length, w):
    hi = jnp.minimum(pos + (w - w // 2), length)
    lo = jnp.maximum(pos - w // 2, 0)
    return (hi - lo).astype(F32)


def _window_sum(xg, w, lead):
    b, k = xg, 1
    while k < w:
        b = b + _shift_up(b, k)
        k *= 2
    return _shift_up(b, HALO - lead)[0:xg.shape[0] - 2 * HALO]


def _pooled(ext, md, pos, length, gw):
    tm = ext.shape[0] - 2 * HALO
    a_ext, parts = _normmod(ext, md, 0)
    valid = jnp.logical_and(pos >= 0, pos < length)
    a_ext = jnp.where(valid, a_ext, 0.0)
    pos_c = pos[HALO:HALO + tm]
    ps = []
    for g, w in enumerate(POOL_WINDOWS):
        xg = a_ext[:, g * gw:(g + 1) * gw]
        s = _window_sum(xg, w, w // 2)
        ps.append(s * (1.0 / _window_count(pos_c, length, w)) - xg[HALO:HALO + tm])
    return ps, parts


def _pool_fwd(h, mods, pw, pscale, layer, *, nct, tm, seg_lens):
    rows, d = h.shape
    groups = mods.shape[0]
    pg, gw = pw.shape[1], pw.shape[-1]

    def body(prev_ref, cur_ref, next_ref, md_ref, pw_ref, ps_ref, out_ref):
        i = pl.program_id(0)
        md = md_ref[...]
        cur = cur_ref[...]
        ext = jnp.concatenate([prev_ref[...], cur, next_ref[...]], axis=0)
        pos, length = _segment_positions(i, tm, nct, groups, seg_lens)
        ps, _ = _pooled(ext, md, pos, length, gw)
        for g in range(pg):
            yg = _dot(ps[g].astype(BF16), pw_ref[g]) * ps_ref[:, g * gw:(g + 1) * gw]
            out_ref[:, g * gw:(g + 1) * gw] = cur[:, g * gw:(g + 1) * gw] + md[G1:G1 + 1, g * gw:(g + 1) * gw] * yg

    j = layer // N_MIXERS
    return pl.pallas_call(
        body, name=f"pool_fwd_{layer}", grid=(rows // tm,),
        in_specs=[*_halo_specs(tm, d, rows), _mods_spec(nct, groups, d),
                  pl.BlockSpec((None, pg, gw, gw), lambda i: (j, 0, 0, 0)), _full((1, d))],
        out_specs=_rows(tm, d),
        out_shape=jax.ShapeDtypeStruct((rows, d), F32),
        compiler_params=_params(),
    )(h, h, h, mods, pw, pscale[j:j + 1])


def _pool_bwd_weights(h, dh1, mods, pw, pscale, layer, *, nct, tm, seg_lens):
    rows, d = h.shape
    groups = mods.shape[0]
    pg, gw = pw.shape[1], pw.shape[-1]

    def body(prev_ref, cur_ref, next_ref, g_ref, md_ref, pw_ref, ps_ref, dp_ref, dmd_ref, dps_ref, dpw_ref):
        i = pl.program_id(0)
        md = md_ref[...]
        ext = jnp.concatenate([prev_ref[...], cur_ref[...], next_ref[...]], axis=0)
        pos, length = _segment_positions(i, tm, nct, groups, seg_lens)
        ps, _ = _pooled(ext, md, pos, length, gw)
        gup = g_ref[...]

        @pl.when(i == 0)
        def _():
            dps_ref[...] = jnp.zeros(dps_ref.shape, F32)
            dpw_ref[...] = jnp.zeros(dpw_ref.shape, F32)

        dg1 = []
        for g in range(pg):
            cols = slice(g * gw, (g + 1) * gw)
            pb = ps[g].astype(BF16)
            yp = _dot(pb, pw_ref[g])
            sc = ps_ref[:, cols]
            dg1.append(_rowsum(gup[:, cols] * (yp * sc)))
            dy = gup[:, cols] * md[G1:G1 + 1, cols]
            dps_ref[0:1, cols] += _rowsum(dy * yp)
            dyp = (dy * sc).astype(BF16)
            dp_ref[:, cols] = _dot_nt(dyp, pw_ref[g])
            dpw_ref[g] += _dot_tn(pb, dyp)
        _acc_rows(dmd_ref, _first_of_group(i, nct, groups), {G1: jnp.concatenate(dg1, axis=1)})

    j = layer // N_MIXERS
    return pl.pallas_call(
        body, name=f"pool_bwd_w_{layer}", grid=(rows // tm,),
        in_specs=[*_halo_specs(tm, d, rows), _rows(tm, d), _mods_spec(nct, groups, d),
                  pl.BlockSpec((None, pg, gw, gw), lambda i: (j, 0, 0, 0)), _full((1, d))],
        out_specs=[_rows(tm, d), _mods_spec(nct, groups, d), _full((8, d)), _full((pg, gw, gw))],
        out_shape=[jax.ShapeDtypeStruct((rows, d), F32), jax.ShapeDtypeStruct((groups, 8, d), F32),
                   jax.ShapeDtypeStruct((8, d), F32), jax.ShapeDtypeStruct((pg, gw, gw), F32)],
        compiler_params=_params(),
    )(h, h, h, dh1, mods, pw, pscale[j:j + 1])


def _pool_bwd_input(dp, h, dh1, mods, layer, *, nct, tm, seg_lens, gw):
    rows, d = h.shape
    groups = mods.shape[0]

    def body(prev_ref, cur_ref, next_ref, h_ref, g_ref, md_ref, dh_ref, dmd_ref):
        i = pl.program_id(0)
        md = md_ref[...]
        dp_cur = cur_ref[...]
        ext = jnp.concatenate([prev_ref[...], dp_cur, next_ref[...]], axis=0)
        pos, length = _segment_positions(i, tm, nct, groups, seg_lens)
        valid = jnp.logical_and(pos >= 0, pos < length)
        das = []
        for g, w in enumerate(POOL_WINDOWS):
            cols = slice(g * gw, (g + 1) * gw)
            q = jnp.where(valid, ext[:, cols] * (1.0 / jnp.maximum(_window_count(pos, length, w), 1.0)), 0.0)
            das.append(_window_sum(q, w, w // 2 - 1) - dp_cur[:, cols])
        da = jnp.concatenate(das, axis=1)
        _, parts = _normmod(h_ref[...], md, 0)
        dx, dsh, dsc, dng = _normmod_bwd(da, parts, md, 0)
        dh_ref[...] = g_ref[...] + dx
        _acc_rows(dmd_ref, _first_of_group(i, nct, groups), {SH1: dsh, SC1: dsc, NG0: dng})

    return pl.pallas_call(
        body, name=f"pool_bwd_x_{layer}", grid=(rows // tm,),
        in_specs=[*_halo_specs(tm, d, rows), _rows(tm, d), _rows(tm, d), _mods_spec(nct, groups, d)],
        out_specs=[pl.BlockSpec((tm, d), lambda i: (jnp.maximum(i - nct, 0), 0)), _mods_spec(nct, groups, d)],
        out_shape=[jax.ShapeDtypeStruct((rows - nct * tm, d), F32), jax.ShapeDtypeStruct((groups, 8, d), F32)],
        compiler_params=_params(),
    )(dp, dp, dp, h, dh1, mods)


def _rope_tables(n_ctx, seq, hd):
    half = hd // 2
    n_rows = seq // GRID_W
    inv = np.float32(ROPE_BASE) ** (-np.arange(0, half, 2, dtype=np.float32) / np.float32(half))
    ar = np.arange(n_rows, dtype=np.float32)[:, None] * inv[None, :]
    ac = np.arange(GRID_W, dtype=np.float32)[:, None] * inv[None, :]

    def over_tokens(row_part, col_part):
        r = jnp.repeat(jnp.asarray(row_part, F32), GRID_W, axis=0)
        c = jnp.tile(jnp.asarray(col_part, F32), (n_rows, 1))
        return r, c

    cr, cc = over_tokens(np.cos(ar), np.cos(ac))
    sr, sc = over_tokens(np.sin(ar), np.sin(ac))
    cos = jnp.concatenate([cr, cr, cc, cc], axis=1)
    sin = jnp.concatenate([-sr, sr, -sc, sc], axis=1)
    cos = jnp.concatenate([jnp.ones((n_ctx, hd), F32), cos], axis=0)
    sin = jnp.concatenate([jnp.zeros((n_ctx, hd), F32), sin], axis=0)
    return cos, sin


def _rope_partner(x):
    hd = x.shape[-1]
    q = hd // 4
    lane = lax.broadcasted_iota(jnp.int32, x.shape, 1)
    first = (lane % (2 * q)) < q
    return jnp.where(first, pltpu.roll(x, hd - q, axis=1), pltpu.roll(x, q, axis=1))


def _qkv_fwd(h, mods, wqkv, cos, sin, gains, *, nh, nkv, nct, tm):
    rows, d = h.shape
    qw = wqkv.shape[0] * wqkv.shape[-1]
    hd = cos.shape[-1]

    def body(h_ref, md_ref, w_ref, cos_ref, sin_ref, gn_ref, xa_ref, qkv_ref, q_ref, k_ref, v_ref):
        a, _ = _normmod(h_ref[...], md_ref[...], 0)
        xa = a.astype(BF16)
        xa_ref[...] = xa
        qkv = _dot_blocks(xa, w_ref)
        qkv_ref[...] = qkv
        c, s = cos_ref[...], sin_ref[...]
        for hh in range(nh + nkv):
            xh = qkv[:, hh * hd:(hh + 1) * hd]
            xhat, _ = _rms_parts(xh)
            y = xhat * (gn_ref[0:1, :] if hh < nh else gn_ref[1:2, :])
            rot = (y * c + _rope_partner(y) * s).astype(BF16)
            if hh < nh:
                q_ref[:, hh * hd:(hh + 1) * hd] = rot
            else:
                k_ref[:, (hh - nh) * hd:(hh - nh + 1) * hd] = rot
        v_ref[...] = qkv[:, (nh + nkv) * hd:].astype(BF16)

    return pl.pallas_call(
        body, name="attn_qkv_fwd", grid=(rows // tm,),
        in_specs=[_rows(tm, d), _mods_spec(nct, 2, d), _full(wqkv.shape), _rows(tm, hd), _rows(tm, hd),
                  _full((8, hd))],
        out_specs=[_rows(tm, d), _rows(tm, qw), pl.BlockSpec((tm, nh * hd), lambda i: (jnp.maximum(i - nct, 0), 0)),
                   _rows(tm, nkv * hd), _rows(tm, nkv * hd)],
        out_shape=[jax.ShapeDtypeStruct((rows, d), BF16), jax.ShapeDtypeStruct((rows, qw), F32),
                   jax.ShapeDtypeStruct((rows - nct * tm, nh * hd), BF16),
                   jax.ShapeDtypeStruct((rows, nkv * hd), BF16), jax.ShapeDtypeStruct((rows, nkv * hd), BF16)],
        compiler_params=_params(),
    )(h, mods, wqkv, cos, sin, gains)


ATTN_Q_TILE_CAP = 1024
ATTN_KV_TILE_CAP = 4224
ATTN_ROW_GROUP = 256
LOG2E = 1.4426950408889634


def _attn_tiles(seq, total):
    tq = _div_tile(seq, ATTN_Q_TILE_CAP)
    return tq, _div_tile(total, ATTN_KV_TILE_CAP), min(ATTN_ROW_GROUP, tq)


def _flash_fwd(q, k, v, *, n_ctx, hd):
    total = k.shape[0]
    seq = total - n_ctx
    nkv = k.shape[1] // hd
    tq, tk, rg = _attn_tiles(seq, total)
    nk = total // tk
    scale = hd ** -0.5
    c2 = scale * LOG2E

    def body(q_ref, k_ref, v_ref, o_ref, lse_ref, m_sc, l_sc, acc_sc):
        ki = pl.program_id(2)

        @pl.when(ki == 0)
        def _():
            m_sc[...] = jnp.full(m_sc.shape, -jnp.inf, F32)
            l_sc[...] = jnp.zeros(l_sc.shape, F32)
            acc_sc[...] = jnp.zeros(acc_sc.shape, F32)

        kk, vv = k_ref[...], v_ref[...]
        groups = [(g, sub) for g in range(2) for sub in range(tq // rg)]

        def scores(g, sub):
            return _dot_nt(q_ref[sub * rg:(sub + 1) * rg, g * hd:(g + 1) * hd], kk)

        s_next = scores(*groups[0])
        for idx, (g, sub) in enumerate(groups):
            s = s_next
            if idx + 1 < len(groups):
                s_next = scores(*groups[idx + 1])
            rows = slice(g * tq + sub * rg, g * tq + (sub + 1) * rg)
            m_old = m_sc[rows]
            m_new = jnp.maximum(m_old, jnp.max(s, axis=-1, keepdims=True))
            alpha = jnp.exp2((m_old - m_new) * c2)
            p = jnp.exp2((s - m_new) * c2)
            l_sc[rows] = alpha * l_sc[rows] + jnp.sum(p, axis=-1, keepdims=True)
            acc_sc[rows] = alpha * acc_sc[rows] + _dot(p.astype(BF16), vv)
            m_sc[rows] = m_new

        @pl.when(ki == nk - 1)
        def _():
            o2 = acc_sc[...] / l_sc[...]
            lse = m_sc[...] * scale + jnp.log(l_sc[...])
            o_ref[:, :hd] = o2[:tq].astype(BF16)
            o_ref[:, hd:] = o2[tq:].astype(BF16)
            lse_ref[:, 0:1] = lse[:tq]
            lse_ref[:, 1:2] = lse[tq:]

    return pl.pallas_call(
        body, name="attn_flash_fwd", grid=(nkv, seq // tq, nk),
        in_specs=[pl.BlockSpec((tq, 2 * hd), lambda h, i, j: (i, h)),
                  pl.BlockSpec((tk, hd), lambda h, i, j: (j, h)),
                  pl.BlockSpec((tk, hd), lambda h, i, j: (j, h))],
        out_specs=[pl.BlockSpec((tq, 2 * hd), lambda h, i, j: (i, h)),
                   pl.BlockSpec((None, tq, 2), lambda h, i, j: (h, i, 0))],
        out_shape=[jax.ShapeDtypeStruct((seq, 2 * nkv * hd), BF16), jax.ShapeDtypeStruct((nkv, seq, 2), F32)],
        scratch_shapes=[pltpu.VMEM((2 * tq, 1), F32), pltpu.VMEM((2 * tq, 1), F32), pltpu.VMEM((2 * tq, hd), F32)],
        compiler_params=_params(),
    )(q, k, v)


def _flash_bwd(q, k, v, o, do, lse, *, n_ctx, hd):
    total = k.shape[0]
    seq = total - n_ctx
    nkv = k.shape[1] // hd
    tq, tk, rg = _attn_tiles(seq, total)
    scale = hd ** -0.5
    c2 = scale * LOG2E

    def body(q_ref, k_ref, v_ref, o_ref, do_ref, lse_ref, dq_ref, dk_ref, dv_ref):
        ki, qi = pl.program_id(1), pl.program_id(2)
        kk, vv = k_ref[...], v_ref[...]

        @pl.when(qi == 0)
        def _():
            dk_ref[...] = jnp.zeros(dk_ref.shape, F32)
            dv_ref[...] = jnp.zeros(dv_ref.shape, F32)

        dk_acc = jnp.zeros((tk, hd), F32)
        dv_acc = jnp.zeros((tk, hd), F32)
        for g in range(2):
            for sub in range(tq // rg):
                rs = slice(sub * rg, (sub + 1) * rg)
                cs = slice(g * hd, (g + 1) * hd)
                qq = q_ref[rs, cs]
                dd = do_ref[rs, cs]
                delta = jnp.sum(dd.astype(F32) * o_ref[rs, cs].astype(F32), axis=-1, keepdims=True)
                p = jnp.exp2(_dot_nt(qq, kk) * c2 - lse_ref[rs, g:g + 1] * LOG2E)
                dp = _dot_nt(dd, vv)
                ds = (p * (dp - delta)).astype(BF16)
                dv_acc = dv_acc + _dot_tn(p.astype(BF16), dd)
                dk_acc = dk_acc + _dot_tn(ds, qq)
                dq = _dot(ds, kk) * scale
                rows = pl.ds(pl.multiple_of(qi * tq, tq) + sub * rg, rg)

                @pl.when(ki == 0)
                def _():
                    dq_ref[rows, cs] = dq

                @pl.when(ki > 0)
                def _():
                    dq_ref[rows, cs] += dq
        dk_ref[...] += dk_acc * scale
        dv_ref[...] += dv_acc

    return pl.pallas_call(
        body, name="attn_flash_bwd", grid=(nkv, total // tk, seq // tq),
        in_specs=[pl.BlockSpec((tq, 2 * hd), lambda h, j, i: (i, h)),
                  pl.BlockSpec((tk, hd), lambda h, j, i: (j, h)),
                  pl.BlockSpec((tk, hd), lambda h, j, i: (j, h)),
                  pl.BlockSpec((tq, 2 * hd), lambda h, j, i: (i, h)),
                  pl.BlockSpec((tq, 2 * hd), lambda h, j, i: (i, h)),
                  pl.BlockSpec((None, tq, 2), lambda h, j, i: (h, i, 0))],
        out_specs=[pl.BlockSpec((seq, 2 * hd), lambda h, j, i: (0, h)),
                   pl.BlockSpec((tk, hd), lambda h, j, i: (j, h)),
                   pl.BlockSpec((tk, hd), lambda h, j, i: (j, h))],
        out_shape=[jax.ShapeDtypeStruct((seq, 2 * nkv * hd), F32), jax.ShapeDtypeStruct((total, nkv * hd), F32),
                   jax.ShapeDtypeStruct((total, nkv * hd), F32)],
        compiler_params=_params(),
    )(q, k, v, o, do, lse)


def _proj_fwd(o, wo, hc, mods, *, n_ctx, tm):
    seq, d = o.shape
    off = n_ctx // tm

    def body(o_ref, w_ref, h_ref, md_ref, h1_ref, y_ref):
        y = _dot(o_ref[...], w_ref[...])
        y_ref[...] = y.astype(BF16)
        h1_ref[...] = h_ref[...] + md_ref[G1:G1 + 1, :] * y

    return pl.pallas_call(
        body, name="attn_proj_fwd", grid=(seq // tm,),
        in_specs=[_rows(tm, d), _full((d, d)),
                  pl.BlockSpec((tm, d), lambda i: (i + off, 0)), pl.BlockSpec((None, 8, d), lambda i: (1, 0, 0))],
        out_specs=[_rows(tm, d), _rows(tm, d)],
        out_shape=[jax.ShapeDtypeStruct((seq, d), F32), jax.ShapeDtypeStruct((seq, d), BF16)],
        compiler_params=_params(),
    )(o, wo, hc, mods)


def _proj_bwd(dh1, y, mods, wo, *, tm):
    seq, d = dh1.shape

    def body(g_ref, y_ref, md_ref, w_ref, do_ref, dyb_ref, dmd_ref):
        i = pl.program_id(0)
        g = g_ref[...]
        dyb = (g * md_ref[G1:G1 + 1, :]).astype(BF16)
        dyb_ref[...] = dyb
        do_ref[...] = _dot_nt(dyb, w_ref[...]).astype(BF16)
        _acc_rows(dmd_ref, i == 0, {G1: _rowsum(g * y_ref[...].astype(F32))})

    return pl.pallas_call(
        body, name="attn_proj_bwd", grid=(seq // tm,),
        in_specs=[_rows(tm, d), _rows(tm, d), pl.BlockSpec((None, 8, d), lambda i: (1, 0, 0)), _full((d, d))],
        out_specs=[_rows(tm, d), _rows(tm, d), pl.BlockSpec((None, 8, d), lambda i: (0, 0, 0))],
        out_shape=[jax.ShapeDtypeStruct((seq, d), BF16), jax.ShapeDtypeStruct((seq, d), BF16),
                   jax.ShapeDtypeStruct((1, 8, d), F32)],
        compiler_params=_params(),
    )(dh1, y, mods, wo)


def _qkv_bwd(qkv, dq, dk, dv, cos, sin, gains, *, nh, nkv, nct, tm):
    rows, qw = qkv.shape
    hd = cos.shape[-1]

    def body(qkv_ref, dq_ref, dk_ref, dv_ref, cos_ref, sin_ref, gn_ref, out_ref, dgn_ref):
        i = pl.program_id(0)
        c, s = cos_ref[...], sin_ref[...]
        is_lat = (i >= nct).astype(F32)
        dqg = jnp.zeros((1, hd), F32)
        dkg = jnp.zeros((1, hd), F32)
        for hh in range(nh + nkv):
            if hh < nh:
                dr = dq_ref[:, hh * hd:(hh + 1) * hd] * is_lat
                gn = gn_ref[0:1, :]
            else:
                dr = dk_ref[:, (hh - nh) * hd:(hh - nh + 1) * hd]
                gn = gn_ref[1:2, :]
            dy = dr * c + _rope_partner(dr * s)
            xhat, r = _rms_parts(qkv_ref[:, hh * hd:(hh + 1) * hd])
            dgh = _rowsum(dy * xhat)
            if hh < nh:
                dqg = dqg + dgh
            else:
                dkg = dkg + dgh
            dxhat = dy * gn
            dx = r * (dxhat - xhat * jnp.mean(dxhat * xhat, axis=-1, keepdims=True))
            out_ref[:, hh * hd:(hh + 1) * hd] = dx.astype(BF16)
        out_ref[:, (nh + nkv) * hd:] = dv_ref[...].astype(BF16)
        _acc_rows(dgn_ref, i == 0, {0: dqg, 1: dkg})

    return pl.pallas_call(
        body, name="attn_qkv_bwd", grid=(rows // tm,),
        in_specs=[_rows(tm, qw), pl.BlockSpec((tm, nh * hd), lambda i: (jnp.maximum(i - nct, 0), 0)),
                  _rows(tm, nkv * hd), _rows(tm, nkv * hd), _rows(tm, hd), _rows(tm, hd), _full((8, hd))],
        out_specs=[_rows(tm, qw), _full((8, hd))],
        out_shape=[jax.ShapeDtypeStruct((rows, qw), BF16), jax.ShapeDtypeStruct((8, hd), F32)],
        compiler_params=_params(),
    )(qkv, dq, dk, dv, cos, sin, gains)


def _attn_in_bwd(dqkv, wqkv, hc, dh1, mods, *, nct, tm):
    rows, d = hc.shape
    qw = dqkv.shape[1]

    def body(dz_ref, w_ref, h_ref, g_ref, md_ref, dh_ref, dmd_ref):
        i = pl.program_id(0)
        md = md_ref[...]
        da = _dot_nt_blocks(dz_ref[...], w_ref)
        _, parts = _normmod(h_ref[...], md, 0)
        dx, dsh, dsc, dng = _normmod_bwd(da, parts, md, 0)
        dh_ref[...] = g_ref[...] * (i >= nct).astype(F32) + dx
        _acc_rows(dmd_ref, _first_of_group(i, nct, 2), {SH1: dsh, SC1: dsc, NG0: dng})

    return pl.pallas_call(
        body, name="attn_in_bwd", grid=(rows // tm,),
        in_specs=[_rows(tm, qw), _full(wqkv.shape), _rows(tm, d),
                  pl.BlockSpec((tm, d), lambda i: (jnp.maximum(i - nct, 0), 0)), _mods_spec(nct, 2, d)],
        out_specs=[_rows(tm, d), _mods_spec(nct, 2, d)],
        out_shape=[jax.ShapeDtypeStruct((rows, d), F32), jax.ShapeDtypeStruct((2, 8, d), F32)],
        compiler_params=_params(),
    )(dqkv, wqkv, hc, dh1, mods)


def _gmlp_gate(z, lng, lnb, ws_ref, bs_ref, gg, ch):
    half = z.shape[1] // 2
    ggw = half // gg
    u, v = z[:, :half], z[:, half:]
    vc = v - jnp.mean(v, axis=-1, keepdims=True)
    rs = lax.rsqrt(jnp.mean(vc * vc, axis=-1, keepdims=True) + EPS)
    vhat = vc * rs
    vln = (vhat * lng + lnb).astype(BF16)
    chunks = []
    for n in range(z.shape[0] // ch):
        groups = []
        for g in range(gg):
            groups.append(_dot(ws_ref[g], vln[n * ch:(n + 1) * ch, g * ggw:(g + 1) * ggw]) + bs_ref[g])
        chunks.append(jnp.concatenate(groups, axis=1))
    sv = jnp.concatenate(chunks, axis=0) if len(chunks) > 1 else chunks[0]
    return u, sv, vhat, rs, vln


def _gmlp_fwd(h, mods, w_in, lng, lnb, ws, bs, w_out, *, tm):
    seq, d = h.shape
    zw = w_in.shape[0] * w_in.shape[-1]
    half = zw // 2
    gg, ch = ws.shape[0], ws.shape[-1]

    def body(h_ref, md_ref, win_ref, lng_ref, lnb_ref, ws_ref, bs_ref, wout_ref, h1_ref, zp_ref, y_ref):
        x = h_ref[...]
        md = md_ref[...]
        a, _ = _normmod(x, md, 0)
        zp = _dot_blocks(a.astype(BF16), win_ref)
        zp_ref[...] = zp.astype(BF16)
        u, sv, _, _, _ = _gmlp_gate(_gelu(zp), lng_ref[...], lnb_ref[...], ws_ref, bs_ref, gg, ch)
        y = _dot((u * sv).astype(BF16), wout_ref[...])
        y_ref[...] = y.astype(BF16)
        h1_ref[...] = x + md[G1:G1 + 1] * y

    return pl.pallas_call(
        body, name="gmlp_fwd", grid=(seq // tm,),
        in_specs=[_rows(tm, d), pl.BlockSpec((None, 8, d), lambda i: (1, 0, 0)),
                  _full(w_in.shape), _full((1, half)), _full((1, half)),
                  _full((gg, ch, ch)), _full((gg, ch, 1)), _full((half, d))],
        out_specs=[_rows(tm, d), _rows(tm, zw), _rows(tm, d)],
        out_shape=[jax.ShapeDtypeStruct((seq, d), F32), jax.ShapeDtypeStruct((seq, zw), BF16),
                   jax.ShapeDtypeStruct((seq, d), BF16)],
        compiler_params=_params(),
    )(h, mods, w_in, lng, lnb, ws, bs, w_out)


def _gmlp_bwd(h, dh1, zpre, y, mods, w_in, lng, lnb, ws, ws_t, bs, w_out, *, tm):
    seq, d = h.shape
    zw = w_in.shape[0] * w_in.shape[-1]
    half = zw // 2
    gg, ch = ws.shape[0], ws.shape[-1]
    ggw = half // gg

    def body(h_ref, g_ref, zp_ref, y_ref, md_ref, win_ref, lng_ref, lnb_ref, ws_ref, wst_ref, bs_ref, wout_ref,
             dh_ref, dzp_ref, gated_ref, dyb_ref, ab_ref, dmd_ref, dln_ref, dws_ref, dbs_ref):
        i = pl.program_id(0)
        x = h_ref[...]
        md = md_ref[...]
        a, parts = _normmod(x, md, 0)
        ab_ref[...] = a.astype(BF16)
        zp = zp_ref[...].astype(F32)
        lng_v = lng_ref[...]
        z, dgelu = _gelu_and_grad(zp)
        u, sv, vhat, rs, vln = _gmlp_gate(z, lng_v, lnb_ref[...], ws_ref, bs_ref, gg, ch)
        g = g_ref[...]
        dg1 = _rowsum(g * y_ref[...].astype(F32))
        dyb = (g * md[G1:G1 + 1]).astype(BF16)
        dyb_ref[...] = dyb
        gated_ref[...] = (u * sv).astype(BF16)
        dgated = _dot_nt(dyb, wout_ref[...])
        du = dgated * sv
        dsv = dgated * u

        @pl.when(i == 0)
        def _():
            dws_ref[...] = jnp.zeros(dws_ref.shape, F32)
            dbs_ref[...] = jnp.zeros(dbs_ref.shape, F32)
            dln_ref[...] = jnp.zeros(dln_ref.shape, F32)

        chunks = []
        for n in range(tm // ch):
            groups = []
            for gi in range(gg):
                blk = dsv[n * ch:(n + 1) * ch, gi * ggw:(gi + 1) * ggw]
                dbs_ref[gi] += jnp.sum(blk, axis=-1, keepdims=True)
                blk_b = blk.astype(BF16)
                dws_ref[gi] += _dot_nt(blk_b, vln[n * ch:(n + 1) * ch, gi * ggw:(gi + 1) * ggw])
                groups.append(_dot(wst_ref[gi], blk_b))
            chunks.append(jnp.concatenate(groups, axis=1))
        dvln = jnp.concatenate(chunks, axis=0) if len(chunks) > 1 else chunks[0]
        dln_ref[0:1, :] += _rowsum(dvln * vhat)
        dln_ref[1:2, :] += _rowsum(dvln)
        dvhat = dvln * lng_v
        dv = rs * (dvhat - jnp.mean(dvhat, axis=-1, keepdims=True)
                   - vhat * jnp.mean(dvhat * vhat, axis=-1, keepdims=True))
        dzp = (jnp.concatenate([du, dv], axis=1) * dgelu).astype(BF16)
        dzp_ref[...] = dzp
        da = _dot_nt_blocks(dzp, win_ref)
        dx, dsh, dsc, dng = _normmod_bwd(da, parts, md, 0)
        dh_ref[...] = g + dx
        _acc_rows(dmd_ref, i == 0, {SH1: dsh, SC1: dsc, G1: dg1, NG0: dng})

    return pl.pallas_call(
        body, name="gmlp_bwd", grid=(seq // tm,),
        in_specs=[_rows(tm, d), _rows(tm, d), _rows(tm, zw), _rows(tm, d),
                  pl.BlockSpec((None, 8, d), lambda i: (1, 0, 0)),
                  _full(w_in.shape), _full((1, half)), _full((1, half)),
                  _full((gg, ch, ch)), _full((gg, ch, ch)), _full((gg, ch, 1)), _full((half, d))],
        out_specs=[_rows(tm, d), _rows(tm, zw), _rows(tm, half), _rows(tm, d), _rows(tm, d),
                   pl.BlockSpec((None, 8, d), lambda i: (0, 0, 0)), _full((8, half)), _full((gg, ch, ch)),
                   _full((gg, ch, 1))],
        out_shape=[jax.ShapeDtypeStruct((seq, d), F32), jax.ShapeDtypeStruct((seq, zw), BF16),
                   jax.ShapeDtypeStruct((seq, half), BF16), jax.ShapeDtypeStruct((seq, d), BF16),
                   jax.ShapeDtypeStruct((seq, d), BF16), jax.ShapeDtypeStruct((1, 8, d), F32),
                   jax.ShapeDtypeStruct((8, half), F32), jax.ShapeDtypeStruct((gg, ch, ch), F32),
                   jax.ShapeDtypeStruct((gg, ch, 1), F32)],
        compiler_params=_params(),
    )(h, dh1, zpre, y, mods, w_in, lng, lnb, ws, ws_t, bs, w_out)


def _final_loss(h, tgt, fg, *, tm):
    seq, d = h.shape

    def body(h_ref, t_ref, g_ref, dh_ref, acc_ref):
        i = pl.program_id(0)
        gain = g_ref[...]
        xhat, r = _rms_parts(h_ref[...])
        err = xhat * gain - t_ref[...]
        dy = err * (1.0 / d)
        dxhat = dy * gain
        dh_ref[...] = r * (dxhat - xhat * jnp.mean(dxhat * xhat, axis=-1, keepdims=True))
        part = jnp.sum(_rowsum(err * err), axis=-1, keepdims=True) * (0.5 / d)
        _acc_rows(acc_ref, i == 0, {0: _rowsum(dy * xhat), 1: jnp.broadcast_to(part, (1, d))})

    return pl.pallas_call(
        body, name="final_loss", grid=(seq // tm,),
        in_specs=[_rows(tm, d), _rows(tm, d), _full((1, d))],
        out_specs=[_rows(tm, d), _full((8, d))],
        out_shape=[jax.ShapeDtypeStruct((seq, d), F32), jax.ShapeDtypeStruct((8, d), F32)],
        compiler_params=_params(),
    )(h, tgt, fg)


def _ada_fwd(c_all, ada_w, ada_b_cols):
    depth, d, ncs = ada_w.shape

    def body(c_ref, w_ref, b_ref, o_ref):
        s = _silu(c_ref[...]).astype(BF16)
        o_ref[...] = _dot(s, w_ref[...].astype(BF16)) + b_ref[...]

    return pl.pallas_call(
        body, name="ada_fwd", grid=(depth,),
        in_specs=[_full((16, d)), pl.BlockSpec((None, d, ncs), lambda i: (i, 0, 0)),
                  pl.BlockSpec((None, 1, ncs), lambda i: (i, 0, 0))],
        out_specs=pl.BlockSpec((None, 16, ncs), lambda i: (i, 0, 0)),
        out_shape=jax.ShapeDtypeStruct((depth, 16, ncs), F32),
        compiler_params=_params(),
    )(c_all, ada_w, ada_b_cols.reshape(depth, 1, ncs))


def _ada_bwd(c_all, c_all_t, dmod, ada_w):
    depth, d, ncs = ada_w.shape

    def body(c_ref, ct_ref, dm_ref, w_ref, gw_ref, dc_ref):
        i = pl.program_id(0)
        dm = dm_ref[...]
        dctx = _rowsum(dm[8:16])
        rid = lax.broadcasted_iota(jnp.int32, (8, ncs), 0)
        low = jnp.where(rid == 0, jnp.broadcast_to(dctx, (8, ncs)), 0.0)
        dm16 = jnp.concatenate([dm[0:8], low], axis=0).astype(BF16)
        gw_ref[...] = _dot(_silu(ct_ref[...]).astype(BF16), dm16)

        @pl.when(i == 0)
        def _():
            dc_ref[...] = jnp.zeros(dc_ref.shape, F32)

        dc_ref[...] += _dot_nt(low.astype(BF16), w_ref[...].astype(BF16)) * _silu_grad(c_ref[8:9, :])

    return pl.pallas_call(
        body, name="ada_bwd", grid=(depth,),
        in_specs=[_full((16, d)), _full((d, 16)), pl.BlockSpec((None, 16, ncs), lambda i: (i, 0, 0)),
                  pl.BlockSpec((None, d, ncs), lambda i: (i, 0, 0))],
        out_specs=[pl.BlockSpec((None, d, ncs), lambda i: (i, 0, 0)), _full((8, d))],
        out_shape=[jax.ShapeDtypeStruct((depth, d, ncs), F32), jax.ShapeDtypeStruct((8, d), F32)],
        compiler_params=_params(),
    )(c_all, c_all_t, dmod, ada_w)


def _adamw_math(w, g, m, v):
    m = ADAM_B1 * m + (1.0 - ADAM_B1) * g
    v = ADAM_B2 * v + (1.0 - ADAM_B2) * jnp.square(g)
    m_hat = m * (1.0 / (1.0 - ADAM_B1 ** ADAM_STEP))
    v_hat = v * (1.0 / (1.0 - ADAM_B2 ** ADAM_STEP))
    delta = -ADAM_LR * (m_hat / (jnp.sqrt(v_hat) + ADAM_EPS) + ADAM_WD * w)
    return delta, m, v


def _adamw(ga, gb, w, m, v, name):
    rows, cols = w.shape
    tr = rows
    while tr * cols * 4 > (1 << 20) and tr % 16 == 0:
        tr //= 2
    grads = [ga] if gb is None else [ga, gb]

    def body(*refs):
        w_ref, m_ref, v_ref, g_out, d_out, m_out, v_out = refs[len(grads):]
        g = refs[0][...] if gb is None else refs[0][...] + refs[1][...]
        delta, m_new, v_new = _adamw_math(w_ref[...], g, m_ref[...], v_ref[...])
        g_out[...] = g
        d_out[...] = delta
        m_out[...] = m_new
        v_out[...] = v_new

    spec = _rows(tr, cols)
    return pl.pallas_call(
        body, name=name, grid=(rows // tr,),
        in_specs=[spec] * (len(grads) + 3), out_specs=[spec] * 4,
        out_shape=[jax.ShapeDtypeStruct((rows, cols), F32)] * 4,
        compiler_params=_params(),
    )(*grads, w, m, v)


def _sum_devices(gathered, name):
    n, rows, cols = gathered.shape
    tr = rows
    while tr * cols * 4 * n > (4 << 20) and tr % 16 == 0:
        tr //= 2

    def body(x_ref, o_ref):
        acc = x_ref[0]
        for j in range(1, n):
            acc = acc + x_ref[j]
        o_ref[...] = acc

    return pl.pallas_call(
        body, name=name, grid=(rows // tr,),
        in_specs=[pl.BlockSpec((n, tr, cols), lambda i: (0, i, 0))], out_specs=_rows(tr, cols),
        out_shape=jax.ShapeDtypeStruct((rows, cols), F32),
        compiler_params=_params(),
    )(gathered)


def _sum_partials(blocked, landeds, chip, name):
    n = len(blocked)
    cols = blocked[0].shape[-1]
    blocked = [b.reshape(N_CHIPS, -1, cols) for b in blocked]
    landeds = [l.reshape(3, -1, cols) for l in landeds]
    rows = blocked[0].shape[1]
    tr = rows
    while tr * cols * 2 * n > (1 << 20) and tr % 32 == 0:
        tr //= 2

    def body(chip_ref, *refs):
        out_ref = refs[-1]
        for li in range(n):
            acc = refs[li][...].astype(F32)
            for p in range(3):
                acc = acc + refs[n + li][p].astype(F32)
            out_ref[li] = acc

    out = pl.pallas_call(
        body, name=name,
        grid_spec=pltpu.PrefetchScalarGridSpec(
            num_scalar_prefetch=1, grid=(rows // tr,),
            in_specs=[pl.BlockSpec((None, tr, cols), lambda i, k: (k[0], i, 0))] * n
            + [pl.BlockSpec((3, tr, cols), lambda i, k: (0, i, 0))] * n,
            out_specs=pl.BlockSpec((n, tr, cols), lambda i, k: (0, i, 0))),
        out_shape=jax.ShapeDtypeStruct((n, rows, cols), F32),
        compiler_params=_params(),
    )(jnp.reshape(chip, (1,)).astype(jnp.int32), *blocked, *landeds)
    return out.reshape(n * rows, cols)


def _my_place():
    return lax.axis_index("x"), lax.axis_index("y"), lax.axis_index("c")


def _other_chips(x, y):
    return [(1 - x, y), (x, 1 - y), (1 - x, 1 - y)]


def _all_gather_small(block, name):
    rows, cols = block.shape

    def body(x_ref, out_ref, send_sems, recv_sems, local_sem):
        x, y, c = _my_place()
        me, sibling = (x, y, c), (x, y, 1 - c)
        chips = _other_chips(x, y)

        def slot(px, py, pc):
            return out_ref.at[4 * px + 2 * py + pc]

        def copy(k, blk, to, src=None):
            return pltpu.make_async_remote_copy(
                src_ref=slot(*blk) if src is None else src, dst_ref=slot(*blk),
                send_sem=send_sems.at[k], recv_sem=recv_sems.at[k], device_id=to, device_id_type=MESH)

        mine = pltpu.make_async_copy(x_ref, slot(*me), local_sem)
        mine.start()
        first = [copy(0, me, sibling, src=x_ref)]
        first += [copy(1 + j, me, (*chip, c), src=x_ref) for j, chip in enumerate(chips)]
        for cp in first:
            cp.start()
        passed = [copy(4 + j, (*chip, c), sibling) for j, chip in enumerate(chips)]
        for j, chip in enumerate(chips):
            copy(1 + j, (*chip, c), me).wait_recv()
            passed[j].start()
        copy(0, sibling, me).wait_recv()
        for j, chip in enumerate(chips):
            copy(4 + j, (*chip, 1 - c), me).wait_recv()
        for cp in first + passed:
            cp.wait_send()
        mine.wait()

    return pl.pallas_call(
        body, name=name,
        out_shape=jax.ShapeDtypeStruct((N_DEV, rows, cols), block.dtype),
        in_specs=[pl.BlockSpec(memory_space=pltpu.VMEM)],
        out_specs=pl.BlockSpec(memory_space=pltpu.VMEM),
        scratch_shapes=[pltpu.SemaphoreType.DMA((7,)), pltpu.SemaphoreType.DMA((7,)), pltpu.SemaphoreType.DMA],
        compiler_params=_params(),
    )(block)


HBM_SPEC = pl.BlockSpec(memory_space=pltpu.HBM)
SEM_SPEC = pl.BlockSpec(memory_space=pltpu.SEMAPHORE)
DATAFLOW_EFFECT = pltpu.SideEffectType.DATAFLOW_SIDE_EFFECTING


def _same_core_of_other_chips(x, y, c):
    return [(*chip, c) for chip in _other_chips(x, y)]


def _sibling_core(x, y, c):
    return [(x, y, 1 - c)]


def _gather_views(src, land, p, x, y):
    return src, land.at[2 * x + y]


def _scatter_views(src, land, p, x, y):
    peer_chip = (2 * (1 - x) + y, 2 * x + (1 - y), 2 * (1 - x) + (1 - y))[p]
    return src.at[peer_chip], land.at[p]


def _whole_views(src, land, p, x, y):
    return src, land


GATHER_PLAN = (_same_core_of_other_chips, _gather_views, 3)
SCATTER_PLAN = (_same_core_of_other_chips, _scatter_views, 3)
SIBLING_PLAN = (_sibling_core, _whole_views, 1)


def _exchange_copies(srcs, lands, send_sems, recv_sems, plan):
    peers_of, views, n_peers = plan
    x, y, c = _my_place()
    copies = []
    for j, (src, land) in enumerate(zip(srcs, lands)):
        for p, peer in enumerate(peers_of(x, y, c)):
            s_view, d_view = views(src, land, p, x, y)
            k = n_peers * j + p
            copies.append(pltpu.make_async_remote_copy(
                src_ref=s_view, dst_ref=d_view, send_sem=send_sems.at[k], recv_sem=recv_sems.at[k],
                device_id=peer, device_id_type=MESH))
    return copies


def _exchange_start(srcs, lands, plan, name):
    n = len(srcs)

    def body(*refs):
        send_sems, recv_sems = refs[2 * n], refs[2 * n + 1]
        token = refs[-1]
        for cp in _exchange_copies(refs[:n], refs[n:2 * n], send_sems, recv_sems, plan):
            cp.start()
        token[...] = jnp.zeros(token.shape, token.dtype)

    operands = [pltpu.with_memory_space_constraint(a, pltpu.HBM) for a in (*srcs, *lands)]
    out = pl.pallas_call(
        body, name=name,
        out_shape=(pltpu.SemaphoreType.DMA((plan[2] * n,)), pltpu.SemaphoreType.DMA((plan[2] * n,)),
                   *[pltpu.HBM(a.shape, a.dtype) for a in operands], jax.ShapeDtypeStruct((8, LANES), F32)),
        in_specs=[HBM_SPEC] * (2 * n),
        out_specs=(SEM_SPEC, SEM_SPEC, *[HBM_SPEC] * (2 * n), pl.BlockSpec(memory_space=pltpu.VMEM)),
        input_output_aliases={i: 2 + i for i in range(2 * n)},
        compiler_params=pltpu.CompilerParams(has_side_effects=DATAFLOW_EFFECT),
    )(*operands)
    return out[0], out[1], list(out[2:2 + n]), list(out[2 + n:2 + 2 * n]), out[-1]


def _exchange_wait(send_sems, recv_sems, srcs, lands, plan, after, name):
    n = len(srcs)

    def body(*refs):
        send, recv = refs[2 * n], refs[2 * n + 1]
        for cp in _exchange_copies(refs[:n], refs[n:2 * n], send, recv, plan):
            cp.wait_send()
            cp.wait_recv()

    out = pl.pallas_call(
        body, name=name,
        out_shape=tuple(pltpu.HBM(a.shape, a.dtype) for a in (*srcs, *lands)),
        in_specs=[HBM_SPEC] * (2 * n) + [SEM_SPEC, SEM_SPEC, HBM_SPEC],
        out_specs=tuple([HBM_SPEC] * (2 * n)),
        input_output_aliases={i: i for i in range(2 * n)},
        compiler_params=pltpu.CompilerParams(has_side_effects=DATAFLOW_EFFECT),
    )(*srcs, *lands, send_sems, recv_sems, pltpu.with_memory_space_constraint(after, pltpu.HBM))
    return list(out[:n]), list(out[n:])


def _landing_for_gather(shard, chip):
    land = lax.empty((N_CHIPS, *shard.shape), shard.dtype)
    return lax.dynamic_update_index_in_dim(land, shard, chip, 0)


TILE_ELEMS = SUBLANES * LANES


def _pack(arrays):
    parts = []
    for a in arrays:
        flat = a.reshape(-1).astype(F32)
        pad = (-flat.shape[0]) % TILE_ELEMS
        if pad:
            flat = jnp.concatenate([flat, jnp.zeros((pad,), F32)])
        parts.append(flat.reshape(-1, LANES))
    return jnp.concatenate(parts, axis=0) if len(parts) > 1 else parts[0]


def _unpack(buf, shapes):
    out, r = [], 0
    lead = buf.shape[:-2]
    for shp in shapes:
        size = math.prod(shp)
        nr = -(-size // TILE_ELEMS) * SUBLANES
        flat = buf[..., r:r + nr, :].reshape(*lead, nr * LANES)[..., :size]
        out.append(flat.reshape(*lead, *shp))
        r += nr
    return out


def _chip_cols(a, k, width):
    return lax.dynamic_slice_in_dim(a, k * width, width, axis=a.ndim - 1)


def _across_chips(gathered, c0_only_shape):
    return gathered.reshape(2, 2, 2, *c0_only_shape)[:, :, 0].reshape(N_CHIPS, *c0_only_shape)


def kernel(x, c, ctx, c_ctx, ada_w, ada_b, norm_g, mlp_w1, mlp_w2, pool_w, pool_scale, attn_w_qkv, attn_w_o, attn_q_g, attn_k_g, gm_w_in, gm_ln_g, gm_ln_b, gm_ws, gm_bs, gm_w_out, final_g, loss_target, m_c_ctx, m_ada_w, m_ada_b, m_norm_g, m_mlp_w1, m_mlp_w2, m_pool_w, m_pool_scale, m_attn_w_qkv, m_attn_w_o, m_attn_q_g, m_attn_k_g, m_gm_w_in, m_gm_ln_g, m_gm_ln_b, m_gm_ws, m_gm_bs, m_gm_w_out, m_final_g, v_c_ctx, v_ada_w, v_ada_b, v_norm_g, v_mlp_w1, v_mlp_w2, v_pool_w, v_pool_scale, v_attn_w_qkv, v_attn_w_o, v_attn_q_g, v_attn_k_g, v_gm_w_in, v_gm_ln_g, v_gm_ln_b, v_gm_ws, v_gm_bs, v_gm_w_out, v_final_g):
    seq, d = x.shape[1], x.shape[2]
    n_ctx = ctx.shape[1]
    total = n_ctx + seq
    hd = attn_q_g.shape[-1]
    nh = d // hd
    nkv = nh // 2
    gg, ch = gm_ws.shape[1], gm_ws.shape[-1]
    half = gm_w_out.shape[1] * N_CHIPS
    pgw = pool_w.shape[-1]
    tm = min(256, n_ctx)
    nct = n_ctx // tm
    seg_lens = (n_ctx, seq)

    mx, my, mc = _my_place()
    chip = 2 * mx + my
    me = 4 * mx + 2 * my + mc

    c_rows = jnp.concatenate([c, jnp.zeros((7, d), F32)], axis=0)
    c_gath = _all_gather_small(c_rows, "gather_cond")[:, 0, :]
    c_all = jnp.concatenate([c_gath, c_ctx[None, :], jnp.zeros((7, d), F32)], axis=0)
    ncs = ada_w.shape[-1]
    ada_cols = _ada_fwd(c_all, ada_w, _chip_cols(ada_b, chip, ncs))
    small_shapes = [ada_cols.shape, norm_g.shape, pool_scale.shape, gm_ln_g.shape, gm_ln_b.shape]
    gathered = _all_gather_small(_pack([ada_cols, norm_g, pool_scale, gm_ln_g, gm_ln_b]), "gather_small_params")
    per_chip = _across_chips(gathered, gathered.shape[1:])
    ada_g, ng_g, ps_g, lng_g, lnb_g = _unpack(per_chip, small_shapes)

    def join_last(a):
        return jnp.moveaxis(a, 0, -2).reshape(*a.shape[1:-1], N_CHIPS * a.shape[-1])

    ada_full = join_last(ada_g)
    ng_full = join_last(ng_g)
    ps_full = join_last(ps_g)
    lng_full = join_last(lng_g)
    lnb_full = join_last(lnb_g)
    mod_lat = lax.dynamic_slice_in_dim(ada_full, me, 1, axis=1).reshape(DEPTH, 6, d)
    mod_ctx = ada_full[:, 8].reshape(DEPTH, 6, d)
    mods = jnp.stack([jnp.concatenate([mod_ctx, ng_full], axis=1), jnp.concatenate([mod_lat, ng_full], axis=1)],
                     axis=1)

    weight_groups = [
        [pool_w],
        [mlp_w1[0], mlp_w2[0]],
        [attn_w_qkv[0], attn_w_o[0]],
        [mlp_w1[1], mlp_w2[1], mlp_w1[2], mlp_w2[2], gm_w_in[0], gm_w_out[0], mlp_w1[3], mlp_w2[3]],
    ]
    gathers = [None] * len(weight_groups)

    def gather_start(gi, after):
        shards, _ = lax.optimization_barrier(([w.astype(BF16) for w in weight_groups[gi]], after))
        lands = [_landing_for_gather(s, chip) for s in shards]
        gathers[gi] = _exchange_start(shards, lands, GATHER_PLAN, f"gather_weights_{gi}_start")
        return gathers[gi][4][0:1, 0:1]

    def gathered(gi, after):
        send, recv, srcs, lands, _ = gathers[gi]
        return _exchange_wait(send, recv, srcs, lands, GATHER_PLAN, after, f"gather_weights_{gi}_wait")[1]

    def rows_joined(a):
        return a.reshape(-1, a.shape[-1])

    w1_b, w2_b = [None] * DEPTH, [None] * DEPTH
    gather_start(0, mods)
    pw_land, = gathered(0, ps_full)
    behind_gather_1 = gather_start(1, pw_land)
    pw_f = jnp.transpose(pw_land, (1, 2, 0, 3, 4)).reshape(pool_w.shape[0], pool_w.shape[1], pgw, pgw)

    gains = jnp.concatenate([attn_q_g, attn_k_g, jnp.zeros((6, hd), F32)], axis=0)
    ws_b = gm_ws[0].astype(BF16)
    ws_t = jnp.swapaxes(gm_ws[0], 1, 2).astype(BF16)
    bs_col = gm_bs[0][:, :, None]
    cos, sin = _rope_tables(n_ctx, seq, hd)
    lat = lambda i: mods[i, 1:2]

    hc0 = jnp.concatenate([ctx[0] + behind_gather_1, x[0]], axis=0)
    ha0 = _pool_fwd(hc0, mods[0] + behind_gather_1, pw_f, ps_full, 0, nct=nct, tm=tm, seg_lens=seg_lens)
    w1_b[0], w2_b[0] = gathered(1, ha0)
    mods0 = mods[0] + gather_start(2, w1_b[0])
    hc1, u0, o0 = _mlp_fwd(ha0, mods0, w1_b[0], w2_b[0], 0, nct=nct, tm=tm)
    wqkv_b, wo_land = gathered(2, hc1)
    mods1 = mods[1] + gather_start(3, wqkv_b)
    wo_f = rows_joined(wo_land)
    xa1, qkv, q_r, k_r, v_b = _qkv_fwd(hc1, mods1, wqkv_b, cos, sin, gains, nh=nh, nkv=nkv, nct=nct, tm=tm)
    o_att, lse = _flash_fwd(q_r, k_r, v_b, n_ctx=n_ctx, hd=hd)
    ha1, y1 = _proj_fwd(o_att, wo_f, hc1, mods1, n_ctx=n_ctx, tm=tm)
    w1_b[1], w2_b[1], w1_b[2], w2_b[2], win_b, wout_land, w1_b[3], w2_b[3] = gathered(3, ha1)
    h2, u1, o1 = _mlp_fwd(ha1, lat(1), w1_b[1], w2_b[1], 1, nct=0, tm=tm)
    wout_f = rows_joined(wout_land)
    ha2, zpre, y2 = _gmlp_fwd(h2, mods[2], win_b, lng_full, lnb_full, ws_b, bs_col, wout_f, tm=tm)
    h3, u2, o2 = _mlp_fwd(ha2, lat(2), w1_b[2], w2_b[2], 2, nct=0, tm=tm)
    ha3 = _pool_fwd(h3, lat(3), pw_f, ps_full, 3, nct=0, tm=tm, seg_lens=seg_lens)
    h4, u3, o3 = _mlp_fwd(ha3, lat(3), w1_b[3], w2_b[3], 3, nct=0, tm=tm)
    dh4, fin_acc = _final_loss(h4, loss_target[0], final_g[None, :], tm=tm)

    dmods = [None] * DEPTH
    scatters = [None] * (DEPTH + 1)

    def blocked_rows(g):
        return g.reshape(N_CHIPS, g.shape[1] // N_CHIPS, g.shape[2])

    def blocked_pool(dpw):
        pg = dpw.shape[0]
        return jnp.transpose(dpw.astype(BF16).reshape(pg, N_CHIPS, pgw // N_CHIPS, pgw), (1, 0, 2, 3))

    def scatter_start(i, grads):
        lands = [lax.empty((3, *g.shape[1:]), g.dtype) for g in grads]
        scatters[i] = _exchange_start(grads, lands, SCATTER_PLAN, f"scatter_grads_{i}_start")
        return scatters[i][4][0:1, 0:1]

    def mlp_back(i, h_in, dh_out, u, o, md, n_ct):
        dh_in, du, dob, mb, dmd = _mlp_bwd(h_in, dh_out, u, o, md, w1_b[i], w2_b[i], i, nct=n_ct, tm=tm)
        dw1 = _mm_tn(mb, du, f"mlp_dw1_{i}", col_blocks=N_CHIPS)
        dw2 = blocked_rows(_mm_tn(u, dob, f"mlp_dw2_{i}", relu2=True))
        return dh_in, dmd, [dw1, dw2]

    def pool_back(i, h_in, dh_out, md, n_ct):
        dp, dmd_a, dps, dpw = _pool_bwd_weights(h_in, dh_out, md, pw_f, ps_full, i, nct=n_ct, tm=tm,
                                                seg_lens=seg_lens)
        dh_in, dmd_b = _pool_bwd_input(dp, h_in, dh_out, md, i, nct=n_ct, tm=tm, seg_lens=seg_lens, gw=pgw)
        return dh_in, dmd_a + dmd_b, dps, dpw

    zero_grp = jnp.zeros((1, 8, d), F32)
    dha3, dmd3, dws3 = mlp_back(3, ha3, dh4, u3, o3, lat(3), 0)
    dh3, dmd3p, dps3, dpw3 = pool_back(3, h3, dha3, lat(3), 0)
    dmods[3] = jnp.concatenate([zero_grp, dmd3 + dmd3p], axis=0)
    tok = scatter_start(3, dws3 + [blocked_pool(dpw3)])
    dha2, dmd2, dws2 = mlp_back(2, ha2, dh3, u2, o2, lat(2) + tok, 0)
    dh2, dzpre, gated, dyb2, ab2, dmd2g, dln, dws, dbs = _gmlp_bwd(
        h2, dha2, zpre, y2, mods[2], win_b, lng_full, lnb_full, ws_b, ws_t, bs_col, wout_f, tm=tm)
    dwin = _mm_tn(ab2, dzpre, "gmlp_dw_in", col_blocks=N_CHIPS)
    dwout = blocked_rows(_mm_tn(gated, dyb2, "gmlp_dw_out"))
    dmods[2] = jnp.concatenate([zero_grp, dmd2 + dmd2g], axis=0)
    tok = scatter_start(2, dws2 + [dwin, dwout])
    dha1, dmd1, dws1 = mlp_back(1, ha1, dh2, u1, o1, lat(1) + tok, 0)
    do_att, dyb1, dmd1p = _proj_bwd(dha1, y1, mods[1], wo_f, tm=tm)
    dwo = blocked_rows(_mm_tn(o_att, dyb1, "attn_dw_o"))
    dq, dk, dv = _flash_bwd(q_r, k_r, v_b, o_att, do_att, lse, n_ctx=n_ctx, hd=hd)
    dqkv, dgains = _qkv_bwd(qkv, dq, dk, dv, cos, sin, gains, nh=nh, nkv=nkv, nct=nct, tm=tm)
    dwqkv = _mm_tn(xa1, dqkv, "attn_dw_qkv", col_blocks=N_CHIPS)
    dhc1, dmd1i = _attn_in_bwd(dqkv, wqkv_b, hc1, dha1, mods[1], nct=nct, tm=tm)
    dmods[1] = dmd1i + jnp.concatenate([zero_grp, dmd1 + dmd1p], axis=0)
    tok = scatter_start(1, dws1 + [dwqkv, dwo])
    dha0, dmd0, dws0 = mlp_back(0, ha0, dhc1, u0, o0, mods[0] + tok, nct)
    tok = scatter_start(0, dws0)
    dhc0, dmd0p, dps0, dpw0 = pool_back(0, hc0, dha0, mods[0] + tok, nct)
    dmods[0] = dmd0 + dmd0p
    grad_x = dhc0[None]
    scatter_start(DEPTH, [blocked_pool(dpw0)])

    dmods_all = jnp.stack(dmods, axis=0)
    small_grads = [dmods_all, dws, dbs, dgains, dln, dps0, dps3, fin_acc]
    sg_shapes = [a.shape for a in small_grads]
    sg_gath = _all_gather_small(_pack(small_grads), "gather_small_grads")
    sg_sum = _sum_devices(sg_gath, "sum_small_grads")
    s_dmods, s_dws, s_dbs, s_dgains, s_dln, s_dps0, s_dps3, s_fin = _unpack(sg_sum, sg_shapes)
    loss = s_fin[1, 0]

    sources, landed = [None] * len(scatters), [None] * len(scatters)
    for i in (3, 2, 1, 0, DEPTH):
        send, recv, srcs, lands, _ = scatters[i]
        sources[i], landed[i] = _exchange_wait(send, recv, srcs, lands, SCATTER_PLAN, sg_sum, f"scatter_grads_{i}_wait")

    def summed(name, picks):
        return _sum_partials([sources[i][j] for i, j in picks], [landed[i][j] for i, j in picks], chip,
                             f"sum_chips_{name}")

    big = [("mlp_w1", mlp_w1, m_mlp_w1, v_mlp_w1, [(i, 0) for i in range(DEPTH)]),
           ("mlp_w2", mlp_w2, m_mlp_w2, v_mlp_w2, [(i, 1) for i in range(DEPTH)]),
           ("pool_w", pool_w, m_pool_w, v_pool_w, [(DEPTH, 0), (3, 2)]),
           ("attn_w_qkv", attn_w_qkv, m_attn_w_qkv, v_attn_w_qkv, [(1, 2)]),
           ("attn_w_o", attn_w_o, m_attn_w_o, v_attn_w_o, [(1, 3)]),
           ("gm_w_in", gm_w_in, m_gm_w_in, v_gm_w_in, [(2, 2)]),
           ("gm_w_out", gm_w_out, m_gm_w_out, v_gm_w_out, [(2, 3)])]
    partial = [summed(name, picks) for name, _, _, _, picks in big]
    swap = _exchange_start(partial, [lax.empty(p.shape, p.dtype) for p in partial], SIBLING_PLAN,
                           "swap_with_sibling_start")
    behind_swap = swap[4][0:1, 0:1]

    dm_dev = _unpack(sg_gath, sg_shapes[:1])[0]
    dm_lat = jnp.moveaxis(dm_dev[:, :, 1, :6, :], 0, 1).reshape(DEPTH, N_DEV, 6 * d)
    dm_ctx = jnp.moveaxis(dm_dev[:, :, 0, :6, :], 0, 1).reshape(DEPTH, N_DEV, 6 * d)
    dmod16 = _chip_cols(jnp.concatenate([dm_lat, dm_ctx], axis=1), chip, ncs) + behind_swap
    g_ada_w, dcc_part = _ada_bwd(c_all, c_all.T, dmod16, ada_w)
    dcc_gath = _all_gather_small(dcc_part, "gather_d_c_ctx")
    dcc_chips = _across_chips(dcc_gath, dcc_gath.shape[1:])
    dcc_rows = _sum_devices(dcc_chips, "sum_d_c_ctx")
    dcc = dcc_rows[0]
    ada_res = _adamw(g_ada_w.reshape(-1, ncs), None, ada_w.reshape(-1, ncs),
                     m_ada_w.reshape(-1, ncs), v_ada_w.reshape(-1, ncs), "adamw_ada_w")

    partial, from_sibling = _exchange_wait(swap[0], swap[1], swap[2], swap[3], SIBLING_PLAN, ada_res[1],
                                           "swap_with_sibling_wait")
    big_out = {}
    for (name, w, m, v, _), mine, theirs in zip(big, partial, from_sibling):
        cols = w.shape[-1]
        res = _adamw(mine, theirs, w.reshape(-1, cols), m.reshape(-1, cols), v.reshape(-1, cols), f"adamw_{name}")
        big_out[name] = [r.reshape(w.shape) for r in res]
    big_out["ada_w"] = [r.reshape(ada_w.shape) for r in ada_res]

    def cols_of(a, width):
        return _chip_cols(a, chip, width)

    zero = lambda a: jnp.zeros(a.shape, F32)
    ngw = norm_g.shape[-1]
    small = {
        "c_ctx": (dcc, zero(dcc), c_ctx, m_c_ctx, v_c_ctx),
        "ada_b": (s_dmods[:, 0, :6].reshape(DEPTH, 6 * d), s_dmods[:, 1, :6].reshape(DEPTH, 6 * d), ada_b, m_ada_b,
                  v_ada_b),
        "norm_g": (cols_of(s_dmods[:, 0, 6:8], ngw), cols_of(s_dmods[:, 1, 6:8], ngw), norm_g, m_norm_g, v_norm_g),
        "pool_scale": (cols_of(jnp.stack([s_dps0[0], s_dps3[0]]), pool_scale.shape[-1]), zero(pool_scale),
                       pool_scale, m_pool_scale, v_pool_scale),
        "attn_q_g": (s_dgains[0:1], zero(attn_q_g), attn_q_g, m_attn_q_g, v_attn_q_g),
        "attn_k_g": (s_dgains[1:2], zero(attn_k_g), attn_k_g, m_attn_k_g, v_attn_k_g),
        "gm_ln_g": (cols_of(s_dln[0:1], gm_ln_g.shape[-1]), zero(gm_ln_g), gm_ln_g, m_gm_ln_g, v_gm_ln_g),
        "gm_ln_b": (cols_of(s_dln[1:2], gm_ln_b.shape[-1]), zero(gm_ln_b), gm_ln_b, m_gm_ln_b, v_gm_ln_b),
        "gm_ws": (s_dws[None], zero(gm_ws), gm_ws, m_gm_ws, v_gm_ws),
        "gm_bs": (s_dbs[None, :, :, 0], zero(gm_bs), gm_bs, m_gm_bs, v_gm_bs),
        "final_g": (s_fin[0], zero(final_g), final_g, m_final_g, v_final_g),
    }
    keys = list(small)
    packed = [_pack([small[k][t] for k in keys]) for t in range(5)]
    res = _adamw(*packed, "adamw_small")
    shapes = [small[k][2].shape for k in keys]
    small_out = {k: [] for k in keys}
    for r in res:
        for k, a in zip(keys, _unpack(r, shapes)):
            small_out[k].append(a)

    order = ["c_ctx", "ada_w", "ada_b", "norm_g", "mlp_w1", "mlp_w2", "pool_w", "pool_scale", "attn_w_qkv",
             "attn_w_o", "attn_q_g", "attn_k_g", "gm_w_in", "gm_ln_g", "gm_ln_b", "gm_ws", "gm_bs", "gm_w_out",
             "final_g"]
    allo = {**big_out, **small_out}
    outs = [loss, grad_x]
    for t in range(4):
        outs += [allo[k][t] for k in order]
    return tuple(outs)
```

```python
import functools
import math

import numpy as np
import jax
import jax.numpy as jnp
from jax import lax
from jax.experimental import pallas as pl
from jax.experimental.pallas import tpu as pltpu

F32 = jnp.float32
BF16 = jnp.bfloat16
MESH = pl.DeviceIdType.MESH

EPS = 1e-6
GRID_W = 64
ROPE_BASE = 10000.0
POOL_WINDOWS = (2, 4, 8, 16)
HALO = 8
DEPTH = 4
N_MIXERS = 3

ADAM_LR = 0.001
ADAM_B1 = 0.9
ADAM_B2 = 0.999
ADAM_EPS = 1e-08
ADAM_WD = 0.01
ADAM_STEP = 10

VMEM_LIMIT_BYTES = 56 * 1024 * 1024
LANES = 128
SUBLANES = 8
N_DEV = 8
N_CHIPS = 4

SH1, SC1, G1, SH2, SC2, G2, NG0, NG1 = range(8)


def _dot(a, b):
    return jnp.dot(a, b, preferred_element_type=F32)


def _dot_nt(a, b):
    return lax.dot_general(a, b, (((1,), (1,)), ((), ())), preferred_element_type=F32)


def _dot_tn(a, b):
    return lax.dot_general(a, b, (((0,), (0,)), ((), ())), preferred_element_type=F32)


def _dot_blocks(a, w_ref):
    return jnp.concatenate([_dot(a, w_ref[k]) for k in range(w_ref.shape[0])], axis=1)


def _dot_nt_blocks(a, w_ref):
    nb, _, w = w_ref.shape
    acc = _dot_nt(a[:, 0:w], w_ref[0])
    for k in range(1, nb):
        acc = acc + _dot_nt(a[:, k * w:(k + 1) * w], w_ref[k])
    return acc


def _params(**kw):
    return pltpu.CompilerParams(vmem_limit_bytes=VMEM_LIMIT_BYTES, **kw)


def _full(shape):
    nd = len(shape)
    return pl.BlockSpec(shape, lambda *_: (0,) * nd)


def _rows(tm, width):
    return pl.BlockSpec((tm, width), lambda i: (i, 0))


def _group_of(nct, groups):
    if groups == 1:
        return lambda i: 0
    return lambda i: jnp.where(i >= nct, 1, 0)


def _mods_spec(nct, groups, d):
    grp = _group_of(nct, groups)
    return pl.BlockSpec((None, 8, d), lambda i: (grp(i), 0, 0))


def _first_of_group(i, nct, groups):
    if groups == 1:
        return i == 0
    return jnp.logical_or(i == 0, i == nct)


def _rowsum(v):
    return jnp.sum(v, axis=0, keepdims=True)


def _rms_parts(x):
    r = lax.rsqrt(jnp.mean(x * x, axis=-1, keepdims=True) + EPS)
    return x * r, r


def _normmod(x, md, which):
    ng, sh, sc = (md[NG0:NG0 + 1], md[SH1:SH1 + 1], md[SC1:SC1 + 1]) if which == 0 else (
        md[NG1:NG1 + 1], md[SH2:SH2 + 1], md[SC2:SC2 + 1])
    xhat, r = _rms_parts(x)
    n = xhat * ng
    return n * (1.0 + sc) + sh, (xhat, r, n)


def _normmod_bwd(da, parts, md, which):
    xhat, r, n = parts
    ng, sc = (md[NG0:NG0 + 1], md[SC1:SC1 + 1]) if which == 0 else (md[NG1:NG1 + 1], md[SC2:SC2 + 1])
    dsh = _rowsum(da)
    dsc = _rowsum(da * n)
    dn = da * (1.0 + sc)
    dng = _rowsum(dn * xhat)
    dxhat = dn * ng
    dx = r * (dxhat - xhat * jnp.mean(dxhat * xhat, axis=-1, keepdims=True))
    return dx, dsh, dsc, dng


def _acc_rows(ref, first, rows):
    @pl.when(first)
    def _():
        ref[...] = jnp.zeros(ref.shape, ref.dtype)

    for r, v in rows.items():
        ref[r:r + 1, :] += v


def _shift_up(x, k):
    if k == 0:
        return x
    return pltpu.roll(x, x.shape[0] - k, axis=0)


def _gelu(x):
    k = math.sqrt(2.0 / math.pi)
    return 0.5 * x * (1.0 + jnp.tanh(k * (x + 0.044715 * x * x * x)))


def _gelu_grad(x):
    k = math.sqrt(2.0 / math.pi)
    t = jnp.tanh(k * (x + 0.044715 * x * x * x))
    return 0.5 * (1.0 + t) + 0.5 * x * (1.0 - t * t) * k * (1.0 + 3.0 * 0.044715 * x * x)


def _silu(x):
    return x / (1.0 + jnp.exp(-x))


def _silu_grad(x):
    s = 1.0 / (1.0 + jnp.exp(-x))
    return s * (1.0 + x * (1.0 - s))


def _mlp_fwd(h, mods, w1, w2, layer, *, nct, tm):
    rows, d = h.shape
    groups = mods.shape[0]
    nb, _, fc = w1.shape
    ff = nb * fc

    def body(h_ref, md_ref, w1_ref, w2_ref, h2_ref, u_ref, o_ref):
        x = h_ref[...]
        md = md_ref[...]
        m, _ = _normmod(x, md, 1)
        mb = m.astype(BF16)
        acc = jnp.zeros((tm, d), F32)
        for k in range(nb):
            u = _dot(mb, w1_ref[k])
            u_ref[:, k * fc:(k + 1) * fc] = u.astype(BF16)
            acc = acc + _dot(jnp.square(jnp.maximum(u, 0.0)).astype(BF16), w2_ref[k])
        o_ref[...] = acc.astype(BF16)
        h2_ref[...] = x + md[G2:G2 + 1] * acc

    return pl.pallas_call(
        body, name=f"mlp_fwd_{layer}", grid=(rows // tm,),
        in_specs=[_rows(tm, d), _mods_spec(nct, groups, d), _full(w1.shape), _full(w2.shape)],
        out_specs=[_rows(tm, d), _rows(tm, ff), _rows(tm, d)],
        out_shape=[jax.ShapeDtypeStruct((rows, d), F32), jax.ShapeDtypeStruct((rows, ff), BF16),
                   jax.ShapeDtypeStruct((rows, d), BF16)],
        compiler_params=_params(),
    )(h, mods, w1, w2)


def _mlp_bwd(h1, dh2, u, o, mods, w1, w2, layer, *, nct, tm):
    rows, d = h1.shape
    groups = mods.shape[0]
    nb, _, fc = w1.shape
    ff = nb * fc

    def body(h_ref, g_ref, u_ref, o_ref, md_ref, w1_ref, w2_ref, dh_ref, du_ref, dob_ref, mb_ref, dmd_ref):
        i = pl.program_id(0)
        x = h_ref[...]
        g = g_ref[...]
        md = md_ref[...]
        m, parts = _normmod(x, md, 1)
        mb_ref[...] = m.astype(BF16)
        dg2 = _rowsum(g * o_ref[...].astype(F32))
        dob = (g * md[G2:G2 + 1]).astype(BF16)
        dob_ref[...] = dob
        dm = jnp.zeros((tm, d), F32)
        for k in range(nb):
            uk = u_ref[:, k * fc:(k + 1) * fc].astype(F32)
            dr = _dot_nt(dob, w2_ref[k])
            duk = (dr * (2.0 * jnp.maximum(uk, 0.0))).astype(BF16)
            du_ref[:, k * fc:(k + 1) * fc] = duk
            dm = dm + _dot_nt(duk, w1_ref[k])
        dx, dsh, dsc, dng = _normmod_bwd(dm, parts, md, 1)
        dh_ref[...] = g + dx
        _acc_rows(dmd_ref, _first_of_group(i, nct, groups), {SH2: dsh, SC2: dsc, G2: dg2, NG1: dng})

    return pl.pallas_call(
        body, name=f"mlp_bwd_{layer}", grid=(rows // tm,),
        in_specs=[_rows(tm, d), _rows(tm, d), _rows(tm, ff), _rows(tm, d), _mods_spec(nct, groups, d),
                  _full(w1.shape), _full(w2.shape)],
        out_specs=[_rows(tm, d), _rows(tm, ff), _rows(tm, d), _rows(tm, d), _mods_spec(nct, groups, d)],
        out_shape=[jax.ShapeDtypeStruct((rows, d), F32), jax.ShapeDtypeStruct((rows, ff), BF16),
                   jax.ShapeDtypeStruct((rows, d), BF16), jax.ShapeDtypeStruct((rows, d), BF16),
                   jax.ShapeDtypeStruct((groups, 8, d), F32)],
        compiler_params=_params(),
    )(h1, dh2, u, o, mods, w1, w2)


def _div_tile(n, cap):
    if n <= cap:
        return n
    return max(t for t in range(LANES, cap + 1, LANES) if n % t == 0)


DW_TOKEN_TILE_CAP = 4224


def _mm_tn(a, b, name, *, relu2=False, col_blocks=1):
    rows, m = a.shape
    n = b.shape[1]
    tmm = min(m, 1024)
    tn = min(n // col_blocks, 2048)
    per_block = n // col_blocks // tn
    tr = _div_tile(rows, DW_TOKEN_TILE_CAP)

    def body(a_ref, b_ref, o_ref, acc_ref):
        r = pl.program_id(2)

        @pl.when(r == 0)
        def _():
            acc_ref[...] = jnp.zeros(acc_ref.shape, F32)

        av = a_ref[...]
        if relu2:
            av = jnp.square(jnp.maximum(av.astype(F32), 0.0)).astype(BF16)
        acc_ref[...] += _dot_tn(av, b_ref[...])

        @pl.when(r == pl.num_programs(2) - 1)
        def _():
            o_ref[...] = acc_ref[...].astype(BF16)

    return pl.pallas_call(
        body, name=name, grid=(m // tmm, n // tn, rows // tr),
        in_specs=[pl.BlockSpec((tr, tmm), lambda i, j, r: (r, i)), pl.BlockSpec((tr, tn), lambda i, j, r: (r, j))],
        out_specs=pl.BlockSpec((None, tmm, tn), lambda i, j, r: (j // per_block, i, j % per_block)),
        out_shape=jax.ShapeDtypeStruct((col_blocks, m, n // col_blocks), BF16),
        scratch_shapes=[pltpu.VMEM((tmm, tn), F32)],
        compiler_params=_params(),
    )(a, b)


def _halo_specs(tm, d, rows):
    per = tm // HALO
    prev = pl.BlockSpec((HALO, d), lambda i: (jnp.maximum(i * per - 1, 0), 0))
    nxt = pl.BlockSpec((HALO, d), lambda i: (jnp.minimum((i + 1) * per, rows // HALO - 1), 0))
    return prev, _rows(tm, d), nxt


def _segment_positions(i, tm, nct, groups, seg_lens):
    if groups == 1:
        start, length = 0, seg_lens[-1]
    else:
        start = jnp.where(i >= nct, nct, 0)
        length = jnp.where(i >= nct, seg_lens[1], seg_lens[0])
    rid = lax.broadcasted_iota(jnp.int32, (tm + 2 * HALO, 1), 0)
    pos = (i - start) * tm - HALO + rid
    return pos, length


def _window_count(pos, length, w):
    hi = jnp.minimum(pos + (w - w // 2), length)
    lo = jnp.maximum(pos - w // 2, 0)
    return (hi - lo).astype(F32)


def _window_sum(xg, w, lead):
    b, k = xg, 1
    while k < w:
        b = b + _shift_up(b, k)
        k *= 2
    return _shift_up(b, HALO - lead)[0:xg.shape[0] - 2 * HALO]


def _pooled(ext, md, pos, length, gw):
    tm = ext.shape[0] - 2 * HALO
    a_ext, parts = _normmod(ext, md, 0)
    valid = jnp.logical_and(pos >= 0, pos < length)
    a_ext = jnp.where(valid, a_ext, 0.0)
    pos_c = pos[HALO:HALO + tm]
    ps = []
    for g, w in enumerate(POOL_WINDOWS):
        xg = a_ext[:, g * gw:(g + 1) * gw]
        s = _window_sum(xg, w, w // 2)
        ps.append(s * (1.0 / _window_count(pos_c, length, w)) - xg[HALO:HALO + tm])
    return ps, parts


def _pool_fwd(h, mods, pw, pscale, layer, *, nct, tm, seg_lens):
    rows, d = h.shape
    groups = mods.shape[0]
    pg, gw = pw.shape[1], pw.shape[-1]

    def body(prev_ref, cur_ref, next_ref, md_ref, pw_ref, ps_ref, out_ref):
        i = pl.program_id(0)
        md = md_ref[...]
        cur = cur_ref[...]
        ext = jnp.concatenate([prev_ref[...], cur, next_ref[...]], axis=0)
        pos, length = _segment_positions(i, tm, nct, groups, seg_lens)
        ps, _ = _pooled(ext, md, pos, length, gw)
        for g in range(pg):
            yg = _dot(ps[g].astype(BF16), pw_ref[g]) * ps_ref[:, g * gw:(g + 1) * gw]
            out_ref[:, g * gw:(g + 1) * gw] = cur[:, g * gw:(g + 1) * gw] + md[G1:G1 + 1, g * gw:(g + 1) * gw] * yg

    j = layer // N_MIXERS
    return pl.pallas_call(
        body, name=f"pool_fwd_{layer}", grid=(rows // tm,),
        in_specs=[*_halo_specs(tm, d, rows), _mods_spec(nct, groups, d),
                  pl.BlockSpec((None, pg, gw, gw), lambda i: (j, 0, 0, 0)), _full((1, d))],
        out_specs=_rows(tm, d),
        out_shape=jax.ShapeDtypeStruct((rows, d), F32),
        compiler_params=_params(),
    )(h, h, h, mods, pw, pscale[j:j + 1])


def _pool_bwd_weights(h, dh1, mods, pw, pscale, layer, *, nct, tm, seg_lens):
    rows, d = h.shape
    groups = mods.shape[0]
    pg, gw = pw.shape[1], pw.shape[-1]

    def body(prev_ref, cur_ref, next_ref, g_ref, md_ref, pw_ref, ps_ref, dp_ref, dmd_ref, dps_ref, dpw_ref):
        i = pl.program_id(0)
        md = md_ref[...]
        ext = jnp.concatenate([prev_ref[...], cur_ref[...], next_ref[...]], axis=0)
        pos, length = _segment_positions(i, tm, nct, groups, seg_lens)
        ps, _ = _pooled(ext, md, pos, length, gw)
        gup = g_ref[...]

        @pl.when(i == 0)
        def _():
            dps_ref[...] = jnp.zeros(dps_ref.shape, F32)
            dpw_ref[...] = jnp.zeros(dpw_ref.shape, F32)

        dg1 = []
        for g in range(pg):
            cols = slice(g * gw, (g + 1) * gw)
            pb = ps[g].astype(BF16)
            yp = _dot(pb, pw_ref[g])
            sc = ps_ref[:, cols]
            dg1.append(_rowsum(gup[:, cols] * (yp * sc)))
            dy = gup[:, cols] * md[G1:G1 + 1, cols]
            dps_ref[0:1, cols] += _rowsum(dy * yp)
            dyp = (dy * sc).astype(BF16)
            dp_ref[:, cols] = _dot_nt(dyp, pw_ref[g])
            dpw_ref[g] += _dot_tn(pb, dyp)
        _acc_rows(dmd_ref, _first_of_group(i, nct, groups), {G1: jnp.concatenate(dg1, axis=1)})

    j = layer // N_MIXERS
    return pl.pallas_call(
        body, name=f"pool_bwd_w_{layer}", grid=(rows // tm,),
        in_specs=[*_halo_specs(tm, d, rows), _rows(tm, d), _mods_spec(nct, groups, d),
                  pl.BlockSpec((None, pg, gw, gw), lambda i: (j, 0, 0, 0)), _full((1, d))],
        out_specs=[_rows(tm, d), _mods_spec(nct, groups, d), _full((8, d)), _full((pg, gw, gw))],
        out_shape=[jax.ShapeDtypeStruct((rows, d), F32), jax.ShapeDtypeStruct((groups, 8, d), F32),
                   jax.ShapeDtypeStruct((8, d), F32), jax.ShapeDtypeStruct((pg, gw, gw), F32)],
        compiler_params=_params(),
    )(h, h, h, dh1, mods, pw, pscale[j:j + 1])


def _pool_bwd_input(dp, h, dh1, mods, layer, *, nct, tm, seg_lens, gw):
    rows, d = h.shape
    groups = mods.shape[0]

    def body(prev_ref, cur_ref, next_ref, h_ref, g_ref, md_ref, dh_ref, dmd_ref):
        i = pl.program_id(0)
        md = md_ref[...]
        dp_cur = cur_ref[...]
        ext = jnp.concatenate([prev_ref[...], dp_cur, next_ref[...]], axis=0)
        pos, length = _segment_positions(i, tm, nct, groups, seg_lens)
        valid = jnp.logical_and(pos >= 0, pos < length)
        das = []
        for g, w in enumerate(POOL_WINDOWS):
            cols = slice(g * gw, (g + 1) * gw)
            q = jnp.where(valid, ext[:, cols] * (1.0 / jnp.maximum(_window_count(pos, length, w), 1.0)), 0.0)
            das.append(_window_sum(q, w, w // 2 - 1) - dp_cur[:, cols])
        da = jnp.concatenate(das, axis=1)
        _, parts = _normmod(h_ref[...], md, 0)
        dx, dsh, dsc, dng = _normmod_bwd(da, parts, md, 0)
        dh_ref[...] = g_ref[...] + dx
        _acc_rows(dmd_ref, _first_of_group(i, nct, groups), {SH1: dsh, SC1: dsc, NG0: dng})

    return pl.pallas_call(
        body, name=f"pool_bwd_x_{layer}", grid=(rows // tm,),
        in_specs=[*_halo_specs(tm, d, rows), _rows(tm, d), _rows(tm, d), _mods_spec(nct, groups, d)],
        out_specs=[pl.BlockSpec((tm, d), lambda i: (jnp.maximum(i - nct, 0), 0)), _mods_spec(nct, groups, d)],
        out_shape=[jax.ShapeDtypeStruct((rows - nct * tm, d), F32), jax.ShapeDtypeStruct((groups, 8, d), F32)],
        compiler_params=_params(),
    )(dp, dp, dp, h, dh1, mods)


def _rope_tables(n_ctx, seq, hd):
    half = hd // 2
    n_rows = seq // GRID_W
    inv = np.float32(ROPE_BASE) ** (-np.arange(0, half, 2, dtype=np.float32) / np.float32(half))
    ar = np.arange(n_rows, dtype=np.float32)[:, None] * inv[None, :]
    ac = np.arange(GRID_W, dtype=np.float32)[:, None] * inv[None, :]

    def over_tokens(row_part, col_part):
        r = jnp.repeat(jnp.asarray(row_part, F32), GRID_W, axis=0)
        c = jnp.tile(jnp.asarray(col_part, F32), (n_rows, 1))
        return r, c

    cr, cc = over_tokens(np.cos(ar), np.cos(ac))
    sr, sc = over_tokens(np.sin(ar), np.sin(ac))
    cos = jnp.concatenate([cr, cr, cc, cc], axis=1)
    sin = jnp.concatenate([-sr, sr, -sc, sc], axis=1)
    cos = jnp.concatenate([jnp.ones((n_ctx, hd), F32), cos], axis=0)
    sin = jnp.concatenate([jnp.zeros((n_ctx, hd), F32), sin], axis=0)
    return cos, sin


def _rope_partner(x):
    hd = x.shape[-1]
    q = hd // 4
    lane = lax.broadcasted_iota(jnp.int32, x.shape, 1)
    first = (lane % (2 * q)) < q
    return jnp.where(first, pltpu.roll(x, hd - q, axis=1), pltpu.roll(x, q, axis=1))


def _qkv_fwd(h, mods, wqkv, cos, sin, gains, *, nh, nkv, nct, tm):
    rows, d = h.shape
    qw = wqkv.shape[0] * wqkv.shape[-1]
    hd = cos.shape[-1]

    def body(h_ref, md_ref, w_ref, cos_ref, sin_ref, gn_ref, xa_ref, qkv_ref, q_ref, k_ref, v_ref):
        a, _ = _normmod(h_ref[...], md_ref[...], 0)
        xa = a.astype(BF16)
        xa_ref[...] = xa
        qkv = _dot_blocks(xa, w_ref)
        qkv_ref[...] = qkv
        c, s = cos_ref[...], sin_ref[...]
        for hh in range(nh + nkv):
            xh = qkv[:, hh * hd:(hh + 1) * hd]
            xhat, _ = _rms_parts(xh)
            y = xhat * (gn_ref[0:1, :] if hh < nh else gn_ref[1:2, :])
            rot = (y * c + _rope_partner(y) * s).astype(BF16)
            if hh < nh:
                q_ref[:, hh * hd:(hh + 1) * hd] = rot
            else:
                k_ref[:, (hh - nh) * hd:(hh - nh + 1) * hd] = rot
        v_ref[...] = qkv[:, (nh + nkv) * hd:].astype(BF16)

    return pl.pallas_call(
        body, name="attn_qkv_fwd", grid=(rows // tm,),
        in_specs=[_rows(tm, d), _mods_spec(nct, 2, d), _full(wqkv.shape), _rows(tm, hd), _rows(tm, hd),
                  _full((8, hd))],
        out_specs=[_rows(tm, d), _rows(tm, qw), pl.BlockSpec((tm, nh * hd), lambda i: (jnp.maximum(i - nct, 0), 0)),
                   _rows(tm, nkv * hd), _rows(tm, nkv * hd)],
        out_shape=[jax.ShapeDtypeStruct((rows, d), BF16), jax.ShapeDtypeStruct((rows, qw), F32),
                   jax.ShapeDtypeStruct((rows - nct * tm, nh * hd), BF16),
                   jax.ShapeDtypeStruct((rows, nkv * hd), BF16), jax.ShapeDtypeStruct((rows, nkv * hd), BF16)],
        compiler_params=_params(),
    )(h, mods, wqkv, cos, sin, gains)


ATTN_Q_TILE_CAP = 1024
ATTN_KV_TILE_CAP = 4224
ATTN_ROW_GROUP = 256
LOG2E = 1.4426950408889634


def _attn_tiles(seq, total):
    tq = _div_tile(seq, ATTN_Q_TILE_CAP)
    return tq, _div_tile(total, ATTN_KV_TILE_CAP), min(ATTN_ROW_GROUP, tq)


def _flash_fwd(q, k, v, *, n_ctx, hd):
    total = k.shape[0]
    seq = total - n_ctx
    nkv = k.shape[1] // hd
    tq, tk, rg = _attn_tiles(seq, total)
    nk = total // tk
    scale = hd ** -0.5
    c2 = scale * LOG2E

    def body(q_ref, k_ref, v_ref, o_ref, lse_ref, m_sc, l_sc, acc_sc):
        ki = pl.program_id(2)

        @pl.when(ki == 0)
        def _():
            m_sc[...] = jnp.full(m_sc.shape, -jnp.inf, F32)
            l_sc[...] = jnp.zeros(l_sc.shape, F32)
            acc_sc[...] = jnp.zeros(acc_sc.shape, F32)

        kk, vv = k_ref[...], v_ref[...]
        groups = [(g, sub) for g in range(2) for sub in range(tq // rg)]

        def scores(g, sub):
            return _dot_nt(q_ref[sub * rg:(sub + 1) * rg, g * hd:(g + 1) * hd], kk)

        s_next = scores(*groups[0])
        for idx, (g, sub) in enumerate(groups):
            s = s_next
            if idx + 1 < len(groups):
                s_next = scores(*groups[idx + 1])
            rows = slice(g * tq + sub * rg, g * tq + (sub + 1) * rg)
            m_old = m_sc[rows]
            m_new = jnp.maximum(m_old, jnp.max(s, axis=-1, keepdims=True))
            alpha = jnp.exp2((m_old - m_new) * c2)
            p = jnp.exp2((s - m_new) * c2)
            l_sc[rows] = alpha * l_sc[rows] + jnp.sum(p, axis=-1, keepdims=True)
            acc_sc[rows] = alpha * acc_sc[rows] + _dot(p.astype(BF16), vv)
            m_sc[rows] = m_new

        @pl.when(ki == nk - 1)
        def _():
            o2 = acc_sc[...] / l_sc[...]
            lse = m_sc[...] * scale + jnp.log(l_sc[...])
            o_ref[:, :hd] = o2[:tq].astype(BF16)
            o_ref[:, hd:] = o2[tq:].astype(BF16)
            lse_ref[:, 0:1] = lse[:tq]
            lse_ref[:, 1:2] = lse[tq:]

    return pl.pallas_call(
        body, name="attn_flash_fwd", grid=(nkv, seq // tq, nk),
        in_specs=[pl.BlockSpec((tq, 2 * hd), lambda h, i, j: (i, h)),
                  pl.BlockSpec((tk, hd), lambda h, i, j: (j, h)),
                  pl.BlockSpec((tk, hd), lambda h, i, j: (j, h))],
        out_specs=[pl.BlockSpec((tq, 2 * hd), lambda h, i, j: (i, h)),
                   pl.BlockSpec((None, tq, 2), lambda h, i, j: (h, i, 0))],
        out_shape=[jax.ShapeDtypeStruct((seq, 2 * nkv * hd), BF16), jax.ShapeDtypeStruct((nkv, seq, 2), F32)],
        scratch_shapes=[pltpu.VMEM((2 * tq, 1), F32), pltpu.VMEM((2 * tq, 1), F32), pltpu.VMEM((2 * tq, hd), F32)],
        compiler_params=_params(),
    )(q, k, v)


def _flash_bwd(q, k, v, o, do, lse, *, n_ctx, hd):
    total = k.shape[0]
    seq = total - n_ctx
    nkv = k.shape[1] // hd
    tq, tk, rg = _attn_tiles(seq, total)
    scale = hd ** -0.5
    c2 = scale * LOG2E

    def body(q_ref, k_ref, v_ref, o_ref, do_ref, lse_ref, dq_ref, dk_ref, dv_ref):
        ki, qi = pl.program_id(1), pl.program_id(2)
        kk, vv = k_ref[...], v_ref[...]

        @pl.when(qi == 0)
        def _():
            dk_ref[...] = jnp.zeros(dk_ref.shape, F32)
            dv_ref[...] = jnp.zeros(dv_ref.shape, F32)

        dk_acc = jnp.zeros((tk, hd), F32)
        dv_acc = jnp.zeros((tk, hd), F32)
        for g in range(2):
            for sub in range(tq // rg):
                rs = slice(sub * rg, (sub + 1) * rg)
                cs = slice(g * hd, (g + 1) * hd)
                qq = q_ref[rs, cs]
                dd = do_ref[rs, cs]
                delta = jnp.sum(dd.astype(F32) * o_ref[rs, cs].astype(F32), axis=-1, keepdims=True)
                p = jnp.exp2(_dot_nt(qq, kk) * c2 - lse_ref[rs, g:g + 1] * LOG2E)
                dp = _dot_nt(dd, vv)
                ds = (p * (dp - delta) * scale).astype(BF16)
                dv_acc = dv_acc + _dot_tn(p.astype(BF16), dd)
                dk_acc = dk_acc + _dot_tn(ds, qq)
                dq = _dot(ds, kk)
                rows = pl.ds(pl.multiple_of(qi * tq, tq) + sub * rg, rg)

                @pl.when(ki == 0)
                def _():
                    dq_ref[rows, cs] = dq

                @pl.when(ki > 0)
                def _():
                    dq_ref[rows, cs] += dq
        dk_ref[...] += dk_acc
        dv_ref[...] += dv_acc

    return pl.pallas_call(
        body, name="attn_flash_bwd", grid=(nkv, total // tk, seq // tq),
        in_specs=[pl.BlockSpec((tq, 2 * hd), lambda h, j, i: (i, h)),
                  pl.BlockSpec((tk, hd), lambda h, j, i: (j, h)),
                  pl.BlockSpec((tk, hd), lambda h, j, i: (j, h)),
                  pl.BlockSpec((tq, 2 * hd), lambda h, j, i: (i, h)),
                  pl.BlockSpec((tq, 2 * hd), lambda h, j, i: (i, h)),
                  pl.BlockSpec((None, tq, 2), lambda h, j, i: (h, i, 0))],
        out_specs=[pl.BlockSpec((seq, 2 * hd), lambda h, j, i: (0, h)),
                   pl.BlockSpec((tk, hd), lambda h, j, i: (j, h)),
                   pl.BlockSpec((tk, hd), lambda h, j, i: (j, h))],
        out_shape=[jax.ShapeDtypeStruct((seq, 2 * nkv * hd), F32), jax.ShapeDtypeStruct((total, nkv * hd), F32),
                   jax.ShapeDtypeStruct((total, nkv * hd), F32)],
        compiler_params=_params(),
    )(q, k, v, o, do, lse)


def _proj_fwd(o, wo, hc, mods, *, n_ctx, tm):
    seq, d = o.shape
    off = n_ctx // tm

    def body(o_ref, w_ref, h_ref, md_ref, h1_ref, y_ref):
        y = _dot(o_ref[...], w_ref[...])
        y_ref[...] = y.astype(BF16)
        h1_ref[...] = h_ref[...] + md_ref[G1:G1 + 1, :] * y

    return pl.pallas_call(
        body, name="attn_proj_fwd", grid=(seq // tm,),
        in_specs=[_rows(tm, d), _full((d, d)),
                  pl.BlockSpec((tm, d), lambda i: (i + off, 0)), pl.BlockSpec((None, 8, d), lambda i: (1, 0, 0))],
        out_specs=[_rows(tm, d), _rows(tm, d)],
        out_shape=[jax.ShapeDtypeStruct((seq, d), F32), jax.ShapeDtypeStruct((seq, d), BF16)],
        compiler_params=_params(),
    )(o, wo, hc, mods)


def _proj_bwd(dh1, y, mods, wo, *, tm):
    seq, d = dh1.shape

    def body(g_ref, y_ref, md_ref, w_ref, do_ref, dyb_ref, dmd_ref):
        i = pl.program_id(0)
        g = g_ref[...]
        dyb = (g * md_ref[G1:G1 + 1, :]).astype(BF16)
        dyb_ref[...] = dyb
        do_ref[...] = _dot_nt(dyb, w_ref[...]).astype(BF16)
        _acc_rows(dmd_ref, i == 0, {G1: _rowsum(g * y_ref[...].astype(F32))})

    return pl.pallas_call(
        body, name="attn_proj_bwd", grid=(seq // tm,),
        in_specs=[_rows(tm, d), _rows(tm, d), pl.BlockSpec((None, 8, d), lambda i: (1, 0, 0)), _full((d, d))],
        out_specs=[_rows(tm, d), _rows(tm, d), pl.BlockSpec((None, 8, d), lambda i: (0, 0, 0))],
        out_shape=[jax.ShapeDtypeStruct((seq, d), BF16), jax.ShapeDtypeStruct((seq, d), BF16),
                   jax.ShapeDtypeStruct((1, 8, d), F32)],
        compiler_params=_params(),
    )(dh1, y, mods, wo)


def _qkv_bwd(qkv, dq, dk, dv, cos, sin, gains, *, nh, nkv, nct, tm):
    rows, qw = qkv.shape
    hd = cos.shape[-1]

    def body(qkv_ref, dq_ref, dk_ref, dv_ref, cos_ref, sin_ref, gn_ref, out_ref, dgn_ref):
        i = pl.program_id(0)
        c, s = cos_ref[...], sin_ref[...]
        is_lat = (i >= nct).astype(F32)
        dqg = jnp.zeros((1, hd), F32)
        dkg = jnp.zeros((1, hd), F32)
        for hh in range(nh + nkv):
            if hh < nh:
                dr = dq_ref[:, hh * hd:(hh + 1) * hd] * is_lat
                gn = gn_ref[0:1, :]
            else:
                dr = dk_ref[:, (hh - nh) * hd:(hh - nh + 1) * hd]
                gn = gn_ref[1:2, :]
            dy = dr * c + _rope_partner(dr * s)
            xhat, r = _rms_parts(qkv_ref[:, hh * hd:(hh + 1) * hd])
            dgh = _rowsum(dy * xhat)
            if hh < nh:
                dqg = dqg + dgh
            else:
                dkg = dkg + dgh
            dxhat = dy * gn
            dx = r * (dxhat - xhat * jnp.mean(dxhat * xhat, axis=-1, keepdims=True))
            out_ref[:, hh * hd:(hh + 1) * hd] = dx.astype(BF16)
        out_ref[:, (nh + nkv) * hd:] = dv_ref[...].astype(BF16)
        _acc_rows(dgn_ref, i == 0, {0: dqg, 1: dkg})

    return pl.pallas_call(
        body, name="attn_qkv_bwd", grid=(rows // tm,),
        in_specs=[_rows(tm, qw), pl.BlockSpec((tm, nh * hd), lambda i: (jnp.maximum(i - nct, 0), 0)),
                  _rows(tm, nkv * hd), _rows(tm, nkv * hd), _rows(tm, hd), _rows(tm, hd), _full((8, hd))],
        out_specs=[_rows(tm, qw), _full((8, hd))],
        out_shape=[jax.ShapeDtypeStruct((rows, qw), BF16), jax.ShapeDtypeStruct((8, hd), F32)],
        compiler_params=_params(),
    )(qkv, dq, dk, dv, cos, sin, gains)


def _attn_in_bwd(dqkv, wqkv, hc, dh1, mods, *, nct, tm):
    rows, d = hc.shape
    qw = dqkv.shape[1]

    def body(dz_ref, w_ref, h_ref, g_ref, md_ref, dh_ref, dmd_ref):
        i = pl.program_id(0)
        md = md_ref[...]
        da = _dot_nt_blocks(dz_ref[...], w_ref)
        _, parts = _normmod(h_ref[...], md, 0)
        dx, dsh, dsc, dng = _normmod_bwd(da, parts, md, 0)
        dh_ref[...] = g_ref[...] * (i >= nct).astype(F32) + dx
        _acc_rows(dmd_ref, _first_of_group(i, nct, 2), {SH1: dsh, SC1: dsc, NG0: dng})

    return pl.pallas_call(
        body, name="attn_in_bwd", grid=(rows // tm,),
        in_specs=[_rows(tm, qw), _full(wqkv.shape), _rows(tm, d),
                  pl.BlockSpec((tm, d), lambda i: (jnp.maximum(i - nct, 0), 0)), _mods_spec(nct, 2, d)],
        out_specs=[_rows(tm, d), _mods_spec(nct, 2, d)],
        out_shape=[jax.ShapeDtypeStruct((rows, d), F32), jax.ShapeDtypeStruct((2, 8, d), F32)],
        compiler_params=_params(),
    )(dqkv, wqkv, hc, dh1, mods)


def _gmlp_gate(z, lng, lnb, ws_ref, bs_ref, gg, ch):
    half = z.shape[1] // 2
    ggw = half // gg
    u, v = z[:, :half], z[:, half:]
    vc = v - jnp.mean(v, axis=-1, keepdims=True)
    rs = lax.rsqrt(jnp.mean(vc * vc, axis=-1, keepdims=True) + EPS)
    vhat = vc * rs
    vln = (vhat * lng + lnb).astype(BF16)
    chunks = []
    for n in range(z.shape[0] // ch):
        groups = []
        for g in range(gg):
            groups.append(_dot(ws_ref[g], vln[n * ch:(n + 1) * ch, g * ggw:(g + 1) * ggw]) + bs_ref[g])
        chunks.append(jnp.concatenate(groups, axis=1))
    sv = jnp.concatenate(chunks, axis=0) if len(chunks) > 1 else chunks[0]
    return u, sv, vhat, rs, vln


def _gmlp_fwd(h, mods, w_in, lng, lnb, ws, bs, w_out, *, tm):
    seq, d = h.shape
    zw = w_in.shape[0] * w_in.shape[-1]
    half = zw // 2
    gg, ch = ws.shape[0], ws.shape[-1]

    def body(h_ref, md_ref, win_ref, lng_ref, lnb_ref, ws_ref, bs_ref, wout_ref, h1_ref, zp_ref, y_ref):
        x = h_ref[...]
        md = md_ref[...]
        a, _ = _normmod(x, md, 0)
        zp = _dot_blocks(a.astype(BF16), win_ref)
        zp_ref[...] = zp.astype(BF16)
        u, sv, _, _, _ = _gmlp_gate(_gelu(zp), lng_ref[...], lnb_ref[...], ws_ref, bs_ref, gg, ch)
        y = _dot((u * sv).astype(BF16), wout_ref[...])
        y_ref[...] = y.astype(BF16)
        h1_ref[...] = x + md[G1:G1 + 1] * y

    return pl.pallas_call(
        body, name="gmlp_fwd", grid=(seq // tm,),
        in_specs=[_rows(tm, d), pl.BlockSpec((None, 8, d), lambda i: (1, 0, 0)),
                  _full(w_in.shape), _full((1, half)), _full((1, half)),
                  _full((gg, ch, ch)), _full((gg, ch, 1)), _full((half, d))],
        out_specs=[_rows(tm, d), _rows(tm, zw), _rows(tm, d)],
        out_shape=[jax.ShapeDtypeStruct((seq, d), F32), jax.ShapeDtypeStruct((seq, zw), BF16),
                   jax.ShapeDtypeStruct((seq, d), BF16)],
        compiler_params=_params(),
    )(h, mods, w_in, lng, lnb, ws, bs, w_out)


def _gmlp_bwd(h, dh1, zpre, y, mods, w_in, lng, lnb, ws, ws_t, bs, w_out, *, tm):
    seq, d = h.shape
    zw = w_in.shape[0] * w_in.shape[-1]
    half = zw // 2
    gg, ch = ws.shape[0], ws.shape[-1]
    ggw = half // gg

    def body(h_ref, g_ref, zp_ref, y_ref, md_ref, win_ref, lng_ref, lnb_ref, ws_ref, wst_ref, bs_ref, wout_ref,
             dh_ref, dzp_ref, gated_ref, dyb_ref, ab_ref, dmd_ref, dln_ref, dws_ref, dbs_ref):
        i = pl.program_id(0)
        x = h_ref[...]
        md = md_ref[...]
        a, parts = _normmod(x, md, 0)
        ab_ref[...] = a.astype(BF16)
        zp = zp_ref[...].astype(F32)
        lng_v = lng_ref[...]
        u, sv, vhat, rs, vln = _gmlp_gate(_gelu(zp), lng_v, lnb_ref[...], ws_ref, bs_ref, gg, ch)
        g = g_ref[...]
        dg1 = _rowsum(g * y_ref[...].astype(F32))
        dyb = (g * md[G1:G1 + 1]).astype(BF16)
        dyb_ref[...] = dyb
        gated_ref[...] = (u * sv).astype(BF16)
        dgated = _dot_nt(dyb, wout_ref[...])
        du = dgated * sv
        dsv = dgated * u

        @pl.when(i == 0)
        def _():
            dws_ref[...] = jnp.zeros(dws_ref.shape, F32)
            dbs_ref[...] = jnp.zeros(dbs_ref.shape, F32)
            dln_ref[...] = jnp.zeros(dln_ref.shape, F32)

        chunks = []
        for n in range(tm // ch):
            groups = []
            for gi in range(gg):
                blk = dsv[n * ch:(n + 1) * ch, gi * ggw:(gi + 1) * ggw]
                dbs_ref[gi] += jnp.sum(blk, axis=-1, keepdims=True)
                blk_b = blk.astype(BF16)
                dws_ref[gi] += _dot_nt(blk_b, vln[n * ch:(n + 1) * ch, gi * ggw:(gi + 1) * ggw])
                groups.append(_dot(wst_ref[gi], blk_b))
            chunks.append(jnp.concatenate(groups, axis=1))
        dvln = jnp.concatenate(chunks, axis=0) if len(chunks) > 1 else chunks[0]
        dln_ref[0:1, :] += _rowsum(dvln * vhat)
        dln_ref[1:2, :] += _rowsum(dvln)
        dvhat = dvln * lng_v
        dv = rs * (dvhat - jnp.mean(dvhat, axis=-1, keepdims=True)
                   - vhat * jnp.mean(dvhat * vhat, axis=-1, keepdims=True))
        dzp = (jnp.concatenate([du, dv], axis=1) * _gelu_grad(zp)).astype(BF16)
        dzp_ref[...] = dzp
        da = _dot_nt_blocks(dzp, win_ref)
        dx, dsh, dsc, dng = _normmod_bwd(da, parts, md, 0)
        dh_ref[...] = g + dx
        _acc_rows(dmd_ref, i == 0, {SH1: dsh, SC1: dsc, G1: dg1, NG0: dng})

    return pl.pallas_call(
        body, name="gmlp_bwd", grid=(seq // tm,),
        in_specs=[_rows(tm, d), _rows(tm, d), _rows(tm, zw), _rows(tm, d),
                  pl.BlockSpec((None, 8, d), lambda i: (1, 0, 0)),
                  _full(w_in.shape), _full((1, half)), _full((1, half)),
                  _full((gg, ch, ch)), _full((gg, ch, ch)), _full((gg, ch, 1)), _full((half, d))],
        out_specs=[_rows(tm, d), _rows(tm, zw), _rows(tm, half), _rows(tm, d), _rows(tm, d),
                   pl.BlockSpec((None, 8, d), lambda i: (0, 0, 0)), _full((8, half)), _full((gg, ch, ch)),
                   _full((gg, ch, 1))],
        out_shape=[jax.ShapeDtypeStruct((seq, d), F32), jax.ShapeDtypeStruct((seq, zw), BF16),
                   jax.ShapeDtypeStruct((seq, half), BF16), jax.ShapeDtypeStruct((seq, d), BF16),
                   jax.ShapeDtypeStruct((seq, d), BF16), jax.ShapeDtypeStruct((1, 8, d), F32),
                   jax.ShapeDtypeStruct((8, half), F32), jax.ShapeDtypeStruct((gg, ch, ch), F32),
                   jax.ShapeDtypeStruct((gg, ch, 1), F32)],
        compiler_params=_params(),
    )(h, dh1, zpre, y, mods, w_in, lng, lnb, ws, ws_t, bs, w_out)


def _final_loss(h, tgt, fg, *, tm):
    seq, d = h.shape

    def body(h_ref, t_ref, g_ref, dh_ref, acc_ref):
        i = pl.program_id(0)
        gain = g_ref[...]
        xhat, r = _rms_parts(h_ref[...])
        err = xhat * gain - t_ref[...]
        dy = err * (1.0 / d)
        dxhat = dy * gain
        dh_ref[...] = r * (dxhat - xhat * jnp.mean(dxhat * xhat, axis=-1, keepdims=True))
        part = jnp.sum(_rowsum(err * err), axis=-1, keepdims=True) * (0.5 / d)
        _acc_rows(acc_ref, i == 0, {0: _rowsum(dy * xhat), 1: jnp.broadcast_to(part, (1, d))})

    return pl.pallas_call(
        body, name="final_loss", grid=(seq // tm,),
        in_specs=[_rows(tm, d), _rows(tm, d), _full((1, d))],
        out_specs=[_rows(tm, d), _full((8, d))],
        out_shape=[jax.ShapeDtypeStruct((seq, d), F32), jax.ShapeDtypeStruct((8, d), F32)],
        compiler_params=_params(),
    )(h, tgt, fg)


def _ada_fwd(c_all, ada_w, ada_b_cols):
    depth, d, ncs = ada_w.shape

    def body(c_ref, w_ref, b_ref, o_ref):
        s = _silu(c_ref[...]).astype(BF16)
        o_ref[...] = _dot(s, w_ref[...].astype(BF16)) + b_ref[...]

    return pl.pallas_call(
        body, name="ada_fwd", grid=(depth,),
        in_specs=[_full((16, d)), pl.BlockSpec((None, d, ncs), lambda i: (i, 0, 0)),
                  pl.BlockSpec((None, 1, ncs), lambda i: (i, 0, 0))],
        out_specs=pl.BlockSpec((None, 16, ncs), lambda i: (i, 0, 0)),
        out_shape=jax.ShapeDtypeStruct((depth, 16, ncs), F32),
        compiler_params=_params(),
    )(c_all, ada_w, ada_b_cols.reshape(depth, 1, ncs))


def _ada_bwd(c_all, c_all_t, dmod, ada_w):
    depth, d, ncs = ada_w.shape

    def body(c_ref, ct_ref, dm_ref, w_ref, gw_ref, dc_ref):
        i = pl.program_id(0)
        dm = dm_ref[...]
        dctx = _rowsum(dm[8:16])
        rid = lax.broadcasted_iota(jnp.int32, (8, ncs), 0)
        low = jnp.where(rid == 0, jnp.broadcast_to(dctx, (8, ncs)), 0.0)
        dm16 = jnp.concatenate([dm[0:8], low], axis=0).astype(BF16)
        gw_ref[...] = _dot(_silu(ct_ref[...]).astype(BF16), dm16)

        @pl.when(i == 0)
        def _():
            dc_ref[...] = jnp.zeros(dc_ref.shape, F32)

        dc_ref[...] += _dot_nt(low.astype(BF16), w_ref[...].astype(BF16)) * _silu_grad(c_ref[8:9, :])

    return pl.pallas_call(
        body, name="ada_bwd", grid=(depth,),
        in_specs=[_full((16, d)), _full((d, 16)), pl.BlockSpec((None, 16, ncs), lambda i: (i, 0, 0)),
                  pl.BlockSpec((None, d, ncs), lambda i: (i, 0, 0))],
        out_specs=[pl.BlockSpec((None, d, ncs), lambda i: (i, 0, 0)), _full((8, d))],
        out_shape=[jax.ShapeDtypeStruct((depth, d, ncs), F32), jax.ShapeDtypeStruct((8, d), F32)],
        compiler_params=_params(),
    )(c_all, c_all_t, dmod, ada_w)


def _adamw_math(w, g, m, v):
    m = ADAM_B1 * m + (1.0 - ADAM_B1) * g
    v = ADAM_B2 * v + (1.0 - ADAM_B2) * jnp.square(g)
    m_hat = m * (1.0 / (1.0 - ADAM_B1 ** ADAM_STEP))
    v_hat = v * (1.0 / (1.0 - ADAM_B2 ** ADAM_STEP))
    delta = -ADAM_LR * (m_hat / (jnp.sqrt(v_hat) + ADAM_EPS) + ADAM_WD * w)
    return delta, m, v


def _adamw(ga, gb, w, m, v, name):
    rows, cols = w.shape
    tr = rows
    while tr * cols * 4 > (1 << 20) and tr % 16 == 0:
        tr //= 2
    grads = [ga] if gb is None else [ga, gb]

    def body(*refs):
        w_ref, m_ref, v_ref, g_out, d_out, m_out, v_out = refs[len(grads):]
        g = refs[0][...] if gb is None else refs[0][...] + refs[1][...]
        delta, m_new, v_new = _adamw_math(w_ref[...], g, m_ref[...], v_ref[...])
        g_out[...] = g
        d_out[...] = delta
        m_out[...] = m_new
        v_out[...] = v_new

    spec = _rows(tr, cols)
    return pl.pallas_call(
        body, name=name, grid=(rows // tr,),
        in_specs=[spec] * (len(grads) + 3), out_specs=[spec] * 4,
        out_shape=[jax.ShapeDtypeStruct((rows, cols), F32)] * 4,
        compiler_params=_params(),
    )(*grads, w, m, v)


def _sum_devices(gathered, name):
    n, rows, cols = gathered.shape
    tr = rows
    while tr * cols * 4 * n > (4 << 20) and tr % 16 == 0:
        tr //= 2

    def body(x_ref, o_ref):
        acc = x_ref[0]
        for j in range(1, n):
            acc = acc + x_ref[j]
        o_ref[...] = acc

    return pl.pallas_call(
        body, name=name, grid=(rows // tr,),
        in_specs=[pl.BlockSpec((n, tr, cols), lambda i: (0, i, 0))], out_specs=_rows(tr, cols),
        out_shape=jax.ShapeDtypeStruct((rows, cols), F32),
        compiler_params=_params(),
    )(gathered)


def _sum_partials(blocked, landeds, chip, name):
    n = len(blocked)
    cols = blocked[0].shape[-1]
    blocked = [b.reshape(N_CHIPS, -1, cols) for b in blocked]
    landeds = [l.reshape(3, -1, cols) for l in landeds]
    rows = blocked[0].shape[1]
    tr = rows
    while tr * cols * 2 * n > (1 << 20) and tr % 32 == 0:
        tr //= 2

    def body(chip_ref, *refs):
        out_ref = refs[-1]
        for li in range(n):
            acc = refs[li][...].astype(F32)
            for p in range(3):
                acc = acc + refs[n + li][p].astype(F32)
            out_ref[li] = acc

    out = pl.pallas_call(
        body, name=name,
        grid_spec=pltpu.PrefetchScalarGridSpec(
            num_scalar_prefetch=1, grid=(rows // tr,),
            in_specs=[pl.BlockSpec((None, tr, cols), lambda i, k: (k[0], i, 0))] * n
            + [pl.BlockSpec((3, tr, cols), lambda i, k: (0, i, 0))] * n,
            out_specs=pl.BlockSpec((n, tr, cols), lambda i, k: (0, i, 0))),
        out_shape=jax.ShapeDtypeStruct((n, rows, cols), F32),
        compiler_params=_params(),
    )(jnp.reshape(chip, (1,)).astype(jnp.int32), *blocked, *landeds)
    return out.reshape(n * rows, cols)


def _my_place():
    return lax.axis_index("x"), lax.axis_index("y"), lax.axis_index("c")


def _other_chips(x, y):
    return [(1 - x, y), (x, 1 - y), (1 - x, 1 - y)]


def _all_gather_small(block, name):
    rows, cols = block.shape

    def body(x_ref, out_ref, send_sems, recv_sems, local_sem):
        x, y, c = _my_place()
        me, sibling = (x, y, c), (x, y, 1 - c)
        chips = _other_chips(x, y)

        def slot(px, py, pc):
            return out_ref.at[4 * px + 2 * py + pc]

        def copy(k, blk, to, src=None):
            return pltpu.make_async_remote_copy(
                src_ref=slot(*blk) if src is None else src, dst_ref=slot(*blk),
                send_sem=send_sems.at[k], recv_sem=recv_sems.at[k], device_id=to, device_id_type=MESH)

        mine = pltpu.make_async_copy(x_ref, slot(*me), local_sem)
        mine.start()
        first = [copy(0, me, sibling, src=x_ref)]
        first += [copy(1 + j, me, (*chip, c), src=x_ref) for j, chip in enumerate(chips)]
        for cp in first:
            cp.start()
        passed = [copy(4 + j, (*chip, c), sibling) for j, chip in enumerate(chips)]
        for j, chip in enumerate(chips):
            copy(1 + j, (*chip, c), me).wait_recv()
            passed[j].start()
        copy(0, sibling, me).wait_recv()
        for j, chip in enumerate(chips):
            copy(4 + j, (*chip, 1 - c), me).wait_recv()
        for cp in first + passed:
            cp.wait_send()
        mine.wait()

    return pl.pallas_call(
        body, name=name,
        out_shape=jax.ShapeDtypeStruct((N_DEV, rows, cols), block.dtype),
        in_specs=[pl.BlockSpec(memory_space=pltpu.VMEM)],
        out_specs=pl.BlockSpec(memory_space=pltpu.VMEM),
        scratch_shapes=[pltpu.SemaphoreType.DMA((7,)), pltpu.SemaphoreType.DMA((7,)), pltpu.SemaphoreType.DMA],
        compiler_params=_params(),
    )(block)


HBM_SPEC = pl.BlockSpec(memory_space=pltpu.HBM)
SEM_SPEC = pl.BlockSpec(memory_space=pltpu.SEMAPHORE)
DATAFLOW_EFFECT = pltpu.SideEffectType.DATAFLOW_SIDE_EFFECTING


def _same_core_of_other_chips(x, y, c):
    return [(*chip, c) for chip in _other_chips(x, y)]


def _sibling_core(x, y, c):
    return [(x, y, 1 - c)]


def _gather_views(src, land, p, x, y):
    return src, land.at[2 * x + y]


def _scatter_views(src, land, p, x, y):
    peer_chip = (2 * (1 - x) + y, 2 * x + (1 - y), 2 * (1 - x) + (1 - y))[p]
    return src.at[peer_chip], land.at[p]


def _whole_views(src, land, p, x, y):
    return src, land


GATHER_PLAN = (_same_core_of_other_chips, _gather_views, 3)
SCATTER_PLAN = (_same_core_of_other_chips, _scatter_views, 3)
SIBLING_PLAN = (_sibling_core, _whole_views, 1)


def _exchange_copies(srcs, lands, send_sems, recv_sems, plan):
    peers_of, views, n_peers = plan
    x, y, c = _my_place()
    copies = []
    for j, (src, land) in enumerate(zip(srcs, lands)):
        for p, peer in enumerate(peers_of(x, y, c)):
            s_view, d_view = views(src, land, p, x, y)
            k = n_peers * j + p
            copies.append(pltpu.make_async_remote_copy(
                src_ref=s_view, dst_ref=d_view, send_sem=send_sems.at[k], recv_sem=recv_sems.at[k],
                device_id=peer, device_id_type=MESH))
    return copies


def _exchange_start(srcs, lands, plan, name):
    n = len(srcs)

    def body(*refs):
        send_sems, recv_sems = refs[2 * n], refs[2 * n + 1]
        token = refs[-1]
        for cp in _exchange_copies(refs[:n], refs[n:2 * n], send_sems, recv_sems, plan):
            cp.start()
        token[...] = jnp.zeros(token.shape, token.dtype)

    operands = [pltpu.with_memory_space_constraint(a, pltpu.HBM) for a in (*srcs, *lands)]
    out = pl.pallas_call(
        body, name=name,
        out_shape=(pltpu.SemaphoreType.DMA((plan[2] * n,)), pltpu.SemaphoreType.DMA((plan[2] * n,)),
                   *[pltpu.HBM(a.shape, a.dtype) for a in operands], jax.ShapeDtypeStruct((8, LANES), F32)),
        in_specs=[HBM_SPEC] * (2 * n),
        out_specs=(SEM_SPEC, SEM_SPEC, *[HBM_SPEC] * (2 * n), pl.BlockSpec(memory_space=pltpu.VMEM)),
        input_output_aliases={i: 2 + i for i in range(2 * n)},
        compiler_params=pltpu.CompilerParams(has_side_effects=DATAFLOW_EFFECT),
    )(*operands)
    return out[0], out[1], list(out[2:2 + n]), list(out[2 + n:2 + 2 * n]), out[-1]


def _exchange_wait(send_sems, recv_sems, srcs, lands, plan, after, name):
    n = len(srcs)

    def body(*refs):
        send, recv = refs[2 * n], refs[2 * n + 1]
        for cp in _exchange_copies(refs[:n], refs[n:2 * n], send, recv, plan):
            cp.wait_send()
            cp.wait_recv()

    out = pl.pallas_call(
        body, name=name,
        out_shape=tuple(pltpu.HBM(a.shape, a.dtype) for a in (*srcs, *lands)),
        in_specs=[HBM_SPEC] * (2 * n) + [SEM_SPEC, SEM_SPEC, HBM_SPEC],
        out_specs=tuple([HBM_SPEC] * (2 * n)),
        input_output_aliases={i: i for i in range(2 * n)},
        compiler_params=pltpu.CompilerParams(has_side_effects=DATAFLOW_EFFECT),
    )(*srcs, *lands, send_sems, recv_sems, pltpu.with_memory_space_constraint(after, pltpu.HBM))
    return list(out[:n]), list(out[n:])


def _landing_for_gather(shard, chip):
    land = lax.empty((N_CHIPS, *shard.shape), shard.dtype)
    return lax.dynamic_update_index_in_dim(land, shard, chip, 0)


TILE_ELEMS = SUBLANES * LANES


def _pack(arrays):
    parts = []
    for a in arrays:
        flat = a.reshape(-1).astype(F32)
        pad = (-flat.shape[0]) % TILE_ELEMS
        if pad:
            flat = jnp.concatenate([flat, jnp.zeros((pad,), F32)])
        parts.append(flat.reshape(-1, LANES))
    return jnp.concatenate(parts, axis=0) if len(parts) > 1 else parts[0]


def _unpack(buf, shapes):
    out, r = [], 0
    lead = buf.shape[:-2]
    for shp in shapes:
        size = math.prod(shp)
        nr = -(-size // TILE_ELEMS) * SUBLANES
        flat = buf[..., r:r + nr, :].reshape(*lead, nr * LANES)[..., :size]
        out.append(flat.reshape(*lead, *shp))
        r += nr
    return out


def _chip_cols(a, k, width):
    return lax.dynamic_slice_in_dim(a, k * width, width, axis=a.ndim - 1)


def _across_chips(gathered, c0_only_shape):
    return gathered.reshape(2, 2, 2, *c0_only_shape)[:, :, 0].reshape(N_CHIPS, *c0_only_shape)


def kernel(x, c, ctx, c_ctx, ada_w, ada_b, norm_g, mlp_w1, mlp_w2, pool_w, pool_scale, attn_w_qkv, attn_w_o, attn_q_g, attn_k_g, gm_w_in, gm_ln_g, gm_ln_b, gm_ws, gm_bs, gm_w_out, final_g, loss_target, m_c_ctx, m_ada_w, m_ada_b, m_norm_g, m_mlp_w1, m_mlp_w2, m_pool_w, m_pool_scale, m_attn_w_qkv, m_attn_w_o, m_attn_q_g, m_attn_k_g, m_gm_w_in, m_gm_ln_g, m_gm_ln_b, m_gm_ws, m_gm_bs, m_gm_w_out, m_final_g, v_c_ctx, v_ada_w, v_ada_b, v_norm_g, v_mlp_w1, v_mlp_w2, v_pool_w, v_pool_scale, v_attn_w_qkv, v_attn_w_o, v_attn_q_g, v_attn_k_g, v_gm_w_in, v_gm_ln_g, v_gm_ln_b, v_gm_ws, v_gm_bs, v_gm_w_out, v_final_g):
    seq, d = x.shape[1], x.shape[2]
    n_ctx = ctx.shape[1]
    total = n_ctx + seq
    hd = attn_q_g.shape[-1]
    nh = d // hd
    nkv = nh // 2
    gg, ch = gm_ws.shape[1], gm_ws.shape[-1]
    half = gm_w_out.shape[1] * N_CHIPS
    pgw = pool_w.shape[-1]
    tm = min(256, n_ctx)
    nct = n_ctx // tm
    seg_lens = (n_ctx, seq)

    mx, my, mc = _my_place()
    chip = 2 * mx + my
    me = 4 * mx + 2 * my + mc

    c_rows = jnp.concatenate([c, jnp.zeros((7, d), F32)], axis=0)
    c_gath = _all_gather_small(c_rows, "gather_cond")[:, 0, :]
    c_all = jnp.concatenate([c_gath, c_ctx[None, :], jnp.zeros((7, d), F32)], axis=0)
    ncs = ada_w.shape[-1]
    ada_cols = _ada_fwd(c_all, ada_w, _chip_cols(ada_b, chip, ncs))
    small_shapes = [ada_cols.shape, norm_g.shape, pool_scale.shape, gm_ln_g.shape, gm_ln_b.shape]
    gathered = _all_gather_small(_pack([ada_cols, norm_g, pool_scale, gm_ln_g, gm_ln_b]), "gather_small_params")
    per_chip = _across_chips(gathered, gathered.shape[1:])
    ada_g, ng_g, ps_g, lng_g, lnb_g = _unpack(per_chip, small_shapes)

    def join_last(a):
        return jnp.moveaxis(a, 0, -2).reshape(*a.shape[1:-1], N_CHIPS * a.shape[-1])

    ada_full = join_last(ada_g)
    ng_full = join_last(ng_g)
    ps_full = join_last(ps_g)
    lng_full = join_last(lng_g)
    lnb_full = join_last(lnb_g)
    mod_lat = lax.dynamic_slice_in_dim(ada_full, me, 1, axis=1).reshape(DEPTH, 6, d)
    mod_ctx = ada_full[:, 8].reshape(DEPTH, 6, d)
    mods = jnp.stack([jnp.concatenate([mod_ctx, ng_full], axis=1), jnp.concatenate([mod_lat, ng_full], axis=1)],
                     axis=1)

    weight_groups = [
        [pool_w],
        [mlp_w1[0], mlp_w2[0]],
        [attn_w_qkv[0], attn_w_o[0]],
        [mlp_w1[1], mlp_w2[1], mlp_w1[2], mlp_w2[2], gm_w_in[0], gm_w_out[0], mlp_w1[3], mlp_w2[3]],
    ]
    gathers = [None] * len(weight_groups)

    def gather_start(gi, after):
        shards, _ = lax.optimization_barrier(([w.astype(BF16) for w in weight_groups[gi]], after))
        lands = [_landing_for_gather(s, chip) for s in shards]
        gathers[gi] = _exchange_start(shards, lands, GATHER_PLAN, f"gather_weights_{gi}_start")
        return gathers[gi][4][0:1, 0:1]

    def gathered(gi, after):
        send, recv, srcs, lands, _ = gathers[gi]
        return _exchange_wait(send, recv, srcs, lands, GATHER_PLAN, after, f"gather_weights_{gi}_wait")[1]

    def rows_joined(a):
        return a.reshape(-1, a.shape[-1])

    w1_b, w2_b = [None] * DEPTH, [None] * DEPTH
    gather_start(0, mods)
    pw_land, = gathered(0, ps_full)
    behind_gather_1 = gather_start(1, pw_land)
    pw_f = jnp.transpose(pw_land, (1, 2, 0, 3, 4)).reshape(pool_w.shape[0], pool_w.shape[1], pgw, pgw)

    gains = jnp.concatenate([attn_q_g, attn_k_g, jnp.zeros((6, hd), F32)], axis=0)
    ws_b = gm_ws[0].astype(BF16)
    ws_t = jnp.swapaxes(gm_ws[0], 1, 2).astype(BF16)
    bs_col = gm_bs[0][:, :, None]
    cos, sin = _rope_tables(n_ctx, seq, hd)
    lat = lambda i: mods[i, 1:2]

    hc0 = jnp.concatenate([ctx[0] + behind_gather_1, x[0]], axis=0)
    ha0 = _pool_fwd(hc0, mods[0] + behind_gather_1, pw_f, ps_full, 0, nct=nct, tm=tm, seg_lens=seg_lens)
    w1_b[0], w2_b[0] = gathered(1, ha0)
    mods0 = mods[0] + gather_start(2, w1_b[0])
    hc1, u0, o0 = _mlp_fwd(ha0, mods0, w1_b[0], w2_b[0], 0, nct=nct, tm=tm)
    wqkv_b, wo_land = gathered(2, hc1)
    mods1 = mods[1] + gather_start(3, wqkv_b)
    wo_f = rows_joined(wo_land)
    xa1, qkv, q_r, k_r, v_b = _qkv_fwd(hc1, mods1, wqkv_b, cos, sin, gains, nh=nh, nkv=nkv, nct=nct, tm=tm)
    o_att, lse = _flash_fwd(q_r, k_r, v_b, n_ctx=n_ctx, hd=hd)
    ha1, y1 = _proj_fwd(o_att, wo_f, hc1, mods1, n_ctx=n_ctx, tm=tm)
    w1_b[1], w2_b[1], w1_b[2], w2_b[2], win_b, wout_land, w1_b[3], w2_b[3] = gathered(3, ha1)
    h2, u1, o1 = _mlp_fwd(ha1, lat(1), w1_b[1], w2_b[1], 1, nct=0, tm=tm)
    wout_f = rows_joined(wout_land)
    ha2, zpre, y2 = _gmlp_fwd(h2, mods[2], win_b, lng_full, lnb_full, ws_b, bs_col, wout_f, tm=tm)
    h3, u2, o2 = _mlp_fwd(ha2, lat(2), w1_b[2], w2_b[2], 2, nct=0, tm=tm)
    ha3 = _pool_fwd(h3, lat(3), pw_f, ps_full, 3, nct=0, tm=tm, seg_lens=seg_lens)
    h4, u3, o3 = _mlp_fwd(ha3, lat(3), w1_b[3], w2_b[3], 3, nct=0, tm=tm)
    dh4, fin_acc = _final_loss(h4, loss_target[0], final_g[None, :], tm=tm)

    dmods = [None] * DEPTH
    scatters = [None] * (DEPTH + 1)

    def blocked_rows(g):
        return g.reshape(N_CHIPS, g.shape[1] // N_CHIPS, g.shape[2])

    def blocked_pool(dpw):
        pg = dpw.shape[0]
        return jnp.transpose(dpw.astype(BF16).reshape(pg, N_CHIPS, pgw // N_CHIPS, pgw), (1, 0, 2, 3))

    def scatter_start(i, grads):
        lands = [lax.empty((3, *g.shape[1:]), g.dtype) for g in grads]
        scatters[i] = _exchange_start(grads, lands, SCATTER_PLAN, f"scatter_grads_{i}_start")
        return scatters[i][4][0:1, 0:1]

    def mlp_back(i, h_in, dh_out, u, o, md, n_ct):
        dh_in, du, dob, mb, dmd = _mlp_bwd(h_in, dh_out, u, o, md, w1_b[i], w2_b[i], i, nct=n_ct, tm=tm)
        dw1 = _mm_tn(mb, du, f"mlp_dw1_{i}", col_blocks=N_CHIPS)
        dw2 = blocked_rows(_mm_tn(u, dob, f"mlp_dw2_{i}", relu2=True))
        return dh_in, dmd, [dw1, dw2]

    def pool_back(i, h_in, dh_out, md, n_ct):
        dp, dmd_a, dps, dpw = _pool_bwd_weights(h_in, dh_out, md, pw_f, ps_full, i, nct=n_ct, tm=tm,
                                                seg_lens=seg_lens)
        dh_in, dmd_b = _pool_bwd_input(dp, h_in, dh_out, md, i, nct=n_ct, tm=tm, seg_lens=seg_lens, gw=pgw)
        return dh_in, dmd_a + dmd_b, dps, dpw

    zero_grp = jnp.zeros((1, 8, d), F32)
    dha3, dmd3, dws3 = mlp_back(3, ha3, dh4, u3, o3, lat(3), 0)
    dh3, dmd3p, dps3, dpw3 = pool_back(3, h3, dha3, lat(3), 0)
    dmods[3] = jnp.concatenate([zero_grp, dmd3 + dmd3p], axis=0)
    tok = scatter_start(3, dws3 + [blocked_pool(dpw3)])
    dha2, dmd2, dws2 = mlp_back(2, ha2, dh3, u2, o2, lat(2) + tok, 0)
    dh2, dzpre, gated, dyb2, ab2, dmd2g, dln, dws, dbs = _gmlp_bwd(
        h2, dha2, zpre, y2, mods[2], win_b, lng_full, lnb_full, ws_b, ws_t, bs_col, wout_f, tm=tm)
    dwin = _mm_tn(ab2, dzpre, "gmlp_dw_in", col_blocks=N_CHIPS)
    dwout = blocked_rows(_mm_tn(gated, dyb2, "gmlp_dw_out"))
    dmods[2] = jnp.concatenate([zero_grp, dmd2 + dmd2g], axis=0)
    tok = scatter_start(2, dws2 + [dwin, dwout])
    dha1, dmd1, dws1 = mlp_back(1, ha1, dh2, u1, o1, lat(1) + tok, 0)
    do_att, dyb1, dmd1p = _proj_bwd(dha1, y1, mods[1], wo_f, tm=tm)
    dwo = blocked_rows(_mm_tn(o_att, dyb1, "attn_dw_o"))
    dq, dk, dv = _flash_bwd(q_r, k_r, v_b, o_att, do_att, lse, n_ctx=n_ctx, hd=hd)
    dqkv, dgains = _qkv_bwd(qkv, dq, dk, dv, cos, sin, gains, nh=nh, nkv=nkv, nct=nct, tm=tm)
    dwqkv = _mm_tn(xa1, dqkv, "attn_dw_qkv", col_blocks=N_CHIPS)
    dhc1, dmd1i = _attn_in_bwd(dqkv, wqkv_b, hc1, dha1, mods[1], nct=nct, tm=tm)
    dmods[1] = dmd1i + jnp.concatenate([zero_grp, dmd1 + dmd1p], axis=0)
    tok = scatter_start(1, dws1 + [dwqkv, dwo])
    dha0, dmd0, dws0 = mlp_back(0, ha0, dhc1, u0, o0, mods[0] + tok, nct)
    tok = scatter_start(0, dws0)
    dhc0, dmd0p, dps0, dpw0 = pool_back(0, hc0, dha0, mods[0] + tok, nct)
    dmods[0] = dmd0 + dmd0p
    grad_x = dhc0[None]
    scatter_start(DEPTH, [blocked_pool(dpw0)])

    dmods_all = jnp.stack(dmods, axis=0)
    small_grads = [dmods_all, dws, dbs, dgains, dln, dps0, dps3, fin_acc]
    sg_shapes = [a.shape for a in small_grads]
    sg_gath = _all_gather_small(_pack(small_grads), "gather_small_grads")
    sg_sum = _sum_devices(sg_gath, "sum_small_grads")
    s_dmods, s_dws, s_dbs, s_dgains, s_dln, s_dps0, s_dps3, s_fin = _unpack(sg_sum, sg_shapes)
    loss = s_fin[1, 0]

    sources, landed = [None] * len(scatters), [None] * len(scatters)
    for i in (3, 2, 1, 0, DEPTH):
        send, recv, srcs, lands, _ = scatters[i]
        sources[i], landed[i] = _exchange_wait(send, recv, srcs, lands, SCATTER_PLAN, sg_sum, f"scatter_grads_{i}_wait")

    def summed(name, picks):
        return _sum_partials([sources[i][j] for i, j in picks], [landed[i][j] for i, j in picks], chip,
                             f"sum_chips_{name}")

    big = [("mlp_w1", mlp_w1, m_mlp_w1, v_mlp_w1, [(i, 0) for i in range(DEPTH)]),
           ("mlp_w2", mlp_w2, m_mlp_w2, v_mlp_w2, [(i, 1) for i in range(DEPTH)]),
           ("pool_w", pool_w, m_pool_w, v_pool_w, [(DEPTH, 0), (3, 2)]),
           ("attn_w_qkv", attn_w_qkv, m_attn_w_qkv, v_attn_w_qkv, [(1, 2)]),
           ("attn_w_o", attn_w_o, m_attn_w_o, v_attn_w_o, [(1, 3)]),
           ("gm_w_in", gm_w_in, m_gm_w_in, v_gm_w_in, [(2, 2)]),
           ("gm_w_out", gm_w_out, m_gm_w_out, v_gm_w_out, [(2, 3)])]
    partial = [summed(name, picks) for name, _, _, _, picks in big]
    swap = _exchange_start(partial, [lax.empty(p.shape, p.dtype) for p in partial], SIBLING_PLAN,
                           "swap_with_sibling_start")
    behind_swap = swap[4][0:1, 0:1]

    dm_dev = _unpack(sg_gath, sg_shapes[:1])[0]
    dm_lat = jnp.moveaxis(dm_dev[:, :, 1, :6, :], 0, 1).reshape(DEPTH, N_DEV, 6 * d)
    dm_ctx = jnp.moveaxis(dm_dev[:, :, 0, :6, :], 0, 1).reshape(DEPTH, N_DEV, 6 * d)
    dmod16 = _chip_cols(jnp.concatenate([dm_lat, dm_ctx], axis=1), chip, ncs) + behind_swap
    g_ada_w, dcc_part = _ada_bwd(c_all, c_all.T, dmod16, ada_w)
    dcc_gath = _all_gather_small(dcc_part, "gather_d_c_ctx")
    dcc_chips = _across_chips(dcc_gath, dcc_gath.shape[1:])
    dcc_rows = _sum_devices(dcc_chips, "sum_d_c_ctx")
    dcc = dcc_rows[0]
    ada_res = _adamw(g_ada_w.reshape(-1, ncs), None, ada_w.reshape(-1, ncs),
                     m_ada_w.reshape(-1, ncs), v_ada_w.reshape(-1, ncs), "adamw_ada_w")

    partial, from_sibling = _exchange_wait(swap[0], swap[1], swap[2], swap[3], SIBLING_PLAN, ada_res[1],
                                           "swap_with_sibling_wait")
    big_out = {}
    for (name, w, m, v, _), mine, theirs in zip(big, partial, from_sibling):
        cols = w.shape[-1]
        res = _adamw(mine, theirs, w.reshape(-1, cols), m.reshape(-1, cols), v.reshape(-1, cols), f"adamw_{name}")
        big_out[name] = [r.reshape(w.shape) for r in res]
    big_out["ada_w"] = [r.reshape(ada_w.shape) for r in ada_res]

    def cols_of(a, width):
        return _chip_cols(a, chip, width)

    zero = lambda a: jnp.zeros(a.shape, F32)
    ngw = norm_g.shape[-1]
    small = {
        "c_ctx": (dcc, zero(dcc), c_ctx, m_c_ctx, v_c_ctx),
        "ada_b": (s_dmods[:, 0, :6].reshape(DEPTH, 6 * d), s_dmods[:, 1, :6].reshape(DEPTH, 6 * d), ada_b, m_ada_b,
                  v_ada_b),
        "norm_g": (cols_of(s_dmods[:, 0, 6:8], ngw), cols_of(s_dmods[:, 1, 6:8], ngw), norm_g, m_norm_g, v_norm_g),
        "pool_scale": (cols_of(jnp.stack([s_dps0[0], s_dps3[0]]), pool_scale.shape[-1]), zero(pool_scale),
                       pool_scale, m_pool_scale, v_pool_scale),
        "attn_q_g": (s_dgains[0:1], zero(attn_q_g), attn_q_g, m_attn_q_g, v_attn_q_g),
        "attn_k_g": (s_dgains[1:2], zero(attn_k_g), attn_k_g, m_attn_k_g, v_attn_k_g),
        "gm_ln_g": (cols_of(s_dln[0:1], gm_ln_g.shape[-1]), zero(gm_ln_g), gm_ln_g, m_gm_ln_g, v_gm_ln_g),
        "gm_ln_b": (cols_of(s_dln[1:2], gm_ln_b.shape[-1]), zero(gm_ln_b), gm_ln_b, m_gm_ln_b, v_gm_ln_b),
        "gm_ws": (s_dws[None], zero(gm_ws), gm_ws, m_gm_ws, v_gm_ws),
        "gm_bs": (s_dbs[None, :, :, 0], zero(gm_bs), gm_bs, m_gm_bs, v_gm_bs),
        "final_g": (s_fin[0], zero(final_g), final_g, m_final_g, v_final_g),
    }
    keys = list(small)
    packed = [_pack([small[k][t] for k in keys]) for t in range(5)]
    res = _adamw(*packed, "adamw_small")
    shapes = [small[k][2].shape for k in keys]
    small_out = {k: [] for k in keys}
    for r in res:
        for k, a in zip(keys, _unpack(r, shapes)):
            small_out[k].append(a)

    order = ["c_ctx", "ada_w", "ada_b", "norm_g", "mlp_w1", "mlp_w2", "pool_w", "pool_scale", "attn_w_qkv",
             "attn_w_o", "attn_q_g", "attn_k_g", "gm_w_in", "gm_ln_g", "gm_ln_b", "gm_ws", "gm_bs", "gm_w_out",
             "final_g"]
    allo = {**big_out, **small_out}
    outs = [loss, grad_x]
    for t in range(4):
        outs += [allo[k][t] for k in order]
    return tuple(outs)
```

```python
import functools
import math

import numpy as np
import jax
import jax.numpy as jnp
from jax import lax
from jax.experimental import pallas as pl
from jax.experimental.pallas import tpu as pltpu

F32 = jnp.float32
BF16 = jnp.bfloat16
MESH = pl.DeviceIdType.MESH

EPS = 1e-6
GRID_W = 64
ROPE_BASE = 10000.0
POOL_WINDOWS = (2, 4, 8, 16)
HALO = 8
DEPTH = 4
N_MIXERS = 3

ADAM_LR = 0.001
ADAM_B1 = 0.9
ADAM_B2 = 0.999
ADAM_EPS = 1e-08
ADAM_WD = 0.01
ADAM_STEP = 10

VMEM_LIMIT_BYTES = 56 * 1024 * 1024
LANES = 128
SUBLANES = 8
N_DEV = 8
N_CHIPS = 4

SH1, SC1, G1, SH2, SC2, G2, NG0, NG1 = range(8)


def _dot(a, b):
    return jnp.dot(a, b, preferred_element_type=F32)


def _dot_nt(a, b):
    return lax.dot_general(a, b, (((1,), (1,)), ((), ())), preferred_element_type=F32)


def _dot_tn(a, b):
    return lax.dot_general(a, b, (((0,), (0,)), ((), ())), preferred_element_type=F32)


def _dot_blocks(a, w_ref):
    return jnp.concatenate([_dot(a, w_ref[k]) for k in range(w_ref.shape[0])], axis=1)


def _dot_nt_blocks(a, w_ref):
    nb, _, w = w_ref.shape
    acc = _dot_nt(a[:, 0:w], w_ref[0])
    for k in range(1, nb):
        acc = acc + _dot_nt(a[:, k * w:(k + 1) * w], w_ref[k])
    return acc


def _params(**kw):
    return pltpu.CompilerParams(vmem_limit_bytes=VMEM_LIMIT_BYTES, **kw)


def _full(shape):
    nd = len(shape)
    return pl.BlockSpec(shape, lambda *_: (0,) * nd)


def _rows(tm, width):
    return pl.BlockSpec((tm, width), lambda i: (i, 0))


def _group_of(nct, groups):
    if groups == 1:
        return lambda i: 0
    return lambda i: jnp.where(i >= nct, 1, 0)


def _mods_spec(nct, groups, d):
    grp = _group_of(nct, groups)
    return pl.BlockSpec((None, 8, d), lambda i: (grp(i), 0, 0))


def _first_of_group(i, nct, groups):
    if groups == 1:
        return i == 0
    return jnp.logical_or(i == 0, i == nct)


def _rowsum(v):
    return jnp.sum(v, axis=0, keepdims=True)


def _rms_parts(x):
    r = lax.rsqrt(jnp.mean(x * x, axis=-1, keepdims=True) + EPS)
    return x * r, r


def _normmod(x, md, which):
    ng, sh, sc = (md[NG0:NG0 + 1], md[SH1:SH1 + 1], md[SC1:SC1 + 1]) if which == 0 else (
        md[NG1:NG1 + 1], md[SH2:SH2 + 1], md[SC2:SC2 + 1])
    xhat, r = _rms_parts(x)
    n = xhat * ng
    return n * (1.0 + sc) + sh, (xhat, r, n)


def _normmod_bwd(da, parts, md, which):
    xhat, r, n = parts
    ng, sc = (md[NG0:NG0 + 1], md[SC1:SC1 + 1]) if which == 0 else (md[NG1:NG1 + 1], md[SC2:SC2 + 1])
    dsh = _rowsum(da)
    dsc = _rowsum(da * n)
    dn = da * (1.0 + sc)
    dng = _rowsum(dn * xhat)
    dxhat = dn * ng
    dx = r * (dxhat - xhat * jnp.mean(dxhat * xhat, axis=-1, keepdims=True))
    return dx, dsh, dsc, dng


def _acc_rows(ref, first, rows):
    @pl.when(first)
    def _():
        ref[...] = jnp.zeros(ref.shape, ref.dtype)

    for r, v in rows.items():
        ref[r:r + 1, :] += v


def _shift_up(x, k):
    if k == 0:
        return x
    return pltpu.roll(x, x.shape[0] - k, axis=0)


def _gelu(x):
    k = math.sqrt(2.0 / math.pi)
    return 0.5 * x * (1.0 + jnp.tanh(k * (x + 0.044715 * x * x * x)))


def _gelu_grad(x):
    k = math.sqrt(2.0 / math.pi)
    t = jnp.tanh(k * (x + 0.044715 * x * x * x))
    return 0.5 * (1.0 + t) + 0.5 * x * (1.0 - t * t) * k * (1.0 + 3.0 * 0.044715 * x * x)


def _silu(x):
    return x / (1.0 + jnp.exp(-x))


def _silu_grad(x):
    s = 1.0 / (1.0 + jnp.exp(-x))
    return s * (1.0 + x * (1.0 - s))


def _mlp_fwd(h, mods, w1, w2, layer, *, nct, tm):
    rows, d = h.shape
    groups = mods.shape[0]
    nb, _, fc = w1.shape
    ff = nb * fc

    def body(h_ref, md_ref, w1_ref, w2_ref, h2_ref, u_ref, o_ref):
        x = h_ref[...]
        md = md_ref[...]
        m, _ = _normmod(x, md, 1)
        mb = m.astype(BF16)
        acc = jnp.zeros((tm, d), F32)
        for k in range(nb):
            u = _dot(mb, w1_ref[k])
            u_ref[:, k * fc:(k + 1) * fc] = u.astype(BF16)
            acc = acc + _dot(jnp.square(jnp.maximum(u, 0.0)).astype(BF16), w2_ref[k])
        o_ref[...] = acc.astype(BF16)
        h2_ref[...] = x + md[G2:G2 + 1] * acc

    return pl.pallas_call(
        body, name=f"mlp_fwd_{layer}", grid=(rows // tm,),
        in_specs=[_rows(tm, d), _mods_spec(nct, groups, d), _full(w1.shape), _full(w2.shape)],
        out_specs=[_rows(tm, d), _rows(tm, ff), _rows(tm, d)],
        out_shape=[jax.ShapeDtypeStruct((rows, d), F32), jax.ShapeDtypeStruct((rows, ff), BF16),
                   jax.ShapeDtypeStruct((rows, d), BF16)],
        compiler_params=_params(),
    )(h, mods, w1, w2)


def _mlp_bwd(h1, dh2, u, o, mods, w1, w2, layer, *, nct, tm):
    rows, d = h1.shape
    groups = mods.shape[0]
    nb, _, fc = w1.shape
    ff = nb * fc

    def body(h_ref, g_ref, u_ref, o_ref, md_ref, w1_ref, w2_ref, dh_ref, du_ref, dob_ref, mb_ref, dmd_ref):
        i = pl.program_id(0)
        x = h_ref[...]
        g = g_ref[...]
        md = md_ref[...]
        m, parts = _normmod(x, md, 1)
        mb_ref[...] = m.astype(BF16)
        dg2 = _rowsum(g * o_ref[...].astype(F32))
        dob = (g * md[G2:G2 + 1]).astype(BF16)
        dob_ref[...] = dob
        dm = jnp.zeros((tm, d), F32)
        for k in range(nb):
            uk = u_ref[:, k * fc:(k + 1) * fc].astype(F32)
            dr = _dot_nt(dob, w2_ref[k])
            duk = (dr * (2.0 * jnp.maximum(uk, 0.0))).astype(BF16)
            du_ref[:, k * fc:(k + 1) * fc] = duk
            dm = dm + _dot_nt(duk, w1_ref[k])
        dx, dsh, dsc, dng = _normmod_bwd(dm, parts, md, 1)
        dh_ref[...] = g + dx
        _acc_rows(dmd_ref, _first_of_group(i, nct, groups), {SH2: dsh, SC2: dsc, G2: dg2, NG1: dng})

    return pl.pallas_call(
        body, name=f"mlp_bwd_{layer}", grid=(rows // tm,),
        in_specs=[_rows(tm, d), _rows(tm, d), _rows(tm, ff), _rows(tm, d), _mods_spec(nct, groups, d),
                  _full(w1.shape), _full(w2.shape)],
        out_specs=[_rows(tm, d), _rows(tm, ff), _rows(tm, d), _rows(tm, d), _mods_spec(nct, groups, d)],
        out_shape=[jax.ShapeDtypeStruct((rows, d), F32), jax.ShapeDtypeStruct((rows, ff), BF16),
                   jax.ShapeDtypeStruct((rows, d), BF16), jax.ShapeDtypeStruct((rows, d), BF16),
                   jax.ShapeDtypeStruct((groups, 8, d), F32)],
        compiler_params=_params(),
    )(h1, dh2, u, o, mods, w1, w2)


def _div_tile(n, cap):
    if n <= cap:
        return n
    return max(t for t in range(LANES, cap + 1, LANES) if n % t == 0)


DW_TOKEN_TILE_CAP = 4224


def _mm_tn(a, b, name, *, relu2=False, col_blocks=1):
    rows, m = a.shape
    n = b.shape[1]
    tmm = min(m, 1024)
    tn = min(n // col_blocks, 2048)
    per_block = n // col_blocks // tn
    tr = _div_tile(rows, DW_TOKEN_TILE_CAP)

    def body(a_ref, b_ref, o_ref, acc_ref):
        r = pl.program_id(2)

        @pl.when(r == 0)
        def _():
            acc_ref[...] = jnp.zeros(acc_ref.shape, F32)

        av = a_ref[...]
        if relu2:
            av = jnp.square(jnp.maximum(av.astype(F32), 0.0)).astype(BF16)
        acc_ref[...] += _dot_tn(av, b_ref[...])

        @pl.when(r == pl.num_programs(2) - 1)
        def _():
            o_ref[...] = acc_ref[...].astype(BF16)

    return pl.pallas_call(
        body, name=name, grid=(m // tmm, n // tn, rows // tr),
        in_specs=[pl.BlockSpec((tr, tmm), lambda i, j, r: (r, i)), pl.BlockSpec((tr, tn), lambda i, j, r: (r, j))],
        out_specs=pl.BlockSpec((None, tmm, tn), lambda i, j, r: (j // per_block, i, j % per_block)),
        out_shape=jax.ShapeDtypeStruct((col_blocks, m, n // col_blocks), BF16),
        scratch_shapes=[pltpu.VMEM((tmm, tn), F32)],
        compiler_params=_params(),
    )(a, b)


def _halo_specs(tm, d, rows):
    per = tm // HALO
    prev = pl.BlockSpec((HALO, d), lambda i: (jnp.maximum(i * per - 1, 0), 0))
    nxt = pl.BlockSpec((HALO, d), lambda i: (jnp.minimum((i + 1) * per, rows // HALO - 1), 0))
    return prev, _rows(tm, d), nxt


def _segment_positions(i, tm, nct, groups, seg_lens):
    if groups == 1:
        start, length = 0, seg_lens[-1]
    else:
        start = jnp.where(i >= nct, nct, 0)
        length = jnp.where(i >= nct, seg_lens[1], seg_lens[0])
    rid = lax.broadcasted_iota(jnp.int32, (tm + 2 * HALO, 1), 0)
    pos = (i - start) * tm - HALO + rid
    return pos, length


def _window_count(pos, length, w):
    hi = jnp.minimum(pos + (w - w // 2), length)
    lo = jnp.maximum(pos - w // 2, 0)
    return (hi - lo).astype(F32)


def _window_sum(xg, w, lead):
    b, k = xg, 1
    while k < w:
        b = b + _shift_up(b, k)
        k *= 2
    return _shift_up(b, HALO - lead)[0:xg.shape[0] - 2 * HALO]


def _pooled(ext, md, pos, length, gw):
    tm = ext.shape[0] - 2 * HALO
    a_ext, parts = _normmod(ext, md, 0)
    valid = jnp.logical_and(pos >= 0, pos < length)
    a_ext = jnp.where(valid, a_ext, 0.0)
    pos_c = pos[HALO:HALO + tm]
    ps = []
    for g, w in enumerate(POOL_WINDOWS):
        xg = a_ext[:, g * gw:(g + 1) * gw]
        s = _window_sum(xg, w, w // 2)
        ps.append(s * (1.0 / _window_count(pos_c, length, w)) - xg[HALO:HALO + tm])
    return ps, parts


def _pool_fwd(h, mods, pw, pscale, layer, *, nct, tm, seg_lens):
    rows, d = h.shape
    groups = mods.shape[0]
    pg, gw = pw.shape[1], pw.shape[-1]

    def body(prev_ref, cur_ref, next_ref, md_ref, pw_ref, ps_ref, out_ref, p_ref):
        i = pl.program_id(0)
        md = md_ref[...]
        cur = cur_ref[...]
        ext = jnp.concatenate([prev_ref[...], cur, next_ref[...]], axis=0)
        pos, length = _segment_positions(i, tm, nct, groups, seg_lens)
        ps, _ = _pooled(ext, md, pos, length, gw)
        for g in range(pg):
            pb = ps[g].astype(BF16)
            p_ref[:, g * gw:(g + 1) * gw] = pb
            yg = _dot(pb, pw_ref[g]) * ps_ref[:, g * gw:(g + 1) * gw]
            out_ref[:, g * gw:(g + 1) * gw] = cur[:, g * gw:(g + 1) * gw] + md[G1:G1 + 1, g * gw:(g + 1) * gw] * yg

    j = layer // N_MIXERS
    return pl.pallas_call(
        body, name=f"pool_fwd_{layer}", grid=(rows // tm,),
        in_specs=[*_halo_specs(tm, d, rows), _mods_spec(nct, groups, d),
                  pl.BlockSpec((None, pg, gw, gw), lambda i: (j, 0, 0, 0)), _full((1, d))],
        out_specs=[_rows(tm, d), _rows(tm, d)],
        out_shape=[jax.ShapeDtypeStruct((rows, d), F32), jax.ShapeDtypeStruct((rows, d), BF16)],
        compiler_params=_params(),
    )(h, h, h, mods, pw, pscale[j:j + 1])


def _pool_bwd_weights(p, dh1, mods, pw, pscale, layer, *, nct, tm):
    rows, d = p.shape
    groups = mods.shape[0]
    pg, gw = pw.shape[1], pw.shape[-1]

    def body(p_ref, g_ref, md_ref, pw_ref, ps_ref, dp_ref, dmd_ref, dps_ref, dpw_ref):
        i = pl.program_id(0)
        md = md_ref[...]
        gup = g_ref[...]

        @pl.when(i == 0)
        def _():
            dps_ref[...] = jnp.zeros(dps_ref.shape, F32)
            dpw_ref[...] = jnp.zeros(dpw_ref.shape, F32)

        dg1 = []
        for g in range(pg):
            cols = slice(g * gw, (g + 1) * gw)
            pb = p_ref[:, cols]
            yp = _dot(pb, pw_ref[g])
            sc = ps_ref[:, cols]
            dg1.append(_rowsum(gup[:, cols] * (yp * sc)))
            dy = gup[:, cols] * md[G1:G1 + 1, cols]
            dps_ref[0:1, cols] += _rowsum(dy * yp)
            dyp = (dy * sc).astype(BF16)
            dp_ref[:, cols] = _dot_nt(dyp, pw_ref[g])
            dpw_ref[g] += _dot_tn(pb, dyp)
        _acc_rows(dmd_ref, _first_of_group(i, nct, groups), {G1: jnp.concatenate(dg1, axis=1)})

    j = layer // N_MIXERS
    return pl.pallas_call(
        body, name=f"pool_bwd_w_{layer}", grid=(rows // tm,),
        in_specs=[_rows(tm, d), _rows(tm, d), _mods_spec(nct, groups, d),
                  pl.BlockSpec((None, pg, gw, gw), lambda i: (j, 0, 0, 0)), _full((1, d))],
        out_specs=[_rows(tm, d), _mods_spec(nct, groups, d), _full((8, d)), _full((pg, gw, gw))],
        out_shape=[jax.ShapeDtypeStruct((rows, d), F32), jax.ShapeDtypeStruct((groups, 8, d), F32),
                   jax.ShapeDtypeStruct((8, d), F32), jax.ShapeDtypeStruct((pg, gw, gw), F32)],
        compiler_params=_params(),
    )(p, dh1, mods, pw, pscale[j:j + 1])


def _pool_bwd_input(dp, h, dh1, mods, layer, *, nct, tm, seg_lens, gw):
    rows, d = h.shape
    groups = mods.shape[0]

    def body(prev_ref, cur_ref, next_ref, h_ref, g_ref, md_ref, dh_ref, dmd_ref):
        i = pl.program_id(0)
        md = md_ref[...]
        dp_cur = cur_ref[...]
        ext = jnp.concatenate([prev_ref[...], dp_cur, next_ref[...]], axis=0)
        pos, length = _segment_positions(i, tm, nct, groups, seg_lens)
        valid = jnp.logical_and(pos >= 0, pos < length)
        das = []
        for g, w in enumerate(POOL_WINDOWS):
            cols = slice(g * gw, (g + 1) * gw)
            q = jnp.where(valid, ext[:, cols] * (1.0 / jnp.maximum(_window_count(pos, length, w), 1.0)), 0.0)
            das.append(_window_sum(q, w, w // 2 - 1) - dp_cur[:, cols])
        da = jnp.concatenate(das, axis=1)
        _, parts = _normmod(h_ref[...], md, 0)
        dx, dsh, dsc, dng = _normmod_bwd(da, parts, md, 0)
        dh_ref[...] = g_ref[...] + dx
        _acc_rows(dmd_ref, _first_of_group(i, nct, groups), {SH1: dsh, SC1: dsc, NG0: dng})

    return pl.pallas_call(
        body, name=f"pool_bwd_x_{layer}", grid=(rows // tm,),
        in_specs=[*_halo_specs(tm, d, rows), _rows(tm, d), _rows(tm, d), _mods_spec(nct, groups, d)],
        out_specs=[pl.BlockSpec((tm, d), lambda i: (jnp.maximum(i - nct, 0), 0)), _mods_spec(nct, groups, d)],
        out_shape=[jax.ShapeDtypeStruct((rows - nct * tm, d), F32), jax.ShapeDtypeStruct((groups, 8, d), F32)],
        compiler_params=_params(),
    )(dp, dp, dp, h, dh1, mods)


def _rope_tables(n_ctx, seq, hd):
    half = hd // 2
    n_rows = seq // GRID_W
    inv = np.float32(ROPE_BASE) ** (-np.arange(0, half, 2, dtype=np.float32) / np.float32(half))
    ar = np.arange(n_rows, dtype=np.float32)[:, None] * inv[None, :]
    ac = np.arange(GRID_W, dtype=np.float32)[:, None] * inv[None, :]

    def over_tokens(row_part, col_part):
        r = jnp.repeat(jnp.asarray(row_part, F32), GRID_W, axis=0)
        c = jnp.tile(jnp.asarray(col_part, F32), (n_rows, 1))
        return r, c

    cr, cc = over_tokens(np.cos(ar), np.cos(ac))
    sr, sc = over_tokens(np.sin(ar), np.sin(ac))
    cos = jnp.concatenate([cr, cr, cc, cc], axis=1)
    sin = jnp.concatenate([-sr, sr, -sc, sc], axis=1)
    cos = jnp.concatenate([jnp.ones((n_ctx, hd), F32), cos], axis=0)
    sin = jnp.concatenate([jnp.zeros((n_ctx, hd), F32), sin], axis=0)
    return cos, sin


def _rope_partner(x):
    hd = x.shape[-1]
    q = hd // 4
    lane = lax.broadcasted_iota(jnp.int32, x.shape, 1)
    first = (lane % (2 * q)) < q
    return jnp.where(first, pltpu.roll(x, hd - q, axis=1), pltpu.roll(x, q, axis=1))


def _qkv_fwd(h, mods, wqkv, cos, sin, gains, *, nh, nkv, nct, tm):
    rows, d = h.shape
    qw = wqkv.shape[0] * wqkv.shape[-1]
    hd = cos.shape[-1]

    def body(h_ref, md_ref, w_ref, cos_ref, sin_ref, gn_ref, xa_ref, qkv_ref, q_ref, k_ref, v_ref):
        a, _ = _normmod(h_ref[...], md_ref[...], 0)
        xa = a.astype(BF16)
        xa_ref[...] = xa
        qkv = _dot_blocks(xa, w_ref)
        qkv_ref[...] = qkv
        c, s = cos_ref[...], sin_ref[...]
        for hh in range(nh + nkv):
            xh = qkv[:, hh * hd:(hh + 1) * hd]
            xhat, _ = _rms_parts(xh)
            y = xhat * (gn_ref[0:1, :] if hh < nh else gn_ref[1:2, :])
            rot = (y * c + _rope_partner(y) * s).astype(BF16)
            if hh < nh:
                q_ref[:, hh * hd:(hh + 1) * hd] = rot
            else:
                k_ref[:, (hh - nh) * hd:(hh - nh + 1) * hd] = rot
        v_ref[...] = qkv[:, (nh + nkv) * hd:].astype(BF16)

    return pl.pallas_call(
        body, name="attn_qkv_fwd", grid=(rows // tm,),
        in_specs=[_rows(tm, d), _mods_spec(nct, 2, d), _full(wqkv.shape), _rows(tm, hd), _rows(tm, hd),
                  _full((8, hd))],
        out_specs=[_rows(tm, d), _rows(tm, qw), pl.BlockSpec((tm, nh * hd), lambda i: (jnp.maximum(i - nct, 0), 0)),
                   _rows(tm, nkv * hd), _rows(tm, nkv * hd)],
        out_shape=[jax.ShapeDtypeStruct((rows, d), BF16), jax.ShapeDtypeStruct((rows, qw), F32),
                   jax.ShapeDtypeStruct((rows - nct * tm, nh * hd), BF16),
                   jax.ShapeDtypeStruct((rows, nkv * hd), BF16), jax.ShapeDtypeStruct((rows, nkv * hd), BF16)],
        compiler_params=_params(),
    )(h, mods, wqkv, cos, sin, gains)


ATTN_Q_TILE_CAP = 1024
ATTN_KV_TILE_CAP = 4224
ATTN_ROW_GROUP = 256
LOG2E = 1.4426950408889634


def _attn_tiles(seq, total):
    tq = _div_tile(seq, ATTN_Q_TILE_CAP)
    return tq, _div_tile(total, ATTN_KV_TILE_CAP), min(ATTN_ROW_GROUP, tq)


def _flash_fwd(q, k, v, *, n_ctx, hd):
    total = k.shape[0]
    seq = total - n_ctx
    nkv = k.shape[1] // hd
    tq, tk, rg = _attn_tiles(seq, total)
    nk = total // tk
    scale = hd ** -0.5
    c2 = scale * LOG2E

    def body(q_ref, k_ref, v_ref, o_ref, lse_ref, m_sc, l_sc, acc_sc):
        ki = pl.program_id(2)

        @pl.when(ki == 0)
        def _():
            m_sc[...] = jnp.full(m_sc.shape, -jnp.inf, F32)
            l_sc[...] = jnp.zeros(l_sc.shape, F32)
            acc_sc[...] = jnp.zeros(acc_sc.shape, F32)

        kk, vv = k_ref[...], v_ref[...]
        groups = [(g, sub) for g in range(2) for sub in range(tq // rg)]

        def scores(g, sub):
            return _dot_nt(q_ref[sub * rg:(sub + 1) * rg, g * hd:(g + 1) * hd], kk)

        s_next = scores(*groups[0])
        for idx, (g, sub) in enumerate(groups):
            s = s_next
            if idx + 1 < len(groups):
                s_next = scores(*groups[idx + 1])
            rows = slice(g * tq + sub * rg, g * tq + (sub + 1) * rg)
            m_old = m_sc[rows]
            m_new = jnp.maximum(m_old, jnp.max(s, axis=-1, keepdims=True))
            alpha = jnp.exp2((m_old - m_new) * c2)
            p = jnp.exp2((s - m_new) * c2)
            l_sc[rows] = alpha * l_sc[rows] + jnp.sum(p, axis=-1, keepdims=True)
            acc_sc[rows] = alpha * acc_sc[rows] + _dot(p.astype(BF16), vv)
            m_sc[rows] = m_new

        @pl.when(ki == nk - 1)
        def _():
            o2 = acc_sc[...] / l_sc[...]
            lse = m_sc[...] * scale + jnp.log(l_sc[...])
            o_ref[:, :hd] = o2[:tq].astype(BF16)
            o_ref[:, hd:] = o2[tq:].astype(BF16)
            lse_ref[:, 0:1] = lse[:tq]
            lse_ref[:, 1:2] = lse[tq:]

    return pl.pallas_call(
        body, name="attn_flash_fwd", grid=(nkv, seq // tq, nk),
        in_specs=[pl.BlockSpec((tq, 2 * hd), lambda h, i, j: (i, h)),
                  pl.BlockSpec((tk, hd), lambda h, i, j: (j, h)),
                  pl.BlockSpec((tk, hd), lambda h, i, j: (j, h))],
        out_specs=[pl.BlockSpec((tq, 2 * hd), lambda h, i, j: (i, h)),
                   pl.BlockSpec((None, tq, 2), lambda h, i, j: (h, i, 0))],
        out_shape=[jax.ShapeDtypeStruct((seq, 2 * nkv * hd), BF16), jax.ShapeDtypeStruct((nkv, seq, 2), F32)],
        scratch_shapes=[pltpu.VMEM((2 * tq, 1), F32), pltpu.VMEM((2 * tq, 1), F32), pltpu.VMEM((2 * tq, hd), F32)],
        compiler_params=_params(),
    )(q, k, v)


def _flash_bwd(q, k, v, o, do, lse, *, n_ctx, hd):
    total = k.shape[0]
    seq = total - n_ctx
    nkv = k.shape[1] // hd
    tq, tk, rg = _attn_tiles(seq, total)
    scale = hd ** -0.5
    c2 = scale * LOG2E

    def body(q_ref, k_ref, v_ref, o_ref, do_ref, lse_ref, dq_ref, dk_ref, dv_ref):
        ki, qi = pl.program_id(1), pl.program_id(2)
        kk, vv = k_ref[...], v_ref[...]

        @pl.when(qi == 0)
        def _():
            dk_ref[...] = jnp.zeros(dk_ref.shape, F32)
            dv_ref[...] = jnp.zeros(dv_ref.shape, F32)

        dk_acc = jnp.zeros((tk, hd), F32)
        dv_acc = jnp.zeros((tk, hd), F32)
        for g in range(2):
            for sub in range(tq // rg):
                rs = slice(sub * rg, (sub + 1) * rg)
                cs = slice(g * hd, (g + 1) * hd)
                qq = q_ref[rs, cs]
                dd = do_ref[rs, cs]
                delta = jnp.sum(dd.astype(F32) * o_ref[rs, cs].astype(F32), axis=-1, keepdims=True)
                p = jnp.exp2(_dot_nt(qq, kk) * c2 - lse_ref[rs, g:g + 1] * LOG2E)
                dp = _dot_nt(dd, vv)
                ds = (p * (dp - delta) * scale).astype(BF16)
                dv_acc = dv_acc + _dot_tn(p.astype(BF16), dd)
                dk_acc = dk_acc + _dot_tn(ds, qq)
                dq = _dot(ds, kk)
                rows = pl.ds(pl.multiple_of(qi * tq, tq) + sub * rg, rg)

                @pl.when(ki == 0)
                def _():
                    dq_ref[rows, cs] = dq

                @pl.when(ki > 0)
                def _():
                    dq_ref[rows, cs] += dq
        dk_ref[...] += dk_acc
        dv_ref[...] += dv_acc

    return pl.pallas_call(
        body, name="attn_flash_bwd", grid=(nkv, total // tk, seq // tq),
        in_specs=[pl.BlockSpec((tq, 2 * hd), lambda h, j, i: (i, h)),
                  pl.BlockSpec((tk, hd), lambda h, j, i: (j, h)),
                  pl.BlockSpec((tk, hd), lambda h, j, i: (j, h)),
                  pl.BlockSpec((tq, 2 * hd), lambda h, j, i: (i, h)),
                  pl.BlockSpec((tq, 2 * hd), lambda h, j, i: (i, h)),
                  pl.BlockSpec((None, tq, 2), lambda h, j, i: (h, i, 0))],
        out_specs=[pl.BlockSpec((seq, 2 * hd), lambda h, j, i: (0, h)),
                   pl.BlockSpec((tk, hd), lambda h, j, i: (j, h)),
                   pl.BlockSpec((tk, hd), lambda h, j, i: (j, h))],
        out_shape=[jax.ShapeDtypeStruct((seq, 2 * nkv * hd), F32), jax.ShapeDtypeStruct((total, nkv * hd), F32),
                   jax.ShapeDtypeStruct((total, nkv * hd), F32)],
        compiler_params=_params(),
    )(q, k, v, o, do, lse)


def _proj_fwd(o, wo, hc, mods, *, n_ctx, tm):
    seq, d = o.shape
    off = n_ctx // tm

    def body(o_ref, w_ref, h_ref, md_ref, h1_ref, y_ref):
        y = _dot(o_ref[...], w_ref[...])
        y_ref[...] = y.astype(BF16)
        h1_ref[...] = h_ref[...] + md_ref[G1:G1 + 1, :] * y

    return pl.pallas_call(
        body, name="attn_proj_fwd", grid=(seq // tm,),
        in_specs=[_rows(tm, d), _full((d, d)),
                  pl.BlockSpec((tm, d), lambda i: (i + off, 0)), pl.BlockSpec((None, 8, d), lambda i: (1, 0, 0))],
        out_specs=[_rows(tm, d), _rows(tm, d)],
        out_shape=[jax.ShapeDtypeStruct((seq, d), F32), jax.ShapeDtypeStruct((seq, d), BF16)],
        compiler_params=_params(),
    )(o, wo, hc, mods)


def _proj_bwd(dh1, y, mods, wo, *, tm):
    seq, d = dh1.shape

    def body(g_ref, y_ref, md_ref, w_ref, do_ref, dyb_ref, dmd_ref):
        i = pl.program_id(0)
        g = g_ref[...]
        dyb = (g * md_ref[G1:G1 + 1, :]).astype(BF16)
        dyb_ref[...] = dyb
        do_ref[...] = _dot_nt(dyb, w_ref[...]).astype(BF16)
        _acc_rows(dmd_ref, i == 0, {G1: _rowsum(g * y_ref[...].astype(F32))})

    return pl.pallas_call(
        body, name="attn_proj_bwd", grid=(seq // tm,),
        in_specs=[_rows(tm, d), _rows(tm, d), pl.BlockSpec((None, 8, d), lambda i: (1, 0, 0)), _full((d, d))],
        out_specs=[_rows(tm, d), _rows(tm, d), pl.BlockSpec((None, 8, d), lambda i: (0, 0, 0))],
        out_shape=[jax.ShapeDtypeStruct((seq, d), BF16), jax.ShapeDtypeStruct((seq, d), BF16),
                   jax.ShapeDtypeStruct((1, 8, d), F32)],
        compiler_params=_params(),
    )(dh1, y, mods, wo)


def _qkv_bwd(qkv, dq, dk, dv, cos, sin, gains, *, nh, nkv, nct, tm):
    rows, qw = qkv.shape
    hd = cos.shape[-1]

    def body(qkv_ref, dq_ref, dk_ref, dv_ref, cos_ref, sin_ref, gn_ref, out_ref, dgn_ref):
        i = pl.program_id(0)
        c, s = cos_ref[...], sin_ref[...]
        is_lat = (i >= nct).astype(F32)
        dqg = jnp.zeros((1, hd), F32)
        dkg = jnp.zeros((1, hd), F32)
        for hh in range(nh + nkv):
            if hh < nh:
                dr = dq_ref[:, hh * hd:(hh + 1) * hd] * is_lat
                gn = gn_ref[0:1, :]
            else:
                dr = dk_ref[:, (hh - nh) * hd:(hh - nh + 1) * hd]
                gn = gn_ref[1:2, :]
            dy = dr * c + _rope_partner(dr * s)
            xhat, r = _rms_parts(qkv_ref[:, hh * hd:(hh + 1) * hd])
            dgh = _rowsum(dy * xhat)
            if hh < nh:
                dqg = dqg + dgh
            else:
                dkg = dkg + dgh
            dxhat = dy * gn
            dx = r * (dxhat - xhat * jnp.mean(dxhat * xhat, axis=-1, keepdims=True))
            out_ref[:, hh * hd:(hh + 1) * hd] = dx.astype(BF16)
        out_ref[:, (nh + nkv) * hd:] = dv_ref[...].astype(BF16)
        _acc_rows(dgn_ref, i == 0, {0: dqg, 1: dkg})

    return pl.pallas_call(
        body, name="attn_qkv_bwd", grid=(rows // tm,),
        in_specs=[_rows(tm, qw), pl.BlockSpec((tm, nh * hd), lambda i: (jnp.maximum(i - nct, 0), 0)),
                  _rows(tm, nkv * hd), _rows(tm, nkv * hd), _rows(tm, hd), _rows(tm, hd), _full((8, hd))],
        out_specs=[_rows(tm, qw), _full((8, hd))],
        out_shape=[jax.ShapeDtypeStruct((rows, qw), BF16), jax.ShapeDtypeStruct((8, hd), F32)],
        compiler_params=_params(),
    )(qkv, dq, dk, dv, cos, sin, gains)


def _attn_in_bwd(dqkv, wqkv, hc, dh1, mods, *, nct, tm):
    rows, d = hc.shape
    qw = dqkv.shape[1]

    def body(dz_ref, w_ref, h_ref, g_ref, md_ref, dh_ref, dmd_ref):
        i = pl.program_id(0)
        md = md_ref[...]
        da = _dot_nt_blocks(dz_ref[...], w_ref)
        _, parts = _normmod(h_ref[...], md, 0)
        dx, dsh, dsc, dng = _normmod_bwd(da, parts, md, 0)
        dh_ref[...] = g_ref[...] * (i >= nct).astype(F32) + dx
        _acc_rows(dmd_ref, _first_of_group(i, nct, 2), {SH1: dsh, SC1: dsc, NG0: dng})

    return pl.pallas_call(
        body, name="attn_in_bwd", grid=(rows // tm,),
        in_specs=[_rows(tm, qw), _full(wqkv.shape), _rows(tm, d),
                  pl.BlockSpec((tm, d), lambda i: (jnp.maximum(i - nct, 0), 0)), _mods_spec(nct, 2, d)],
        out_specs=[_rows(tm, d), _mods_spec(nct, 2, d)],
        out_shape=[jax.ShapeDtypeStruct((rows, d), F32), jax.ShapeDtypeStruct((2, 8, d), F32)],
        compiler_params=_params(),
    )(dqkv, wqkv, hc, dh1, mods)


def _gmlp_gate(z, lng, lnb, ws_ref, bs_ref, gg, ch):
    half = z.shape[1] // 2
    ggw = half // gg
    u, v = z[:, :half], z[:, half:]
    vc = v - jnp.mean(v, axis=-1, keepdims=True)
    rs = lax.rsqrt(jnp.mean(vc * vc, axis=-1, keepdims=True) + EPS)
    vhat = vc * rs
    vln = (vhat * lng + lnb).astype(BF16)
    chunks = []
    for n in range(z.shape[0] // ch):
        groups = []
        for g in range(gg):
            groups.append(_dot(ws_ref[g], vln[n * ch:(n + 1) * ch, g * ggw:(g + 1) * ggw]) + bs_ref[g])
        chunks.append(jnp.concatenate(groups, axis=1))
    sv = jnp.concatenate(chunks, axis=0) if len(chunks) > 1 else chunks[0]
    return u, sv, vhat, rs, vln


def _gmlp_fwd(h, mods, w_in, lng, lnb, ws, bs, w_out, *, tm):
    seq, d = h.shape
    zw = w_in.shape[0] * w_in.shape[-1]
    half = zw // 2
    gg, ch = ws.shape[0], ws.shape[-1]

    def body(h_ref, md_ref, win_ref, lng_ref, lnb_ref, ws_ref, bs_ref, wout_ref, h1_ref, zp_ref, y_ref):
        x = h_ref[...]
        md = md_ref[...]
        a, _ = _normmod(x, md, 0)
        zp = _dot_blocks(a.astype(BF16), win_ref)
        zp_ref[...] = zp.astype(BF16)
        u, sv, _, _, _ = _gmlp_gate(_gelu(zp), lng_ref[...], lnb_ref[...], ws_ref, bs_ref, gg, ch)
        y = _dot((u * sv).astype(BF16), wout_ref[...])
        y_ref[...] = y.astype(BF16)
        h1_ref[...] = x + md[G1:G1 + 1] * y

    return pl.pallas_call(
        body, name="gmlp_fwd", grid=(seq // tm,),
        in_specs=[_rows(tm, d), pl.BlockSpec((None, 8, d), lambda i: (1, 0, 0)),
                  _full(w_in.shape), _full((1, half)), _full((1, half)),
                  _full((gg, ch, ch)), _full((gg, ch, 1)), _full((half, d))],
        out_specs=[_rows(tm, d), _rows(tm, zw), _rows(tm, d)],
        out_shape=[jax.ShapeDtypeStruct((seq, d), F32), jax.ShapeDtypeStruct((seq, zw), BF16),
                   jax.ShapeDtypeStruct((seq, d), BF16)],
        compiler_params=_params(),
    )(h, mods, w_in, lng, lnb, ws, bs, w_out)


def _gmlp_bwd(h, dh1, zpre, y, mods, w_in, lng, lnb, ws, ws_t, bs, w_out, *, tm):
    seq, d = h.shape
    zw = w_in.shape[0] * w_in.shape[-1]
    half = zw // 2
    gg, ch = ws.shape[0], ws.shape[-1]
    ggw = half // gg

    def body(h_ref, g_ref, zp_ref, y_ref, md_ref, win_ref, lng_ref, lnb_ref, ws_ref, wst_ref, bs_ref, wout_ref,
             dh_ref, dzp_ref, gated_ref, dyb_ref, ab_ref, dmd_ref, dln_ref, dws_ref, dbs_ref):
        i = pl.program_id(0)
        x = h_ref[...]
        md = md_ref[...]
        a, parts = _normmod(x, md, 0)
        ab_ref[...] = a.astype(BF16)
        zp = zp_ref[...].astype(F32)
        lng_v = lng_ref[...]
        u, sv, vhat, rs, vln = _gmlp_gate(_gelu(zp), lng_v, lnb_ref[...], ws_ref, bs_ref, gg, ch)
        g = g_ref[...]
        dg1 = _rowsum(g * y_ref[...].astype(F32))
        dyb = (g * md[G1:G1 + 1]).astype(BF16)
        dyb_ref[...] = dyb
        gated_ref[...] = (u * sv).astype(BF16)
        dgated = _dot_nt(dyb, wout_ref[...])
        du = dgated * sv
        dsv = dgated * u

        @pl.when(i == 0)
        def _():
            dws_ref[...] = jnp.zeros(dws_ref.shape, F32)
            dbs_ref[...] = jnp.zeros(dbs_ref.shape, F32)
            dln_ref[...] = jnp.zeros(dln_ref.shape, F32)

        chunks = []
        for n in range(tm // ch):
            groups = []
            for gi in range(gg):
                blk = dsv[n * ch:(n + 1) * ch, gi * ggw:(gi + 1) * ggw]
                dbs_ref[gi] += jnp.sum(blk, axis=-1, keepdims=True)
                blk_b = blk.astype(BF16)
                dws_ref[gi] += _dot_nt(blk_b, vln[n * ch:(n + 1) * ch, gi * ggw:(gi + 1) * ggw])
                groups.append(_dot(wst_ref[gi], blk_b))
            chunks.append(jnp.concatenate(groups, axis=1))
        dvln = jnp.concatenate(chunks, axis=0) if len(chunks) > 1 else chunks[0]
        dln_ref[0:1, :] += _rowsum(dvln * vhat)
        dln_ref[1:2, :] += _rowsum(dvln)
        dvhat = dvln * lng_v
        dv = rs * (dvhat - jnp.mean(dvhat, axis=-1, keepdims=True)
                   - vhat * jnp.mean(dvhat * vhat, axis=-1, keepdims=True))
        dzp = (jnp.concatenate([du, dv], axis=1) * _gelu_grad(zp)).astype(BF16)
        dzp_ref[...] = dzp
        da = _dot_nt_blocks(dzp, win_ref)
        dx, dsh, dsc, dng = _normmod_bwd(da, parts, md, 0)
        dh_ref[...] = g + dx
        _acc_rows(dmd_ref, i == 0, {SH1: dsh, SC1: dsc, G1: dg1, NG0: dng})

    return pl.pallas_call(
        body, name="gmlp_bwd", grid=(seq // tm,),
        in_specs=[_rows(tm, d), _rows(tm, d), _rows(tm, zw), _rows(tm, d),
                  pl.BlockSpec((None, 8, d), lambda i: (1, 0, 0)),
                  _full(w_in.shape), _full((1, half)), _full((1, half)),
                  _full((gg, ch, ch)), _full((gg, ch, ch)), _full((gg, ch, 1)), _full((half, d))],
        out_specs=[_rows(tm, d), _rows(tm, zw), _rows(tm, half), _rows(tm, d), _rows(tm, d),
                   pl.BlockSpec((None, 8, d), lambda i: (0, 0, 0)), _full((8, half)), _full((gg, ch, ch)),
                   _full((gg, ch, 1))],
        out_shape=[jax.ShapeDtypeStruct((seq, d), F32), jax.ShapeDtypeStruct((seq, zw), BF16),
                   jax.ShapeDtypeStruct((seq, half), BF16), jax.ShapeDtypeStruct((seq, d), BF16),
                   jax.ShapeDtypeStruct((seq, d), BF16), jax.ShapeDtypeStruct((1, 8, d), F32),
                   jax.ShapeDtypeStruct((8, half), F32), jax.ShapeDtypeStruct((gg, ch, ch), F32),
                   jax.ShapeDtypeStruct((gg, ch, 1), F32)],
        compiler_params=_params(),
    )(h, dh1, zpre, y, mods, w_in, lng, lnb, ws, ws_t, bs, w_out)


def _final_loss(h, tgt, fg, *, tm):
    seq, d = h.shape

    def body(h_ref, t_ref, g_ref, dh_ref, acc_ref):
        i = pl.program_id(0)
        gain = g_ref[...]
        xhat, r = _rms_parts(h_ref[...])
        err = xhat * gain - t_ref[...]
        dy = err * (1.0 / d)
        dxhat = dy * gain
        dh_ref[...] = r * (dxhat - xhat * jnp.mean(dxhat * xhat, axis=-1, keepdims=True))
        part = jnp.sum(_rowsum(err * err), axis=-1, keepdims=True) * (0.5 / d)
        _acc_rows(acc_ref, i == 0, {0: _rowsum(dy * xhat), 1: jnp.broadcast_to(part, (1, d))})

    return pl.pallas_call(
        body, name="final_loss", grid=(seq // tm,),
        in_specs=[_rows(tm, d), _rows(tm, d), _full((1, d))],
        out_specs=[_rows(tm, d), _full((8, d))],
        out_shape=[jax.ShapeDtypeStruct((seq, d), F32), jax.ShapeDtypeStruct((8, d), F32)],
        compiler_params=_params(),
    )(h, tgt, fg)


def _ada_fwd(c_all, ada_w, ada_b_cols):
    depth, d, ncs = ada_w.shape

    def body(c_ref, w_ref, b_ref, o_ref):
        s = _silu(c_ref[...]).astype(BF16)
        o_ref[...] = _dot(s, w_ref[...].astype(BF16)) + b_ref[...]

    return pl.pallas_call(
        body, name="ada_fwd", grid=(depth,),
        in_specs=[_full((16, d)), pl.BlockSpec((None, d, ncs), lambda i: (i, 0, 0)),
                  pl.BlockSpec((None, 1, ncs), lambda i: (i, 0, 0))],
        out_specs=pl.BlockSpec((None, 16, ncs), lambda i: (i, 0, 0)),
        out_shape=jax.ShapeDtypeStruct((depth, 16, ncs), F32),
        compiler_params=_params(),
    )(c_all, ada_w, ada_b_cols.reshape(depth, 1, ncs))


def _ada_bwd(c_all, c_all_t, dmod, ada_w):
    depth, d, ncs = ada_w.shape

    def body(c_ref, ct_ref, dm_ref, w_ref, gw_ref, dc_ref):
        i = pl.program_id(0)
        dm = dm_ref[...]
        dctx = _rowsum(dm[8:16])
        rid = lax.broadcasted_iota(jnp.int32, (8, ncs), 0)
        low = jnp.where(rid == 0, jnp.broadcast_to(dctx, (8, ncs)), 0.0)
        dm16 = jnp.concatenate([dm[0:8], low], axis=0).astype(BF16)
        gw_ref[...] = _dot(_silu(ct_ref[...]).astype(BF16), dm16)

        @pl.when(i == 0)
        def _():
            dc_ref[...] = jnp.zeros(dc_ref.shape, F32)

        dc_ref[...] += _dot_nt(low.astype(BF16), w_ref[...].astype(BF16)) * _silu_grad(c_ref[8:9, :])

    return pl.pallas_call(
        body, name="ada_bwd", grid=(depth,),
        in_specs=[_full((16, d)), _full((d, 16)), pl.BlockSpec((None, 16, ncs), lambda i: (i, 0, 0)),
                  pl.BlockSpec((None, d, ncs), lambda i: (i, 0, 0))],
        out_specs=[pl.BlockSpec((None, d, ncs), lambda i: (i, 0, 0)), _full((8, d))],
        out_shape=[jax.ShapeDtypeStruct((depth, d, ncs), F32), jax.ShapeDtypeStruct((8, d), F32)],
        compiler_params=_params(),
    )(c_all, c_all_t, dmod, ada_w)


def _adamw_math(w, g, m, v):
    m = ADAM_B1 * m + (1.0 - ADAM_B1) * g
    v = ADAM_B2 * v + (1.0 - ADAM_B2) * jnp.square(g)
    m_hat = m * (1.0 / (1.0 - ADAM_B1 ** ADAM_STEP))
    v_hat = v * (1.0 / (1.0 - ADAM_B2 ** ADAM_STEP))
    delta = -ADAM_LR * (m_hat / (jnp.sqrt(v_hat) + ADAM_EPS) + ADAM_WD * w)
    return delta, m, v


def _adamw(ga, gb, w, m, v, name):
    rows, cols = w.shape
    tr = rows
    while tr * cols * 4 > (1 << 20) and tr % 16 == 0:
        tr //= 2
    grads = [ga] if gb is None else [ga, gb]

    def body(*refs):
        w_ref, m_ref, v_ref, g_out, d_out, m_out, v_out = refs[len(grads):]
        g = refs[0][...] if gb is None else refs[0][...] + refs[1][...]
        delta, m_new, v_new = _adamw_math(w_ref[...], g, m_ref[...], v_ref[...])
        g_out[...] = g
        d_out[...] = delta
        m_out[...] = m_new
        v_out[...] = v_new

    spec = _rows(tr, cols)
    return pl.pallas_call(
        body, name=name, grid=(rows // tr,),
        in_specs=[spec] * (len(grads) + 3), out_specs=[spec] * 4,
        out_shape=[jax.ShapeDtypeStruct((rows, cols), F32)] * 4,
        compiler_params=_params(),
    )(*grads, w, m, v)


def _sum_devices(gathered, name):
    n, rows, cols = gathered.shape
    tr = rows
    while tr * cols * 4 * n > (4 << 20) and tr % 16 == 0:
        tr //= 2

    def body(x_ref, o_ref):
        acc = x_ref[0]
        for j in range(1, n):
            acc = acc + x_ref[j]
        o_ref[...] = acc

    return pl.pallas_call(
        body, name=name, grid=(rows // tr,),
        in_specs=[pl.BlockSpec((n, tr, cols), lambda i: (0, i, 0))], out_specs=_rows(tr, cols),
        out_shape=jax.ShapeDtypeStruct((rows, cols), F32),
        compiler_params=_params(),
    )(gathered)


def _sum_partials(blocked, landeds, chip, name):
    n = len(blocked)
    cols = blocked[0].shape[-1]
    blocked = [b.reshape(N_CHIPS, -1, cols) for b in blocked]
    landeds = [l.reshape(3, -1, cols) for l in landeds]
    rows = blocked[0].shape[1]
    tr = rows
    while tr * cols * 2 * n > (1 << 20) and tr % 32 == 0:
        tr //= 2

    def body(chip_ref, *refs):
        out_ref = refs[-1]
        for li in range(n):
            acc = refs[li][...].astype(F32)
            for p in range(3):
                acc = acc + refs[n + li][p].astype(F32)
            out_ref[li] = acc

    out = pl.pallas_call(
        body, name=name,
        grid_spec=pltpu.PrefetchScalarGridSpec(
            num_scalar_prefetch=1, grid=(rows // tr,),
            in_specs=[pl.BlockSpec((None, tr, cols), lambda i, k: (k[0], i, 0))] * n
            + [pl.BlockSpec((3, tr, cols), lambda i, k: (0, i, 0))] * n,
            out_specs=pl.BlockSpec((n, tr, cols), lambda i, k: (0, i, 0))),
        out_shape=jax.ShapeDtypeStruct((n, rows, cols), F32),
        compiler_params=_params(),
    )(jnp.reshape(chip, (1,)).astype(jnp.int32), *blocked, *landeds)
    return out.reshape(n * rows, cols)


def _my_place():
    return lax.axis_index("x"), lax.axis_index("y"), lax.axis_index("c")


def _other_chips(x, y):
    return [(1 - x, y), (x, 1 - y), (1 - x, 1 - y)]


def _all_gather_small(block, name):
    rows, cols = block.shape

    def body(x_ref, out_ref, send_sems, recv_sems, local_sem):
        x, y, c = _my_place()
        me, sibling = (x, y, c), (x, y, 1 - c)
        chips = _other_chips(x, y)

        def slot(px, py, pc):
            return out_ref.at[4 * px + 2 * py + pc]

        def copy(k, blk, to, src=None):
            return pltpu.make_async_remote_copy(
                src_ref=slot(*blk) if src is None else src, dst_ref=slot(*blk),
                send_sem=send_sems.at[k], recv_sem=recv_sems.at[k], device_id=to, device_id_type=MESH)

        mine = pltpu.make_async_copy(x_ref, slot(*me), local_sem)
        mine.start()
        first = [copy(0, me, sibling, src=x_ref)]
        first += [copy(1 + j, me, (*chip, c), src=x_ref) for j, chip in enumerate(chips)]
        for cp in first:
            cp.start()
        passed = [copy(4 + j, (*chip, c), sibling) for j, chip in enumerate(chips)]
        for j, chip in enumerate(chips):
            copy(1 + j, (*chip, c), me).wait_recv()
            passed[j].start()
        copy(0, sibling, me).wait_recv()
        for j, chip in enumerate(chips):
            copy(4 + j, (*chip, 1 - c), me).wait_recv()
        for cp in first + passed:
            cp.wait_send()
        mine.wait()

    return pl.pallas_call(
        body, name=name,
        out_shape=jax.ShapeDtypeStruct((N_DEV, rows, cols), block.dtype),
        in_specs=[pl.BlockSpec(memory_space=pltpu.VMEM)],
        out_specs=pl.BlockSpec(memory_space=pltpu.VMEM),
        scratch_shapes=[pltpu.SemaphoreType.DMA((7,)), pltpu.SemaphoreType.DMA((7,)), pltpu.SemaphoreType.DMA],
        compiler_params=_params(),
    )(block)


HBM_SPEC = pl.BlockSpec(memory_space=pltpu.HBM)
SEM_SPEC = pl.BlockSpec(memory_space=pltpu.SEMAPHORE)
DATAFLOW_EFFECT = pltpu.SideEffectType.DATAFLOW_SIDE_EFFECTING


def _same_core_of_other_chips(x, y, c):
    return [(*chip, c) for chip in _other_chips(x, y)]


def _sibling_core(x, y, c):
    return [(x, y, 1 - c)]


def _gather_views(src, land, p, x, y):
    return src, land.at[2 * x + y]


def _scatter_views(src, land, p, x, y):
    peer_chip = (2 * (1 - x) + y, 2 * x + (1 - y), 2 * (1 - x) + (1 - y))[p]
    return src.at[peer_chip], land.at[p]


def _whole_views(src, land, p, x, y):
    return src, land


GATHER_PLAN = (_same_core_of_other_chips, _gather_views, 3)
SCATTER_PLAN = (_same_core_of_other_chips, _scatter_views, 3)
SIBLING_PLAN = (_sibling_core, _whole_views, 1)


def _exchange_copies(srcs, lands, send_sems, recv_sems, plan):
    peers_of, views, n_peers = plan
    x, y, c = _my_place()
    copies = []
    for j, (src, land) in enumerate(zip(srcs, lands)):
        for p, peer in enumerate(peers_of(x, y, c)):
            s_view, d_view = views(src, land, p, x, y)
            k = n_peers * j + p
            copies.append(pltpu.make_async_remote_copy(
                src_ref=s_view, dst_ref=d_view, send_sem=send_sems.at[k], recv_sem=recv_sems.at[k],
                device_id=peer, device_id_type=MESH))
    return copies


def _exchange_start(srcs, lands, plan, name):
    n = len(srcs)

    def body(*refs):
        send_sems, recv_sems = refs[2 * n], refs[2 * n + 1]
        token = refs[-1]
        for cp in _exchange_copies(refs[:n], refs[n:2 * n], send_sems, recv_sems, plan):
            cp.start()
        token[...] = jnp.zeros(token.shape, token.dtype)

    operands = [pltpu.with_memory_space_constraint(a, pltpu.HBM) for a in (*srcs, *lands)]
    out = pl.pallas_call(
        body, name=name,
        out_shape=(pltpu.SemaphoreType.DMA((plan[2] * n,)), pltpu.SemaphoreType.DMA((plan[2] * n,)),
                   *[pltpu.HBM(a.shape, a.dtype) for a in operands], jax.ShapeDtypeStruct((8, LANES), F32)),
        in_specs=[HBM_SPEC] * (2 * n),
        out_specs=(SEM_SPEC, SEM_SPEC, *[HBM_SPEC] * (2 * n), pl.BlockSpec(memory_space=pltpu.VMEM)),
        input_output_aliases={i: 2 + i for i in range(2 * n)},
        compiler_params=pltpu.CompilerParams(has_side_effects=DATAFLOW_EFFECT),
    )(*operands)
    return out[0], out[1], list(out[2:2 + n]), list(out[2 + n:2 + 2 * n]), out[-1]


def _exchange_wait(send_sems, recv_sems, srcs, lands, plan, after, name):
    n = len(srcs)

    def body(*refs):
        send, recv = refs[2 * n], refs[2 * n + 1]
        for cp in _exchange_copies(refs[:n], refs[n:2 * n], send, recv, plan):
            cp.wait_send()
            cp.wait_recv()

    out = pl.pallas_call(
        body, name=name,
        out_shape=tuple(pltpu.HBM(a.shape, a.dtype) for a in (*srcs, *lands)),
        in_specs=[HBM_SPEC] * (2 * n) + [SEM_SPEC, SEM_SPEC, HBM_SPEC],
        out_specs=tuple([HBM_SPEC] * (2 * n)),
        input_output_aliases={i: i for i in range(2 * n)},
        compiler_params=pltpu.CompilerParams(has_side_effects=DATAFLOW_EFFECT),
    )(*srcs, *lands, send_sems, recv_sems, pltpu.with_memory_space_constraint(after, pltpu.HBM))
    return list(out[:n]), list(out[n:])


def _landing_for_gather(shard, chip):
    land = lax.empty((N_CHIPS, *shard.shape), shard.dtype)
    return lax.dynamic_update_index_in_dim(land, shard, chip, 0)


TILE_ELEMS = SUBLANES * LANES


def _pack(arrays):
    parts = []
    for a in arrays:
        flat = a.reshape(-1).astype(F32)
        pad = (-flat.shape[0]) % TILE_ELEMS
        if pad:
            flat = jnp.concatenate([flat, jnp.zeros((pad,), F32)])
        parts.append(flat.reshape(-1, LANES))
    return jnp.concatenate(parts, axis=0) if len(parts) > 1 else parts[0]


def _unpack(buf, shapes):
    out, r = [], 0
    lead = buf.shape[:-2]
    for shp in shapes:
        size = math.prod(shp)
        nr = -(-size // TILE_ELEMS) * SUBLANES
        flat = buf[..., r:r + nr, :].reshape(*lead, nr * LANES)[..., :size]
        out.append(flat.reshape(*lead, *shp))
        r += nr
    return out


def _chip_cols(a, k, width):
    return lax.dynamic_slice_in_dim(a, k * width, width, axis=a.ndim - 1)


def _across_chips(gathered, c0_only_shape):
    return gathered.reshape(2, 2, 2, *c0_only_shape)[:, :, 0].reshape(N_CHIPS, *c0_only_shape)


def kernel(x, c, ctx, c_ctx, ada_w, ada_b, norm_g, mlp_w1, mlp_w2, pool_w, pool_scale, attn_w_qkv, attn_w_o, attn_q_g, attn_k_g, gm_w_in, gm_ln_g, gm_ln_b, gm_ws, gm_bs, gm_w_out, final_g, loss_target, m_c_ctx, m_ada_w, m_ada_b, m_norm_g, m_mlp_w1, m_mlp_w2, m_pool_w, m_pool_scale, m_attn_w_qkv, m_attn_w_o, m_attn_q_g, m_attn_k_g, m_gm_w_in, m_gm_ln_g, m_gm_ln_b, m_gm_ws, m_gm_bs, m_gm_w_out, m_final_g, v_c_ctx, v_ada_w, v_ada_b, v_norm_g, v_mlp_w1, v_mlp_w2, v_pool_w, v_pool_scale, v_attn_w_qkv, v_attn_w_o, v_attn_q_g, v_attn_k_g, v_gm_w_in, v_gm_ln_g, v_gm_ln_b, v_gm_ws, v_gm_bs, v_gm_w_out, v_final_g):
    seq, d = x.shape[1], x.shape[2]
    n_ctx = ctx.shape[1]
    total = n_ctx + seq
    hd = attn_q_g.shape[-1]
    nh = d // hd
    nkv = nh // 2
    gg, ch = gm_ws.shape[1], gm_ws.shape[-1]
    half = gm_w_out.shape[1] * N_CHIPS
    pgw = pool_w.shape[-1]
    tm = min(256, n_ctx)
    nct = n_ctx // tm
    seg_lens = (n_ctx, seq)

    mx, my, mc = _my_place()
    chip = 2 * mx + my
    me = 4 * mx + 2 * my + mc

    c_rows = jnp.concatenate([c, jnp.zeros((7, d), F32)], axis=0)
    c_gath = _all_gather_small(c_rows, "gather_cond")[:, 0, :]
    c_all = jnp.concatenate([c_gath, c_ctx[None, :], jnp.zeros((7, d), F32)], axis=0)
    ncs = ada_w.shape[-1]
    ada_cols = _ada_fwd(c_all, ada_w, _chip_cols(ada_b, chip, ncs))
    small_shapes = [ada_cols.shape, norm_g.shape, pool_scale.shape, gm_ln_g.shape, gm_ln_b.shape]
    gathered = _all_gather_small(_pack([ada_cols, norm_g, pool_scale, gm_ln_g, gm_ln_b]), "gather_small_params")
    per_chip = _across_chips(gathered, gathered.shape[1:])
    ada_g, ng_g, ps_g, lng_g, lnb_g = _unpack(per_chip, small_shapes)

    def join_last(a):
        return jnp.moveaxis(a, 0, -2).reshape(*a.shape[1:-1], N_CHIPS * a.shape[-1])

    ada_full = join_last(ada_g)
    ng_full = join_last(ng_g)
    ps_full = join_last(ps_g)
    lng_full = join_last(lng_g)
    lnb_full = join_last(lnb_g)
    mod_lat = lax.dynamic_slice_in_dim(ada_full, me, 1, axis=1).reshape(DEPTH, 6, d)
    mod_ctx = ada_full[:, 8].reshape(DEPTH, 6, d)
    mods = jnp.stack([jnp.concatenate([mod_ctx, ng_full], axis=1), jnp.concatenate([mod_lat, ng_full], axis=1)],
                     axis=1)

    weight_groups = [
        [pool_w],
        [mlp_w1[0], mlp_w2[0]],
        [attn_w_qkv[0], attn_w_o[0]],
        [mlp_w1[1], mlp_w2[1], mlp_w1[2], mlp_w2[2], gm_w_in[0], gm_w_out[0], mlp_w1[3], mlp_w2[3]],
    ]
    gathers = [None] * len(weight_groups)

    def gather_start(gi, after):
        shards, _ = lax.optimization_barrier(([w.astype(BF16) for w in weight_groups[gi]], after))
        lands = [_landing_for_gather(s, chip) for s in shards]
        gathers[gi] = _exchange_start(shards, lands, GATHER_PLAN, f"gather_weights_{gi}_start")
        return gathers[gi][4][0:1, 0:1]

    def gathered(gi, after):
        send, recv, srcs, lands, _ = gathers[gi]
        return _exchange_wait(send, recv, srcs, lands, GATHER_PLAN, after, f"gather_weights_{gi}_wait")[1]

    def rows_joined(a):
        return a.reshape(-1, a.shape[-1])

    w1_b, w2_b = [None] * DEPTH, [None] * DEPTH
    gather_start(0, mods)
    pw_land, = gathered(0, ps_full)
    behind_gather_1 = gather_start(1, pw_land)
    pw_f = jnp.transpose(pw_land, (1, 2, 0, 3, 4)).reshape(pool_w.shape[0], pool_w.shape[1], pgw, pgw)

    gains = jnp.concatenate([attn_q_g, attn_k_g, jnp.zeros((6, hd), F32)], axis=0)
    ws_b = gm_ws[0].astype(BF16)
    ws_t = jnp.swapaxes(gm_ws[0], 1, 2).astype(BF16)
    bs_col = gm_bs[0][:, :, None]
    cos, sin = _rope_tables(n_ctx, seq, hd)
    lat = lambda i: mods[i, 1:2]

    hc0 = jnp.concatenate([ctx[0] + behind_gather_1, x[0]], axis=0)
    ha0, p0 = _pool_fwd(hc0, mods[0] + behind_gather_1, pw_f, ps_full, 0, nct=nct, tm=tm, seg_lens=seg_lens)
    w1_b[0], w2_b[0] = gathered(1, ha0)
    mods0 = mods[0] + gather_start(2, w1_b[0])
    hc1, u0, o0 = _mlp_fwd(ha0, mods0, w1_b[0], w2_b[0], 0, nct=nct, tm=tm)
    wqkv_b, wo_land = gathered(2, hc1)
    mods1 = mods[1] + gather_start(3, wqkv_b)
    wo_f = rows_joined(wo_land)
    xa1, qkv, q_r, k_r, v_b = _qkv_fwd(hc1, mods1, wqkv_b, cos, sin, gains, nh=nh, nkv=nkv, nct=nct, tm=tm)
    o_att, lse = _flash_fwd(q_r, k_r, v_b, n_ctx=n_ctx, hd=hd)
    ha1, y1 = _proj_fwd(o_att, wo_f, hc1, mods1, n_ctx=n_ctx, tm=tm)
    w1_b[1], w2_b[1], w1_b[2], w2_b[2], win_b, wout_land, w1_b[3], w2_b[3] = gathered(3, ha1)
    h2, u1, o1 = _mlp_fwd(ha1, lat(1), w1_b[1], w2_b[1], 1, nct=0, tm=tm)
    wout_f = rows_joined(wout_land)
    ha2, zpre, y2 = _gmlp_fwd(h2, mods[2], win_b, lng_full, lnb_full, ws_b, bs_col, wout_f, tm=tm)
    h3, u2, o2 = _mlp_fwd(ha2, lat(2), w1_b[2], w2_b[2], 2, nct=0, tm=tm)
    ha3, p3 = _pool_fwd(h3, lat(3), pw_f, ps_full, 3, nct=0, tm=tm, seg_lens=seg_lens)
    h4, u3, o3 = _mlp_fwd(ha3, lat(3), w1_b[3], w2_b[3], 3, nct=0, tm=tm)
    dh4, fin_acc = _final_loss(h4, loss_target[0], final_g[None, :], tm=tm)

    dmods = [None] * DEPTH
    scatters = [None] * (DEPTH + 1)

    def blocked_rows(g):
        return g.reshape(N_CHIPS, g.shape[1] // N_CHIPS, g.shape[2])

    def blocked_pool(dpw):
        pg = dpw.shape[0]
        return jnp.transpose(dpw.astype(BF16).reshape(pg, N_CHIPS, pgw // N_CHIPS, pgw), (1, 0, 2, 3))

    def scatter_start(i, grads):
        lands = [lax.empty((3, *g.shape[1:]), g.dtype) for g in grads]
        scatters[i] = _exchange_start(grads, lands, SCATTER_PLAN, f"scatter_grads_{i}_start")
        return scatters[i][4][0:1, 0:1]

    def mlp_back(i, h_in, dh_out, u, o, md, n_ct):
        dh_in, du, dob, mb, dmd = _mlp_bwd(h_in, dh_out, u, o, md, w1_b[i], w2_b[i], i, nct=n_ct, tm=tm)
        dw1 = _mm_tn(mb, du, f"mlp_dw1_{i}", col_blocks=N_CHIPS)
        dw2 = blocked_rows(_mm_tn(u, dob, f"mlp_dw2_{i}", relu2=True))
        return dh_in, dmd, [dw1, dw2]

    def pool_back(i, h_in, p_in, dh_out, md, n_ct):
        dp, dmd_a, dps, dpw = _pool_bwd_weights(p_in, dh_out, md, pw_f, ps_full, i, nct=n_ct, tm=tm)
        dh_in, dmd_b = _pool_bwd_input(dp, h_in, dh_out, md, i, nct=n_ct, tm=tm, seg_lens=seg_lens, gw=pgw)
        return dh_in, dmd_a + dmd_b, dps, dpw

    zero_grp = jnp.zeros((1, 8, d), F32)
    dha3, dmd3, dws3 = mlp_back(3, ha3, dh4, u3, o3, lat(3), 0)
    dh3, dmd3p, dps3, dpw3 = pool_back(3, h3, p3, dha3, lat(3), 0)
    dmods[3] = jnp.concatenate([zero_grp, dmd3 + dmd3p], axis=0)
    tok = scatter_start(3, dws3 + [blocked_pool(dpw3)])
    dha2, dmd2, dws2 = mlp_back(2, ha2, dh3, u2, o2, lat(2) + tok, 0)
    dh2, dzpre, gated, dyb2, ab2, dmd2g, dln, dws, dbs = _gmlp_bwd(
        h2, dha2, zpre, y2, mods[2], win_b, lng_full, lnb_full, ws_b, ws_t, bs_col, wout_f, tm=tm)
    dwin = _mm_tn(ab2, dzpre, "gmlp_dw_in", col_blocks=N_CHIPS)
    dwout = blocked_rows(_mm_tn(gated, dyb2, "gmlp_dw_out"))
    dmods[2] = jnp.concatenate([zero_grp, dmd2 + dmd2g], axis=0)
    tok = scatter_start(2, dws2 + [dwin, dwout])
    dha1, dmd1, dws1 = mlp_back(1, ha1, dh2, u1, o1, lat(1) + tok, 0)
    do_att, dyb1, dmd1p = _proj_bwd(dha1, y1, mods[1], wo_f, tm=tm)
    dwo = blocked_rows(_mm_tn(o_att, dyb1, "attn_dw_o"))
    dq, dk, dv = _flash_bwd(q_r, k_r, v_b, o_att, do_att, lse, n_ctx=n_ctx, hd=hd)
    dqkv, dgains = _qkv_bwd(qkv, dq, dk, dv, cos, sin, gains, nh=nh, nkv=nkv, nct=nct, tm=tm)
    dwqkv = _mm_tn(xa1, dqkv, "attn_dw_qkv", col_blocks=N_CHIPS)
    dhc1, dmd1i = _attn_in_bwd(dqkv, wqkv_b, hc1, dha1, mods[1], nct=nct, tm=tm)
    dmods[1] = dmd1i + jnp.concatenate([zero_grp, dmd1 + dmd1p], axis=0)
    tok = scatter_start(1, dws1 + [dwqkv, dwo])
    dha0, dmd0, dws0 = mlp_back(0, ha0, dhc1, u0, o0, mods[0] + tok, nct)
    tok = scatter_start(0, dws0)
    dhc0, dmd0p, dps0, dpw0 = pool_back(0, hc0, p0, dha0, mods[0] + tok, nct)
    dmods[0] = dmd0 + dmd0p
    grad_x = dhc0[None]
    scatter_start(DEPTH, [blocked_pool(dpw0)])

    dmods_all = jnp.stack(dmods, axis=0)
    small_grads = [dmods_all, dws, dbs, dgains, dln, dps0, dps3, fin_acc]
    sg_shapes = [a.shape for a in small_grads]
    sg_gath = _all_gather_small(_pack(small_grads), "gather_small_grads")
    sg_sum = _sum_devices(sg_gath, "sum_small_grads")
    s_dmods, s_dws, s_dbs, s_dgains, s_dln, s_dps0, s_dps3, s_fin = _unpack(sg_sum, sg_shapes)
    loss = s_fin[1, 0]

    sources, landed = [None] * len(scatters), [None] * len(scatters)
    for i in (3, 2, 1, 0, DEPTH):
        send, recv, srcs, lands, _ = scatters[i]
        sources[i], landed[i] = _exchange_wait(send, recv, srcs, lands, SCATTER_PLAN, sg_sum, f"scatter_grads_{i}_wait")

    def summed(name, picks):
        return _sum_partials([sources[i][j] for i, j in picks], [landed[i][j] for i, j in picks], chip,
                             f"sum_chips_{name}")

    big = [("mlp_w1", mlp_w1, m_mlp_w1, v_mlp_w1, [(i, 0) for i in range(DEPTH)]),
           ("mlp_w2", mlp_w2, m_mlp_w2, v_mlp_w2, [(i, 1) for i in range(DEPTH)]),
           ("pool_w", pool_w, m_pool_w, v_pool_w, [(DEPTH, 0), (3, 2)]),
           ("attn_w_qkv", attn_w_qkv, m_attn_w_qkv, v_attn_w_qkv, [(1, 2)]),
           ("attn_w_o", attn_w_o, m_attn_w_o, v_attn_w_o, [(1, 3)]),
           ("gm_w_in", gm_w_in, m_gm_w_in, v_gm_w_in, [(2, 2)]),
           ("gm_w_out", gm_w_out, m_gm_w_out, v_gm_w_out, [(2, 3)])]
    partial = [summed(name, picks) for name, _, _, _, picks in big]
    swap = _exchange_start(partial, [lax.empty(p.shape, p.dtype) for p in partial], SIBLING_PLAN,
                           "swap_with_sibling_start")
    behind_swap = swap[4][0:1, 0:1]

    dm_dev = _unpack(sg_gath, sg_shapes[:1])[0]
    dm_lat = jnp.moveaxis(dm_dev[:, :, 1, :6, :], 0, 1).reshape(DEPTH, N_DEV, 6 * d)
    dm_ctx = jnp.moveaxis(dm_dev[:, :, 0, :6, :], 0, 1).reshape(DEPTH, N_DEV, 6 * d)
    dmod16 = _chip_cols(jnp.concatenate([dm_lat, dm_ctx], axis=1), chip, ncs) + behind_swap
    g_ada_w, dcc_part = _ada_bwd(c_all, c_all.T, dmod16, ada_w)
    dcc_gath = _all_gather_small(dcc_part, "gather_d_c_ctx")
    dcc_chips = _across_chips(dcc_gath, dcc_gath.shape[1:])
    dcc_rows = _sum_devices(dcc_chips, "sum_d_c_ctx")
    dcc = dcc_rows[0]
    ada_res = _adamw(g_ada_w.reshape(-1, ncs), None, ada_w.reshape(-1, ncs),
                     m_ada_w.reshape(-1, ncs), v_ada_w.reshape(-1, ncs), "adamw_ada_w")

    partial, from_sibling = _exchange_wait(swap[0], swap[1], swap[2], swap[3], SIBLING_PLAN, ada_res[1],
                                           "swap_with_sibling_wait")
    big_out = {}
    for (name, w, m, v, _), mine, theirs in zip(big, partial, from_sibling):
        cols = w.shape[-1]
        res = _adamw(mine, theirs, w.reshape(-1, cols), m.reshape(-1, cols), v.reshape(-1, cols), f"adamw_{name}")
        big_out[name] = [r.reshape(w.shape) for r in res]
    big_out["ada_w"] = [r.reshape(ada_w.shape) for r in ada_res]

    def cols_of(a, width):
        return _chip_cols(a, chip, width)

    zero = lambda a: jnp.zeros(a.shape, F32)
    ngw = norm_g.shape[-1]
    small = {
        "c_ctx": (dcc, zero(dcc), c_ctx, m_c_ctx, v_c_ctx),
        "ada_b": (s_dmods[:, 0, :6].reshape(DEPTH, 6 * d), s_dmods[:, 1, :6].reshape(DEPTH, 6 * d), ada_b, m_ada_b,
                  v_ada_b),
        "norm_g": (cols_of(s_dmods[:, 0, 6:8], ngw), cols_of(s_dmods[:, 1, 6:8], ngw), norm_g, m_norm_g, v_norm_g),
        "pool_scale": (cols_of(jnp.stack([s_dps0[0], s_dps3[0]]), pool_scale.shape[-1]), zero(pool_scale),
                       pool_scale, m_pool_scale, v_pool_scale),
        "attn_q_g": (s_dgains[0:1], zero(attn_q_g), attn_q_g, m_attn_q_g, v_attn_q_g),
        "attn_k_g": (s_dgains[1:2], zero(attn_k_g), attn_k_g, m_attn_k_g, v_attn_k_g),
        "gm_ln_g": (cols_of(s_dln[0:1], gm_ln_g.shape[-1]), zero(gm_ln_g), gm_ln_g, m_gm_ln_g, v_gm_ln_g),
        "gm_ln_b": (cols_of(s_dln[1:2], gm_ln_b.shape[-1]), zero(gm_ln_b), gm_ln_b, m_gm_ln_b, v_gm_ln_b),
        "gm_ws": (s_dws[None], zero(gm_ws), gm_ws, m_gm_ws, v_gm_ws),
        "gm_bs": (s_dbs[None, :, :, 0], zero(gm_bs), gm_bs, m_gm_bs, v_gm_bs),
        "final_g": (s_fin[0], zero(final_g), final_g, m_final_g, v_final_g),
    }
    keys = list(small)
    packed = [_pack([small[k][t] for k in keys]) for t in range(5)]
    res = _adamw(*packed, "adamw_small")
    shapes = [small[k][2].shape for k in keys]
    small_out = {k: [] for k in keys}
    for r in res:
        for k, a in zip(keys, _unpack(r, shapes)):
            small_out[k].append(a)

    order = ["c_ctx", "ada_w", "ada_b", "norm_g", "mlp_w1", "mlp_w2", "pool_w", "pool_scale", "attn_w_qkv",
             "attn_w_o", "attn_q_g", "attn_k_g", "gm_w_in", "gm_ln_g", "gm_ln_b", "gm_ws", "gm_bs", "gm_w_out",
             "final_g"]
    allo = {**big_out, **small_out}
    outs = [loss, grad_x]
    for t in range(4):
        outs += [allo[k][t] for k in order]
    return tuple(outs)
```

```python
import functools
import math

import numpy as np
import jax
import jax.numpy as jnp
from jax import lax
from jax.experimental import pallas as pl
from jax.experimental.pallas import tpu as pltpu

F32 = jnp.float32
BF16 = jnp.bfloat16
MESH = pl.DeviceIdType.MESH

EPS = 1e-6
GRID_W = 64
ROPE_BASE = 10000.0
POOL_WINDOWS = (2, 4, 8, 16)
HALO = 8
DEPTH = 4
N_MIXERS = 3

ADAM_LR = 0.001
ADAM_B1 = 0.9
ADAM_B2 = 0.999
ADAM_EPS = 1e-08
ADAM_WD = 0.01
ADAM_STEP = 10

VMEM_LIMIT_BYTES = 56 * 1024 * 1024
LANES = 128
SUBLANES = 8
N_DEV = 8
N_CHIPS = 4

SH1, SC1, G1, SH2, SC2, G2, NG0, NG1 = range(8)


def _dot(a, b):
    return jnp.dot(a, b, preferred_element_type=F32)


def _dot_nt(a, b):
    return lax.dot_general(a, b, (((1,), (1,)), ((), ())), preferred_element_type=F32)


def _dot_tn(a, b):
    return lax.dot_general(a, b, (((0,), (0,)), ((), ())), preferred_element_type=F32)


def _dot_blocks(a, w_ref):
    return jnp.concatenate([_dot(a, w_ref[k]) for k in range(w_ref.shape[0])], axis=1)


def _dot_nt_blocks(a, w_ref):
    nb, _, w = w_ref.shape
    acc = _dot_nt(a[:, 0:w], w_ref[0])
    for k in range(1, nb):
        acc = acc + _dot_nt(a[:, k * w:(k + 1) * w], w_ref[k])
    return acc


def _params(**kw):
    return pltpu.CompilerParams(vmem_limit_bytes=VMEM_LIMIT_BYTES, **kw)


def _full(shape):
    nd = len(shape)
    return pl.BlockSpec(shape, lambda *_: (0,) * nd)


def _rows(tm, width):
    return pl.BlockSpec((tm, width), lambda i: (i, 0))


def _group_of(nct, groups):
    if groups == 1:
        return lambda i: 0
    return lambda i: jnp.where(i >= nct, 1, 0)


def _mods_spec(nct, groups, d):
    grp = _group_of(nct, groups)
    return pl.BlockSpec((None, 8, d), lambda i: (grp(i), 0, 0))


def _first_of_group(i, nct, groups):
    if groups == 1:
        return i == 0
    return jnp.logical_or(i == 0, i == nct)


def _rowsum(v):
    return jnp.sum(v, axis=0, keepdims=True)


def _rms_parts(x):
    r = lax.rsqrt(jnp.mean(x * x, axis=-1, keepdims=True) + EPS)
    return x * r, r


def _normmod(x, md, which):
    ng, sh, sc = (md[NG0:NG0 + 1], md[SH1:SH1 + 1], md[SC1:SC1 + 1]) if which == 0 else (
        md[NG1:NG1 + 1], md[SH2:SH2 + 1], md[SC2:SC2 + 1])
    xhat, r = _rms_parts(x)
    n = xhat * ng
    return n * (1.0 + sc) + sh, (xhat, r, n)


def _normmod_bwd(da, parts, md, which):
    xhat, r, n = parts
    ng, sc = (md[NG0:NG0 + 1], md[SC1:SC1 + 1]) if which == 0 else (md[NG1:NG1 + 1], md[SC2:SC2 + 1])
    dsh = _rowsum(da)
    dsc = _rowsum(da * n)
    dn = da * (1.0 + sc)
    dng = _rowsum(dn * xhat)
    dxhat = dn * ng
    dx = r * (dxhat - xhat * jnp.mean(dxhat * xhat, axis=-1, keepdims=True))
    return dx, dsh, dsc, dng


def _acc_rows(ref, first, rows):
    @pl.when(first)
    def _():
        ref[...] = jnp.zeros(ref.shape, ref.dtype)

    for r, v in rows.items():
        ref[r:r + 1, :] += v


def _shift_up(x, k):
    if k == 0:
        return x
    return pltpu.roll(x, x.shape[0] - k, axis=0)


def _gelu(x):
    k = math.sqrt(2.0 / math.pi)
    return 0.5 * x * (1.0 + jnp.tanh(k * (x + 0.044715 * x * x * x)))


def _gelu_grad(x):
    k = math.sqrt(2.0 / math.pi)
    t = jnp.tanh(k * (x + 0.044715 * x * x * x))
    return 0.5 * (1.0 + t) + 0.5 * x * (1.0 - t * t) * k * (1.0 + 3.0 * 0.044715 * x * x)


def _silu(x):
    return x / (1.0 + jnp.exp(-x))


def _silu_grad(x):
    s = 1.0 / (1.0 + jnp.exp(-x))
    return s * (1.0 + x * (1.0 - s))


def _mlp_fwd(h, mods, w1, w2, layer, *, nct, tm):
    rows, d = h.shape
    groups = mods.shape[0]
    nb, _, fc = w1.shape
    ff = nb * fc

    def body(h_ref, md_ref, w1_ref, w2_ref, h2_ref, u_ref, o_ref):
        x = h_ref[...]
        md = md_ref[...]
        m, _ = _normmod(x, md, 1)
        mb = m.astype(BF16)
        acc = jnp.zeros((tm, d), F32)
        for k in range(nb):
            u = _dot(mb, w1_ref[k])
            u_ref[:, k * fc:(k + 1) * fc] = u.astype(BF16)
            acc = acc + _dot(jnp.square(jnp.maximum(u, 0.0)).astype(BF16), w2_ref[k])
        o_ref[...] = acc.astype(BF16)
        h2_ref[...] = x + md[G2:G2 + 1] * acc

    return pl.pallas_call(
        body, name=f"mlp_fwd_{layer}", grid=(rows // tm,),
        in_specs=[_rows(tm, d), _mods_spec(nct, groups, d), _full(w1.shape), _full(w2.shape)],
        out_specs=[_rows(tm, d), _rows(tm, ff), _rows(tm, d)],
        out_shape=[jax.ShapeDtypeStruct((rows, d), F32), jax.ShapeDtypeStruct((rows, ff), BF16),
                   jax.ShapeDtypeStruct((rows, d), BF16)],
        compiler_params=_params(),
    )(h, mods, w1, w2)


def _mlp_bwd(h1, dh2, u, o, mods, w1, w2, layer, *, nct, tm):
    rows, d = h1.shape
    groups = mods.shape[0]
    nb, _, fc = w1.shape
    ff = nb * fc

    def body(h_ref, g_ref, u_ref, o_ref, md_ref, w1_ref, w2_ref, dh_ref, du_ref, dob_ref, mb_ref, dmd_ref):
        i = pl.program_id(0)
        x = h_ref[...]
        g = g_ref[...]
        md = md_ref[...]
        m, parts = _normmod(x, md, 1)
        mb_ref[...] = m.astype(BF16)
        dg2 = _rowsum(g * o_ref[...].astype(F32))
        dob = (g * md[G2:G2 + 1]).astype(BF16)
        dob_ref[...] = dob
        dm = jnp.zeros((tm, d), F32)
        for k in range(nb):
            uk = u_ref[:, k * fc:(k + 1) * fc].astype(F32)
            dr = _dot_nt(dob, w2_ref[k])
            duk = (dr * (2.0 * jnp.maximum(uk, 0.0))).astype(BF16)
            du_ref[:, k * fc:(k + 1) * fc] = duk
            dm = dm + _dot_nt(duk, w1_ref[k])
        dx, dsh, dsc, dng = _normmod_bwd(dm, parts, md, 1)
        dh_ref[...] = g + dx
        _acc_rows(dmd_ref, _first_of_group(i, nct, groups), {SH2: dsh, SC2: dsc, G2: dg2, NG1: dng})

    return pl.pallas_call(
        body, name=f"mlp_bwd_{layer}", grid=(rows // tm,),
        in_specs=[_rows(tm, d), _rows(tm, d), _rows(tm, ff), _rows(tm, d), _mods_spec(nct, groups, d),
                  _full(w1.shape), _full(w2.shape)],
        out_specs=[_rows(tm, d), _rows(tm, ff), _rows(tm, d), _rows(tm, d), _mods_spec(nct, groups, d)],
        out_shape=[jax.ShapeDtypeStruct((rows, d), F32), jax.ShapeDtypeStruct((rows, ff), BF16),
                   jax.ShapeDtypeStruct((rows, d), BF16), jax.ShapeDtypeStruct((rows, d), BF16),
                   jax.ShapeDtypeStruct((groups, 8, d), F32)],
        compiler_params=_params(),
    )(h1, dh2, u, o, mods, w1, w2)


def _div_tile(n, cap):
    if n <= cap:
        return n
    return max(t for t in range(LANES, cap + 1, LANES) if n % t == 0)


DW_TOKEN_TILE_CAP = 4224


def _mm_tn(a, b, name, *, relu2=False, col_blocks=1, after=None):
    rows, m = a.shape
    n = b.shape[1]
    tmm = min(m, 1024)
    tn = min(n // col_blocks, 2048)
    per_block = n // col_blocks // tn
    tr = _div_tile(rows, DW_TOKEN_TILE_CAP)
    tokens = [] if after is None else [after]

    def body(a_ref, b_ref, *rest):
        o_ref, acc_ref = rest[len(tokens):]
        r = pl.program_id(2)

        @pl.when(r == 0)
        def _():
            acc_ref[...] = jnp.zeros(acc_ref.shape, F32)

        av = a_ref[...]
        if relu2:
            av = jnp.square(jnp.maximum(av.astype(F32), 0.0)).astype(BF16)
        acc_ref[...] += _dot_tn(av, b_ref[...])

        @pl.when(r == pl.num_programs(2) - 1)
        def _():
            o_ref[...] = acc_ref[...].astype(BF16)

    return pl.pallas_call(
        body, name=name, grid=(m // tmm, n // tn, rows // tr),
        in_specs=[pl.BlockSpec((tr, tmm), lambda i, j, r: (r, i)), pl.BlockSpec((tr, tn), lambda i, j, r: (r, j))]
        + [pl.BlockSpec((8, LANES), lambda i, j, r: (0, 0))] * len(tokens),
        out_specs=pl.BlockSpec((None, tmm, tn), lambda i, j, r: (j // per_block, i, j % per_block)),
        out_shape=jax.ShapeDtypeStruct((col_blocks, m, n // col_blocks), BF16),
        scratch_shapes=[pltpu.VMEM((tmm, tn), F32)],
        compiler_params=_params(),
    )(a, b, *tokens)


def _halo_specs(tm, d, rows):
    per = tm // HALO
    prev = pl.BlockSpec((HALO, d), lambda i: (jnp.maximum(i * per - 1, 0), 0))
    nxt = pl.BlockSpec((HALO, d), lambda i: (jnp.minimum((i + 1) * per, rows // HALO - 1), 0))
    return prev, _rows(tm, d), nxt


def _segment_positions(i, tm, nct, groups, seg_lens):
    if groups == 1:
        start, length = 0, seg_lens[-1]
    else:
        start = jnp.where(i >= nct, nct, 0)
        length = jnp.where(i >= nct, seg_lens[1], seg_lens[0])
    rid = lax.broadcasted_iota(jnp.int32, (tm + 2 * HALO, 1), 0)
    pos = (i - start) * tm - HALO + rid
    return pos, length


def _window_count(pos, length, w):
    hi = jnp.minimum(pos + (w - w // 2), length)
    lo = jnp.maximum(pos - w // 2, 0)
    return (hi - lo).astype(F32)


def _window_sum(xg, w, lead):
    b, k = xg, 1
    while k < w:
        b = b + _shift_up(b, k)
        k *= 2
    return _shift_up(b, HALO - lead)[0:xg.shape[0] - 2 * HALO]


def _pooled(ext, md, pos, length, gw):
    tm = ext.shape[0] - 2 * HALO
    a_ext, parts = _normmod(ext, md, 0)
    valid = jnp.logical_and(pos >= 0, pos < length)
    a_ext = jnp.where(valid, a_ext, 0.0)
    pos_c = pos[HALO:HALO + tm]
    ps = []
    for g, w in enumerate(POOL_WINDOWS):
        xg = a_ext[:, g * gw:(g + 1) * gw]
        s = _window_sum(xg, w, w // 2)
        ps.append(s * (1.0 / _window_count(pos_c, length, w)) - xg[HALO:HALO + tm])
    return ps, parts


def _pool_fwd(h, mods, pw, pscale, layer, *, nct, tm, seg_lens):
    rows, d = h.shape
    groups = mods.shape[0]
    pg, gw = pw.shape[1], pw.shape[-1]

    def body(prev_ref, cur_ref, next_ref, md_ref, pw_ref, ps_ref, out_ref, p_ref):
        i = pl.program_id(0)
        md = md_ref[...]
        cur = cur_ref[...]
        ext = jnp.concatenate([prev_ref[...], cur, next_ref[...]], axis=0)
        pos, length = _segment_positions(i, tm, nct, groups, seg_lens)
        ps, _ = _pooled(ext, md, pos, length, gw)
        for g in range(pg):
            pb = ps[g].astype(BF16)
            p_ref[:, g * gw:(g + 1) * gw] = pb
            yg = _dot(pb, pw_ref[g]) * ps_ref[:, g * gw:(g + 1) * gw]
            out_ref[:, g * gw:(g + 1) * gw] = cur[:, g * gw:(g + 1) * gw] + md[G1:G1 + 1, g * gw:(g + 1) * gw] * yg

    j = layer // N_MIXERS
    return pl.pallas_call(
        body, name=f"pool_fwd_{layer}", grid=(rows // tm,),
        in_specs=[*_halo_specs(tm, d, rows), _mods_spec(nct, groups, d),
                  pl.BlockSpec((None, pg, gw, gw), lambda i: (j, 0, 0, 0)), _full((1, d))],
        out_specs=[_rows(tm, d), _rows(tm, d)],
        out_shape=[jax.ShapeDtypeStruct((rows, d), F32), jax.ShapeDtypeStruct((rows, d), BF16)],
        compiler_params=_params(),
    )(h, h, h, mods, pw, pscale[j:j + 1])


def _pool_bwd_weights(p, dh1, mods, pw, pscale, layer, *, nct, tm):
    rows, d = p.shape
    groups = mods.shape[0]
    pg, gw = pw.shape[1], pw.shape[-1]

    def body(p_ref, g_ref, md_ref, pw_ref, ps_ref, dp_ref, dmd_ref, dps_ref, dpw_ref):
        i = pl.program_id(0)
        md = md_ref[...]
        gup = g_ref[...]

        @pl.when(i == 0)
        def _():
            dps_ref[...] = jnp.zeros(dps_ref.shape, F32)
            dpw_ref[...] = jnp.zeros(dpw_ref.shape, F32)

        dg1 = []
        for g in range(pg):
            cols = slice(g * gw, (g + 1) * gw)
            pb = p_ref[:, cols]
            yp = _dot(pb, pw_ref[g])
            sc = ps_ref[:, cols]
            dg1.append(_rowsum(gup[:, cols] * (yp * sc)))
            dy = gup[:, cols] * md[G1:G1 + 1, cols]
            dps_ref[0:1, cols] += _rowsum(dy * yp)
            dyp = (dy * sc).astype(BF16)
            dp_ref[:, cols] = _dot_nt(dyp, pw_ref[g])
            dpw_ref[g] += _dot_tn(pb, dyp)
        _acc_rows(dmd_ref, _first_of_group(i, nct, groups), {G1: jnp.concatenate(dg1, axis=1)})

    j = layer // N_MIXERS
    return pl.pallas_call(
        body, name=f"pool_bwd_w_{layer}", grid=(rows // tm,),
        in_specs=[_rows(tm, d), _rows(tm, d), _mods_spec(nct, groups, d),
                  pl.BlockSpec((None, pg, gw, gw), lambda i: (j, 0, 0, 0)), _full((1, d))],
        out_specs=[_rows(tm, d), _mods_spec(nct, groups, d), _full((8, d)), _full((pg, gw, gw))],
        out_shape=[jax.ShapeDtypeStruct((rows, d), F32), jax.ShapeDtypeStruct((groups, 8, d), F32),
                   jax.ShapeDtypeStruct((8, d), F32), jax.ShapeDtypeStruct((pg, gw, gw), F32)],
        compiler_params=_params(),
    )(p, dh1, mods, pw, pscale[j:j + 1])


def _pool_bwd_input(dp, h, dh1, mods, layer, *, nct, tm, seg_lens, gw):
    rows, d = h.shape
    groups = mods.shape[0]

    def body(prev_ref, cur_ref, next_ref, h_ref, g_ref, md_ref, dh_ref, dmd_ref):
        i = pl.program_id(0)
        md = md_ref[...]
        dp_cur = cur_ref[...]
        ext = jnp.concatenate([prev_ref[...], dp_cur, next_ref[...]], axis=0)
        pos, length = _segment_positions(i, tm, nct, groups, seg_lens)
        valid = jnp.logical_and(pos >= 0, pos < length)
        das = []
        for g, w in enumerate(POOL_WINDOWS):
            cols = slice(g * gw, (g + 1) * gw)
            q = jnp.where(valid, ext[:, cols] * (1.0 / jnp.maximum(_window_count(pos, length, w), 1.0)), 0.0)
            das.append(_window_sum(q, w, w // 2 - 1) - dp_cur[:, cols])
        da = jnp.concatenate(das, axis=1)
        _, parts = _normmod(h_ref[...], md, 0)
        dx, dsh, dsc, dng = _normmod_bwd(da, parts, md, 0)
        dh_ref[...] = g_ref[...] + dx
        _acc_rows(dmd_ref, _first_of_group(i, nct, groups), {SH1: dsh, SC1: dsc, NG0: dng})

    return pl.pallas_call(
        body, name=f"pool_bwd_x_{layer}", grid=(rows // tm,),
        in_specs=[*_halo_specs(tm, d, rows), _rows(tm, d), _rows(tm, d), _mods_spec(nct, groups, d)],
        out_specs=[pl.BlockSpec((tm, d), lambda i: (jnp.maximum(i - nct, 0), 0)), _mods_spec(nct, groups, d)],
        out_shape=[jax.ShapeDtypeStruct((rows - nct * tm, d), F32), jax.ShapeDtypeStruct((groups, 8, d), F32)],
        compiler_params=_params(),
    )(dp, dp, dp, h, dh1, mods)


def _rope_tables(n_ctx, seq, hd):
    half = hd // 2
    n_rows = seq // GRID_W
    inv = np.float32(ROPE_BASE) ** (-np.arange(0, half, 2, dtype=np.float32) / np.float32(half))
    ar = np.arange(n_rows, dtype=np.float32)[:, None] * inv[None, :]
    ac = np.arange(GRID_W, dtype=np.float32)[:, None] * inv[None, :]

    def over_tokens(row_part, col_part):
        r = jnp.repeat(jnp.asarray(row_part, F32), GRID_W, axis=0)
        c = jnp.tile(jnp.asarray(col_part, F32), (n_rows, 1))
        return r, c

    cr, cc = over_tokens(np.cos(ar), np.cos(ac))
    sr, sc = over_tokens(np.sin(ar), np.sin(ac))
    cos = jnp.concatenate([cr, cr, cc, cc], axis=1)
    sin = jnp.concatenate([-sr, sr, -sc, sc], axis=1)
    cos = jnp.concatenate([jnp.ones((n_ctx, hd), F32), cos], axis=0)
    sin = jnp.concatenate([jnp.zeros((n_ctx, hd), F32), sin], axis=0)
    return cos, sin


def _rope_partner(x):
    hd = x.shape[-1]
    q = hd // 4
    lane = lax.broadcasted_iota(jnp.int32, x.shape, 1)
    first = (lane % (2 * q)) < q
    return jnp.where(first, pltpu.roll(x, hd - q, axis=1), pltpu.roll(x, q, axis=1))


def _qkv_fwd(h, mods, wqkv, cos, sin, gains, *, nh, nkv, nct, tm):
    rows, d = h.shape
    qw = wqkv.shape[0] * wqkv.shape[-1]
    hd = cos.shape[-1]

    def body(h_ref, md_ref, w_ref, cos_ref, sin_ref, gn_ref, xa_ref, qkv_ref, q_ref, k_ref, v_ref):
        a, _ = _normmod(h_ref[...], md_ref[...], 0)
        xa = a.astype(BF16)
        xa_ref[...] = xa
        qkv = _dot_blocks(xa, w_ref)
        qkv_ref[...] = qkv
        c, s = cos_ref[...], sin_ref[...]
        for hh in range(nh + nkv):
            xh = qkv[:, hh * hd:(hh + 1) * hd]
            xhat, _ = _rms_parts(xh)
            y = xhat * (gn_ref[0:1, :] if hh < nh else gn_ref[1:2, :])
            rot = (y * c + _rope_partner(y) * s).astype(BF16)
            if hh < nh:
                q_ref[:, hh * hd:(hh + 1) * hd] = rot
            else:
                k_ref[:, (hh - nh) * hd:(hh - nh + 1) * hd] = rot
        v_ref[...] = qkv[:, (nh + nkv) * hd:].astype(BF16)

    return pl.pallas_call(
        body, name="attn_qkv_fwd", grid=(rows // tm,),
        in_specs=[_rows(tm, d), _mods_spec(nct, 2, d), _full(wqkv.shape), _rows(tm, hd), _rows(tm, hd),
                  _full((8, hd))],
        out_specs=[_rows(tm, d), _rows(tm, qw), pl.BlockSpec((tm, nh * hd), lambda i: (jnp.maximum(i - nct, 0), 0)),
                   _rows(tm, nkv * hd), _rows(tm, nkv * hd)],
        out_shape=[jax.ShapeDtypeStruct((rows, d), BF16), jax.ShapeDtypeStruct((rows, qw), F32),
                   jax.ShapeDtypeStruct((rows - nct * tm, nh * hd), BF16),
                   jax.ShapeDtypeStruct((rows, nkv * hd), BF16), jax.ShapeDtypeStruct((rows, nkv * hd), BF16)],
        compiler_params=_params(),
    )(h, mods, wqkv, cos, sin, gains)


ATTN_Q_TILE_CAP = 1024
ATTN_KV_TILE_CAP = 4224
ATTN_ROW_GROUP = 256
LOG2E = 1.4426950408889634


def _attn_tiles(seq, total):
    tq = _div_tile(seq, ATTN_Q_TILE_CAP)
    return tq, _div_tile(total, ATTN_KV_TILE_CAP), min(ATTN_ROW_GROUP, tq)


def _flash_fwd(q, k, v, *, n_ctx, hd):
    total = k.shape[0]
    seq = total - n_ctx
    nkv = k.shape[1] // hd
    tq, tk, rg = _attn_tiles(seq, total)
    nk = total // tk
    scale = hd ** -0.5
    c2 = scale * LOG2E

    def body(q_ref, k_ref, v_ref, o_ref, lse_ref, m_sc, l_sc, acc_sc):
        ki = pl.program_id(2)

        @pl.when(ki == 0)
        def _():
            m_sc[...] = jnp.full(m_sc.shape, -jnp.inf, F32)
            l_sc[...] = jnp.zeros(l_sc.shape, F32)
            acc_sc[...] = jnp.zeros(acc_sc.shape, F32)

        kk, vv = k_ref[...], v_ref[...]
        groups = [(g, sub) for g in range(2) for sub in range(tq // rg)]

        def scores(g, sub):
            return _dot_nt(q_ref[sub * rg:(sub + 1) * rg, g * hd:(g + 1) * hd], kk)

        s_next = scores(*groups[0])
        for idx, (g, sub) in enumerate(groups):
            s = s_next
            if idx + 1 < len(groups):
                s_next = scores(*groups[idx + 1])
            rows = slice(g * tq + sub * rg, g * tq + (sub + 1) * rg)
            m_old = m_sc[rows]
            m_new = jnp.maximum(m_old, jnp.max(s, axis=-1, keepdims=True))
            alpha = jnp.exp2((m_old - m_new) * c2)
            p = jnp.exp2((s - m_new) * c2)
            l_sc[rows] = alpha * l_sc[rows] + jnp.sum(p, axis=-1, keepdims=True)
            acc_sc[rows] = alpha * acc_sc[rows] + _dot(p.astype(BF16), vv)
            m_sc[rows] = m_new

        @pl.when(ki == nk - 1)
        def _():
            o2 = acc_sc[...] / l_sc[...]
            lse = m_sc[...] * scale + jnp.log(l_sc[...])
            o_ref[:, :hd] = o2[:tq].astype(BF16)
            o_ref[:, hd:] = o2[tq:].astype(BF16)
            lse_ref[:, 0:1] = lse[:tq]
            lse_ref[:, 1:2] = lse[tq:]

    return pl.pallas_call(
        body, name="attn_flash_fwd", grid=(nkv, seq // tq, nk),
        in_specs=[pl.BlockSpec((tq, 2 * hd), lambda h, i, j: (i, h)),
                  pl.BlockSpec((tk, hd), lambda h, i, j: (j, h)),
                  pl.BlockSpec((tk, hd), lambda h, i, j: (j, h))],
        out_specs=[pl.BlockSpec((tq, 2 * hd), lambda h, i, j: (i, h)),
                   pl.BlockSpec((None, tq, 2), lambda h, i, j: (h, i, 0))],
        out_shape=[jax.ShapeDtypeStruct((seq, 2 * nkv * hd), BF16), jax.ShapeDtypeStruct((nkv, seq, 2), F32)],
        scratch_shapes=[pltpu.VMEM((2 * tq, 1), F32), pltpu.VMEM((2 * tq, 1), F32), pltpu.VMEM((2 * tq, hd), F32)],
        compiler_params=_params(),
    )(q, k, v)


def _flash_bwd(q, k, v, o, do, lse, *, n_ctx, hd):
    total = k.shape[0]
    seq = total - n_ctx
    nkv = k.shape[1] // hd
    tq, tk, rg = _attn_tiles(seq, total)
    scale = hd ** -0.5
    c2 = scale * LOG2E

    def body(q_ref, k_ref, v_ref, o_ref, do_ref, lse_ref, dq_ref, dk_ref, dv_ref):
        ki, qi = pl.program_id(1), pl.program_id(2)
        kk, vv = k_ref[...], v_ref[...]

        @pl.when(qi == 0)
        def _():
            dk_ref[...] = jnp.zeros(dk_ref.shape, F32)
            dv_ref[...] = jnp.zeros(dv_ref.shape, F32)

        dk_acc = jnp.zeros((tk, hd), F32)
        dv_acc = jnp.zeros((tk, hd), F32)
        for g in range(2):
            for sub in range(tq // rg):
                rs = slice(sub * rg, (sub + 1) * rg)
                cs = slice(g * hd, (g + 1) * hd)
                qq = q_ref[rs, cs]
                dd = do_ref[rs, cs]
                delta = jnp.sum(dd.astype(F32) * o_ref[rs, cs].astype(F32), axis=-1, keepdims=True)
                p = jnp.exp2(_dot_nt(qq, kk) * c2 - lse_ref[rs, g:g + 1] * LOG2E)
                dp = _dot_nt(dd, vv)
                ds = (p * (dp - delta) * scale).astype(BF16)
                dv_acc = dv_acc + _dot_tn(p.astype(BF16), dd)
                dk_acc = dk_acc + _dot_tn(ds, qq)
                dq = _dot(ds, kk)
                rows = pl.ds(pl.multiple_of(qi * tq, tq) + sub * rg, rg)

                @pl.when(ki == 0)
                def _():
                    dq_ref[rows, cs] = dq

                @pl.when(ki > 0)
                def _():
                    dq_ref[rows, cs] += dq
        dk_ref[...] += dk_acc
        dv_ref[...] += dv_acc

    return pl.pallas_call(
        body, name="attn_flash_bwd", grid=(nkv, total // tk, seq // tq),
        in_specs=[pl.BlockSpec((tq, 2 * hd), lambda h, j, i: (i, h)),
                  pl.BlockSpec((tk, hd), lambda h, j, i: (j, h)),
                  pl.BlockSpec((tk, hd), lambda h, j, i: (j, h)),
                  pl.BlockSpec((tq, 2 * hd), lambda h, j, i: (i, h)),
                  pl.BlockSpec((tq, 2 * hd), lambda h, j, i: (i, h)),
                  pl.BlockSpec((None, tq, 2), lambda h, j, i: (h, i, 0))],
        out_specs=[pl.BlockSpec((seq, 2 * hd), lambda h, j, i: (0, h)),
                   pl.BlockSpec((tk, hd), lambda h, j, i: (j, h)),
                   pl.BlockSpec((tk, hd), lambda h, j, i: (j, h))],
        out_shape=[jax.ShapeDtypeStruct((seq, 2 * nkv * hd), F32), jax.ShapeDtypeStruct((total, nkv * hd), F32),
                   jax.ShapeDtypeStruct((total, nkv * hd), F32)],
        compiler_params=_params(),
    )(q, k, v, o, do, lse)


def _proj_fwd(o, wo, hc, mods, *, n_ctx, tm):
    seq, d = o.shape
    off = n_ctx // tm

    def body(o_ref, w_ref, h_ref, md_ref, h1_ref, y_ref):
        y = _dot(o_ref[...], w_ref[...])
        y_ref[...] = y.astype(BF16)
        h1_ref[...] = h_ref[...] + md_ref[G1:G1 + 1, :] * y

    return pl.pallas_call(
        body, name="attn_proj_fwd", grid=(seq // tm,),
        in_specs=[_rows(tm, d), _full((d, d)),
                  pl.BlockSpec((tm, d), lambda i: (i + off, 0)), pl.BlockSpec((None, 8, d), lambda i: (1, 0, 0))],
        out_specs=[_rows(tm, d), _rows(tm, d)],
        out_shape=[jax.ShapeDtypeStruct((seq, d), F32), jax.ShapeDtypeStruct((seq, d), BF16)],
        compiler_params=_params(),
    )(o, wo, hc, mods)


def _proj_bwd(dh1, y, mods, wo, *, tm):
    seq, d = dh1.shape

    def body(g_ref, y_ref, md_ref, w_ref, do_ref, dyb_ref, dmd_ref):
        i = pl.program_id(0)
        g = g_ref[...]
        dyb = (g * md_ref[G1:G1 + 1, :]).astype(BF16)
        dyb_ref[...] = dyb
        do_ref[...] = _dot_nt(dyb, w_ref[...]).astype(BF16)
        _acc_rows(dmd_ref, i == 0, {G1: _rowsum(g * y_ref[...].astype(F32))})

    return pl.pallas_call(
        body, name="attn_proj_bwd", grid=(seq // tm,),
        in_specs=[_rows(tm, d), _rows(tm, d), pl.BlockSpec((None, 8, d), lambda i: (1, 0, 0)), _full((d, d))],
        out_specs=[_rows(tm, d), _rows(tm, d), pl.BlockSpec((None, 8, d), lambda i: (0, 0, 0))],
        out_shape=[jax.ShapeDtypeStruct((seq, d), BF16), jax.ShapeDtypeStruct((seq, d), BF16),
                   jax.ShapeDtypeStruct((1, 8, d), F32)],
        compiler_params=_params(),
    )(dh1, y, mods, wo)


def _qkv_bwd(qkv, dq, dk, dv, cos, sin, gains, *, nh, nkv, nct, tm):
    rows, qw = qkv.shape
    hd = cos.shape[-1]

    def body(qkv_ref, dq_ref, dk_ref, dv_ref, cos_ref, sin_ref, gn_ref, out_ref, dgn_ref):
        i = pl.program_id(0)
        c, s = cos_ref[...], sin_ref[...]
        is_lat = (i >= nct).astype(F32)
        dqg = jnp.zeros((1, hd), F32)
        dkg = jnp.zeros((1, hd), F32)
        for hh in range(nh + nkv):
            if hh < nh:
                dr = dq_ref[:, hh * hd:(hh + 1) * hd] * is_lat
                gn = gn_ref[0:1, :]
            else:
                dr = dk_ref[:, (hh - nh) * hd:(hh - nh + 1) * hd]
                gn = gn_ref[1:2, :]
            dy = dr * c + _rope_partner(dr * s)
            xhat, r = _rms_parts(qkv_ref[:, hh * hd:(hh + 1) * hd])
            dgh = _rowsum(dy * xhat)
            if hh < nh:
                dqg = dqg + dgh
            else:
                dkg = dkg + dgh
            dxhat = dy * gn
            dx = r * (dxhat - xhat * jnp.mean(dxhat * xhat, axis=-1, keepdims=True))
            out_ref[:, hh * hd:(hh + 1) * hd] = dx.astype(BF16)
        out_ref[:, (nh + nkv) * hd:] = dv_ref[...].astype(BF16)
        _acc_rows(dgn_ref, i == 0, {0: dqg, 1: dkg})

    return pl.pallas_call(
        body, name="attn_qkv_bwd", grid=(rows // tm,),
        in_specs=[_rows(tm, qw), pl.BlockSpec((tm, nh * hd), lambda i: (jnp.maximum(i - nct, 0), 0)),
                  _rows(tm, nkv * hd), _rows(tm, nkv * hd), _rows(tm, hd), _rows(tm, hd), _full((8, hd))],
        out_specs=[_rows(tm, qw), _full((8, hd))],
        out_shape=[jax.ShapeDtypeStruct((rows, qw), BF16), jax.ShapeDtypeStruct((8, hd), F32)],
        compiler_params=_params(),
    )(qkv, dq, dk, dv, cos, sin, gains)


def _attn_in_bwd(dqkv, wqkv, hc, dh1, mods, *, nct, tm):
    rows, d = hc.shape
    qw = dqkv.shape[1]

    def body(dz_ref, w_ref, h_ref, g_ref, md_ref, dh_ref, dmd_ref):
        i = pl.program_id(0)
        md = md_ref[...]
        da = _dot_nt_blocks(dz_ref[...], w_ref)
        _, parts = _normmod(h_ref[...], md, 0)
        dx, dsh, dsc, dng = _normmod_bwd(da, parts, md, 0)
        dh_ref[...] = g_ref[...] * (i >= nct).astype(F32) + dx
        _acc_rows(dmd_ref, _first_of_group(i, nct, 2), {SH1: dsh, SC1: dsc, NG0: dng})

    return pl.pallas_call(
        body, name="attn_in_bwd", grid=(rows // tm,),
        in_specs=[_rows(tm, qw), _full(wqkv.shape), _rows(tm, d),
                  pl.BlockSpec((tm, d), lambda i: (jnp.maximum(i - nct, 0), 0)), _mods_spec(nct, 2, d)],
        out_specs=[_rows(tm, d), _mods_spec(nct, 2, d)],
        out_shape=[jax.ShapeDtypeStruct((rows, d), F32), jax.ShapeDtypeStruct((2, 8, d), F32)],
        compiler_params=_params(),
    )(dqkv, wqkv, hc, dh1, mods)


def _gmlp_gate(z, lng, lnb, ws_ref, bs_ref, gg, ch):
    half = z.shape[1] // 2
    ggw = half // gg
    u, v = z[:, :half], z[:, half:]
    vc = v - jnp.mean(v, axis=-1, keepdims=True)
    rs = lax.rsqrt(jnp.mean(vc * vc, axis=-1, keepdims=True) + EPS)
    vhat = vc * rs
    vln = (vhat * lng + lnb).astype(BF16)
    chunks = []
    for n in range(z.shape[0] // ch):
        groups = []
        for g in range(gg):
            groups.append(_dot(ws_ref[g], vln[n * ch:(n + 1) * ch, g * ggw:(g + 1) * ggw]) + bs_ref[g])
        chunks.append(jnp.concatenate(groups, axis=1))
    sv = jnp.concatenate(chunks, axis=0) if len(chunks) > 1 else chunks[0]
    return u, sv, vhat, rs, vln


def _gmlp_fwd(h, mods, w_in, lng, lnb, ws, bs, w_out, *, tm):
    seq, d = h.shape
    zw = w_in.shape[0] * w_in.shape[-1]
    half = zw // 2
    gg, ch = ws.shape[0], ws.shape[-1]

    def body(h_ref, md_ref, win_ref, lng_ref, lnb_ref, ws_ref, bs_ref, wout_ref, h1_ref, zp_ref, y_ref):
        x = h_ref[...]
        md = md_ref[...]
        a, _ = _normmod(x, md, 0)
        zp = _dot_blocks(a.astype(BF16), win_ref)
        zp_ref[...] = zp.astype(BF16)
        u, sv, _, _, _ = _gmlp_gate(_gelu(zp), lng_ref[...], lnb_ref[...], ws_ref, bs_ref, gg, ch)
        y = _dot((u * sv).astype(BF16), wout_ref[...])
        y_ref[...] = y.astype(BF16)
        h1_ref[...] = x + md[G1:G1 + 1] * y

    return pl.pallas_call(
        body, name="gmlp_fwd", grid=(seq // tm,),
        in_specs=[_rows(tm, d), pl.BlockSpec((None, 8, d), lambda i: (1, 0, 0)),
                  _full(w_in.shape), _full((1, half)), _full((1, half)),
                  _full((gg, ch, ch)), _full((gg, ch, 1)), _full((half, d))],
        out_specs=[_rows(tm, d), _rows(tm, zw), _rows(tm, d)],
        out_shape=[jax.ShapeDtypeStruct((seq, d), F32), jax.ShapeDtypeStruct((seq, zw), BF16),
                   jax.ShapeDtypeStruct((seq, d), BF16)],
        compiler_params=_params(),
    )(h, mods, w_in, lng, lnb, ws, bs, w_out)


def _gmlp_bwd(h, dh1, zpre, y, mods, w_in, lng, lnb, ws, ws_t, bs, w_out, *, tm):
    seq, d = h.shape
    zw = w_in.shape[0] * w_in.shape[-1]
    half = zw // 2
    gg, ch = ws.shape[0], ws.shape[-1]
    ggw = half // gg

    def body(h_ref, g_ref, zp_ref, y_ref, md_ref, win_ref, lng_ref, lnb_ref, ws_ref, wst_ref, bs_ref, wout_ref,
             dh_ref, dzp_ref, gated_ref, dyb_ref, ab_ref, dmd_ref, dln_ref, dws_ref, dbs_ref):
        i = pl.program_id(0)
        x = h_ref[...]
        md = md_ref[...]
        a, parts = _normmod(x, md, 0)
        ab_ref[...] = a.astype(BF16)
        zp = zp_ref[...].astype(F32)
        lng_v = lng_ref[...]
        u, sv, vhat, rs, vln = _gmlp_gate(_gelu(zp), lng_v, lnb_ref[...], ws_ref, bs_ref, gg, ch)
        g = g_ref[...]
        dg1 = _rowsum(g * y_ref[...].astype(F32))
        dyb = (g * md[G1:G1 + 1]).astype(BF16)
        dyb_ref[...] = dyb
        gated_ref[...] = (u * sv).astype(BF16)
        dgated = _dot_nt(dyb, wout_ref[...])
        du = dgated * sv
        dsv = dgated * u

        @pl.when(i == 0)
        def _():
            dws_ref[...] = jnp.zeros(dws_ref.shape, F32)
            dbs_ref[...] = jnp.zeros(dbs_ref.shape, F32)
            dln_ref[...] = jnp.zeros(dln_ref.shape, F32)

        chunks = []
        for n in range(tm // ch):
            groups = []
            for gi in range(gg):
                blk = dsv[n * ch:(n + 1) * ch, gi * ggw:(gi + 1) * ggw]
                dbs_ref[gi] += jnp.sum(blk, axis=-1, keepdims=True)
                blk_b = blk.astype(BF16)
                dws_ref[gi] += _dot_nt(blk_b, vln[n * ch:(n + 1) * ch, gi * ggw:(gi + 1) * ggw])
                groups.append(_dot(wst_ref[gi], blk_b))
            chunks.append(jnp.concatenate(groups, axis=1))
        dvln = jnp.concatenate(chunks, axis=0) if len(chunks) > 1 else chunks[0]
        dln_ref[0:1, :] += _rowsum(dvln * vhat)
        dln_ref[1:2, :] += _rowsum(dvln)
        dvhat = dvln * lng_v
        dv = rs * (dvhat - jnp.mean(dvhat, axis=-1, keepdims=True)
                   - vhat * jnp.mean(dvhat * vhat, axis=-1, keepdims=True))
        dzp = (jnp.concatenate([du, dv], axis=1) * _gelu_grad(zp)).astype(BF16)
        dzp_ref[...] = dzp
        da = _dot_nt_blocks(dzp, win_ref)
        dx, dsh, dsc, dng = _normmod_bwd(da, parts, md, 0)
        dh_ref[...] = g + dx
        _acc_rows(dmd_ref, i == 0, {SH1: dsh, SC1: dsc, G1: dg1, NG0: dng})

    return pl.pallas_call(
        body, name="gmlp_bwd", grid=(seq // tm,),
        in_specs=[_rows(tm, d), _rows(tm, d), _rows(tm, zw), _rows(tm, d),
                  pl.BlockSpec((None, 8, d), lambda i: (1, 0, 0)),
                  _full(w_in.shape), _full((1, half)), _full((1, half)),
                  _full((gg, ch, ch)), _full((gg, ch, ch)), _full((gg, ch, 1)), _full((half, d))],
        out_specs=[_rows(tm, d), _rows(tm, zw), _rows(tm, half), _rows(tm, d), _rows(tm, d),
                   pl.BlockSpec((None, 8, d), lambda i: (0, 0, 0)), _full((8, half)), _full((gg, ch, ch)),
                   _full((gg, ch, 1))],
        out_shape=[jax.ShapeDtypeStruct((seq, d), F32), jax.ShapeDtypeStruct((seq, zw), BF16),
                   jax.ShapeDtypeStruct((seq, half), BF16), jax.ShapeDtypeStruct((seq, d), BF16),
                   jax.ShapeDtypeStruct((seq, d), BF16), jax.ShapeDtypeStruct((1, 8, d), F32),
                   jax.ShapeDtypeStruct((8, half), F32), jax.ShapeDtypeStruct((gg, ch, ch), F32),
                   jax.ShapeDtypeStruct((gg, ch, 1), F32)],
        compiler_params=_params(),
    )(h, dh1, zpre, y, mods, w_in, lng, lnb, ws, ws_t, bs, w_out)


def _final_loss(h, tgt, fg, *, tm):
    seq, d = h.shape

    def body(h_ref, t_ref, g_ref, dh_ref, acc_ref):
        i = pl.program_id(0)
        gain = g_ref[...]
        xhat, r = _rms_parts(h_ref[...])
        err = xhat * gain - t_ref[...]
        dy = err * (1.0 / d)
        dxhat = dy * gain
        dh_ref[...] = r * (dxhat - xhat * jnp.mean(dxhat * xhat, axis=-1, keepdims=True))
        part = jnp.sum(_rowsum(err * err), axis=-1, keepdims=True) * (0.5 / d)
        _acc_rows(acc_ref, i == 0, {0: _rowsum(dy * xhat), 1: jnp.broadcast_to(part, (1, d))})

    return pl.pallas_call(
        body, name="final_loss", grid=(seq // tm,),
        in_specs=[_rows(tm, d), _rows(tm, d), _full((1, d))],
        out_specs=[_rows(tm, d), _full((8, d))],
        out_shape=[jax.ShapeDtypeStruct((seq, d), F32), jax.ShapeDtypeStruct((8, d), F32)],
        compiler_params=_params(),
    )(h, tgt, fg)


def _ada_fwd(c_all, ada_w, ada_b_cols):
    depth, d, ncs = ada_w.shape

    def body(c_ref, w_ref, b_ref, o_ref):
        s = _silu(c_ref[...]).astype(BF16)
        o_ref[...] = _dot(s, w_ref[...].astype(BF16)) + b_ref[...]

    return pl.pallas_call(
        body, name="ada_fwd", grid=(depth,),
        in_specs=[_full((16, d)), pl.BlockSpec((None, d, ncs), lambda i: (i, 0, 0)),
                  pl.BlockSpec((None, 1, ncs), lambda i: (i, 0, 0))],
        out_specs=pl.BlockSpec((None, 16, ncs), lambda i: (i, 0, 0)),
        out_shape=jax.ShapeDtypeStruct((depth, 16, ncs), F32),
        compiler_params=_params(),
    )(c_all, ada_w, ada_b_cols.reshape(depth, 1, ncs))


def _ada_bwd(c_all, c_all_t, dmod, ada_w):
    depth, d, ncs = ada_w.shape

    def body(c_ref, ct_ref, dm_ref, w_ref, gw_ref, dc_ref):
        i = pl.program_id(0)
        dm = dm_ref[...]
        dctx = _rowsum(dm[8:16])
        rid = lax.broadcasted_iota(jnp.int32, (8, ncs), 0)
        low = jnp.where(rid == 0, jnp.broadcast_to(dctx, (8, ncs)), 0.0)
        dm16 = jnp.concatenate([dm[0:8], low], axis=0).astype(BF16)
        gw_ref[...] = _dot(_silu(ct_ref[...]).astype(BF16), dm16)

        @pl.when(i == 0)
        def _():
            dc_ref[...] = jnp.zeros(dc_ref.shape, F32)

        dc_ref[...] += _dot_nt(low.astype(BF16), w_ref[...].astype(BF16)) * _silu_grad(c_ref[8:9, :])

    return pl.pallas_call(
        body, name="ada_bwd", grid=(depth,),
        in_specs=[_full((16, d)), _full((d, 16)), pl.BlockSpec((None, 16, ncs), lambda i: (i, 0, 0)),
                  pl.BlockSpec((None, d, ncs), lambda i: (i, 0, 0))],
        out_specs=[pl.BlockSpec((None, d, ncs), lambda i: (i, 0, 0)), _full((8, d))],
        out_shape=[jax.ShapeDtypeStruct((depth, d, ncs), F32), jax.ShapeDtypeStruct((8, d), F32)],
        compiler_params=_params(),
    )(c_all, c_all_t, dmod, ada_w)


def _adamw_math(w, g, m, v):
    m = ADAM_B1 * m + (1.0 - ADAM_B1) * g
    v = ADAM_B2 * v + (1.0 - ADAM_B2) * jnp.square(g)
    m_hat = m * (1.0 / (1.0 - ADAM_B1 ** ADAM_STEP))
    v_hat = v * (1.0 / (1.0 - ADAM_B2 ** ADAM_STEP))
    delta = -ADAM_LR * (m_hat / (jnp.sqrt(v_hat) + ADAM_EPS) + ADAM_WD * w)
    return delta, m, v


def _adamw(ga, gb, w, m, v, name):
    rows, cols = w.shape
    tr = rows
    while tr * cols * 4 > (1 << 20) and tr % 16 == 0:
        tr //= 2
    grads = [ga] if gb is None else [ga, gb]

    def body(*refs):
        w_ref, m_ref, v_ref, g_out, d_out, m_out, v_out = refs[len(grads):]
        g = refs[0][...] if gb is None else refs[0][...] + refs[1][...]
        delta, m_new, v_new = _adamw_math(w_ref[...], g, m_ref[...], v_ref[...])
        g_out[...] = g
        d_out[...] = delta
        m_out[...] = m_new
        v_out[...] = v_new

    spec = _rows(tr, cols)
    return pl.pallas_call(
        body, name=name, grid=(rows // tr,),
        in_specs=[spec] * (len(grads) + 3), out_specs=[spec] * 4,
        out_shape=[jax.ShapeDtypeStruct((rows, cols), F32)] * 4,
        compiler_params=_params(),
    )(*grads, w, m, v)


def _sum_devices(gathered, name):
    n, rows, cols = gathered.shape
    tr = rows
    while tr * cols * 4 * n > (4 << 20) and tr % 16 == 0:
        tr //= 2

    def body(x_ref, o_ref):
        acc = x_ref[0]
        for j in range(1, n):
            acc = acc + x_ref[j]
        o_ref[...] = acc

    return pl.pallas_call(
        body, name=name, grid=(rows // tr,),
        in_specs=[pl.BlockSpec((n, tr, cols), lambda i: (0, i, 0))], out_specs=_rows(tr, cols),
        out_shape=jax.ShapeDtypeStruct((rows, cols), F32),
        compiler_params=_params(),
    )(gathered)


def _sum_partials(blocked, landeds, chip, name):
    n = len(blocked)
    cols = blocked[0].shape[-1]
    blocked = [b.reshape(N_CHIPS, -1, cols) for b in blocked]
    landeds = [l.reshape(3, -1, cols) for l in landeds]
    rows = blocked[0].shape[1]
    tr = rows
    while tr * cols * 2 * n > (1 << 20) and tr % 32 == 0:
        tr //= 2

    def body(chip_ref, *refs):
        out_ref = refs[-1]
        for li in range(n):
            acc = refs[li][...].astype(F32)
            for p in range(3):
                acc = acc + refs[n + li][p].astype(F32)
            out_ref[li] = acc

    out = pl.pallas_call(
        body, name=name,
        grid_spec=pltpu.PrefetchScalarGridSpec(
            num_scalar_prefetch=1, grid=(rows // tr,),
            in_specs=[pl.BlockSpec((None, tr, cols), lambda i, k: (k[0], i, 0))] * n
            + [pl.BlockSpec((3, tr, cols), lambda i, k: (0, i, 0))] * n,
            out_specs=pl.BlockSpec((n, tr, cols), lambda i, k: (0, i, 0))),
        out_shape=jax.ShapeDtypeStruct((n, rows, cols), F32),
        compiler_params=_params(),
    )(jnp.reshape(chip, (1,)).astype(jnp.int32), *blocked, *landeds)
    return out.reshape(n * rows, cols)


def _my_place():
    return lax.axis_index("x"), lax.axis_index("y"), lax.axis_index("c")


def _other_chips(x, y):
    return [(1 - x, y), (x, 1 - y), (1 - x, 1 - y)]


def _all_gather_small(block, name):
    rows, cols = block.shape

    def body(x_ref, out_ref, send_sems, recv_sems, local_sem):
        x, y, c = _my_place()
        me, sibling = (x, y, c), (x, y, 1 - c)
        chips = _other_chips(x, y)

        def slot(px, py, pc):
            return out_ref.at[4 * px + 2 * py + pc]

        def copy(k, blk, to, src=None):
            return pltpu.make_async_remote_copy(
                src_ref=slot(*blk) if src is None else src, dst_ref=slot(*blk),
                send_sem=send_sems.at[k], recv_sem=recv_sems.at[k], device_id=to, device_id_type=MESH)

        mine = pltpu.make_async_copy(x_ref, slot(*me), local_sem)
        mine.start()
        first = [copy(0, me, sibling, src=x_ref)]
        first += [copy(1 + j, me, (*chip, c), src=x_ref) for j, chip in enumerate(chips)]
        for cp in first:
            cp.start()
        passed = [copy(4 + j, (*chip, c), sibling) for j, chip in enumerate(chips)]
        for j, chip in enumerate(chips):
            copy(1 + j, (*chip, c), me).wait_recv()
            passed[j].start()
        copy(0, sibling, me).wait_recv()
        for j, chip in enumerate(chips):
            copy(4 + j, (*chip, 1 - c), me).wait_recv()
        for cp in first + passed:
            cp.wait_send()
        mine.wait()

    return pl.pallas_call(
        body, name=name,
        out_shape=jax.ShapeDtypeStruct((N_DEV, rows, cols), block.dtype),
        in_specs=[pl.BlockSpec(memory_space=pltpu.VMEM)],
        out_specs=pl.BlockSpec(memory_space=pltpu.VMEM),
        scratch_shapes=[pltpu.SemaphoreType.DMA((7,)), pltpu.SemaphoreType.DMA((7,)), pltpu.SemaphoreType.DMA],
        compiler_params=_params(),
    )(block)


HBM_SPEC = pl.BlockSpec(memory_space=pltpu.HBM)
SEM_SPEC = pl.BlockSpec(memory_space=pltpu.SEMAPHORE)
DATAFLOW_EFFECT = pltpu.SideEffectType.DATAFLOW_SIDE_EFFECTING


def _same_core_of_other_chips(x, y, c):
    return [(*chip, c) for chip in _other_chips(x, y)]


def _sibling_core(x, y, c):
    return [(x, y, 1 - c)]


def _gather_views(src, land, p, x, y):
    return src, land.at[2 * x + y]


def _scatter_views(src, land, p, x, y):
    peer_chip = (2 * (1 - x) + y, 2 * x + (1 - y), 2 * (1 - x) + (1 - y))[p]
    return src.at[peer_chip], land.at[p]


def _whole_views(src, land, p, x, y):
    return src, land


GATHER_PLAN = (_same_core_of_other_chips, _gather_views, 3)
SCATTER_PLAN = (_same_core_of_other_chips, _scatter_views, 3)
SIBLING_PLAN = (_sibling_core, _whole_views, 1)


def _exchange_copies(srcs, lands, send_sems, recv_sems, plan):
    peers_of, views, n_peers = plan
    x, y, c = _my_place()
    copies = []
    for j, (src, land) in enumerate(zip(srcs, lands)):
        for p, peer in enumerate(peers_of(x, y, c)):
            s_view, d_view = views(src, land, p, x, y)
            k = n_peers * j + p
            copies.append(pltpu.make_async_remote_copy(
                src_ref=s_view, dst_ref=d_view, send_sem=send_sems.at[k], recv_sem=recv_sems.at[k],
                device_id=peer, device_id_type=MESH))
    return copies


def _exchange_start(srcs, lands, plan, name):
    n = len(srcs)

    def body(*refs):
        send_sems, recv_sems = refs[2 * n], refs[2 * n + 1]
        token = refs[-1]
        for cp in _exchange_copies(refs[:n], refs[n:2 * n], send_sems, recv_sems, plan):
            cp.start()
        token[...] = jnp.zeros(token.shape, token.dtype)

    operands = [pltpu.with_memory_space_constraint(a, pltpu.HBM) for a in (*srcs, *lands)]
    out = pl.pallas_call(
        body, name=name,
        out_shape=(pltpu.SemaphoreType.DMA((plan[2] * n,)), pltpu.SemaphoreType.DMA((plan[2] * n,)),
                   *[pltpu.HBM(a.shape, a.dtype) for a in operands], jax.ShapeDtypeStruct((8, LANES), F32)),
        in_specs=[HBM_SPEC] * (2 * n),
        out_specs=(SEM_SPEC, SEM_SPEC, *[HBM_SPEC] * (2 * n), pl.BlockSpec(memory_space=pltpu.VMEM)),
        input_output_aliases={i: 2 + i for i in range(2 * n)},
        compiler_params=pltpu.CompilerParams(has_side_effects=DATAFLOW_EFFECT),
    )(*operands)
    return out[0], out[1], list(out[2:2 + n]), list(out[2 + n:2 + 2 * n]), out[-1]


def _exchange_wait(send_sems, recv_sems, srcs, lands, plan, after, name):
    n = len(srcs)

    def body(*refs):
        send, recv = refs[2 * n], refs[2 * n + 1]
        for cp in _exchange_copies(refs[:n], refs[n:2 * n], send, recv, plan):
            cp.wait_send()
            cp.wait_recv()

    out = pl.pallas_call(
        body, name=name,
        out_shape=tuple(pltpu.HBM(a.shape, a.dtype) for a in (*srcs, *lands)),
        in_specs=[HBM_SPEC] * (2 * n) + [SEM_SPEC, SEM_SPEC, HBM_SPEC],
        out_specs=tuple([HBM_SPEC] * (2 * n)),
        input_output_aliases={i: i for i in range(2 * n)},
        compiler_params=pltpu.CompilerParams(has_side_effects=DATAFLOW_EFFECT),
    )(*srcs, *lands, send_sems, recv_sems, pltpu.with_memory_space_constraint(after, pltpu.HBM))
    return list(out[:n]), list(out[n:])


def _landing_for_gather(shard, chip):
    land = lax.empty((N_CHIPS, *shard.shape), shard.dtype)
    return lax.dynamic_update_index_in_dim(land, shard, chip, 0)


TILE_ELEMS = SUBLANES * LANES


def _pack(arrays):
    parts = []
    for a in arrays:
        flat = a.reshape(-1).astype(F32)
        pad = (-flat.shape[0]) % TILE_ELEMS
        if pad:
            flat = jnp.concatenate([flat, jnp.zeros((pad,), F32)])
        parts.append(flat.reshape(-1, LANES))
    return jnp.concatenate(parts, axis=0) if len(parts) > 1 else parts[0]


def _unpack(buf, shapes):
    out, r = [], 0
    lead = buf.shape[:-2]
    for shp in shapes:
        size = math.prod(shp)
        nr = -(-size // TILE_ELEMS) * SUBLANES
        flat = buf[..., r:r + nr, :].reshape(*lead, nr * LANES)[..., :size]
        out.append(flat.reshape(*lead, *shp))
        r += nr
    return out


def _chip_cols(a, k, width):
    return lax.dynamic_slice_in_dim(a, k * width, width, axis=a.ndim - 1)


def _across_chips(gathered, c0_only_shape):
    return gathered.reshape(2, 2, 2, *c0_only_shape)[:, :, 0].reshape(N_CHIPS, *c0_only_shape)


def kernel(x, c, ctx, c_ctx, ada_w, ada_b, norm_g, mlp_w1, mlp_w2, pool_w, pool_scale, attn_w_qkv, attn_w_o, attn_q_g, attn_k_g, gm_w_in, gm_ln_g, gm_ln_b, gm_ws, gm_bs, gm_w_out, final_g, loss_target, m_c_ctx, m_ada_w, m_ada_b, m_norm_g, m_mlp_w1, m_mlp_w2, m_pool_w, m_pool_scale, m_attn_w_qkv, m_attn_w_o, m_attn_q_g, m_attn_k_g, m_gm_w_in, m_gm_ln_g, m_gm_ln_b, m_gm_ws, m_gm_bs, m_gm_w_out, m_final_g, v_c_ctx, v_ada_w, v_ada_b, v_norm_g, v_mlp_w1, v_mlp_w2, v_pool_w, v_pool_scale, v_attn_w_qkv, v_attn_w_o, v_attn_q_g, v_attn_k_g, v_gm_w_in, v_gm_ln_g, v_gm_ln_b, v_gm_ws, v_gm_bs, v_gm_w_out, v_final_g):
    seq, d = x.shape[1], x.shape[2]
    n_ctx = ctx.shape[1]
    total = n_ctx + seq
    hd = attn_q_g.shape[-1]
    nh = d // hd
    nkv = nh // 2
    gg, ch = gm_ws.shape[1], gm_ws.shape[-1]
    half = gm_w_out.shape[1] * N_CHIPS
    pgw = pool_w.shape[-1]
    tm = min(256, n_ctx)
    nct = n_ctx // tm
    seg_lens = (n_ctx, seq)

    mx, my, mc = _my_place()
    chip = 2 * mx + my
    me = 4 * mx + 2 * my + mc

    c_rows = jnp.concatenate([c, jnp.zeros((7, d), F32)], axis=0)
    c_gath = _all_gather_small(c_rows, "gather_cond")[:, 0, :]
    c_all = jnp.concatenate([c_gath, c_ctx[None, :], jnp.zeros((7, d), F32)], axis=0)
    ncs = ada_w.shape[-1]
    ada_cols = _ada_fwd(c_all, ada_w, _chip_cols(ada_b, chip, ncs))
    small_shapes = [ada_cols.shape, norm_g.shape, pool_scale.shape, gm_ln_g.shape, gm_ln_b.shape]
    gathered = _all_gather_small(_pack([ada_cols, norm_g, pool_scale, gm_ln_g, gm_ln_b]), "gather_small_params")
    per_chip = _across_chips(gathered, gathered.shape[1:])
    ada_g, ng_g, ps_g, lng_g, lnb_g = _unpack(per_chip, small_shapes)

    def join_last(a):
        return jnp.moveaxis(a, 0, -2).reshape(*a.shape[1:-1], N_CHIPS * a.shape[-1])

    ada_full = join_last(ada_g)
    ng_full = join_last(ng_g)
    ps_full = join_last(ps_g)
    lng_full = join_last(lng_g)
    lnb_full = join_last(lnb_g)
    mod_lat = lax.dynamic_slice_in_dim(ada_full, me, 1, axis=1).reshape(DEPTH, 6, d)
    mod_ctx = ada_full[:, 8].reshape(DEPTH, 6, d)
    mods = jnp.stack([jnp.concatenate([mod_ctx, ng_full], axis=1), jnp.concatenate([mod_lat, ng_full], axis=1)],
                     axis=1)

    weight_groups = [
        [pool_w],
        [mlp_w1[0], mlp_w2[0]],
        [attn_w_qkv[0], attn_w_o[0]],
        [mlp_w1[1], mlp_w2[1], mlp_w1[2], mlp_w2[2], gm_w_in[0], gm_w_out[0], mlp_w1[3], mlp_w2[3]],
    ]
    gathers = [None] * len(weight_groups)

    def gather_start(gi, after):
        shards, _ = lax.optimization_barrier(([w.astype(BF16) for w in weight_groups[gi]], after))
        lands = [_landing_for_gather(s, chip) for s in shards]
        gathers[gi] = _exchange_start(shards, lands, GATHER_PLAN, f"gather_weights_{gi}_start")
        return gathers[gi][4][0:1, 0:1]

    def gathered(gi, after):
        send, recv, srcs, lands, _ = gathers[gi]
        return _exchange_wait(send, recv, srcs, lands, GATHER_PLAN, after, f"gather_weights_{gi}_wait")[1]

    def rows_joined(a):
        return a.reshape(-1, a.shape[-1])

    w1_b, w2_b = [None] * DEPTH, [None] * DEPTH
    gather_start(0, mods)
    pw_land, = gathered(0, ps_full)
    behind_gather_1 = gather_start(1, pw_land)
    pw_f = jnp.transpose(pw_land, (1, 2, 0, 3, 4)).reshape(pool_w.shape[0], pool_w.shape[1], pgw, pgw)

    gains = jnp.concatenate([attn_q_g, attn_k_g, jnp.zeros((6, hd), F32)], axis=0)
    ws_b = gm_ws[0].astype(BF16)
    ws_t = jnp.swapaxes(gm_ws[0], 1, 2).astype(BF16)
    bs_col = gm_bs[0][:, :, None]
    cos, sin = _rope_tables(n_ctx, seq, hd)
    lat = lambda i: mods[i, 1:2]

    hc0 = jnp.concatenate([ctx[0] + behind_gather_1, x[0]], axis=0)
    ha0, p0 = _pool_fwd(hc0, mods[0] + behind_gather_1, pw_f, ps_full, 0, nct=nct, tm=tm, seg_lens=seg_lens)
    w1_b[0], w2_b[0] = gathered(1, ha0)
    mods0 = mods[0] + gather_start(2, w1_b[0])
    hc1, u0, o0 = _mlp_fwd(ha0, mods0, w1_b[0], w2_b[0], 0, nct=nct, tm=tm)
    wqkv_b, wo_land = gathered(2, hc1)
    mods1 = mods[1] + gather_start(3, wqkv_b)
    wo_f = rows_joined(wo_land)
    xa1, qkv, q_r, k_r, v_b = _qkv_fwd(hc1, mods1, wqkv_b, cos, sin, gains, nh=nh, nkv=nkv, nct=nct, tm=tm)
    o_att, lse = _flash_fwd(q_r, k_r, v_b, n_ctx=n_ctx, hd=hd)
    ha1, y1 = _proj_fwd(o_att, wo_f, hc1, mods1, n_ctx=n_ctx, tm=tm)
    w1_b[1], w2_b[1], w1_b[2], w2_b[2], win_b, wout_land, w1_b[3], w2_b[3] = gathered(3, ha1)
    h2, u1, o1 = _mlp_fwd(ha1, lat(1), w1_b[1], w2_b[1], 1, nct=0, tm=tm)
    wout_f = rows_joined(wout_land)
    ha2, zpre, y2 = _gmlp_fwd(h2, mods[2], win_b, lng_full, lnb_full, ws_b, bs_col, wout_f, tm=tm)
    h3, u2, o2 = _mlp_fwd(ha2, lat(2), w1_b[2], w2_b[2], 2, nct=0, tm=tm)
    ha3, p3 = _pool_fwd(h3, lat(3), pw_f, ps_full, 3, nct=0, tm=tm, seg_lens=seg_lens)
    h4, u3, o3 = _mlp_fwd(ha3, lat(3), w1_b[3], w2_b[3], 3, nct=0, tm=tm)
    dh4, fin_acc = _final_loss(h4, loss_target[0], final_g[None, :], tm=tm)

    dmods = [None] * DEPTH
    scatters = [None] * (DEPTH + 2)

    def blocked_rows(g):
        return g.reshape(N_CHIPS, g.shape[1] // N_CHIPS, g.shape[2])

    def blocked_pool(dpw):
        pg = dpw.shape[0]
        return jnp.transpose(dpw.astype(BF16).reshape(pg, N_CHIPS, pgw // N_CHIPS, pgw), (1, 0, 2, 3))

    def scatter_start(i, grads):
        lands = [lax.empty((3, *g.shape[1:]), g.dtype) for g in grads]
        scatters[i] = _exchange_start(grads, lands, SCATTER_PLAN, f"scatter_grads_{i}_start")
        return scatters[i][4][0:1, 0:1]

    def mlp_back(i, h_in, dh_out, u, o, md, n_ct):
        dh_in, du, dob, mb, dmd = _mlp_bwd(h_in, dh_out, u, o, md, w1_b[i], w2_b[i], i, nct=n_ct, tm=tm)
        dw1 = _mm_tn(mb, du, f"mlp_dw1_{i}", col_blocks=N_CHIPS)
        dw2 = blocked_rows(_mm_tn(u, dob, f"mlp_dw2_{i}", relu2=True))
        return dh_in, dmd, [dw1, dw2]

    def pool_back(i, h_in, p_in, dh_out, md, n_ct):
        dp, dmd_a, dps, dpw = _pool_bwd_weights(p_in, dh_out, md, pw_f, ps_full, i, nct=n_ct, tm=tm)
        dh_in, dmd_b = _pool_bwd_input(dp, h_in, dh_out, md, i, nct=n_ct, tm=tm, seg_lens=seg_lens, gw=pgw)
        return dh_in, dmd_a + dmd_b, dps, dpw

    zero_grp = jnp.zeros((1, 8, d), F32)
    dha3, dmd3, dws3 = mlp_back(3, ha3, dh4, u3, o3, lat(3), 0)
    dh3, dmd3p, dps3, dpw3 = pool_back(3, h3, p3, dha3, lat(3), 0)
    dmods[3] = jnp.concatenate([zero_grp, dmd3 + dmd3p], axis=0)
    tok = scatter_start(3, dws3 + [blocked_pool(dpw3)])
    dha2, dmd2, dws2 = mlp_back(2, ha2, dh3, u2, o2, lat(2) + tok, 0)
    dh2, dzpre, gated, dyb2, ab2, dmd2g, dln, dws, dbs = _gmlp_bwd(
        h2, dha2, zpre, y2, mods[2], win_b, lng_full, lnb_full, ws_b, ws_t, bs_col, wout_f, tm=tm)
    dwin = _mm_tn(ab2, dzpre, "gmlp_dw_in", col_blocks=N_CHIPS)
    dwout = blocked_rows(_mm_tn(gated, dyb2, "gmlp_dw_out"))
    dmods[2] = jnp.concatenate([zero_grp, dmd2 + dmd2g], axis=0)
    tok = scatter_start(2, dws2 + [dwin, dwout])
    dha1, dmd1, dws1 = mlp_back(1, ha1, dh2, u1, o1, lat(1) + tok, 0)
    do_att, dyb1, dmd1p = _proj_bwd(dha1, y1, mods[1], wo_f, tm=tm)
    dwo = blocked_rows(_mm_tn(o_att, dyb1, "attn_dw_o"))
    dq, dk, dv = _flash_bwd(q_r, k_r, v_b, o_att, do_att, lse, n_ctx=n_ctx, hd=hd)
    dqkv, dgains = _qkv_bwd(qkv, dq, dk, dv, cos, sin, gains, nh=nh, nkv=nkv, nct=nct, tm=tm)
    dwqkv = _mm_tn(xa1, dqkv, "attn_dw_qkv", col_blocks=N_CHIPS)
    dhc1, dmd1i = _attn_in_bwd(dqkv, wqkv_b, hc1, dha1, mods[1], nct=nct, tm=tm)
    dmods[1] = dmd1i + jnp.concatenate([zero_grp, dmd1 + dmd1p], axis=0)
    tok = scatter_start(1, dws1 + [dwqkv, dwo])
    dha0, du0, dob0, mb0, dmd0 = _mlp_bwd(ha0, dhc1, u0, o0, mods[0] + tok, w1_b[0], w2_b[0], 0, nct=nct, tm=tm)
    scatter_start(DEPTH + 1, [blocked_rows(_mm_tn(u0, dob0, "mlp_dw2_0", relu2=True))])
    dw1_0 = _mm_tn(mb0, du0, "mlp_dw1_0", col_blocks=N_CHIPS, after=scatters[DEPTH + 1][4])
    tok = scatter_start(0, [dw1_0])
    dhc0, dmd0p, dps0, dpw0 = pool_back(0, hc0, p0, dha0, mods[0] + tok, nct)
    dmods[0] = dmd0 + dmd0p
    grad_x = dhc0[None]
    scatter_start(DEPTH, [blocked_pool(dpw0)])

    dmods_all = jnp.stack(dmods, axis=0)
    small_grads = [dmods_all, dws, dbs, dgains, dln, dps0, dps3, fin_acc]
    sg_shapes = [a.shape for a in small_grads]
    sg_gath = _all_gather_small(_pack(small_grads), "gather_small_grads")
    sg_sum = _sum_devices(sg_gath, "sum_small_grads")
    s_dmods, s_dws, s_dbs, s_dgains, s_dln, s_dps0, s_dps3, s_fin = _unpack(sg_sum, sg_shapes)
    loss = s_fin[1, 0]

    sources, landed = [None] * len(scatters), [None] * len(scatters)
    for i in (3, 2, 1, DEPTH + 1, 0, DEPTH):
        send, recv, srcs, lands, _ = scatters[i]
        sources[i], landed[i] = _exchange_wait(send, recv, srcs, lands, SCATTER_PLAN, sg_sum, f"scatter_grads_{i}_wait")

    def summed(name, picks):
        return _sum_partials([sources[i][j] for i, j in picks], [landed[i][j] for i, j in picks], chip,
                             f"sum_chips_{name}")

    big = [("mlp_w1", mlp_w1, m_mlp_w1, v_mlp_w1, [(i, 0) for i in range(DEPTH)]),
           ("mlp_w2", mlp_w2, m_mlp_w2, v_mlp_w2, [(DEPTH + 1, 0)] + [(i, 1) for i in range(1, DEPTH)]),
           ("pool_w", pool_w, m_pool_w, v_pool_w, [(DEPTH, 0), (3, 2)]),
           ("attn_w_qkv", attn_w_qkv, m_attn_w_qkv, v_attn_w_qkv, [(1, 2)]),
           ("attn_w_o", attn_w_o, m_attn_w_o, v_attn_w_o, [(1, 3)]),
           ("gm_w_in", gm_w_in, m_gm_w_in, v_gm_w_in, [(2, 2)]),
           ("gm_w_out", gm_w_out, m_gm_w_out, v_gm_w_out, [(2, 3)])]
    partial = [summed(name, picks) for name, _, _, _, picks in big]
    swap = _exchange_start(partial, [lax.empty(p.shape, p.dtype) for p in partial], SIBLING_PLAN,
                           "swap_with_sibling_start")
    behind_swap = swap[4][0:1, 0:1]

    dm_dev = _unpack(sg_gath, sg_shapes[:1])[0]
    dm_lat = jnp.moveaxis(dm_dev[:, :, 1, :6, :], 0, 1).reshape(DEPTH, N_DEV, 6 * d)
    dm_ctx = jnp.moveaxis(dm_dev[:, :, 0, :6, :], 0, 1).reshape(DEPTH, N_DEV, 6 * d)
    dmod16 = _chip_cols(jnp.concatenate([dm_lat, dm_ctx], axis=1), chip, ncs) + behind_swap
    g_ada_w, dcc_part = _ada_bwd(c_all, c_all.T, dmod16, ada_w)
    dcc_gath = _all_gather_small(dcc_part, "gather_d_c_ctx")
    dcc_chips = _across_chips(dcc_gath, dcc_gath.shape[1:])
    dcc_rows = _sum_devices(dcc_chips, "sum_d_c_ctx")
    dcc = dcc_rows[0]
    ada_res = _adamw(g_ada_w.reshape(-1, ncs), None, ada_w.reshape(-1, ncs),
                     m_ada_w.reshape(-1, ncs), v_ada_w.reshape(-1, ncs), "adamw_ada_w")

    partial, from_sibling = _exchange_wait(swap[0], swap[1], swap[2], swap[3], SIBLING_PLAN, ada_res[1],
                                           "swap_with_sibling_wait")
    big_out = {}
    for (name, w, m, v, _), mine, theirs in zip(big, partial, from_sibling):
        cols = w.shape[-1]
        res = _adamw(mine, theirs, w.reshape(-1, cols), m.reshape(-1, cols), v.reshape(-1, cols), f"adamw_{name}")
        big_out[name] = [r.reshape(w.shape) for r in res]
    big_out["ada_w"] = [r.reshape(ada_w.shape) for r in ada_res]

    def cols_of(a, width):
        return _chip_cols(a, chip, width)

    zero = lambda a: jnp.zeros(a.shape, F32)
    ngw = norm_g.shape[-1]
    small = {
        "c_ctx": (dcc, zero(dcc), c_ctx, m_c_ctx, v_c_ctx),
        "ada_b": (s_dmods[:, 0, :6].reshape(DEPTH, 6 * d), s_dmods[:, 1, :6].reshape(DEPTH, 6 * d), ada_b, m_ada_b,
                  v_ada_b),
        "norm_g": (cols_of(s_dmods[:, 0, 6:8], ngw), cols_of(s_dmods[:, 1, 6:8], ngw), norm_g, m_norm_g, v_norm_g),
        "pool_scale": (cols_of(jnp.stack([s_dps0[0], s_dps3[0]]), pool_scale.shape[-1]), zero(pool_scale),
                       pool_scale, m_pool_scale, v_pool_scale),
        "attn_q_g": (s_dgains[0:1], zero(attn_q_g), attn_q_g, m_attn_q_g, v_attn_q_g),
        "attn_k_g": (s_dgains[1:2], zero(attn_k_g), attn_k_g, m_attn_k_g, v_attn_k_g),
        "gm_ln_g": (cols_of(s_dln[0:1], gm_ln_g.shape[-1]), zero(gm_ln_g), gm_ln_g, m_gm_ln_g, v_gm_ln_g),
        "gm_ln_b": (cols_of(s_dln[1:2], gm_ln_b.shape[-1]), zero(gm_ln_b), gm_ln_b, m_gm_ln_b, v_gm_ln_b),
        "gm_ws": (s_dws[None], zero(gm_ws), gm_ws, m_gm_ws, v_gm_ws),
        "gm_bs": (s_dbs[None, :, :, 0], zero(gm_bs), gm_bs, m_gm_bs, v_gm_bs),
        "final_g": (s_fin[0], zero(final_g), final_g, m_final_g, v_final_g),
    }
    keys = list(small)
    packed = [_pack([small[k][t] for k in keys]) for t in range(5)]
    res = _adamw(*packed, "adamw_small")
    shapes = [small[k][2].shape for k in keys]
    small_out = {k: [] for k in keys}
    for r in res:
        for k, a in zip(keys, _unpack(r, shapes)):
            small_out[k].append(a)

    order = ["c_ctx", "ada_w", "ada_b", "norm_g", "mlp_w1", "mlp_w2", "pool_w", "pool_scale", "attn_w_qkv",
             "attn_w_o", "attn_q_g", "attn_k_g", "gm_w_in", "gm_ln_g", "gm_ln_b", "gm_ws", "gm_bs", "gm_w_out",
             "final_g"]
    allo = {**big_out, **small_out}
    outs = [loss, grad_x]
    for t in range(4):
        outs += [allo[k][t] for k in order]
    return tuple(outs)
```

```python
import functools
import math

import numpy as np
import jax
import jax.numpy as jnp
from jax import lax
from jax.experimental import pallas as pl
from jax.experimental.pallas import tpu as pltpu

F32 = jnp.float32
BF16 = jnp.bfloat16
MESH = pl.DeviceIdType.MESH

EPS = 1e-6
GRID_W = 64
ROPE_BASE = 10000.0
POOL_WINDOWS = (2, 4, 8, 16)
HALO = 8
DEPTH = 4
N_MIXERS = 3

ADAM_LR = 0.001
ADAM_B1 = 0.9
ADAM_B2 = 0.999
ADAM_EPS = 1e-08
ADAM_WD = 0.01
ADAM_STEP = 10

VMEM_LIMIT_BYTES = 56 * 1024 * 1024
LANES = 128
SUBLANES = 8
N_DEV = 8
N_CHIPS = 4

SH1, SC1, G1, SH2, SC2, G2, NG0, NG1 = range(8)


def _dot(a, b):
    return jnp.dot(a, b, preferred_element_type=F32)


def _dot_nt(a, b):
    return lax.dot_general(a, b, (((1,), (1,)), ((), ())), preferred_element_type=F32)


def _dot_tn(a, b):
    return lax.dot_general(a, b, (((0,), (0,)), ((), ())), preferred_element_type=F32)


def _dot_blocks(a, w_ref):
    return jnp.concatenate([_dot(a, w_ref[k]) for k in range(w_ref.shape[0])], axis=1)


def _dot_nt_blocks(a, w_ref):
    nb, _, w = w_ref.shape
    acc = _dot_nt(a[:, 0:w], w_ref[0])
    for k in range(1, nb):
        acc = acc + _dot_nt(a[:, k * w:(k + 1) * w], w_ref[k])
    return acc


def _params(**kw):
    return pltpu.CompilerParams(vmem_limit_bytes=VMEM_LIMIT_BYTES, **kw)


def _full(shape):
    nd = len(shape)
    return pl.BlockSpec(shape, lambda *_: (0,) * nd)


def _rows(tm, width):
    return pl.BlockSpec((tm, width), lambda i: (i, 0))


def _group_of(nct, groups):
    if groups == 1:
        return lambda i: 0
    return lambda i: jnp.where(i >= nct, 1, 0)


def _mods_spec(nct, groups, d):
    grp = _group_of(nct, groups)
    return pl.BlockSpec((None, 8, d), lambda i: (grp(i), 0, 0))


def _first_of_group(i, nct, groups):
    if groups == 1:
        return i == 0
    return jnp.logical_or(i == 0, i == nct)


def _rowsum(v):
    return jnp.sum(v, axis=0, keepdims=True)


def _rms_parts(x):
    r = lax.rsqrt(jnp.mean(x * x, axis=-1, keepdims=True) + EPS)
    return x * r, r


def _normmod(x, md, which):
    ng, sh, sc = (md[NG0:NG0 + 1], md[SH1:SH1 + 1], md[SC1:SC1 + 1]) if which == 0 else (
        md[NG1:NG1 + 1], md[SH2:SH2 + 1], md[SC2:SC2 + 1])
    xhat, r = _rms_parts(x)
    n = xhat * ng
    return n * (1.0 + sc) + sh, (xhat, r, n)


def _normmod_bwd(da, parts, md, which):
    xhat, r, n = parts
    ng, sc = (md[NG0:NG0 + 1], md[SC1:SC1 + 1]) if which == 0 else (md[NG1:NG1 + 1], md[SC2:SC2 + 1])
    dsh = _rowsum(da)
    dsc = _rowsum(da * n)
    dn = da * (1.0 + sc)
    dng = _rowsum(dn * xhat)
    dxhat = dn * ng
    dx = r * (dxhat - xhat * jnp.mean(dxhat * xhat, axis=-1, keepdims=True))
    return dx, dsh, dsc, dng


def _acc_rows(ref, first, rows):
    @pl.when(first)
    def _():
        ref[...] = jnp.zeros(ref.shape, ref.dtype)

    for r, v in rows.items():
        ref[r:r + 1, :] += v


def _shift_up(x, k):
    if k == 0:
        return x
    return pltpu.roll(x, x.shape[0] - k, axis=0)


def _gelu(x):
    k = math.sqrt(2.0 / math.pi)
    return 0.5 * x * (1.0 + jnp.tanh(k * (x + 0.044715 * x * x * x)))


def _gelu_grad(x):
    k = math.sqrt(2.0 / math.pi)
    t = jnp.tanh(k * (x + 0.044715 * x * x * x))
    return 0.5 * (1.0 + t) + 0.5 * x * (1.0 - t * t) * k * (1.0 + 3.0 * 0.044715 * x * x)


def _silu(x):
    return x / (1.0 + jnp.exp(-x))


def _silu_grad(x):
    s = 1.0 / (1.0 + jnp.exp(-x))
    return s * (1.0 + x * (1.0 - s))


def _mlp_fwd(h, mods, w1, w2, layer, *, nct, tm):
    rows, d = h.shape
    groups = mods.shape[0]
    nb, _, fc = w1.shape
    ff = nb * fc

    def body(h_ref, md_ref, w1_ref, w2_ref, h2_ref, u_ref, o_ref):
        x = h_ref[...]
        md = md_ref[...]
        m, _ = _normmod(x, md, 1)
        mb = m.astype(BF16)
        acc = jnp.zeros((tm, d), F32)
        for k in range(nb):
            u = _dot(mb, w1_ref[k])
            u_ref[:, k * fc:(k + 1) * fc] = u.astype(BF16)
            acc = acc + _dot(jnp.square(jnp.maximum(u, 0.0)).astype(BF16), w2_ref[k])
        o_ref[...] = acc.astype(BF16)
        h2_ref[...] = x + md[G2:G2 + 1] * acc

    return pl.pallas_call(
        body, name=f"mlp_fwd_{layer}", grid=(rows // tm,),
        in_specs=[_rows(tm, d), _mods_spec(nct, groups, d), _full(w1.shape), _full(w2.shape)],
        out_specs=[_rows(tm, d), _rows(tm, ff), _rows(tm, d)],
        out_shape=[jax.ShapeDtypeStruct((rows, d), F32), jax.ShapeDtypeStruct((rows, ff), BF16),
                   jax.ShapeDtypeStruct((rows, d), BF16)],
        compiler_params=_params(),
    )(h, mods, w1, w2)


def _mlp_up(h, mods, w1, layer, *, nct, tm):
    rows, d = h.shape
    groups = mods.shape[0]
    nb, _, fc = w1.shape

    def body(h_ref, md_ref, w1_ref, u_ref):
        m, _ = _normmod(h_ref[...], md_ref[...], 1)
        mb = m.astype(BF16)
        for k in range(nb):
            u_ref[:, k * fc:(k + 1) * fc] = _dot(mb, w1_ref[k]).astype(BF16)

    return pl.pallas_call(
        body, name=f"mlp_up_{layer}", grid=(rows // tm,),
        in_specs=[_rows(tm, d), _mods_spec(nct, groups, d), _full(w1.shape)],
        out_specs=_rows(tm, nb * fc), out_shape=jax.ShapeDtypeStruct((rows, nb * fc), BF16),
        compiler_params=_params(),
    )(h, mods, w1)


def _mlp_down(h, u, mods, w2, layer, *, nct, tm):
    rows, d = h.shape
    groups = mods.shape[0]
    nb, fc, _ = w2.shape

    def body(h_ref, u_ref, md_ref, w2_ref, h2_ref, o_ref):
        acc = jnp.zeros((tm, d), F32)
        for k in range(nb):
            uk = u_ref[:, k * fc:(k + 1) * fc].astype(F32)
            acc = acc + _dot(jnp.square(jnp.maximum(uk, 0.0)).astype(BF16), w2_ref[k])
        o_ref[...] = acc.astype(BF16)
        h2_ref[...] = h_ref[...] + md_ref[G2:G2 + 1, :] * acc

    return pl.pallas_call(
        body, name=f"mlp_down_{layer}", grid=(rows // tm,),
        in_specs=[_rows(tm, d), _rows(tm, nb * fc), _mods_spec(nct, groups, d), _full(w2.shape)],
        out_specs=[_rows(tm, d), _rows(tm, d)],
        out_shape=[jax.ShapeDtypeStruct((rows, d), F32), jax.ShapeDtypeStruct((rows, d), BF16)],
        compiler_params=_params(),
    )(h, u, mods, w2)


def _mlp_bwd(h1, dh2, u, o, mods, w1, w2, layer, *, nct, tm):
    rows, d = h1.shape
    groups = mods.shape[0]
    nb, _, fc = w1.shape
    ff = nb * fc

    def body(h_ref, g_ref, u_ref, o_ref, md_ref, w1_ref, w2_ref, dh_ref, du_ref, dob_ref, mb_ref, dmd_ref):
        i = pl.program_id(0)
        x = h_ref[...]
        g = g_ref[...]
        md = md_ref[...]
        m, parts = _normmod(x, md, 1)
        mb_ref[...] = m.astype(BF16)
        dg2 = _rowsum(g * o_ref[...].astype(F32))
        dob = (g * md[G2:G2 + 1]).astype(BF16)
        dob_ref[...] = dob
        dm = jnp.zeros((tm, d), F32)
        for k in range(nb):
            uk = u_ref[:, k * fc:(k + 1) * fc].astype(F32)
            dr = _dot_nt(dob, w2_ref[k])
            duk = (dr * (2.0 * jnp.maximum(uk, 0.0))).astype(BF16)
            du_ref[:, k * fc:(k + 1) * fc] = duk
            dm = dm + _dot_nt(duk, w1_ref[k])
        dx, dsh, dsc, dng = _normmod_bwd(dm, parts, md, 1)
        dh_ref[...] = g + dx
        _acc_rows(dmd_ref, _first_of_group(i, nct, groups), {SH2: dsh, SC2: dsc, G2: dg2, NG1: dng})

    return pl.pallas_call(
        body, name=f"mlp_bwd_{layer}", grid=(rows // tm,),
        in_specs=[_rows(tm, d), _rows(tm, d), _rows(tm, ff), _rows(tm, d), _mods_spec(nct, groups, d),
                  _full(w1.shape), _full(w2.shape)],
        out_specs=[_rows(tm, d), _rows(tm, ff), _rows(tm, d), _rows(tm, d), _mods_spec(nct, groups, d)],
        out_shape=[jax.ShapeDtypeStruct((rows, d), F32), jax.ShapeDtypeStruct((rows, ff), BF16),
                   jax.ShapeDtypeStruct((rows, d), BF16), jax.ShapeDtypeStruct((rows, d), BF16),
                   jax.ShapeDtypeStruct((groups, 8, d), F32)],
        compiler_params=_params(),
    )(h1, dh2, u, o, mods, w1, w2)


def _div_tile(n, cap):
    if n <= cap:
        return n
    return max(t for t in range(LANES, cap + 1, LANES) if n % t == 0)


DW_TOKEN_TILE_CAP = 4224


def _mm_tn(a, b, name, *, relu2=False, col_blocks=1, after=None):
    rows, m = a.shape
    n = b.shape[1]
    tmm = min(m, 1024)
    tn = min(n // col_blocks, 2048)
    per_block = n // col_blocks // tn
    tr = _div_tile(rows, DW_TOKEN_TILE_CAP)
    tokens = [] if after is None else [after]

    def body(a_ref, b_ref, *rest):
        o_ref, acc_ref = rest[len(tokens):]
        r = pl.program_id(2)

        @pl.when(r == 0)
        def _():
            acc_ref[...] = jnp.zeros(acc_ref.shape, F32)

        av = a_ref[...]
        if relu2:
            av = jnp.square(jnp.maximum(av.astype(F32), 0.0)).astype(BF16)
        acc_ref[...] += _dot_tn(av, b_ref[...])

        @pl.when(r == pl.num_programs(2) - 1)
        def _():
            o_ref[...] = acc_ref[...].astype(BF16)

    return pl.pallas_call(
        body, name=name, grid=(m // tmm, n // tn, rows // tr),
        in_specs=[pl.BlockSpec((tr, tmm), lambda i, j, r: (r, i)), pl.BlockSpec((tr, tn), lambda i, j, r: (r, j))]
        + [pl.BlockSpec((8, LANES), lambda i, j, r: (0, 0))] * len(tokens),
        out_specs=pl.BlockSpec((None, tmm, tn), lambda i, j, r: (j // per_block, i, j % per_block)),
        out_shape=jax.ShapeDtypeStruct((col_blocks, m, n // col_blocks), BF16),
        scratch_shapes=[pltpu.VMEM((tmm, tn), F32)],
        compiler_params=_params(),
    )(a, b, *tokens)


def _halo_specs(tm, d, rows):
    per = tm // HALO
    prev = pl.BlockSpec((HALO, d), lambda i: (jnp.maximum(i * per - 1, 0), 0))
    nxt = pl.BlockSpec((HALO, d), lambda i: (jnp.minimum((i + 1) * per, rows // HALO - 1), 0))
    return prev, _rows(tm, d), nxt


def _segment_positions(i, tm, nct, groups, seg_lens):
    if groups == 1:
        start, length = 0, seg_lens[-1]
    else:
        start = jnp.where(i >= nct, nct, 0)
        length = jnp.where(i >= nct, seg_lens[1], seg_lens[0])
    rid = lax.broadcasted_iota(jnp.int32, (tm + 2 * HALO, 1), 0)
    pos = (i - start) * tm - HALO + rid
    return pos, length


def _window_count(pos, length, w):
    hi = jnp.minimum(pos + (w - w // 2), length)
    lo = jnp.maximum(pos - w // 2, 0)
    return (hi - lo).astype(F32)


def _window_sum(xg, w, lead):
    b, k = xg, 1
    while k < w:
        b = b + _shift_up(b, k)
        k *= 2
    return _shift_up(b, HALO - lead)[0:xg.shape[0] - 2 * HALO]


def _pooled(ext, md, pos, length, gw):
    tm = ext.shape[0] - 2 * HALO
    a_ext, parts = _normmod(ext, md, 0)
    valid = jnp.logical_and(pos >= 0, pos < length)
    a_ext = jnp.where(valid, a_ext, 0.0)
    pos_c = pos[HALO:HALO + tm]
    ps = []
    for g, w in enumerate(POOL_WINDOWS):
        xg = a_ext[:, g * gw:(g + 1) * gw]
        s = _window_sum(xg, w, w // 2)
        ps.append(s * (1.0 / _window_count(pos_c, length, w)) - xg[HALO:HALO + tm])
    return ps, parts


def _pool_fwd(h, mods, pw, pscale, layer, *, nct, tm, seg_lens):
    rows, d = h.shape
    groups = mods.shape[0]
    pg, gw = pw.shape[1], pw.shape[-1]

    def body(prev_ref, cur_ref, next_ref, md_ref, pw_ref, ps_ref, out_ref, p_ref):
        i = pl.program_id(0)
        md = md_ref[...]
        cur = cur_ref[...]
        ext = jnp.concatenate([prev_ref[...], cur, next_ref[...]], axis=0)
        pos, length = _segment_positions(i, tm, nct, groups, seg_lens)
        ps, _ = _pooled(ext, md, pos, length, gw)
        for g in range(pg):
            pb = ps[g].astype(BF16)
            p_ref[:, g * gw:(g + 1) * gw] = pb
            yg = _dot(pb, pw_ref[g]) * ps_ref[:, g * gw:(g + 1) * gw]
            out_ref[:, g * gw:(g + 1) * gw] = cur[:, g * gw:(g + 1) * gw] + md[G1:G1 + 1, g * gw:(g + 1) * gw] * yg

    j = layer // N_MIXERS
    return pl.pallas_call(
        body, name=f"pool_fwd_{layer}", grid=(rows // tm,),
        in_specs=[*_halo_specs(tm, d, rows), _mods_spec(nct, groups, d),
                  pl.BlockSpec((None, pg, gw, gw), lambda i: (j, 0, 0, 0)), _full((1, d))],
        out_specs=[_rows(tm, d), _rows(tm, d)],
        out_shape=[jax.ShapeDtypeStruct((rows, d), F32), jax.ShapeDtypeStruct((rows, d), BF16)],
        compiler_params=_params(),
    )(h, h, h, mods, pw, pscale[j:j + 1])


def _pool_bwd_weights(p, dh1, mods, pw, pscale, layer, *, nct, tm):
    rows, d = p.shape
    groups = mods.shape[0]
    pg, gw = pw.shape[1], pw.shape[-1]

    def body(p_ref, g_ref, md_ref, pw_ref, ps_ref, dp_ref, dmd_ref, dps_ref, dpw_ref):
        i = pl.program_id(0)
        md = md_ref[...]
        gup = g_ref[...]

        @pl.when(i == 0)
        def _():
            dps_ref[...] = jnp.zeros(dps_ref.shape, F32)
            dpw_ref[...] = jnp.zeros(dpw_ref.shape, F32)

        dg1 = []
        for g in range(pg):
            cols = slice(g * gw, (g + 1) * gw)
            pb = p_ref[:, cols]
            yp = _dot(pb, pw_ref[g])
            sc = ps_ref[:, cols]
            dg1.append(_rowsum(gup[:, cols] * (yp * sc)))
            dy = gup[:, cols] * md[G1:G1 + 1, cols]
            dps_ref[0:1, cols] += _rowsum(dy * yp)
            dyp = (dy * sc).astype(BF16)
            dp_ref[:, cols] = _dot_nt(dyp, pw_ref[g])
            dpw_ref[g] += _dot_tn(pb, dyp)
        _acc_rows(dmd_ref, _first_of_group(i, nct, groups), {G1: jnp.concatenate(dg1, axis=1)})

    j = layer // N_MIXERS
    return pl.pallas_call(
        body, name=f"pool_bwd_w_{layer}", grid=(rows // tm,),
        in_specs=[_rows(tm, d), _rows(tm, d), _mods_spec(nct, groups, d),
                  pl.BlockSpec((None, pg, gw, gw), lambda i: (j, 0, 0, 0)), _full((1, d))],
        out_specs=[_rows(tm, d), _mods_spec(nct, groups, d), _full((8, d)), _full((pg, gw, gw))],
        out_shape=[jax.ShapeDtypeStruct((rows, d), F32), jax.ShapeDtypeStruct((groups, 8, d), F32),
                   jax.ShapeDtypeStruct((8, d), F32), jax.ShapeDtypeStruct((pg, gw, gw), F32)],
        compiler_params=_params(),
    )(p, dh1, mods, pw, pscale[j:j + 1])


def _pool_bwd_input(dp, h, dh1, mods, layer, *, nct, tm, seg_lens, gw):
    rows, d = h.shape
    groups = mods.shape[0]

    def body(prev_ref, cur_ref, next_ref, h_ref, g_ref, md_ref, dh_ref, dmd_ref):
        i = pl.program_id(0)
        md = md_ref[...]
        dp_cur = cur_ref[...]
        ext = jnp.concatenate([prev_ref[...], dp_cur, next_ref[...]], axis=0)
        pos, length = _segment_positions(i, tm, nct, groups, seg_lens)
        valid = jnp.logical_and(pos >= 0, pos < length)
        das = []
        for g, w in enumerate(POOL_WINDOWS):
            cols = slice(g * gw, (g + 1) * gw)
            q = jnp.where(valid, ext[:, cols] * (1.0 / jnp.maximum(_window_count(pos, length, w), 1.0)), 0.0)
            das.append(_window_sum(q, w, w // 2 - 1) - dp_cur[:, cols])
        da = jnp.concatenate(das, axis=1)
        _, parts = _normmod(h_ref[...], md, 0)
        dx, dsh, dsc, dng = _normmod_bwd(da, parts, md, 0)
        dh_ref[...] = g_ref[...] + dx
        _acc_rows(dmd_ref, _first_of_group(i, nct, groups), {SH1: dsh, SC1: dsc, NG0: dng})

    return pl.pallas_call(
        body, name=f"pool_bwd_x_{layer}", grid=(rows // tm,),
        in_specs=[*_halo_specs(tm, d, rows), _rows(tm, d), _rows(tm, d), _mods_spec(nct, groups, d)],
        out_specs=[pl.BlockSpec((tm, d), lambda i: (jnp.maximum(i - nct, 0), 0)), _mods_spec(nct, groups, d)],
        out_shape=[jax.ShapeDtypeStruct((rows - nct * tm, d), F32), jax.ShapeDtypeStruct((groups, 8, d), F32)],
        compiler_params=_params(),
    )(dp, dp, dp, h, dh1, mods)


def _rope_tables(n_ctx, seq, hd):
    half = hd // 2
    n_rows = seq // GRID_W
    inv = np.float32(ROPE_BASE) ** (-np.arange(0, half, 2, dtype=np.float32) / np.float32(half))
    ar = np.arange(n_rows, dtype=np.float32)[:, None] * inv[None, :]
    ac = np.arange(GRID_W, dtype=np.float32)[:, None] * inv[None, :]

    def over_tokens(row_part, col_part):
        r = jnp.repeat(jnp.asarray(row_part, F32), GRID_W, axis=0)
        c = jnp.tile(jnp.asarray(col_part, F32), (n_rows, 1))
        return r, c

    cr, cc = over_tokens(np.cos(ar), np.cos(ac))
    sr, sc = over_tokens(np.sin(ar), np.sin(ac))
    cos = jnp.concatenate([cr, cr, cc, cc], axis=1)
    sin = jnp.concatenate([-sr, sr, -sc, sc], axis=1)
    cos = jnp.concatenate([jnp.ones((n_ctx, hd), F32), cos], axis=0)
    sin = jnp.concatenate([jnp.zeros((n_ctx, hd), F32), sin], axis=0)
    return cos, sin


def _rope_partner(x):
    hd = x.shape[-1]
    q = hd // 4
    lane = lax.broadcasted_iota(jnp.int32, x.shape, 1)
    first = (lane % (2 * q)) < q
    return jnp.where(first, pltpu.roll(x, hd - q, axis=1), pltpu.roll(x, q, axis=1))


def _qkv_fwd(h, mods, wqkv, cos, sin, gains, *, nh, nkv, nct, tm):
    rows, d = h.shape
    qw = wqkv.shape[0] * wqkv.shape[-1]
    hd = cos.shape[-1]

    def body(h_ref, md_ref, w_ref, cos_ref, sin_ref, gn_ref, xa_ref, qkv_ref, q_ref, k_ref, v_ref):
        a, _ = _normmod(h_ref[...], md_ref[...], 0)
        xa = a.astype(BF16)
        xa_ref[...] = xa
        qkv = _dot_blocks(xa, w_ref)
        qkv_ref[...] = qkv
        c, s = cos_ref[...], sin_ref[...]
        for hh in range(nh + nkv):
            xh = qkv[:, hh * hd:(hh + 1) * hd]
            xhat, _ = _rms_parts(xh)
            y = xhat * (gn_ref[0:1, :] if hh < nh else gn_ref[1:2, :])
            rot = (y * c + _rope_partner(y) * s).astype(BF16)
            if hh < nh:
                q_ref[:, hh * hd:(hh + 1) * hd] = rot
            else:
                k_ref[:, (hh - nh) * hd:(hh - nh + 1) * hd] = rot
        v_ref[...] = qkv[:, (nh + nkv) * hd:].astype(BF16)

    return pl.pallas_call(
        body, name="attn_qkv_fwd", grid=(rows // tm,),
        in_specs=[_rows(tm, d), _mods_spec(nct, 2, d), _full(wqkv.shape), _rows(tm, hd), _rows(tm, hd),
                  _full((8, hd))],
        out_specs=[_rows(tm, d), _rows(tm, qw), pl.BlockSpec((tm, nh * hd), lambda i: (jnp.maximum(i - nct, 0), 0)),
                   _rows(tm, nkv * hd), _rows(tm, nkv * hd)],
        out_shape=[jax.ShapeDtypeStruct((rows, d), BF16), jax.ShapeDtypeStruct((rows, qw), F32),
                   jax.ShapeDtypeStruct((rows - nct * tm, nh * hd), BF16),
                   jax.ShapeDtypeStruct((rows, nkv * hd), BF16), jax.ShapeDtypeStruct((rows, nkv * hd), BF16)],
        compiler_params=_params(),
    )(h, mods, wqkv, cos, sin, gains)


ATTN_Q_TILE_CAP = 1024
ATTN_KV_TILE_CAP = 4224
ATTN_ROW_GROUP = 256
LOG2E = 1.4426950408889634


def _attn_tiles(seq, total):
    tq = _div_tile(seq, ATTN_Q_TILE_CAP)
    return tq, _div_tile(total, ATTN_KV_TILE_CAP), min(ATTN_ROW_GROUP, tq)


def _flash_fwd(q, k, v, *, n_ctx, hd):
    total = k.shape[0]
    seq = total - n_ctx
    nkv = k.shape[1] // hd
    tq, tk, rg = _attn_tiles(seq, total)
    nk = total // tk
    scale = hd ** -0.5
    c2 = scale * LOG2E

    def body(q_ref, k_ref, v_ref, o_ref, lse_ref, m_sc, l_sc, acc_sc):
        ki = pl.program_id(2)

        @pl.when(ki == 0)
        def _():
            m_sc[...] = jnp.full(m_sc.shape, -jnp.inf, F32)
            l_sc[...] = jnp.zeros(l_sc.shape, F32)
            acc_sc[...] = jnp.zeros(acc_sc.shape, F32)

        kk, vv = k_ref[...], v_ref[...]
        groups = [(g, sub) for g in range(2) for sub in range(tq // rg)]

        def scores(g, sub):
            return _dot_nt(q_ref[sub * rg:(sub + 1) * rg, g * hd:(g + 1) * hd], kk)

        s_next = scores(*groups[0])
        for idx, (g, sub) in enumerate(groups):
            s = s_next
            if idx + 1 < len(groups):
                s_next = scores(*groups[idx + 1])
            rows = slice(g * tq + sub * rg, g * tq + (sub + 1) * rg)
            m_old = m_sc[rows]
            m_new = jnp.maximum(m_old, jnp.max(s, axis=-1, keepdims=True))
            alpha = jnp.exp2((m_old - m_new) * c2)
            p = jnp.exp2((s - m_new) * c2)
            l_sc[rows] = alpha * l_sc[rows] + jnp.sum(p, axis=-1, keepdims=True)
            acc_sc[rows] = alpha * acc_sc[rows] + _dot(p.astype(BF16), vv)
            m_sc[rows] = m_new

        @pl.when(ki == nk - 1)
        def _():
            o2 = acc_sc[...] / l_sc[...]
            lse = m_sc[...] * scale + jnp.log(l_sc[...])
            o_ref[:, :hd] = o2[:tq].astype(BF16)
            o_ref[:, hd:] = o2[tq:].astype(BF16)
            lse_ref[:, 0:1] = lse[:tq]
            lse_ref[:, 1:2] = lse[tq:]

    return pl.pallas_call(
        body, name="attn_flash_fwd", grid=(nkv, seq // tq, nk),
        in_specs=[pl.BlockSpec((tq, 2 * hd), lambda h, i, j: (i, h)),
                  pl.BlockSpec((tk, hd), lambda h, i, j: (j, h)),
                  pl.BlockSpec((tk, hd), lambda h, i, j: (j, h))],
        out_specs=[pl.BlockSpec((tq, 2 * hd), lambda h, i, j: (i, h)),
                   pl.BlockSpec((None, tq, 2), lambda h, i, j: (h, i, 0))],
        out_shape=[jax.ShapeDtypeStruct((seq, 2 * nkv * hd), BF16), jax.ShapeDtypeStruct((nkv, seq, 2), F32)],
        scratch_shapes=[pltpu.VMEM((2 * tq, 1), F32), pltpu.VMEM((2 * tq, 1), F32), pltpu.VMEM((2 * tq, hd), F32)],
        compiler_params=_params(),
    )(q, k, v)


def _flash_bwd(q, k, v, o, do, lse, *, n_ctx, hd):
    total = k.shape[0]
    seq = total - n_ctx
    nkv = k.shape[1] // hd
    tq, tk, rg = _attn_tiles(seq, total)
    scale = hd ** -0.5
    c2 = scale * LOG2E

    def body(q_ref, k_ref, v_ref, o_ref, do_ref, lse_ref, dq_ref, dk_ref, dv_ref):
        ki, qi = pl.program_id(1), pl.program_id(2)
        kk, vv = k_ref[...], v_ref[...]

        @pl.when(qi == 0)
        def _():
            dk_ref[...] = jnp.zeros(dk_ref.shape, F32)
            dv_ref[...] = jnp.zeros(dv_ref.shape, F32)

        dk_acc = jnp.zeros((tk, hd), F32)
        dv_acc = jnp.zeros((tk, hd), F32)
        for g in range(2):
            for sub in range(tq // rg):
                rs = slice(sub * rg, (sub + 1) * rg)
                cs = slice(g * hd, (g + 1) * hd)
                qq = q_ref[rs, cs]
                dd = do_ref[rs, cs]
                delta = jnp.sum(dd.astype(F32) * o_ref[rs, cs].astype(F32), axis=-1, keepdims=True)
                p = jnp.exp2(_dot_nt(qq, kk) * c2 - lse_ref[rs, g:g + 1] * LOG2E)
                dp = _dot_nt(dd, vv)
                ds = (p * (dp - delta) * scale).astype(BF16)
                dv_acc = dv_acc + _dot_tn(p.astype(BF16), dd)
                dk_acc = dk_acc + _dot_tn(ds, qq)
                dq = _dot(ds, kk)
                rows = pl.ds(pl.multiple_of(qi * tq, tq) + sub * rg, rg)

                @pl.when(ki == 0)
                def _():
                    dq_ref[rows, cs] = dq

                @pl.when(ki > 0)
                def _():
                    dq_ref[rows, cs] += dq
        dk_ref[...] += dk_acc
        dv_ref[...] += dv_acc

    return pl.pallas_call(
        body, name="attn_flash_bwd", grid=(nkv, total // tk, seq // tq),
        in_specs=[pl.BlockSpec((tq, 2 * hd), lambda h, j, i: (i, h)),
                  pl.BlockSpec((tk, hd), lambda h, j, i: (j, h)),
                  pl.BlockSpec((tk, hd), lambda h, j, i: (j, h)),
                  pl.BlockSpec((tq, 2 * hd), lambda h, j, i: (i, h)),
                  pl.BlockSpec((tq, 2 * hd), lambda h, j, i: (i, h)),
                  pl.BlockSpec((None, tq, 2), lambda h, j, i: (h, i, 0))],
        out_specs=[pl.BlockSpec((seq, 2 * hd), lambda h, j, i: (0, h)),
                   pl.BlockSpec((tk, hd), lambda h, j, i: (j, h)),
                   pl.BlockSpec((tk, hd), lambda h, j, i: (j, h))],
        out_shape=[jax.ShapeDtypeStruct((seq, 2 * nkv * hd), F32), jax.ShapeDtypeStruct((total, nkv * hd), F32),
                   jax.ShapeDtypeStruct((total, nkv * hd), F32)],
        compiler_params=_params(),
    )(q, k, v, o, do, lse)


def _proj_fwd(o, wo, hc, mods, *, n_ctx, tm):
    seq, d = o.shape
    off = n_ctx // tm

    def body(o_ref, w_ref, h_ref, md_ref, h1_ref, y_ref):
        y = _dot(o_ref[...], w_ref[...])
        y_ref[...] = y.astype(BF16)
        h1_ref[...] = h_ref[...] + md_ref[G1:G1 + 1, :] * y

    return pl.pallas_call(
        body, name="attn_proj_fwd", grid=(seq // tm,),
        in_specs=[_rows(tm, d), _full((d, d)),
                  pl.BlockSpec((tm, d), lambda i: (i + off, 0)), pl.BlockSpec((None, 8, d), lambda i: (1, 0, 0))],
        out_specs=[_rows(tm, d), _rows(tm, d)],
        out_shape=[jax.ShapeDtypeStruct((seq, d), F32), jax.ShapeDtypeStruct((seq, d), BF16)],
        compiler_params=_params(),
    )(o, wo, hc, mods)


def _proj_bwd(dh1, y, mods, wo, *, tm):
    seq, d = dh1.shape

    def body(g_ref, y_ref, md_ref, w_ref, do_ref, dyb_ref, dmd_ref):
        i = pl.program_id(0)
        g = g_ref[...]
        dyb = (g * md_ref[G1:G1 + 1, :]).astype(BF16)
        dyb_ref[...] = dyb
        do_ref[...] = _dot_nt(dyb, w_ref[...]).astype(BF16)
        _acc_rows(dmd_ref, i == 0, {G1: _rowsum(g * y_ref[...].astype(F32))})

    return pl.pallas_call(
        body, name="attn_proj_bwd", grid=(seq // tm,),
        in_specs=[_rows(tm, d), _rows(tm, d), pl.BlockSpec((None, 8, d), lambda i: (1, 0, 0)), _full((d, d))],
        out_specs=[_rows(tm, d), _rows(tm, d), pl.BlockSpec((None, 8, d), lambda i: (0, 0, 0))],
        out_shape=[jax.ShapeDtypeStruct((seq, d), BF16), jax.ShapeDtypeStruct((seq, d), BF16),
                   jax.ShapeDtypeStruct((1, 8, d), F32)],
        compiler_params=_params(),
    )(dh1, y, mods, wo)


def _qkv_bwd(qkv, dq, dk, dv, cos, sin, gains, *, nh, nkv, nct, tm):
    rows, qw = qkv.shape
    hd = cos.shape[-1]

    def body(qkv_ref, dq_ref, dk_ref, dv_ref, cos_ref, sin_ref, gn_ref, out_ref, dgn_ref):
        i = pl.program_id(0)
        c, s = cos_ref[...], sin_ref[...]
        is_lat = (i >= nct).astype(F32)
        dqg = jnp.zeros((1, hd), F32)
        dkg = jnp.zeros((1, hd), F32)
        for hh in range(nh + nkv):
            if hh < nh:
                dr = dq_ref[:, hh * hd:(hh + 1) * hd] * is_lat
                gn = gn_ref[0:1, :]
            else:
                dr = dk_ref[:, (hh - nh) * hd:(hh - nh + 1) * hd]
                gn = gn_ref[1:2, :]
            dy = dr * c + _rope_partner(dr * s)
            xhat, r = _rms_parts(qkv_ref[:, hh * hd:(hh + 1) * hd])
            dgh = _rowsum(dy * xhat)
            if hh < nh:
                dqg = dqg + dgh
            else:
                dkg = dkg + dgh
            dxhat = dy * gn
            dx = r * (dxhat - xhat * jnp.mean(dxhat * xhat, axis=-1, keepdims=True))
            out_ref[:, hh * hd:(hh + 1) * hd] = dx.astype(BF16)
        out_ref[:, (nh + nkv) * hd:] = dv_ref[...].astype(BF16)
        _acc_rows(dgn_ref, i == 0, {0: dqg, 1: dkg})

    return pl.pallas_call(
        body, name="attn_qkv_bwd", grid=(rows // tm,),
        in_specs=[_rows(tm, qw), pl.BlockSpec((tm, nh * hd), lambda i: (jnp.maximum(i - nct, 0), 0)),
                  _rows(tm, nkv * hd), _rows(tm, nkv * hd), _rows(tm, hd), _rows(tm, hd), _full((8, hd))],
        out_specs=[_rows(tm, qw), _full((8, hd))],
        out_shape=[jax.ShapeDtypeStruct((rows, qw), BF16), jax.ShapeDtypeStruct((8, hd), F32)],
        compiler_params=_params(),
    )(qkv, dq, dk, dv, cos, sin, gains)


def _attn_in_bwd(dqkv, wqkv, hc, dh1, mods, *, nct, tm):
    rows, d = hc.shape
    qw = dqkv.shape[1]

    def body(dz_ref, w_ref, h_ref, g_ref, md_ref, dh_ref, dmd_ref):
        i = pl.program_id(0)
        md = md_ref[...]
        da = _dot_nt_blocks(dz_ref[...], w_ref)
        _, parts = _normmod(h_ref[...], md, 0)
        dx, dsh, dsc, dng = _normmod_bwd(da, parts, md, 0)
        dh_ref[...] = g_ref[...] * (i >= nct).astype(F32) + dx
        _acc_rows(dmd_ref, _first_of_group(i, nct, 2), {SH1: dsh, SC1: dsc, NG0: dng})

    return pl.pallas_call(
        body, name="attn_in_bwd", grid=(rows // tm,),
        in_specs=[_rows(tm, qw), _full(wqkv.shape), _rows(tm, d),
                  pl.BlockSpec((tm, d), lambda i: (jnp.maximum(i - nct, 0), 0)), _mods_spec(nct, 2, d)],
        out_specs=[_rows(tm, d), _mods_spec(nct, 2, d)],
        out_shape=[jax.ShapeDtypeStruct((rows, d), F32), jax.ShapeDtypeStruct((2, 8, d), F32)],
        compiler_params=_params(),
    )(dqkv, wqkv, hc, dh1, mods)


def _gmlp_gate(z, lng, lnb, ws_ref, bs_ref, gg, ch):
    half = z.shape[1] // 2
    ggw = half // gg
    u, v = z[:, :half], z[:, half:]
    vc = v - jnp.mean(v, axis=-1, keepdims=True)
    rs = lax.rsqrt(jnp.mean(vc * vc, axis=-1, keepdims=True) + EPS)
    vhat = vc * rs
    vln = (vhat * lng + lnb).astype(BF16)
    chunks = []
    for n in range(z.shape[0] // ch):
        groups = []
        for g in range(gg):
            groups.append(_dot(ws_ref[g], vln[n * ch:(n + 1) * ch, g * ggw:(g + 1) * ggw]) + bs_ref[g])
        chunks.append(jnp.concatenate(groups, axis=1))
    sv = jnp.concatenate(chunks, axis=0) if len(chunks) > 1 else chunks[0]
    return u, sv, vhat, rs, vln


def _gmlp_fwd(h, mods, w_in, lng, lnb, ws, bs, w_out, *, tm):
    seq, d = h.shape
    zw = w_in.shape[0] * w_in.shape[-1]
    half = zw // 2
    gg, ch = ws.shape[0], ws.shape[-1]

    def body(h_ref, md_ref, win_ref, lng_ref, lnb_ref, ws_ref, bs_ref, wout_ref, h1_ref, zp_ref, y_ref):
        x = h_ref[...]
        md = md_ref[...]
        a, _ = _normmod(x, md, 0)
        zp = _dot_blocks(a.astype(BF16), win_ref)
        zp_ref[...] = zp.astype(BF16)
        u, sv, _, _, _ = _gmlp_gate(_gelu(zp), lng_ref[...], lnb_ref[...], ws_ref, bs_ref, gg, ch)
        y = _dot((u * sv).astype(BF16), wout_ref[...])
        y_ref[...] = y.astype(BF16)
        h1_ref[...] = x + md[G1:G1 + 1] * y

    return pl.pallas_call(
        body, name="gmlp_fwd", grid=(seq // tm,),
        in_specs=[_rows(tm, d), pl.BlockSpec((None, 8, d), lambda i: (1, 0, 0)),
                  _full(w_in.shape), _full((1, half)), _full((1, half)),
                  _full((gg, ch, ch)), _full((gg, ch, 1)), _full((half, d))],
        out_specs=[_rows(tm, d), _rows(tm, zw), _rows(tm, d)],
        out_shape=[jax.ShapeDtypeStruct((seq, d), F32), jax.ShapeDtypeStruct((seq, zw), BF16),
                   jax.ShapeDtypeStruct((seq, d), BF16)],
        compiler_params=_params(),
    )(h, mods, w_in, lng, lnb, ws, bs, w_out)


def _gmlp_bwd(h, dh1, zpre, y, mods, w_in, lng, lnb, ws, ws_t, bs, w_out, *, tm):
    seq, d = h.shape
    zw = w_in.shape[0] * w_in.shape[-1]
    half = zw // 2
    gg, ch = ws.shape[0], ws.shape[-1]
    ggw = half // gg

    def body(h_ref, g_ref, zp_ref, y_ref, md_ref, win_ref, lng_ref, lnb_ref, ws_ref, wst_ref, bs_ref, wout_ref,
             dh_ref, dzp_ref, gated_ref, dyb_ref, ab_ref, dmd_ref, dln_ref, dws_ref, dbs_ref):
        i = pl.program_id(0)
        x = h_ref[...]
        md = md_ref[...]
        a, parts = _normmod(x, md, 0)
        ab_ref[...] = a.astype(BF16)
        zp = zp_ref[...].astype(F32)
        lng_v = lng_ref[...]
        u, sv, vhat, rs, vln = _gmlp_gate(_gelu(zp), lng_v, lnb_ref[...], ws_ref, bs_ref, gg, ch)
        g = g_ref[...]
        dg1 = _rowsum(g * y_ref[...].astype(F32))
        dyb = (g * md[G1:G1 + 1]).astype(BF16)
        dyb_ref[...] = dyb
        gated_ref[...] = (u * sv).astype(BF16)
        dgated = _dot_nt(dyb, wout_ref[...])
        du = dgated * sv
        dsv = dgated * u

        @pl.when(i == 0)
        def _():
            dws_ref[...] = jnp.zeros(dws_ref.shape, F32)
            dbs_ref[...] = jnp.zeros(dbs_ref.shape, F32)
            dln_ref[...] = jnp.zeros(dln_ref.shape, F32)

        chunks = []
        for n in range(tm // ch):
            groups = []
            for gi in range(gg):
                blk = dsv[n * ch:(n + 1) * ch, gi * ggw:(gi + 1) * ggw]
                dbs_ref[gi] += jnp.sum(blk, axis=-1, keepdims=True)
                blk_b = blk.astype(BF16)
                dws_ref[gi] += _dot_nt(blk_b, vln[n * ch:(n + 1) * ch, gi * ggw:(gi + 1) * ggw])
                groups.append(_dot(wst_ref[gi], blk_b))
            chunks.append(jnp.concatenate(groups, axis=1))
        dvln = jnp.concatenate(chunks, axis=0) if len(chunks) > 1 else chunks[0]
        dln_ref[0:1, :] += _rowsum(dvln * vhat)
        dln_ref[1:2, :] += _rowsum(dvln)
        dvhat = dvln * lng_v
        dv = rs * (dvhat - jnp.mean(dvhat, axis=-1, keepdims=True)
                   - vhat * jnp.mean(dvhat * vhat, axis=-1, keepdims=True))
        dzp = (jnp.concatenate([du, dv], axis=1) * _gelu_grad(zp)).astype(BF16)
        dzp_ref[...] = dzp
        da = _dot_nt_blocks(dzp, win_ref)
        dx, dsh, dsc, dng = _normmod_bwd(da, parts, md, 0)
        dh_ref[...] = g + dx
        _acc_rows(dmd_ref, i == 0, {SH1: dsh, SC1: dsc, G1: dg1, NG0: dng})

    return pl.pallas_call(
        body, name="gmlp_bwd", grid=(seq // tm,),
        in_specs=[_rows(tm, d), _rows(tm, d), _rows(tm, zw), _rows(tm, d),
                  pl.BlockSpec((None, 8, d), lambda i: (1, 0, 0)),
                  _full(w_in.shape), _full((1, half)), _full((1, half)),
                  _full((gg, ch, ch)), _full((gg, ch, ch)), _full((gg, ch, 1)), _full((half, d))],
        out_specs=[_rows(tm, d), _rows(tm, zw), _rows(tm, half), _rows(tm, d), _rows(tm, d),
                   pl.BlockSpec((None, 8, d), lambda i: (0, 0, 0)), _full((8, half)), _full((gg, ch, ch)),
                   _full((gg, ch, 1))],
        out_shape=[jax.ShapeDtypeStruct((seq, d), F32), jax.ShapeDtypeStruct((seq, zw), BF16),
                   jax.ShapeDtypeStruct((seq, half), BF16), jax.ShapeDtypeStruct((seq, d), BF16),
                   jax.ShapeDtypeStruct((seq, d), BF16), jax.ShapeDtypeStruct((1, 8, d), F32),
                   jax.ShapeDtypeStruct((8, half), F32), jax.ShapeDtypeStruct((gg, ch, ch), F32),
                   jax.ShapeDtypeStruct((gg, ch, 1), F32)],
        compiler_params=_params(),
    )(h, dh1, zpre, y, mods, w_in, lng, lnb, ws, ws_t, bs, w_out)


def _final_loss(h, tgt, fg, *, tm):
    seq, d = h.shape

    def body(h_ref, t_ref, g_ref, dh_ref, acc_ref):
        i = pl.program_id(0)
        gain = g_ref[...]
        xhat, r = _rms_parts(h_ref[...])
        err = xhat * gain - t_ref[...]
        dy = err * (1.0 / d)
        dxhat = dy * gain
        dh_ref[...] = r * (dxhat - xhat * jnp.mean(dxhat * xhat, axis=-1, keepdims=True))
        part = jnp.sum(_rowsum(err * err), axis=-1, keepdims=True) * (0.5 / d)
        _acc_rows(acc_ref, i == 0, {0: _rowsum(dy * xhat), 1: jnp.broadcast_to(part, (1, d))})

    return pl.pallas_call(
        body, name="final_loss", grid=(seq // tm,),
        in_specs=[_rows(tm, d), _rows(tm, d), _full((1, d))],
        out_specs=[_rows(tm, d), _full((8, d))],
        out_shape=[jax.ShapeDtypeStruct((seq, d), F32), jax.ShapeDtypeStruct((8, d), F32)],
        compiler_params=_params(),
    )(h, tgt, fg)


def _ada_fwd(c_all, ada_w, ada_b_cols):
    depth, d, ncs = ada_w.shape

    def body(c_ref, w_ref, b_ref, o_ref):
        s = _silu(c_ref[...]).astype(BF16)
        o_ref[...] = _dot(s, w_ref[...].astype(BF16)) + b_ref[...]

    return pl.pallas_call(
        body, name="ada_fwd", grid=(depth,),
        in_specs=[_full((16, d)), pl.BlockSpec((None, d, ncs), lambda i: (i, 0, 0)),
                  pl.BlockSpec((None, 1, ncs), lambda i: (i, 0, 0))],
        out_specs=pl.BlockSpec((None, 16, ncs), lambda i: (i, 0, 0)),
        out_shape=jax.ShapeDtypeStruct((depth, 16, ncs), F32),
        compiler_params=_params(),
    )(c_all, ada_w, ada_b_cols.reshape(depth, 1, ncs))


def _ada_bwd(c_all, c_all_t, dmod, ada_w):
    depth, d, ncs = ada_w.shape

    def body(c_ref, ct_ref, dm_ref, w_ref, gw_ref, dc_ref):
        i = pl.program_id(0)
        dm = dm_ref[...]
        dctx = _rowsum(dm[8:16])
        rid = lax.broadcasted_iota(jnp.int32, (8, ncs), 0)
        low = jnp.where(rid == 0, jnp.broadcast_to(dctx, (8, ncs)), 0.0)
        dm16 = jnp.concatenate([dm[0:8], low], axis=0).astype(BF16)
        gw_ref[...] = _dot(_silu(ct_ref[...]).astype(BF16), dm16)

        @pl.when(i == 0)
        def _():
            dc_ref[...] = jnp.zeros(dc_ref.shape, F32)

        dc_ref[...] += _dot_nt(low.astype(BF16), w_ref[...].astype(BF16)) * _silu_grad(c_ref[8:9, :])

    return pl.pallas_call(
        body, name="ada_bwd", grid=(depth,),
        in_specs=[_full((16, d)), _full((d, 16)), pl.BlockSpec((None, 16, ncs), lambda i: (i, 0, 0)),
                  pl.BlockSpec((None, d, ncs), lambda i: (i, 0, 0))],
        out_specs=[pl.BlockSpec((None, d, ncs), lambda i: (i, 0, 0)), _full((8, d))],
        out_shape=[jax.ShapeDtypeStruct((depth, d, ncs), F32), jax.ShapeDtypeStruct((8, d), F32)],
        compiler_params=_params(),
    )(c_all, c_all_t, dmod, ada_w)


def _adamw_math(w, g, m, v):
    m = ADAM_B1 * m + (1.0 - ADAM_B1) * g
    v = ADAM_B2 * v + (1.0 - ADAM_B2) * jnp.square(g)
    m_hat = m * (1.0 / (1.0 - ADAM_B1 ** ADAM_STEP))
    v_hat = v * (1.0 / (1.0 - ADAM_B2 ** ADAM_STEP))
    delta = -ADAM_LR * (m_hat / (jnp.sqrt(v_hat) + ADAM_EPS) + ADAM_WD * w)
    return delta, m, v


def _adamw(ga, gb, w, m, v, name):
    rows, cols = w.shape
    tr = rows
    while tr * cols * 4 > (1 << 20) and tr % 16 == 0:
        tr //= 2
    grads = [ga] if gb is None else [ga, gb]

    def body(*refs):
        w_ref, m_ref, v_ref, g_out, d_out, m_out, v_out = refs[len(grads):]
        g = refs[0][...] if gb is None else refs[0][...] + refs[1][...]
        delta, m_new, v_new = _adamw_math(w_ref[...], g, m_ref[...], v_ref[...])
        g_out[...] = g
        d_out[...] = delta
        m_out[...] = m_new
        v_out[...] = v_new

    spec = _rows(tr, cols)
    return pl.pallas_call(
        body, name=name, grid=(rows // tr,),
        in_specs=[spec] * (len(grads) + 3), out_specs=[spec] * 4,
        out_shape=[jax.ShapeDtypeStruct((rows, cols), F32)] * 4,
        compiler_params=_params(),
    )(*grads, w, m, v)


def _sum_devices(gathered, name):
    n, rows, cols = gathered.shape
    tr = rows
    while tr * cols * 4 * n > (4 << 20) and tr % 16 == 0:
        tr //= 2

    def body(x_ref, o_ref):
        acc = x_ref[0]
        for j in range(1, n):
            acc = acc + x_ref[j]
        o_ref[...] = acc

    return pl.pallas_call(
        body, name=name, grid=(rows // tr,),
        in_specs=[pl.BlockSpec((n, tr, cols), lambda i: (0, i, 0))], out_specs=_rows(tr, cols),
        out_shape=jax.ShapeDtypeStruct((rows, cols), F32),
        compiler_params=_params(),
    )(gathered)


def _sum_partials(blocked, landeds, chip, name):
    n = len(blocked)
    cols = blocked[0].shape[-1]
    blocked = [b.reshape(N_CHIPS, -1, cols) for b in blocked]
    landeds = [l.reshape(3, -1, cols) for l in landeds]
    rows = blocked[0].shape[1]
    tr = rows
    while tr * cols * 2 * n > (1 << 20) and tr % 32 == 0:
        tr //= 2

    def body(chip_ref, *refs):
        out_ref = refs[-1]
        for li in range(n):
            acc = refs[li][...].astype(F32)
            for p in range(3):
                acc = acc + refs[n + li][p].astype(F32)
            out_ref[li] = acc

    out = pl.pallas_call(
        body, name=name,
        grid_spec=pltpu.PrefetchScalarGridSpec(
            num_scalar_prefetch=1, grid=(rows // tr,),
            in_specs=[pl.BlockSpec((None, tr, cols), lambda i, k: (k[0], i, 0))] * n
            + [pl.BlockSpec((3, tr, cols), lambda i, k: (0, i, 0))] * n,
            out_specs=pl.BlockSpec((n, tr, cols), lambda i, k: (0, i, 0))),
        out_shape=jax.ShapeDtypeStruct((n, rows, cols), F32),
        compiler_params=_params(),
    )(jnp.reshape(chip, (1,)).astype(jnp.int32), *blocked, *landeds)
    return out.reshape(n * rows, cols)


def _my_place():
    return lax.axis_index("x"), lax.axis_index("y"), lax.axis_index("c")


def _other_chips(x, y):
    return [(1 - x, y), (x, 1 - y), (1 - x, 1 - y)]


def _all_gather_small(block, name):
    rows, cols = block.shape

    def body(x_ref, out_ref, send_sems, recv_sems, local_sem):
        x, y, c = _my_place()
        me, sibling = (x, y, c), (x, y, 1 - c)
        chips = _other_chips(x, y)

        def slot(px, py, pc):
            return out_ref.at[4 * px + 2 * py + pc]

        def copy(k, blk, to, src=None):
            return pltpu.make_async_remote_copy(
                src_ref=slot(*blk) if src is None else src, dst_ref=slot(*blk),
                send_sem=send_sems.at[k], recv_sem=recv_sems.at[k], device_id=to, device_id_type=MESH)

        mine = pltpu.make_async_copy(x_ref, slot(*me), local_sem)
        mine.start()
        first = [copy(0, me, sibling, src=x_ref)]
        first += [copy(1 + j, me, (*chip, c), src=x_ref) for j, chip in enumerate(chips)]
        for cp in first:
            cp.start()
        passed = [copy(4 + j, (*chip, c), sibling) for j, chip in enumerate(chips)]
        for j, chip in enumerate(chips):
            copy(1 + j, (*chip, c), me).wait_recv()
            passed[j].start()
        copy(0, sibling, me).wait_recv()
        for j, chip in enumerate(chips):
            copy(4 + j, (*chip, 1 - c), me).wait_recv()
        for cp in first + passed:
            cp.wait_send()
        mine.wait()

    return pl.pallas_call(
        body, name=name,
        out_shape=jax.ShapeDtypeStruct((N_DEV, rows, cols), block.dtype),
        in_specs=[pl.BlockSpec(memory_space=pltpu.VMEM)],
        out_specs=pl.BlockSpec(memory_space=pltpu.VMEM),
        scratch_shapes=[pltpu.SemaphoreType.DMA((7,)), pltpu.SemaphoreType.DMA((7,)), pltpu.SemaphoreType.DMA],
        compiler_params=_params(),
    )(block)


HBM_SPEC = pl.BlockSpec(memory_space=pltpu.HBM)
SEM_SPEC = pl.BlockSpec(memory_space=pltpu.SEMAPHORE)
DATAFLOW_EFFECT = pltpu.SideEffectType.DATAFLOW_SIDE_EFFECTING


def _same_core_of_other_chips(x, y, c):
    return [(*chip, c) for chip in _other_chips(x, y)]


def _sibling_core(x, y, c):
    return [(x, y, 1 - c)]


def _gather_views(src, land, p, x, y):
    return src, land.at[2 * x + y]


def _scatter_views(src, land, p, x, y):
    peer_chip = (2 * (1 - x) + y, 2 * x + (1 - y), 2 * (1 - x) + (1 - y))[p]
    return src.at[peer_chip], land.at[p]


def _whole_views(src, land, p, x, y):
    return src, land


GATHER_PLAN = (_same_core_of_other_chips, _gather_views, 3)
SCATTER_PLAN = (_same_core_of_other_chips, _scatter_views, 3)
SIBLING_PLAN = (_sibling_core, _whole_views, 1)


def _exchange_copies(srcs, lands, send_sems, recv_sems, plan):
    peers_of, views, n_peers = plan
    x, y, c = _my_place()
    copies = []
    for j, (src, land) in enumerate(zip(srcs, lands)):
        for p, peer in enumerate(peers_of(x, y, c)):
            s_view, d_view = views(src, land, p, x, y)
            k = n_peers * j + p
            copies.append(pltpu.make_async_remote_copy(
                src_ref=s_view, dst_ref=d_view, send_sem=send_sems.at[k], recv_sem=recv_sems.at[k],
                device_id=peer, device_id_type=MESH))
    return copies


def _exchange_start(srcs, lands, plan, name):
    n = len(srcs)

    def body(*refs):
        send_sems, recv_sems = refs[2 * n], refs[2 * n + 1]
        token = refs[-1]
        for cp in _exchange_copies(refs[:n], refs[n:2 * n], send_sems, recv_sems, plan):
            cp.start()
        token[...] = jnp.zeros(token.shape, token.dtype)

    operands = [pltpu.with_memory_space_constraint(a, pltpu.HBM) for a in (*srcs, *lands)]
    out = pl.pallas_call(
        body, name=name,
        out_shape=(pltpu.SemaphoreType.DMA((plan[2] * n,)), pltpu.SemaphoreType.DMA((plan[2] * n,)),
                   *[pltpu.HBM(a.shape, a.dtype) for a in operands], jax.ShapeDtypeStruct((8, LANES), F32)),
        in_specs=[HBM_SPEC] * (2 * n),
        out_specs=(SEM_SPEC, SEM_SPEC, *[HBM_SPEC] * (2 * n), pl.BlockSpec(memory_space=pltpu.VMEM)),
        input_output_aliases={i: 2 + i for i in range(2 * n)},
        compiler_params=pltpu.CompilerParams(has_side_effects=DATAFLOW_EFFECT),
    )(*operands)
    return out[0], out[1], list(out[2:2 + n]), list(out[2 + n:2 + 2 * n]), out[-1]


def _exchange_wait(send_sems, recv_sems, srcs, lands, plan, after, name):
    n = len(srcs)

    def body(*refs):
        send, recv = refs[2 * n], refs[2 * n + 1]
        for cp in _exchange_copies(refs[:n], refs[n:2 * n], send, recv, plan):
            cp.wait_send()
            cp.wait_recv()

    out = pl.pallas_call(
        body, name=name,
        out_shape=tuple(pltpu.HBM(a.shape, a.dtype) for a in (*srcs, *lands)),
        in_specs=[HBM_SPEC] * (2 * n) + [SEM_SPEC, SEM_SPEC, HBM_SPEC],
        out_specs=tuple([HBM_SPEC] * (2 * n)),
        input_output_aliases={i: i for i in range(2 * n)},
        compiler_params=pltpu.CompilerParams(has_side_effects=DATAFLOW_EFFECT),
    )(*srcs, *lands, send_sems, recv_sems, pltpu.with_memory_space_constraint(after, pltpu.HBM))
    return list(out[:n]), list(out[n:])


def _landing_for_gather(shard, chip):
    land = lax.empty((N_CHIPS, *shard.shape), shard.dtype)
    return lax.dynamic_update_index_in_dim(land, shard, chip, 0)


TILE_ELEMS = SUBLANES * LANES


def _pack(arrays):
    parts = []
    for a in arrays:
        flat = a.reshape(-1).astype(F32)
        pad = (-flat.shape[0]) % TILE_ELEMS
        if pad:
            flat = jnp.concatenate([flat, jnp.zeros((pad,), F32)])
        parts.append(flat.reshape(-1, LANES))
    return jnp.concatenate(parts, axis=0) if len(parts) > 1 else parts[0]


def _unpack(buf, shapes):
    out, r = [], 0
    lead = buf.shape[:-2]
    for shp in shapes:
        size = math.prod(shp)
        nr = -(-size // TILE_ELEMS) * SUBLANES
        flat = buf[..., r:r + nr, :].reshape(*lead, nr * LANES)[..., :size]
        out.append(flat.reshape(*lead, *shp))
        r += nr
    return out


def _chip_cols(a, k, width):
    return lax.dynamic_slice_in_dim(a, k * width, width, axis=a.ndim - 1)


def _across_chips(gathered, c0_only_shape):
    return gathered.reshape(2, 2, 2, *c0_only_shape)[:, :, 0].reshape(N_CHIPS, *c0_only_shape)


def kernel(x, c, ctx, c_ctx, ada_w, ada_b, norm_g, mlp_w1, mlp_w2, pool_w, pool_scale, attn_w_qkv, attn_w_o, attn_q_g, attn_k_g, gm_w_in, gm_ln_g, gm_ln_b, gm_ws, gm_bs, gm_w_out, final_g, loss_target, m_c_ctx, m_ada_w, m_ada_b, m_norm_g, m_mlp_w1, m_mlp_w2, m_pool_w, m_pool_scale, m_attn_w_qkv, m_attn_w_o, m_attn_q_g, m_attn_k_g, m_gm_w_in, m_gm_ln_g, m_gm_ln_b, m_gm_ws, m_gm_bs, m_gm_w_out, m_final_g, v_c_ctx, v_ada_w, v_ada_b, v_norm_g, v_mlp_w1, v_mlp_w2, v_pool_w, v_pool_scale, v_attn_w_qkv, v_attn_w_o, v_attn_q_g, v_attn_k_g, v_gm_w_in, v_gm_ln_g, v_gm_ln_b, v_gm_ws, v_gm_bs, v_gm_w_out, v_final_g):
    seq, d = x.shape[1], x.shape[2]
    n_ctx = ctx.shape[1]
    total = n_ctx + seq
    hd = attn_q_g.shape[-1]
    nh = d // hd
    nkv = nh // 2
    gg, ch = gm_ws.shape[1], gm_ws.shape[-1]
    half = gm_w_out.shape[1] * N_CHIPS
    pgw = pool_w.shape[-1]
    tm = min(256, n_ctx)
    nct = n_ctx // tm
    seg_lens = (n_ctx, seq)

    mx, my, mc = _my_place()
    chip = 2 * mx + my
    me = 4 * mx + 2 * my + mc

    c_rows = jnp.concatenate([c, jnp.zeros((7, d), F32)], axis=0)
    c_gath = _all_gather_small(c_rows, "gather_cond")[:, 0, :]
    c_all = jnp.concatenate([c_gath, c_ctx[None, :], jnp.zeros((7, d), F32)], axis=0)
    ncs = ada_w.shape[-1]
    ada_cols = _ada_fwd(c_all, ada_w, _chip_cols(ada_b, chip, ncs))
    small_shapes = [ada_cols.shape, norm_g.shape, pool_scale.shape, gm_ln_g.shape, gm_ln_b.shape]
    gathered = _all_gather_small(_pack([ada_cols, norm_g, pool_scale, gm_ln_g, gm_ln_b]), "gather_small_params")
    per_chip = _across_chips(gathered, gathered.shape[1:])
    ada_g, ng_g, ps_g, lng_g, lnb_g = _unpack(per_chip, small_shapes)

    def join_last(a):
        return jnp.moveaxis(a, 0, -2).reshape(*a.shape[1:-1], N_CHIPS * a.shape[-1])

    ada_full = join_last(ada_g)
    ng_full = join_last(ng_g)
    ps_full = join_last(ps_g)
    lng_full = join_last(lng_g)
    lnb_full = join_last(lnb_g)
    mod_lat = lax.dynamic_slice_in_dim(ada_full, me, 1, axis=1).reshape(DEPTH, 6, d)
    mod_ctx = ada_full[:, 8].reshape(DEPTH, 6, d)
    mods = jnp.stack([jnp.concatenate([mod_ctx, ng_full], axis=1), jnp.concatenate([mod_lat, ng_full], axis=1)],
                     axis=1)

    weight_groups = [
        [pool_w],
        [mlp_w1[0]],
        [mlp_w2[0]],
        [attn_w_qkv[0], attn_w_o[0]],
        [mlp_w1[1], mlp_w2[1], mlp_w1[2], mlp_w2[2], gm_w_in[0], gm_w_out[0], mlp_w1[3], mlp_w2[3]],
    ]
    gathers = [None] * len(weight_groups)

    def gather_start(gi, after):
        shards, _ = lax.optimization_barrier(([w.astype(BF16) for w in weight_groups[gi]], after))
        lands = [_landing_for_gather(s, chip) for s in shards]
        gathers[gi] = _exchange_start(shards, lands, GATHER_PLAN, f"gather_weights_{gi}_start")
        return gathers[gi][4][0:1, 0:1]

    def gathered(gi, after):
        send, recv, srcs, lands, _ = gathers[gi]
        return _exchange_wait(send, recv, srcs, lands, GATHER_PLAN, after, f"gather_weights_{gi}_wait")[1]

    def rows_joined(a):
        return a.reshape(-1, a.shape[-1])

    w1_b, w2_b = [None] * DEPTH, [None] * DEPTH
    gather_start(0, mods)
    behind_gather_1 = gather_start(1, mods)
    pw_land, = gathered(0, ps_full)
    pw_f = jnp.transpose(pw_land, (1, 2, 0, 3, 4)).reshape(pool_w.shape[0], pool_w.shape[1], pgw, pgw)

    gains = jnp.concatenate([attn_q_g, attn_k_g, jnp.zeros((6, hd), F32)], axis=0)
    ws_b = gm_ws[0].astype(BF16)
    ws_t = jnp.swapaxes(gm_ws[0], 1, 2).astype(BF16)
    bs_col = gm_bs[0][:, :, None]
    cos, sin = _rope_tables(n_ctx, seq, hd)
    lat = lambda i: mods[i, 1:2]

    hc0 = jnp.concatenate([ctx[0] + behind_gather_1, x[0]], axis=0)
    ha0, p0 = _pool_fwd(hc0, mods[0] + behind_gather_1, pw_f, ps_full, 0, nct=nct, tm=tm, seg_lens=seg_lens)
    w1_b[0], = gathered(1, ha0)
    u0 = _mlp_up(ha0, mods[0] + gather_start(2, w1_b[0]), w1_b[0], 0, nct=nct, tm=tm)
    w2_b[0], = gathered(2, u0)
    hc1, o0 = _mlp_down(ha0, u0, mods[0] + gather_start(3, w2_b[0]), w2_b[0], 0, nct=nct, tm=tm)
    wqkv_b, wo_land = gathered(3, hc1)
    mods1 = mods[1] + gather_start(4, wqkv_b)
    wo_f = rows_joined(wo_land)
    xa1, qkv, q_r, k_r, v_b = _qkv_fwd(hc1, mods1, wqkv_b, cos, sin, gains, nh=nh, nkv=nkv, nct=nct, tm=tm)
    o_att, lse = _flash_fwd(q_r, k_r, v_b, n_ctx=n_ctx, hd=hd)
    ha1, y1 = _proj_fwd(o_att, wo_f, hc1, mods1, n_ctx=n_ctx, tm=tm)
    w1_b[1], w2_b[1], w1_b[2], w2_b[2], win_b, wout_land, w1_b[3], w2_b[3] = gathered(4, ha1)
    h2, u1, o1 = _mlp_fwd(ha1, lat(1), w1_b[1], w2_b[1], 1, nct=0, tm=tm)
    wout_f = rows_joined(wout_land)
    ha2, zpre, y2 = _gmlp_fwd(h2, mods[2], win_b, lng_full, lnb_full, ws_b, bs_col, wout_f, tm=tm)
    h3, u2, o2 = _mlp_fwd(ha2, lat(2), w1_b[2], w2_b[2], 2, nct=0, tm=tm)
    ha3, p3 = _pool_fwd(h3, lat(3), pw_f, ps_full, 3, nct=0, tm=tm, seg_lens=seg_lens)
    h4, u3, o3 = _mlp_fwd(ha3, lat(3), w1_b[3], w2_b[3], 3, nct=0, tm=tm)
    dh4, fin_acc = _final_loss(h4, loss_target[0], final_g[None, :], tm=tm)

    dmods = [None] * DEPTH
    scatters = [None] * (DEPTH + 2)

    def blocked_rows(g):
        return g.reshape(N_CHIPS, g.shape[1] // N_CHIPS, g.shape[2])

    def blocked_pool(dpw):
        pg = dpw.shape[0]
        return jnp.transpose(dpw.astype(BF16).reshape(pg, N_CHIPS, pgw // N_CHIPS, pgw), (1, 0, 2, 3))

    def scatter_start(i, grads):
        lands = [lax.empty((3, *g.shape[1:]), g.dtype) for g in grads]
        scatters[i] = _exchange_start(grads, lands, SCATTER_PLAN, f"scatter_grads_{i}_start")
        return scatters[i][4][0:1, 0:1]

    def mlp_back(i, h_in, dh_out, u, o, md, n_ct):
        dh_in, du, dob, mb, dmd = _mlp_bwd(h_in, dh_out, u, o, md, w1_b[i], w2_b[i], i, nct=n_ct, tm=tm)
        dw1 = _mm_tn(mb, du, f"mlp_dw1_{i}", col_blocks=N_CHIPS)
        dw2 = blocked_rows(_mm_tn(u, dob, f"mlp_dw2_{i}", relu2=True))
        return dh_in, dmd, [dw1, dw2]

    def pool_back(i, h_in, p_in, dh_out, md, n_ct):
        dp, dmd_a, dps, dpw = _pool_bwd_weights(p_in, dh_out, md, pw_f, ps_full, i, nct=n_ct, tm=tm)
        dh_in, dmd_b = _pool_bwd_input(dp, h_in, dh_out, md, i, nct=n_ct, tm=tm, seg_lens=seg_lens, gw=pgw)
        return dh_in, dmd_a + dmd_b, dps, dpw

    zero_grp = jnp.zeros((1, 8, d), F32)
    dha3, dmd3, dws3 = mlp_back(3, ha3, dh4, u3, o3, lat(3), 0)
    dh3, dmd3p, dps3, dpw3 = pool_back(3, h3, p3, dha3, lat(3), 0)
    dmods[3] = jnp.concatenate([zero_grp, dmd3 + dmd3p], axis=0)
    tok = scatter_start(3, dws3 + [blocked_pool(dpw3)])
    dha2, dmd2, dws2 = mlp_back(2, ha2, dh3, u2, o2, lat(2) + tok, 0)
    dh2, dzpre, gated, dyb2, ab2, dmd2g, dln, dws, dbs = _gmlp_bwd(
        h2, dha2, zpre, y2, mods[2], win_b, lng_full, lnb_full, ws_b, ws_t, bs_col, wout_f, tm=tm)
    dwin = _mm_tn(ab2, dzpre, "gmlp_dw_in", col_blocks=N_CHIPS)
    dwout = blocked_rows(_mm_tn(gated, dyb2, "gmlp_dw_out"))
    dmods[2] = jnp.concatenate([zero_grp, dmd2 + dmd2g], axis=0)
    tok = scatter_start(2, dws2 + [dwin, dwout])
    dha1, dmd1, dws1 = mlp_back(1, ha1, dh2, u1, o1, lat(1) + tok, 0)
    do_att, dyb1, dmd1p = _proj_bwd(dha1, y1, mods[1], wo_f, tm=tm)
    dwo = blocked_rows(_mm_tn(o_att, dyb1, "attn_dw_o"))
    dq, dk, dv = _flash_bwd(q_r, k_r, v_b, o_att, do_att, lse, n_ctx=n_ctx, hd=hd)
    dqkv, dgains = _qkv_bwd(qkv, dq, dk, dv, cos, sin, gains, nh=nh, nkv=nkv, nct=nct, tm=tm)
    dwqkv = _mm_tn(xa1, dqkv, "attn_dw_qkv", col_blocks=N_CHIPS)
    dhc1, dmd1i = _attn_in_bwd(dqkv, wqkv_b, hc1, dha1, mods[1], nct=nct, tm=tm)
    dmods[1] = dmd1i + jnp.concatenate([zero_grp, dmd1 + dmd1p], axis=0)
    tok = scatter_start(1, dws1 + [dwqkv, dwo])
    dha0, du0, dob0, mb0, dmd0 = _mlp_bwd(ha0, dhc1, u0, o0, mods[0] + tok, w1_b[0], w2_b[0], 0, nct=nct, tm=tm)
    scatter_start(DEPTH + 1, [blocked_rows(_mm_tn(u0, dob0, "mlp_dw2_0", relu2=True))])
    dw1_0 = _mm_tn(mb0, du0, "mlp_dw1_0", col_blocks=N_CHIPS, after=scatters[DEPTH + 1][4])
    tok = scatter_start(0, [dw1_0])
    dhc0, dmd0p, dps0, dpw0 = pool_back(0, hc0, p0, dha0, mods[0] + tok, nct)
    dmods[0] = dmd0 + dmd0p
    grad_x = dhc0[None]
    scatter_start(DEPTH, [blocked_pool(dpw0)])

    dmods_all = jnp.stack(dmods, axis=0)
    small_grads = [dmods_all, dws, dbs, dgains, dln, dps0, dps3, fin_acc]
    sg_shapes = [a.shape for a in small_grads]
    sg_gath = _all_gather_small(_pack(small_grads), "gather_small_grads")
    sg_sum = _sum_devices(sg_gath, "sum_small_grads")
    s_dmods, s_dws, s_dbs, s_dgains, s_dln, s_dps0, s_dps3, s_fin = _unpack(sg_sum, sg_shapes)
    loss = s_fin[1, 0]

    sources, landed = [None] * len(scatters), [None] * len(scatters)
    for i in (3, 2, 1, DEPTH + 1, 0, DEPTH):
        send, recv, srcs, lands, _ = scatters[i]
        sources[i], landed[i] = _exchange_wait(send, recv, srcs, lands, SCATTER_PLAN, sg_sum, f"scatter_grads_{i}_wait")

    def summed(name, picks):
        return _sum_partials([sources[i][j] for i, j in picks], [landed[i][j] for i, j in picks], chip,
                             f"sum_chips_{name}")

    big = [("mlp_w1", mlp_w1, m_mlp_w1, v_mlp_w1, [(i, 0) for i in range(DEPTH)]),
           ("mlp_w2", mlp_w2, m_mlp_w2, v_mlp_w2, [(DEPTH + 1, 0)] + [(i, 1) for i in range(1, DEPTH)]),
           ("pool_w", pool_w, m_pool_w, v_pool_w, [(DEPTH, 0), (3, 2)]),
           ("attn_w_qkv", attn_w_qkv, m_attn_w_qkv, v_attn_w_qkv, [(1, 2)]),
           ("attn_w_o", attn_w_o, m_attn_w_o, v_attn_w_o, [(1, 3)]),
           ("gm_w_in", gm_w_in, m_gm_w_in, v_gm_w_in, [(2, 2)]),
           ("gm_w_out", gm_w_out, m_gm_w_out, v_gm_w_out, [(2, 3)])]
    partial = [summed(name, picks) for name, _, _, _, picks in big]
    swap = _exchange_start(partial, [lax.empty(p.shape, p.dtype) for p in partial], SIBLING_PLAN,
                           "swap_with_sibling_start")
    behind_swap = swap[4][0:1, 0:1]

    dm_dev = _unpack(sg_gath, sg_shapes[:1])[0]
    dm_lat = jnp.moveaxis(dm_dev[:, :, 1, :6, :], 0, 1).reshape(DEPTH, N_DEV, 6 * d)
    dm_ctx = jnp.moveaxis(dm_dev[:, :, 0, :6, :], 0, 1).reshape(DEPTH, N_DEV, 6 * d)
    dmod16 = _chip_cols(jnp.concatenate([dm_lat, dm_ctx], axis=1), chip, ncs) + behind_swap
    g_ada_w, dcc_part = _ada_bwd(c_all, c_all.T, dmod16, ada_w)
    dcc_gath = _all_gather_small(dcc_part, "gather_d_c_ctx")
    dcc_chips = _across_chips(dcc_gath, dcc_gath.shape[1:])
    dcc_rows = _sum_devices(dcc_chips, "sum_d_c_ctx")
    dcc = dcc_rows[0]
    ada_res = _adamw(g_ada_w.reshape(-1, ncs), None, ada_w.reshape(-1, ncs),
                     m_ada_w.reshape(-1, ncs), v_ada_w.reshape(-1, ncs), "adamw_ada_w")

    partial, from_sibling = _exchange_wait(swap[0], swap[1], swap[2], swap[3], SIBLING_PLAN, ada_res[1],
                                           "swap_with_sibling_wait")
    big_out = {}
    for (name, w, m, v, _), mine, theirs in zip(big, partial, from_sibling):
        cols = w.shape[-1]
        res = _adamw(mine, theirs, w.reshape(-1, cols), m.reshape(-1, cols), v.reshape(-1, cols), f"adamw_{name}")
        big_out[name] = [r.reshape(w.shape) for r in res]
    big_out["ada_w"] = [r.reshape(ada_w.shape) for r in ada_res]

    def cols_of(a, width):
        return _chip_cols(a, chip, width)

    zero = lambda a: jnp.zeros(a.shape, F32)
    ngw = norm_g.shape[-1]
    small = {
        "c_ctx": (dcc, zero(dcc), c_ctx, m_c_ctx, v_c_ctx),
        "ada_b": (s_dmods[:, 0, :6].reshape(DEPTH, 6 * d), s_dmods[:, 1, :6].reshape(DEPTH, 6 * d), ada_b, m_ada_b,
                  v_ada_b),
        "norm_g": (cols_of(s_dmods[:, 0, 6:8], ngw), cols_of(s_dmods[:, 1, 6:8], ngw), norm_g, m_norm_g, v_norm_g),
        "pool_scale": (cols_of(jnp.stack([s_dps0[0], s_dps3[0]]), pool_scale.shape[-1]), zero(pool_scale),
                       pool_scale, m_pool_scale, v_pool_scale),
        "attn_q_g": (s_dgains[0:1], zero(attn_q_g), attn_q_g, m_attn_q_g, v_attn_q_g),
        "attn_k_g": (s_dgains[1:2], zero(attn_k_g), attn_k_g, m_attn_k_g, v_attn_k_g),
        "gm_ln_g": (cols_of(s_dln[0:1], gm_ln_g.shape[-1]), zero(gm_ln_g), gm_ln_g, m_gm_ln_g, v_gm_ln_g),
        "gm_ln_b": (cols_of(s_dln[1:2], gm_ln_b.shape[-1]), zero(gm_ln_b), gm_ln_b, m_gm_ln_b, v_gm_ln_b),
        "gm_ws": (s_dws[None], zero(gm_ws), gm_ws, m_gm_ws, v_gm_ws),
        "gm_bs": (s_dbs[None, :, :, 0], zero(gm_bs), gm_bs, m_gm_bs, v_gm_bs),
        "final_g": (s_fin[0], zero(final_g), final_g, m_final_g, v_final_g),
    }
    keys = list(small)
    packed = [_pack([small[k][t] for k in keys]) for t in range(5)]
    res = _adamw(*packed, "adamw_small")
    shapes = [small[k][2].shape for k in keys]
    small_out = {k: [] for k in keys}
    for r in res:
        for k, a in zip(keys, _unpack(r, shapes)):
            small_out[k].append(a)

    order = ["c_ctx", "ada_w", "ada_b", "norm_g", "mlp_w1", "mlp_w2", "pool_w", "pool_scale", "attn_w_qkv",
             "attn_w_o", "attn_q_g", "attn_k_g", "gm_w_in", "gm_ln_g", "gm_ln_b", "gm_ws", "gm_bs", "gm_w_out",
             "final_g"]
    allo = {**big_out, **small_out}
    outs = [loss, grad_x]
    for t in range(4):
        outs += [allo[k][t] for k in order]
    return tuple(outs)
```

```python
import math

import numpy as np
import jax
import jax.numpy as jnp
from jax import lax
from jax.experimental import pallas as pl
from jax.experimental.pallas import tpu as pltpu

F32 = jnp.float32
BF16 = jnp.bfloat16
MESH = pl.DeviceIdType.MESH

EPS = 1e-6
GRID_W = 64
ROPE_BASE = 10000.0
POOL_WINDOWS = (2, 4, 8, 16)
HALO = 8
DEPTH = 4
N_MIXERS = 3

ADAM_LR = 0.001
ADAM_B1 = 0.9
ADAM_B2 = 0.999
ADAM_EPS = 1e-08
ADAM_WD = 0.01
ADAM_STEP = 10

VMEM_LIMIT_BYTES = 56 * 1024 * 1024
LANES = 128
SUBLANES = 8
N_DEV = 8
N_CHIPS = 4

SH1, SC1, G1, SH2, SC2, G2, NG0, NG1 = range(8)


def _dot(a, b):
    return jnp.dot(a, b, preferred_element_type=F32)


def _dot_nt(a, b):
    return lax.dot_general(a, b, (((1,), (1,)), ((), ())), preferred_element_type=F32)


def _dot_tn(a, b):
    return lax.dot_general(a, b, (((0,), (0,)), ((), ())), preferred_element_type=F32)


def _dot_blocks(a, w_ref):
    return jnp.concatenate([_dot(a, w_ref[k]) for k in range(w_ref.shape[0])], axis=1)


def _dot_nt_blocks(a, w_ref):
    nb, _, w = w_ref.shape
    acc = _dot_nt(a[:, 0:w], w_ref[0])
    for k in range(1, nb):
        acc = acc + _dot_nt(a[:, k * w:(k + 1) * w], w_ref[k])
    return acc


def _params(**kw):
    return pltpu.CompilerParams(vmem_limit_bytes=VMEM_LIMIT_BYTES, **kw)


def _full(shape):
    nd = len(shape)
    return pl.BlockSpec(shape, lambda *_: (0,) * nd)


def _rows(tm, width):
    return pl.BlockSpec((tm, width), lambda i: (i, 0))


def _group_of(nct, groups):
    if groups == 1:
        return lambda i: 0
    return lambda i: jnp.where(i >= nct, 1, 0)


def _mods_spec(nct, groups, d):
    grp = _group_of(nct, groups)
    return pl.BlockSpec((None, 8, d), lambda i: (grp(i), 0, 0))


def _first_of_group(i, nct, groups):
    if groups == 1:
        return i == 0
    return jnp.logical_or(i == 0, i == nct)


def _rowsum(v):
    return jnp.sum(v, axis=0, keepdims=True)


def _rms_parts(x):
    r = lax.rsqrt(jnp.mean(x * x, axis=-1, keepdims=True) + EPS)
    return x * r, r


def _normmod(x, md, which):
    ng, sh, sc = (md[NG0:NG0 + 1], md[SH1:SH1 + 1], md[SC1:SC1 + 1]) if which == 0 else (
        md[NG1:NG1 + 1], md[SH2:SH2 + 1], md[SC2:SC2 + 1])
    xhat, r = _rms_parts(x)
    n = xhat * ng
    return n * (1.0 + sc) + sh, (xhat, r, n)


def _normmod_bwd(da, parts, md, which):
    xhat, r, n = parts
    ng, sc = (md[NG0:NG0 + 1], md[SC1:SC1 + 1]) if which == 0 else (md[NG1:NG1 + 1], md[SC2:SC2 + 1])
    dsh = _rowsum(da)
    dsc = _rowsum(da * n)
    dn = da * (1.0 + sc)
    dng = _rowsum(dn * xhat)
    dxhat = dn * ng
    dx = r * (dxhat - xhat * jnp.mean(dxhat * xhat, axis=-1, keepdims=True))
    return dx, dsh, dsc, dng


def _acc_rows(ref, first, rows):
    @pl.when(first)
    def _():
        ref[...] = jnp.zeros(ref.shape, ref.dtype)

    for r, v in rows.items():
        ref[r:r + 1, :] += v


def _shift_up(x, k):
    if k == 0:
        return x
    return pltpu.roll(x, x.shape[0] - k, axis=0)


def _gelu(x):
    k = math.sqrt(2.0 / math.pi)
    return 0.5 * x * (1.0 + jnp.tanh(k * (x + 0.044715 * x * x * x)))


def _gelu_grad(x):
    k = math.sqrt(2.0 / math.pi)
    t = jnp.tanh(k * (x + 0.044715 * x * x * x))
    return 0.5 * (1.0 + t) + 0.5 * x * (1.0 - t * t) * k * (1.0 + 3.0 * 0.044715 * x * x)


def _silu(x):
    return x / (1.0 + jnp.exp(-x))


def _silu_grad(x):
    s = 1.0 / (1.0 + jnp.exp(-x))
    return s * (1.0 + x * (1.0 - s))


def _mlp_fwd(h, mods, w1, w2, layer, *, nct, tm):
    rows, d = h.shape
    groups = mods.shape[0]
    nb, _, fc = w1.shape
    ff = nb * fc

    def body(h_ref, md_ref, w1_ref, w2_ref, h2_ref, u_ref, o_ref):
        x = h_ref[...]
        md = md_ref[...]
        m, _ = _normmod(x, md, 1)
        mb = m.astype(BF16)
        acc = jnp.zeros((tm, d), F32)
        for k in range(nb):
            u = _dot(mb, w1_ref[k])
            u_ref[:, k * fc:(k + 1) * fc] = u.astype(BF16)
            acc = acc + _dot(jnp.square(jnp.maximum(u, 0.0)).astype(BF16), w2_ref[k])
        o_ref[...] = acc.astype(BF16)
        h2_ref[...] = x + md[G2:G2 + 1] * acc

    return pl.pallas_call(
        body, name=f"mlp_fwd_{layer}", grid=(rows // tm,),
        in_specs=[_rows(tm, d), _mods_spec(nct, groups, d), _full(w1.shape), _full(w2.shape)],
        out_specs=[_rows(tm, d), _rows(tm, ff), _rows(tm, d)],
        out_shape=[jax.ShapeDtypeStruct((rows, d), F32), jax.ShapeDtypeStruct((rows, ff), BF16),
                   jax.ShapeDtypeStruct((rows, d), BF16)],
        compiler_params=_params(),
    )(h, mods, w1, w2)


def _mlp_up(h, mods, w1, layer, *, nct, tm):
    rows, d = h.shape
    groups = mods.shape[0]
    nb, _, fc = w1.shape

    def body(h_ref, md_ref, w1_ref, u_ref):
        m, _ = _normmod(h_ref[...], md_ref[...], 1)
        mb = m.astype(BF16)
        for k in range(nb):
            u_ref[:, k * fc:(k + 1) * fc] = _dot(mb, w1_ref[k]).astype(BF16)

    return pl.pallas_call(
        body, name=f"mlp_up_{layer}", grid=(rows // tm,),
        in_specs=[_rows(tm, d), _mods_spec(nct, groups, d), _full(w1.shape)],
        out_specs=_rows(tm, nb * fc), out_shape=jax.ShapeDtypeStruct((rows, nb * fc), BF16),
        compiler_params=_params(),
    )(h, mods, w1)


def _mlp_down(h, u, mods, w2, layer, *, nct, tm):
    rows, d = h.shape
    groups = mods.shape[0]
    nb, fc, _ = w2.shape

    def body(h_ref, u_ref, md_ref, w2_ref, h2_ref, o_ref):
        acc = jnp.zeros((tm, d), F32)
        for k in range(nb):
            uk = u_ref[:, k * fc:(k + 1) * fc].astype(F32)
            acc = acc + _dot(jnp.square(jnp.maximum(uk, 0.0)).astype(BF16), w2_ref[k])
        o_ref[...] = acc.astype(BF16)
        h2_ref[...] = h_ref[...] + md_ref[G2:G2 + 1, :] * acc

    return pl.pallas_call(
        body, name=f"mlp_down_{layer}", grid=(rows // tm,),
        in_specs=[_rows(tm, d), _rows(tm, nb * fc), _mods_spec(nct, groups, d), _full(w2.shape)],
        out_specs=[_rows(tm, d), _rows(tm, d)],
        out_shape=[jax.ShapeDtypeStruct((rows, d), F32), jax.ShapeDtypeStruct((rows, d), BF16)],
        compiler_params=_params(),
    )(h, u, mods, w2)


def _mlp_bwd(h1, dh2, u, o, mods, w1, w2, layer, *, nct, tm):
    rows, d = h1.shape
    groups = mods.shape[0]
    nb, _, fc = w1.shape
    ff = nb * fc

    def body(h_ref, g_ref, u_ref, o_ref, md_ref, w1_ref, w2_ref, dh_ref, du_ref, dob_ref, mb_ref, dmd_ref):
        i = pl.program_id(0)
        x = h_ref[...]
        g = g_ref[...]
        md = md_ref[...]
        m, parts = _normmod(x, md, 1)
        mb_ref[...] = m.astype(BF16)
        dg2 = _rowsum(g * o_ref[...].astype(F32))
        dob = (g * md[G2:G2 + 1]).astype(BF16)
        dob_ref[...] = dob
        dm = jnp.zeros((tm, d), F32)
        for k in range(nb):
            uk = u_ref[:, k * fc:(k + 1) * fc].astype(F32)
            dr = _dot_nt(dob, w2_ref[k])
            duk = (dr * (2.0 * jnp.maximum(uk, 0.0))).astype(BF16)
            du_ref[:, k * fc:(k + 1) * fc] = duk
            dm = dm + _dot_nt(duk, w1_ref[k])
        dx, dsh, dsc, dng = _normmod_bwd(dm, parts, md, 1)
        dh_ref[...] = g + dx
        _acc_rows(dmd_ref, _first_of_group(i, nct, groups), {SH2: dsh, SC2: dsc, G2: dg2, NG1: dng})

    return pl.pallas_call(
        body, name=f"mlp_bwd_{layer}", grid=(rows // tm,),
        in_specs=[_rows(tm, d), _rows(tm, d), _rows(tm, ff), _rows(tm, d), _mods_spec(nct, groups, d),
                  _full(w1.shape), _full(w2.shape)],
        out_specs=[_rows(tm, d), _rows(tm, ff), _rows(tm, d), _rows(tm, d), _mods_spec(nct, groups, d)],
        out_shape=[jax.ShapeDtypeStruct((rows, d), F32), jax.ShapeDtypeStruct((rows, ff), BF16),
                   jax.ShapeDtypeStruct((rows, d), BF16), jax.ShapeDtypeStruct((rows, d), BF16),
                   jax.ShapeDtypeStruct((groups, 8, d), F32)],
        compiler_params=_params(),
    )(h1, dh2, u, o, mods, w1, w2)


def _div_tile(n, cap):
    if n <= cap:
        return n
    return max(t for t in range(LANES, cap + 1, LANES) if n % t == 0)


DW_TOKEN_TILE_CAP = 4224


def _mm_tn(a, b, name, *, relu2=False, col_blocks=1, after=None):
    rows, m = a.shape
    n = b.shape[1]
    tmm = min(m, 1024)
    tn = min(n // col_blocks, 2048)
    per_block = n // col_blocks // tn
    tr = _div_tile(rows, DW_TOKEN_TILE_CAP)
    tokens = [] if after is None else [after]

    def body(a_ref, b_ref, *rest):
        o_ref, acc_ref = rest[len(tokens):]
        r = pl.program_id(2)

        @pl.when(r == 0)
        def _():
            acc_ref[...] = jnp.zeros(acc_ref.shape, F32)

        av = a_ref[...]
        if relu2:
            av = jnp.square(jnp.maximum(av.astype(F32), 0.0)).astype(BF16)
        acc_ref[...] += _dot_tn(av, b_ref[...])

        @pl.when(r == pl.num_programs(2) - 1)
        def _():
            o_ref[...] = acc_ref[...].astype(BF16)

    return pl.pallas_call(
        body, name=name, grid=(m // tmm, n // tn, rows // tr),
        in_specs=[pl.BlockSpec((tr, tmm), lambda i, j, r: (r, i)), pl.BlockSpec((tr, tn), lambda i, j, r: (r, j))]
        + [pl.BlockSpec((8, LANES), lambda i, j, r: (0, 0))] * len(tokens),
        out_specs=pl.BlockSpec((None, tmm, tn), lambda i, j, r: (j // per_block, i, j % per_block)),
        out_shape=jax.ShapeDtypeStruct((col_blocks, m, n // col_blocks), BF16),
        scratch_shapes=[pltpu.VMEM((tmm, tn), F32)],
        compiler_params=_params(),
    )(a, b, *tokens)


def _halo_specs(tm, d, rows):
    per = tm // HALO
    prev = pl.BlockSpec((HALO, d), lambda i: (jnp.maximum(i * per - 1, 0), 0))
    nxt = pl.BlockSpec((HALO, d), lambda i: (jnp.minimum((i + 1) * per, rows // HALO - 1), 0))
    return prev, _rows(tm, d), nxt


def _segment_positions(i, tm, nct, groups, seg_lens):
    if groups == 1:
        start, length = 0, seg_lens[-1]
    else:
        start = jnp.where(i >= nct, nct, 0)
        length = jnp.where(i >= nct, seg_lens[1], seg_lens[0])
    rid = lax.broadcasted_iota(jnp.int32, (tm + 2 * HALO, 1), 0)
    pos = (i - start) * tm - HALO + rid
    return pos, length


def _window_count(pos, length, w):
    hi = jnp.minimum(pos + (w - w // 2), length)
    lo = jnp.maximum(pos - w // 2, 0)
    return (hi - lo).astype(F32)


def _window_sum(xg, w, lead):
    b, k = xg, 1
    while k < w:
        b = b + _shift_up(b, k)
        k *= 2
    return _shift_up(b, HALO - lead)[0:xg.shape[0] - 2 * HALO]


def _pooled(ext, md, pos, length, gw):
    tm = ext.shape[0] - 2 * HALO
    a_ext, parts = _normmod(ext, md, 0)
    valid = jnp.logical_and(pos >= 0, pos < length)
    a_ext = jnp.where(valid, a_ext, 0.0)
    pos_c = pos[HALO:HALO + tm]
    ps = []
    for g, w in enumerate(POOL_WINDOWS):
        xg = a_ext[:, g * gw:(g + 1) * gw]
        s = _window_sum(xg, w, w // 2)
        ps.append(s * (1.0 / _window_count(pos_c, length, w)) - xg[HALO:HALO + tm])
    return ps, parts


def _pool_fwd(h, mods, pw, pscale, layer, *, nct, tm, seg_lens):
    rows, d = h.shape
    groups = mods.shape[0]
    pg, gw = pw.shape[1], pw.shape[-1]

    def body(prev_ref, cur_ref, next_ref, md_ref, pw_ref, ps_ref, out_ref, p_ref):
        i = pl.program_id(0)
        md = md_ref[...]
        cur = cur_ref[...]
        ext = jnp.concatenate([prev_ref[...], cur, next_ref[...]], axis=0)
        pos, length = _segment_positions(i, tm, nct, groups, seg_lens)
        ps, _ = _pooled(ext, md, pos, length, gw)
        for g in range(pg):
            pb = ps[g].astype(BF16)
            p_ref[:, g * gw:(g + 1) * gw] = pb
            yg = _dot(pb, pw_ref[g]) * ps_ref[:, g * gw:(g + 1) * gw]
            out_ref[:, g * gw:(g + 1) * gw] = cur[:, g * gw:(g + 1) * gw] + md[G1:G1 + 1, g * gw:(g + 1) * gw] * yg

    j = layer // N_MIXERS
    return pl.pallas_call(
        body, name=f"pool_fwd_{layer}", grid=(rows // tm,),
        in_specs=[*_halo_specs(tm, d, rows), _mods_spec(nct, groups, d),
                  pl.BlockSpec((None, pg, gw, gw), lambda i: (j, 0, 0, 0)), _full((1, d))],
        out_specs=[_rows(tm, d), _rows(tm, d)],
        out_shape=[jax.ShapeDtypeStruct((rows, d), F32), jax.ShapeDtypeStruct((rows, d), BF16)],
        compiler_params=_params(),
    )(h, h, h, mods, pw, pscale[j:j + 1])


def _pool_bwd_weights(p, dh1, mods, pw, pscale, layer, *, nct, tm):
    rows, d = p.shape
    groups = mods.shape[0]
    pg, gw = pw.shape[1], pw.shape[-1]

    def body(p_ref, g_ref, md_ref, pw_ref, ps_ref, dp_ref, dmd_ref, dps_ref, dpw_ref):
        i = pl.program_id(0)
        md = md_ref[...]
        gup = g_ref[...]

        @pl.when(i == 0)
        def _():
            dps_ref[...] = jnp.zeros(dps_ref.shape, F32)
            dpw_ref[...] = jnp.zeros(dpw_ref.shape, F32)

        dg1 = []
        for g in range(pg):
            cols = slice(g * gw, (g + 1) * gw)
            pb = p_ref[:, cols]
            yp = _dot(pb, pw_ref[g])
            sc = ps_ref[:, cols]
            dg1.append(_rowsum(gup[:, cols] * (yp * sc)))
            dy = gup[:, cols] * md[G1:G1 + 1, cols]
            dps_ref[0:1, cols] += _rowsum(dy * yp)
            dyp = (dy * sc).astype(BF16)
            dp_ref[:, cols] = _dot_nt(dyp, pw_ref[g])
            dpw_ref[g] += _dot_tn(pb, dyp)
        _acc_rows(dmd_ref, _first_of_group(i, nct, groups), {G1: jnp.concatenate(dg1, axis=1)})

    j = layer // N_MIXERS
    return pl.pallas_call(
        body, name=f"pool_bwd_w_{layer}", grid=(rows // tm,),
        in_specs=[_rows(tm, d), _rows(tm, d), _mods_spec(nct, groups, d),
                  pl.BlockSpec((None, pg, gw, gw), lambda i: (j, 0, 0, 0)), _full((1, d))],
        out_specs=[_rows(tm, d), _mods_spec(nct, groups, d), _full((8, d)), _full((pg, gw, gw))],
        out_shape=[jax.ShapeDtypeStruct((rows, d), F32), jax.ShapeDtypeStruct((groups, 8, d), F32),
                   jax.ShapeDtypeStruct((8, d), F32), jax.ShapeDtypeStruct((pg, gw, gw), F32)],
        compiler_params=_params(),
    )(p, dh1, mods, pw, pscale[j:j + 1])


def _pool_bwd_input(dp, h, dh1, mods, layer, *, nct, tm, seg_lens, gw):
    rows, d = h.shape
    groups = mods.shape[0]

    def body(prev_ref, cur_ref, next_ref, h_ref, g_ref, md_ref, dh_ref, dmd_ref):
        i = pl.program_id(0)
        md = md_ref[...]
        dp_cur = cur_ref[...]
        ext = jnp.concatenate([prev_ref[...], dp_cur, next_ref[...]], axis=0)
        pos, length = _segment_positions(i, tm, nct, groups, seg_lens)
        valid = jnp.logical_and(pos >= 0, pos < length)
        das = []
        for g, w in enumerate(POOL_WINDOWS):
            cols = slice(g * gw, (g + 1) * gw)
            q = jnp.where(valid, ext[:, cols] * (1.0 / jnp.maximum(_window_count(pos, length, w), 1.0)), 0.0)
            das.append(_window_sum(q, w, w // 2 - 1) - dp_cur[:, cols])
        da = jnp.concatenate(das, axis=1)
        _, parts = _normmod(h_ref[...], md, 0)
        dx, dsh, dsc, dng = _normmod_bwd(da, parts, md, 0)
        dh_ref[...] = g_ref[...] + dx
        _acc_rows(dmd_ref, _first_of_group(i, nct, groups), {SH1: dsh, SC1: dsc, NG0: dng})

    return pl.pallas_call(
        body, name=f"pool_bwd_x_{layer}", grid=(rows // tm,),
        in_specs=[*_halo_specs(tm, d, rows), _rows(tm, d), _rows(tm, d), _mods_spec(nct, groups, d)],
        out_specs=[pl.BlockSpec((tm, d), lambda i: (jnp.maximum(i - nct, 0), 0)), _mods_spec(nct, groups, d)],
        out_shape=[jax.ShapeDtypeStruct((rows - nct * tm, d), F32), jax.ShapeDtypeStruct((groups, 8, d), F32)],
        compiler_params=_params(),
    )(dp, dp, dp, h, dh1, mods)


def _rope_tables(n_ctx, seq, hd):
    half = hd // 2
    n_rows = seq // GRID_W
    inv = np.float32(ROPE_BASE) ** (-np.arange(0, half, 2, dtype=np.float32) / np.float32(half))
    ar = np.arange(n_rows, dtype=np.float32)[:, None] * inv[None, :]
    ac = np.arange(GRID_W, dtype=np.float32)[:, None] * inv[None, :]

    def over_tokens(row_part, col_part):
        return np.repeat(row_part, GRID_W, axis=0), np.tile(col_part, (n_rows, 1))

    cr, cc = over_tokens(np.cos(ar), np.cos(ac))
    sr, sc = over_tokens(np.sin(ar), np.sin(ac))
    cos = np.concatenate([cr, cr, cc, cc], axis=1)
    sin = np.concatenate([-sr, sr, -sc, sc], axis=1)
    cos = np.concatenate([np.ones((n_ctx, hd), np.float32), cos], axis=0)
    sin = np.concatenate([np.zeros((n_ctx, hd), np.float32), sin], axis=0)
    return jnp.asarray(cos, F32), jnp.asarray(sin, F32)


def _rope_partner(x):
    hd = x.shape[-1]
    q = hd // 4
    lane = lax.broadcasted_iota(jnp.int32, x.shape, 1)
    first = (lane % (2 * q)) < q
    return jnp.where(first, pltpu.roll(x, hd - q, axis=1), pltpu.roll(x, q, axis=1))


def _qkv_fwd(h, mods, wqkv, cos, sin, gains, *, nh, nkv, nct, tm):
    rows, d = h.shape
    qw = wqkv.shape[0] * wqkv.shape[-1]
    hd = cos.shape[-1]

    def body(h_ref, md_ref, w_ref, cos_ref, sin_ref, gn_ref, xa_ref, qkv_ref, q_ref, k_ref, v_ref):
        a, _ = _normmod(h_ref[...], md_ref[...], 0)
        xa = a.astype(BF16)
        xa_ref[...] = xa
        qkv = _dot_blocks(xa, w_ref)
        qkv_ref[...] = qkv
        c, s = cos_ref[...], sin_ref[...]
        for hh in range(nh + nkv):
            xh = qkv[:, hh * hd:(hh + 1) * hd]
            xhat, _ = _rms_parts(xh)
            y = xhat * (gn_ref[0:1, :] if hh < nh else gn_ref[1:2, :])
            rot = (y * c + _rope_partner(y) * s).astype(BF16)
            if hh < nh:
                q_ref[:, hh * hd:(hh + 1) * hd] = rot
            else:
                k_ref[:, (hh - nh) * hd:(hh - nh + 1) * hd] = rot
        v_ref[...] = qkv[:, (nh + nkv) * hd:].astype(BF16)

    return pl.pallas_call(
        body, name="attn_qkv_fwd", grid=(rows // tm,),
        in_specs=[_rows(tm, d), _mods_spec(nct, 2, d), _full(wqkv.shape), _rows(tm, hd), _rows(tm, hd),
                  _full((8, hd))],
        out_specs=[_rows(tm, d), _rows(tm, qw), pl.BlockSpec((tm, nh * hd), lambda i: (jnp.maximum(i - nct, 0), 0)),
                   _rows(tm, nkv * hd), _rows(tm, nkv * hd)],
        out_shape=[jax.ShapeDtypeStruct((rows, d), BF16), jax.ShapeDtypeStruct((rows, qw), F32),
                   jax.ShapeDtypeStruct((rows - nct * tm, nh * hd), BF16),
                   jax.ShapeDtypeStruct((rows, nkv * hd), BF16), jax.ShapeDtypeStruct((rows, nkv * hd), BF16)],
        compiler_params=_params(),
    )(h, mods, wqkv, cos, sin, gains)


ATTN_Q_TILE_CAP = 1024
ATTN_KV_TILE_CAP = 4224
ATTN_ROW_GROUP = 256
LOG2E = 1.4426950408889634


def _attn_tiles(seq, total):
    tq = _div_tile(seq, ATTN_Q_TILE_CAP)
    return tq, _div_tile(total, ATTN_KV_TILE_CAP), min(ATTN_ROW_GROUP, tq)


def _flash_fwd(q, k, v, *, n_ctx, hd):
    total = k.shape[0]
    seq = total - n_ctx
    nkv = k.shape[1] // hd
    tq, tk, rg = _attn_tiles(seq, total)
    nk = total // tk
    scale = hd ** -0.5
    c2 = scale * LOG2E

    def body(q_ref, k_ref, v_ref, o_ref, lse_ref, m_sc, l_sc, acc_sc):
        ki = pl.program_id(2)

        @pl.when(ki == 0)
        def _():
            m_sc[...] = jnp.full(m_sc.shape, -jnp.inf, F32)
            l_sc[...] = jnp.zeros(l_sc.shape, F32)
            acc_sc[...] = jnp.zeros(acc_sc.shape, F32)

        kk, vv = k_ref[...], v_ref[...]
        groups = [(g, sub) for g in range(2) for sub in range(tq // rg)]

        def scores(g, sub):
            return _dot_nt(q_ref[sub * rg:(sub + 1) * rg, g * hd:(g + 1) * hd], kk)

        s_next = scores(*groups[0])
        for idx, (g, sub) in enumerate(groups):
            s = s_next
            if idx + 1 < len(groups):
                s_next = scores(*groups[idx + 1])
            rows = slice(g * tq + sub * rg, g * tq + (sub + 1) * rg)
            m_old = m_sc[rows]
            m_new = jnp.maximum(m_old, jnp.max(s, axis=-1, keepdims=True))
            alpha = jnp.exp2((m_old - m_new) * c2)
            p = jnp.exp2((s - m_new) * c2)
            l_sc[rows] = alpha * l_sc[rows] + jnp.sum(p, axis=-1, keepdims=True)
            acc_sc[rows] = alpha * acc_sc[rows] + _dot(p.astype(BF16), vv)
            m_sc[rows] = m_new

        @pl.when(ki == nk - 1)
        def _():
            o2 = acc_sc[...] / l_sc[...]
            lse = m_sc[...] * scale + jnp.log(l_sc[...])
            o_ref[:, :hd] = o2[:tq].astype(BF16)
            o_ref[:, hd:] = o2[tq:].astype(BF16)
            lse_ref[:, 0:1] = lse[:tq]
            lse_ref[:, 1:2] = lse[tq:]

    return pl.pallas_call(
        body, name="attn_flash_fwd", grid=(nkv, seq // tq, nk),
        in_specs=[pl.BlockSpec((tq, 2 * hd), lambda h, i, j: (i, h)),
                  pl.BlockSpec((tk, hd), lambda h, i, j: (j, h)),
                  pl.BlockSpec((tk, hd), lambda h, i, j: (j, h))],
        out_specs=[pl.BlockSpec((tq, 2 * hd), lambda h, i, j: (i, h)),
                   pl.BlockSpec((None, tq, 2), lambda h, i, j: (h, i, 0))],
        out_shape=[jax.ShapeDtypeStruct((seq, 2 * nkv * hd), BF16), jax.ShapeDtypeStruct((nkv, seq, 2), F32)],
        scratch_shapes=[pltpu.VMEM((2 * tq, 1), F32), pltpu.VMEM((2 * tq, 1), F32), pltpu.VMEM((2 * tq, hd), F32)],
        compiler_params=_params(),
    )(q, k, v)


def _flash_bwd(q, k, v, o, do, lse, *, n_ctx, hd):
    total = k.shape[0]
    seq = total - n_ctx
    nkv = k.shape[1] // hd
    tq, tk, rg = _attn_tiles(seq, total)
    scale = hd ** -0.5
    c2 = scale * LOG2E

    def body(q_ref, k_ref, v_ref, o_ref, do_ref, lse_ref, dq_ref, dk_ref, dv_ref):
        ki, qi = pl.program_id(1), pl.program_id(2)
        kk, vv = k_ref[...], v_ref[...]

        @pl.when(qi == 0)
        def _():
            dk_ref[...] = jnp.zeros(dk_ref.shape, F32)
            dv_ref[...] = jnp.zeros(dv_ref.shape, F32)

        dk_acc = jnp.zeros((tk, hd), F32)
        dv_acc = jnp.zeros((tk, hd), F32)
        for g in range(2):
            for sub in range(tq // rg):
                rs = slice(sub * rg, (sub + 1) * rg)
                cs = slice(g * hd, (g + 1) * hd)
                qq = q_ref[rs, cs]
                dd = do_ref[rs, cs]
                delta = jnp.sum(dd.astype(F32) * o_ref[rs, cs].astype(F32), axis=-1, keepdims=True)
                p = jnp.exp2(_dot_nt(qq, kk) * c2 - lse_ref[rs, g:g + 1] * LOG2E)
                dp = _dot_nt(dd, vv)
                ds = (p * (dp - delta) * scale).astype(BF16)
                dv_acc = dv_acc + _dot_tn(p.astype(BF16), dd)
                dk_acc = dk_acc + _dot_tn(ds, qq)
                dq = _dot(ds, kk)
                rows = pl.ds(pl.multiple_of(qi * tq, tq) + sub * rg, rg)

                @pl.when(ki == 0)
                def _():
                    dq_ref[rows, cs] = dq

                @pl.when(ki > 0)
                def _():
                    dq_ref[rows, cs] += dq
        dk_ref[...] += dk_acc
        dv_ref[...] += dv_acc

    return pl.pallas_call(
        body, name="attn_flash_bwd", grid=(nkv, total // tk, seq // tq),
        in_specs=[pl.BlockSpec((tq, 2 * hd), lambda h, j, i: (i, h)),
                  pl.BlockSpec((tk, hd), lambda h, j, i: (j, h)),
                  pl.BlockSpec((tk, hd), lambda h, j, i: (j, h)),
                  pl.BlockSpec((tq, 2 * hd), lambda h, j, i: (i, h)),
                  pl.BlockSpec((tq, 2 * hd), lambda h, j, i: (i, h)),
                  pl.BlockSpec((None, tq, 2), lambda h, j, i: (h, i, 0))],
        out_specs=[pl.BlockSpec((seq, 2 * hd), lambda h, j, i: (0, h)),
                   pl.BlockSpec((tk, hd), lambda h, j, i: (j, h)),
                   pl.BlockSpec((tk, hd), lambda h, j, i: (j, h))],
        out_shape=[jax.ShapeDtypeStruct((seq, 2 * nkv * hd), F32), jax.ShapeDtypeStruct((total, nkv * hd), F32),
                   jax.ShapeDtypeStruct((total, nkv * hd), F32)],
        compiler_params=_params(),
    )(q, k, v, o, do, lse)


def _proj_fwd(o, wo, hc, mods, *, n_ctx, tm):
    seq, d = o.shape
    off = n_ctx // tm

    def body(o_ref, w_ref, h_ref, md_ref, h1_ref, y_ref):
        y = _dot(o_ref[...], w_ref[...])
        y_ref[...] = y.astype(BF16)
        h1_ref[...] = h_ref[...] + md_ref[G1:G1 + 1, :] * y

    return pl.pallas_call(
        body, name="attn_proj_fwd", grid=(seq // tm,),
        in_specs=[_rows(tm, d), _full((d, d)),
                  pl.BlockSpec((tm, d), lambda i: (i + off, 0)), pl.BlockSpec((None, 8, d), lambda i: (1, 0, 0))],
        out_specs=[_rows(tm, d), _rows(tm, d)],
        out_shape=[jax.ShapeDtypeStruct((seq, d), F32), jax.ShapeDtypeStruct((seq, d), BF16)],
        compiler_params=_params(),
    )(o, wo, hc, mods)


def _proj_bwd(dh1, y, mods, wo, *, tm):
    seq, d = dh1.shape

    def body(g_ref, y_ref, md_ref, w_ref, do_ref, dyb_ref, dmd_ref):
        i = pl.program_id(0)
        g = g_ref[...]
        dyb = (g * md_ref[G1:G1 + 1, :]).astype(BF16)
        dyb_ref[...] = dyb
        do_ref[...] = _dot_nt(dyb, w_ref[...]).astype(BF16)
        _acc_rows(dmd_ref, i == 0, {G1: _rowsum(g * y_ref[...].astype(F32))})

    return pl.pallas_call(
        body, name="attn_proj_bwd", grid=(seq // tm,),
        in_specs=[_rows(tm, d), _rows(tm, d), pl.BlockSpec((None, 8, d), lambda i: (1, 0, 0)), _full((d, d))],
        out_specs=[_rows(tm, d), _rows(tm, d), pl.BlockSpec((None, 8, d), lambda i: (0, 0, 0))],
        out_shape=[jax.ShapeDtypeStruct((seq, d), BF16), jax.ShapeDtypeStruct((seq, d), BF16),
                   jax.ShapeDtypeStruct((1, 8, d), F32)],
        compiler_params=_params(),
    )(dh1, y, mods, wo)


def _qkv_bwd(qkv, dq, dk, dv, cos, sin, gains, *, nh, nkv, nct, tm):
    rows, qw = qkv.shape
    hd = cos.shape[-1]

    def body(qkv_ref, dq_ref, dk_ref, dv_ref, cos_ref, sin_ref, gn_ref, out_ref, dgn_ref):
        i = pl.program_id(0)
        c, s = cos_ref[...], sin_ref[...]
        is_lat = (i >= nct).astype(F32)
        dqg = jnp.zeros((1, hd), F32)
        dkg = jnp.zeros((1, hd), F32)
        for hh in range(nh + nkv):
            if hh < nh:
                dr = dq_ref[:, hh * hd:(hh + 1) * hd] * is_lat
                gn = gn_ref[0:1, :]
            else:
                dr = dk_ref[:, (hh - nh) * hd:(hh - nh + 1) * hd]
                gn = gn_ref[1:2, :]
            dy = dr * c + _rope_partner(dr * s)
            xhat, r = _rms_parts(qkv_ref[:, hh * hd:(hh + 1) * hd])
            dgh = _rowsum(dy * xhat)
            if hh < nh:
                dqg = dqg + dgh
            else:
                dkg = dkg + dgh
            dxhat = dy * gn
            dx = r * (dxhat - xhat * jnp.mean(dxhat * xhat, axis=-1, keepdims=True))
            out_ref[:, hh * hd:(hh + 1) * hd] = dx.astype(BF16)
        out_ref[:, (nh + nkv) * hd:] = dv_ref[...].astype(BF16)
        _acc_rows(dgn_ref, i == 0, {0: dqg, 1: dkg})

    return pl.pallas_call(
        body, name="attn_qkv_bwd", grid=(rows // tm,),
        in_specs=[_rows(tm, qw), pl.BlockSpec((tm, nh * hd), lambda i: (jnp.maximum(i - nct, 0), 0)),
                  _rows(tm, nkv * hd), _rows(tm, nkv * hd), _rows(tm, hd), _rows(tm, hd), _full((8, hd))],
        out_specs=[_rows(tm, qw), _full((8, hd))],
        out_shape=[jax.ShapeDtypeStruct((rows, qw), BF16), jax.ShapeDtypeStruct((8, hd), F32)],
        compiler_params=_params(),
    )(qkv, dq, dk, dv, cos, sin, gains)


def _attn_in_bwd(dqkv, wqkv, hc, dh1, mods, *, nct, tm):
    rows, d = hc.shape
    qw = dqkv.shape[1]

    def body(dz_ref, w_ref, h_ref, g_ref, md_ref, dh_ref, dmd_ref):
        i = pl.program_id(0)
        md = md_ref[...]
        da = _dot_nt_blocks(dz_ref[...], w_ref)
        _, parts = _normmod(h_ref[...], md, 0)
        dx, dsh, dsc, dng = _normmod_bwd(da, parts, md, 0)
        dh_ref[...] = g_ref[...] * (i >= nct).astype(F32) + dx
        _acc_rows(dmd_ref, _first_of_group(i, nct, 2), {SH1: dsh, SC1: dsc, NG0: dng})

    return pl.pallas_call(
        body, name="attn_in_bwd", grid=(rows // tm,),
        in_specs=[_rows(tm, qw), _full(wqkv.shape), _rows(tm, d),
                  pl.BlockSpec((tm, d), lambda i: (jnp.maximum(i - nct, 0), 0)), _mods_spec(nct, 2, d)],
        out_specs=[_rows(tm, d), _mods_spec(nct, 2, d)],
        out_shape=[jax.ShapeDtypeStruct((rows, d), F32), jax.ShapeDtypeStruct((2, 8, d), F32)],
        compiler_params=_params(),
    )(dqkv, wqkv, hc, dh1, mods)


def _gmlp_gate(z, lng, lnb, ws_ref, bs_ref, gg, ch):
    half = z.shape[1] // 2
    ggw = half // gg
    u, v = z[:, :half], z[:, half:]
    vc = v - jnp.mean(v, axis=-1, keepdims=True)
    rs = lax.rsqrt(jnp.mean(vc * vc, axis=-1, keepdims=True) + EPS)
    vhat = vc * rs
    vln = (vhat * lng + lnb).astype(BF16)
    chunks = []
    for n in range(z.shape[0] // ch):
        groups = []
        for g in range(gg):
            groups.append(_dot(ws_ref[g], vln[n * ch:(n + 1) * ch, g * ggw:(g + 1) * ggw]) + bs_ref[g])
        chunks.append(jnp.concatenate(groups, axis=1))
    sv = jnp.concatenate(chunks, axis=0) if len(chunks) > 1 else chunks[0]
    return u, sv, vhat, rs, vln


def _gmlp_fwd(h, mods, w_in, lng, lnb, ws, bs, w_out, *, tm):
    seq, d = h.shape
    zw = w_in.shape[0] * w_in.shape[-1]
    half = zw // 2
    gg, ch = ws.shape[0], ws.shape[-1]

    def body(h_ref, md_ref, win_ref, lng_ref, lnb_ref, ws_ref, bs_ref, wout_ref, h1_ref, zp_ref, y_ref):
        x = h_ref[...]
        md = md_ref[...]
        a, _ = _normmod(x, md, 0)
        zp = _dot_blocks(a.astype(BF16), win_ref)
        zp_ref[...] = zp.astype(BF16)
        u, sv, _, _, _ = _gmlp_gate(_gelu(zp), lng_ref[...], lnb_ref[...], ws_ref, bs_ref, gg, ch)
        y = _dot((u * sv).astype(BF16), wout_ref[...])
        y_ref[...] = y.astype(BF16)
        h1_ref[...] = x + md[G1:G1 + 1] * y

    return pl.pallas_call(
        body, name="gmlp_fwd", grid=(seq // tm,),
        in_specs=[_rows(tm, d), pl.BlockSpec((None, 8, d), lambda i: (1, 0, 0)),
                  _full(w_in.shape), _full((1, half)), _full((1, half)),
                  _full((gg, ch, ch)), _full((gg, ch, 1)), _full((half, d))],
        out_specs=[_rows(tm, d), _rows(tm, zw), _rows(tm, d)],
        out_shape=[jax.ShapeDtypeStruct((seq, d), F32), jax.ShapeDtypeStruct((seq, zw), BF16),
                   jax.ShapeDtypeStruct((seq, d), BF16)],
        compiler_params=_params(),
    )(h, mods, w_in, lng, lnb, ws, bs, w_out)


def _gmlp_bwd(h, dh1, zpre, y, mods, w_in, lng, lnb, ws, ws_t, bs, w_out, *, tm):
    seq, d = h.shape
    zw = w_in.shape[0] * w_in.shape[-1]
    half = zw // 2
    gg, ch = ws.shape[0], ws.shape[-1]
    ggw = half // gg

    def body(h_ref, g_ref, zp_ref, y_ref, md_ref, win_ref, lng_ref, lnb_ref, ws_ref, wst_ref, bs_ref, wout_ref,
             dh_ref, dzp_ref, gated_ref, dyb_ref, ab_ref, dmd_ref, dln_ref, dws_ref, dbs_ref):
        i = pl.program_id(0)
        x = h_ref[...]
        md = md_ref[...]
        a, parts = _normmod(x, md, 0)
        ab_ref[...] = a.astype(BF16)
        zp = zp_ref[...].astype(F32)
        lng_v = lng_ref[...]
        u, sv, vhat, rs, vln = _gmlp_gate(_gelu(zp), lng_v, lnb_ref[...], ws_ref, bs_ref, gg, ch)
        g = g_ref[...]
        dg1 = _rowsum(g * y_ref[...].astype(F32))
        dyb = (g * md[G1:G1 + 1]).astype(BF16)
        dyb_ref[...] = dyb
        gated_ref[...] = (u * sv).astype(BF16)
        dgated = _dot_nt(dyb, wout_ref[...])
        du = dgated * sv
        dsv = dgated * u

        @pl.when(i == 0)
        def _():
            dws_ref[...] = jnp.zeros(dws_ref.shape, F32)
            dbs_ref[...] = jnp.zeros(dbs_ref.shape, F32)
            dln_ref[...] = jnp.zeros(dln_ref.shape, F32)

        chunks = []
        for n in range(tm // ch):
            groups = []
            for gi in range(gg):
                blk = dsv[n * ch:(n + 1) * ch, gi * ggw:(gi + 1) * ggw]
                dbs_ref[gi] += jnp.sum(blk, axis=-1, keepdims=True)
                blk_b = blk.astype(BF16)
                dws_ref[gi] += _dot_nt(blk_b, vln[n * ch:(n + 1) * ch, gi * ggw:(gi + 1) * ggw])
                groups.append(_dot(wst_ref[gi], blk_b))
            chunks.append(jnp.concatenate(groups, axis=1))
        dvln = jnp.concatenate(chunks, axis=0) if len(chunks) > 1 else chunks[0]
        dln_ref[0:1, :] += _rowsum(dvln * vhat)
        dln_ref[1:2, :] += _rowsum(dvln)
        dvhat = dvln * lng_v
        dv = rs * (dvhat - jnp.mean(dvhat, axis=-1, keepdims=True)
                   - vhat * jnp.mean(dvhat * vhat, axis=-1, keepdims=True))
        dzp = (jnp.concatenate([du, dv], axis=1) * _gelu_grad(zp)).astype(BF16)
        dzp_ref[...] = dzp
        da = _dot_nt_blocks(dzp, win_ref)
        dx, dsh, dsc, dng = _normmod_bwd(da, parts, md, 0)
        dh_ref[...] = g + dx
        _acc_rows(dmd_ref, i == 0, {SH1: dsh, SC1: dsc, G1: dg1, NG0: dng})

    return pl.pallas_call(
        body, name="gmlp_bwd", grid=(seq // tm,),
        in_specs=[_rows(tm, d), _rows(tm, d), _rows(tm, zw), _rows(tm, d),
                  pl.BlockSpec((None, 8, d), lambda i: (1, 0, 0)),
                  _full(w_in.shape), _full((1, half)), _full((1, half)),
                  _full((gg, ch, ch)), _full((gg, ch, ch)), _full((gg, ch, 1)), _full((half, d))],
        out_specs=[_rows(tm, d), _rows(tm, zw), _rows(tm, half), _rows(tm, d), _rows(tm, d),
                   pl.BlockSpec((None, 8, d), lambda i: (0, 0, 0)), _full((8, half)), _full((gg, ch, ch)),
                   _full((gg, ch, 1))],
        out_shape=[jax.ShapeDtypeStruct((seq, d), F32), jax.ShapeDtypeStruct((seq, zw), BF16),
                   jax.ShapeDtypeStruct((seq, half), BF16), jax.ShapeDtypeStruct((seq, d), BF16),
                   jax.ShapeDtypeStruct((seq, d), BF16), jax.ShapeDtypeStruct((1, 8, d), F32),
                   jax.ShapeDtypeStruct((8, half), F32), jax.ShapeDtypeStruct((gg, ch, ch), F32),
                   jax.ShapeDtypeStruct((gg, ch, 1), F32)],
        compiler_params=_params(),
    )(h, dh1, zpre, y, mods, w_in, lng, lnb, ws, ws_t, bs, w_out)


def _final_loss(h, tgt, fg, *, tm):
    seq, d = h.shape

    def body(h_ref, t_ref, g_ref, dh_ref, acc_ref):
        i = pl.program_id(0)
        gain = g_ref[...]
        xhat, r = _rms_parts(h_ref[...])
        err = xhat * gain - t_ref[...]
        dy = err * (1.0 / d)
        dxhat = dy * gain
        dh_ref[...] = r * (dxhat - xhat * jnp.mean(dxhat * xhat, axis=-1, keepdims=True))
        part = jnp.sum(_rowsum(err * err), axis=-1, keepdims=True) * (0.5 / d)
        _acc_rows(acc_ref, i == 0, {0: _rowsum(dy * xhat), 1: jnp.broadcast_to(part, (1, d))})

    return pl.pallas_call(
        body, name="final_loss", grid=(seq // tm,),
        in_specs=[_rows(tm, d), _rows(tm, d), _full((1, d))],
        out_specs=[_rows(tm, d), _full((8, d))],
        out_shape=[jax.ShapeDtypeStruct((seq, d), F32), jax.ShapeDtypeStruct((8, d), F32)],
        compiler_params=_params(),
    )(h, tgt, fg)


def _ada_fwd(c_all, ada_w, ada_b_cols):
    depth, d, ncs = ada_w.shape

    def body(c_ref, w_ref, b_ref, o_ref):
        s = _silu(c_ref[...]).astype(BF16)
        o_ref[...] = _dot(s, w_ref[...].astype(BF16)) + b_ref[...]

    return pl.pallas_call(
        body, name="ada_fwd", grid=(depth,),
        in_specs=[_full((16, d)), pl.BlockSpec((None, d, ncs), lambda i: (i, 0, 0)),
                  pl.BlockSpec((None, 1, ncs), lambda i: (i, 0, 0))],
        out_specs=pl.BlockSpec((None, 16, ncs), lambda i: (i, 0, 0)),
        out_shape=jax.ShapeDtypeStruct((depth, 16, ncs), F32),
        compiler_params=_params(),
    )(c_all, ada_w, ada_b_cols.reshape(depth, 1, ncs))


def _ada_bwd(c_all, c_all_t, dmod, ada_w):
    depth, d, ncs = ada_w.shape

    def body(c_ref, ct_ref, dm_ref, w_ref, gw_ref, dc_ref):
        i = pl.program_id(0)
        dm = dm_ref[...]
        dctx = _rowsum(dm[8:16])
        rid = lax.broadcasted_iota(jnp.int32, (8, ncs), 0)
        low = jnp.where(rid == 0, jnp.broadcast_to(dctx, (8, ncs)), 0.0)
        dm16 = jnp.concatenate([dm[0:8], low], axis=0).astype(BF16)
        gw_ref[...] = _dot(_silu(ct_ref[...]).astype(BF16), dm16)

        @pl.when(i == 0)
        def _():
            dc_ref[...] = jnp.zeros(dc_ref.shape, F32)

        dc_ref[...] += _dot_nt(low.astype(BF16), w_ref[...].astype(BF16)) * _silu_grad(c_ref[8:9, :])

    return pl.pallas_call(
        body, name="ada_bwd", grid=(depth,),
        in_specs=[_full((16, d)), _full((d, 16)), pl.BlockSpec((None, 16, ncs), lambda i: (i, 0, 0)),
                  pl.BlockSpec((None, d, ncs), lambda i: (i, 0, 0))],
        out_specs=[pl.BlockSpec((None, d, ncs), lambda i: (i, 0, 0)), _full((8, d))],
        out_shape=[jax.ShapeDtypeStruct((depth, d, ncs), F32), jax.ShapeDtypeStruct((8, d), F32)],
        compiler_params=_params(),
    )(c_all, c_all_t, dmod, ada_w)


def _adamw_math(w, g, m, v):
    m = ADAM_B1 * m + (1.0 - ADAM_B1) * g
    v = ADAM_B2 * v + (1.0 - ADAM_B2) * jnp.square(g)
    m_hat = m * (1.0 / (1.0 - ADAM_B1 ** ADAM_STEP))
    v_hat = v * (1.0 / (1.0 - ADAM_B2 ** ADAM_STEP))
    delta = -ADAM_LR * (m_hat / (jnp.sqrt(v_hat) + ADAM_EPS) + ADAM_WD * w)
    return delta, m, v


def _adamw(ga, gb, w, m, v, name):
    rows, cols = w.shape
    tr = rows
    while tr * cols * 4 > (1 << 20) and tr % 16 == 0:
        tr //= 2
    grads = [ga] if gb is None else [ga, gb]

    def body(*refs):
        w_ref, m_ref, v_ref, g_out, d_out, m_out, v_out = refs[len(grads):]
        g = refs[0][...] if gb is None else refs[0][...] + refs[1][...]
        delta, m_new, v_new = _adamw_math(w_ref[...], g, m_ref[...], v_ref[...])
        g_out[...] = g
        d_out[...] = delta
        m_out[...] = m_new
        v_out[...] = v_new

    spec = _rows(tr, cols)
    return pl.pallas_call(
        body, name=name, grid=(rows // tr,),
        in_specs=[spec] * (len(grads) + 3), out_specs=[spec] * 4,
        out_shape=[jax.ShapeDtypeStruct((rows, cols), F32)] * 4,
        compiler_params=_params(),
    )(*grads, w, m, v)


def _sum_devices(gathered, name):
    n, rows, cols = gathered.shape
    tr = rows
    while tr * cols * 4 * n > (4 << 20) and tr % 16 == 0:
        tr //= 2

    def body(x_ref, o_ref):
        acc = x_ref[0]
        for j in range(1, n):
            acc = acc + x_ref[j]
        o_ref[...] = acc

    return pl.pallas_call(
        body, name=name, grid=(rows // tr,),
        in_specs=[pl.BlockSpec((n, tr, cols), lambda i: (0, i, 0))], out_specs=_rows(tr, cols),
        out_shape=jax.ShapeDtypeStruct((rows, cols), F32),
        compiler_params=_params(),
    )(gathered)


def _sum_partials(blocked, landeds, chip, name):
    n = len(blocked)
    cols = blocked[0].shape[-1]
    blocked = [b.reshape(N_CHIPS, -1, cols) for b in blocked]
    landeds = [l.reshape(3, -1, cols) for l in landeds]
    rows = blocked[0].shape[1]
    tr = rows
    while tr * cols * 2 * n > (1 << 20) and tr % 32 == 0:
        tr //= 2

    def body(chip_ref, *refs):
        out_ref = refs[-1]
        for li in range(n):
            acc = refs[li][...].astype(F32)
            for p in range(3):
                acc = acc + refs[n + li][p].astype(F32)
            out_ref[li] = acc

    out = pl.pallas_call(
        body, name=name,
        grid_spec=pltpu.PrefetchScalarGridSpec(
            num_scalar_prefetch=1, grid=(rows // tr,),
            in_specs=[pl.BlockSpec((None, tr, cols), lambda i, k: (k[0], i, 0))] * n
            + [pl.BlockSpec((3, tr, cols), lambda i, k: (0, i, 0))] * n,
            out_specs=pl.BlockSpec((n, tr, cols), lambda i, k: (0, i, 0))),
        out_shape=jax.ShapeDtypeStruct((n, rows, cols), F32),
        compiler_params=_params(),
    )(jnp.reshape(chip, (1,)).astype(jnp.int32), *blocked, *landeds)
    return out.reshape(n * rows, cols)


def _my_place():
    return lax.axis_index("x"), lax.axis_index("y"), lax.axis_index("c")


def _other_chips(x, y):
    return [(1 - x, y), (x, 1 - y), (1 - x, 1 - y)]


def _all_gather_small(block, name):
    rows, cols = block.shape

    def body(x_ref, out_ref, send_sems, recv_sems, local_sem):
        x, y, c = _my_place()
        me, sibling = (x, y, c), (x, y, 1 - c)
        chips = _other_chips(x, y)

        def slot(px, py, pc):
            return out_ref.at[4 * px + 2 * py + pc]

        def copy(k, blk, to, src=None):
            return pltpu.make_async_remote_copy(
                src_ref=slot(*blk) if src is None else src, dst_ref=slot(*blk),
                send_sem=send_sems.at[k], recv_sem=recv_sems.at[k], device_id=to, device_id_type=MESH)

        mine = pltpu.make_async_copy(x_ref, slot(*me), local_sem)
        mine.start()
        first = [copy(0, me, sibling, src=x_ref)]
        first += [copy(1 + j, me, (*chip, c), src=x_ref) for j, chip in enumerate(chips)]
        for cp in first:
            cp.start()
        passed = [copy(4 + j, (*chip, c), sibling) for j, chip in enumerate(chips)]
        for j, chip in enumerate(chips):
            copy(1 + j, (*chip, c), me).wait_recv()
            passed[j].start()
        copy(0, sibling, me).wait_recv()
        for j, chip in enumerate(chips):
            copy(4 + j, (*chip, 1 - c), me).wait_recv()
        for cp in first + passed:
            cp.wait_send()
        mine.wait()

    return pl.pallas_call(
        body, name=name,
        out_shape=jax.ShapeDtypeStruct((N_DEV, rows, cols), block.dtype),
        in_specs=[pl.BlockSpec(memory_space=pltpu.VMEM)],
        out_specs=pl.BlockSpec(memory_space=pltpu.VMEM),
        scratch_shapes=[pltpu.SemaphoreType.DMA((7,)), pltpu.SemaphoreType.DMA((7,)), pltpu.SemaphoreType.DMA],
        compiler_params=_params(),
    )(block)


HBM_SPEC = pl.BlockSpec(memory_space=pltpu.HBM)
SEM_SPEC = pl.BlockSpec(memory_space=pltpu.SEMAPHORE)
DATAFLOW_EFFECT = pltpu.SideEffectType.DATAFLOW_SIDE_EFFECTING


def _same_core_of_other_chips(x, y, c):
    return [(*chip, c) for chip in _other_chips(x, y)]


def _sibling_core(x, y, c):
    return [(x, y, 1 - c)]


def _gather_views(src, land, p, x, y):
    return src, land.at[2 * x + y]


def _scatter_views(src, land, p, x, y):
    peer_chip = (2 * (1 - x) + y, 2 * x + (1 - y), 2 * (1 - x) + (1 - y))[p]
    return src.at[peer_chip], land.at[p]


def _whole_views(src, land, p, x, y):
    return src, land


GATHER_PLAN = (_same_core_of_other_chips, _gather_views, 3)
SCATTER_PLAN = (_same_core_of_other_chips, _scatter_views, 3)
SIBLING_PLAN = (_sibling_core, _whole_views, 1)


def _exchange_copies(srcs, lands, send_sems, recv_sems, plan):
    peers_of, views, n_peers = plan
    x, y, c = _my_place()
    copies = []
    for j, (src, land) in enumerate(zip(srcs, lands)):
        for p, peer in enumerate(peers_of(x, y, c)):
            s_view, d_view = views(src, land, p, x, y)
            k = n_peers * j + p
            copies.append(pltpu.make_async_remote_copy(
                src_ref=s_view, dst_ref=d_view, send_sem=send_sems.at[k], recv_sem=recv_sems.at[k],
                device_id=peer, device_id_type=MESH))
    return copies


def _exchange_start(srcs, lands, plan, name):
    n = len(srcs)

    def body(*refs):
        send_sems, recv_sems = refs[2 * n], refs[2 * n + 1]
        token = refs[-1]
        for cp in _exchange_copies(refs[:n], refs[n:2 * n], send_sems, recv_sems, plan):
            cp.start()
        token[...] = jnp.zeros(token.shape, token.dtype)

    operands = [pltpu.with_memory_space_constraint(a, pltpu.HBM) for a in (*srcs, *lands)]
    out = pl.pallas_call(
        body, name=name,
        out_shape=(pltpu.SemaphoreType.DMA((plan[2] * n,)), pltpu.SemaphoreType.DMA((plan[2] * n,)),
                   *[pltpu.HBM(a.shape, a.dtype) for a in operands], jax.ShapeDtypeStruct((8, LANES), F32)),
        in_specs=[HBM_SPEC] * (2 * n),
        out_specs=(SEM_SPEC, SEM_SPEC, *[HBM_SPEC] * (2 * n), pl.BlockSpec(memory_space=pltpu.VMEM)),
        input_output_aliases={i: 2 + i for i in range(2 * n)},
        compiler_params=pltpu.CompilerParams(has_side_effects=DATAFLOW_EFFECT),
    )(*operands)
    return out[0], out[1], list(out[2:2 + n]), list(out[2 + n:2 + 2 * n]), out[-1]


def _exchange_wait(send_sems, recv_sems, srcs, lands, plan, after, name):
    n = len(srcs)

    def body(*refs):
        send, recv = refs[2 * n], refs[2 * n + 1]
        for cp in _exchange_copies(refs[:n], refs[n:2 * n], send, recv, plan):
            cp.wait_send()
            cp.wait_recv()

    out = pl.pallas_call(
        body, name=name,
        out_shape=tuple(pltpu.HBM(a.shape, a.dtype) for a in (*srcs, *lands)),
        in_specs=[HBM_SPEC] * (2 * n) + [SEM_SPEC, SEM_SPEC, HBM_SPEC],
        out_specs=tuple([HBM_SPEC] * (2 * n)),
        input_output_aliases={i: i for i in range(2 * n)},
        compiler_params=pltpu.CompilerParams(has_side_effects=DATAFLOW_EFFECT),
    )(*srcs, *lands, send_sems, recv_sems, pltpu.with_memory_space_constraint(after, pltpu.HBM))
    return list(out[:n]), list(out[n:])


def _landing_for_gather(shard, chip):
    land = lax.empty((N_CHIPS, *shard.shape), shard.dtype)
    return lax.dynamic_update_index_in_dim(land, shard, chip, 0)


TILE_ELEMS = SUBLANES * LANES


def _pack(arrays):
    parts = []
    for a in arrays:
        flat = a.reshape(-1).astype(F32)
        pad = (-flat.shape[0]) % TILE_ELEMS
        if pad:
            flat = jnp.concatenate([flat, jnp.zeros((pad,), F32)])
        parts.append(flat.reshape(-1, LANES))
    return jnp.concatenate(parts, axis=0) if len(parts) > 1 else parts[0]


def _unpack(buf, shapes):
    out, r = [], 0
    lead = buf.shape[:-2]
    for shp in shapes:
        size = math.prod(shp)
        nr = -(-size // TILE_ELEMS) * SUBLANES
        flat = buf[..., r:r + nr, :].reshape(*lead, nr * LANES)[..., :size]
        out.append(flat.reshape(*lead, *shp))
        r += nr
    return out


def _chip_cols(a, k, width):
    return lax.dynamic_slice_in_dim(a, k * width, width, axis=a.ndim - 1)


def _across_chips(gathered, c0_only_shape):
    return gathered.reshape(2, 2, 2, *c0_only_shape)[:, :, 0].reshape(N_CHIPS, *c0_only_shape)


def kernel(x, c, ctx, c_ctx, ada_w, ada_b, norm_g, mlp_w1, mlp_w2, pool_w, pool_scale, attn_w_qkv, attn_w_o, attn_q_g, attn_k_g, gm_w_in, gm_ln_g, gm_ln_b, gm_ws, gm_bs, gm_w_out, final_g, loss_target, m_c_ctx, m_ada_w, m_ada_b, m_norm_g, m_mlp_w1, m_mlp_w2, m_pool_w, m_pool_scale, m_attn_w_qkv, m_attn_w_o, m_attn_q_g, m_attn_k_g, m_gm_w_in, m_gm_ln_g, m_gm_ln_b, m_gm_ws, m_gm_bs, m_gm_w_out, m_final_g, v_c_ctx, v_ada_w, v_ada_b, v_norm_g, v_mlp_w1, v_mlp_w2, v_pool_w, v_pool_scale, v_attn_w_qkv, v_attn_w_o, v_attn_q_g, v_attn_k_g, v_gm_w_in, v_gm_ln_g, v_gm_ln_b, v_gm_ws, v_gm_bs, v_gm_w_out, v_final_g):
    seq, d = x.shape[1], x.shape[2]
    n_ctx = ctx.shape[1]
    total = n_ctx + seq
    hd = attn_q_g.shape[-1]
    nh = d // hd
    nkv = nh // 2
    gg, ch = gm_ws.shape[1], gm_ws.shape[-1]
    half = gm_w_out.shape[1] * N_CHIPS
    pgw = pool_w.shape[-1]
    tm = min(256, n_ctx)
    tm_lat = min(2 * tm, seq)
    nct = n_ctx // tm
    seg_lens = (n_ctx, seq)

    mx, my, mc = _my_place()
    chip = 2 * mx + my
    me = 4 * mx + 2 * my + mc

    c_rows = jnp.concatenate([c, jnp.zeros((7, d), F32)], axis=0)
    c_gath = _all_gather_small(c_rows, "gather_cond")[:, 0, :]
    c_all = jnp.concatenate([c_gath, c_ctx[None, :], jnp.zeros((7, d), F32)], axis=0)
    ncs = ada_w.shape[-1]
    ada_cols = _ada_fwd(c_all, ada_w, _chip_cols(ada_b, chip, ncs))
    small_shapes = [ada_cols.shape, norm_g.shape, pool_scale.shape, gm_ln_g.shape, gm_ln_b.shape]
    gathered = _all_gather_small(_pack([ada_cols, norm_g, pool_scale, gm_ln_g, gm_ln_b]), "gather_small_params")
    per_chip = _across_chips(gathered, gathered.shape[1:])
    ada_g, ng_g, ps_g, lng_g, lnb_g = _unpack(per_chip, small_shapes)

    def join_last(a):
        return jnp.moveaxis(a, 0, -2).reshape(*a.shape[1:-1], N_CHIPS * a.shape[-1])

    ada_full = join_last(ada_g)
    ng_full = join_last(ng_g)
    ps_full = join_last(ps_g)
    lng_full = join_last(lng_g)
    lnb_full = join_last(lnb_g)
    mod_lat = lax.dynamic_slice_in_dim(ada_full, me, 1, axis=1).reshape(DEPTH, 6, d)
    mod_ctx = ada_full[:, 8].reshape(DEPTH, 6, d)
    mods = jnp.stack([jnp.concatenate([mod_ctx, ng_full], axis=1), jnp.concatenate([mod_lat, ng_full], axis=1)],
                     axis=1)

    weight_groups = [
        [pool_w],
        [mlp_w1[0]],
        [mlp_w2[0]],
        [attn_w_qkv[0], attn_w_o[0]],
        [mlp_w1[1], mlp_w2[1], mlp_w1[2], mlp_w2[2], gm_w_in[0], gm_w_out[0], mlp_w1[3], mlp_w2[3]],
    ]
    gathers = [None] * len(weight_groups)

    def gather_start(gi, after):
        shards, _ = lax.optimization_barrier(([w.astype(BF16) for w in weight_groups[gi]], after))
        lands = [_landing_for_gather(s, chip) for s in shards]
        gathers[gi] = _exchange_start(shards, lands, GATHER_PLAN, f"gather_weights_{gi}_start")
        return gathers[gi][4][0:1, 0:1]

    def gathered(gi, after):
        send, recv, srcs, lands, _ = gathers[gi]
        return _exchange_wait(send, recv, srcs, lands, GATHER_PLAN, after, f"gather_weights_{gi}_wait")[1]

    def rows_joined(a):
        return a.reshape(-1, a.shape[-1])

    w1_b, w2_b = [None] * DEPTH, [None] * DEPTH
    gather_start(0, mods)
    behind_gather_1 = gather_start(1, mods)
    pw_land, = gathered(0, ps_full)
    pw_f = jnp.transpose(pw_land, (1, 2, 0, 3, 4)).reshape(pool_w.shape[0], pool_w.shape[1], pgw, pgw)

    gains = jnp.concatenate([attn_q_g, attn_k_g, jnp.zeros((6, hd), F32)], axis=0)
    ws_b = gm_ws[0].astype(BF16)
    ws_t = jnp.swapaxes(gm_ws[0], 1, 2).astype(BF16)
    bs_col = gm_bs[0][:, :, None]
    cos, sin = _rope_tables(n_ctx, seq, hd)
    lat = lambda i: mods[i, 1:2]

    hc0 = jnp.concatenate([ctx[0] + behind_gather_1, x[0]], axis=0)
    ha0, p0 = _pool_fwd(hc0, mods[0] + behind_gather_1, pw_f, ps_full, 0, nct=nct, tm=tm, seg_lens=seg_lens)
    w1_b[0], = gathered(1, ha0)
    u0 = _mlp_up(ha0, mods[0] + gather_start(2, w1_b[0]), w1_b[0], 0, nct=nct, tm=tm)
    w2_b[0], = gathered(2, u0)
    hc1, o0 = _mlp_down(ha0, u0, mods[0] + gather_start(3, w2_b[0]), w2_b[0], 0, nct=nct, tm=tm)
    wqkv_b, wo_land = gathered(3, hc1)
    mods1 = mods[1] + gather_start(4, wqkv_b)
    wo_f = rows_joined(wo_land)
    xa1, qkv, q_r, k_r, v_b = _qkv_fwd(hc1, mods1, wqkv_b, cos, sin, gains, nh=nh, nkv=nkv, nct=nct, tm=tm)
    o_att, lse = _flash_fwd(q_r, k_r, v_b, n_ctx=n_ctx, hd=hd)
    ha1, y1 = _proj_fwd(o_att, wo_f, hc1, mods1, n_ctx=n_ctx, tm=tm)
    w1_b[1], w2_b[1], w1_b[2], w2_b[2], win_b, wout_land, w1_b[3], w2_b[3] = gathered(4, ha1)
    h2, u1, o1 = _mlp_fwd(ha1, lat(1), w1_b[1], w2_b[1], 1, nct=0, tm=tm)
    wout_f = rows_joined(wout_land)
    ha2, zpre, y2 = _gmlp_fwd(h2, mods[2], win_b, lng_full, lnb_full, ws_b, bs_col, wout_f, tm=tm)
    h3, u2, o2 = _mlp_fwd(ha2, lat(2), w1_b[2], w2_b[2], 2, nct=0, tm=tm)
    ha3, p3 = _pool_fwd(h3, lat(3), pw_f, ps_full, 3, nct=0, tm=tm_lat, seg_lens=seg_lens)
    h4, u3, o3 = _mlp_fwd(ha3, lat(3), w1_b[3], w2_b[3], 3, nct=0, tm=tm)
    dh4, fin_acc = _final_loss(h4, loss_target[0], final_g[None, :], tm=tm)

    dmods = [None] * DEPTH
    scatters = [None] * (DEPTH + 2)

    def blocked_rows(g):
        return g.reshape(N_CHIPS, g.shape[1] // N_CHIPS, g.shape[2])

    def blocked_pool(dpw):
        pg = dpw.shape[0]
        return jnp.transpose(dpw.astype(BF16).reshape(pg, N_CHIPS, pgw // N_CHIPS, pgw), (1, 0, 2, 3))

    def scatter_start(i, grads):
        lands = [lax.empty((3, *g.shape[1:]), g.dtype) for g in grads]
        scatters[i] = _exchange_start(grads, lands, SCATTER_PLAN, f"scatter_grads_{i}_start")
        return scatters[i][4][0:1, 0:1]

    def mlp_back(i, h_in, dh_out, u, o, md, n_ct):
        dh_in, du, dob, mb, dmd = _mlp_bwd(h_in, dh_out, u, o, md, w1_b[i], w2_b[i], i, nct=n_ct, tm=tm)
        dw1 = _mm_tn(mb, du, f"mlp_dw1_{i}", col_blocks=N_CHIPS)
        dw2 = blocked_rows(_mm_tn(u, dob, f"mlp_dw2_{i}", relu2=True))
        return dh_in, dmd, [dw1, dw2]

    def pool_back(i, h_in, p_in, dh_out, md, n_ct, tile):
        dp, dmd_a, dps, dpw = _pool_bwd_weights(p_in, dh_out, md, pw_f, ps_full, i, nct=n_ct, tm=tile)
        dh_in, dmd_b = _pool_bwd_input(dp, h_in, dh_out, md, i, nct=n_ct, tm=tile, seg_lens=seg_lens, gw=pgw)
        return dh_in, dmd_a + dmd_b, dps, dpw

    zero_grp = jnp.zeros((1, 8, d), F32)
    dha3, dmd3, dws3 = mlp_back(3, ha3, dh4, u3, o3, lat(3), 0)
    dh3, dmd3p, dps3, dpw3 = pool_back(3, h3, p3, dha3, lat(3), 0, tm_lat)
    dmods[3] = jnp.concatenate([zero_grp, dmd3 + dmd3p], axis=0)
    tok = scatter_start(3, dws3 + [blocked_pool(dpw3)])
    dha2, dmd2, dws2 = mlp_back(2, ha2, dh3, u2, o2, lat(2) + tok, 0)
    dh2, dzpre, gated, dyb2, ab2, dmd2g, dln, dws, dbs = _gmlp_bwd(
        h2, dha2, zpre, y2, mods[2], win_b, lng_full, lnb_full, ws_b, ws_t, bs_col, wout_f, tm=tm)
    dwin = _mm_tn(ab2, dzpre, "gmlp_dw_in", col_blocks=N_CHIPS)
    dwout = blocked_rows(_mm_tn(gated, dyb2, "gmlp_dw_out"))
    dmods[2] = jnp.concatenate([zero_grp, dmd2 + dmd2g], axis=0)
    tok = scatter_start(2, dws2 + [dwin, dwout])
    dha1, dmd1, dws1 = mlp_back(1, ha1, dh2, u1, o1, lat(1) + tok, 0)
    do_att, dyb1, dmd1p = _proj_bwd(dha1, y1, mods[1], wo_f, tm=tm)
    dwo = blocked_rows(_mm_tn(o_att, dyb1, "attn_dw_o"))
    dq, dk, dv = _flash_bwd(q_r, k_r, v_b, o_att, do_att, lse, n_ctx=n_ctx, hd=hd)
    dqkv, dgains = _qkv_bwd(qkv, dq, dk, dv, cos, sin, gains, nh=nh, nkv=nkv, nct=nct, tm=tm)
    dwqkv = _mm_tn(xa1, dqkv, "attn_dw_qkv", col_blocks=N_CHIPS)
    dhc1, dmd1i = _attn_in_bwd(dqkv, wqkv_b, hc1, dha1, mods[1], nct=nct, tm=tm)
    dmods[1] = dmd1i + jnp.concatenate([zero_grp, dmd1 + dmd1p], axis=0)
    tok = scatter_start(1, dws1 + [dwqkv, dwo])
    dha0, du0, dob0, mb0, dmd0 = _mlp_bwd(ha0, dhc1, u0, o0, mods[0] + tok, w1_b[0], w2_b[0], 0, nct=nct, tm=tm)
    scatter_start(DEPTH + 1, [blocked_rows(_mm_tn(u0, dob0, "mlp_dw2_0", relu2=True))])
    dw1_0 = _mm_tn(mb0, du0, "mlp_dw1_0", col_blocks=N_CHIPS, after=scatters[DEPTH + 1][4])
    tok = scatter_start(0, [dw1_0])
    dhc0, dmd0p, dps0, dpw0 = pool_back(0, hc0, p0, dha0, mods[0] + tok, nct, tm)
    dmods[0] = dmd0 + dmd0p
    grad_x = dhc0[None]
    scatter_start(DEPTH, [blocked_pool(dpw0)])

    dmods_all = jnp.stack(dmods, axis=0)
    small_grads = [dmods_all, dws, dbs, dgains, dln, dps0, dps3, fin_acc]
    sg_shapes = [a.shape for a in small_grads]
    sg_gath = _all_gather_small(_pack(small_grads), "gather_small_grads")
    sg_sum = _sum_devices(sg_gath, "sum_small_grads")
    s_dmods, s_dws, s_dbs, s_dgains, s_dln, s_dps0, s_dps3, s_fin = _unpack(sg_sum, sg_shapes)
    loss = s_fin[1, 0]

    sources, landed = [None] * len(scatters), [None] * len(scatters)
    for i in (3, 2, 1, DEPTH + 1, 0, DEPTH):
        send, recv, srcs, lands, _ = scatters[i]
        sources[i], landed[i] = _exchange_wait(send, recv, srcs, lands, SCATTER_PLAN, sg_sum, f"scatter_grads_{i}_wait")

    def summed(name, picks):
        return _sum_partials([sources[i][j] for i, j in picks], [landed[i][j] for i, j in picks], chip,
                             f"sum_chips_{name}")

    big = [("mlp_w1", mlp_w1, m_mlp_w1, v_mlp_w1, [(i, 0) for i in range(DEPTH)]),
           ("mlp_w2", mlp_w2, m_mlp_w2, v_mlp_w2, [(DEPTH + 1, 0)] + [(i, 1) for i in range(1, DEPTH)]),
           ("pool_w", pool_w, m_pool_w, v_pool_w, [(DEPTH, 0), (3, 2)]),
           ("attn_w_qkv", attn_w_qkv, m_attn_w_qkv, v_attn_w_qkv, [(1, 2)]),
           ("attn_w_o", attn_w_o, m_attn_w_o, v_attn_w_o, [(1, 3)]),
           ("gm_w_in", gm_w_in, m_gm_w_in, v_gm_w_in, [(2, 2)]),
           ("gm_w_out", gm_w_out, m_gm_w_out, v_gm_w_out, [(2, 3)])]
    partial = [summed(name, picks) for name, _, _, _, picks in big]
    swap = _exchange_start(partial, [lax.empty(p.shape, p.dtype) for p in partial], SIBLING_PLAN,
                           "swap_with_sibling_start")
    behind_swap = swap[4][0:1, 0:1]

    dm_dev = _unpack(sg_gath, sg_shapes[:1])[0]
    dm_lat = jnp.moveaxis(dm_dev[:, :, 1, :6, :], 0, 1).reshape(DEPTH, N_DEV, 6 * d)
    dm_ctx = jnp.moveaxis(dm_dev[:, :, 0, :6, :], 0, 1).reshape(DEPTH, N_DEV, 6 * d)
    dmod16 = _chip_cols(jnp.concatenate([dm_lat, dm_ctx], axis=1), chip, ncs) + behind_swap
    g_ada_w, dcc_part = _ada_bwd(c_all, c_all.T, dmod16, ada_w)
    dcc_gath = _all_gather_small(dcc_part, "gather_d_c_ctx")
    dcc_chips = _across_chips(dcc_gath, dcc_gath.shape[1:])
    dcc_rows = _sum_devices(dcc_chips, "sum_d_c_ctx")
    dcc = dcc_rows[0]
    ada_res = _adamw(g_ada_w.reshape(-1, ncs), None, ada_w.reshape(-1, ncs),
                     m_ada_w.reshape(-1, ncs), v_ada_w.reshape(-1, ncs), "adamw_ada_w")

    partial, from_sibling = _exchange_wait(swap[0], swap[1], swap[2], swap[3], SIBLING_PLAN, ada_res[1],
                                           "swap_with_sibling_wait")
    big_out = {}
    for (name, w, m, v, _), mine, theirs in zip(big, partial, from_sibling):
        cols = w.shape[-1]
        res = _adamw(mine, theirs, w.reshape(-1, cols), m.reshape(-1, cols), v.reshape(-1, cols), f"adamw_{name}")
        big_out[name] = [r.reshape(w.shape) for r in res]
    big_out["ada_w"] = [r.reshape(ada_w.shape) for r in ada_res]

    def cols_of(a, width):
        return _chip_cols(a, chip, width)

    zero = lambda a: jnp.zeros(a.shape, F32)
    ngw = norm_g.shape[-1]
    small = {
        "c_ctx": (dcc, zero(dcc), c_ctx, m_c_ctx, v_c_ctx),
        "ada_b": (s_dmods[:, 0, :6].reshape(DEPTH, 6 * d), s_dmods[:, 1, :6].reshape(DEPTH, 6 * d), ada_b, m_ada_b,
                  v_ada_b),
        "norm_g": (cols_of(s_dmods[:, 0, 6:8], ngw), cols_of(s_dmods[:, 1, 6:8], ngw), norm_g, m_norm_g, v_norm_g),
        "pool_scale": (cols_of(jnp.stack([s_dps0[0], s_dps3[0]]), pool_scale.shape[-1]), zero(pool_scale),
                       pool_scale, m_pool_scale, v_pool_scale),
        "attn_q_g": (s_dgains[0:1], zero(attn_q_g), attn_q_g, m_attn_q_g, v_attn_q_g),
        "attn_k_g": (s_dgains[1:2], zero(attn_k_g), attn_k_g, m_attn_k_g, v_attn_k_g),
        "gm_ln_g": (cols_of(s_dln[0:1], gm_ln_g.shape[-1]), zero(gm_ln_g), gm_ln_g, m_gm_ln_g, v_gm_ln_g),
        "gm_ln_b": (cols_of(s_dln[1:2], gm_ln_b.shape[-1]), zero(gm_ln_b), gm_ln_b, m_gm_ln_b, v_gm_ln_b),
        "gm_ws": (s_dws[None], zero(gm_ws), gm_ws, m_gm_ws, v_gm_ws),
        "gm_bs": (s_dbs[None, :, :, 0], zero(gm_bs), gm_bs, m_gm_bs, v_gm_bs),
        "final_g": (s_fin[0], zero(final_g), final_g, m_final_g, v_final_g),
    }
    keys = list(small)
    packed = [_pack([small[k][t] for k in keys]) for t in range(5)]
    res = _adamw(*packed, "adamw_small")
    shapes = [small[k][2].shape for k in keys]
    small_out = {k: [] for k in keys}
    for r in res:
        for k, a in zip(keys, _unpack(r, shapes)):
            small_out[k].append(a)

    order = ["c_ctx", "ada_w", "ada_b", "norm_g", "mlp_w1", "mlp_w2", "pool_w", "pool_scale", "attn_w_qkv",
             "attn_w_o", "attn_q_g", "attn_k_g", "gm_w_in", "gm_ln_g", "gm_ln_b", "gm_ws", "gm_bs", "gm_w_out",
             "final_g"]
    allo = {**big_out, **small_out}
    outs = [loss, grad_x]
    for t in range(4):
        outs += [allo[k][t] for k in order]
    return tuple(outs)
```

```python
import math

import numpy as np
import jax
import jax.numpy as jnp
from jax import lax
from jax.experimental import pallas as pl
from jax.experimental.pallas import tpu as pltpu

F32 = jnp.float32
BF16 = jnp.bfloat16
MESH = pl.DeviceIdType.MESH

EPS = 1e-6
GRID_W = 64
ROPE_BASE = 10000.0
POOL_WINDOWS = (2, 4, 8, 16)
HALO = 8
DEPTH = 4
N_MIXERS = 3

ADAM_LR = 0.001
ADAM_B1 = 0.9
ADAM_B2 = 0.999
ADAM_EPS = 1e-08
ADAM_WD = 0.01
ADAM_STEP = 10

VMEM_LIMIT_BYTES = 56 * 1024 * 1024
LANES = 128
SUBLANES = 8
N_DEV = 8
N_CHIPS = 4

SH1, SC1, G1, SH2, SC2, G2, NG0, NG1 = range(8)


def _dot(a, b):
    return jnp.dot(a, b, preferred_element_type=F32)


def _dot_nt(a, b):
    return lax.dot_general(a, b, (((1,), (1,)), ((), ())), preferred_element_type=F32)


def _dot_tn(a, b):
    return lax.dot_general(a, b, (((0,), (0,)), ((), ())), preferred_element_type=F32)


def _dot_blocks(a, w_ref):
    return jnp.concatenate([_dot(a, w_ref[k]) for k in range(w_ref.shape[0])], axis=1)


def _dot_nt_blocks(a, w_ref):
    nb, _, w = w_ref.shape
    acc = _dot_nt(a[:, 0:w], w_ref[0])
    for k in range(1, nb):
        acc = acc + _dot_nt(a[:, k * w:(k + 1) * w], w_ref[k])
    return acc


def _params(**kw):
    return pltpu.CompilerParams(vmem_limit_bytes=VMEM_LIMIT_BYTES, **kw)


def _full(shape):
    nd = len(shape)
    return pl.BlockSpec(shape, lambda *_: (0,) * nd)


def _rows(tm, width):
    return pl.BlockSpec((tm, width), lambda i: (i, 0))


def _group_of(nct, groups):
    if groups == 1:
        return lambda i: 0
    return lambda i: jnp.where(i >= nct, 1, 0)


def _mods_spec(nct, groups, d):
    grp = _group_of(nct, groups)
    return pl.BlockSpec((None, 8, d), lambda i: (grp(i), 0, 0))


def _first_of_group(i, nct, groups):
    if groups == 1:
        return i == 0
    return jnp.logical_or(i == 0, i == nct)


def _rowsum(v):
    return jnp.sum(v, axis=0, keepdims=True)


def _rms_parts(x):
    r = lax.rsqrt(jnp.mean(x * x, axis=-1, keepdims=True) + EPS)
    return x * r, r


def _normmod(x, md, which):
    ng, sh, sc = (md[NG0:NG0 + 1], md[SH1:SH1 + 1], md[SC1:SC1 + 1]) if which == 0 else (
        md[NG1:NG1 + 1], md[SH2:SH2 + 1], md[SC2:SC2 + 1])
    xhat, r = _rms_parts(x)
    n = xhat * ng
    return n * (1.0 + sc) + sh, (xhat, r, n)


def _normmod_bwd(da, parts, md, which):
    xhat, r, n = parts
    ng, sc = (md[NG0:NG0 + 1], md[SC1:SC1 + 1]) if which == 0 else (md[NG1:NG1 + 1], md[SC2:SC2 + 1])
    dsh = _rowsum(da)
    dsc = _rowsum(da * n)
    dn = da * (1.0 + sc)
    dng = _rowsum(dn * xhat)
    dxhat = dn * ng
    dx = r * (dxhat - xhat * jnp.mean(dxhat * xhat, axis=-1, keepdims=True))
    return dx, dsh, dsc, dng


def _acc_rows(ref, first, rows):
    @pl.when(first)
    def _():
        ref[...] = jnp.zeros(ref.shape, ref.dtype)

    for r, v in rows.items():
        ref[r:r + 1, :] += v


def _shift_up(x, k):
    if k == 0:
        return x
    return pltpu.roll(x, x.shape[0] - k, axis=0)


def _gelu(x):
    k = math.sqrt(2.0 / math.pi)
    return 0.5 * x * (1.0 + jnp.tanh(k * (x + 0.044715 * x * x * x)))


def _gelu_grad(x):
    k = math.sqrt(2.0 / math.pi)
    t = jnp.tanh(k * (x + 0.044715 * x * x * x))
    return 0.5 * (1.0 + t) + 0.5 * x * (1.0 - t * t) * k * (1.0 + 3.0 * 0.044715 * x * x)


def _silu(x):
    return x / (1.0 + jnp.exp(-x))


def _silu_grad(x):
    s = 1.0 / (1.0 + jnp.exp(-x))
    return s * (1.0 + x * (1.0 - s))


def _mlp_fwd(h, mods, w1, w2, layer, *, nct, tm):
    rows, d = h.shape
    groups = mods.shape[0]
    nb, _, fc = w1.shape
    ff = nb * fc

    def body(h_ref, md_ref, w1_ref, w2_ref, h2_ref, u_ref, o_ref):
        x = h_ref[...]
        md = md_ref[...]
        m, _ = _normmod(x, md, 1)
        mb = m.astype(BF16)
        acc = jnp.zeros((tm, d), F32)
        for k in range(nb):
            u = _dot(mb, w1_ref[k])
            u_ref[:, k * fc:(k + 1) * fc] = u.astype(BF16)
            acc = acc + _dot(jnp.square(jnp.maximum(u, 0.0)).astype(BF16), w2_ref[k])
        o_ref[...] = acc.astype(BF16)
        h2_ref[...] = x + md[G2:G2 + 1] * acc

    return pl.pallas_call(
        body, name=f"mlp_fwd_{layer}", grid=(rows // tm,),
        in_specs=[_rows(tm, d), _mods_spec(nct, groups, d), _full(w1.shape), _full(w2.shape)],
        out_specs=[_rows(tm, d), _rows(tm, ff), _rows(tm, d)],
        out_shape=[jax.ShapeDtypeStruct((rows, d), F32), jax.ShapeDtypeStruct((rows, ff), BF16),
                   jax.ShapeDtypeStruct((rows, d), BF16)],
        compiler_params=_params(),
    )(h, mods, w1, w2)


def _mlp_up(h, mods, w1, layer, *, nct, tm):
    rows, d = h.shape
    groups = mods.shape[0]
    nb, _, fc = w1.shape

    def body(h_ref, md_ref, w1_ref, u_ref):
        m, _ = _normmod(h_ref[...], md_ref[...], 1)
        mb = m.astype(BF16)
        for k in range(nb):
            u_ref[:, k * fc:(k + 1) * fc] = _dot(mb, w1_ref[k]).astype(BF16)

    return pl.pallas_call(
        body, name=f"mlp_up_{layer}", grid=(rows // tm,),
        in_specs=[_rows(tm, d), _mods_spec(nct, groups, d), _full(w1.shape)],
        out_specs=_rows(tm, nb * fc), out_shape=jax.ShapeDtypeStruct((rows, nb * fc), BF16),
        compiler_params=_params(),
    )(h, mods, w1)


def _mlp_down(h, u, mods, w2, layer, *, nct, tm):
    rows, d = h.shape
    groups = mods.shape[0]
    nb, fc, _ = w2.shape

    def body(h_ref, u_ref, md_ref, w2_ref, h2_ref, o_ref):
        acc = jnp.zeros((tm, d), F32)
        for k in range(nb):
            uk = u_ref[:, k * fc:(k + 1) * fc].astype(F32)
            acc = acc + _dot(jnp.square(jnp.maximum(uk, 0.0)).astype(BF16), w2_ref[k])
        o_ref[...] = acc.astype(BF16)
        h2_ref[...] = h_ref[...] + md_ref[G2:G2 + 1, :] * acc

    return pl.pallas_call(
        body, name=f"mlp_down_{layer}", grid=(rows // tm,),
        in_specs=[_rows(tm, d), _rows(tm, nb * fc), _mods_spec(nct, groups, d), _full(w2.shape)],
        out_specs=[_rows(tm, d), _rows(tm, d)],
        out_shape=[jax.ShapeDtypeStruct((rows, d), F32), jax.ShapeDtypeStruct((rows, d), BF16)],
        compiler_params=_params(),
    )(h, u, mods, w2)


def _mlp_bwd(h1, dh2, u, o, mods, w1, w2, layer, *, nct, tm):
    rows, d = h1.shape
    groups = mods.shape[0]
    nb, _, fc = w1.shape
    ff = nb * fc

    def body(h_ref, g_ref, u_ref, o_ref, md_ref, w1_ref, w2_ref, dh_ref, du_ref, dob_ref, mb_ref, dmd_ref):
        i = pl.program_id(0)
        x = h_ref[...]
        g = g_ref[...]
        md = md_ref[...]
        m, parts = _normmod(x, md, 1)
        mb_ref[...] = m.astype(BF16)
        dg2 = _rowsum(g * o_ref[...].astype(F32))
        dob = (g * md[G2:G2 + 1]).astype(BF16)
        dob_ref[...] = dob
        dm = jnp.zeros((tm, d), F32)
        for k in range(nb):
            uk = u_ref[:, k * fc:(k + 1) * fc].astype(F32)
            dr = _dot_nt(dob, w2_ref[k])
            duk = (dr * (2.0 * jnp.maximum(uk, 0.0))).astype(BF16)
            du_ref[:, k * fc:(k + 1) * fc] = duk
            dm = dm + _dot_nt(duk, w1_ref[k])
        dx, dsh, dsc, dng = _normmod_bwd(dm, parts, md, 1)
        dh_ref[...] = g + dx
        _acc_rows(dmd_ref, _first_of_group(i, nct, groups), {SH2: dsh, SC2: dsc, G2: dg2, NG1: dng})

    return pl.pallas_call(
        body, name=f"mlp_bwd_{layer}", grid=(rows // tm,),
        in_specs=[_rows(tm, d), _rows(tm, d), _rows(tm, ff), _rows(tm, d), _mods_spec(nct, groups, d),
                  _full(w1.shape), _full(w2.shape)],
        out_specs=[_rows(tm, d), _rows(tm, ff), _rows(tm, d), _rows(tm, d), _mods_spec(nct, groups, d)],
        out_shape=[jax.ShapeDtypeStruct((rows, d), F32), jax.ShapeDtypeStruct((rows, ff), BF16),
                   jax.ShapeDtypeStruct((rows, d), BF16), jax.ShapeDtypeStruct((rows, d), BF16),
                   jax.ShapeDtypeStruct((groups, 8, d), F32)],
        compiler_params=_params(),
    )(h1, dh2, u, o, mods, w1, w2)


def _div_tile(n, cap):
    if n <= cap:
        return n
    return max(t for t in range(LANES, cap + 1, LANES) if n % t == 0)


DW_TOKEN_TILE_CAP = 4224


def _mm_tn(a, b, name, *, relu2=False, col_blocks=1, after=None):
    rows, m = a.shape
    n = b.shape[1]
    tmm = min(m, 1024)
    tn = min(n // col_blocks, 2048)
    per_block = n // col_blocks // tn
    tr = _div_tile(rows, DW_TOKEN_TILE_CAP)
    tokens = [] if after is None else [after]

    def body(a_ref, b_ref, *rest):
        o_ref, acc_ref = rest[len(tokens):]
        r = pl.program_id(2)

        @pl.when(r == 0)
        def _():
            acc_ref[...] = jnp.zeros(acc_ref.shape, F32)

        av = a_ref[...]
        if relu2:
            av = jnp.square(jnp.maximum(av.astype(F32), 0.0)).astype(BF16)
        acc_ref[...] += _dot_tn(av, b_ref[...])

        @pl.when(r == pl.num_programs(2) - 1)
        def _():
            o_ref[...] = acc_ref[...].astype(BF16)

    return pl.pallas_call(
        body, name=name, grid=(m // tmm, n // tn, rows // tr),
        in_specs=[pl.BlockSpec((tr, tmm), lambda i, j, r: (r, i)), pl.BlockSpec((tr, tn), lambda i, j, r: (r, j))]
        + [pl.BlockSpec((8, LANES), lambda i, j, r: (0, 0))] * len(tokens),
        out_specs=pl.BlockSpec((None, tmm, tn), lambda i, j, r: (j // per_block, i, j % per_block)),
        out_shape=jax.ShapeDtypeStruct((col_blocks, m, n // col_blocks), BF16),
        scratch_shapes=[pltpu.VMEM((tmm, tn), F32)],
        compiler_params=_params(),
    )(a, b, *tokens)


def _halo_specs(tm, d, rows):
    per = tm // HALO
    prev = pl.BlockSpec((HALO, d), lambda i: (jnp.maximum(i * per - 1, 0), 0))
    nxt = pl.BlockSpec((HALO, d), lambda i: (jnp.minimum((i + 1) * per, rows // HALO - 1), 0))
    return prev, _rows(tm, d), nxt


def _segment_positions(i, tm, nct, groups, seg_lens):
    if groups == 1:
        start, length = 0, seg_lens[-1]
    else:
        start = jnp.where(i >= nct, nct, 0)
        length = jnp.where(i >= nct, seg_lens[1], seg_lens[0])
    rid = lax.broadcasted_iota(jnp.int32, (tm + 2 * HALO, 1), 0)
    pos = (i - start) * tm - HALO + rid
    return pos, length


def _window_count(pos, length, w):
    hi = jnp.minimum(pos + (w - w // 2), length)
    lo = jnp.maximum(pos - w // 2, 0)
    return (hi - lo).astype(F32)


def _window_sum(xg, w, lead):
    b, k = xg, 1
    while k < w:
        b = b + _shift_up(b, k)
        k *= 2
    return _shift_up(b, HALO - lead)[0:xg.shape[0] - 2 * HALO]


def _pooled(ext, md, pos, length, gw):
    tm = ext.shape[0] - 2 * HALO
    a_ext, parts = _normmod(ext, md, 0)
    valid = jnp.logical_and(pos >= 0, pos < length)
    a_ext = jnp.where(valid, a_ext, 0.0)
    pos_c = pos[HALO:HALO + tm]
    ps = []
    for g, w in enumerate(POOL_WINDOWS):
        xg = a_ext[:, g * gw:(g + 1) * gw]
        s = _window_sum(xg, w, w // 2)
        ps.append(s * (1.0 / _window_count(pos_c, length, w)) - xg[HALO:HALO + tm])
    return ps, parts


def _pool_fwd(h, mods, pw, pscale, layer, *, nct, tm, seg_lens):
    rows, d = h.shape
    groups = mods.shape[0]
    pg, gw = pw.shape[1], pw.shape[-1]

    def body(prev_ref, cur_ref, next_ref, md_ref, pw_ref, ps_ref, out_ref, p_ref):
        i = pl.program_id(0)
        md = md_ref[...]
        cur = cur_ref[...]
        ext = jnp.concatenate([prev_ref[...], cur, next_ref[...]], axis=0)
        pos, length = _segment_positions(i, tm, nct, groups, seg_lens)
        ps, _ = _pooled(ext, md, pos, length, gw)
        for g in range(pg):
            pb = ps[g].astype(BF16)
            p_ref[:, g * gw:(g + 1) * gw] = pb
            yg = _dot(pb, pw_ref[g]) * ps_ref[:, g * gw:(g + 1) * gw]
            out_ref[:, g * gw:(g + 1) * gw] = cur[:, g * gw:(g + 1) * gw] + md[G1:G1 + 1, g * gw:(g + 1) * gw] * yg

    j = layer // N_MIXERS
    return pl.pallas_call(
        body, name=f"pool_fwd_{layer}", grid=(rows // tm,),
        in_specs=[*_halo_specs(tm, d, rows), _mods_spec(nct, groups, d),
                  pl.BlockSpec((None, pg, gw, gw), lambda i: (j, 0, 0, 0)), _full((1, d))],
        out_specs=[_rows(tm, d), _rows(tm, d)],
        out_shape=[jax.ShapeDtypeStruct((rows, d), F32), jax.ShapeDtypeStruct((rows, d), BF16)],
        compiler_params=_params(),
    )(h, h, h, mods, pw, pscale[j:j + 1])


def _pool_bwd_weights(p, dh1, mods, pw, pscale, layer, *, nct, tm):
    rows, d = p.shape
    groups = mods.shape[0]
    pg, gw = pw.shape[1], pw.shape[-1]

    def body(p_ref, g_ref, md_ref, pw_ref, ps_ref, dp_ref, dmd_ref, dps_ref, dpw_ref):
        i = pl.program_id(0)
        md = md_ref[...]
        gup = g_ref[...]

        @pl.when(i == 0)
        def _():
            dps_ref[...] = jnp.zeros(dps_ref.shape, F32)
            dpw_ref[...] = jnp.zeros(dpw_ref.shape, F32)

        dg1 = []
        for g in range(pg):
            cols = slice(g * gw, (g + 1) * gw)
            pb = p_ref[:, cols]
            yp = _dot(pb, pw_ref[g])
            sc = ps_ref[:, cols]
            dg1.append(_rowsum(gup[:, cols] * (yp * sc)))
            dy = gup[:, cols] * md[G1:G1 + 1, cols]
            dps_ref[0:1, cols] += _rowsum(dy * yp)
            dyp = (dy * sc).astype(BF16)
            dp_ref[:, cols] = _dot_nt(dyp, pw_ref[g])
            dpw_ref[g] += _dot_tn(pb, dyp)
        _acc_rows(dmd_ref, _first_of_group(i, nct, groups), {G1: jnp.concatenate(dg1, axis=1)})

    j = layer // N_MIXERS
    return pl.pallas_call(
        body, name=f"pool_bwd_w_{layer}", grid=(rows // tm,),
        in_specs=[_rows(tm, d), _rows(tm, d), _mods_spec(nct, groups, d),
                  pl.BlockSpec((None, pg, gw, gw), lambda i: (j, 0, 0, 0)), _full((1, d))],
        out_specs=[_rows(tm, d), _mods_spec(nct, groups, d), _full((8, d)), _full((pg, gw, gw))],
        out_shape=[jax.ShapeDtypeStruct((rows, d), F32), jax.ShapeDtypeStruct((groups, 8, d), F32),
                   jax.ShapeDtypeStruct((8, d), F32), jax.ShapeDtypeStruct((pg, gw, gw), F32)],
        compiler_params=_params(),
    )(p, dh1, mods, pw, pscale[j:j + 1])


def _pool_bwd_input(dp, h, dh1, mods, layer, *, nct, tm, seg_lens, gw):
    rows, d = h.shape
    groups = mods.shape[0]

    def body(prev_ref, cur_ref, next_ref, h_ref, g_ref, md_ref, dh_ref, dmd_ref):
        i = pl.program_id(0)
        md = md_ref[...]
        dp_cur = cur_ref[...]
        ext = jnp.concatenate([prev_ref[...], dp_cur, next_ref[...]], axis=0)
        pos, length = _segment_positions(i, tm, nct, groups, seg_lens)
        valid = jnp.logical_and(pos >= 0, pos < length)
        das = []
        for g, w in enumerate(POOL_WINDOWS):
            cols = slice(g * gw, (g + 1) * gw)
            q = jnp.where(valid, ext[:, cols] * (1.0 / jnp.maximum(_window_count(pos, length, w), 1.0)), 0.0)
            das.append(_window_sum(q, w, w // 2 - 1) - dp_cur[:, cols])
        da = jnp.concatenate(das, axis=1)
        _, parts = _normmod(h_ref[...], md, 0)
        dx, dsh, dsc, dng = _normmod_bwd(da, parts, md, 0)
        dh_ref[...] = g_ref[...] + dx
        _acc_rows(dmd_ref, _first_of_group(i, nct, groups), {SH1: dsh, SC1: dsc, NG0: dng})

    return pl.pallas_call(
        body, name=f"pool_bwd_x_{layer}", grid=(rows // tm,),
        in_specs=[*_halo_specs(tm, d, rows), _rows(tm, d), _rows(tm, d), _mods_spec(nct, groups, d)],
        out_specs=[pl.BlockSpec((tm, d), lambda i: (jnp.maximum(i - nct, 0), 0)), _mods_spec(nct, groups, d)],
        out_shape=[jax.ShapeDtypeStruct((rows - nct * tm, d), F32), jax.ShapeDtypeStruct((groups, 8, d), F32)],
        compiler_params=_params(),
    )(dp, dp, dp, h, dh1, mods)


def _rope_tables(n_ctx, seq, hd):
    half = hd // 2
    n_rows = seq // GRID_W
    inv = np.float32(ROPE_BASE) ** (-np.arange(0, half, 2, dtype=np.float32) / np.float32(half))
    ar = np.arange(n_rows, dtype=np.float32)[:, None] * inv[None, :]
    ac = np.arange(GRID_W, dtype=np.float32)[:, None] * inv[None, :]

    def over_tokens(row_part, col_part):
        return np.repeat(row_part, GRID_W, axis=0), np.tile(col_part, (n_rows, 1))

    cr, cc = over_tokens(np.cos(ar), np.cos(ac))
    sr, sc = over_tokens(np.sin(ar), np.sin(ac))
    cos = np.concatenate([cr, cr, cc, cc], axis=1)
    sin = np.concatenate([-sr, sr, -sc, sc], axis=1)
    cos = np.concatenate([np.ones((n_ctx, hd), np.float32), cos], axis=0)
    sin = np.concatenate([np.zeros((n_ctx, hd), np.float32), sin], axis=0)
    return jnp.asarray(cos, F32), jnp.asarray(sin, F32)


def _rope_partner(x):
    hd = x.shape[-1]
    q = hd // 4
    lane = lax.broadcasted_iota(jnp.int32, x.shape, 1)
    first = (lane % (2 * q)) < q
    return jnp.where(first, pltpu.roll(x, hd - q, axis=1), pltpu.roll(x, q, axis=1))


def _qkv_fwd(h, mods, wqkv, cos, sin, gains, *, nh, nkv, nct, tm):
    rows, d = h.shape
    qw = wqkv.shape[0] * wqkv.shape[-1]
    hd = cos.shape[-1]

    def body(h_ref, md_ref, w_ref, cos_ref, sin_ref, gn_ref, xa_ref, qkv_ref, q_ref, k_ref, v_ref):
        a, _ = _normmod(h_ref[...], md_ref[...], 0)
        xa = a.astype(BF16)
        xa_ref[...] = xa
        qkv = _dot_blocks(xa, w_ref)
        qkv_ref[...] = qkv
        c, s = cos_ref[...], sin_ref[...]
        for hh in range(nh + nkv):
            xh = qkv[:, hh * hd:(hh + 1) * hd]
            xhat, _ = _rms_parts(xh)
            y = xhat * (gn_ref[0:1, :] if hh < nh else gn_ref[1:2, :])
            rot = (y * c + _rope_partner(y) * s).astype(BF16)
            if hh < nh:
                q_ref[:, hh * hd:(hh + 1) * hd] = rot
            else:
                k_ref[:, (hh - nh) * hd:(hh - nh + 1) * hd] = rot
        v_ref[...] = qkv[:, (nh + nkv) * hd:].astype(BF16)

    return pl.pallas_call(
        body, name="attn_qkv_fwd", grid=(rows // tm,),
        in_specs=[_rows(tm, d), _mods_spec(nct, 2, d), _full(wqkv.shape), _rows(tm, hd), _rows(tm, hd),
                  _full((8, hd))],
        out_specs=[_rows(tm, d), _rows(tm, qw), pl.BlockSpec((tm, nh * hd), lambda i: (jnp.maximum(i - nct, 0), 0)),
                   _rows(tm, nkv * hd), _rows(tm, nkv * hd)],
        out_shape=[jax.ShapeDtypeStruct((rows, d), BF16), jax.ShapeDtypeStruct((rows, qw), F32),
                   jax.ShapeDtypeStruct((rows - nct * tm, nh * hd), BF16),
                   jax.ShapeDtypeStruct((rows, nkv * hd), BF16), jax.ShapeDtypeStruct((rows, nkv * hd), BF16)],
        compiler_params=_params(),
    )(h, mods, wqkv, cos, sin, gains)


ATTN_Q_TILE_CAP = 1024
ATTN_KV_TILE_CAP = 4224
ATTN_ROW_GROUP = 256
LOG2E = 1.4426950408889634


def _attn_tiles(seq, total):
    tq = _div_tile(seq, ATTN_Q_TILE_CAP)
    return tq, _div_tile(total, ATTN_KV_TILE_CAP), min(ATTN_ROW_GROUP, tq)


def _flash_fwd(q, k, v, *, n_ctx, hd):
    total = k.shape[0]
    seq = total - n_ctx
    nkv = k.shape[1] // hd
    tq, tk, rg = _attn_tiles(seq, total)
    nk = total // tk
    scale = hd ** -0.5
    c2 = scale * LOG2E

    def body(q_ref, k_ref, v_ref, o_ref, lse_ref, m_sc, l_sc, acc_sc):
        ki = pl.program_id(2)

        @pl.when(ki == 0)
        def _():
            m_sc[...] = jnp.full(m_sc.shape, -jnp.inf, F32)
            l_sc[...] = jnp.zeros(l_sc.shape, F32)
            acc_sc[...] = jnp.zeros(acc_sc.shape, F32)

        kk, vv = k_ref[...], v_ref[...]
        groups = [(g, sub) for g in range(2) for sub in range(tq // rg)]

        def scores(g, sub):
            return _dot_nt(q_ref[sub * rg:(sub + 1) * rg, g * hd:(g + 1) * hd], kk)

        s_next = scores(*groups[0])
        for idx, (g, sub) in enumerate(groups):
            s = s_next
            if idx + 1 < len(groups):
                s_next = scores(*groups[idx + 1])
            rows = slice(g * tq + sub * rg, g * tq + (sub + 1) * rg)
            m_old = m_sc[rows]
            m_new = jnp.maximum(m_old, jnp.max(s, axis=-1, keepdims=True))
            alpha = jnp.exp2((m_old - m_new) * c2)
            p = jnp.exp2((s - m_new) * c2)
            l_sc[rows] = alpha * l_sc[rows] + jnp.sum(p, axis=-1, keepdims=True)
            acc_sc[rows] = alpha * acc_sc[rows] + _dot(p.astype(BF16), vv)
            m_sc[rows] = m_new

        @pl.when(ki == nk - 1)
        def _():
            o2 = acc_sc[...] / l_sc[...]
            lse = m_sc[...] * scale + jnp.log(l_sc[...])
            o_ref[:, :hd] = o2[:tq].astype(BF16)
            o_ref[:, hd:] = o2[tq:].astype(BF16)
            lse_ref[:, 0:1] = lse[:tq]
            lse_ref[:, 1:2] = lse[tq:]

    return pl.pallas_call(
        body, name="attn_flash_fwd", grid=(nkv, seq // tq, nk),
        in_specs=[pl.BlockSpec((tq, 2 * hd), lambda h, i, j: (i, h)),
                  pl.BlockSpec((tk, hd), lambda h, i, j: (j, h)),
                  pl.BlockSpec((tk, hd), lambda h, i, j: (j, h))],
        out_specs=[pl.BlockSpec((tq, 2 * hd), lambda h, i, j: (i, h)),
                   pl.BlockSpec((None, tq, 2), lambda h, i, j: (h, i, 0))],
        out_shape=[jax.ShapeDtypeStruct((seq, 2 * nkv * hd), BF16), jax.ShapeDtypeStruct((nkv, seq, 2), F32)],
        scratch_shapes=[pltpu.VMEM((2 * tq, 1), F32), pltpu.VMEM((2 * tq, 1), F32), pltpu.VMEM((2 * tq, hd), F32)],
        compiler_params=_params(),
    )(q, k, v)


def _flash_bwd(q, k, v, o, do, lse, *, n_ctx, hd):
    total = k.shape[0]
    seq = total - n_ctx
    nkv = k.shape[1] // hd
    tq, tk, rg = _attn_tiles(seq, total)
    scale = hd ** -0.5
    c2 = scale * LOG2E

    def body(q_ref, k_ref, v_ref, o_ref, do_ref, lse_ref, dq_ref, dk_ref, dv_ref):
        ki, qi = pl.program_id(1), pl.program_id(2)
        kk, vv = k_ref[...], v_ref[...]

        @pl.when(qi == 0)
        def _():
            dk_ref[...] = jnp.zeros(dk_ref.shape, F32)
            dv_ref[...] = jnp.zeros(dv_ref.shape, F32)

        dk_acc = jnp.zeros((tk, hd), F32)
        dv_acc = jnp.zeros((tk, hd), F32)
        for g in range(2):
            for sub in range(tq // rg):
                rs = slice(sub * rg, (sub + 1) * rg)
                cs = slice(g * hd, (g + 1) * hd)
                qq = q_ref[rs, cs]
                dd = do_ref[rs, cs]
                delta = jnp.sum(dd.astype(F32) * o_ref[rs, cs].astype(F32), axis=-1, keepdims=True)
                p = jnp.exp2(_dot_nt(qq, kk) * c2 - lse_ref[rs, g:g + 1] * LOG2E)
                dp = _dot_nt(dd, vv)
                ds = (p * (dp - delta) * scale).astype(BF16)
                dv_acc = dv_acc + _dot_tn(p.astype(BF16), dd)
                dk_acc = dk_acc + _dot_tn(ds, qq)
                dq = _dot(ds, kk)
                rows = pl.ds(pl.multiple_of(qi * tq, tq) + sub * rg, rg)

                @pl.when(ki == 0)
                def _():
                    dq_ref[rows, cs] = dq

                @pl.when(ki > 0)
                def _():
                    dq_ref[rows, cs] += dq
        dk_ref[...] += dk_acc
        dv_ref[...] += dv_acc

    return pl.pallas_call(
        body, name="attn_flash_bwd", grid=(nkv, total // tk, seq // tq),
        in_specs=[pl.BlockSpec((tq, 2 * hd), lambda h, j, i: (i, h)),
                  pl.BlockSpec((tk, hd), lambda h, j, i: (j, h)),
                  pl.BlockSpec((tk, hd), lambda h, j, i: (j, h)),
                  pl.BlockSpec((tq, 2 * hd), lambda h, j, i: (i, h)),
                  pl.BlockSpec((tq, 2 * hd), lambda h, j, i: (i, h)),
                  pl.BlockSpec((None, tq, 2), lambda h, j, i: (h, i, 0))],
        out_specs=[pl.BlockSpec((seq, 2 * hd), lambda h, j, i: (0, h)),
                   pl.BlockSpec((tk, hd), lambda h, j, i: (j, h)),
                   pl.BlockSpec((tk, hd), lambda h, j, i: (j, h))],
        out_shape=[jax.ShapeDtypeStruct((seq, 2 * nkv * hd), F32), jax.ShapeDtypeStruct((total, nkv * hd), F32),
                   jax.ShapeDtypeStruct((total, nkv * hd), F32)],
        compiler_params=_params(),
    )(q, k, v, o, do, lse)


def _proj_fwd(o, wo, hc, mods, *, n_ctx, tm):
    seq, d = o.shape
    off = n_ctx // tm

    def body(o_ref, w_ref, h_ref, md_ref, h1_ref, y_ref):
        y = _dot(o_ref[...], w_ref[...])
        y_ref[...] = y.astype(BF16)
        h1_ref[...] = h_ref[...] + md_ref[G1:G1 + 1, :] * y

    return pl.pallas_call(
        body, name="attn_proj_fwd", grid=(seq // tm,),
        in_specs=[_rows(tm, d), _full((d, d)),
                  pl.BlockSpec((tm, d), lambda i: (i + off, 0)), pl.BlockSpec((None, 8, d), lambda i: (1, 0, 0))],
        out_specs=[_rows(tm, d), _rows(tm, d)],
        out_shape=[jax.ShapeDtypeStruct((seq, d), F32), jax.ShapeDtypeStruct((seq, d), BF16)],
        compiler_params=_params(),
    )(o, wo, hc, mods)


def _proj_bwd(dh1, y, mods, wo, *, tm):
    seq, d = dh1.shape

    def body(g_ref, y_ref, md_ref, w_ref, do_ref, dyb_ref, dmd_ref):
        i = pl.program_id(0)
        g = g_ref[...]
        dyb = (g * md_ref[G1:G1 + 1, :]).astype(BF16)
        dyb_ref[...] = dyb
        do_ref[...] = _dot_nt(dyb, w_ref[...]).astype(BF16)
        _acc_rows(dmd_ref, i == 0, {G1: _rowsum(g * y_ref[...].astype(F32))})

    return pl.pallas_call(
        body, name="attn_proj_bwd", grid=(seq // tm,),
        in_specs=[_rows(tm, d), _rows(tm, d), pl.BlockSpec((None, 8, d), lambda i: (1, 0, 0)), _full((d, d))],
        out_specs=[_rows(tm, d), _rows(tm, d), pl.BlockSpec((None, 8, d), lambda i: (0, 0, 0))],
        out_shape=[jax.ShapeDtypeStruct((seq, d), BF16), jax.ShapeDtypeStruct((seq, d), BF16),
                   jax.ShapeDtypeStruct((1, 8, d), F32)],
        compiler_params=_params(),
    )(dh1, y, mods, wo)


def _qkv_bwd(qkv, dq, dk, dv, cos, sin, gains, *, nh, nkv, nct, tm):
    rows, qw = qkv.shape
    hd = cos.shape[-1]

    def body(qkv_ref, dq_ref, dk_ref, dv_ref, cos_ref, sin_ref, gn_ref, out_ref, dgn_ref):
        i = pl.program_id(0)
        c, s = cos_ref[...], sin_ref[...]
        is_lat = (i >= nct).astype(F32)
        dqg = jnp.zeros((1, hd), F32)
        dkg = jnp.zeros((1, hd), F32)
        for hh in range(nh + nkv):
            if hh < nh:
                dr = dq_ref[:, hh * hd:(hh + 1) * hd] * is_lat
                gn = gn_ref[0:1, :]
            else:
                dr = dk_ref[:, (hh - nh) * hd:(hh - nh + 1) * hd]
                gn = gn_ref[1:2, :]
            dy = dr * c + _rope_partner(dr * s)
            xhat, r = _rms_parts(qkv_ref[:, hh * hd:(hh + 1) * hd])
            dgh = _rowsum(dy * xhat)
            if hh < nh:
                dqg = dqg + dgh
            else:
                dkg = dkg + dgh
            dxhat = dy * gn
            dx = r * (dxhat - xhat * jnp.mean(dxhat * xhat, axis=-1, keepdims=True))
            out_ref[:, hh * hd:(hh + 1) * hd] = dx.astype(BF16)
        out_ref[:, (nh + nkv) * hd:] = dv_ref[...].astype(BF16)
        _acc_rows(dgn_ref, i == 0, {0: dqg, 1: dkg})

    return pl.pallas_call(
        body, name="attn_qkv_bwd", grid=(rows // tm,),
        in_specs=[_rows(tm, qw), pl.BlockSpec((tm, nh * hd), lambda i: (jnp.maximum(i - nct, 0), 0)),
                  _rows(tm, nkv * hd), _rows(tm, nkv * hd), _rows(tm, hd), _rows(tm, hd), _full((8, hd))],
        out_specs=[_rows(tm, qw), _full((8, hd))],
        out_shape=[jax.ShapeDtypeStruct((rows, qw), BF16), jax.ShapeDtypeStruct((8, hd), F32)],
        compiler_params=_params(),
    )(qkv, dq, dk, dv, cos, sin, gains)


def _attn_in_bwd(dqkv, wqkv, hc, dh1, mods, *, nct, tm):
    rows, d = hc.shape
    qw = dqkv.shape[1]

    def body(dz_ref, w_ref, h_ref, g_ref, md_ref, dh_ref, dmd_ref):
        i = pl.program_id(0)
        md = md_ref[...]
        da = _dot_nt_blocks(dz_ref[...], w_ref)
        _, parts = _normmod(h_ref[...], md, 0)
        dx, dsh, dsc, dng = _normmod_bwd(da, parts, md, 0)
        dh_ref[...] = g_ref[...] * (i >= nct).astype(F32) + dx
        _acc_rows(dmd_ref, _first_of_group(i, nct, 2), {SH1: dsh, SC1: dsc, NG0: dng})

    return pl.pallas_call(
        body, name="attn_in_bwd", grid=(rows // tm,),
        in_specs=[_rows(tm, qw), _full(wqkv.shape), _rows(tm, d),
                  pl.BlockSpec((tm, d), lambda i: (jnp.maximum(i - nct, 0), 0)), _mods_spec(nct, 2, d)],
        out_specs=[_rows(tm, d), _mods_spec(nct, 2, d)],
        out_shape=[jax.ShapeDtypeStruct((rows, d), F32), jax.ShapeDtypeStruct((2, 8, d), F32)],
        compiler_params=_params(),
    )(dqkv, wqkv, hc, dh1, mods)


def _gmlp_gate(z, lng, lnb, ws_ref, bs_ref, gg, ch):
    half = z.shape[1] // 2
    ggw = half // gg
    u, v = z[:, :half], z[:, half:]
    vc = v - jnp.mean(v, axis=-1, keepdims=True)
    rs = lax.rsqrt(jnp.mean(vc * vc, axis=-1, keepdims=True) + EPS)
    vhat = vc * rs
    vln = (vhat * lng + lnb).astype(BF16)
    chunks = []
    for n in range(z.shape[0] // ch):
        groups = []
        for g in range(gg):
            groups.append(_dot(ws_ref[g], vln[n * ch:(n + 1) * ch, g * ggw:(g + 1) * ggw]) + bs_ref[g])
        chunks.append(jnp.concatenate(groups, axis=1))
    sv = jnp.concatenate(chunks, axis=0) if len(chunks) > 1 else chunks[0]
    return u, sv, vhat, rs, vln


def _gmlp_fwd(h, mods, w_in, lng, lnb, ws, bs, w_out, *, tm):
    seq, d = h.shape
    zw = w_in.shape[0] * w_in.shape[-1]
    half = zw // 2
    gg, ch = ws.shape[0], ws.shape[-1]

    def body(h_ref, md_ref, win_ref, lng_ref, lnb_ref, ws_ref, bs_ref, wout_ref, h1_ref, zp_ref, y_ref):
        x = h_ref[...]
        md = md_ref[...]
        a, _ = _normmod(x, md, 0)
        zp = _dot_blocks(a.astype(BF16), win_ref)
        zp_ref[...] = zp.astype(BF16)
        u, sv, _, _, _ = _gmlp_gate(_gelu(zp), lng_ref[...], lnb_ref[...], ws_ref, bs_ref, gg, ch)
        y = _dot((u * sv).astype(BF16), wout_ref[...])
        y_ref[...] = y.astype(BF16)
        h1_ref[...] = x + md[G1:G1 + 1] * y

    return pl.pallas_call(
        body, name="gmlp_fwd", grid=(seq // tm,),
        in_specs=[_rows(tm, d), pl.BlockSpec((None, 8, d), lambda i: (1, 0, 0)),
                  _full(w_in.shape), _full((1, half)), _full((1, half)),
                  _full((gg, ch, ch)), _full((gg, ch, 1)), _full((half, d))],
        out_specs=[_rows(tm, d), _rows(tm, zw), _rows(tm, d)],
        out_shape=[jax.ShapeDtypeStruct((seq, d), F32), jax.ShapeDtypeStruct((seq, zw), BF16),
                   jax.ShapeDtypeStruct((seq, d), BF16)],
        compiler_params=_params(),
    )(h, mods, w_in, lng, lnb, ws, bs, w_out)


def _gmlp_bwd(h, dh1, zpre, y, mods, w_in, lng, lnb, ws, ws_t, bs, w_out, *, tm):
    seq, d = h.shape
    zw = w_in.shape[0] * w_in.shape[-1]
    half = zw // 2
    gg, ch = ws.shape[0], ws.shape[-1]
    ggw = half // gg

    def body(h_ref, g_ref, zp_ref, y_ref, md_ref, win_ref, lng_ref, lnb_ref, ws_ref, wst_ref, bs_ref, wout_ref,
             dh_ref, dzp_ref, gated_ref, dyb_ref, ab_ref, dmd_ref, dln_ref, dws_ref, dbs_ref):
        i = pl.program_id(0)
        x = h_ref[...]
        md = md_ref[...]
        a, parts = _normmod(x, md, 0)
        ab_ref[...] = a.astype(BF16)
        zp = zp_ref[...].astype(F32)
        lng_v = lng_ref[...]
        u, sv, vhat, rs, vln = _gmlp_gate(_gelu(zp), lng_v, lnb_ref[...], ws_ref, bs_ref, gg, ch)
        g = g_ref[...]
        dg1 = _rowsum(g * y_ref[...].astype(F32))
        dyb = (g * md[G1:G1 + 1]).astype(BF16)
        dyb_ref[...] = dyb
        gated_ref[...] = (u * sv).astype(BF16)
        dgated = _dot_nt(dyb, wout_ref[...])
        du = dgated * sv
        dsv = dgated * u

        @pl.when(i == 0)
        def _():
            dws_ref[...] = jnp.zeros(dws_ref.shape, F32)
            dbs_ref[...] = jnp.zeros(dbs_ref.shape, F32)
            dln_ref[...] = jnp.zeros(dln_ref.shape, F32)

        chunks = []
        for n in range(tm // ch):
            groups = []
            for gi in range(gg):
                blk = dsv[n * ch:(n + 1) * ch, gi * ggw:(gi + 1) * ggw]
                dbs_ref[gi] += jnp.sum(blk, axis=-1, keepdims=True)
                blk_b = blk.astype(BF16)
                dws_ref[gi] += _dot_nt(blk_b, vln[n * ch:(n + 1) * ch, gi * ggw:(gi + 1) * ggw])
                groups.append(_dot(wst_ref[gi], blk_b))
            chunks.append(jnp.concatenate(groups, axis=1))
        dvln = jnp.concatenate(chunks, axis=0) if len(chunks) > 1 else chunks[0]
        dln_ref[0:1, :] += _rowsum(dvln * vhat)
        dln_ref[1:2, :] += _rowsum(dvln)
        dvhat = dvln * lng_v
        dv = rs * (dvhat - jnp.mean(dvhat, axis=-1, keepdims=True)
                   - vhat * jnp.mean(dvhat * vhat, axis=-1, keepdims=True))
        dzp = (jnp.concatenate([du, dv], axis=1) * _gelu_grad(zp)).astype(BF16)
        dzp_ref[...] = dzp
        da = _dot_nt_blocks(dzp, win_ref)
        dx, dsh, dsc, dng = _normmod_bwd(da, parts, md, 0)
        dh_ref[...] = g + dx
        _acc_rows(dmd_ref, i == 0, {SH1: dsh, SC1: dsc, G1: dg1, NG0: dng})

    return pl.pallas_call(
        body, name="gmlp_bwd", grid=(seq // tm,),
        in_specs=[_rows(tm, d), _rows(tm, d), _rows(tm, zw), _rows(tm, d),
                  pl.BlockSpec((None, 8, d), lambda i: (1, 0, 0)),
                  _full(w_in.shape), _full((1, half)), _full((1, half)),
                  _full((gg, ch, ch)), _full((gg, ch, ch)), _full((gg, ch, 1)), _full((half, d))],
        out_specs=[_rows(tm, d), _rows(tm, zw), _rows(tm, half), _rows(tm, d), _rows(tm, d),
                   pl.BlockSpec((None, 8, d), lambda i: (0, 0, 0)), _full((8, half)), _full((gg, ch, ch)),
                   _full((gg, ch, 1))],
        out_shape=[jax.ShapeDtypeStruct((seq, d), F32), jax.ShapeDtypeStruct((seq, zw), BF16),
                   jax.ShapeDtypeStruct((seq, half), BF16), jax.ShapeDtypeStruct((seq, d), BF16),
                   jax.ShapeDtypeStruct((seq, d), BF16), jax.ShapeDtypeStruct((1, 8, d), F32),
                   jax.ShapeDtypeStruct((8, half), F32), jax.ShapeDtypeStruct((gg, ch, ch), F32),
                   jax.ShapeDtypeStruct((gg, ch, 1), F32)],
        compiler_params=_params(),
    )(h, dh1, zpre, y, mods, w_in, lng, lnb, ws, ws_t, bs, w_out)


def _final_loss(h, tgt, fg, *, tm):
    seq, d = h.shape

    def body(h_ref, t_ref, g_ref, dh_ref, acc_ref):
        i = pl.program_id(0)
        gain = g_ref[...]
        xhat, r = _rms_parts(h_ref[...])
        err = xhat * gain - t_ref[...]
        dy = err * (1.0 / d)
        dxhat = dy * gain
        dh_ref[...] = r * (dxhat - xhat * jnp.mean(dxhat * xhat, axis=-1, keepdims=True))
        part = jnp.sum(_rowsum(err * err), axis=-1, keepdims=True) * (0.5 / d)
        _acc_rows(acc_ref, i == 0, {0: _rowsum(dy * xhat), 1: jnp.broadcast_to(part, (1, d))})

    return pl.pallas_call(
        body, name="final_loss", grid=(seq // tm,),
        in_specs=[_rows(tm, d), _rows(tm, d), _full((1, d))],
        out_specs=[_rows(tm, d), _full((8, d))],
        out_shape=[jax.ShapeDtypeStruct((seq, d), F32), jax.ShapeDtypeStruct((8, d), F32)],
        compiler_params=_params(),
    )(h, tgt, fg)


def _ada_fwd(c_all, ada_w, ada_b_cols):
    depth, d, ncs = ada_w.shape

    def body(c_ref, w_ref, b_ref, o_ref):
        s = _silu(c_ref[...]).astype(BF16)
        o_ref[...] = _dot(s, w_ref[...].astype(BF16)) + b_ref[...]

    return pl.pallas_call(
        body, name="ada_fwd", grid=(depth,),
        in_specs=[_full((16, d)), pl.BlockSpec((None, d, ncs), lambda i: (i, 0, 0)),
                  pl.BlockSpec((None, 1, ncs), lambda i: (i, 0, 0))],
        out_specs=pl.BlockSpec((None, 16, ncs), lambda i: (i, 0, 0)),
        out_shape=jax.ShapeDtypeStruct((depth, 16, ncs), F32),
        compiler_params=_params(),
    )(c_all, ada_w, ada_b_cols.reshape(depth, 1, ncs))


def _ada_bwd(c_all, c_all_t, dmod, ada_w):
    depth, d, ncs = ada_w.shape

    def body(c_ref, ct_ref, dm_ref, w_ref, gw_ref, dc_ref):
        i = pl.program_id(0)
        dm = dm_ref[...]
        dctx = _rowsum(dm[8:16])
        rid = lax.broadcasted_iota(jnp.int32, (8, ncs), 0)
        low = jnp.where(rid == 0, jnp.broadcast_to(dctx, (8, ncs)), 0.0)
        dm16 = jnp.concatenate([dm[0:8], low], axis=0).astype(BF16)
        gw_ref[...] = _dot(_silu(ct_ref[...]).astype(BF16), dm16)

        @pl.when(i == 0)
        def _():
            dc_ref[...] = jnp.zeros(dc_ref.shape, F32)

        dc_ref[...] += _dot_nt(low.astype(BF16), w_ref[...].astype(BF16)) * _silu_grad(c_ref[8:9, :])

    return pl.pallas_call(
        body, name="ada_bwd", grid=(depth,),
        in_specs=[_full((16, d)), _full((d, 16)), pl.BlockSpec((None, 16, ncs), lambda i: (i, 0, 0)),
                  pl.BlockSpec((None, d, ncs), lambda i: (i, 0, 0))],
        out_specs=[pl.BlockSpec((None, d, ncs), lambda i: (i, 0, 0)), _full((8, d))],
        out_shape=[jax.ShapeDtypeStruct((depth, d, ncs), F32), jax.ShapeDtypeStruct((8, d), F32)],
        compiler_params=_params(),
    )(c_all, c_all_t, dmod, ada_w)


def _adamw_math(w, g, m, v):
    m = ADAM_B1 * m + (1.0 - ADAM_B1) * g
    v = ADAM_B2 * v + (1.0 - ADAM_B2) * jnp.square(g)
    m_hat = m * (1.0 / (1.0 - ADAM_B1 ** ADAM_STEP))
    v_hat = v * (1.0 / (1.0 - ADAM_B2 ** ADAM_STEP))
    delta = -ADAM_LR * (m_hat / (jnp.sqrt(v_hat) + ADAM_EPS) + ADAM_WD * w)
    return delta, m, v


def _adamw(ga, gb, w, m, v, name):
    rows, cols = w.shape
    tr = rows
    while tr * cols * 4 > (1 << 20) and tr % 16 == 0:
        tr //= 2
    grads = [ga] if gb is None else [ga, gb]

    def body(*refs):
        w_ref, m_ref, v_ref, g_out, d_out, m_out, v_out = refs[len(grads):]
        g = refs[0][...] if gb is None else refs[0][...] + refs[1][...]
        delta, m_new, v_new = _adamw_math(w_ref[...], g, m_ref[...], v_ref[...])
        g_out[...] = g
        d_out[...] = delta
        m_out[...] = m_new
        v_out[...] = v_new

    spec = _rows(tr, cols)
    return pl.pallas_call(
        body, name=name, grid=(rows // tr,),
        in_specs=[spec] * (len(grads) + 3), out_specs=[spec] * 4,
        out_shape=[jax.ShapeDtypeStruct((rows, cols), F32)] * 4,
        compiler_params=_params(),
    )(*grads, w, m, v)


def _sum_devices(gathered, name):
    n, rows, cols = gathered.shape
    tr = rows
    while tr * cols * 4 * n > (4 << 20) and tr % 16 == 0:
        tr //= 2

    def body(x_ref, o_ref):
        acc = x_ref[0]
        for j in range(1, n):
            acc = acc + x_ref[j]
        o_ref[...] = acc

    return pl.pallas_call(
        body, name=name, grid=(rows // tr,),
        in_specs=[pl.BlockSpec((n, tr, cols), lambda i: (0, i, 0))], out_specs=_rows(tr, cols),
        out_shape=jax.ShapeDtypeStruct((rows, cols), F32),
        compiler_params=_params(),
    )(gathered)


def _sum_partials(blocked, landeds, chip, name):
    n = len(blocked)
    cols = blocked[0].shape[-1]
    blocked = [b.reshape(N_CHIPS, -1, cols) for b in blocked]
    landeds = [l.reshape(3, -1, cols) for l in landeds]
    rows = blocked[0].shape[1]
    tr = rows
    while tr * cols * 2 * n > (1 << 20) and tr % 32 == 0:
        tr //= 2

    def body(chip_ref, *refs):
        out_ref = refs[-1]
        for li in range(n):
            acc = refs[li][...].astype(F32)
            for p in range(3):
                acc = acc + refs[n + li][p].astype(F32)
            out_ref[li] = acc

    out = pl.pallas_call(
        body, name=name,
        grid_spec=pltpu.PrefetchScalarGridSpec(
            num_scalar_prefetch=1, grid=(rows // tr,),
            in_specs=[pl.BlockSpec((None, tr, cols), lambda i, k: (k[0], i, 0))] * n
            + [pl.BlockSpec((3, tr, cols), lambda i, k: (0, i, 0))] * n,
            out_specs=pl.BlockSpec((n, tr, cols), lambda i, k: (0, i, 0))),
        out_shape=jax.ShapeDtypeStruct((n, rows, cols), F32),
        compiler_params=_params(),
    )(jnp.reshape(chip, (1,)).astype(jnp.int32), *blocked, *landeds)
    return out.reshape(n * rows, cols)


def _my_place():
    return lax.axis_index("x"), lax.axis_index("y"), lax.axis_index("c")


def _other_chips(x, y):
    return [(1 - x, y), (x, 1 - y), (1 - x, 1 - y)]


def _all_gather_small(block, name):
    rows, cols = block.shape

    def body(x_ref, out_ref, send_sems, recv_sems, local_sem):
        x, y, c = _my_place()
        me, sibling = (x, y, c), (x, y, 1 - c)
        chips = _other_chips(x, y)

        def slot(px, py, pc):
            return out_ref.at[4 * px + 2 * py + pc]

        def copy(k, blk, to, src=None):
            return pltpu.make_async_remote_copy(
                src_ref=slot(*blk) if src is None else src, dst_ref=slot(*blk),
                send_sem=send_sems.at[k], recv_sem=recv_sems.at[k], device_id=to, device_id_type=MESH)

        mine = pltpu.make_async_copy(x_ref, slot(*me), local_sem)
        mine.start()
        first = [copy(0, me, sibling, src=x_ref)]
        first += [copy(1 + j, me, (*chip, c), src=x_ref) for j, chip in enumerate(chips)]
        for cp in first:
            cp.start()
        passed = [copy(4 + j, (*chip, c), sibling) for j, chip in enumerate(chips)]
        for j, chip in enumerate(chips):
            copy(1 + j, (*chip, c), me).wait_recv()
            passed[j].start()
        copy(0, sibling, me).wait_recv()
        for j, chip in enumerate(chips):
            copy(4 + j, (*chip, 1 - c), me).wait_recv()
        for cp in first + passed:
            cp.wait_send()
        mine.wait()

    return pl.pallas_call(
        body, name=name,
        out_shape=jax.ShapeDtypeStruct((N_DEV, rows, cols), block.dtype),
        in_specs=[pl.BlockSpec(memory_space=pltpu.VMEM)],
        out_specs=pl.BlockSpec(memory_space=pltpu.VMEM),
        scratch_shapes=[pltpu.SemaphoreType.DMA((7,)), pltpu.SemaphoreType.DMA((7,)), pltpu.SemaphoreType.DMA],
        compiler_params=_params(),
    )(block)


HBM_SPEC = pl.BlockSpec(memory_space=pltpu.HBM)
SEM_SPEC = pl.BlockSpec(memory_space=pltpu.SEMAPHORE)
DATAFLOW_EFFECT = pltpu.SideEffectType.DATAFLOW_SIDE_EFFECTING


def _same_core_of_other_chips(x, y, c):
    return [(*chip, c) for chip in _other_chips(x, y)]


def _sibling_core(x, y, c):
    return [(x, y, 1 - c)]


def _gather_views(src, land, p, x, y):
    return src, land.at[2 * x + y]


def _scatter_views(src, land, p, x, y):
    peer_chip = (2 * (1 - x) + y, 2 * x + (1 - y), 2 * (1 - x) + (1 - y))[p]
    return src.at[peer_chip], land.at[p]


def _whole_views(src, land, p, x, y):
    return src, land


GATHER_PLAN = (_same_core_of_other_chips, _gather_views, 3)
SCATTER_PLAN = (_same_core_of_other_chips, _scatter_views, 3)
SIBLING_PLAN = (_sibling_core, _whole_views, 1)


def _exchange_copies(srcs, lands, send_sems, recv_sems, plan):
    peers_of, views, n_peers = plan
    x, y, c = _my_place()
    copies = []
    for j, (src, land) in enumerate(zip(srcs, lands)):
        for p, peer in enumerate(peers_of(x, y, c)):
            s_view, d_view = views(src, land, p, x, y)
            k = n_peers * j + p
            copies.append(pltpu.make_async_remote_copy(
                src_ref=s_view, dst_ref=d_view, send_sem=send_sems.at[k], recv_sem=recv_sems.at[k],
                device_id=peer, device_id_type=MESH))
    return copies


def _exchange_start(srcs, lands, plan, name):
    n = len(srcs)

    def body(*refs):
        send_sems, recv_sems = refs[2 * n], refs[2 * n + 1]
        token = refs[-1]
        for cp in _exchange_copies(refs[:n], refs[n:2 * n], send_sems, recv_sems, plan):
            cp.start()
        token[...] = jnp.zeros(token.shape, token.dtype)

    operands = [pltpu.with_memory_space_constraint(a, pltpu.HBM) for a in (*srcs, *lands)]
    out = pl.pallas_call(
        body, name=name,
        out_shape=(pltpu.SemaphoreType.DMA((plan[2] * n,)), pltpu.SemaphoreType.DMA((plan[2] * n,)),
                   *[pltpu.HBM(a.shape, a.dtype) for a in operands], jax.ShapeDtypeStruct((8, LANES), F32)),
        in_specs=[HBM_SPEC] * (2 * n),
        out_specs=(SEM_SPEC, SEM_SPEC, *[HBM_SPEC] * (2 * n), pl.BlockSpec(memory_space=pltpu.VMEM)),
        input_output_aliases={i: 2 + i for i in range(2 * n)},
        compiler_params=pltpu.CompilerParams(has_side_effects=DATAFLOW_EFFECT),
    )(*operands)
    return out[0], out[1], list(out[2:2 + n]), list(out[2 + n:2 + 2 * n]), out[-1]


def _exchange_wait(send_sems, recv_sems, srcs, lands, plan, after, name):
    n = len(srcs)

    def body(*refs):
        send, recv = refs[2 * n], refs[2 * n + 1]
        for cp in _exchange_copies(refs[:n], refs[n:2 * n], send, recv, plan):
            cp.wait_send()
            cp.wait_recv()

    out = pl.pallas_call(
        body, name=name,
        out_shape=tuple(pltpu.HBM(a.shape, a.dtype) for a in (*srcs, *lands)),
        in_specs=[HBM_SPEC] * (2 * n) + [SEM_SPEC, SEM_SPEC, HBM_SPEC],
        out_specs=tuple([HBM_SPEC] * (2 * n)),
        input_output_aliases={i: i for i in range(2 * n)},
        compiler_params=pltpu.CompilerParams(has_side_effects=DATAFLOW_EFFECT),
    )(*srcs, *lands, send_sems, recv_sems, pltpu.with_memory_space_constraint(after, pltpu.HBM))
    return list(out[:n]), list(out[n:])


def _landing_for_gather(shard, chip):
    land = lax.empty((N_CHIPS, *shard.shape), shard.dtype)
    return lax.dynamic_update_index_in_dim(land, shard, chip, 0)


TILE_ELEMS = SUBLANES * LANES


def _pack(arrays):
    parts = []
    for a in arrays:
        flat = a.reshape(-1).astype(F32)
        pad = (-flat.shape[0]) % TILE_ELEMS
        if pad:
            flat = jnp.concatenate([flat, jnp.zeros((pad,), F32)])
        parts.append(flat.reshape(-1, LANES))
    return jnp.concatenate(parts, axis=0) if len(parts) > 1 else parts[0]


def _unpack(buf, shapes):
    out, r = [], 0
    lead = buf.shape[:-2]
    for shp in shapes:
        size = math.prod(shp)
        nr = -(-size // TILE_ELEMS) * SUBLANES
        flat = buf[..., r:r + nr, :].reshape(*lead, nr * LANES)[..., :size]
        out.append(flat.reshape(*lead, *shp))
        r += nr
    return out


def _chip_cols(a, k, width):
    return lax.dynamic_slice_in_dim(a, k * width, width, axis=a.ndim - 1)


def _across_chips(gathered, c0_only_shape):
    return gathered.reshape(2, 2, 2, *c0_only_shape)[:, :, 0].reshape(N_CHIPS, *c0_only_shape)


def kernel(x, c, ctx, c_ctx, ada_w, ada_b, norm_g, mlp_w1, mlp_w2, pool_w, pool_scale, attn_w_qkv, attn_w_o, attn_q_g, attn_k_g, gm_w_in, gm_ln_g, gm_ln_b, gm_ws, gm_bs, gm_w_out, final_g, loss_target, m_c_ctx, m_ada_w, m_ada_b, m_norm_g, m_mlp_w1, m_mlp_w2, m_pool_w, m_pool_scale, m_attn_w_qkv, m_attn_w_o, m_attn_q_g, m_attn_k_g, m_gm_w_in, m_gm_ln_g, m_gm_ln_b, m_gm_ws, m_gm_bs, m_gm_w_out, m_final_g, v_c_ctx, v_ada_w, v_ada_b, v_norm_g, v_mlp_w1, v_mlp_w2, v_pool_w, v_pool_scale, v_attn_w_qkv, v_attn_w_o, v_attn_q_g, v_attn_k_g, v_gm_w_in, v_gm_ln_g, v_gm_ln_b, v_gm_ws, v_gm_bs, v_gm_w_out, v_final_g):
    seq, d = x.shape[1], x.shape[2]
    n_ctx = ctx.shape[1]
    total = n_ctx + seq
    hd = attn_q_g.shape[-1]
    nh = d // hd
    nkv = nh // 2
    gg, ch = gm_ws.shape[1], gm_ws.shape[-1]
    half = gm_w_out.shape[1] * N_CHIPS
    pgw = pool_w.shape[-1]
    tm = min(256, n_ctx)
    tm_lat = min(2 * tm, seq)
    nct = n_ctx // tm
    seg_lens = (n_ctx, seq)

    mx, my, mc = _my_place()
    chip = 2 * mx + my
    me = 4 * mx + 2 * my + mc

    c_rows = jnp.concatenate([c, jnp.zeros((7, d), F32)], axis=0)
    c_gath = _all_gather_small(c_rows, "gather_cond")[:, 0, :]
    c_all = jnp.concatenate([c_gath, c_ctx[None, :], jnp.zeros((7, d), F32)], axis=0)
    ncs = ada_w.shape[-1]
    ada_cols = _ada_fwd(c_all, ada_w, _chip_cols(ada_b, chip, ncs))
    small_shapes = [ada_cols.shape, norm_g.shape, pool_scale.shape, gm_ln_g.shape, gm_ln_b.shape]
    gathered = _all_gather_small(_pack([ada_cols, norm_g, pool_scale, gm_ln_g, gm_ln_b]), "gather_small_params")
    per_chip = _across_chips(gathered, gathered.shape[1:])
    ada_g, ng_g, ps_g, lng_g, lnb_g = _unpack(per_chip, small_shapes)

    def join_last(a):
        return jnp.moveaxis(a, 0, -2).reshape(*a.shape[1:-1], N_CHIPS * a.shape[-1])

    ada_full = join_last(ada_g)
    ng_full = join_last(ng_g)
    ps_full = join_last(ps_g)
    lng_full = join_last(lng_g)
    lnb_full = join_last(lnb_g)
    mod_lat = lax.dynamic_slice_in_dim(ada_full, me, 1, axis=1).reshape(DEPTH, 6, d)
    mod_ctx = ada_full[:, 8].reshape(DEPTH, 6, d)
    mods = jnp.stack([jnp.concatenate([mod_ctx, ng_full], axis=1), jnp.concatenate([mod_lat, ng_full], axis=1)],
                     axis=1)

    weight_groups = [
        [pool_w],
        [mlp_w1[0]],
        [mlp_w2[0]],
        [attn_w_qkv[0], attn_w_o[0]],
        [mlp_w1[1], mlp_w2[1], mlp_w1[2], mlp_w2[2], gm_w_in[0], gm_w_out[0], mlp_w1[3], mlp_w2[3]],
    ]
    gathers = [None] * len(weight_groups)

    def gather_start(gi, after):
        shards, _ = lax.optimization_barrier(([w.astype(BF16) for w in weight_groups[gi]], after))
        lands = [_landing_for_gather(s, chip) for s in shards]
        gathers[gi] = _exchange_start(shards, lands, GATHER_PLAN, f"gather_weights_{gi}_start")
        return gathers[gi][4][0:1, 0:1]

    def gathered(gi, after):
        send, recv, srcs, lands, _ = gathers[gi]
        return _exchange_wait(send, recv, srcs, lands, GATHER_PLAN, after, f"gather_weights_{gi}_wait")[1]

    def rows_joined(a):
        return a.reshape(-1, a.shape[-1])

    w1_b, w2_b = [None] * DEPTH, [None] * DEPTH
    gather_start(0, mods)
    behind_gather_1 = gather_start(1, mods)
    pw_land, = gathered(0, ps_full)
    pw_f = jnp.transpose(pw_land, (1, 2, 0, 3, 4)).reshape(pool_w.shape[0], pool_w.shape[1], pgw, pgw)

    gains = jnp.concatenate([attn_q_g, attn_k_g, jnp.zeros((6, hd), F32)], axis=0)
    ws_b = gm_ws[0].astype(BF16)
    ws_t = jnp.swapaxes(gm_ws[0], 1, 2).astype(BF16)
    bs_col = gm_bs[0][:, :, None]
    cos, sin = _rope_tables(n_ctx, seq, hd)
    lat = lambda i: mods[i, 1:2]

    hc0 = jnp.concatenate([ctx[0] + behind_gather_1, x[0]], axis=0)
    ha0, p0 = _pool_fwd(hc0, mods[0] + behind_gather_1, pw_f, ps_full, 0, nct=nct, tm=tm, seg_lens=seg_lens)
    w1_b[0], = gathered(1, ha0)
    u0 = _mlp_up(ha0, mods[0] + gather_start(2, w1_b[0]), w1_b[0], 0, nct=nct, tm=tm)
    w2_b[0], = gathered(2, u0)
    hc1, o0 = _mlp_down(ha0, u0, mods[0] + gather_start(3, w2_b[0]), w2_b[0], 0, nct=nct, tm=tm)
    wqkv_b, wo_land = gathered(3, hc1)
    mods1 = mods[1] + gather_start(4, wqkv_b)
    wo_f = rows_joined(wo_land)
    xa1, qkv, q_r, k_r, v_b = _qkv_fwd(hc1, mods1, wqkv_b, cos, sin, gains, nh=nh, nkv=nkv, nct=nct, tm=tm)
    o_att, lse = _flash_fwd(q_r, k_r, v_b, n_ctx=n_ctx, hd=hd)
    ha1, y1 = _proj_fwd(o_att, wo_f, hc1, mods1, n_ctx=n_ctx, tm=tm)
    w1_b[1], w2_b[1], w1_b[2], w2_b[2], win_b, wout_land, w1_b[3], w2_b[3] = gathered(4, ha1)
    h2, u1, o1 = _mlp_fwd(ha1, lat(1), w1_b[1], w2_b[1], 1, nct=0, tm=tm_lat)
    wout_f = rows_joined(wout_land)
    ha2, zpre, y2 = _gmlp_fwd(h2, mods[2], win_b, lng_full, lnb_full, ws_b, bs_col, wout_f, tm=tm)
    h3, u2, o2 = _mlp_fwd(ha2, lat(2), w1_b[2], w2_b[2], 2, nct=0, tm=tm_lat)
    ha3, p3 = _pool_fwd(h3, lat(3), pw_f, ps_full, 3, nct=0, tm=tm_lat, seg_lens=seg_lens)
    h4, u3, o3 = _mlp_fwd(ha3, lat(3), w1_b[3], w2_b[3], 3, nct=0, tm=tm_lat)
    dh4, fin_acc = _final_loss(h4, loss_target[0], final_g[None, :], tm=tm_lat)

    dmods = [None] * DEPTH
    scatters = [None] * (DEPTH + 2)

    def blocked_rows(g):
        return g.reshape(N_CHIPS, g.shape[1] // N_CHIPS, g.shape[2])

    def blocked_pool(dpw):
        pg = dpw.shape[0]
        return jnp.transpose(dpw.astype(BF16).reshape(pg, N_CHIPS, pgw // N_CHIPS, pgw), (1, 0, 2, 3))

    def scatter_start(i, grads):
        lands = [lax.empty((3, *g.shape[1:]), g.dtype) for g in grads]
        scatters[i] = _exchange_start(grads, lands, SCATTER_PLAN, f"scatter_grads_{i}_start")
        return scatters[i][4][0:1, 0:1]

    def mlp_back(i, h_in, dh_out, u, o, md, n_ct):
        dh_in, du, dob, mb, dmd = _mlp_bwd(h_in, dh_out, u, o, md, w1_b[i], w2_b[i], i, nct=n_ct, tm=tm_lat)
        dw1 = _mm_tn(mb, du, f"mlp_dw1_{i}", col_blocks=N_CHIPS)
        dw2 = blocked_rows(_mm_tn(u, dob, f"mlp_dw2_{i}", relu2=True))
        return dh_in, dmd, [dw1, dw2]

    def pool_back(i, h_in, p_in, dh_out, md, n_ct, tile):
        dp, dmd_a, dps, dpw = _pool_bwd_weights(p_in, dh_out, md, pw_f, ps_full, i, nct=n_ct, tm=tile)
        dh_in, dmd_b = _pool_bwd_input(dp, h_in, dh_out, md, i, nct=n_ct, tm=tile, seg_lens=seg_lens, gw=pgw)
        return dh_in, dmd_a + dmd_b, dps, dpw

    zero_grp = jnp.zeros((1, 8, d), F32)
    dha3, dmd3, dws3 = mlp_back(3, ha3, dh4, u3, o3, lat(3), 0)
    dh3, dmd3p, dps3, dpw3 = pool_back(3, h3, p3, dha3, lat(3), 0, tm_lat)
    dmods[3] = jnp.concatenate([zero_grp, dmd3 + dmd3p], axis=0)
    tok = scatter_start(3, dws3 + [blocked_pool(dpw3)])
    dha2, dmd2, dws2 = mlp_back(2, ha2, dh3, u2, o2, lat(2) + tok, 0)
    dh2, dzpre, gated, dyb2, ab2, dmd2g, dln, dws, dbs = _gmlp_bwd(
        h2, dha2, zpre, y2, mods[2], win_b, lng_full, lnb_full, ws_b, ws_t, bs_col, wout_f, tm=tm)
    dwin = _mm_tn(ab2, dzpre, "gmlp_dw_in", col_blocks=N_CHIPS)
    dwout = blocked_rows(_mm_tn(gated, dyb2, "gmlp_dw_out"))
    dmods[2] = jnp.concatenate([zero_grp, dmd2 + dmd2g], axis=0)
    tok = scatter_start(2, dws2 + [dwin, dwout])
    dha1, dmd1, dws1 = mlp_back(1, ha1, dh2, u1, o1, lat(1) + tok, 0)
    do_att, dyb1, dmd1p = _proj_bwd(dha1, y1, mods[1], wo_f, tm=tm_lat)
    dwo = blocked_rows(_mm_tn(o_att, dyb1, "attn_dw_o"))
    dq, dk, dv = _flash_bwd(q_r, k_r, v_b, o_att, do_att, lse, n_ctx=n_ctx, hd=hd)
    dqkv, dgains = _qkv_bwd(qkv, dq, dk, dv, cos, sin, gains, nh=nh, nkv=nkv, nct=nct, tm=tm)
    dwqkv = _mm_tn(xa1, dqkv, "attn_dw_qkv", col_blocks=N_CHIPS)
    dhc1, dmd1i = _attn_in_bwd(dqkv, wqkv_b, hc1, dha1, mods[1], nct=nct, tm=tm)
    dmods[1] = dmd1i + jnp.concatenate([zero_grp, dmd1 + dmd1p], axis=0)
    tok = scatter_start(1, dws1 + [dwqkv, dwo])
    dha0, du0, dob0, mb0, dmd0 = _mlp_bwd(ha0, dhc1, u0, o0, mods[0] + tok, w1_b[0], w2_b[0], 0, nct=nct, tm=tm)
    scatter_start(DEPTH + 1, [blocked_rows(_mm_tn(u0, dob0, "mlp_dw2_0", relu2=True))])
    dw1_0 = _mm_tn(mb0, du0, "mlp_dw1_0", col_blocks=N_CHIPS, after=scatters[DEPTH + 1][4])
    tok = scatter_start(0, [dw1_0])
    dhc0, dmd0p, dps0, dpw0 = pool_back(0, hc0, p0, dha0, mods[0] + tok, nct, tm)
    dmods[0] = dmd0 + dmd0p
    grad_x = dhc0[None]
    scatter_start(DEPTH, [blocked_pool(dpw0)])

    dmods_all = jnp.stack(dmods, axis=0)
    small_grads = [dmods_all, dws, dbs, dgains, dln, dps0, dps3, fin_acc]
    sg_shapes = [a.shape for a in small_grads]
    sg_gath = _all_gather_small(_pack(small_grads), "gather_small_grads")
    sg_sum = _sum_devices(sg_gath, "sum_small_grads")
    s_dmods, s_dws, s_dbs, s_dgains, s_dln, s_dps0, s_dps3, s_fin = _unpack(sg_sum, sg_shapes)
    loss = s_fin[1, 0]

    sources, landed = [None] * len(scatters), [None] * len(scatters)
    for i in (3, 2, 1, DEPTH + 1, 0, DEPTH):
        send, recv, srcs, lands, _ = scatters[i]
        sources[i], landed[i] = _exchange_wait(send, recv, srcs, lands, SCATTER_PLAN, sg_sum, f"scatter_grads_{i}_wait")

    def summed(name, picks):
        return _sum_partials([sources[i][j] for i, j in picks], [landed[i][j] for i, j in picks], chip,
                             f"sum_chips_{name}")

    big = [("mlp_w1", mlp_w1, m_mlp_w1, v_mlp_w1, [(i, 0) for i in range(DEPTH)]),
           ("mlp_w2", mlp_w2, m_mlp_w2, v_mlp_w2, [(DEPTH + 1, 0)] + [(i, 1) for i in range(1, DEPTH)]),
           ("pool_w", pool_w, m_pool_w, v_pool_w, [(DEPTH, 0), (3, 2)]),
           ("attn_w_qkv", attn_w_qkv, m_attn_w_qkv, v_attn_w_qkv, [(1, 2)]),
           ("attn_w_o", attn_w_o, m_attn_w_o, v_attn_w_o, [(1, 3)]),
           ("gm_w_in", gm_w_in, m_gm_w_in, v_gm_w_in, [(2, 2)]),
           ("gm_w_out", gm_w_out, m_gm_w_out, v_gm_w_out, [(2, 3)])]
    partial = [summed(name, picks) for name, _, _, _, picks in big]
    swap = _exchange_start(partial, [lax.empty(p.shape, p.dtype) for p in partial], SIBLING_PLAN,
                           "swap_with_sibling_start")
    behind_swap = swap[4][0:1, 0:1]

    dm_dev = _unpack(sg_gath, sg_shapes[:1])[0]
    dm_lat = jnp.moveaxis(dm_dev[:, :, 1, :6, :], 0, 1).reshape(DEPTH, N_DEV, 6 * d)
    dm_ctx = jnp.moveaxis(dm_dev[:, :, 0, :6, :], 0, 1).reshape(DEPTH, N_DEV, 6 * d)
    dmod16 = _chip_cols(jnp.concatenate([dm_lat, dm_ctx], axis=1), chip, ncs) + behind_swap
    g_ada_w, dcc_part = _ada_bwd(c_all, c_all.T, dmod16, ada_w)
    dcc_gath = _all_gather_small(dcc_part, "gather_d_c_ctx")
    dcc_chips = _across_chips(dcc_gath, dcc_gath.shape[1:])
    dcc_rows = _sum_devices(dcc_chips, "sum_d_c_ctx")
    dcc = dcc_rows[0]
    ada_res = _adamw(g_ada_w.reshape(-1, ncs), None, ada_w.reshape(-1, ncs),
                     m_ada_w.reshape(-1, ncs), v_ada_w.reshape(-1, ncs), "adamw_ada_w")

    partial, from_sibling = _exchange_wait(swap[0], swap[1], swap[2], swap[3], SIBLING_PLAN, ada_res[1],
                                           "swap_with_sibling_wait")
    big_out = {}
    for (name, w, m, v, _), mine, theirs in zip(big, partial, from_sibling):
        cols = w.shape[-1]
        res = _adamw(mine, theirs, w.reshape(-1, cols), m.reshape(-1, cols), v.reshape(-1, cols), f"adamw_{name}")
        big_out[name] = [r.reshape(w.shape) for r in res]
    big_out["ada_w"] = [r.reshape(ada_w.shape) for r in ada_res]

    def cols_of(a, width):
        return _chip_cols(a, chip, width)

    zero = lambda a: jnp.zeros(a.shape, F32)
    ngw = norm_g.shape[-1]
    small = {
        "c_ctx": (dcc, zero(dcc), c_ctx, m_c_ctx, v_c_ctx),
        "ada_b": (s_dmods[:, 0, :6].reshape(DEPTH, 6 * d), s_dmods[:, 1, :6].reshape(DEPTH, 6 * d), ada_b, m_ada_b,
                  v_ada_b),
        "norm_g": (cols_of(s_dmods[:, 0, 6:8], ngw), cols_of(s_dmods[:, 1, 6:8], ngw), norm_g, m_norm_g, v_norm_g),
        "pool_scale": (cols_of(jnp.stack([s_dps0[0], s_dps3[0]]), pool_scale.shape[-1]), zero(pool_scale),
                       pool_scale, m_pool_scale, v_pool_scale),
        "attn_q_g": (s_dgains[0:1], zero(attn_q_g), attn_q_g, m_attn_q_g, v_attn_q_g),
        "attn_k_g": (s_dgains[1:2], zero(attn_k_g), attn_k_g, m_attn_k_g, v_attn_k_g),
        "gm_ln_g": (cols_of(s_dln[0:1], gm_ln_g.shape[-1]), zero(gm_ln_g), gm_ln_g, m_gm_ln_g, v_gm_ln_g),
        "gm_ln_b": (cols_of(s_dln[1:2], gm_ln_b.shape[-1]), zero(gm_ln_b), gm_ln_b, m_gm_ln_b, v_gm_ln_b),
        "gm_ws": (s_dws[None], zero(gm_ws), gm_ws, m_gm_ws, v_gm_ws),
        "gm_bs": (s_dbs[None, :, :, 0], zero(gm_bs), gm_bs, m_gm_bs, v_gm_bs),
        "final_g": (s_fin[0], zero(final_g), final_g, m_final_g, v_final_g),
    }
    keys = list(small)
    packed = [_pack([small[k][t] for k in keys]) for t in range(5)]
    res = _adamw(*packed, "adamw_small")
    shapes = [small[k][2].shape for k in keys]
    small_out = {k: [] for k in keys}
    for r in res:
        for k, a in zip(keys, _unpack(r, shapes)):
            small_out[k].append(a)

    order = ["c_ctx", "ada_w", "ada_b", "norm_g", "mlp_w1", "mlp_w2", "pool_w", "pool_scale", "attn_w_qkv",
             "attn_w_o", "attn_q_g", "attn_k_g", "gm_w_in", "gm_ln_g", "gm_ln_b", "gm_ws", "gm_bs", "gm_w_out",
             "final_g"]
    allo = {**big_out, **small_out}
    outs = [loss, grad_x]
    for t in range(4):
        outs += [allo[k][t] for k in order]
    return tuple(outs)
```

```python
import math

import numpy as np
import jax
import jax.numpy as jnp
from jax import lax
from jax.experimental import pallas as pl
from jax.experimental.pallas import tpu as pltpu

F32 = jnp.float32
BF16 = jnp.bfloat16
MESH = pl.DeviceIdType.MESH

EPS = 1e-6
GRID_W = 64
ROPE_BASE = 10000.0
POOL_WINDOWS = (2, 4, 8, 16)
HALO = 8
DEPTH = 4
N_MIXERS = 3

ADAM_LR = 0.001
ADAM_B1 = 0.9
ADAM_B2 = 0.999
ADAM_EPS = 1e-08
ADAM_WD = 0.01
ADAM_STEP = 10

VMEM_LIMIT_BYTES = 56 * 1024 * 1024
LANES = 128
SUBLANES = 8
N_DEV = 8
N_CHIPS = 4

SH1, SC1, G1, SH2, SC2, G2, NG0, NG1 = range(8)


def _dot(a, b):
    return jnp.dot(a, b, preferred_element_type=F32)


def _dot_nt(a, b):
    return lax.dot_general(a, b, (((1,), (1,)), ((), ())), preferred_element_type=F32)


def _dot_tn(a, b):
    return lax.dot_general(a, b, (((0,), (0,)), ((), ())), preferred_element_type=F32)


def _dot_blocks(a, w_ref):
    return jnp.concatenate([_dot(a, w_ref[k]) for k in range(w_ref.shape[0])], axis=1)


def _dot_nt_blocks(a, w_ref):
    nb, _, w = w_ref.shape
    acc = _dot_nt(a[:, 0:w], w_ref[0])
    for k in range(1, nb):
        acc = acc + _dot_nt(a[:, k * w:(k + 1) * w], w_ref[k])
    return acc


def _params(**kw):
    return pltpu.CompilerParams(vmem_limit_bytes=VMEM_LIMIT_BYTES, **kw)


def _full(shape):
    nd = len(shape)
    return pl.BlockSpec(shape, lambda *_: (0,) * nd)


def _rows(tm, width):
    return pl.BlockSpec((tm, width), lambda i: (i, 0))


def _group_of(nct, groups):
    if groups == 1:
        return lambda i: 0
    return lambda i: jnp.where(i >= nct, 1, 0)


def _mods_spec(nct, groups, d):
    grp = _group_of(nct, groups)
    return pl.BlockSpec((None, 8, d), lambda i: (grp(i), 0, 0))


def _first_of_group(i, nct, groups):
    if groups == 1:
        return i == 0
    return jnp.logical_or(i == 0, i == nct)


def _rowsum(v):
    return jnp.sum(v, axis=0, keepdims=True)


def _rms_parts(x):
    r = lax.rsqrt(jnp.mean(x * x, axis=-1, keepdims=True) + EPS)
    return x * r, r


def _normmod(x, md, which):
    ng, sh, sc = (md[NG0:NG0 + 1], md[SH1:SH1 + 1], md[SC1:SC1 + 1]) if which == 0 else (
        md[NG1:NG1 + 1], md[SH2:SH2 + 1], md[SC2:SC2 + 1])
    xhat, r = _rms_parts(x)
    n = xhat * ng
    return n * (1.0 + sc) + sh, (xhat, r, n)


def _normmod_bwd(da, parts, md, which):
    xhat, r, n = parts
    ng, sc = (md[NG0:NG0 + 1], md[SC1:SC1 + 1]) if which == 0 else (md[NG1:NG1 + 1], md[SC2:SC2 + 1])
    dsh = _rowsum(da)
    dsc = _rowsum(da * n)
    dn = da * (1.0 + sc)
    dng = _rowsum(dn * xhat)
    dxhat = dn * ng
    dx = r * (dxhat - xhat * jnp.mean(dxhat * xhat, axis=-1, keepdims=True))
    return dx, dsh, dsc, dng


def _acc_rows(ref, first, rows):
    @pl.when(first)
    def _():
        ref[...] = jnp.zeros(ref.shape, ref.dtype)

    for r, v in rows.items():
        ref[r:r + 1, :] += v


def _shift_up(x, k):
    if k == 0:
        return x
    return pltpu.roll(x, x.shape[0] - k, axis=0)


def _gelu(x):
    k = math.sqrt(2.0 / math.pi)
    return 0.5 * x * (1.0 + jnp.tanh(k * (x + 0.044715 * x * x * x)))


def _gelu_grad(x):
    k = math.sqrt(2.0 / math.pi)
    t = jnp.tanh(k * (x + 0.044715 * x * x * x))
    return 0.5 * (1.0 + t) + 0.5 * x * (1.0 - t * t) * k * (1.0 + 3.0 * 0.044715 * x * x)


def _silu(x):
    return x / (1.0 + jnp.exp(-x))


def _silu_grad(x):
    s = 1.0 / (1.0 + jnp.exp(-x))
    return s * (1.0 + x * (1.0 - s))


def _mlp_fwd(h, mods, w1, w2, layer, *, nct, tm):
    rows, d = h.shape
    groups = mods.shape[0]
    nb, _, fc = w1.shape
    ff = nb * fc

    def body(h_ref, md_ref, w1_ref, w2_ref, h2_ref, u_ref, o_ref):
        x = h_ref[...]
        md = md_ref[...]
        m, _ = _normmod(x, md, 1)
        mb = m.astype(BF16)
        acc = jnp.zeros((tm, d), F32)
        for k in range(nb):
            u = _dot(mb, w1_ref[k])
            u_ref[:, k * fc:(k + 1) * fc] = u.astype(BF16)
            acc = acc + _dot(jnp.square(jnp.maximum(u, 0.0)).astype(BF16), w2_ref[k])
        o_ref[...] = acc.astype(BF16)
        h2_ref[...] = x + md[G2:G2 + 1] * acc

    return pl.pallas_call(
        body, name=f"mlp_fwd_{layer}", grid=(rows // tm,),
        in_specs=[_rows(tm, d), _mods_spec(nct, groups, d), _full(w1.shape), _full(w2.shape)],
        out_specs=[_rows(tm, d), _rows(tm, ff), _rows(tm, d)],
        out_shape=[jax.ShapeDtypeStruct((rows, d), F32), jax.ShapeDtypeStruct((rows, ff), BF16),
                   jax.ShapeDtypeStruct((rows, d), BF16)],
        compiler_params=_params(),
    )(h, mods, w1, w2)


def _mlp_up(h, mods, w1, layer, *, nct, tm):
    rows, d = h.shape
    groups = mods.shape[0]
    nb, _, fc = w1.shape

    def body(h_ref, md_ref, w1_ref, u_ref):
        m, _ = _normmod(h_ref[...], md_ref[...], 1)
        mb = m.astype(BF16)
        for k in range(nb):
            u_ref[:, k * fc:(k + 1) * fc] = _dot(mb, w1_ref[k]).astype(BF16)

    return pl.pallas_call(
        body, name=f"mlp_up_{layer}", grid=(rows // tm,),
        in_specs=[_rows(tm, d), _mods_spec(nct, groups, d), _full(w1.shape)],
        out_specs=_rows(tm, nb * fc), out_shape=jax.ShapeDtypeStruct((rows, nb * fc), BF16),
        compiler_params=_params(),
    )(h, mods, w1)


def _mlp_down(h, u, mods, w2, layer, *, nct, tm):
    rows, d = h.shape
    groups = mods.shape[0]
    nb, fc, _ = w2.shape

    def body(h_ref, u_ref, md_ref, w2_ref, h2_ref, o_ref):
        acc = jnp.zeros((tm, d), F32)
        for k in range(nb):
            uk = u_ref[:, k * fc:(k + 1) * fc].astype(F32)
            acc = acc + _dot(jnp.square(jnp.maximum(uk, 0.0)).astype(BF16), w2_ref[k])
        o_ref[...] = acc.astype(BF16)
        h2_ref[...] = h_ref[...] + md_ref[G2:G2 + 1, :] * acc

    return pl.pallas_call(
        body, name=f"mlp_down_{layer}", grid=(rows // tm,),
        in_specs=[_rows(tm, d), _rows(tm, nb * fc), _mods_spec(nct, groups, d), _full(w2.shape)],
        out_specs=[_rows(tm, d), _rows(tm, d)],
        out_shape=[jax.ShapeDtypeStruct((rows, d), F32), jax.ShapeDtypeStruct((rows, d), BF16)],
        compiler_params=_params(),
    )(h, u, mods, w2)


def _mlp_bwd(h1, dh2, u, o, mods, w1, w2, layer, *, nct, tm):
    rows, d = h1.shape
    groups = mods.shape[0]
    nb, _, fc = w1.shape
    ff = nb * fc

    def body(h_ref, g_ref, u_ref, o_ref, md_ref, w1_ref, w2_ref, dh_ref, du_ref, dob_ref, mb_ref, dmd_ref):
        i = pl.program_id(0)
        x = h_ref[...]
        g = g_ref[...]
        md = md_ref[...]
        m, parts = _normmod(x, md, 1)
        mb_ref[...] = m.astype(BF16)
        dg2 = _rowsum(g * o_ref[...].astype(F32))
        dob = (g * md[G2:G2 + 1]).astype(BF16)
        dob_ref[...] = dob
        dm = jnp.zeros((tm, d), F32)
        for k in range(nb):
            uk = u_ref[:, k * fc:(k + 1) * fc].astype(F32)
            dr = _dot_nt(dob, w2_ref[k])
            duk = (dr * (2.0 * jnp.maximum(uk, 0.0))).astype(BF16)
            du_ref[:, k * fc:(k + 1) * fc] = duk
            dm = dm + _dot_nt(duk, w1_ref[k])
        dx, dsh, dsc, dng = _normmod_bwd(dm, parts, md, 1)
        dh_ref[...] = g + dx
        _acc_rows(dmd_ref, _first_of_group(i, nct, groups), {SH2: dsh, SC2: dsc, G2: dg2, NG1: dng})

    return pl.pallas_call(
        body, name=f"mlp_bwd_{layer}", grid=(rows // tm,),
        in_specs=[_rows(tm, d), _rows(tm, d), _rows(tm, ff), _rows(tm, d), _mods_spec(nct, groups, d),
                  _full(w1.shape), _full(w2.shape)],
        out_specs=[_rows(tm, d), _rows(tm, ff), _rows(tm, d), _rows(tm, d), _mods_spec(nct, groups, d)],
        out_shape=[jax.ShapeDtypeStruct((rows, d), F32), jax.ShapeDtypeStruct((rows, ff), BF16),
                   jax.ShapeDtypeStruct((rows, d), BF16), jax.ShapeDtypeStruct((rows, d), BF16),
                   jax.ShapeDtypeStruct((groups, 8, d), F32)],
        compiler_params=_params(),
    )(h1, dh2, u, o, mods, w1, w2)


def _div_tile(n, cap):
    if n <= cap:
        return n
    return max(t for t in range(LANES, cap + 1, LANES) if n % t == 0)


DW_TOKEN_TILE_CAP = 4224


def _mm_tn(a, b, name, *, relu2=False, col_blocks=1, after=None):
    rows, m = a.shape
    n = b.shape[1]
    tmm = min(m, 1024)
    tn = min(n // col_blocks, 2048)
    per_block = n // col_blocks // tn
    tr = _div_tile(rows, DW_TOKEN_TILE_CAP)
    tokens = [] if after is None else [after]

    def body(a_ref, b_ref, *rest):
        o_ref, acc_ref = rest[len(tokens):]
        r = pl.program_id(2)

        @pl.when(r == 0)
        def _():
            acc_ref[...] = jnp.zeros(acc_ref.shape, F32)

        av = a_ref[...]
        if relu2:
            av = jnp.square(jnp.maximum(av.astype(F32), 0.0)).astype(BF16)
        acc_ref[...] += _dot_tn(av, b_ref[...])

        @pl.when(r == pl.num_programs(2) - 1)
        def _():
            o_ref[...] = acc_ref[...].astype(BF16)

    return pl.pallas_call(
        body, name=name, grid=(m // tmm, n // tn, rows // tr),
        in_specs=[pl.BlockSpec((tr, tmm), lambda i, j, r: (r, i)), pl.BlockSpec((tr, tn), lambda i, j, r: (r, j))]
        + [pl.BlockSpec((8, LANES), lambda i, j, r: (0, 0))] * len(tokens),
        out_specs=pl.BlockSpec((None, tmm, tn), lambda i, j, r: (j // per_block, i, j % per_block)),
        out_shape=jax.ShapeDtypeStruct((col_blocks, m, n // col_blocks), BF16),
        scratch_shapes=[pltpu.VMEM((tmm, tn), F32)],
        compiler_params=_params(),
    )(a, b, *tokens)


def _halo_specs(tm, d, rows):
    per = tm // HALO
    prev = pl.BlockSpec((HALO, d), lambda i: (jnp.maximum(i * per - 1, 0), 0))
    nxt = pl.BlockSpec((HALO, d), lambda i: (jnp.minimum((i + 1) * per, rows // HALO - 1), 0))
    return prev, _rows(tm, d), nxt


def _segment_positions(i, tm, nct, groups, seg_lens):
    if groups == 1:
        start, length = 0, seg_lens[-1]
    else:
        start = jnp.where(i >= nct, nct, 0)
        length = jnp.where(i >= nct, seg_lens[1], seg_lens[0])
    rid = lax.broadcasted_iota(jnp.int32, (tm + 2 * HALO, 1), 0)
    pos = (i - start) * tm - HALO + rid
    return pos, length


def _window_count(pos, length, w):
    hi = jnp.minimum(pos + (w - w // 2), length)
    lo = jnp.maximum(pos - w // 2, 0)
    return (hi - lo).astype(F32)


def _window_sum(xg, w, lead):
    b, k = xg, 1
    while k < w:
        b = b + _shift_up(b, k)
        k *= 2
    return _shift_up(b, HALO - lead)[0:xg.shape[0] - 2 * HALO]


def _pooled(ext, md, pos, length, gw):
    tm = ext.shape[0] - 2 * HALO
    a_ext, parts = _normmod(ext, md, 0)
    valid = jnp.logical_and(pos >= 0, pos < length)
    a_ext = jnp.where(valid, a_ext, 0.0)
    pos_c = pos[HALO:HALO + tm]
    ps = []
    for g, w in enumerate(POOL_WINDOWS):
        xg = a_ext[:, g * gw:(g + 1) * gw]
        s = _window_sum(xg, w, w // 2)
        ps.append(s * (1.0 / _window_count(pos_c, length, w)) - xg[HALO:HALO + tm])
    return ps, parts


def _pool_fwd(h, mods, pw, pscale, layer, *, nct, tm, seg_lens):
    rows, d = h.shape
    groups = mods.shape[0]
    pg, gw = pw.shape[1], pw.shape[-1]

    def body(prev_ref, cur_ref, next_ref, md_ref, pw_ref, ps_ref, out_ref, p_ref):
        i = pl.program_id(0)
        md = md_ref[...]
        cur = cur_ref[...]
        ext = jnp.concatenate([prev_ref[...], cur, next_ref[...]], axis=0)
        pos, length = _segment_positions(i, tm, nct, groups, seg_lens)
        ps, _ = _pooled(ext, md, pos, length, gw)
        for g in range(pg):
            pb = ps[g].astype(BF16)
            p_ref[:, g * gw:(g + 1) * gw] = pb
            yg = _dot(pb, pw_ref[g]) * ps_ref[:, g * gw:(g + 1) * gw]
            out_ref[:, g * gw:(g + 1) * gw] = cur[:, g * gw:(g + 1) * gw] + md[G1:G1 + 1, g * gw:(g + 1) * gw] * yg

    j = layer // N_MIXERS
    return pl.pallas_call(
        body, name=f"pool_fwd_{layer}", grid=(rows // tm,),
        in_specs=[*_halo_specs(tm, d, rows), _mods_spec(nct, groups, d),
                  pl.BlockSpec((None, pg, gw, gw), lambda i: (j, 0, 0, 0)), _full((1, d))],
        out_specs=[_rows(tm, d), _rows(tm, d)],
        out_shape=[jax.ShapeDtypeStruct((rows, d), F32), jax.ShapeDtypeStruct((rows, d), BF16)],
        compiler_params=_params(),
    )(h, h, h, mods, pw, pscale[j:j + 1])


def _pool_bwd_weights(p, dh1, mods, pw, pscale, layer, *, nct, tm):
    rows, d = p.shape
    groups = mods.shape[0]
    pg, gw = pw.shape[1], pw.shape[-1]

    def body(p_ref, g_ref, md_ref, pw_ref, ps_ref, dp_ref, dmd_ref, dps_ref, dpw_ref):
        i = pl.program_id(0)
        md = md_ref[...]
        gup = g_ref[...]

        @pl.when(i == 0)
        def _():
            dps_ref[...] = jnp.zeros(dps_ref.shape, F32)
            dpw_ref[...] = jnp.zeros(dpw_ref.shape, F32)

        dg1 = []
        for g in range(pg):
            cols = slice(g * gw, (g + 1) * gw)
            pb = p_ref[:, cols]
            yp = _dot(pb, pw_ref[g])
            sc = ps_ref[:, cols]
            dg1.append(_rowsum(gup[:, cols] * (yp * sc)))
            dy = gup[:, cols] * md[G1:G1 + 1, cols]
            dps_ref[0:1, cols] += _rowsum(dy * yp)
            dyp = (dy * sc).astype(BF16)
            dp_ref[:, cols] = _dot_nt(dyp, pw_ref[g])
            dpw_ref[g] += _dot_tn(pb, dyp)
        _acc_rows(dmd_ref, _first_of_group(i, nct, groups), {G1: jnp.concatenate(dg1, axis=1)})

    j = layer // N_MIXERS
    return pl.pallas_call(
        body, name=f"pool_bwd_w_{layer}", grid=(rows // tm,),
        in_specs=[_rows(tm, d), _rows(tm, d), _mods_spec(nct, groups, d),
                  pl.BlockSpec((None, pg, gw, gw), lambda i: (j, 0, 0, 0)), _full((1, d))],
        out_specs=[_rows(tm, d), _mods_spec(nct, groups, d), _full((8, d)), _full((pg, gw, gw))],
        out_shape=[jax.ShapeDtypeStruct((rows, d), F32), jax.ShapeDtypeStruct((groups, 8, d), F32),
                   jax.ShapeDtypeStruct((8, d), F32), jax.ShapeDtypeStruct((pg, gw, gw), F32)],
        compiler_params=_params(),
    )(p, dh1, mods, pw, pscale[j:j + 1])


def _pool_bwd_input(dp, h, dh1, mods, layer, *, nct, tm, seg_lens, gw):
    rows, d = h.shape
    groups = mods.shape[0]

    def body(prev_ref, cur_ref, next_ref, h_ref, g_ref, md_ref, dh_ref, dmd_ref):
        i = pl.program_id(0)
        md = md_ref[...]
        dp_cur = cur_ref[...]
        ext = jnp.concatenate([prev_ref[...], dp_cur, next_ref[...]], axis=0)
        pos, length = _segment_positions(i, tm, nct, groups, seg_lens)
        valid = jnp.logical_and(pos >= 0, pos < length)
        das = []
        for g, w in enumerate(POOL_WINDOWS):
            cols = slice(g * gw, (g + 1) * gw)
            q = jnp.where(valid, ext[:, cols] * (1.0 / jnp.maximum(_window_count(pos, length, w), 1.0)), 0.0)
            das.append(_window_sum(q, w, w // 2 - 1) - dp_cur[:, cols])
        da = jnp.concatenate(das, axis=1)
        _, parts = _normmod(h_ref[...], md, 0)
        dx, dsh, dsc, dng = _normmod_bwd(da, parts, md, 0)
        dh_ref[...] = g_ref[...] + dx
        _acc_rows(dmd_ref, _first_of_group(i, nct, groups), {SH1: dsh, SC1: dsc, NG0: dng})

    return pl.pallas_call(
        body, name=f"pool_bwd_x_{layer}", grid=(rows // tm,),
        in_specs=[*_halo_specs(tm, d, rows), _rows(tm, d), _rows(tm, d), _mods_spec(nct, groups, d)],
        out_specs=[pl.BlockSpec((tm, d), lambda i: (jnp.maximum(i - nct, 0), 0)), _mods_spec(nct, groups, d)],
        out_shape=[jax.ShapeDtypeStruct((rows - nct * tm, d), F32), jax.ShapeDtypeStruct((groups, 8, d), F32)],
        compiler_params=_params(),
    )(dp, dp, dp, h, dh1, mods)


def _rope_tables(n_ctx, seq, hd):
    half = hd // 2
    n_rows = seq // GRID_W
    inv = np.float32(ROPE_BASE) ** (-np.arange(0, half, 2, dtype=np.float32) / np.float32(half))
    ar = np.arange(n_rows, dtype=np.float32)[:, None] * inv[None, :]
    ac = np.arange(GRID_W, dtype=np.float32)[:, None] * inv[None, :]

    def over_tokens(row_part, col_part):
        return np.repeat(row_part, GRID_W, axis=0), np.tile(col_part, (n_rows, 1))

    cr, cc = over_tokens(np.cos(ar), np.cos(ac))
    sr, sc = over_tokens(np.sin(ar), np.sin(ac))
    cos = np.concatenate([cr, cr, cc, cc], axis=1)
    sin = np.concatenate([-sr, sr, -sc, sc], axis=1)
    cos = np.concatenate([np.ones((n_ctx, hd), np.float32), cos], axis=0)
    sin = np.concatenate([np.zeros((n_ctx, hd), np.float32), sin], axis=0)
    return jnp.asarray(cos, F32), jnp.asarray(sin, F32)


def _rope_partner(x):
    hd = x.shape[-1]
    q = hd // 4
    lane = lax.broadcasted_iota(jnp.int32, x.shape, 1)
    first = (lane % (2 * q)) < q
    return jnp.where(first, pltpu.roll(x, hd - q, axis=1), pltpu.roll(x, q, axis=1))


def _qkv_fwd(h, mods, wqkv, cos, sin, gains, *, nh, nkv, nct, tm):
    rows, d = h.shape
    qw = wqkv.shape[0] * wqkv.shape[-1]
    hd = cos.shape[-1]

    def body(h_ref, md_ref, w_ref, cos_ref, sin_ref, gn_ref, xa_ref, qkv_ref, q_ref, k_ref, v_ref):
        a, _ = _normmod(h_ref[...], md_ref[...], 0)
        xa = a.astype(BF16)
        xa_ref[...] = xa
        qkv = _dot_blocks(xa, w_ref)
        qkv_ref[...] = qkv
        c, s = cos_ref[...], sin_ref[...]
        for hh in range(nh + nkv):
            xh = qkv[:, hh * hd:(hh + 1) * hd]
            xhat, _ = _rms_parts(xh)
            y = xhat * (gn_ref[0:1, :] if hh < nh else gn_ref[1:2, :])
            rot = (y * c + _rope_partner(y) * s).astype(BF16)
            if hh < nh:
                q_ref[:, hh * hd:(hh + 1) * hd] = rot
            else:
                k_ref[:, (hh - nh) * hd:(hh - nh + 1) * hd] = rot
        v_ref[...] = qkv[:, (nh + nkv) * hd:].astype(BF16)

    return pl.pallas_call(
        body, name="attn_qkv_fwd", grid=(rows // tm,),
        in_specs=[_rows(tm, d), _mods_spec(nct, 2, d), _full(wqkv.shape), _rows(tm, hd), _rows(tm, hd),
                  _full((8, hd))],
        out_specs=[_rows(tm, d), _rows(tm, qw), pl.BlockSpec((tm, nh * hd), lambda i: (jnp.maximum(i - nct, 0), 0)),
                   _rows(tm, nkv * hd), _rows(tm, nkv * hd)],
        out_shape=[jax.ShapeDtypeStruct((rows, d), BF16), jax.ShapeDtypeStruct((rows, qw), F32),
                   jax.ShapeDtypeStruct((rows - nct * tm, nh * hd), BF16),
                   jax.ShapeDtypeStruct((rows, nkv * hd), BF16), jax.ShapeDtypeStruct((rows, nkv * hd), BF16)],
        compiler_params=_params(),
    )(h, mods, wqkv, cos, sin, gains)


ATTN_Q_TILE_CAP = 1024
ATTN_KV_TILE_CAP = 4224
ATTN_ROW_GROUP = 256
LOG2E = 1.4426950408889634


def _attn_tiles(seq, total):
    tq = _div_tile(seq, ATTN_Q_TILE_CAP)
    return tq, _div_tile(total, ATTN_KV_TILE_CAP), min(ATTN_ROW_GROUP, tq)


def _flash_fwd(q, k, v, *, n_ctx, hd):
    total = k.shape[0]
    seq = total - n_ctx
    nkv = k.shape[1] // hd
    tq, tk, rg = _attn_tiles(seq, total)
    nk = total // tk
    scale = hd ** -0.5
    c2 = scale * LOG2E

    def body(q_ref, k_ref, v_ref, o_ref, lse_ref, m_sc, l_sc, acc_sc):
        ki = pl.program_id(2)

        @pl.when(ki == 0)
        def _():
            m_sc[...] = jnp.full(m_sc.shape, -jnp.inf, F32)
            l_sc[...] = jnp.zeros(l_sc.shape, F32)
            acc_sc[...] = jnp.zeros(acc_sc.shape, F32)

        kk, vv = k_ref[...], v_ref[...]
        groups = [(g, sub) for g in range(2) for sub in range(tq // rg)]

        def scores(g, sub):
            return _dot_nt(q_ref[sub * rg:(sub + 1) * rg, g * hd:(g + 1) * hd], kk)

        s_next = scores(*groups[0])
        for idx, (g, sub) in enumerate(groups):
            s = s_next
            if idx + 1 < len(groups):
                s_next = scores(*groups[idx + 1])
            rows = slice(g * tq + sub * rg, g * tq + (sub + 1) * rg)
            m_old = m_sc[rows]
            m_new = jnp.maximum(m_old, jnp.max(s, axis=-1, keepdims=True))
            alpha = jnp.exp2((m_old - m_new) * c2)
            p = jnp.exp2((s - m_new) * c2)
            l_sc[rows] = alpha * l_sc[rows] + jnp.sum(p, axis=-1, keepdims=True)
            acc_sc[rows] = alpha * acc_sc[rows] + _dot(p.astype(BF16), vv)
            m_sc[rows] = m_new

        @pl.when(ki == nk - 1)
        def _():
            o2 = acc_sc[...] / l_sc[...]
            lse = m_sc[...] * scale + jnp.log(l_sc[...])
            o_ref[:, :hd] = o2[:tq].astype(BF16)
            o_ref[:, hd:] = o2[tq:].astype(BF16)
            lse_ref[:, 0:1] = lse[:tq]
            lse_ref[:, 1:2] = lse[tq:]

    return pl.pallas_call(
        body, name="attn_flash_fwd", grid=(nkv, seq // tq, nk),
        in_specs=[pl.BlockSpec((tq, 2 * hd), lambda h, i, j: (i, h)),
                  pl.BlockSpec((tk, hd), lambda h, i, j: (j, h)),
                  pl.BlockSpec((tk, hd), lambda h, i, j: (j, h))],
        out_specs=[pl.BlockSpec((tq, 2 * hd), lambda h, i, j: (i, h)),
                   pl.BlockSpec((None, tq, 2), lambda h, i, j: (h, i, 0))],
        out_shape=[jax.ShapeDtypeStruct((seq, 2 * nkv * hd), BF16), jax.ShapeDtypeStruct((nkv, seq, 2), F32)],
        scratch_shapes=[pltpu.VMEM((2 * tq, 1), F32), pltpu.VMEM((2 * tq, 1), F32), pltpu.VMEM((2 * tq, hd), F32)],
        compiler_params=_params(),
    )(q, k, v)


def _flash_bwd(q, k, v, o, do, lse, *, n_ctx, hd):
    total = k.shape[0]
    seq = total - n_ctx
    nkv = k.shape[1] // hd
    tq, tk, rg = _attn_tiles(seq, total)
    scale = hd ** -0.5
    c2 = scale * LOG2E

    def body(q_ref, k_ref, v_ref, o_ref, do_ref, lse_ref, dq_ref, dk_ref, dv_ref):
        ki, qi = pl.program_id(1), pl.program_id(2)
        kk, vv = k_ref[...], v_ref[...]

        @pl.when(qi == 0)
        def _():
            dk_ref[...] = jnp.zeros(dk_ref.shape, F32)
            dv_ref[...] = jnp.zeros(dv_ref.shape, F32)

        dk_acc = jnp.zeros((tk, hd), F32)
        dv_acc = jnp.zeros((tk, hd), F32)
        for g in range(2):
            for sub in range(tq // rg):
                rs = slice(sub * rg, (sub + 1) * rg)
                cs = slice(g * hd, (g + 1) * hd)
                qq = q_ref[rs, cs]
                dd = do_ref[rs, cs]
                delta = jnp.sum(dd.astype(F32) * o_ref[rs, cs].astype(F32), axis=-1, keepdims=True)
                p = jnp.exp2(_dot_nt(qq, kk) * c2 - lse_ref[rs, g:g + 1] * LOG2E)
                dp = _dot_nt(dd, vv)
                ds = (p * (dp - delta) * scale).astype(BF16)
                dv_acc = dv_acc + _dot_tn(p.astype(BF16), dd)
                dk_acc = dk_acc + _dot_tn(ds, qq)
                dq = _dot(ds, kk)
                rows = pl.ds(pl.multiple_of(qi * tq, tq) + sub * rg, rg)

                @pl.when(ki == 0)
                def _():
                    dq_ref[rows, cs] = dq

                @pl.when(ki > 0)
                def _():
                    dq_ref[rows, cs] += dq
        dk_ref[...] += dk_acc
        dv_ref[...] += dv_acc

    return pl.pallas_call(
        body, name="attn_flash_bwd", grid=(nkv, total // tk, seq // tq),
        in_specs=[pl.BlockSpec((tq, 2 * hd), lambda h, j, i: (i, h)),
                  pl.BlockSpec((tk, hd), lambda h, j, i: (j, h)),
                  pl.BlockSpec((tk, hd), lambda h, j, i: (j, h)),
                  pl.BlockSpec((tq, 2 * hd), lambda h, j, i: (i, h)),
                  pl.BlockSpec((tq, 2 * hd), lambda h, j, i: (i, h)),
                  pl.BlockSpec((None, tq, 2), lambda h, j, i: (h, i, 0))],
        out_specs=[pl.BlockSpec((seq, 2 * hd), lambda h, j, i: (0, h)),
                   pl.BlockSpec((tk, hd), lambda h, j, i: (j, h)),
                   pl.BlockSpec((tk, hd), lambda h, j, i: (j, h))],
        out_shape=[jax.ShapeDtypeStruct((seq, 2 * nkv * hd), F32), jax.ShapeDtypeStruct((total, nkv * hd), F32),
                   jax.ShapeDtypeStruct((total, nkv * hd), F32)],
        compiler_params=_params(),
    )(q, k, v, o, do, lse)


def _proj_fwd(o, wo, hc, mods, *, n_ctx, tm):
    seq, d = o.shape
    off = n_ctx // tm

    def body(o_ref, w_ref, h_ref, md_ref, h1_ref, y_ref):
        y = _dot(o_ref[...], w_ref[...])
        y_ref[...] = y.astype(BF16)
        h1_ref[...] = h_ref[...] + md_ref[G1:G1 + 1, :] * y

    return pl.pallas_call(
        body, name="attn_proj_fwd", grid=(seq // tm,),
        in_specs=[_rows(tm, d), _full((d, d)),
                  pl.BlockSpec((tm, d), lambda i: (i + off, 0)), pl.BlockSpec((None, 8, d), lambda i: (1, 0, 0))],
        out_specs=[_rows(tm, d), _rows(tm, d)],
        out_shape=[jax.ShapeDtypeStruct((seq, d), F32), jax.ShapeDtypeStruct((seq, d), BF16)],
        compiler_params=_params(),
    )(o, wo, hc, mods)


def _proj_bwd(dh1, y, mods, wo, *, tm):
    seq, d = dh1.shape

    def body(g_ref, y_ref, md_ref, w_ref, do_ref, dyb_ref, dmd_ref):
        i = pl.program_id(0)
        g = g_ref[...]
        dyb = (g * md_ref[G1:G1 + 1, :]).astype(BF16)
        dyb_ref[...] = dyb
        do_ref[...] = _dot_nt(dyb, w_ref[...]).astype(BF16)
        _acc_rows(dmd_ref, i == 0, {G1: _rowsum(g * y_ref[...].astype(F32))})

    return pl.pallas_call(
        body, name="attn_proj_bwd", grid=(seq // tm,),
        in_specs=[_rows(tm, d), _rows(tm, d), pl.BlockSpec((None, 8, d), lambda i: (1, 0, 0)), _full((d, d))],
        out_specs=[_rows(tm, d), _rows(tm, d), pl.BlockSpec((None, 8, d), lambda i: (0, 0, 0))],
        out_shape=[jax.ShapeDtypeStruct((seq, d), BF16), jax.ShapeDtypeStruct((seq, d), BF16),
                   jax.ShapeDtypeStruct((1, 8, d), F32)],
        compiler_params=_params(),
    )(dh1, y, mods, wo)


def _qkv_bwd(qkv, dq, dk, dv, cos, sin, gains, *, nh, nkv, nct, tm):
    rows, qw = qkv.shape
    hd = cos.shape[-1]

    def body(qkv_ref, dq_ref, dk_ref, dv_ref, cos_ref, sin_ref, gn_ref, out_ref, dgn_ref):
        i = pl.program_id(0)
        c, s = cos_ref[...], sin_ref[...]
        is_lat = (i >= nct).astype(F32)
        dqg = jnp.zeros((1, hd), F32)
        dkg = jnp.zeros((1, hd), F32)
        for hh in range(nh + nkv):
            if hh < nh:
                dr = dq_ref[:, hh * hd:(hh + 1) * hd] * is_lat
                gn = gn_ref[0:1, :]
            else:
                dr = dk_ref[:, (hh - nh) * hd:(hh - nh + 1) * hd]
                gn = gn_ref[1:2, :]
            dy = dr * c + _rope_partner(dr * s)
            xhat, r = _rms_parts(qkv_ref[:, hh * hd:(hh + 1) * hd])
            dgh = _rowsum(dy * xhat)
            if hh < nh:
                dqg = dqg + dgh
            else:
                dkg = dkg + dgh
            dxhat = dy * gn
            dx = r * (dxhat - xhat * jnp.mean(dxhat * xhat, axis=-1, keepdims=True))
            out_ref[:, hh * hd:(hh + 1) * hd] = dx.astype(BF16)
        out_ref[:, (nh + nkv) * hd:] = dv_ref[...].astype(BF16)
        _acc_rows(dgn_ref, i == 0, {0: dqg, 1: dkg})

    return pl.pallas_call(
        body, name="attn_qkv_bwd", grid=(rows // tm,),
        in_specs=[_rows(tm, qw), pl.BlockSpec((tm, nh * hd), lambda i: (jnp.maximum(i - nct, 0), 0)),
                  _rows(tm, nkv * hd), _rows(tm, nkv * hd), _rows(tm, hd), _rows(tm, hd), _full((8, hd))],
        out_specs=[_rows(tm, qw), _full((8, hd))],
        out_shape=[jax.ShapeDtypeStruct((rows, qw), BF16), jax.ShapeDtypeStruct((8, hd), F32)],
        compiler_params=_params(),
    )(qkv, dq, dk, dv, cos, sin, gains)


def _attn_in_bwd(dqkv, wqkv, hc, dh1, mods, *, nct, tm):
    rows, d = hc.shape
    qw = dqkv.shape[1]

    def body(dz_ref, w_ref, h_ref, g_ref, md_ref, dh_ref, dmd_ref):
        i = pl.program_id(0)
        md = md_ref[...]
        da = _dot_nt_blocks(dz_ref[...], w_ref)
        _, parts = _normmod(h_ref[...], md, 0)
        dx, dsh, dsc, dng = _normmod_bwd(da, parts, md, 0)
        dh_ref[...] = g_ref[...] * (i >= nct).astype(F32) + dx
        _acc_rows(dmd_ref, _first_of_group(i, nct, 2), {SH1: dsh, SC1: dsc, NG0: dng})

    return pl.pallas_call(
        body, name="attn_in_bwd", grid=(rows // tm,),
        in_specs=[_rows(tm, qw), _full(wqkv.shape), _rows(tm, d),
                  pl.BlockSpec((tm, d), lambda i: (jnp.maximum(i - nct, 0), 0)), _mods_spec(nct, 2, d)],
        out_specs=[_rows(tm, d), _mods_spec(nct, 2, d)],
        out_shape=[jax.ShapeDtypeStruct((rows, d), F32), jax.ShapeDtypeStruct((2, 8, d), F32)],
        compiler_params=_params(),
    )(dqkv, wqkv, hc, dh1, mods)


def _gmlp_gate(z, lng, lnb, ws_ref, bs_ref, gg, ch):
    half = z.shape[1] // 2
    ggw = half // gg
    u, v = z[:, :half], z[:, half:]
    vc = v - jnp.mean(v, axis=-1, keepdims=True)
    rs = lax.rsqrt(jnp.mean(vc * vc, axis=-1, keepdims=True) + EPS)
    vhat = vc * rs
    vln = (vhat * lng + lnb).astype(BF16)
    chunks = []
    for n in range(z.shape[0] // ch):
        groups = []
        for g in range(gg):
            groups.append(_dot(ws_ref[g], vln[n * ch:(n + 1) * ch, g * ggw:(g + 1) * ggw]) + bs_ref[g])
        chunks.append(jnp.concatenate(groups, axis=1))
    sv = jnp.concatenate(chunks, axis=0) if len(chunks) > 1 else chunks[0]
    return u, sv, vhat, rs, vln


def _gmlp_fwd(h, mods, w_in, lng, lnb, ws, bs, w_out, *, tm):
    seq, d = h.shape
    zw = w_in.shape[0] * w_in.shape[-1]
    half = zw // 2
    gg, ch = ws.shape[0], ws.shape[-1]

    def body(h_ref, md_ref, win_ref, lng_ref, lnb_ref, ws_ref, bs_ref, wout_ref, h1_ref, zp_ref, y_ref):
        x = h_ref[...]
        md = md_ref[...]
        a, _ = _normmod(x, md, 0)
        zp = _dot_blocks(a.astype(BF16), win_ref)
        zp_ref[...] = zp.astype(BF16)
        u, sv, _, _, _ = _gmlp_gate(_gelu(zp), lng_ref[...], lnb_ref[...], ws_ref, bs_ref, gg, ch)
        y = _dot((u * sv).astype(BF16), wout_ref[...])
        y_ref[...] = y.astype(BF16)
        h1_ref[...] = x + md[G1:G1 + 1] * y

    return pl.pallas_call(
        body, name="gmlp_fwd", grid=(seq // tm,),
        in_specs=[_rows(tm, d), pl.BlockSpec((None, 8, d), lambda i: (1, 0, 0)),
                  _full(w_in.shape), _full((1, half)), _full((1, half)),
                  _full((gg, ch, ch)), _full((gg, ch, 1)), _full((half, d))],
        out_specs=[_rows(tm, d), _rows(tm, zw), _rows(tm, d)],
        out_shape=[jax.ShapeDtypeStruct((seq, d), F32), jax.ShapeDtypeStruct((seq, zw), BF16),
                   jax.ShapeDtypeStruct((seq, d), BF16)],
        compiler_params=_params(),
    )(h, mods, w_in, lng, lnb, ws, bs, w_out)


def _gmlp_bwd(h, dh1, zpre, y, mods, w_in, lng, lnb, ws, ws_t, bs, w_out, *, tm):
    seq, d = h.shape
    zw = w_in.shape[0] * w_in.shape[-1]
    half = zw // 2
    gg, ch = ws.shape[0], ws.shape[-1]
    ggw = half // gg

    def body(h_ref, g_ref, zp_ref, y_ref, md_ref, win_ref, lng_ref, lnb_ref, ws_ref, wst_ref, bs_ref, wout_ref,
             dh_ref, dzp_ref, gated_ref, dyb_ref, ab_ref, dmd_ref, dln_ref, dws_ref, dbs_ref):
        i = pl.program_id(0)
        x = h_ref[...]
        md = md_ref[...]
        a, parts = _normmod(x, md, 0)
        ab_ref[...] = a.astype(BF16)
        zp = zp_ref[...].astype(F32)
        lng_v = lng_ref[...]
        u, sv, vhat, rs, vln = _gmlp_gate(_gelu(zp), lng_v, lnb_ref[...], ws_ref, bs_ref, gg, ch)
        g = g_ref[...]
        dg1 = _rowsum(g * y_ref[...].astype(F32))
        dyb = (g * md[G1:G1 + 1]).astype(BF16)
        dyb_ref[...] = dyb
        gated_ref[...] = (u * sv).astype(BF16)
        dgated = _dot_nt(dyb, wout_ref[...])
        du = dgated * sv
        dsv = dgated * u

        @pl.when(i == 0)
        def _():
            dws_ref[...] = jnp.zeros(dws_ref.shape, F32)
            dbs_ref[...] = jnp.zeros(dbs_ref.shape, F32)
            dln_ref[...] = jnp.zeros(dln_ref.shape, F32)

        chunks = []
        for n in range(tm // ch):
            groups = []
            for gi in range(gg):
                blk = dsv[n * ch:(n + 1) * ch, gi * ggw:(gi + 1) * ggw]
                dbs_ref[gi] += jnp.sum(blk, axis=-1, keepdims=True)
                blk_b = blk.astype(BF16)
                dws_ref[gi] += _dot_nt(blk_b, vln[n * ch:(n + 1) * ch, gi * ggw:(gi + 1) * ggw])
                groups.append(_dot(wst_ref[gi], blk_b))
            chunks.append(jnp.concatenate(groups, axis=1))
        dvln = jnp.concatenate(chunks, axis=0) if len(chunks) > 1 else chunks[0]
        dln_ref[0:1, :] += _rowsum(dvln * vhat)
        dln_ref[1:2, :] += _rowsum(dvln)
        dvhat = dvln * lng_v
        dv = rs * (dvhat - jnp.mean(dvhat, axis=-1, keepdims=True)
                   - vhat * jnp.mean(dvhat * vhat, axis=-1, keepdims=True))
        dzp = (jnp.concatenate([du, dv], axis=1) * _gelu_grad(zp)).astype(BF16)
        dzp_ref[...] = dzp
        da = _dot_nt_blocks(dzp, win_ref)
        dx, dsh, dsc, dng = _normmod_bwd(da, parts, md, 0)
        dh_ref[...] = g + dx
        _acc_rows(dmd_ref, i == 0, {SH1: dsh, SC1: dsc, G1: dg1, NG0: dng})

    return pl.pallas_call(
        body, name="gmlp_bwd", grid=(seq // tm,),
        in_specs=[_rows(tm, d), _rows(tm, d), _rows(tm, zw), _rows(tm, d),
                  pl.BlockSpec((None, 8, d), lambda i: (1, 0, 0)),
                  _full(w_in.shape), _full((1, half)), _full((1, half)),
                  _full((gg, ch, ch)), _full((gg, ch, ch)), _full((gg, ch, 1)), _full((half, d))],
        out_specs=[_rows(tm, d), _rows(tm, zw), _rows(tm, half), _rows(tm, d), _rows(tm, d),
                   pl.BlockSpec((None, 8, d), lambda i: (0, 0, 0)), _full((8, half)), _full((gg, ch, ch)),
                   _full((gg, ch, 1))],
        out_shape=[jax.ShapeDtypeStruct((seq, d), F32), jax.ShapeDtypeStruct((seq, zw), BF16),
                   jax.ShapeDtypeStruct((seq, half), BF16), jax.ShapeDtypeStruct((seq, d), BF16),
                   jax.ShapeDtypeStruct((seq, d), BF16), jax.ShapeDtypeStruct((1, 8, d), F32),
                   jax.ShapeDtypeStruct((8, half), F32), jax.ShapeDtypeStruct((gg, ch, ch), F32),
                   jax.ShapeDtypeStruct((gg, ch, 1), F32)],
        compiler_params=_params(),
    )(h, dh1, zpre, y, mods, w_in, lng, lnb, ws, ws_t, bs, w_out)


def _final_loss(h, tgt, fg, *, tm):
    seq, d = h.shape

    def body(h_ref, t_ref, g_ref, dh_ref, acc_ref):
        i = pl.program_id(0)
        gain = g_ref[...]
        xhat, r = _rms_parts(h_ref[...])
        err = xhat * gain - t_ref[...]
        dy = err * (1.0 / d)
        dxhat = dy * gain
        dh_ref[...] = r * (dxhat - xhat * jnp.mean(dxhat * xhat, axis=-1, keepdims=True))
        part = jnp.sum(_rowsum(err * err), axis=-1, keepdims=True) * (0.5 / d)
        _acc_rows(acc_ref, i == 0, {0: _rowsum(dy * xhat), 1: jnp.broadcast_to(part, (1, d))})

    return pl.pallas_call(
        body, name="final_loss", grid=(seq // tm,),
        in_specs=[_rows(tm, d), _rows(tm, d), _full((1, d))],
        out_specs=[_rows(tm, d), _full((8, d))],
        out_shape=[jax.ShapeDtypeStruct((seq, d), F32), jax.ShapeDtypeStruct((8, d), F32)],
        compiler_params=_params(),
    )(h, tgt, fg)


def _ada_fwd(c_all, ada_w, ada_b_cols):
    depth, d, ncs = ada_w.shape

    def body(c_ref, w_ref, b_ref, o_ref):
        s = _silu(c_ref[...]).astype(BF16)
        o_ref[...] = _dot(s, w_ref[...].astype(BF16)) + b_ref[...]

    return pl.pallas_call(
        body, name="ada_fwd", grid=(depth,),
        in_specs=[_full((16, d)), pl.BlockSpec((None, d, ncs), lambda i: (i, 0, 0)),
                  pl.BlockSpec((None, 1, ncs), lambda i: (i, 0, 0))],
        out_specs=pl.BlockSpec((None, 16, ncs), lambda i: (i, 0, 0)),
        out_shape=jax.ShapeDtypeStruct((depth, 16, ncs), F32),
        compiler_params=_params(),
    )(c_all, ada_w, ada_b_cols.reshape(depth, 1, ncs))


def _ada_bwd(c_all, c_all_t, dmod, ada_w):
    depth, d, ncs = ada_w.shape

    def body(c_ref, ct_ref, dm_ref, w_ref, gw_ref, dc_ref):
        i = pl.program_id(0)
        dm = dm_ref[...]
        dctx = _rowsum(dm[8:16])
        rid = lax.broadcasted_iota(jnp.int32, (8, ncs), 0)
        low = jnp.where(rid == 0, jnp.broadcast_to(dctx, (8, ncs)), 0.0)
        dm16 = jnp.concatenate([dm[0:8], low], axis=0).astype(BF16)
        gw_ref[...] = _dot(_silu(ct_ref[...]).astype(BF16), dm16)

        @pl.when(i == 0)
        def _():
            dc_ref[...] = jnp.zeros(dc_ref.shape, F32)

        dc_ref[...] += _dot_nt(low.astype(BF16), w_ref[...].astype(BF16)) * _silu_grad(c_ref[8:9, :])

    return pl.pallas_call(
        body, name="ada_bwd", grid=(depth,),
        in_specs=[_full((16, d)), _full((d, 16)), pl.BlockSpec((None, 16, ncs), lambda i: (i, 0, 0)),
                  pl.BlockSpec((None, d, ncs), lambda i: (i, 0, 0))],
        out_specs=[pl.BlockSpec((None, d, ncs), lambda i: (i, 0, 0)), _full((8, d))],
        out_shape=[jax.ShapeDtypeStruct((depth, d, ncs), F32), jax.ShapeDtypeStruct((8, d), F32)],
        compiler_params=_params(),
    )(c_all, c_all_t, dmod, ada_w)


def _adamw_math(w, g, m, v):
    m = ADAM_B1 * m + (1.0 - ADAM_B1) * g
    v = ADAM_B2 * v + (1.0 - ADAM_B2) * jnp.square(g)
    m_hat = m * (1.0 / (1.0 - ADAM_B1 ** ADAM_STEP))
    v_hat = v * (1.0 / (1.0 - ADAM_B2 ** ADAM_STEP))
    delta = -ADAM_LR * (m_hat / (jnp.sqrt(v_hat) + ADAM_EPS) + ADAM_WD * w)
    return delta, m, v


def _adamw(ga, gb, w, m, v, name):
    rows, cols = w.shape
    tr = rows
    while tr * cols * 4 > (1 << 20) and tr % 16 == 0:
        tr //= 2
    grads = [ga] if gb is None else [ga, gb]

    def body(*refs):
        w_ref, m_ref, v_ref, g_out, d_out, m_out, v_out = refs[len(grads):]
        g = refs[0][...] if gb is None else refs[0][...] + refs[1][...]
        delta, m_new, v_new = _adamw_math(w_ref[...], g, m_ref[...], v_ref[...])
        g_out[...] = g
        d_out[...] = delta
        m_out[...] = m_new
        v_out[...] = v_new

    spec = _rows(tr, cols)
    return pl.pallas_call(
        body, name=name, grid=(rows // tr,),
        in_specs=[spec] * (len(grads) + 3), out_specs=[spec] * 4,
        out_shape=[jax.ShapeDtypeStruct((rows, cols), F32)] * 4,
        compiler_params=_params(),
    )(*grads, w, m, v)


def _sum_devices(gathered, name):
    n, rows, cols = gathered.shape
    tr = rows
    while tr * cols * 4 * n > (4 << 20) and tr % 16 == 0:
        tr //= 2

    def body(x_ref, o_ref):
        acc = x_ref[0]
        for j in range(1, n):
            acc = acc + x_ref[j]
        o_ref[...] = acc

    return pl.pallas_call(
        body, name=name, grid=(rows // tr,),
        in_specs=[pl.BlockSpec((n, tr, cols), lambda i: (0, i, 0))], out_specs=_rows(tr, cols),
        out_shape=jax.ShapeDtypeStruct((rows, cols), F32),
        compiler_params=_params(),
    )(gathered)


def _sum_partials(blocked, landeds, chip, name):
    n = len(blocked)
    cols = blocked[0].shape[-1]
    blocked = [b.reshape(N_CHIPS, -1, cols) for b in blocked]
    landeds = [l.reshape(3, -1, cols) for l in landeds]
    rows = blocked[0].shape[1]
    tr = rows
    while tr * cols * 2 * n > (1 << 20) and tr % 32 == 0:
        tr //= 2

    def body(chip_ref, *refs):
        out_ref = refs[-1]
        for li in range(n):
            acc = refs[li][...].astype(F32)
            for p in range(3):
                acc = acc + refs[n + li][p].astype(F32)
            out_ref[li] = acc

    out = pl.pallas_call(
        body, name=name,
        grid_spec=pltpu.PrefetchScalarGridSpec(
            num_scalar_prefetch=1, grid=(rows // tr,),
            in_specs=[pl.BlockSpec((None, tr, cols), lambda i, k: (k[0], i, 0))] * n
            + [pl.BlockSpec((3, tr, cols), lambda i, k: (0, i, 0))] * n,
            out_specs=pl.BlockSpec((n, tr, cols), lambda i, k: (0, i, 0))),
        out_shape=jax.ShapeDtypeStruct((n, rows, cols), F32),
        compiler_params=_params(),
    )(jnp.reshape(chip, (1,)).astype(jnp.int32), *blocked, *landeds)
    return out.reshape(n * rows, cols)


def _my_place():
    return lax.axis_index("x"), lax.axis_index("y"), lax.axis_index("c")


def _other_chips(x, y):
    return [(1 - x, y), (x, 1 - y), (1 - x, 1 - y)]


def _all_gather_small(block, name):
    rows, cols = block.shape

    def body(x_ref, out_ref, send_sems, recv_sems, local_sem):
        x, y, c = _my_place()
        me, sibling = (x, y, c), (x, y, 1 - c)
        chips = _other_chips(x, y)

        def slot(px, py, pc):
            return out_ref.at[4 * px + 2 * py + pc]

        def copy(k, blk, to, src=None):
            return pltpu.make_async_remote_copy(
                src_ref=slot(*blk) if src is None else src, dst_ref=slot(*blk),
                send_sem=send_sems.at[k], recv_sem=recv_sems.at[k], device_id=to, device_id_type=MESH)

        mine = pltpu.make_async_copy(x_ref, slot(*me), local_sem)
        mine.start()
        first = [copy(0, me, sibling, src=x_ref)]
        first += [copy(1 + j, me, (*chip, c), src=x_ref) for j, chip in enumerate(chips)]
        for cp in first:
            cp.start()
        passed = [copy(4 + j, (*chip, c), sibling) for j, chip in enumerate(chips)]
        for j, chip in enumerate(chips):
            copy(1 + j, (*chip, c), me).wait_recv()
            passed[j].start()
        copy(0, sibling, me).wait_recv()
        for j, chip in enumerate(chips):
            copy(4 + j, (*chip, 1 - c), me).wait_recv()
        for cp in first + passed:
            cp.wait_send()
        mine.wait()

    return pl.pallas_call(
        body, name=name,
        out_shape=jax.ShapeDtypeStruct((N_DEV, rows, cols), block.dtype),
        in_specs=[pl.BlockSpec(memory_space=pltpu.VMEM)],
        out_specs=pl.BlockSpec(memory_space=pltpu.VMEM),
        scratch_shapes=[pltpu.SemaphoreType.DMA((7,)), pltpu.SemaphoreType.DMA((7,)), pltpu.SemaphoreType.DMA],
        compiler_params=_params(),
    )(block)


HBM_SPEC = pl.BlockSpec(memory_space=pltpu.HBM)
SEM_SPEC = pl.BlockSpec(memory_space=pltpu.SEMAPHORE)
DATAFLOW_EFFECT = pltpu.SideEffectType.DATAFLOW_SIDE_EFFECTING


def _same_core_of_other_chips(x, y, c):
    return [(*chip, c) for chip in _other_chips(x, y)]


def _sibling_core(x, y, c):
    return [(x, y, 1 - c)]


def _gather_views(src, land, p, x, y):
    return src, land.at[2 * x + y]


def _scatter_views(src, land, p, x, y):
    peer_chip = (2 * (1 - x) + y, 2 * x + (1 - y), 2 * (1 - x) + (1 - y))[p]
    return src.at[peer_chip], land.at[p]


def _whole_views(src, land, p, x, y):
    return src, land


GATHER_PLAN = (_same_core_of_other_chips, _gather_views, 3)
SCATTER_PLAN = (_same_core_of_other_chips, _scatter_views, 3)
SIBLING_PLAN = (_sibling_core, _whole_views, 1)


def _exchange_copies(srcs, lands, send_sems, recv_sems, plan):
    peers_of, views, n_peers = plan
    x, y, c = _my_place()
    copies = []
    for j, (src, land) in enumerate(zip(srcs, lands)):
        for p, peer in enumerate(peers_of(x, y, c)):
            s_view, d_view = views(src, land, p, x, y)
            k = n_peers * j + p
            copies.append(pltpu.make_async_remote_copy(
                src_ref=s_view, dst_ref=d_view, send_sem=send_sems.at[k], recv_sem=recv_sems.at[k],
                device_id=peer, device_id_type=MESH))
    return copies


def _exchange_start(srcs, lands, plan, name):
    n = len(srcs)

    def body(*refs):
        send_sems, recv_sems = refs[2 * n], refs[2 * n + 1]
        token = refs[-1]
        for cp in _exchange_copies(refs[:n], refs[n:2 * n], send_sems, recv_sems, plan):
            cp.start()
        token[...] = jnp.zeros(token.shape, token.dtype)

    operands = [pltpu.with_memory_space_constraint(a, pltpu.HBM) for a in (*srcs, *lands)]
    out = pl.pallas_call(
        body, name=name,
        out_shape=(pltpu.SemaphoreType.DMA((plan[2] * n,)), pltpu.SemaphoreType.DMA((plan[2] * n,)),
                   *[pltpu.HBM(a.shape, a.dtype) for a in operands], jax.ShapeDtypeStruct((8, LANES), F32)),
        in_specs=[HBM_SPEC] * (2 * n),
        out_specs=(SEM_SPEC, SEM_SPEC, *[HBM_SPEC] * (2 * n), pl.BlockSpec(memory_space=pltpu.VMEM)),
        input_output_aliases={i: 2 + i for i in range(2 * n)},
        compiler_params=pltpu.CompilerParams(has_side_effects=DATAFLOW_EFFECT),
    )(*operands)
    return out[0], out[1], list(out[2:2 + n]), list(out[2 + n:2 + 2 * n]), out[-1]


def _exchange_wait(send_sems, recv_sems, srcs, lands, plan, after, name):
    n = len(srcs)

    def body(*refs):
        send, recv = refs[2 * n], refs[2 * n + 1]
        for cp in _exchange_copies(refs[:n], refs[n:2 * n], send, recv, plan):
            cp.wait_send()
            cp.wait_recv()

    out = pl.pallas_call(
        body, name=name,
        out_shape=tuple(pltpu.HBM(a.shape, a.dtype) for a in (*srcs, *lands)),
        in_specs=[HBM_SPEC] * (2 * n) + [SEM_SPEC, SEM_SPEC, HBM_SPEC],
        out_specs=tuple([HBM_SPEC] * (2 * n)),
        input_output_aliases={i: i for i in range(2 * n)},
        compiler_params=pltpu.CompilerParams(has_side_effects=DATAFLOW_EFFECT),
    )(*srcs, *lands, send_sems, recv_sems, pltpu.with_memory_space_constraint(after, pltpu.HBM))
    return list(out[:n]), list(out[n:])


def _landing_for_gather(shard, chip):
    land = lax.empty((N_CHIPS, *shard.shape), shard.dtype)
    return lax.dynamic_update_index_in_dim(land, shard, chip, 0)


TILE_ELEMS = SUBLANES * LANES


def _pack(arrays):
    parts = []
    for a in arrays:
        flat = a.reshape(-1).astype(F32)
        pad = (-flat.shape[0]) % TILE_ELEMS
        if pad:
            flat = jnp.concatenate([flat, jnp.zeros((pad,), F32)])
        parts.append(flat.reshape(-1, LANES))
    return jnp.concatenate(parts, axis=0) if len(parts) > 1 else parts[0]


def _unpack(buf, shapes):
    out, r = [], 0
    lead = buf.shape[:-2]
    for shp in shapes:
        size = math.prod(shp)
        nr = -(-size // TILE_ELEMS) * SUBLANES
        flat = buf[..., r:r + nr, :].reshape(*lead, nr * LANES)[..., :size]
        out.append(flat.reshape(*lead, *shp))
        r += nr
    return out


def _chip_cols(a, k, width):
    return lax.dynamic_slice_in_dim(a, k * width, width, axis=a.ndim - 1)


def _across_chips(gathered, c0_only_shape):
    return gathered.reshape(2, 2, 2, *c0_only_shape)[:, :, 0].reshape(N_CHIPS, *c0_only_shape)


def kernel(x, c, ctx, c_ctx, ada_w, ada_b, norm_g, mlp_w1, mlp_w2, pool_w, pool_scale, attn_w_qkv, attn_w_o, attn_q_g, attn_k_g, gm_w_in, gm_ln_g, gm_ln_b, gm_ws, gm_bs, gm_w_out, final_g, loss_target, m_c_ctx, m_ada_w, m_ada_b, m_norm_g, m_mlp_w1, m_mlp_w2, m_pool_w, m_pool_scale, m_attn_w_qkv, m_attn_w_o, m_attn_q_g, m_attn_k_g, m_gm_w_in, m_gm_ln_g, m_gm_ln_b, m_gm_ws, m_gm_bs, m_gm_w_out, m_final_g, v_c_ctx, v_ada_w, v_ada_b, v_norm_g, v_mlp_w1, v_mlp_w2, v_pool_w, v_pool_scale, v_attn_w_qkv, v_attn_w_o, v_attn_q_g, v_attn_k_g, v_gm_w_in, v_gm_ln_g, v_gm_ln_b, v_gm_ws, v_gm_bs, v_gm_w_out, v_final_g):
    seq, d = x.shape[1], x.shape[2]
    n_ctx = ctx.shape[1]
    total = n_ctx + seq
    hd = attn_q_g.shape[-1]
    nh = d // hd
    nkv = nh // 2
    gg, ch = gm_ws.shape[1], gm_ws.shape[-1]
    half = gm_w_out.shape[1] * N_CHIPS
    pgw = pool_w.shape[-1]
    tm = min(256, n_ctx)
    tm_lat = min(2 * tm, seq)
    tm_stream = min(4 * tm, seq)
    nct = n_ctx // tm
    seg_lens = (n_ctx, seq)

    mx, my, mc = _my_place()
    chip = 2 * mx + my
    me = 4 * mx + 2 * my + mc

    c_rows = jnp.concatenate([c, jnp.zeros((7, d), F32)], axis=0)
    c_gath = _all_gather_small(c_rows, "gather_cond")[:, 0, :]
    c_all = jnp.concatenate([c_gath, c_ctx[None, :], jnp.zeros((7, d), F32)], axis=0)
    ncs = ada_w.shape[-1]
    ada_cols = _ada_fwd(c_all, ada_w, _chip_cols(ada_b, chip, ncs))
    small_shapes = [ada_cols.shape, norm_g.shape, pool_scale.shape, gm_ln_g.shape, gm_ln_b.shape]
    gathered = _all_gather_small(_pack([ada_cols, norm_g, pool_scale, gm_ln_g, gm_ln_b]), "gather_small_params")
    per_chip = _across_chips(gathered, gathered.shape[1:])
    ada_g, ng_g, ps_g, lng_g, lnb_g = _unpack(per_chip, small_shapes)

    def join_last(a):
        return jnp.moveaxis(a, 0, -2).reshape(*a.shape[1:-1], N_CHIPS * a.shape[-1])

    ada_full = join_last(ada_g)
    ng_full = join_last(ng_g)
    ps_full = join_last(ps_g)
    lng_full = join_last(lng_g)
    lnb_full = join_last(lnb_g)
    mod_lat = lax.dynamic_slice_in_dim(ada_full, me, 1, axis=1).reshape(DEPTH, 6, d)
    mod_ctx = ada_full[:, 8].reshape(DEPTH, 6, d)
    mods = jnp.stack([jnp.concatenate([mod_ctx, ng_full], axis=1), jnp.concatenate([mod_lat, ng_full], axis=1)],
                     axis=1)

    weight_groups = [
        [pool_w],
        [mlp_w1[0]],
        [mlp_w2[0]],
        [attn_w_qkv[0], attn_w_o[0]],
        [mlp_w1[1], mlp_w2[1], mlp_w1[2], mlp_w2[2], gm_w_in[0], gm_w_out[0], mlp_w1[3], mlp_w2[3]],
    ]
    gathers = [None] * len(weight_groups)

    def gather_start(gi, after):
        shards, _ = lax.optimization_barrier(([w.astype(BF16) for w in weight_groups[gi]], after))
        lands = [_landing_for_gather(s, chip) for s in shards]
        gathers[gi] = _exchange_start(shards, lands, GATHER_PLAN, f"gather_weights_{gi}_start")
        return gathers[gi][4][0:1, 0:1]

    def gathered(gi, after):
        send, recv, srcs, lands, _ = gathers[gi]
        return _exchange_wait(send, recv, srcs, lands, GATHER_PLAN, after, f"gather_weights_{gi}_wait")[1]

    def rows_joined(a):
        return a.reshape(-1, a.shape[-1])

    w1_b, w2_b = [None] * DEPTH, [None] * DEPTH
    gather_start(0, mods)
    behind_gather_1 = gather_start(1, mods)
    pw_land, = gathered(0, ps_full)
    pw_f = jnp.transpose(pw_land, (1, 2, 0, 3, 4)).reshape(pool_w.shape[0], pool_w.shape[1], pgw, pgw)

    gains = jnp.concatenate([attn_q_g, attn_k_g, jnp.zeros((6, hd), F32)], axis=0)
    ws_b = gm_ws[0].astype(BF16)
    ws_t = jnp.swapaxes(gm_ws[0], 1, 2).astype(BF16)
    bs_col = gm_bs[0][:, :, None]
    cos, sin = _rope_tables(n_ctx, seq, hd)
    lat = lambda i: mods[i, 1:2]

    hc0 = jnp.concatenate([ctx[0] + behind_gather_1, x[0]], axis=0)
    ha0, p0 = _pool_fwd(hc0, mods[0] + behind_gather_1, pw_f, ps_full, 0, nct=nct, tm=tm, seg_lens=seg_lens)
    w1_b[0], = gathered(1, ha0)
    u0 = _mlp_up(ha0, mods[0] + gather_start(2, w1_b[0]), w1_b[0], 0, nct=nct, tm=tm)
    w2_b[0], = gathered(2, u0)
    hc1, o0 = _mlp_down(ha0, u0, mods[0] + gather_start(3, w2_b[0]), w2_b[0], 0, nct=nct, tm=tm)
    wqkv_b, wo_land = gathered(3, hc1)
    mods1 = mods[1] + gather_start(4, wqkv_b)
    wo_f = rows_joined(wo_land)
    xa1, qkv, q_r, k_r, v_b = _qkv_fwd(hc1, mods1, wqkv_b, cos, sin, gains, nh=nh, nkv=nkv, nct=nct, tm=tm)
    o_att, lse = _flash_fwd(q_r, k_r, v_b, n_ctx=n_ctx, hd=hd)
    ha1, y1 = _proj_fwd(o_att, wo_f, hc1, mods1, n_ctx=n_ctx, tm=tm)
    w1_b[1], w2_b[1], w1_b[2], w2_b[2], win_b, wout_land, w1_b[3], w2_b[3] = gathered(4, ha1)
    h2, u1, o1 = _mlp_fwd(ha1, lat(1), w1_b[1], w2_b[1], 1, nct=0, tm=tm_lat)
    wout_f = rows_joined(wout_land)
    ha2, zpre, y2 = _gmlp_fwd(h2, mods[2], win_b, lng_full, lnb_full, ws_b, bs_col, wout_f, tm=tm)
    h3, u2, o2 = _mlp_fwd(ha2, lat(2), w1_b[2], w2_b[2], 2, nct=0, tm=tm_lat)
    ha3, p3 = _pool_fwd(h3, lat(3), pw_f, ps_full, 3, nct=0, tm=tm_stream, seg_lens=seg_lens)
    h4, u3, o3 = _mlp_fwd(ha3, lat(3), w1_b[3], w2_b[3], 3, nct=0, tm=tm_lat)
    dh4, fin_acc = _final_loss(h4, loss_target[0], final_g[None, :], tm=tm_stream)

    dmods = [None] * DEPTH
    scatters = [None] * (DEPTH + 2)

    def blocked_rows(g):
        return g.reshape(N_CHIPS, g.shape[1] // N_CHIPS, g.shape[2])

    def blocked_pool(dpw):
        pg = dpw.shape[0]
        return jnp.transpose(dpw.astype(BF16).reshape(pg, N_CHIPS, pgw // N_CHIPS, pgw), (1, 0, 2, 3))

    def scatter_start(i, grads):
        lands = [lax.empty((3, *g.shape[1:]), g.dtype) for g in grads]
        scatters[i] = _exchange_start(grads, lands, SCATTER_PLAN, f"scatter_grads_{i}_start")
        return scatters[i][4][0:1, 0:1]

    def mlp_back(i, h_in, dh_out, u, o, md, n_ct):
        dh_in, du, dob, mb, dmd = _mlp_bwd(h_in, dh_out, u, o, md, w1_b[i], w2_b[i], i, nct=n_ct, tm=tm_lat)
        dw1 = _mm_tn(mb, du, f"mlp_dw1_{i}", col_blocks=N_CHIPS)
        dw2 = blocked_rows(_mm_tn(u, dob, f"mlp_dw2_{i}", relu2=True))
        return dh_in, dmd, [dw1, dw2]

    def pool_back(i, h_in, p_in, dh_out, md, n_ct, tile):
        dp, dmd_a, dps, dpw = _pool_bwd_weights(p_in, dh_out, md, pw_f, ps_full, i, nct=n_ct, tm=tile)
        dh_in, dmd_b = _pool_bwd_input(dp, h_in, dh_out, md, i, nct=n_ct, tm=tile, seg_lens=seg_lens, gw=pgw)
        return dh_in, dmd_a + dmd_b, dps, dpw

    zero_grp = jnp.zeros((1, 8, d), F32)
    dha3, dmd3, dws3 = mlp_back(3, ha3, dh4, u3, o3, lat(3), 0)
    dh3, dmd3p, dps3, dpw3 = pool_back(3, h3, p3, dha3, lat(3), 0, tm_stream)
    dmods[3] = jnp.concatenate([zero_grp, dmd3 + dmd3p], axis=0)
    tok = scatter_start(3, dws3 + [blocked_pool(dpw3)])
    dha2, dmd2, dws2 = mlp_back(2, ha2, dh3, u2, o2, lat(2) + tok, 0)
    dh2, dzpre, gated, dyb2, ab2, dmd2g, dln, dws, dbs = _gmlp_bwd(
        h2, dha2, zpre, y2, mods[2], win_b, lng_full, lnb_full, ws_b, ws_t, bs_col, wout_f, tm=tm)
    dwin = _mm_tn(ab2, dzpre, "gmlp_dw_in", col_blocks=N_CHIPS)
    dwout = blocked_rows(_mm_tn(gated, dyb2, "gmlp_dw_out"))
    dmods[2] = jnp.concatenate([zero_grp, dmd2 + dmd2g], axis=0)
    tok = scatter_start(2, dws2 + [dwin, dwout])
    dha1, dmd1, dws1 = mlp_back(1, ha1, dh2, u1, o1, lat(1) + tok, 0)
    do_att, dyb1, dmd1p = _proj_bwd(dha1, y1, mods[1], wo_f, tm=tm_lat)
    dwo = blocked_rows(_mm_tn(o_att, dyb1, "attn_dw_o"))
    dq, dk, dv = _flash_bwd(q_r, k_r, v_b, o_att, do_att, lse, n_ctx=n_ctx, hd=hd)
    dqkv, dgains = _qkv_bwd(qkv, dq, dk, dv, cos, sin, gains, nh=nh, nkv=nkv, nct=nct, tm=tm)
    dwqkv = _mm_tn(xa1, dqkv, "attn_dw_qkv", col_blocks=N_CHIPS)
    dhc1, dmd1i = _attn_in_bwd(dqkv, wqkv_b, hc1, dha1, mods[1], nct=nct, tm=tm)
    dmods[1] = dmd1i + jnp.concatenate([zero_grp, dmd1 + dmd1p], axis=0)
    tok = scatter_start(1, dws1 + [dwqkv, dwo])
    dha0, du0, dob0, mb0, dmd0 = _mlp_bwd(ha0, dhc1, u0, o0, mods[0] + tok, w1_b[0], w2_b[0], 0, nct=nct, tm=tm)
    scatter_start(DEPTH + 1, [blocked_rows(_mm_tn(u0, dob0, "mlp_dw2_0", relu2=True))])
    dw1_0 = _mm_tn(mb0, du0, "mlp_dw1_0", col_blocks=N_CHIPS, after=scatters[DEPTH + 1][4])
    tok = scatter_start(0, [dw1_0])
    dhc0, dmd0p, dps0, dpw0 = pool_back(0, hc0, p0, dha0, mods[0] + tok, nct, tm)
    dmods[0] = dmd0 + dmd0p
    grad_x = dhc0[None]
    scatter_start(DEPTH, [blocked_pool(dpw0)])

    dmods_all = jnp.stack(dmods, axis=0)
    small_grads = [dmods_all, dws, dbs, dgains, dln, dps0, dps3, fin_acc]
    sg_shapes = [a.shape for a in small_grads]
    sg_gath = _all_gather_small(_pack(small_grads), "gather_small_grads")
    sg_sum = _sum_devices(sg_gath, "sum_small_grads")
    s_dmods, s_dws, s_dbs, s_dgains, s_dln, s_dps0, s_dps3, s_fin = _unpack(sg_sum, sg_shapes)
    loss = s_fin[1, 0]

    sources, landed = [None] * len(scatters), [None] * len(scatters)
    for i in (3, 2, 1, DEPTH + 1, 0, DEPTH):
        send, recv, srcs, lands, _ = scatters[i]
        sources[i], landed[i] = _exchange_wait(send, recv, srcs, lands, SCATTER_PLAN, sg_sum, f"scatter_grads_{i}_wait")

    def summed(name, picks):
        return _sum_partials([sources[i][j] for i, j in picks], [landed[i][j] for i, j in picks], chip,
                             f"sum_chips_{name}")

    big = [("mlp_w1", mlp_w1, m_mlp_w1, v_mlp_w1, [(i, 0) for i in range(DEPTH)]),
           ("mlp_w2", mlp_w2, m_mlp_w2, v_mlp_w2, [(DEPTH + 1, 0)] + [(i, 1) for i in range(1, DEPTH)]),
           ("pool_w", pool_w, m_pool_w, v_pool_w, [(DEPTH, 0), (3, 2)]),
           ("attn_w_qkv", attn_w_qkv, m_attn_w_qkv, v_attn_w_qkv, [(1, 2)]),
           ("attn_w_o", attn_w_o, m_attn_w_o, v_attn_w_o, [(1, 3)]),
           ("gm_w_in", gm_w_in, m_gm_w_in, v_gm_w_in, [(2, 2)]),
           ("gm_w_out", gm_w_out, m_gm_w_out, v_gm_w_out, [(2, 3)])]
    partial = [summed(name, picks) for name, _, _, _, picks in big]
    swap = _exchange_start(partial, [lax.empty(p.shape, p.dtype) for p in partial], SIBLING_PLAN,
                           "swap_with_sibling_start")
    behind_swap = swap[4][0:1, 0:1]

    dm_dev = _unpack(sg_gath, sg_shapes[:1])[0]
    dm_lat = jnp.moveaxis(dm_dev[:, :, 1, :6, :], 0, 1).reshape(DEPTH, N_DEV, 6 * d)
    dm_ctx = jnp.moveaxis(dm_dev[:, :, 0, :6, :], 0, 1).reshape(DEPTH, N_DEV, 6 * d)
    dmod16 = _chip_cols(jnp.concatenate([dm_lat, dm_ctx], axis=1), chip, ncs) + behind_swap
    g_ada_w, dcc_part = _ada_bwd(c_all, c_all.T, dmod16, ada_w)
    dcc_gath = _all_gather_small(dcc_part, "gather_d_c_ctx")
    dcc_chips = _across_chips(dcc_gath, dcc_gath.shape[1:])
    dcc_rows = _sum_devices(dcc_chips, "sum_d_c_ctx")
    dcc = dcc_rows[0]
    ada_res = _adamw(g_ada_w.reshape(-1, ncs), None, ada_w.reshape(-1, ncs),
                     m_ada_w.reshape(-1, ncs), v_ada_w.reshape(-1, ncs), "adamw_ada_w")

    partial, from_sibling = _exchange_wait(swap[0], swap[1], swap[2], swap[3], SIBLING_PLAN, ada_res[1],
                                           "swap_with_sibling_wait")
    big_out = {}
    for (name, w, m, v, _), mine, theirs in zip(big, partial, from_sibling):
        cols = w.shape[-1]
        res = _adamw(mine, theirs, w.reshape(-1, cols), m.reshape(-1, cols), v.reshape(-1, cols), f"adamw_{name}")
        big_out[name] = [r.reshape(w.shape) for r in res]
    big_out["ada_w"] = [r.reshape(ada_w.shape) for r in ada_res]

    def cols_of(a, width):
        return _chip_cols(a, chip, width)

    zero = lambda a: jnp.zeros(a.shape, F32)
    ngw = norm_g.shape[-1]
    small = {
        "c_ctx": (dcc, zero(dcc), c_ctx, m_c_ctx, v_c_ctx),
        "ada_b": (s_dmods[:, 0, :6].reshape(DEPTH, 6 * d), s_dmods[:, 1, :6].reshape(DEPTH, 6 * d), ada_b, m_ada_b,
                  v_ada_b),
        "norm_g": (cols_of(s_dmods[:, 0, 6:8], ngw), cols_of(s_dmods[:, 1, 6:8], ngw), norm_g, m_norm_g, v_norm_g),
        "pool_scale": (cols_of(jnp.stack([s_dps0[0], s_dps3[0]]), pool_scale.shape[-1]), zero(pool_scale),
                       pool_scale, m_pool_scale, v_pool_scale),
        "attn_q_g": (s_dgains[0:1], zero(attn_q_g), attn_q_g, m_attn_q_g, v_attn_q_g),
        "attn_k_g": (s_dgains[1:2], zero(attn_k_g), attn_k_g, m_attn_k_g, v_attn_k_g),
        "gm_ln_g": (cols_of(s_dln[0:1], gm_ln_g.shape[-1]), zero(gm_ln_g), gm_ln_g, m_gm_ln_g, v_gm_ln_g),
        "gm_ln_b": (cols_of(s_dln[1:2], gm_ln_b.shape[-1]), zero(gm_ln_b), gm_ln_b, m_gm_ln_b, v_gm_ln_b),
        "gm_ws": (s_dws[None], zero(gm_ws), gm_ws, m_gm_ws, v_gm_ws),
        "gm_bs": (s_dbs[None, :, :, 0], zero(gm_bs), gm_bs, m_gm_bs, v_gm_bs),
        "final_g": (s_fin[0], zero(final_g), final_g, m_final_g, v_final_g),
    }
    keys = list(small)
    packed = [_pack([small[k][t] for k in keys]) for t in range(5)]
    res = _adamw(*packed, "adamw_small")
    shapes = [small[k][2].shape for k in keys]
    small_out = {k: [] for k in keys}
    for r in res:
        for k, a in zip(keys, _unpack(r, shapes)):
            small_out[k].append(a)

    order = ["c_ctx", "ada_w", "ada_b", "norm_g", "mlp_w1", "mlp_w2", "pool_w", "pool_scale", "attn_w_qkv",
             "attn_w_o", "attn_q_g", "attn_k_g", "gm_w_in", "gm_ln_g", "gm_ln_b", "gm_ws", "gm_bs", "gm_w_out",
             "final_g"]
    allo = {**big_out, **small_out}
    outs = [loss, grad_x]
    for t in range(4):
        outs += [allo[k][t] for k in order]
    return tuple(outs)
```

```python
import math

import numpy as np
import jax
import jax.numpy as jnp
from jax import lax
from jax.experimental import pallas as pl
from jax.experimental.pallas import tpu as pltpu

F32 = jnp.float32
BF16 = jnp.bfloat16
MESH = pl.DeviceIdType.MESH

EPS = 1e-6
GRID_W = 64
ROPE_BASE = 10000.0
POOL_WINDOWS = (2, 4, 8, 16)
HALO = 8
DEPTH = 4
N_MIXERS = 3

ADAM_LR = 0.001
ADAM_B1 = 0.9
ADAM_B2 = 0.999
ADAM_EPS = 1e-08
ADAM_WD = 0.01
ADAM_STEP = 10

VMEM_LIMIT_BYTES = 56 * 1024 * 1024
LANES = 128
SUBLANES = 8
N_DEV = 8
N_CHIPS = 4

SH1, SC1, G1, SH2, SC2, G2, NG0, NG1 = range(8)


def _dot(a, b):
    return jnp.dot(a, b, preferred_element_type=F32)


def _dot_nt(a, b):
    return lax.dot_general(a, b, (((1,), (1,)), ((), ())), preferred_element_type=F32)


def _dot_tn(a, b):
    return lax.dot_general(a, b, (((0,), (0,)), ((), ())), preferred_element_type=F32)


def _dot_blocks(a, w_ref):
    return jnp.concatenate([_dot(a, w_ref[k]) for k in range(w_ref.shape[0])], axis=1)


def _dot_nt_blocks(a, w_ref):
    nb, _, w = w_ref.shape
    acc = _dot_nt(a[:, 0:w], w_ref[0])
    for k in range(1, nb):
        acc = acc + _dot_nt(a[:, k * w:(k + 1) * w], w_ref[k])
    return acc


def _params(**kw):
    return pltpu.CompilerParams(vmem_limit_bytes=VMEM_LIMIT_BYTES, **kw)


def _full(shape):
    nd = len(shape)
    return pl.BlockSpec(shape, lambda *_: (0,) * nd)


def _rows(tm, width):
    return pl.BlockSpec((tm, width), lambda i: (i, 0))


def _group_of(nct, groups):
    if groups == 1:
        return lambda i: 0
    return lambda i: jnp.where(i >= nct, 1, 0)


def _mods_spec(nct, groups, d):
    grp = _group_of(nct, groups)
    return pl.BlockSpec((None, 8, d), lambda i: (grp(i), 0, 0))


def _first_of_group(i, nct, groups):
    if groups == 1:
        return i == 0
    return jnp.logical_or(i == 0, i == nct)


def _rowsum(v):
    return jnp.sum(v, axis=0, keepdims=True)


def _rms_parts(x):
    r = lax.rsqrt(jnp.mean(x * x, axis=-1, keepdims=True) + EPS)
    return x * r, r


def _normmod(x, md, which):
    ng, sh, sc = (md[NG0:NG0 + 1], md[SH1:SH1 + 1], md[SC1:SC1 + 1]) if which == 0 else (
        md[NG1:NG1 + 1], md[SH2:SH2 + 1], md[SC2:SC2 + 1])
    xhat, r = _rms_parts(x)
    n = xhat * ng
    return n * (1.0 + sc) + sh, (xhat, r, n)


def _normmod_bwd(da, parts, md, which):
    xhat, r, n = parts
    ng, sc = (md[NG0:NG0 + 1], md[SC1:SC1 + 1]) if which == 0 else (md[NG1:NG1 + 1], md[SC2:SC2 + 1])
    dsh = _rowsum(da)
    dsc = _rowsum(da * n)
    dn = da * (1.0 + sc)
    dng = _rowsum(dn * xhat)
    dxhat = dn * ng
    dx = r * (dxhat - xhat * jnp.mean(dxhat * xhat, axis=-1, keepdims=True))
    return dx, dsh, dsc, dng


def _acc_rows(ref, first, rows):
    @pl.when(first)
    def _():
        ref[...] = jnp.zeros(ref.shape, ref.dtype)

    for r, v in rows.items():
        ref[r:r + 1, :] += v


def _shift_up(x, k):
    if k == 0:
        return x
    return pltpu.roll(x, x.shape[0] - k, axis=0)


def _gelu(x):
    k = math.sqrt(2.0 / math.pi)
    return 0.5 * x * (1.0 + jnp.tanh(k * (x + 0.044715 * x * x * x)))


def _gelu_grad(x):
    k = math.sqrt(2.0 / math.pi)
    t = jnp.tanh(k * (x + 0.044715 * x * x * x))
    return 0.5 * (1.0 + t) + 0.5 * x * (1.0 - t * t) * k * (1.0 + 3.0 * 0.044715 * x * x)


def _silu(x):
    return x / (1.0 + jnp.exp(-x))


def _silu_grad(x):
    s = 1.0 / (1.0 + jnp.exp(-x))
    return s * (1.0 + x * (1.0 - s))


def _mlp_fwd(h, mods, w1, w2, layer, *, nct, tm):
    rows, d = h.shape
    groups = mods.shape[0]
    nb, _, fc = w1.shape
    ff = nb * fc

    def body(h_ref, md_ref, w1_ref, w2_ref, h2_ref, u_ref, o_ref):
        x = h_ref[...]
        md = md_ref[...]
        m, _ = _normmod(x, md, 1)
        mb = m.astype(BF16)
        acc = jnp.zeros((tm, d), F32)
        for k in range(nb):
            u = _dot(mb, w1_ref[k])
            u_ref[:, k * fc:(k + 1) * fc] = u.astype(BF16)
            acc = acc + _dot(jnp.square(jnp.maximum(u, 0.0)).astype(BF16), w2_ref[k])
        o_ref[...] = acc.astype(BF16)
        h2_ref[...] = x + md[G2:G2 + 1] * acc

    return pl.pallas_call(
        body, name=f"mlp_fwd_{layer}", grid=(rows // tm,),
        in_specs=[_rows(tm, d), _mods_spec(nct, groups, d), _full(w1.shape), _full(w2.shape)],
        out_specs=[_rows(tm, d), _rows(tm, ff), _rows(tm, d)],
        out_shape=[jax.ShapeDtypeStruct((rows, d), F32), jax.ShapeDtypeStruct((rows, ff), BF16),
                   jax.ShapeDtypeStruct((rows, d), BF16)],
        compiler_params=_params(),
    )(h, mods, w1, w2)


def _mlp_up(h, mods, w1, layer, *, nct, tm):
    rows, d = h.shape
    groups = mods.shape[0]
    nb, _, fc = w1.shape

    def body(h_ref, md_ref, w1_ref, u_ref):
        m, _ = _normmod(h_ref[...], md_ref[...], 1)
        mb = m.astype(BF16)
        for k in range(nb):
            u_ref[:, k * fc:(k + 1) * fc] = _dot(mb, w1_ref[k]).astype(BF16)

    return pl.pallas_call(
        body, name=f"mlp_up_{layer}", grid=(rows // tm,),
        in_specs=[_rows(tm, d), _mods_spec(nct, groups, d), _full(w1.shape)],
        out_specs=_rows(tm, nb * fc), out_shape=jax.ShapeDtypeStruct((rows, nb * fc), BF16),
        compiler_params=_params(),
    )(h, mods, w1)


def _mlp_down(h, u, mods, w2, layer, *, nct, tm):
    rows, d = h.shape
    groups = mods.shape[0]
    nb, fc, _ = w2.shape

    def body(h_ref, u_ref, md_ref, w2_ref, h2_ref, o_ref):
        acc = jnp.zeros((tm, d), F32)
        for k in range(nb):
            uk = u_ref[:, k * fc:(k + 1) * fc].astype(F32)
            acc = acc + _dot(jnp.square(jnp.maximum(uk, 0.0)).astype(BF16), w2_ref[k])
        o_ref[...] = acc.astype(BF16)
        h2_ref[...] = h_ref[...] + md_ref[G2:G2 + 1, :] * acc

    return pl.pallas_call(
        body, name=f"mlp_down_{layer}", grid=(rows // tm,),
        in_specs=[_rows(tm, d), _rows(tm, nb * fc), _mods_spec(nct, groups, d), _full(w2.shape)],
        out_specs=[_rows(tm, d), _rows(tm, d)],
        out_shape=[jax.ShapeDtypeStruct((rows, d), F32), jax.ShapeDtypeStruct((rows, d), BF16)],
        compiler_params=_params(),
    )(h, u, mods, w2)


def _mlp_bwd(h1, dh2, u, o, mods, w1, w2, layer, *, nct, tm):
    rows, d = h1.shape
    groups = mods.shape[0]
    nb, _, fc = w1.shape
    ff = nb * fc

    def body(h_ref, g_ref, u_ref, o_ref, md_ref, w1_ref, w2_ref, dh_ref, du_ref, dob_ref, mb_ref, dmd_ref):
        i = pl.program_id(0)
        x = h_ref[...]
        g = g_ref[...]
        md = md_ref[...]
        m, parts = _normmod(x, md, 1)
        mb_ref[...] = m.astype(BF16)
        dg2 = _rowsum(g * o_ref[...].astype(F32))
        dob = (g * md[G2:G2 + 1]).astype(BF16)
        dob_ref[...] = dob
        dm = jnp.zeros((tm, d), F32)
        for k in range(nb):
            uk = u_ref[:, k * fc:(k + 1) * fc].astype(F32)
            dr = _dot_nt(dob, w2_ref[k])
            duk = (dr * (2.0 * jnp.maximum(uk, 0.0))).astype(BF16)
            du_ref[:, k * fc:(k + 1) * fc] = duk
            dm = dm + _dot_nt(duk, w1_ref[k])
        dx, dsh, dsc, dng = _normmod_bwd(dm, parts, md, 1)
        dh_ref[...] = g + dx
        _acc_rows(dmd_ref, _first_of_group(i, nct, groups), {SH2: dsh, SC2: dsc, G2: dg2, NG1: dng})

    return pl.pallas_call(
        body, name=f"mlp_bwd_{layer}", grid=(rows // tm,),
        in_specs=[_rows(tm, d), _rows(tm, d), _rows(tm, ff), _rows(tm, d), _mods_spec(nct, groups, d),
                  _full(w1.shape), _full(w2.shape)],
        out_specs=[_rows(tm, d), _rows(tm, ff), _rows(tm, d), _rows(tm, d), _mods_spec(nct, groups, d)],
        out_shape=[jax.ShapeDtypeStruct((rows, d), F32), jax.ShapeDtypeStruct((rows, ff), BF16),
                   jax.ShapeDtypeStruct((rows, d), BF16), jax.ShapeDtypeStruct((rows, d), BF16),
                   jax.ShapeDtypeStruct((groups, 8, d), F32)],
        compiler_params=_params(),
    )(h1, dh2, u, o, mods, w1, w2)


def _div_tile(n, cap):
    if n <= cap:
        return n
    return max(t for t in range(LANES, cap + 1, LANES) if n % t == 0)


DW_TOKEN_TILE_CAP = 4224


def _mm_tn(a, b, name, *, relu2=False, col_blocks=1, after=None):
    rows, m = a.shape
    n = b.shape[1]
    tmm = min(m, 1024)
    tn = min(n // col_blocks, 2048)
    per_block = n // col_blocks // tn
    tr = _div_tile(rows, DW_TOKEN_TILE_CAP)
    tokens = [] if after is None else [after]

    def body(a_ref, b_ref, *rest):
        o_ref, acc_ref = rest[len(tokens):]
        r = pl.program_id(2)

        @pl.when(r == 0)
        def _():
            acc_ref[...] = jnp.zeros(acc_ref.shape, F32)

        av = a_ref[...]
        if relu2:
            av = jnp.square(jnp.maximum(av.astype(F32), 0.0)).astype(BF16)
        acc_ref[...] += _dot_tn(av, b_ref[...])

        @pl.when(r == pl.num_programs(2) - 1)
        def _():
            o_ref[...] = acc_ref[...].astype(BF16)

    return pl.pallas_call(
        body, name=name, grid=(m // tmm, n // tn, rows // tr),
        in_specs=[pl.BlockSpec((tr, tmm), lambda i, j, r: (r, i)), pl.BlockSpec((tr, tn), lambda i, j, r: (r, j))]
        + [pl.BlockSpec((8, LANES), lambda i, j, r: (0, 0))] * len(tokens),
        out_specs=pl.BlockSpec((None, tmm, tn), lambda i, j, r: (j // per_block, i, j % per_block)),
        out_shape=jax.ShapeDtypeStruct((col_blocks, m, n // col_blocks), BF16),
        scratch_shapes=[pltpu.VMEM((tmm, tn), F32)],
        compiler_params=_params(),
    )(a, b, *tokens)


def _halo_specs(tm, d, rows):
    per = tm // HALO
    prev = pl.BlockSpec((HALO, d), lambda i: (jnp.maximum(i * per - 1, 0), 0))
    nxt = pl.BlockSpec((HALO, d), lambda i: (jnp.minimum((i + 1) * per, rows // HALO - 1), 0))
    return prev, _rows(tm, d), nxt


def _segment_positions(i, tm, nct, groups, seg_lens):
    if groups == 1:
        start, length = 0, seg_lens[-1]
    else:
        start = jnp.where(i >= nct, nct, 0)
        length = jnp.where(i >= nct, seg_lens[1], seg_lens[0])
    rid = lax.broadcasted_iota(jnp.int32, (tm + 2 * HALO, 1), 0)
    pos = (i - start) * tm - HALO + rid
    return pos, length


def _window_count(pos, length, w):
    hi = jnp.minimum(pos + (w - w // 2), length)
    lo = jnp.maximum(pos - w // 2, 0)
    return (hi - lo).astype(F32)


def _window_sum(xg, w, lead):
    b, k = xg, 1
    while k < w:
        b = b + _shift_up(b, k)
        k *= 2
    return _shift_up(b, HALO - lead)[0:xg.shape[0] - 2 * HALO]


def _pooled(ext, md, pos, length, gw):
    tm = ext.shape[0] - 2 * HALO
    a_ext, parts = _normmod(ext, md, 0)
    valid = jnp.logical_and(pos >= 0, pos < length)
    a_ext = jnp.where(valid, a_ext, 0.0)
    pos_c = pos[HALO:HALO + tm]
    ps = []
    for g, w in enumerate(POOL_WINDOWS):
        xg = a_ext[:, g * gw:(g + 1) * gw]
        s = _window_sum(xg, w, w // 2)
        ps.append(s * (1.0 / _window_count(pos_c, length, w)) - xg[HALO:HALO + tm])
    return ps, parts


def _pool_fwd(h, mods, pw, pscale, layer, *, nct, tm, seg_lens):
    rows, d = h.shape
    groups = mods.shape[0]
    pg, gw = pw.shape[1], pw.shape[-1]

    def body(prev_ref, cur_ref, next_ref, md_ref, pw_ref, ps_ref, out_ref, p_ref):
        i = pl.program_id(0)
        md = md_ref[...]
        cur = cur_ref[...]
        ext = jnp.concatenate([prev_ref[...], cur, next_ref[...]], axis=0)
        pos, length = _segment_positions(i, tm, nct, groups, seg_lens)
        ps, _ = _pooled(ext, md, pos, length, gw)
        for g in range(pg):
            pb = ps[g].astype(BF16)
            p_ref[:, g * gw:(g + 1) * gw] = pb
            yg = _dot(pb, pw_ref[g]) * ps_ref[:, g * gw:(g + 1) * gw]
            out_ref[:, g * gw:(g + 1) * gw] = cur[:, g * gw:(g + 1) * gw] + md[G1:G1 + 1, g * gw:(g + 1) * gw] * yg

    j = layer // N_MIXERS
    return pl.pallas_call(
        body, name=f"pool_fwd_{layer}", grid=(rows // tm,),
        in_specs=[*_halo_specs(tm, d, rows), _mods_spec(nct, groups, d),
                  pl.BlockSpec((None, pg, gw, gw), lambda i: (j, 0, 0, 0)), _full((1, d))],
        out_specs=[_rows(tm, d), _rows(tm, d)],
        out_shape=[jax.ShapeDtypeStruct((rows, d), F32), jax.ShapeDtypeStruct((rows, d), BF16)],
        compiler_params=_params(),
    )(h, h, h, mods, pw, pscale[j:j + 1])


def _pool_bwd_weights(p, dh1, mods, pw, pscale, layer, *, nct, tm):
    rows, d = p.shape
    groups = mods.shape[0]
    pg, gw = pw.shape[1], pw.shape[-1]

    def body(p_ref, g_ref, md_ref, pw_ref, ps_ref, dp_ref, dmd_ref, dps_ref, dpw_ref):
        i = pl.program_id(0)
        md = md_ref[...]
        gup = g_ref[...]

        @pl.when(i == 0)
        def _():
            dps_ref[...] = jnp.zeros(dps_ref.shape, F32)
            dpw_ref[...] = jnp.zeros(dpw_ref.shape, F32)

        dg1 = []
        for g in range(pg):
            cols = slice(g * gw, (g + 1) * gw)
            pb = p_ref[:, cols]
            yp = _dot(pb, pw_ref[g])
            sc = ps_ref[:, cols]
            dg1.append(_rowsum(gup[:, cols] * (yp * sc)))
            dy = gup[:, cols] * md[G1:G1 + 1, cols]
            dps_ref[0:1, cols] += _rowsum(dy * yp)
            dyp = (dy * sc).astype(BF16)
            dp_ref[:, cols] = _dot_nt(dyp, pw_ref[g])
            dpw_ref[g] += _dot_tn(pb, dyp)
        _acc_rows(dmd_ref, _first_of_group(i, nct, groups), {G1: jnp.concatenate(dg1, axis=1)})

    j = layer // N_MIXERS
    return pl.pallas_call(
        body, name=f"pool_bwd_w_{layer}", grid=(rows // tm,),
        in_specs=[_rows(tm, d), _rows(tm, d), _mods_spec(nct, groups, d),
                  pl.BlockSpec((None, pg, gw, gw), lambda i: (j, 0, 0, 0)), _full((1, d))],
        out_specs=[_rows(tm, d), _mods_spec(nct, groups, d), _full((8, d)), _full((pg, gw, gw))],
        out_shape=[jax.ShapeDtypeStruct((rows, d), F32), jax.ShapeDtypeStruct((groups, 8, d), F32),
                   jax.ShapeDtypeStruct((8, d), F32), jax.ShapeDtypeStruct((pg, gw, gw), F32)],
        compiler_params=_params(),
    )(p, dh1, mods, pw, pscale[j:j + 1])


def _pool_bwd_input(dp, h, dh1, mods, layer, *, nct, tm, seg_lens, gw):
    rows, d = h.shape
    groups = mods.shape[0]

    def body(prev_ref, cur_ref, next_ref, h_ref, g_ref, md_ref, dh_ref, dmd_ref):
        i = pl.program_id(0)
        md = md_ref[...]
        dp_cur = cur_ref[...]
        ext = jnp.concatenate([prev_ref[...], dp_cur, next_ref[...]], axis=0)
        pos, length = _segment_positions(i, tm, nct, groups, seg_lens)
        valid = jnp.logical_and(pos >= 0, pos < length)
        das = []
        for g, w in enumerate(POOL_WINDOWS):
            cols = slice(g * gw, (g + 1) * gw)
            q = jnp.where(valid, ext[:, cols] * (1.0 / jnp.maximum(_window_count(pos, length, w), 1.0)), 0.0)
            das.append(_window_sum(q, w, w // 2 - 1) - dp_cur[:, cols])
        da = jnp.concatenate(das, axis=1)
        _, parts = _normmod(h_ref[...], md, 0)
        dx, dsh, dsc, dng = _normmod_bwd(da, parts, md, 0)
        dh_ref[...] = g_ref[...] + dx
        _acc_rows(dmd_ref, _first_of_group(i, nct, groups), {SH1: dsh, SC1: dsc, NG0: dng})

    return pl.pallas_call(
        body, name=f"pool_bwd_x_{layer}", grid=(rows // tm,),
        in_specs=[*_halo_specs(tm, d, rows), _rows(tm, d), _rows(tm, d), _mods_spec(nct, groups, d)],
        out_specs=[pl.BlockSpec((tm, d), lambda i: (jnp.maximum(i - nct, 0), 0)), _mods_spec(nct, groups, d)],
        out_shape=[jax.ShapeDtypeStruct((rows - nct * tm, d), F32), jax.ShapeDtypeStruct((groups, 8, d), F32)],
        compiler_params=_params(),
    )(dp, dp, dp, h, dh1, mods)


def _rope_tables(n_ctx, seq, hd):
    half = hd // 2
    n_rows = seq // GRID_W
    inv = np.float32(ROPE_BASE) ** (-np.arange(0, half, 2, dtype=np.float32) / np.float32(half))
    ar = np.arange(n_rows, dtype=np.float32)[:, None] * inv[None, :]
    ac = np.arange(GRID_W, dtype=np.float32)[:, None] * inv[None, :]

    def over_tokens(row_part, col_part):
        return np.repeat(row_part, GRID_W, axis=0), np.tile(col_part, (n_rows, 1))

    cr, cc = over_tokens(np.cos(ar), np.cos(ac))
    sr, sc = over_tokens(np.sin(ar), np.sin(ac))
    cos = np.concatenate([cr, cr, cc, cc], axis=1)
    sin = np.concatenate([-sr, sr, -sc, sc], axis=1)
    cos = np.concatenate([np.ones((n_ctx, hd), np.float32), cos], axis=0)
    sin = np.concatenate([np.zeros((n_ctx, hd), np.float32), sin], axis=0)
    return jnp.asarray(cos, F32), jnp.asarray(sin, F32)


def _rope_partner(x):
    hd = x.shape[-1]
    q = hd // 4
    lane = lax.broadcasted_iota(jnp.int32, x.shape, 1)
    first = (lane % (2 * q)) < q
    return jnp.where(first, pltpu.roll(x, hd - q, axis=1), pltpu.roll(x, q, axis=1))


def _qkv_fwd(h, mods, wqkv, cos, sin, gains, *, nh, nkv, nct, tm):
    rows, d = h.shape
    qw = wqkv.shape[0] * wqkv.shape[-1]
    hd = cos.shape[-1]

    def body(h_ref, md_ref, w_ref, cos_ref, sin_ref, gn_ref, xa_ref, qkv_ref, q_ref, k_ref, v_ref):
        a, _ = _normmod(h_ref[...], md_ref[...], 0)
        xa = a.astype(BF16)
        xa_ref[...] = xa
        qkv = _dot_blocks(xa, w_ref)
        qkv_ref[...] = qkv
        c, s = cos_ref[...], sin_ref[...]
        for hh in range(nh + nkv):
            xh = qkv[:, hh * hd:(hh + 1) * hd]
            xhat, _ = _rms_parts(xh)
            y = xhat * (gn_ref[0:1, :] if hh < nh else gn_ref[1:2, :])
            rot = (y * c + _rope_partner(y) * s).astype(BF16)
            if hh < nh:
                q_ref[:, hh * hd:(hh + 1) * hd] = rot
            else:
                k_ref[:, (hh - nh) * hd:(hh - nh + 1) * hd] = rot
        v_ref[...] = qkv[:, (nh + nkv) * hd:].astype(BF16)

    return pl.pallas_call(
        body, name="attn_qkv_fwd", grid=(rows // tm,),
        in_specs=[_rows(tm, d), _mods_spec(nct, 2, d), _full(wqkv.shape), _rows(tm, hd), _rows(tm, hd),
                  _full((8, hd))],
        out_specs=[_rows(tm, d), _rows(tm, qw), pl.BlockSpec((tm, nh * hd), lambda i: (jnp.maximum(i - nct, 0), 0)),
                   _rows(tm, nkv * hd), _rows(tm, nkv * hd)],
        out_shape=[jax.ShapeDtypeStruct((rows, d), BF16), jax.ShapeDtypeStruct((rows, qw), F32),
                   jax.ShapeDtypeStruct((rows - nct * tm, nh * hd), BF16),
                   jax.ShapeDtypeStruct((rows, nkv * hd), BF16), jax.ShapeDtypeStruct((rows, nkv * hd), BF16)],
        compiler_params=_params(),
    )(h, mods, wqkv, cos, sin, gains)


ATTN_Q_TILE_CAP = 1024
ATTN_KV_TILE_CAP = 4224
ATTN_ROW_GROUP = 256
LOG2E = 1.4426950408889634


def _attn_tiles(seq, total):
    tq = _div_tile(seq, ATTN_Q_TILE_CAP)
    return tq, _div_tile(total, ATTN_KV_TILE_CAP), min(ATTN_ROW_GROUP, tq)


def _flash_fwd(q, k, v, *, n_ctx, hd):
    total = k.shape[0]
    seq = total - n_ctx
    nkv = k.shape[1] // hd
    tq, tk, rg = _attn_tiles(seq, total)
    nk = total // tk
    scale = hd ** -0.5
    c2 = scale * LOG2E

    def body(q_ref, k_ref, v_ref, o_ref, lse_ref, m_sc, l_sc, acc_sc):
        ki = pl.program_id(2)

        @pl.when(ki == 0)
        def _():
            m_sc[...] = jnp.full(m_sc.shape, -jnp.inf, F32)
            l_sc[...] = jnp.zeros(l_sc.shape, F32)
            acc_sc[...] = jnp.zeros(acc_sc.shape, F32)

        kk, vv = k_ref[...], v_ref[...]
        groups = [(g, sub) for g in range(2) for sub in range(tq // rg)]

        def scores(g, sub):
            return _dot_nt(q_ref[sub * rg:(sub + 1) * rg, g * hd:(g + 1) * hd], kk)

        s_next = scores(*groups[0])
        for idx, (g, sub) in enumerate(groups):
            s = s_next
            if idx + 1 < len(groups):
                s_next = scores(*groups[idx + 1])
            rows = slice(g * tq + sub * rg, g * tq + (sub + 1) * rg)
            m_old = m_sc[rows]
            m_new = jnp.maximum(m_old, jnp.max(s, axis=-1, keepdims=True))
            alpha = jnp.exp2((m_old - m_new) * c2)
            p = jnp.exp2((s - m_new) * c2)
            l_sc[rows] = alpha * l_sc[rows] + jnp.sum(p, axis=-1, keepdims=True)
            acc_sc[rows] = alpha * acc_sc[rows] + _dot(p.astype(BF16), vv)
            m_sc[rows] = m_new

        @pl.when(ki == nk - 1)
        def _():
            o2 = acc_sc[...] / l_sc[...]
            lse = m_sc[...] * scale + jnp.log(l_sc[...])
            o_ref[:, :hd] = o2[:tq].astype(BF16)
            o_ref[:, hd:] = o2[tq:].astype(BF16)
            lse_ref[:, 0:1] = lse[:tq]
            lse_ref[:, 1:2] = lse[tq:]

    return pl.pallas_call(
        body, name="attn_flash_fwd", grid=(nkv, seq // tq, nk),
        in_specs=[pl.BlockSpec((tq, 2 * hd), lambda h, i, j: (i, h)),
                  pl.BlockSpec((tk, hd), lambda h, i, j: (j, h)),
                  pl.BlockSpec((tk, hd), lambda h, i, j: (j, h))],
        out_specs=[pl.BlockSpec((tq, 2 * hd), lambda h, i, j: (i, h)),
                   pl.BlockSpec((None, tq, 2), lambda h, i, j: (h, i, 0))],
        out_shape=[jax.ShapeDtypeStruct((seq, 2 * nkv * hd), BF16), jax.ShapeDtypeStruct((nkv, seq, 2), F32)],
        scratch_shapes=[pltpu.VMEM((2 * tq, 1), F32), pltpu.VMEM((2 * tq, 1), F32), pltpu.VMEM((2 * tq, hd), F32)],
        compiler_params=_params(),
    )(q, k, v)


def _flash_bwd(q, k, v, o, do, lse, *, n_ctx, hd):
    total = k.shape[0]
    seq = total - n_ctx
    nkv = k.shape[1] // hd
    tq, tk, rg = _attn_tiles(seq, total)
    scale = hd ** -0.5
    c2 = scale * LOG2E

    def body(q_ref, k_ref, v_ref, o_ref, do_ref, lse_ref, dq_ref, dk_ref, dv_ref):
        ki, qi = pl.program_id(1), pl.program_id(2)
        kk, vv = k_ref[...], v_ref[...]

        @pl.when(qi == 0)
        def _():
            dk_ref[...] = jnp.zeros(dk_ref.shape, F32)
            dv_ref[...] = jnp.zeros(dv_ref.shape, F32)

        dk_acc = jnp.zeros((tk, hd), F32)
        dv_acc = jnp.zeros((tk, hd), F32)
        for g in range(2):
            for sub in range(tq // rg):
                rs = slice(sub * rg, (sub + 1) * rg)
                cs = slice(g * hd, (g + 1) * hd)
                qq = q_ref[rs, cs]
                dd = do_ref[rs, cs]
                delta = jnp.sum(dd.astype(F32) * o_ref[rs, cs].astype(F32), axis=-1, keepdims=True)
                p = jnp.exp2(_dot_nt(qq, kk) * c2 - lse_ref[rs, g:g + 1] * LOG2E)
                dp = _dot_nt(dd, vv)
                ds = (p * (dp - delta) * scale).astype(BF16)
                dv_acc = dv_acc + _dot_tn(p.astype(BF16), dd)
                dk_acc = dk_acc + _dot_tn(ds, qq)
                dq = _dot(ds, kk)
                rows = pl.ds(pl.multiple_of(qi * tq, tq) + sub * rg, rg)

                @pl.when(ki == 0)
                def _():
                    dq_ref[rows, cs] = dq

                @pl.when(ki > 0)
                def _():
                    dq_ref[rows, cs] += dq
        dk_ref[...] += dk_acc
        dv_ref[...] += dv_acc

    return pl.pallas_call(
        body, name="attn_flash_bwd", grid=(nkv, total // tk, seq // tq),
        in_specs=[pl.BlockSpec((tq, 2 * hd), lambda h, j, i: (i, h)),
                  pl.BlockSpec((tk, hd), lambda h, j, i: (j, h)),
                  pl.BlockSpec((tk, hd), lambda h, j, i: (j, h)),
                  pl.BlockSpec((tq, 2 * hd), lambda h, j, i: (i, h)),
                  pl.BlockSpec((tq, 2 * hd), lambda h, j, i: (i, h)),
                  pl.BlockSpec((None, tq, 2), lambda h, j, i: (h, i, 0))],
        out_specs=[pl.BlockSpec((seq, 2 * hd), lambda h, j, i: (0, h)),
                   pl.BlockSpec((tk, hd), lambda h, j, i: (j, h)),
                   pl.BlockSpec((tk, hd), lambda h, j, i: (j, h))],
        out_shape=[jax.ShapeDtypeStruct((seq, 2 * nkv * hd), F32), jax.ShapeDtypeStruct((total, nkv * hd), F32),
                   jax.ShapeDtypeStruct((total, nkv * hd), F32)],
        compiler_params=_params(),
    )(q, k, v, o, do, lse)


def _proj_fwd(o, wo, hc, mods, *, n_ctx, tm):
    seq, d = o.shape
    off = n_ctx // tm

    def body(o_ref, w_ref, h_ref, md_ref, h1_ref, y_ref):
        y = _dot(o_ref[...], w_ref[...])
        y_ref[...] = y.astype(BF16)
        h1_ref[...] = h_ref[...] + md_ref[G1:G1 + 1, :] * y

    return pl.pallas_call(
        body, name="attn_proj_fwd", grid=(seq // tm,),
        in_specs=[_rows(tm, d), _full((d, d)),
                  pl.BlockSpec((tm, d), lambda i: (i + off, 0)), pl.BlockSpec((None, 8, d), lambda i: (1, 0, 0))],
        out_specs=[_rows(tm, d), _rows(tm, d)],
        out_shape=[jax.ShapeDtypeStruct((seq, d), F32), jax.ShapeDtypeStruct((seq, d), BF16)],
        compiler_params=_params(),
    )(o, wo, hc, mods)


def _proj_bwd(dh1, y, mods, wo, *, tm):
    seq, d = dh1.shape

    def body(g_ref, y_ref, md_ref, w_ref, do_ref, dyb_ref, dmd_ref):
        i = pl.program_id(0)
        g = g_ref[...]
        dyb = (g * md_ref[G1:G1 + 1, :]).astype(BF16)
        dyb_ref[...] = dyb
        do_ref[...] = _dot_nt(dyb, w_ref[...]).astype(BF16)
        _acc_rows(dmd_ref, i == 0, {G1: _rowsum(g * y_ref[...].astype(F32))})

    return pl.pallas_call(
        body, name="attn_proj_bwd", grid=(seq // tm,),
        in_specs=[_rows(tm, d), _rows(tm, d), pl.BlockSpec((None, 8, d), lambda i: (1, 0, 0)), _full((d, d))],
        out_specs=[_rows(tm, d), _rows(tm, d), pl.BlockSpec((None, 8, d), lambda i: (0, 0, 0))],
        out_shape=[jax.ShapeDtypeStruct((seq, d), BF16), jax.ShapeDtypeStruct((seq, d), BF16),
                   jax.ShapeDtypeStruct((1, 8, d), F32)],
        compiler_params=_params(),
    )(dh1, y, mods, wo)


def _qkv_bwd(qkv, dq, dk, dv, cos, sin, gains, *, nh, nkv, nct, tm):
    rows, qw = qkv.shape
    hd = cos.shape[-1]

    def body(qkv_ref, dq_ref, dk_ref, dv_ref, cos_ref, sin_ref, gn_ref, out_ref, dgn_ref):
        i = pl.program_id(0)
        c, s = cos_ref[...], sin_ref[...]
        is_lat = (i >= nct).astype(F32)
        dqg = jnp.zeros((1, hd), F32)
        dkg = jnp.zeros((1, hd), F32)
        for hh in range(nh + nkv):
            if hh < nh:
                dr = dq_ref[:, hh * hd:(hh + 1) * hd] * is_lat
                gn = gn_ref[0:1, :]
            else:
                dr = dk_ref[:, (hh - nh) * hd:(hh - nh + 1) * hd]
                gn = gn_ref[1:2, :]
            dy = dr * c + _rope_partner(dr * s)
            xhat, r = _rms_parts(qkv_ref[:, hh * hd:(hh + 1) * hd])
            dgh = _rowsum(dy * xhat)
            if hh < nh:
                dqg = dqg + dgh
            else:
                dkg = dkg + dgh
            dxhat = dy * gn
            dx = r * (dxhat - xhat * jnp.mean(dxhat * xhat, axis=-1, keepdims=True))
            out_ref[:, hh * hd:(hh + 1) * hd] = dx.astype(BF16)
        out_ref[:, (nh + nkv) * hd:] = dv_ref[...].astype(BF16)
        _acc_rows(dgn_ref, i == 0, {0: dqg, 1: dkg})

    return pl.pallas_call(
        body, name="attn_qkv_bwd", grid=(rows // tm,),
        in_specs=[_rows(tm, qw), pl.BlockSpec((tm, nh * hd), lambda i: (jnp.maximum(i - nct, 0), 0)),
                  _rows(tm, nkv * hd), _rows(tm, nkv * hd), _rows(tm, hd), _rows(tm, hd), _full((8, hd))],
        out_specs=[_rows(tm, qw), _full((8, hd))],
        out_shape=[jax.ShapeDtypeStruct((rows, qw), BF16), jax.ShapeDtypeStruct((8, hd), F32)],
        compiler_params=_params(),
    )(qkv, dq, dk, dv, cos, sin, gains)


def _attn_in_bwd(dqkv, wqkv, hc, dh1, mods, *, nct, tm):
    rows, d = hc.shape
    qw = dqkv.shape[1]

    def body(dz_ref, w_ref, h_ref, g_ref, md_ref, dh_ref, dmd_ref):
        i = pl.program_id(0)
        md = md_ref[...]
        da = _dot_nt_blocks(dz_ref[...], w_ref)
        _, parts = _normmod(h_ref[...], md, 0)
        dx, dsh, dsc, dng = _normmod_bwd(da, parts, md, 0)
        dh_ref[...] = g_ref[...] * (i >= nct).astype(F32) + dx
        _acc_rows(dmd_ref, _first_of_group(i, nct, 2), {SH1: dsh, SC1: dsc, NG0: dng})

    return pl.pallas_call(
        body, name="attn_in_bwd", grid=(rows // tm,),
        in_specs=[_rows(tm, qw), _full(wqkv.shape), _rows(tm, d),
                  pl.BlockSpec((tm, d), lambda i: (jnp.maximum(i - nct, 0), 0)), _mods_spec(nct, 2, d)],
        out_specs=[_rows(tm, d), _mods_spec(nct, 2, d)],
        out_shape=[jax.ShapeDtypeStruct((rows, d), F32), jax.ShapeDtypeStruct((2, 8, d), F32)],
        compiler_params=_params(),
    )(dqkv, wqkv, hc, dh1, mods)


def _gmlp_gate(z, lng, lnb, ws_ref, bs_ref, gg, ch):
    half = z.shape[1] // 2
    ggw = half // gg
    u, v = z[:, :half], z[:, half:]
    vc = v - jnp.mean(v, axis=-1, keepdims=True)
    rs = lax.rsqrt(jnp.mean(vc * vc, axis=-1, keepdims=True) + EPS)
    vhat = vc * rs
    vln = (vhat * lng + lnb).astype(BF16)
    chunks = []
    for n in range(z.shape[0] // ch):
        groups = []
        for g in range(gg):
            groups.append(_dot(ws_ref[g], vln[n * ch:(n + 1) * ch, g * ggw:(g + 1) * ggw]) + bs_ref[g])
        chunks.append(jnp.concatenate(groups, axis=1))
    sv = jnp.concatenate(chunks, axis=0) if len(chunks) > 1 else chunks[0]
    return u, sv, vhat, rs, vln


def _gmlp_fwd(h, mods, w_in, lng, lnb, ws, bs, w_out, *, tm):
    seq, d = h.shape
    zw = w_in.shape[0] * w_in.shape[-1]
    half = zw // 2
    gg, ch = ws.shape[0], ws.shape[-1]

    def body(h_ref, md_ref, win_ref, lng_ref, lnb_ref, ws_ref, bs_ref, wout_ref, h1_ref, zp_ref, y_ref):
        x = h_ref[...]
        md = md_ref[...]
        a, _ = _normmod(x, md, 0)
        zp = _dot_blocks(a.astype(BF16), win_ref)
        zp_ref[...] = zp.astype(BF16)
        u, sv, _, _, _ = _gmlp_gate(_gelu(zp), lng_ref[...], lnb_ref[...], ws_ref, bs_ref, gg, ch)
        y = _dot((u * sv).astype(BF16), wout_ref[...])
        y_ref[...] = y.astype(BF16)
        h1_ref[...] = x + md[G1:G1 + 1] * y

    return pl.pallas_call(
        body, name="gmlp_fwd", grid=(seq // tm,),
        in_specs=[_rows(tm, d), pl.BlockSpec((None, 8, d), lambda i: (1, 0, 0)),
                  _full(w_in.shape), _full((1, half)), _full((1, half)),
                  _full((gg, ch, ch)), _full((gg, ch, 1)), _full((half, d))],
        out_specs=[_rows(tm, d), _rows(tm, zw), _rows(tm, d)],
        out_shape=[jax.ShapeDtypeStruct((seq, d), F32), jax.ShapeDtypeStruct((seq, zw), BF16),
                   jax.ShapeDtypeStruct((seq, d), BF16)],
        compiler_params=_params(),
    )(h, mods, w_in, lng, lnb, ws, bs, w_out)


def _gmlp_bwd(h, dh1, zpre, y, mods, w_in, lng, lnb, ws, ws_t, bs, w_out, *, tm):
    seq, d = h.shape
    zw = w_in.shape[0] * w_in.shape[-1]
    half = zw // 2
    gg, ch = ws.shape[0], ws.shape[-1]
    ggw = half // gg

    def body(h_ref, g_ref, zp_ref, y_ref, md_ref, win_ref, lng_ref, lnb_ref, ws_ref, wst_ref, bs_ref, wout_ref,
             dh_ref, dzp_ref, gated_ref, dyb_ref, ab_ref, dmd_ref, dln_ref, dws_ref, dbs_ref):
        i = pl.program_id(0)
        x = h_ref[...]
        md = md_ref[...]
        a, parts = _normmod(x, md, 0)
        ab_ref[...] = a.astype(BF16)
        zp = zp_ref[...].astype(F32)
        lng_v = lng_ref[...]
        u, sv, vhat, rs, vln = _gmlp_gate(_gelu(zp), lng_v, lnb_ref[...], ws_ref, bs_ref, gg, ch)
        g = g_ref[...]
        dg1 = _rowsum(g * y_ref[...].astype(F32))
        dyb = (g * md[G1:G1 + 1]).astype(BF16)
        dyb_ref[...] = dyb
        gated_ref[...] = (u * sv).astype(BF16)
        dgated = _dot_nt(dyb, wout_ref[...])
        du = dgated * sv
        dsv = dgated * u

        @pl.when(i == 0)
        def _():
            dws_ref[...] = jnp.zeros(dws_ref.shape, F32)
            dbs_ref[...] = jnp.zeros(dbs_ref.shape, F32)
            dln_ref[...] = jnp.zeros(dln_ref.shape, F32)

        chunks = []
        for n in range(tm // ch):
            groups = []
            for gi in range(gg):
                blk = dsv[n * ch:(n + 1) * ch, gi * ggw:(gi + 1) * ggw]
                dbs_ref[gi] += jnp.sum(blk, axis=-1, keepdims=True)
                blk_b = blk.astype(BF16)
                dws_ref[gi] += _dot_nt(blk_b, vln[n * ch:(n + 1) * ch, gi * ggw:(gi + 1) * ggw])
                groups.append(_dot(wst_ref[gi], blk_b))
            chunks.append(jnp.concatenate(groups, axis=1))
        dvln = jnp.concatenate(chunks, axis=0) if len(chunks) > 1 else chunks[0]
        dln_ref[0:1, :] += _rowsum(dvln * vhat)
        dln_ref[1:2, :] += _rowsum(dvln)
        dvhat = dvln * lng_v
        dv = rs * (dvhat - jnp.mean(dvhat, axis=-1, keepdims=True)
                   - vhat * jnp.mean(dvhat * vhat, axis=-1, keepdims=True))
        dzp = (jnp.concatenate([du, dv], axis=1) * _gelu_grad(zp)).astype(BF16)
        dzp_ref[...] = dzp
        da = _dot_nt_blocks(dzp, win_ref)
        dx, dsh, dsc, dng = _normmod_bwd(da, parts, md, 0)
        dh_ref[...] = g + dx
        _acc_rows(dmd_ref, i == 0, {SH1: dsh, SC1: dsc, G1: dg1, NG0: dng})

    return pl.pallas_call(
        body, name="gmlp_bwd", grid=(seq // tm,),
        in_specs=[_rows(tm, d), _rows(tm, d), _rows(tm, zw), _rows(tm, d),
                  pl.BlockSpec((None, 8, d), lambda i: (1, 0, 0)),
                  _full(w_in.shape), _full((1, half)), _full((1, half)),
                  _full((gg, ch, ch)), _full((gg, ch, ch)), _full((gg, ch, 1)), _full((half, d))],
        out_specs=[_rows(tm, d), _rows(tm, zw), _rows(tm, half), _rows(tm, d), _rows(tm, d),
                   pl.BlockSpec((None, 8, d), lambda i: (0, 0, 0)), _full((8, half)), _full((gg, ch, ch)),
                   _full((gg, ch, 1))],
        out_shape=[jax.ShapeDtypeStruct((seq, d), F32), jax.ShapeDtypeStruct((seq, zw), BF16),
                   jax.ShapeDtypeStruct((seq, half), BF16), jax.ShapeDtypeStruct((seq, d), BF16),
                   jax.ShapeDtypeStruct((seq, d), BF16), jax.ShapeDtypeStruct((1, 8, d), F32),
                   jax.ShapeDtypeStruct((8, half), F32), jax.ShapeDtypeStruct((gg, ch, ch), F32),
                   jax.ShapeDtypeStruct((gg, ch, 1), F32)],
        compiler_params=_params(),
    )(h, dh1, zpre, y, mods, w_in, lng, lnb, ws, ws_t, bs, w_out)


def _final_loss(h, tgt, fg, *, tm):
    seq, d = h.shape

    def body(h_ref, t_ref, g_ref, dh_ref, acc_ref):
        i = pl.program_id(0)
        gain = g_ref[...]
        xhat, r = _rms_parts(h_ref[...])
        err = xhat * gain - t_ref[...]
        dy = err * (1.0 / d)
        dxhat = dy * gain
        dh_ref[...] = r * (dxhat - xhat * jnp.mean(dxhat * xhat, axis=-1, keepdims=True))
        part = jnp.sum(_rowsum(err * err), axis=-1, keepdims=True) * (0.5 / d)
        _acc_rows(acc_ref, i == 0, {0: _rowsum(dy * xhat), 1: jnp.broadcast_to(part, (1, d))})

    return pl.pallas_call(
        body, name="final_loss", grid=(seq // tm,),
        in_specs=[_rows(tm, d), _rows(tm, d), _full((1, d))],
        out_specs=[_rows(tm, d), _full((8, d))],
        out_shape=[jax.ShapeDtypeStruct((seq, d), F32), jax.ShapeDtypeStruct((8, d), F32)],
        compiler_params=_params(),
    )(h, tgt, fg)


def _ada_fwd(c_all, ada_w, ada_b_cols):
    depth, d, ncs = ada_w.shape

    def body(c_ref, w_ref, b_ref, o_ref):
        s = _silu(c_ref[...]).astype(BF16)
        o_ref[...] = _dot(s, w_ref[...].astype(BF16)) + b_ref[...]

    return pl.pallas_call(
        body, name="ada_fwd", grid=(depth,),
        in_specs=[_full((16, d)), pl.BlockSpec((None, d, ncs), lambda i: (i, 0, 0)),
                  pl.BlockSpec((None, 1, ncs), lambda i: (i, 0, 0))],
        out_specs=pl.BlockSpec((None, 16, ncs), lambda i: (i, 0, 0)),
        out_shape=jax.ShapeDtypeStruct((depth, 16, ncs), F32),
        compiler_params=_params(),
    )(c_all, ada_w, ada_b_cols.reshape(depth, 1, ncs))


def _ada_bwd(c_all, c_all_t, dmod, ada_w):
    depth, d, ncs = ada_w.shape

    def body(c_ref, ct_ref, dm_ref, w_ref, gw_ref, dc_ref):
        i = pl.program_id(0)
        dm = dm_ref[...]
        dctx = _rowsum(dm[8:16])
        rid = lax.broadcasted_iota(jnp.int32, (8, ncs), 0)
        low = jnp.where(rid == 0, jnp.broadcast_to(dctx, (8, ncs)), 0.0)
        dm16 = jnp.concatenate([dm[0:8], low], axis=0).astype(BF16)
        gw_ref[...] = _dot(_silu(ct_ref[...]).astype(BF16), dm16)

        @pl.when(i == 0)
        def _():
            dc_ref[...] = jnp.zeros(dc_ref.shape, F32)

        dc_ref[...] += _dot_nt(low.astype(BF16), w_ref[...].astype(BF16)) * _silu_grad(c_ref[8:9, :])

    return pl.pallas_call(
        body, name="ada_bwd", grid=(depth,),
        in_specs=[_full((16, d)), _full((d, 16)), pl.BlockSpec((None, 16, ncs), lambda i: (i, 0, 0)),
                  pl.BlockSpec((None, d, ncs), lambda i: (i, 0, 0))],
        out_specs=[pl.BlockSpec((None, d, ncs), lambda i: (i, 0, 0)), _full((8, d))],
        out_shape=[jax.ShapeDtypeStruct((depth, d, ncs), F32), jax.ShapeDtypeStruct((8, d), F32)],
        compiler_params=_params(),
    )(c_all, c_all_t, dmod, ada_w)


def _adamw_math(w, g, m, v):
    m = ADAM_B1 * m + (1.0 - ADAM_B1) * g
    v = ADAM_B2 * v + (1.0 - ADAM_B2) * jnp.square(g)
    m_hat = m * (1.0 / (1.0 - ADAM_B1 ** ADAM_STEP))
    v_hat = v * (1.0 / (1.0 - ADAM_B2 ** ADAM_STEP))
    delta = -ADAM_LR * (m_hat / (jnp.sqrt(v_hat) + ADAM_EPS) + ADAM_WD * w)
    return delta, m, v


def _adamw(ga, gb, w, m, v, name):
    rows, cols = w.shape
    tr = rows
    while tr * cols * 4 > (1 << 20) and tr % 16 == 0:
        tr //= 2
    grads = [ga] if gb is None else [ga, gb]

    def body(*refs):
        w_ref, m_ref, v_ref, g_out, d_out, m_out, v_out = refs[len(grads):]
        g = refs[0][...].astype(F32) if gb is None else refs[0][...].astype(F32) + refs[1][...].astype(F32)
        delta, m_new, v_new = _adamw_math(w_ref[...], g, m_ref[...], v_ref[...])
        g_out[...] = g
        d_out[...] = delta
        m_out[...] = m_new
        v_out[...] = v_new

    spec = _rows(tr, cols)
    return pl.pallas_call(
        body, name=name, grid=(rows // tr,),
        in_specs=[spec] * (len(grads) + 3), out_specs=[spec] * 4,
        out_shape=[jax.ShapeDtypeStruct((rows, cols), F32)] * 4,
        compiler_params=_params(),
    )(*grads, w, m, v)


def _sum_devices(gathered, name):
    n, rows, cols = gathered.shape
    tr = rows
    while tr * cols * 4 * n > (4 << 20) and tr % 16 == 0:
        tr //= 2

    def body(x_ref, o_ref):
        acc = x_ref[0]
        for j in range(1, n):
            acc = acc + x_ref[j]
        o_ref[...] = acc

    return pl.pallas_call(
        body, name=name, grid=(rows // tr,),
        in_specs=[pl.BlockSpec((n, tr, cols), lambda i: (0, i, 0))], out_specs=_rows(tr, cols),
        out_shape=jax.ShapeDtypeStruct((rows, cols), F32),
        compiler_params=_params(),
    )(gathered)


def _sum_partials(blocked, landeds, chip, name):
    n = len(blocked)
    cols = blocked[0].shape[-1]
    blocked = [b.reshape(N_CHIPS, -1, cols) for b in blocked]
    landeds = [l.reshape(3, -1, cols) for l in landeds]
    rows = blocked[0].shape[1]
    tr = rows
    while tr * cols * 2 * n > (1 << 20) and tr % 32 == 0:
        tr //= 2

    def body(chip_ref, *refs):
        out_ref = refs[-1]
        for li in range(n):
            acc = refs[li][...].astype(F32)
            for p in range(3):
                acc = acc + refs[n + li][p].astype(F32)
            out_ref[li] = acc.astype(BF16)

    out = pl.pallas_call(
        body, name=name,
        grid_spec=pltpu.PrefetchScalarGridSpec(
            num_scalar_prefetch=1, grid=(rows // tr,),
            in_specs=[pl.BlockSpec((None, tr, cols), lambda i, k: (k[0], i, 0))] * n
            + [pl.BlockSpec((3, tr, cols), lambda i, k: (0, i, 0))] * n,
            out_specs=pl.BlockSpec((n, tr, cols), lambda i, k: (0, i, 0))),
        out_shape=jax.ShapeDtypeStruct((n, rows, cols), BF16),
        compiler_params=_params(),
    )(jnp.reshape(chip, (1,)).astype(jnp.int32), *blocked, *landeds)
    return out.reshape(n * rows, cols)


def _my_place():
    return lax.axis_index("x"), lax.axis_index("y"), lax.axis_index("c")


def _other_chips(x, y):
    return [(1 - x, y), (x, 1 - y), (1 - x, 1 - y)]


def _all_gather_small(block, name):
    rows, cols = block.shape

    def body(x_ref, out_ref, send_sems, recv_sems, local_sem):
        x, y, c = _my_place()
        me, sibling = (x, y, c), (x, y, 1 - c)
        chips = _other_chips(x, y)

        def slot(px, py, pc):
            return out_ref.at[4 * px + 2 * py + pc]

        def copy(k, blk, to, src=None):
            return pltpu.make_async_remote_copy(
                src_ref=slot(*blk) if src is None else src, dst_ref=slot(*blk),
                send_sem=send_sems.at[k], recv_sem=recv_sems.at[k], device_id=to, device_id_type=MESH)

        mine = pltpu.make_async_copy(x_ref, slot(*me), local_sem)
        mine.start()
        first = [copy(0, me, sibling, src=x_ref)]
        first += [copy(1 + j, me, (*chip, c), src=x_ref) for j, chip in enumerate(chips)]
        for cp in first:
            cp.start()
        passed = [copy(4 + j, (*chip, c), sibling) for j, chip in enumerate(chips)]
        for j, chip in enumerate(chips):
            copy(1 + j, (*chip, c), me).wait_recv()
            passed[j].start()
        copy(0, sibling, me).wait_recv()
        for j, chip in enumerate(chips):
            copy(4 + j, (*chip, 1 - c), me).wait_recv()
        for cp in first + passed:
            cp.wait_send()
        mine.wait()

    return pl.pallas_call(
        body, name=name,
        out_shape=jax.ShapeDtypeStruct((N_DEV, rows, cols), block.dtype),
        in_specs=[pl.BlockSpec(memory_space=pltpu.VMEM)],
        out_specs=pl.BlockSpec(memory_space=pltpu.VMEM),
        scratch_shapes=[pltpu.SemaphoreType.DMA((7,)), pltpu.SemaphoreType.DMA((7,)), pltpu.SemaphoreType.DMA],
        compiler_params=_params(),
    )(block)


HBM_SPEC = pl.BlockSpec(memory_space=pltpu.HBM)
SEM_SPEC = pl.BlockSpec(memory_space=pltpu.SEMAPHORE)
DATAFLOW_EFFECT = pltpu.SideEffectType.DATAFLOW_SIDE_EFFECTING


def _same_core_of_other_chips(x, y, c):
    return [(*chip, c) for chip in _other_chips(x, y)]


def _sibling_core(x, y, c):
    return [(x, y, 1 - c)]


def _gather_views(src, land, p, x, y):
    return src, land.at[2 * x + y]


def _scatter_views(src, land, p, x, y):
    peer_chip = (2 * (1 - x) + y, 2 * x + (1 - y), 2 * (1 - x) + (1 - y))[p]
    return src.at[peer_chip], land.at[p]


def _whole_views(src, land, p, x, y):
    return src, land


GATHER_PLAN = (_same_core_of_other_chips, _gather_views, 3)
SCATTER_PLAN = (_same_core_of_other_chips, _scatter_views, 3)
SIBLING_PLAN = (_sibling_core, _whole_views, 1)


def _exchange_copies(srcs, lands, send_sems, recv_sems, plan):
    peers_of, views, n_peers = plan
    x, y, c = _my_place()
    copies = []
    for j, (src, land) in enumerate(zip(srcs, lands)):
        for p, peer in enumerate(peers_of(x, y, c)):
            s_view, d_view = views(src, land, p, x, y)
            k = n_peers * j + p
            copies.append(pltpu.make_async_remote_copy(
                src_ref=s_view, dst_ref=d_view, send_sem=send_sems.at[k], recv_sem=recv_sems.at[k],
                device_id=peer, device_id_type=MESH))
    return copies


def _exchange_start(srcs, lands, plan, name):
    n = len(srcs)

    def body(*refs):
        send_sems, recv_sems = refs[2 * n], refs[2 * n + 1]
        token = refs[-1]
        for cp in _exchange_copies(refs[:n], refs[n:2 * n], send_sems, recv_sems, plan):
            cp.start()
        token[...] = jnp.zeros(token.shape, token.dtype)

    operands = [pltpu.with_memory_space_constraint(a, pltpu.HBM) for a in (*srcs, *lands)]
    out = pl.pallas_call(
        body, name=name,
        out_shape=(pltpu.SemaphoreType.DMA((plan[2] * n,)), pltpu.SemaphoreType.DMA((plan[2] * n,)),
                   *[pltpu.HBM(a.shape, a.dtype) for a in operands], jax.ShapeDtypeStruct((8, LANES), F32)),
        in_specs=[HBM_SPEC] * (2 * n),
        out_specs=(SEM_SPEC, SEM_SPEC, *[HBM_SPEC] * (2 * n), pl.BlockSpec(memory_space=pltpu.VMEM)),
        input_output_aliases={i: 2 + i for i in range(2 * n)},
        compiler_params=pltpu.CompilerParams(has_side_effects=DATAFLOW_EFFECT),
    )(*operands)
    return out[0], out[1], list(out[2:2 + n]), list(out[2 + n:2 + 2 * n]), out[-1]


def _exchange_wait(send_sems, recv_sems, srcs, lands, plan, after, name):
    n = len(srcs)

    def body(*refs):
        send, recv = refs[2 * n], refs[2 * n + 1]
        for cp in _exchange_copies(refs[:n], refs[n:2 * n], send, recv, plan):
            cp.wait_send()
            cp.wait_recv()

    out = pl.pallas_call(
        body, name=name,
        out_shape=tuple(pltpu.HBM(a.shape, a.dtype) for a in (*srcs, *lands)),
        in_specs=[HBM_SPEC] * (2 * n) + [SEM_SPEC, SEM_SPEC, HBM_SPEC],
        out_specs=tuple([HBM_SPEC] * (2 * n)),
        input_output_aliases={i: i for i in range(2 * n)},
        compiler_params=pltpu.CompilerParams(has_side_effects=DATAFLOW_EFFECT),
    )(*srcs, *lands, send_sems, recv_sems, pltpu.with_memory_space_constraint(after, pltpu.HBM))
    return list(out[:n]), list(out[n:])


def _landing_for_gather(shard, chip):
    land = lax.empty((N_CHIPS, *shard.shape), shard.dtype)
    return lax.dynamic_update_index_in_dim(land, shard, chip, 0)


TILE_ELEMS = SUBLANES * LANES


def _pack(arrays):
    parts = []
    for a in arrays:
        flat = a.reshape(-1).astype(F32)
        pad = (-flat.shape[0]) % TILE_ELEMS
        if pad:
            flat = jnp.concatenate([flat, jnp.zeros((pad,), F32)])
        parts.append(flat.reshape(-1, LANES))
    return jnp.concatenate(parts, axis=0) if len(parts) > 1 else parts[0]


def _unpack(buf, shapes):
    out, r = [], 0
    lead = buf.shape[:-2]
    for shp in shapes:
        size = math.prod(shp)
        nr = -(-size // TILE_ELEMS) * SUBLANES
        flat = buf[..., r:r + nr, :].reshape(*lead, nr * LANES)[..., :size]
        out.append(flat.reshape(*lead, *shp))
        r += nr
    return out


def _chip_cols(a, k, width):
    return lax.dynamic_slice_in_dim(a, k * width, width, axis=a.ndim - 1)


def _across_chips(gathered, c0_only_shape):
    return gathered.reshape(2, 2, 2, *c0_only_shape)[:, :, 0].reshape(N_CHIPS, *c0_only_shape)


def kernel(x, c, ctx, c_ctx, ada_w, ada_b, norm_g, mlp_w1, mlp_w2, pool_w, pool_scale, attn_w_qkv, attn_w_o, attn_q_g, attn_k_g, gm_w_in, gm_ln_g, gm_ln_b, gm_ws, gm_bs, gm_w_out, final_g, loss_target, m_c_ctx, m_ada_w, m_ada_b, m_norm_g, m_mlp_w1, m_mlp_w2, m_pool_w, m_pool_scale, m_attn_w_qkv, m_attn_w_o, m_attn_q_g, m_attn_k_g, m_gm_w_in, m_gm_ln_g, m_gm_ln_b, m_gm_ws, m_gm_bs, m_gm_w_out, m_final_g, v_c_ctx, v_ada_w, v_ada_b, v_norm_g, v_mlp_w1, v_mlp_w2, v_pool_w, v_pool_scale, v_attn_w_qkv, v_attn_w_o, v_attn_q_g, v_attn_k_g, v_gm_w_in, v_gm_ln_g, v_gm_ln_b, v_gm_ws, v_gm_bs, v_gm_w_out, v_final_g):
    seq, d = x.shape[1], x.shape[2]
    n_ctx = ctx.shape[1]
    total = n_ctx + seq
    hd = attn_q_g.shape[-1]
    nh = d // hd
    nkv = nh // 2
    gg, ch = gm_ws.shape[1], gm_ws.shape[-1]
    half = gm_w_out.shape[1] * N_CHIPS
    pgw = pool_w.shape[-1]
    tm = min(256, n_ctx)
    tm_lat = min(2 * tm, seq)
    nct = n_ctx // tm
    seg_lens = (n_ctx, seq)

    mx, my, mc = _my_place()
    chip = 2 * mx + my
    me = 4 * mx + 2 * my + mc

    c_rows = jnp.concatenate([c, jnp.zeros((7, d), F32)], axis=0)
    c_gath = _all_gather_small(c_rows, "gather_cond")[:, 0, :]
    c_all = jnp.concatenate([c_gath, c_ctx[None, :], jnp.zeros((7, d), F32)], axis=0)
    ncs = ada_w.shape[-1]
    ada_cols = _ada_fwd(c_all, ada_w, _chip_cols(ada_b, chip, ncs))
    small_shapes = [ada_cols.shape, norm_g.shape, pool_scale.shape, gm_ln_g.shape, gm_ln_b.shape]
    gathered = _all_gather_small(_pack([ada_cols, norm_g, pool_scale, gm_ln_g, gm_ln_b]), "gather_small_params")
    per_chip = _across_chips(gathered, gathered.shape[1:])
    ada_g, ng_g, ps_g, lng_g, lnb_g = _unpack(per_chip, small_shapes)

    def join_last(a):
        return jnp.moveaxis(a, 0, -2).reshape(*a.shape[1:-1], N_CHIPS * a.shape[-1])

    ada_full = join_last(ada_g)
    ng_full = join_last(ng_g)
    ps_full = join_last(ps_g)
    lng_full = join_last(lng_g)
    lnb_full = join_last(lnb_g)
    mod_lat = lax.dynamic_slice_in_dim(ada_full, me, 1, axis=1).reshape(DEPTH, 6, d)
    mod_ctx = ada_full[:, 8].reshape(DEPTH, 6, d)
    mods = jnp.stack([jnp.concatenate([mod_ctx, ng_full], axis=1), jnp.concatenate([mod_lat, ng_full], axis=1)],
                     axis=1)

    weight_groups = [
        [pool_w],
        [mlp_w1[0]],
        [mlp_w2[0]],
        [attn_w_qkv[0], attn_w_o[0]],
        [mlp_w1[1], mlp_w2[1], mlp_w1[2], mlp_w2[2], gm_w_in[0], gm_w_out[0], mlp_w1[3], mlp_w2[3]],
    ]
    gathers = [None] * len(weight_groups)

    def gather_start(gi, after):
        shards, _ = lax.optimization_barrier(([w.astype(BF16) for w in weight_groups[gi]], after))
        lands = [_landing_for_gather(s, chip) for s in shards]
        gathers[gi] = _exchange_start(shards, lands, GATHER_PLAN, f"gather_weights_{gi}_start")
        return gathers[gi][4][0:1, 0:1]

    def gathered(gi, after):
        send, recv, srcs, lands, _ = gathers[gi]
        return _exchange_wait(send, recv, srcs, lands, GATHER_PLAN, after, f"gather_weights_{gi}_wait")[1]

    def rows_joined(a):
        return a.reshape(-1, a.shape[-1])

    w1_b, w2_b = [None] * DEPTH, [None] * DEPTH
    gather_start(0, mods)
    behind_gather_1 = gather_start(1, mods)
    pw_land, = gathered(0, ps_full)
    pw_f = jnp.transpose(pw_land, (1, 2, 0, 3, 4)).reshape(pool_w.shape[0], pool_w.shape[1], pgw, pgw)

    gains = jnp.concatenate([attn_q_g, attn_k_g, jnp.zeros((6, hd), F32)], axis=0)
    ws_b = gm_ws[0].astype(BF16)
    ws_t = jnp.swapaxes(gm_ws[0], 1, 2).astype(BF16)
    bs_col = gm_bs[0][:, :, None]
    cos, sin = _rope_tables(n_ctx, seq, hd)
    lat = lambda i: mods[i, 1:2]

    hc0 = jnp.concatenate([ctx[0] + behind_gather_1, x[0]], axis=0)
    ha0, p0 = _pool_fwd(hc0, mods[0] + behind_gather_1, pw_f, ps_full, 0, nct=nct, tm=tm, seg_lens=seg_lens)
    w1_b[0], = gathered(1, ha0)
    u0 = _mlp_up(ha0, mods[0] + gather_start(2, w1_b[0]), w1_b[0], 0, nct=nct, tm=tm)
    w2_b[0], = gathered(2, u0)
    hc1, o0 = _mlp_down(ha0, u0, mods[0] + gather_start(3, w2_b[0]), w2_b[0], 0, nct=nct, tm=tm)
    wqkv_b, wo_land = gathered(3, hc1)
    mods1 = mods[1] + gather_start(4, wqkv_b)
    wo_f = rows_joined(wo_land)
    xa1, qkv, q_r, k_r, v_b = _qkv_fwd(hc1, mods1, wqkv_b, cos, sin, gains, nh=nh, nkv=nkv, nct=nct, tm=tm)
    o_att, lse = _flash_fwd(q_r, k_r, v_b, n_ctx=n_ctx, hd=hd)
    ha1, y1 = _proj_fwd(o_att, wo_f, hc1, mods1, n_ctx=n_ctx, tm=tm)
    w1_b[1], w2_b[1], w1_b[2], w2_b[2], win_b, wout_land, w1_b[3], w2_b[3] = gathered(4, ha1)
    h2, u1, o1 = _mlp_fwd(ha1, lat(1), w1_b[1], w2_b[1], 1, nct=0, tm=tm_lat)
    wout_f = rows_joined(wout_land)
    ha2, zpre, y2 = _gmlp_fwd(h2, mods[2], win_b, lng_full, lnb_full, ws_b, bs_col, wout_f, tm=tm)
    h3, u2, o2 = _mlp_fwd(ha2, lat(2), w1_b[2], w2_b[2], 2, nct=0, tm=tm_lat)
    ha3, p3 = _pool_fwd(h3, lat(3), pw_f, ps_full, 3, nct=0, tm=tm_lat, seg_lens=seg_lens)
    h4, u3, o3 = _mlp_fwd(ha3, lat(3), w1_b[3], w2_b[3], 3, nct=0, tm=tm_lat)
    dh4, fin_acc = _final_loss(h4, loss_target[0], final_g[None, :], tm=tm_lat)

    dmods = [None] * DEPTH
    scatters = [None] * (DEPTH + 2)

    def blocked_rows(g):
        return g.reshape(N_CHIPS, g.shape[1] // N_CHIPS, g.shape[2])

    def blocked_pool(dpw):
        pg = dpw.shape[0]
        return jnp.transpose(dpw.astype(BF16).reshape(pg, N_CHIPS, pgw // N_CHIPS, pgw), (1, 0, 2, 3))

    def scatter_start(i, grads):
        lands = [lax.empty((3, *g.shape[1:]), g.dtype) for g in grads]
        scatters[i] = _exchange_start(grads, lands, SCATTER_PLAN, f"scatter_grads_{i}_start")
        return scatters[i][4][0:1, 0:1]

    def mlp_back(i, h_in, dh_out, u, o, md, n_ct):
        dh_in, du, dob, mb, dmd = _mlp_bwd(h_in, dh_out, u, o, md, w1_b[i], w2_b[i], i, nct=n_ct, tm=tm_lat)
        dw1 = _mm_tn(mb, du, f"mlp_dw1_{i}", col_blocks=N_CHIPS)
        dw2 = blocked_rows(_mm_tn(u, dob, f"mlp_dw2_{i}", relu2=True))
        return dh_in, dmd, [dw1, dw2]

    def pool_back(i, h_in, p_in, dh_out, md, n_ct, tile):
        dp, dmd_a, dps, dpw = _pool_bwd_weights(p_in, dh_out, md, pw_f, ps_full, i, nct=n_ct, tm=tile)
        dh_in, dmd_b = _pool_bwd_input(dp, h_in, dh_out, md, i, nct=n_ct, tm=tile, seg_lens=seg_lens, gw=pgw)
        return dh_in, dmd_a + dmd_b, dps, dpw

    zero_grp = jnp.zeros((1, 8, d), F32)
    dha3, dmd3, dws3 = mlp_back(3, ha3, dh4, u3, o3, lat(3), 0)
    dh3, dmd3p, dps3, dpw3 = pool_back(3, h3, p3, dha3, lat(3), 0, tm_lat)
    dmods[3] = jnp.concatenate([zero_grp, dmd3 + dmd3p], axis=0)
    tok = scatter_start(3, dws3 + [blocked_pool(dpw3)])
    dha2, dmd2, dws2 = mlp_back(2, ha2, dh3, u2, o2, lat(2) + tok, 0)
    dh2, dzpre, gated, dyb2, ab2, dmd2g, dln, dws, dbs = _gmlp_bwd(
        h2, dha2, zpre, y2, mods[2], win_b, lng_full, lnb_full, ws_b, ws_t, bs_col, wout_f, tm=tm)
    dwin = _mm_tn(ab2, dzpre, "gmlp_dw_in", col_blocks=N_CHIPS)
    dwout = blocked_rows(_mm_tn(gated, dyb2, "gmlp_dw_out"))
    dmods[2] = jnp.concatenate([zero_grp, dmd2 + dmd2g], axis=0)
    tok = scatter_start(2, dws2 + [dwin, dwout])
    dha1, dmd1, dws1 = mlp_back(1, ha1, dh2, u1, o1, lat(1) + tok, 0)
    do_att, dyb1, dmd1p = _proj_bwd(dha1, y1, mods[1], wo_f, tm=tm_lat)
    dwo = blocked_rows(_mm_tn(o_att, dyb1, "attn_dw_o"))
    dq, dk, dv = _flash_bwd(q_r, k_r, v_b, o_att, do_att, lse, n_ctx=n_ctx, hd=hd)
    dqkv, dgains = _qkv_bwd(qkv, dq, dk, dv, cos, sin, gains, nh=nh, nkv=nkv, nct=nct, tm=tm)
    dwqkv = _mm_tn(xa1, dqkv, "attn_dw_qkv", col_blocks=N_CHIPS)
    dhc1, dmd1i = _attn_in_bwd(dqkv, wqkv_b, hc1, dha1, mods[1], nct=nct, tm=tm)
    dmods[1] = dmd1i + jnp.concatenate([zero_grp, dmd1 + dmd1p], axis=0)
    tok = scatter_start(1, dws1 + [dwqkv, dwo])
    dha0, du0, dob0, mb0, dmd0 = _mlp_bwd(ha0, dhc1, u0, o0, mods[0] + tok, w1_b[0], w2_b[0], 0, nct=nct, tm=tm)
    scatter_start(DEPTH + 1, [blocked_rows(_mm_tn(u0, dob0, "mlp_dw2_0", relu2=True))])
    dw1_0 = _mm_tn(mb0, du0, "mlp_dw1_0", col_blocks=N_CHIPS, after=scatters[DEPTH + 1][4])
    tok = scatter_start(0, [dw1_0])
    dhc0, dmd0p, dps0, dpw0 = pool_back(0, hc0, p0, dha0, mods[0] + tok, nct, tm)
    dmods[0] = dmd0 + dmd0p
    grad_x = dhc0[None]
    scatter_start(DEPTH, [blocked_pool(dpw0)])

    dmods_all = jnp.stack(dmods, axis=0)
    small_grads = [dmods_all, dws, dbs, dgains, dln, dps0, dps3, fin_acc]
    sg_shapes = [a.shape for a in small_grads]
    sg_gath = _all_gather_small(_pack(small_grads), "gather_small_grads")
    sg_sum = _sum_devices(sg_gath, "sum_small_grads")
    s_dmods, s_dws, s_dbs, s_dgains, s_dln, s_dps0, s_dps3, s_fin = _unpack(sg_sum, sg_shapes)
    loss = s_fin[1, 0]

    sources, landed = [None] * len(scatters), [None] * len(scatters)
    for i in (3, 2, 1, DEPTH + 1, 0, DEPTH):
        send, recv, srcs, lands, _ = scatters[i]
        sources[i], landed[i] = _exchange_wait(send, recv, srcs, lands, SCATTER_PLAN, sg_sum, f"scatter_grads_{i}_wait")

    def summed(name, picks):
        return _sum_partials([sources[i][j] for i, j in picks], [landed[i][j] for i, j in picks], chip,
                             f"sum_chips_{name}")

    big = [("mlp_w1", mlp_w1, m_mlp_w1, v_mlp_w1, [(i, 0) for i in range(DEPTH)]),
           ("mlp_w2", mlp_w2, m_mlp_w2, v_mlp_w2, [(DEPTH + 1, 0)] + [(i, 1) for i in range(1, DEPTH)]),
           ("pool_w", pool_w, m_pool_w, v_pool_w, [(DEPTH, 0), (3, 2)]),
           ("attn_w_qkv", attn_w_qkv, m_attn_w_qkv, v_attn_w_qkv, [(1, 2)]),
           ("attn_w_o", attn_w_o, m_attn_w_o, v_attn_w_o, [(1, 3)]),
           ("gm_w_in", gm_w_in, m_gm_w_in, v_gm_w_in, [(2, 2)]),
           ("gm_w_out", gm_w_out, m_gm_w_out, v_gm_w_out, [(2, 3)])]
    partial = [summed(name, picks) for name, _, _, _, picks in big]
    swap = _exchange_start(partial, [lax.empty(p.shape, p.dtype) for p in partial], SIBLING_PLAN,
                           "swap_with_sibling_start")
    behind_swap = swap[4][0:1, 0:1]

    dm_dev = _unpack(sg_gath, sg_shapes[:1])[0]
    dm_lat = jnp.moveaxis(dm_dev[:, :, 1, :6, :], 0, 1).reshape(DEPTH, N_DEV, 6 * d)
    dm_ctx = jnp.moveaxis(dm_dev[:, :, 0, :6, :], 0, 1).reshape(DEPTH, N_DEV, 6 * d)
    dmod16 = _chip_cols(jnp.concatenate([dm_lat, dm_ctx], axis=1), chip, ncs) + behind_swap
    g_ada_w, dcc_part = _ada_bwd(c_all, c_all.T, dmod16, ada_w)
    dcc_gath = _all_gather_small(dcc_part, "gather_d_c_ctx")
    dcc_chips = _across_chips(dcc_gath, dcc_gath.shape[1:])
    dcc_rows = _sum_devices(dcc_chips, "sum_d_c_ctx")
    dcc = dcc_rows[0]
    ada_res = _adamw(g_ada_w.reshape(-1, ncs), None, ada_w.reshape(-1, ncs),
                     m_ada_w.reshape(-1, ncs), v_ada_w.reshape(-1, ncs), "adamw_ada_w")

    partial, from_sibling = _exchange_wait(swap[0], swap[1], swap[2], swap[3], SIBLING_PLAN, ada_res[1],
                                           "swap_with_sibling_wait")
    big_out = {}
    for (name, w, m, v, _), mine, theirs in zip(big, partial, from_sibling):
        cols = w.shape[-1]
        res = _adamw(mine, theirs, w.reshape(-1, cols), m.reshape(-1, cols), v.reshape(-1, cols), f"adamw_{name}")
        big_out[name] = [r.reshape(w.shape) for r in res]
    big_out["ada_w"] = [r.reshape(ada_w.shape) for r in ada_res]

    def cols_of(a, width):
        return _chip_cols(a, chip, width)

    zero = lambda a: jnp.zeros(a.shape, F32)
    ngw = norm_g.shape[-1]
    small = {
        "c_ctx": (dcc, zero(dcc), c_ctx, m_c_ctx, v_c_ctx),
        "ada_b": (s_dmods[:, 0, :6].reshape(DEPTH, 6 * d), s_dmods[:, 1, :6].reshape(DEPTH, 6 * d), ada_b, m_ada_b,
                  v_ada_b),
        "norm_g": (cols_of(s_dmods[:, 0, 6:8], ngw), cols_of(s_dmods[:, 1, 6:8], ngw), norm_g, m_norm_g, v_norm_g),
        "pool_scale": (cols_of(jnp.stack([s_dps0[0], s_dps3[0]]), pool_scale.shape[-1]), zero(pool_scale),
                       pool_scale, m_pool_scale, v_pool_scale),
        "attn_q_g": (s_dgains[0:1], zero(attn_q_g), attn_q_g, m_attn_q_g, v_attn_q_g),
        "attn_k_g": (s_dgains[1:2], zero(attn_k_g), attn_k_g, m_attn_k_g, v_attn_k_g),
        "gm_ln_g": (cols_of(s_dln[0:1], gm_ln_g.shape[-1]), zero(gm_ln_g), gm_ln_g, m_gm_ln_g, v_gm_ln_g),
        "gm_ln_b": (cols_of(s_dln[1:2], gm_ln_b.shape[-1]), zero(gm_ln_b), gm_ln_b, m_gm_ln_b, v_gm_ln_b),
        "gm_ws": (s_dws[None], zero(gm_ws), gm_ws, m_gm_ws, v_gm_ws),
        "gm_bs": (s_dbs[None, :, :, 0], zero(gm_bs), gm_bs, m_gm_bs, v_gm_bs),
        "final_g": (s_fin[0], zero(final_g), final_g, m_final_g, v_final_g),
    }
    keys = list(small)
    packed = [_pack([small[k][t] for k in keys]) for t in range(5)]
    res = _adamw(*packed, "adamw_small")
    shapes = [small[k][2].shape for k in keys]
    small_out = {k: [] for k in keys}
    for r in res:
        for k, a in zip(keys, _unpack(r, shapes)):
            small_out[k].append(a)

    order = ["c_ctx", "ada_w", "ada_b", "norm_g", "mlp_w1", "mlp_w2", "pool_w", "pool_scale", "attn_w_qkv",
             "attn_w_o", "attn_q_g", "attn_k_g", "gm_w_in", "gm_ln_g", "gm_ln_b", "gm_ws", "gm_bs", "gm_w_out",
             "final_g"]
    allo = {**big_out, **small_out}
    outs = [loss, grad_x]
    for t in range(4):
        outs += [allo[k][t] for k in order]
    return tuple(outs)
```
